```python
import jax, jax.numpy as jnp
from jax import lax
import numpy as np

D_MODEL = 4096
BATCH = 8
SEQ = 4096
DEPTH = 1

MIX_WIDTH = D_MODEL
SB_WIDTH = MIX_WIDTH // 2
CONV_WIDTH = MIX_WIDTH - SB_WIDTH
HEAD_DIM = 128
SB_HEADS = SB_WIDTH // HEAD_DIM
CONV_KERNEL = 31
Q_BLOCK = 128
EPS = 1e-6
IN_COLS = 4 * SB_WIDTH + 3 * CONV_WIDTH

kernel_name = "hybrid_stickbreak_conformer_adaln"


def rms_norm(x, g):
    x32 = x.astype(jnp.float32)
    y = x32 * lax.rsqrt(jnp.mean(x32 * x32, axis=-1, keepdims=True) + EPS)
    return (y * g.astype(jnp.float32)).astype(x.dtype)


def layer_norm(x, g, b):
    x32 = x.astype(jnp.float32)
    mu = jnp.mean(x32, axis=-1, keepdims=True)
    xc = x32 - mu
    var = jnp.mean(xc * xc, axis=-1, keepdims=True)
    y = xc * lax.rsqrt(var + EPS) * g.astype(jnp.float32) + b.astype(jnp.float32)
    return y.astype(x.dtype)


def stick_breaking_attention(q, k, v):
    B, H, S, Dh = q.shape
    nb = S // Q_BLOCK
    scale = Dh ** -0.5
    q_blocks = q.reshape(B, H, nb, Q_BLOCK, Dh).transpose(2, 0, 1, 3, 4)
    starts = jnp.arange(nb, dtype=jnp.int32) * Q_BLOCK
    key_pos = jnp.arange(S, dtype=jnp.int32)

    def one_block(args):
        qb, start = args
        z = jnp.einsum('bhqd,bhkd->bhqk', qb, k).astype(jnp.float32) * scale
        q_pos = start + jnp.arange(Q_BLOCK, dtype=jnp.int32)
        causal = key_pos[None, :] < q_pos[:, None]
        log_stay = jnp.where(causal, -jax.nn.softplus(z), 0.0)
        stay_after = lax.cumsum(log_stay, axis=3, reverse=True) - log_stay
        log_w = jax.nn.log_sigmoid(z) + stay_after
        w = jnp.where(causal, jnp.exp(log_w), 0.0)
        return jnp.einsum('bhqk,bhkd->bhqd', w.astype(v.dtype), v)

    out = lax.map(one_block, (q_blocks, starts))
    return out.transpose(1, 2, 0, 3, 4).reshape(B, H, S, Dh)


def conformer_conv(u, g_glu, w_dw, b_dw, ln_g, ln_b, w_pw, b_pw):
    C = u.shape[-1]
    h = u * jax.nn.sigmoid(g_glu)
    h = lax.conv_general_dilated(
        h, w_dw[:, None, :], window_strides=(1,),
        padding=[(CONV_KERNEL - 1, 0)],
        dimension_numbers=('NWC', 'WIO', 'NWC'),
        feature_group_count=C) + b_dw
    h = jax.nn.silu(layer_norm(h, ln_g, ln_b))
    return h @ w_pw + b_pw


def hybrid_layer(x, c, norm_g, w_ada, b_ada, w_in, q_norm_g, k_norm_g,
                 w_dw, b_dw, ln_g, ln_b, w_pw, b_pw, w_out):
    B, S, _ = x.shape
    mod = jax.nn.silu(c) @ w_ada + b_ada
    shift, scale, gate = jnp.split(mod, 3, axis=-1)
    h = rms_norm(x, norm_g) * (1.0 + scale[:, None, :]) + shift[:, None, :]

    proj = h @ w_in
    splits = [SB_WIDTH, 2 * SB_WIDTH, 3 * SB_WIDTH, 4 * SB_WIDTH,
              4 * SB_WIDTH + CONV_WIDTH, 4 * SB_WIDTH + 2 * CONV_WIDTH]
    q, k, v, g_sb, u, g_glu, g_conv = jnp.split(proj, splits, axis=-1)

    def heads(t):
        return t.reshape(B, S, SB_HEADS, HEAD_DIM).transpose(0, 2, 1, 3)

    qh = rms_norm(heads(q), q_norm_g)
    kh = rms_norm(heads(k), k_norm_g)
    o = stick_breaking_attention(qh, kh, heads(v))
    o = o.transpose(0, 2, 1, 3).reshape(B, S, SB_WIDTH)
    y_sb = o * jax.nn.silu(g_sb)

    y_conv = conformer_conv(u, g_glu, w_dw, b_dw, ln_g, ln_b, w_pw, b_pw) * jax.nn.silu(g_conv)

    y = jnp.concatenate([y_sb, y_conv], axis=-1) @ w_out
    return x + gate[:, None, :] * y


def _fwd_setup_inputs(seed: int = 0) -> dict:
    key = jax.random.key(seed)
    ks = jax.random.split(key, 16)
    f32 = jnp.float32
    D = D_MODEL
    x = jax.random.normal(ks[0], (BATCH, SEQ, D), f32)
    c = jax.random.normal(ks[1], (BATCH, D), f32)
    norm_g = 1.0 + 0.02 * jax.random.normal(ks[2], (DEPTH, D), f32)
    w_ada = 0.5 * D ** -0.5 * jax.random.normal(ks[3], (DEPTH, D, 3 * D), f32)
    b_ada = 0.02 * jax.random.normal(ks[4], (DEPTH, 3 * D), f32)
    w_in = D ** -0.5 * jax.random.normal(ks[5], (DEPTH, D, IN_COLS), f32)
    q_norm_g = 1.0 + 0.02 * jax.random.normal(ks[6], (DEPTH, HEAD_DIM), f32)
    k_norm_g = 1.0 + 0.02 * jax.random.normal(ks[7], (DEPTH, HEAD_DIM), f32)
    w_dw = CONV_KERNEL ** -0.5 * jax.random.normal(ks[8], (DEPTH, CONV_KERNEL, CONV_WIDTH), f32)
    b_dw = 0.02 * jax.random.normal(ks[9], (DEPTH, CONV_WIDTH), f32)
    ln_g = 1.0 + 0.02 * jax.random.normal(ks[10], (DEPTH, CONV_WIDTH), f32)
    ln_b = 0.02 * jax.random.normal(ks[11], (DEPTH, CONV_WIDTH), f32)
    w_pw = CONV_WIDTH ** -0.5 * jax.random.normal(ks[12], (DEPTH, CONV_WIDTH, CONV_WIDTH), f32)
    b_pw = 0.02 * jax.random.normal(ks[13], (DEPTH, CONV_WIDTH), f32)
    w_out = MIX_WIDTH ** -0.5 * jax.random.normal(ks[14], (DEPTH, MIX_WIDTH, D), f32)
    return {"x": x, "c": c, "norm_g": norm_g, "w_ada": w_ada, "b_ada": b_ada,
            "w_in": w_in, "q_norm_g": q_norm_g, "k_norm_g": k_norm_g,
            "w_dw": w_dw, "b_dw": b_dw, "ln_g": ln_g, "ln_b": ln_b,
            "w_pw": w_pw, "b_pw": b_pw, "w_out": w_out}


def _fwd_reference(x, c, norm_g, w_ada, b_ada, w_in, q_norm_g, k_norm_g,
              w_dw, b_dw, ln_g, ln_b, w_pw, b_pw, w_out):
    for layer in range(DEPTH):
        x = hybrid_layer(x, c, norm_g[layer], w_ada[layer], b_ada[layer], w_in[layer],
                         q_norm_g[layer], k_norm_g[layer], w_dw[layer], b_dw[layer],
                         ln_g[layer], ln_b[layer], w_pw[layer], b_pw[layer], w_out[layer])
    return x


import jax as _jax
import jax.numpy as _jnp

TWIN_FORMAT = 'train_step'
FWD_PARAMS = ['x', 'c', 'norm_g', 'w_ada', 'b_ada', 'w_in', 'q_norm_g', 'k_norm_g', 'w_dw', 'b_dw', 'ln_g', 'ln_b', 'w_pw', 'b_pw', 'w_out']
TWIN_WEIGHTS = ['norm_g', 'w_ada', 'b_ada', 'w_in', 'q_norm_g', 'k_norm_g', 'w_dw', 'b_dw', 'ln_g', 'ln_b', 'w_pw', 'b_pw', 'w_out']
TWIN_DIFF_INPUT = 'x'
TWIN_INPUTS = ['x', 'c', 'norm_g', 'w_ada', 'b_ada', 'w_in', 'q_norm_g', 'k_norm_g', 'w_dw', 'b_dw', 'ln_g', 'ln_b', 'w_pw', 'b_pw', 'w_out', 'loss_target', 'm_norm_g', 'm_w_ada', 'm_b_ada', 'm_w_in', 'm_q_norm_g', 'm_k_norm_g', 'm_w_dw', 'm_b_dw', 'm_ln_g', 'm_ln_b', 'm_w_pw', 'm_b_pw', 'm_w_out', 'v_norm_g', 'v_w_ada', 'v_b_ada', 'v_w_in', 'v_q_norm_g', 'v_k_norm_g', 'v_w_dw', 'v_b_dw', 'v_ln_g', 'v_ln_b', 'v_w_pw', 'v_b_pw', 'v_w_out']
TWIN_OUTPUTS = ['loss', 'grad_x', 'grad_norm_g', 'grad_w_ada', 'grad_b_ada', 'grad_w_in', 'grad_q_norm_g', 'grad_k_norm_g', 'grad_w_dw', 'grad_b_dw', 'grad_ln_g', 'grad_ln_b', 'grad_w_pw', 'grad_b_pw', 'grad_w_out', 'delta_norm_g', 'delta_w_ada', 'delta_b_ada', 'delta_w_in', 'delta_q_norm_g', 'delta_k_norm_g', 'delta_w_dw', 'delta_b_dw', 'delta_ln_g', 'delta_ln_b', 'delta_w_pw', 'delta_b_pw', 'delta_w_out', 'new_m_norm_g', 'new_m_w_ada', 'new_m_b_ada', 'new_m_w_in', 'new_m_q_norm_g', 'new_m_k_norm_g', 'new_m_w_dw', 'new_m_b_dw', 'new_m_ln_g', 'new_m_ln_b', 'new_m_w_pw', 'new_m_b_pw', 'new_m_w_out', 'new_v_norm_g', 'new_v_w_ada', 'new_v_b_ada', 'new_v_w_in', 'new_v_q_norm_g', 'new_v_k_norm_g', 'new_v_w_dw', 'new_v_b_dw', 'new_v_ln_g', 'new_v_ln_b', 'new_v_w_pw', 'new_v_b_pw', 'new_v_w_out']
TWIN_LEAF_KINDS = {'loss': 'loss', 'grad_x': 'grad_x', 'grad_norm_g': 'grad_w', 'grad_w_ada': 'grad_w', 'grad_b_ada': 'grad_w', 'grad_w_in': 'grad_w', 'grad_q_norm_g': 'grad_w', 'grad_k_norm_g': 'grad_w', 'grad_w_dw': 'grad_w', 'grad_b_dw': 'grad_w', 'grad_ln_g': 'grad_w', 'grad_ln_b': 'grad_w', 'grad_w_pw': 'grad_w', 'grad_b_pw': 'grad_w', 'grad_w_out': 'grad_w', 'delta_norm_g': 'delta_w', 'delta_w_ada': 'delta_w', 'delta_b_ada': 'delta_w', 'delta_w_in': 'delta_w', 'delta_q_norm_g': 'delta_w', 'delta_k_norm_g': 'delta_w', 'delta_w_dw': 'delta_w', 'delta_b_dw': 'delta_w', 'delta_ln_g': 'delta_w', 'delta_ln_b': 'delta_w', 'delta_w_pw': 'delta_w', 'delta_b_pw': 'delta_w', 'delta_w_out': 'delta_w', 'new_m_norm_g': 'new_m', 'new_m_w_ada': 'new_m', 'new_m_b_ada': 'new_m', 'new_m_w_in': 'new_m', 'new_m_q_norm_g': 'new_m', 'new_m_k_norm_g': 'new_m', 'new_m_w_dw': 'new_m', 'new_m_b_dw': 'new_m', 'new_m_ln_g': 'new_m', 'new_m_ln_b': 'new_m', 'new_m_w_pw': 'new_m', 'new_m_b_pw': 'new_m', 'new_m_w_out': 'new_m', 'new_v_norm_g': 'new_v', 'new_v_w_ada': 'new_v', 'new_v_b_ada': 'new_v', 'new_v_w_in': 'new_v', 'new_v_q_norm_g': 'new_v', 'new_v_k_norm_g': 'new_v', 'new_v_w_dw': 'new_v', 'new_v_b_dw': 'new_v', 'new_v_ln_g': 'new_v', 'new_v_ln_b': 'new_v', 'new_v_w_pw': 'new_v', 'new_v_b_pw': 'new_v', 'new_v_w_out': 'new_v'}


def _forward(args):
    return _fwd_reference(*[args[k] for k in FWD_PARAMS])


def _output_shape():
    out = _jax.eval_shape(lambda: _forward(_fwd_setup_inputs(0)))
    return out.shape, out.dtype

N_MICROBATCH = 1
ADAM_LR = 0.001
ADAM_B1 = 0.9
ADAM_B2 = 0.999
ADAM_EPS = 1e-08
ADAM_WD = 0.01
ADAM_STEP = 10
PER_EXAMPLE_BATCH_AXIS = {'x': 0, 'c': 0, 'loss_target': 0}
SHARED_INPUTS = []
_WEIGHT_DTYPES = {'norm_g': _jnp.float32, 'w_ada': _jnp.float32, 'b_ada': _jnp.float32, 'w_in': _jnp.float32, 'q_norm_g': _jnp.float32, 'k_norm_g': _jnp.float32, 'w_dw': _jnp.float32, 'b_dw': _jnp.float32, 'ln_g': _jnp.float32, 'ln_b': _jnp.float32, 'w_pw': _jnp.float32, 'b_pw': _jnp.float32, 'w_out': _jnp.float32}
MOMENT_SCALE = {'norm_g': 2.312585e-01, 'w_ada': 8.833316e-02, 'b_ada': 2.011720e-01, 'w_in': 1.124152e-02, 'q_norm_g': 3.301104e-01, 'k_norm_g': 3.303629e-01, 'w_dw': 1.005938e-02, 'b_dw': 5.312840e-02, 'ln_g': 1.396550e-01, 'ln_b': 9.054667e-02, 'w_pw': 1.275212e-02, 'b_pw': 6.036844e-02, 'w_out': 9.034528e-03}


def _to_microbatches(a, axis):
    t = _jnp.moveaxis(a, axis, 0)
    t = t.reshape((N_MICROBATCH, t.shape[0] // N_MICROBATCH) + t.shape[1:])
    return _jnp.moveaxis(t, 1, axis + 1)


def setup_inputs(seed: int = 0) -> dict:
    inp = _fwd_setup_inputs(seed)
    key = _jax.random.fold_in(_jax.random.key(seed), 7919)
    shape, _ = _output_shape()
    out = dict(inp)
    out["loss_target"] = _jax.random.normal(_jax.random.fold_in(key, 0), shape, _jnp.float32)
    for i, name in enumerate(TWIN_WEIGHTS):
        w = inp[name].astype(_jnp.float32)
        if MOMENT_SCALE is None:
            s = _jnp.sqrt(_jnp.mean(_jnp.square(w)) + 1e-30)
        else:
            s = MOMENT_SCALE[name]
        km, kv = _jax.random.split(_jax.random.fold_in(key, i + 1))
        out[name] = w
        out["m_" + name] = s * _jax.random.normal(km, w.shape, _jnp.float32)
        out["v_" + name] = (s * s) * _jax.random.uniform(kv, w.shape, _jnp.float32, 0.5, 1.5)
    if N_MICROBATCH > 1:
        for name, axis in PER_EXAMPLE_BATCH_AXIS.items():
            out[name] = _to_microbatches(out[name], axis)
    return {'x': out['x'], 'c': out['c'], 'norm_g': out['norm_g'], 'w_ada': out['w_ada'], 'b_ada': out['b_ada'], 'w_in': out['w_in'], 'q_norm_g': out['q_norm_g'], 'k_norm_g': out['k_norm_g'], 'w_dw': out['w_dw'], 'b_dw': out['b_dw'], 'ln_g': out['ln_g'], 'ln_b': out['ln_b'], 'w_pw': out['w_pw'], 'b_pw': out['b_pw'], 'w_out': out['w_out'], 'loss_target': out['loss_target'], 'm_norm_g': out['m_norm_g'], 'm_w_ada': out['m_w_ada'], 'm_b_ada': out['m_b_ada'], 'm_w_in': out['m_w_in'], 'm_q_norm_g': out['m_q_norm_g'], 'm_k_norm_g': out['m_k_norm_g'], 'm_w_dw': out['m_w_dw'], 'm_b_dw': out['m_b_dw'], 'm_ln_g': out['m_ln_g'], 'm_ln_b': out['m_ln_b'], 'm_w_pw': out['m_w_pw'], 'm_b_pw': out['m_b_pw'], 'm_w_out': out['m_w_out'], 'v_norm_g': out['v_norm_g'], 'v_w_ada': out['v_w_ada'], 'v_b_ada': out['v_b_ada'], 'v_w_in': out['v_w_in'], 'v_q_norm_g': out['v_q_norm_g'], 'v_k_norm_g': out['v_k_norm_g'], 'v_w_dw': out['v_w_dw'], 'v_b_dw': out['v_b_dw'], 'v_ln_g': out['v_ln_g'], 'v_ln_b': out['v_ln_b'], 'v_w_pw': out['v_w_pw'], 'v_b_pw': out['v_b_pw'], 'v_w_out': out['v_w_out']}


def _loss(weights, diff, rest, loss_target):
    with _jax.named_scope("forward"):
        args = {**rest, TWIN_DIFF_INPUT: diff, **{k: w.astype(_WEIGHT_DTYPES[k]) for k, w in weights.items()}}
        y = _forward(args)
    with _jax.named_scope("loss_head"):
        err = _jnp.square(y.astype(_jnp.float32) - loss_target)
        return 0.5 * _jnp.sum(_jnp.mean(err, axis=-1)) if err.ndim else 0.5 * err


def _adamw(w, g, m, v):
    m = ADAM_B1 * m + (1.0 - ADAM_B1) * g
    v = ADAM_B2 * v + (1.0 - ADAM_B2) * _jnp.square(g)
    m_hat = m / (1.0 - ADAM_B1 ** ADAM_STEP)
    v_hat = v / (1.0 - ADAM_B2 ** ADAM_STEP)
    delta = -ADAM_LR * (m_hat / (_jnp.sqrt(v_hat) + ADAM_EPS) + ADAM_WD * w)
    return delta, m, v


def reference(x, c, norm_g, w_ada, b_ada, w_in, q_norm_g, k_norm_g, w_dw, b_dw, ln_g, ln_b, w_pw, b_pw, w_out, loss_target, m_norm_g, m_w_ada, m_b_ada, m_w_in, m_q_norm_g, m_k_norm_g, m_w_dw, m_b_dw, m_ln_g, m_ln_b, m_w_pw, m_b_pw, m_w_out, v_norm_g, v_w_ada, v_b_ada, v_w_in, v_q_norm_g, v_k_norm_g, v_w_dw, v_b_dw, v_ln_g, v_ln_b, v_w_pw, v_b_pw, v_w_out):
    given = dict(x=x, c=c, norm_g=norm_g, w_ada=w_ada, b_ada=b_ada, w_in=w_in, q_norm_g=q_norm_g, k_norm_g=k_norm_g, w_dw=w_dw, b_dw=b_dw, ln_g=ln_g, ln_b=ln_b, w_pw=w_pw, b_pw=b_pw, w_out=w_out, loss_target=loss_target, m_norm_g=m_norm_g, m_w_ada=m_w_ada, m_b_ada=m_b_ada, m_w_in=m_w_in, m_q_norm_g=m_q_norm_g, m_k_norm_g=m_k_norm_g, m_w_dw=m_w_dw, m_b_dw=m_b_dw, m_ln_g=m_ln_g, m_ln_b=m_ln_b, m_w_pw=m_w_pw, m_b_pw=m_b_pw, m_w_out=m_w_out, v_norm_g=v_norm_g, v_w_ada=v_w_ada, v_b_ada=v_b_ada, v_w_in=v_w_in, v_q_norm_g=v_q_norm_g, v_k_norm_g=v_k_norm_g, v_w_dw=v_w_dw, v_b_dw=v_b_dw, v_ln_g=v_ln_g, v_ln_b=v_ln_b, v_w_pw=v_w_pw, v_b_pw=v_b_pw, v_w_out=v_w_out)
    weights = {n: given[n] for n in TWIN_WEIGHTS}
    shared = {n: given[n] for n in SHARED_INPUTS}
    per_example = {n: given[n] for n in ['x', 'c']}
    grad_fn = _jax.value_and_grad(_loss, argnums=(0, 1))

    def one_microbatch(ex, loss_target):
        ex = dict(ex)
        diff = ex.pop(TWIN_DIFF_INPUT)
        return grad_fn(weights, diff, {**shared, **ex}, loss_target)

    if N_MICROBATCH == 1:
        loss, (grad_w, grad_x) = one_microbatch(per_example, given["loss_target"])
    else:
        def body(carry, xs):
            loss_sum, grad_sum = carry
            l_k, (gw_k, gx_k) = one_microbatch(xs[0], xs[1])
            with _jax.named_scope("update"):
                return (loss_sum + l_k, _jax.tree.map(_jnp.add, grad_sum, gw_k)), gx_k

        init = (_jnp.zeros((), _jnp.float32), _jax.tree.map(_jnp.zeros_like, weights))
        (loss, grad_w), grad_x = _jax.lax.scan(body, init, (per_example, given["loss_target"]))
    with _jax.named_scope("update"):
        delta_w, new_m, new_v = {}, {}, {}
        for n in TWIN_WEIGHTS:
            delta_w[n], new_m[n], new_v[n] = _adamw(weights[n], grad_w[n], given["m_" + n], given["v_" + n])
    return (loss, grad_x, *[grad_w[n] for n in TWIN_WEIGHTS], *[delta_w[n] for n in TWIN_WEIGHTS],
            *[new_m[n] for n in TWIN_WEIGHTS], *[new_v[n] for n in TWIN_WEIGHTS])
```

```python
import functools

import jax
import jax.numpy as jnp
from jax import lax
from jax.experimental import pallas as pl
from jax.experimental.pallas import tpu as pltpu

F32 = jnp.float32
BF16 = jnp.bfloat16
NDEV = 8
HEAD_DIM = 128
LANES = 128
HALO = 32
EPS = 1e-6
VMEM_LIMIT = 56 * 1024 * 1024
MESH = pl.DeviceIdType.MESH

ADAM_LR = 0.001
ADAM_B1 = 0.9
ADAM_B2 = 0.999
ADAM_EPS = 1e-08
ADAM_WD = 0.01
ADAM_STEP = 10


def _params(sem=None):
    return pltpu.CompilerParams(dimension_semantics=sem, vmem_limit_bytes=VMEM_LIMIT)


def _pick(n, pref, unit=LANES):
    best = None
    for d in range(unit, min(n, pref) + 1, unit):
        if n % d == 0:
            best = d
    return best if best is not None else n


def _sigmoid(z):
    return 1.0 / (1.0 + jnp.exp(-z))


def _silu(z):
    return z * _sigmoid(z)


def _dsilu(z):
    s = _sigmoid(z)
    return s * (1.0 + z * (1.0 - s))


def _softplus(z):
    return jnp.maximum(z, 0.0) + jnp.log(1.0 + jnp.exp(-jnp.abs(z)))


def _dot(a, b, dims):
    return lax.dot_general(a, b, (dims, ((), ())), preferred_element_type=F32)


NN = ((1,), (0,))
NT = ((1,), (1,))
TN = ((0,), (0,))


def _adam(w, g, m, v):
    m = ADAM_B1 * m + (1.0 - ADAM_B1) * g
    v = ADAM_B2 * v + (1.0 - ADAM_B2) * (g * g)
    m_hat = m / (1.0 - ADAM_B1 ** ADAM_STEP)
    v_hat = v / (1.0 - ADAM_B2 ** ADAM_STEP)
    delta = -ADAM_LR * (m_hat / (jnp.sqrt(v_hat) + ADAM_EPS) + ADAM_WD * w)
    return delta, m, v


def _place():
    x, y, c = lax.axis_index("x"), lax.axis_index("y"), lax.axis_index("c")
    return x, y, c


def _flip(v, bit):
    return 1 - v if bit else v


def _all_gather(arrs, name, space):
    n = len(arrs)

    def body(*refs):
        ins, outs = refs[:n], refs[n:2 * n]
        send_sems, recv_sems, local_sems = refs[2 * n:]
        x, y, c = _place()
        me, sibling = (x, y, c), (x, y, 1 - c)
        chips = [(1 - x, y), (x, 1 - y), (1 - x, 1 - y)]

        def rows(a, p):
            return outs[a].at[4 * p[0] + 2 * p[1] + p[2]]

        def copy(a, k, block, to, src=None):
            return pltpu.make_async_remote_copy(
                src_ref=rows(a, block) if src is None else src, dst_ref=rows(a, block),
                send_sem=send_sems.at[7 * a + k], recv_sem=recv_sems.at[7 * a + k],
                device_id=to, device_id_type=MESH)

        mine = [pltpu.make_async_copy(ins[a], rows(a, me), local_sems.at[a]) for a in range(n)]
        for cp in mine:
            cp.start()
        first = []
        for a in range(n):
            first.append(copy(a, 0, me, sibling, src=ins[a]))
            first += [copy(a, 1 + j, me, (*chip, c), src=ins[a]) for j, chip in enumerate(chips)]
        for cp in first:
            cp.start()
        passed = []
        for j, chip in enumerate(chips):
            for a in range(n):
                copy(a, 1 + j, (*chip, c), me).wait_recv()
                fwd = copy(a, 4 + j, (*chip, c), sibling)
                fwd.start()
                passed.append(fwd)
        for a in range(n):
            copy(a, 0, sibling, me).wait_recv()
            for j, chip in enumerate(chips):
                copy(a, 4 + j, (*chip, 1 - c), me).wait_recv()
        for cp in first + passed:
            cp.wait_send()
        for cp in mine:
            cp.wait()

    spec = pl.BlockSpec(memory_space=space)
    return pl.pallas_call(
        body, name=name,
        out_shape=[jax.ShapeDtypeStruct((NDEV,) + a.shape, a.dtype) for a in arrs],
        in_specs=[spec] * n, out_specs=[spec] * n,
        scratch_shapes=[pltpu.SemaphoreType.DMA((7 * n,)), pltpu.SemaphoreType.DMA((7 * n,)),
                        pltpu.SemaphoreType.DMA((n,))],
        compiler_params=pltpu.CompilerParams(vmem_limit_bytes=VMEM_LIMIT),
    )(*arrs)


def _all_to_all(arrs, name):
    n = len(arrs)

    def body(*refs):
        ins, outs = refs[:n], refs[n:2 * n]
        send_sems, recv_sems, local_sems = refs[2 * n:]
        x, y, c = _place()
        my = 4 * x + 2 * y + c
        mine = [pltpu.make_async_copy(ins[a].at[my], outs[a].at[my], local_sems.at[a]) for a in range(n)]
        for cp in mine:
            cp.start()
        sent = []
        for k in range(1, NDEV):
            peer = (_flip(x, k & 4), _flip(y, k & 2), _flip(c, k & 1))
            peer_idx = 4 * peer[0] + 2 * peer[1] + peer[2]
            for a in range(n):
                cp = pltpu.make_async_remote_copy(
                    src_ref=ins[a].at[peer_idx], dst_ref=outs[a].at[my],
                    send_sem=send_sems.at[7 * a + k - 1], recv_sem=recv_sems.at[7 * a + k - 1],
                    device_id=peer, device_id_type=MESH)
                cp.start()
                sent.append(cp)
        for k in range(1, NDEV):
            peer = (_flip(x, k & 4), _flip(y, k & 2), _flip(c, k & 1))
            peer_idx = 4 * peer[0] + 2 * peer[1] + peer[2]
            for a in range(n):
                pltpu.make_async_remote_copy(
                    src_ref=ins[a].at[my], dst_ref=outs[a].at[peer_idx],
                    send_sem=send_sems.at[7 * a + k - 1], recv_sem=recv_sems.at[7 * a + k - 1],
                    device_id=peer, device_id_type=MESH).wait_recv()
        for cp in sent:
            cp.wait_send()
        for cp in mine:
            cp.wait()

    spec = pl.BlockSpec(memory_space=pl.ANY)
    return pl.pallas_call(
        body, name=name,
        out_shape=[jax.ShapeDtypeStruct(a.shape, a.dtype) for a in arrs],
        in_specs=[spec] * n, out_specs=[spec] * n,
        scratch_shapes=[pltpu.SemaphoreType.DMA((7 * n,)), pltpu.SemaphoreType.DMA((7 * n,)),
                        pltpu.SemaphoreType.DMA((n,))],
        compiler_params=pltpu.CompilerParams(vmem_limit_bytes=VMEM_LIMIT),
    )(*arrs)


def _ada_matmul(c_all, w_loc, b_loc):
    d, n = w_loc.shape
    bn = _pick(n, 512)

    def body(c_ref, w_ref, b_ref, o_ref):
        s = _silu(c_ref[...]).astype(BF16)
        o_ref[...] = _dot(s, w_ref[...].astype(BF16), NN) + b_ref[...]

    return pl.pallas_call(
        body, name="ada_matmul", grid=(n // bn,),
        out_shape=jax.ShapeDtypeStruct((NDEV, n), F32),
        in_specs=[pl.BlockSpec((NDEV, d), lambda j: (0, 0)), pl.BlockSpec((d, bn), lambda j: (0, j)),
                  pl.BlockSpec((1, bn), lambda j: (0, j))],
        out_specs=pl.BlockSpec((NDEV, bn), lambda j: (0, j)),
        compiler_params=_params(("arbitrary",)),
    )(c_all, w_loc, b_loc)


def _modulated_norm(x, norm_g, mod):
    t, d = x.shape
    tt = _pick(t, 256, 16)

    def body(x_ref, g_ref, mod_ref, h_ref):
        xv = x_ref[...]
        r = lax.rsqrt(jnp.mean(xv * xv, axis=-1, keepdims=True) + EPS)
        h = (xv * r) * g_ref[...] * (1.0 + mod_ref[1:2, :]) + mod_ref[0:1, :]
        h_ref[...] = h.astype(BF16)

    return pl.pallas_call(
        body, name="modulated_norm", grid=(t // tt,),
        out_shape=jax.ShapeDtypeStruct((t, d), BF16),
        in_specs=[pl.BlockSpec((tt, d), lambda i: (i, 0)), pl.BlockSpec((1, d), lambda i: (0, 0)),
                  pl.BlockSpec((3, d), lambda i: (0, 0))],
        out_specs=pl.BlockSpec((tt, d), lambda i: (i, 0)),
        compiler_params=_params(("arbitrary",)),
    )(x, norm_g, mod)


def _proj_matmul(h, wg):
    t, d = h.shape
    ns = wg.shape[2]
    tm, tk = _pick(t, 512, 16), _pick(d, 1024)
    nk = d // tk

    def body(a_ref, b_ref, o_ref, acc):
        k = pl.program_id(2)

        @pl.when(k == 0)
        def _():
            acc[...] = jnp.zeros_like(acc)

        acc[...] += _dot(a_ref[...], b_ref[...], NN)

        @pl.when(k == nk - 1)
        def _():
            o_ref[...] = acc[...]

    return pl.pallas_call(
        body, name="proj_matmul", grid=(t // tm, NDEV, nk),
        out_shape=jax.ShapeDtypeStruct((t, NDEV * ns), F32),
        in_specs=[pl.BlockSpec((tm, tk), lambda m, j, k: (m, k)),
                  pl.BlockSpec((None, tk, ns), lambda m, j, k: (j, k, 0))],
        out_specs=pl.BlockSpec((tm, ns), lambda m, j, k: (m, j)),
        scratch_shapes=[pltpu.VMEM((tm, ns), F32)],
        compiler_params=_params(("arbitrary", "arbitrary", "arbitrary")),
    )(h, wg)


def _attention_fwd(proj, qg, kg, nh, d_model):
    t = proj.shape[0]
    tq = _pick(t, 256, 16)
    nq = t // tq
    scale = HEAD_DIM ** -0.5

    def body(q_ref, k_ref, v_ref, g_ref, qg_ref, kg_ref, o_ref, tot_ref, y_ref, qn, kn, vb):
        def norm(src, gain, dst):
            v = src[...]
            r = lax.rsqrt(jnp.mean(v * v, axis=-1, keepdims=True) + EPS)
            dst[...] = ((v * r) * gain[...]).astype(BF16)

        norm(q_ref, qg_ref, qn)
        norm(k_ref, kg_ref, kn)
        vb[...] = v_ref[...].astype(BF16)
        row = lax.broadcasted_iota(jnp.int32, (tq, tq), 0)
        col = lax.broadcasted_iota(jnp.int32, (tq, tq), 1)
        upper = (row > col).astype(BF16)
        causal = col < row

        def block(qi, j, carry, acc, masked):
            ks = pl.ds(pl.multiple_of(j * tq, tq), tq)
            z = _dot(qi, kn[ks, :], NT) * scale
            sp = _softplus(z)
            ls = -sp
            if masked:
                ls = jnp.where(causal, ls, 0.0)
            hi = ls.astype(BF16)
            lo = (ls - hi.astype(F32)).astype(BF16)
            after = _dot(hi, upper, NN) + _dot(lo, upper, NN)
            w = jnp.exp(z - sp + after + carry)
            if masked:
                w = jnp.where(causal, w, 0.0)
            acc = acc + _dot(w.astype(BF16), vb[ks, :], NN)
            carry = carry + jnp.sum(ls, axis=1, keepdims=True)
            return carry, acc

        def q_block(i, _):
            qs = pl.ds(pl.multiple_of(i * tq, tq), tq)
            qi = qn[qs, :]
            carry, acc = block(qi, i, jnp.zeros((tq, 1), F32), jnp.zeros((tq, HEAD_DIM), F32), True)

            def k_step(jj, ca):
                return block(qi, i - 1 - jj, ca[0], ca[1], False)

            carry, acc = lax.fori_loop(0, i, k_step, (carry, acc))
            o_ref[qs, :] = acc
            tot_ref[qs, :] = jnp.broadcast_to(carry, (tq, HEAD_DIM))
            y_ref[qs, :] = (acc * _silu(g_ref[qs, :])).astype(BF16)
            return 0

        lax.fori_loop(0, nq, q_block, 0)

    col_block = lambda off: pl.BlockSpec((t, HEAD_DIM), lambda h: (0, off + h))
    vec = pl.BlockSpec((1, HEAD_DIM), lambda h: (0, 0))
    return pl.pallas_call(
        body, name="attention_fwd", grid=(nh,),
        out_shape=[jax.ShapeDtypeStruct((t, nh * HEAD_DIM), F32), jax.ShapeDtypeStruct((t, nh * HEAD_DIM), F32),
                   jax.ShapeDtypeStruct((t, d_model), BF16)],
        in_specs=[col_block(0), col_block(nh), col_block(2 * nh), col_block(3 * nh), vec, vec],
        out_specs=[col_block(0), col_block(0), col_block(0)],
        scratch_shapes=[pltpu.VMEM((t, HEAD_DIM), BF16)] * 3,
        compiler_params=_params(("arbitrary",)),
    )(proj, proj, proj, proj, qg, kg)


def _conv_taps(buf, w_ref, kc, lo, n_rows, cw, first_tap_row):
    acc = None
    for j in range(kc):
        term = w_ref[j:j + 1, lo:lo + cw] * buf[pl.ds(first_tap_row(j), n_rows), pl.ds(lo, cw)]
        acc = term if acc is None else acc + term
    return acc


def _glu_rows(u_ref, g_ref):
    return u_ref[...] * _sigmoid(g_ref[...])


def _conv_fwd(proj, w_dw, b_dw, ln_g, ln_b, kc, cw):
    t = proj.shape[0]
    tt = _pick(t, 128, HALO)
    per = tt // HALO
    chunk = _pick(cw, 256)

    def body(u_ref, g_ref, up_ref, gp_ref, w_ref, b_ref, lg_ref, lb_ref, a_ref, hc_ref, buf):
        i = pl.program_id(0)
        buf[pl.ds(HALO, tt), :] = _glu_rows(u_ref, g_ref)
        halo = _glu_rows(up_ref, gp_ref)
        buf[pl.ds(0, HALO), :] = jnp.where(i > 0, halo, 0.0)
        for lo in range(0, cw, chunk):
            conv = _conv_taps(buf, w_ref, kc, lo, tt, chunk, lambda j: HALO - (kc - 1) + j)
            hc_ref[:, lo:lo + chunk] = conv + b_ref[:, lo:lo + chunk]
        hc = hc_ref[...]
        mu = jnp.mean(hc, axis=-1, keepdims=True)
        xc = hc - mu
        var = jnp.mean(xc * xc, axis=-1, keepdims=True)
        ln = xc * lax.rsqrt(var + EPS) * lg_ref[...] + lb_ref[...]
        a_ref[...] = _silu(ln).astype(BF16)

    ncol = proj.shape[1] // cw
    tile = lambda g: pl.BlockSpec((tt, cw), lambda i: (i, g))
    prev = lambda g: pl.BlockSpec((HALO, cw), lambda i: (jnp.maximum(i * per - 1, 0), g))
    full = lambda r: pl.BlockSpec((r, cw), lambda i: (0, 0))
    return pl.pallas_call(
        body, name="conv_fwd", grid=(t // tt,),
        out_shape=[jax.ShapeDtypeStruct((t, cw), BF16), jax.ShapeDtypeStruct((t, cw), F32)],
        in_specs=[tile(ncol - 3), tile(ncol - 2), prev(ncol - 3), prev(ncol - 2),
                  full(w_dw.shape[0]), full(1), full(1), full(1)],
        out_specs=[pl.BlockSpec((tt, cw), lambda i: (i, 0))] * 2,
        scratch_shapes=[pltpu.VMEM((HALO + tt, cw), F32)],
        compiler_params=_params(("arbitrary",)),
    )(proj, proj, proj, proj, w_dw, b_dw, ln_g, ln_b)


def _pointwise_fwd(a, wpw, b_pw, proj, ycat, cw):
    t = a.shape[0]
    rows = wpw.shape[1]
    tm = _pick(t, 512, 16)
    ncol = proj.shape[1] // cw
    ycol = ycat.shape[1] // cw - 1

    def body(a_ref, w_ref, b_ref, g_ref, y_in, z_ref, y_ref, acc):
        k = pl.program_id(1)

        @pl.when(k == 0)
        def _():
            acc[...] = jnp.zeros_like(acc)

        acc[...] += _dot(a_ref[...], w_ref[...], NN)

        @pl.when(k == NDEV - 1)
        def _():
            z = acc[...] + b_ref[...]
            z_ref[...] = z
            y_ref[...] = (z * _silu(g_ref[...])).astype(BF16)

    return pl.pallas_call(
        body, name="pointwise_fwd", grid=(t // tm, NDEV),
        out_shape=[jax.ShapeDtypeStruct((t, cw), F32), jax.ShapeDtypeStruct(ycat.shape, ycat.dtype)],
        in_specs=[pl.BlockSpec((tm, rows), lambda m, k: (m, k)),
                  pl.BlockSpec((None, rows, cw), lambda m, k: (k, 0, 0)),
                  pl.BlockSpec((1, cw), lambda m, k: (0, 0)),
                  pl.BlockSpec((tm, cw), lambda m, k: (m, ncol - 1)),
                  pl.BlockSpec(memory_space=pl.ANY)],
        out_specs=[pl.BlockSpec((tm, cw), lambda m, k: (m, 0)), pl.BlockSpec((tm, cw), lambda m, k: (m, ycol))],
        scratch_shapes=[pltpu.VMEM((tm, cw), F32)],
        input_output_aliases={4: 1},
        compiler_params=_params(("arbitrary", "arbitrary")),
    )(a, wpw, b_pw, proj, ycat)


def _out_matmul(ycat, wout, x, target, mod):
    t, d = x.shape
    rows = wout.shape[1]
    tm, tn = _pick(t, 512, 16), _pick(d, 1024)
    inv_d = 1.0 / d

    def body(a_ref, w_ref, x_ref, tg_ref, mod_ref, dout_ref, dy_ref, sums_ref, acc):
        m, k = pl.program_id(1), pl.program_id(2)

        @pl.when(k == 0)
        def _():
            acc[...] = jnp.zeros_like(acc)

        acc[...] += _dot(a_ref[...], w_ref[...], NN)

        @pl.when((k == NDEV - 1) & (m == 0))
        def _():
            sums_ref[...] = jnp.zeros_like(sums_ref)

        @pl.when(k == NDEV - 1)
        def _():
            y = acc[...]
            gate = mod_ref[2:3, :]
            err = (x_ref[...] + gate * y) - tg_ref[...]
            dout = err * inv_d
            dout_ref[...] = dout
            dy_ref[...] = (dout * gate).astype(BF16)
            sums_ref[0:1, :] += jnp.sum(dout * y, axis=0, keepdims=True)
            sums_ref[1:2, :] += jnp.sum(err * err, axis=0, keepdims=True)

    mn = lambda n, m, k: (m, n)
    return pl.pallas_call(
        body, name="out_matmul", grid=(d // tn, t // tm, NDEV),
        out_shape=[jax.ShapeDtypeStruct((t, d), F32), jax.ShapeDtypeStruct((t, d), BF16),
                   jax.ShapeDtypeStruct((8, d), F32)],
        in_specs=[pl.BlockSpec((tm, rows), lambda n, m, k: (m, k)),
                  pl.BlockSpec((None, rows, tn), lambda n, m, k: (k, 0, n)),
                  pl.BlockSpec((tm, tn), mn), pl.BlockSpec((tm, tn), mn),
                  pl.BlockSpec((3, tn), lambda n, m, k: (0, n))],
        out_specs=[pl.BlockSpec((tm, tn), mn), pl.BlockSpec((tm, tn), mn),
                   pl.BlockSpec((8, tn), lambda n, m, k: (0, n))],
        scratch_shapes=[pltpu.VMEM((tm, tn), F32)],
        compiler_params=_params(("arbitrary", "arbitrary", "arbitrary")),
    )(ycat, wout, x, target, mod)


def _matmul_nt_slabs(a, wg, name, out_dtype=F32):
    t, kdim = a.shape
    r = wg.shape[1]
    tm, tk = _pick(t, 512, 16), _pick(kdim, 2048)
    nk = kdim // tk

    def body(a_ref, b_ref, o_ref, acc):
        k = pl.program_id(2)

        @pl.when(k == 0)
        def _():
            acc[...] = jnp.zeros_like(acc)

        acc[...] += _dot(a_ref[...], b_ref[...], NT)

        @pl.when(k == nk - 1)
        def _():
            o_ref[...] = acc[...].astype(out_dtype)

    return pl.pallas_call(
        body, name=name, grid=(t // tm, NDEV, nk),
        out_shape=jax.ShapeDtypeStruct((t, NDEV * r), out_dtype),
        in_specs=[pl.BlockSpec((tm, tk), lambda m, j, k: (m, k)),
                  pl.BlockSpec((None, r, tk), lambda m, j, k: (j, 0, k))],
        out_specs=pl.BlockSpec((tm, r), lambda m, j, k: (m, j)),
        scratch_shapes=[pltpu.VMEM((tm, r), F32)],
        compiler_params=_params(("arbitrary", "arbitrary", "arbitrary")),
    )(a, wg)


def _weight_grad_row_slabs(a, b, name):
    t = a.shape[0]
    r = a.shape[1] // NDEV
    n = b.shape[1]
    tn, tt = _pick(n, 2048), _pick(t, 512, 16)
    nt = t // tt

    def body(a_ref, b_ref, o_ref, acc):
        k = pl.program_id(2)

        @pl.when(k == 0)
        def _():
            acc[...] = jnp.zeros_like(acc)

        acc[...] += _dot(a_ref[...], b_ref[...], TN)

        @pl.when(k == nt - 1)
        def _():
            o_ref[...] = acc[...].astype(BF16)

    return pl.pallas_call(
        body, name=name, grid=(NDEV, n // tn, nt),
        out_shape=jax.ShapeDtypeStruct((NDEV, r, n), BF16),
        in_specs=[pl.BlockSpec((tt, r), lambda j, c, k: (k, j)), pl.BlockSpec((tt, tn), lambda j, c, k: (k, c))],
        out_specs=pl.BlockSpec((None, r, tn), lambda j, c, k: (j, 0, c)),
        scratch_shapes=[pltpu.VMEM((r, tn), F32)],
        compiler_params=_params(("arbitrary", "arbitrary", "arbitrary")),
    )(a, b)


def _weight_grad_col_slabs(h, dproj, ns):
    t, d = h.shape
    tm, tt = _pick(d, 512), _pick(t, 512, 16)
    nt = t // tt

    def body(a_ref, b_ref, o_ref, acc):
        k = pl.program_id(2)

        @pl.when(k == 0)
        def _():
            acc[...] = jnp.zeros_like(acc)

        acc[...] += _dot(a_ref[...], b_ref[...], TN)

        @pl.when(k == nt - 1)
        def _():
            o_ref[...] = acc[...].astype(BF16)

    return pl.pallas_call(
        body, name="w_in_grad", grid=(NDEV, d // tm, nt),
        out_shape=jax.ShapeDtypeStruct((NDEV, d, ns), BF16),
        in_specs=[pl.BlockSpec((tt, tm), lambda j, m, k: (k, m)), pl.BlockSpec((tt, ns), lambda j, m, k: (k, j))],
        out_specs=pl.BlockSpec((None, tm, ns), lambda j, m, k: (j, m, 0)),
        scratch_shapes=[pltpu.VMEM((tm, ns), F32)],
        compiler_params=_params(("arbitrary", "arbitrary", "arbitrary")),
    )(h, dproj)


def _dh_matmul(dproj, wg):
    t = dproj.shape[0]
    d, ns = wg.shape[1], wg.shape[2]
    tm, tn = _pick(t, 512, 16), _pick(d, 1024)

    def body(a_ref, b_ref, o_ref, acc):
        k = pl.program_id(2)

        @pl.when(k == 0)
        def _():
            acc[...] = jnp.zeros_like(acc)

        acc[...] += _dot(a_ref[...], b_ref[...], NT)

        @pl.when(k == NDEV - 1)
        def _():
            o_ref[...] = acc[...]

    return pl.pallas_call(
        body, name="dh_matmul", grid=(t // tm, d // tn, NDEV),
        out_shape=jax.ShapeDtypeStruct((t, d), F32),
        in_specs=[pl.BlockSpec((tm, ns), lambda m, n, k: (m, k)),
                  pl.BlockSpec((None, tn, ns), lambda m, n, k: (k, n, 0))],
        out_specs=pl.BlockSpec((tm, tn), lambda m, n, k: (m, n)),
        scratch_shapes=[pltpu.VMEM((tm, tn), F32)],
        compiler_params=_params(("arbitrary", "arbitrary", "arbitrary")),
    )(dproj, wg)


def _attention_bwd(proj, o, tot, dycat, qg, kg, nh):
    t, in_cols = proj.shape
    tq = _pick(t, 256, 16)
    nq = t // tq
    scale = HEAD_DIM ** -0.5

    def body(q_ref, k_ref, v_ref, g_ref, o_ref, tot_ref, dy_ref, qg_ref, kg_ref, dproj_ref, gains_ref,
             qn, kn, vb, dob, dk_acc, dv_acc, dq_acc, outs, sems):
        h = pl.program_id(0)

        def norm(src, gain, dst):
            v = src[...]
            r = lax.rsqrt(jnp.mean(v * v, axis=-1, keepdims=True) + EPS)
            dst[...] = ((v * r) * gain[...]).astype(BF16)

        norm(q_ref, qg_ref, qn)
        norm(k_ref, kg_ref, kn)
        vb[...] = v_ref[...].astype(BF16)
        gs = g_ref[...]
        dyv = dy_ref[...]
        dob[...] = (dyv * _silu(gs)).astype(BF16)
        outs[3] = (dyv * o_ref[...] * _dsilu(gs)).astype(BF16)
        dk_acc[...] = jnp.zeros_like(dk_acc)
        dv_acc[...] = jnp.zeros_like(dv_acc)

        row = lax.broadcasted_iota(jnp.int32, (tq, tq), 0)
        col = lax.broadcasted_iota(jnp.int32, (tq, tq), 1)
        incl = (col <= row).astype(BF16)
        excl = (col < row).astype(BF16)
        causal = row < col

        def block(i, j, qi, doi, tot_row, p_left, g_left, dq, masked):
            ks = pl.ds(pl.multiple_of(j * tq, tq), tq)
            kj = kn[ks, :]
            z = _dot(kj, qi, NT) * scale
            sp = _softplus(z)
            ls = -sp
            if masked:
                ls = jnp.where(causal, ls, 0.0)
            hi = ls.astype(BF16)
            lo = (ls - hi.astype(F32)).astype(BF16)
            p_inc = _dot(incl, hi, NN) + _dot(incl, lo, NN) + p_left
            beta = jnp.exp(z - sp)
            w = beta * jnp.exp(tot_row - p_inc)
            if masked:
                w = jnp.where(causal, w, 0.0)
            dw = _dot(vb[ks, :], doi, NT)
            g = w * dw
            g_before = _dot(excl, g.astype(BF16), NN) + g_left
            dz = g * (1.0 - beta) - beta * g_before
            if masked:
                dz = jnp.where(causal, dz, 0.0)
            dzb = dz.astype(BF16)
            dv_acc[ks, :] += _dot(w.astype(BF16), doi, NN)
            dk_acc[ks, :] += _dot(dzb, qi, NN)
            dq = dq + _dot(dzb, kj, TN)
            p_left = p_left + jnp.sum(ls, axis=0, keepdims=True)
            g_left = g_left + jnp.sum(g, axis=0, keepdims=True)
            return p_left, g_left, dq

        def q_block(i, _):
            qs = pl.ds(pl.multiple_of(i * tq, tq), tq)
            qi = qn[qs, :]
            doi = dob[qs, :]
            tot_row = jnp.transpose(tot_ref[qs, :])[0:1, :]
            zero_row = jnp.zeros((1, tq), F32)

            def k_step(j, carry):
                return block(i, j, qi, doi, tot_row, carry[0], carry[1], carry[2], False)

            carry = lax.fori_loop(0, i, k_step, (zero_row, zero_row, jnp.zeros((tq, HEAD_DIM), F32)))
            _, _, dq = block(i, i, qi, doi, tot_row, carry[0], carry[1], carry[2], True)
            dq_acc[qs, :] = dq * scale
            return 0

        lax.fori_loop(0, nq, q_block, 0)

        def norm_bwd(src, gain, dn, slot, gain_row):
            v = src[...]
            r = lax.rsqrt(jnp.mean(v * v, axis=-1, keepdims=True) + EPS)
            vhat = v * r
            gains_ref[gain_row:gain_row + 1, :] = jnp.sum(dn * vhat, axis=0, keepdims=True)
            dhat = dn * gain[...]
            outs[slot] = (r * (dhat - vhat * jnp.mean(dhat * vhat, axis=-1, keepdims=True))).astype(BF16)

        gains_ref[...] = jnp.zeros_like(gains_ref)
        norm_bwd(q_ref, qg_ref, dq_acc[...], 0, 0)
        norm_bwd(k_ref, kg_ref, dk_acc[...] * scale, 1, 1)
        outs[2] = dv_acc[...].astype(BF16)
        copies = [pltpu.make_async_copy(
            outs.at[s], dproj_ref.at[:, pl.ds(pl.multiple_of((s * nh + h) * HEAD_DIM, HEAD_DIM), HEAD_DIM)], sems.at[s])
            for s in range(4)]
        for cp in copies:
            cp.start()
        for cp in copies:
            cp.wait()

    col_block = lambda off: pl.BlockSpec((t, HEAD_DIM), lambda h: (0, off + h))
    vec = pl.BlockSpec((1, HEAD_DIM), lambda h: (0, 0))
    head_scr = lambda dt: pltpu.VMEM((t, HEAD_DIM), dt)
    return pl.pallas_call(
        body, name="attention_bwd", grid=(nh,),
        out_shape=[jax.ShapeDtypeStruct((t, in_cols), BF16), jax.ShapeDtypeStruct((nh, 8, HEAD_DIM), F32)],
        in_specs=[col_block(0), col_block(nh), col_block(2 * nh), col_block(3 * nh),
                  col_block(0), col_block(0), col_block(0), vec, vec],
        out_specs=[pl.BlockSpec(memory_space=pl.ANY), pl.BlockSpec((None, 8, HEAD_DIM), lambda h: (h, 0, 0))],
        scratch_shapes=[head_scr(BF16), head_scr(BF16), head_scr(BF16), head_scr(BF16),
                        head_scr(F32), head_scr(F32), head_scr(F32),
                        pltpu.VMEM((4, t, HEAD_DIM), BF16), pltpu.SemaphoreType.DMA((4,))],
        compiler_params=_params(("arbitrary",)),
    )(proj, proj, proj, proj, o, tot, dycat, qg, kg)


def _gate_bwd(dycat, z, proj, dproj, cw):
    t = z.shape[0]
    tt = _pick(t, 256, 16)
    ncol = proj.shape[1] // cw
    ycol = dycat.shape[1] // cw - 1

    def body(dy_ref, z_ref, g_ref, dp_in, dz_ref, dp_ref, sums_ref):
        i = pl.program_id(0)

        @pl.when(i == 0)
        def _():
            sums_ref[...] = jnp.zeros_like(sums_ref)

        g = g_ref[...]
        dy = dy_ref[...]
        dz = dy * _silu(g)
        dz_ref[...] = dz.astype(BF16)
        dp_ref[...] = (dy * z_ref[...] * _dsilu(g)).astype(BF16)
        sums_ref[0:1, :] += jnp.sum(dz, axis=0, keepdims=True)

    return pl.pallas_call(
        body, name="gate_bwd", grid=(t // tt,),
        out_shape=[jax.ShapeDtypeStruct((t, cw), BF16), jax.ShapeDtypeStruct(dproj.shape, dproj.dtype),
                   jax.ShapeDtypeStruct((8, cw), F32)],
        in_specs=[pl.BlockSpec((tt, cw), lambda i: (i, ycol)), pl.BlockSpec((tt, cw), lambda i: (i, 0)),
                  pl.BlockSpec((tt, cw), lambda i: (i, ncol - 1)), pl.BlockSpec(memory_space=pl.ANY)],
        out_specs=[pl.BlockSpec((tt, cw), lambda i: (i, 0)), pl.BlockSpec((tt, cw), lambda i: (i, ncol - 1)),
                   pl.BlockSpec((8, cw), lambda i: (0, 0))],
        input_output_aliases={3: 1},
        compiler_params=_params(("arbitrary",)),
    )(dycat, z, proj, dproj)


def _ln_bwd(da, hc, ln_g, ln_b):
    t, cw = hc.shape
    tt = _pick(t, 256, 16)

    def body(da_ref, hc_ref, lg_ref, lb_ref, dh_ref, sums_ref):
        i = pl.program_id(0)

        @pl.when(i == 0)
        def _():
            sums_ref[...] = jnp.zeros_like(sums_ref)

        hcv = hc_ref[...]
        mu = jnp.mean(hcv, axis=-1, keepdims=True)
        xc = hcv - mu
        r = lax.rsqrt(jnp.mean(xc * xc, axis=-1, keepdims=True) + EPS)
        xhat = xc * r
        ln = xhat * lg_ref[...] + lb_ref[...]
        dln = da_ref[...] * _dsilu(ln)
        dxhat = dln * lg_ref[...]
        dhc = r * (dxhat - jnp.mean(dxhat, axis=-1, keepdims=True)
                   - xhat * jnp.mean(dxhat * xhat, axis=-1, keepdims=True))
        dh_ref[...] = dhc
        sums_ref[0:1, :] += jnp.sum(dln * xhat, axis=0, keepdims=True)
        sums_ref[1:2, :] += jnp.sum(dln, axis=0, keepdims=True)
        sums_ref[2:3, :] += jnp.sum(dhc, axis=0, keepdims=True)

    tile = pl.BlockSpec((tt, cw), lambda i: (i, 0))
    vec = pl.BlockSpec((1, cw), lambda i: (0, 0))
    return pl.pallas_call(
        body, name="ln_bwd", grid=(t // tt,),
        out_shape=[jax.ShapeDtypeStruct((t, cw), F32), jax.ShapeDtypeStruct((8, cw), F32)],
        in_specs=[tile, tile, vec, vec],
        out_specs=[tile, pl.BlockSpec((8, cw), lambda i: (0, 0))],
        compiler_params=_params(("arbitrary",)),
    )(da, hc, ln_g, ln_b)


def _conv_bwd(dhc, proj, w_dw, dproj, kc, cw):
    t = proj.shape[0]
    tt = _pick(t, 128, HALO)
    per = tt // HALO
    nt = t // tt
    chunk = _pick(cw, 256)
    ncol = proj.shape[1] // cw
    wr = w_dw.shape[0]

    def body(d_ref, dn_ref, u_ref, g_ref, up_ref, gp_ref, w_ref, dp_in, dp_ref, dw_ref, dbuf, hbuf):
        i = pl.program_id(0)

        @pl.when(i == 0)
        def _():
            dw_ref[...] = jnp.zeros_like(dw_ref)

        dbuf[pl.ds(0, tt), :] = d_ref[...]
        dbuf[pl.ds(tt, HALO), :] = jnp.where(i < nt - 1, dn_ref[...], 0.0)
        hbuf[pl.ds(HALO, tt), :] = _glu_rows(u_ref, g_ref)
        hbuf[pl.ds(0, HALO), :] = jnp.where(i > 0, _glu_rows(up_ref, gp_ref), 0.0)
        for lo in range(0, cw, chunk):
            dhg = _conv_taps(dbuf, w_ref, kc, lo, tt, chunk, lambda j: (kc - 1) - j)
            u = u_ref[:, lo:lo + chunk]
            sg = _sigmoid(g_ref[:, lo:lo + chunk])
            dp_ref[:, lo:lo + chunk] = (dhg * sg).astype(BF16)
            dp_ref[:, cw + lo:cw + lo + chunk] = (dhg * u * sg * (1.0 - sg)).astype(BF16)
            dtile = d_ref[:, lo:lo + chunk]
            for j in range(kc):
                prod = dtile * hbuf[pl.ds(HALO - (kc - 1) + j, tt), pl.ds(lo, chunk)]
                dw_ref[j:j + 1, lo:lo + chunk] += jnp.sum(prod, axis=0, keepdims=True)

    tile = lambda g: pl.BlockSpec((tt, cw), lambda i: (i, g))
    prev = lambda g: pl.BlockSpec((HALO, cw), lambda i: (jnp.maximum(i * per - 1, 0), g))
    return pl.pallas_call(
        body, name="conv_bwd", grid=(nt,),
        out_shape=[jax.ShapeDtypeStruct(dproj.shape, dproj.dtype), jax.ShapeDtypeStruct((wr, cw), F32)],
        in_specs=[pl.BlockSpec((tt, cw), lambda i: (i, 0)),
                  pl.BlockSpec((HALO, cw), lambda i: (jnp.minimum((i + 1) * per, nt * per - 1), 0)),
                  tile(ncol - 3), tile(ncol - 2), prev(ncol - 3), prev(ncol - 2),
                  pl.BlockSpec((wr, cw), lambda i: (0, 0)), pl.BlockSpec(memory_space=pl.ANY)],
        out_specs=[pl.BlockSpec((tt, 2 * cw), lambda i: (i, (ncol - 3) // 2)),
                   pl.BlockSpec((wr, cw), lambda i: (0, 0))],
        scratch_shapes=[pltpu.VMEM((tt + HALO, cw), F32), pltpu.VMEM((HALO + tt, cw), F32)],
        input_output_aliases={7: 0},
        compiler_params=_params(("arbitrary",)),
    )(dhc, dhc, proj, proj, proj, proj, w_dw, dproj)


def _input_grad(dh, x, dout, norm_g, mod):
    t, d = x.shape
    tt = _pick(t, 128, 16)

    def body(dh_ref, x_ref, do_ref, g_ref, mod_ref, gx_ref, sums_ref):
        i = pl.program_id(0)

        @pl.when(i == 0)
        def _():
            sums_ref[...] = jnp.zeros_like(sums_ref)

        xv = x_ref[...]
        dhv = dh_ref[...]
        r = lax.rsqrt(jnp.mean(xv * xv, axis=-1, keepdims=True) + EPS)
        xn = xv * r
        g = g_ref[...]
        one_scale = 1.0 + mod_ref[1:2, :]
        dxn = dhv * g * one_scale
        gx_ref[...] = do_ref[...] + r * (dxn - xn * jnp.mean(dxn * xn, axis=-1, keepdims=True))
        sums_ref[0:1, :] += jnp.sum(dhv, axis=0, keepdims=True)
        sums_ref[1:2, :] += jnp.sum(dhv * (xn * g), axis=0, keepdims=True)
        sums_ref[2:3, :] += jnp.sum(dhv * one_scale * xn, axis=0, keepdims=True)

    tile = pl.BlockSpec((tt, d), lambda i: (i, 0))
    return pl.pallas_call(
        body, name="input_grad", grid=(t // tt,),
        out_shape=[jax.ShapeDtypeStruct((t, d), F32), jax.ShapeDtypeStruct((8, d), F32)],
        in_specs=[tile, tile, tile, pl.BlockSpec((1, d), lambda i: (0, 0)), pl.BlockSpec((3, d), lambda i: (0, 0))],
        out_specs=[tile, pl.BlockSpec((8, d), lambda i: (0, 0))],
        compiler_params=_params(("arbitrary",)),
    )(dh, x, dout, norm_g, mod)


def _sum_adam(parts, w, m, v, name):
    r, c = w.shape
    tr = _pick(r, 128, 16) if r % 16 == 0 else r
    tc = _pick(c, 2048)

    def body(p_ref, w_ref, m_ref, v_ref, g_ref, d_ref, nm_ref, nv_ref):
        g = p_ref[0].astype(F32)
        for i in range(1, NDEV):
            g = g + p_ref[i].astype(F32)
        d, nm, nv = _adam(w_ref[...], g, m_ref[...], v_ref[...])
        g_ref[...] = g
        d_ref[...] = d
        nm_ref[...] = nm
        nv_ref[...] = nv

    tile = pl.BlockSpec((tr, tc), lambda i, j: (i, j))
    out = jax.ShapeDtypeStruct((r, c), F32)
    return pl.pallas_call(
        body, name=name, grid=(r // tr, c // tc),
        out_shape=[out] * 4,
        in_specs=[pl.BlockSpec((NDEV, tr, tc), lambda i, j: (0, i, j)), tile, tile, tile],
        out_specs=[tile] * 4,
        compiler_params=_params(("arbitrary", "arbitrary")),
    )(parts, w, m, v)


def _ada_grad_adam(s_t, dm, w, m, v):
    d, n = w.shape
    tr = _pick(d, 256, 16)

    def body(s_ref, dm_ref, w_ref, m_ref, v_ref, g_ref, d_ref, nm_ref, nv_ref):
        g = lax.dot_general(s_ref[...], dm_ref[...], (NN, ((), ())), preferred_element_type=F32,
                            precision=lax.Precision.HIGHEST)
        dl, nm, nv = _adam(w_ref[...], g, m_ref[...], v_ref[...])
        g_ref[...] = g
        d_ref[...] = dl
        nm_ref[...] = nm
        nv_ref[...] = nv

    tile = pl.BlockSpec((tr, n), lambda i: (i, 0))
    out = jax.ShapeDtypeStruct((d, n), F32)
    return pl.pallas_call(
        body, name="ada_grad_adam", grid=(d // tr,),
        out_shape=[out] * 4,
        in_specs=[pl.BlockSpec((tr, NDEV), lambda i: (i, 0)), pl.BlockSpec((NDEV, n), lambda i: (0, 0)),
                  tile, tile, tile],
        out_specs=[tile] * 4,
        compiler_params=_params(("arbitrary",)),
    )(s_t, dm, w, m, v)


def _silu_t(c_all):
    n, d = c_all.shape

    def body(c_ref, o_ref):
        o_ref[...] = jnp.transpose(_silu(c_ref[...]))

    return pl.pallas_call(
        body, name="silu_t", out_shape=jax.ShapeDtypeStruct((d, n), F32),
        in_specs=[pl.BlockSpec(memory_space=pltpu.VMEM)], out_specs=pl.BlockSpec(memory_space=pltpu.VMEM),
        compiler_params=pltpu.CompilerParams(vmem_limit_bytes=VMEM_LIMIT),
    )(c_all)


def _rows128(v):
    return v.reshape(-1, LANES)


def _pad_rows(a, rows):
    return jnp.pad(a, ((0, rows - a.shape[0]), (0, 0)))


def kernel(x, c, norm_g, w_ada, b_ada, w_in, q_norm_g, k_norm_g, w_dw, b_dw, ln_g, ln_b, w_pw, b_pw, w_out, loss_target, m_norm_g, m_w_ada, m_b_ada, m_w_in, m_q_norm_g, m_k_norm_g, m_w_dw, m_b_dw, m_ln_g, m_ln_b, m_w_pw, m_b_pw, m_w_out, v_norm_g, v_w_ada, v_b_ada, v_w_in, v_q_norm_g, v_k_norm_g, v_w_dw, v_b_dw, v_ln_g, v_ln_b, v_w_pw, v_b_pw, v_w_out):
    _, t, d = x.shape
    n_ada = w_ada.shape[2]
    ns = w_in.shape[2]
    kc, cwl = w_dw.shape[1], w_dw.shape[2]
    cw = cwl * NDEV
    sb = d - cw
    nh = sb // HEAD_DIM
    assert sb == cw and kc - 1 <= HALO and NDEV * ns == 4 * sb + 3 * cw
    my = 4 * lax.axis_index("x") + 2 * lax.axis_index("y") + lax.axis_index("c")

    x2, tg2 = x[0], loss_target[0]

    wdw_rows = -(-kc // 8) * 8
    wdw_pad = _pad_rows(w_dw[0], wdw_rows)
    pay1 = jnp.concatenate([_rows128(c[0]), _rows128(wdw_pad.reshape(-1))], axis=0)
    (g1,) = _all_gather([pay1], "gather_cond", pltpu.VMEM)
    c_rows = d // LANES
    c_all = g1[:, :c_rows].reshape(NDEV, d)
    wdw_all = g1[:, c_rows:].reshape(NDEV, wdw_rows, cwl).transpose(1, 0, 2).reshape(wdw_rows, cw)

    b_ada_loc = lax.dynamic_slice(b_ada, (0, my * n_ada), (1, n_ada))
    mod_cols = _ada_matmul(c_all, w_ada[0], b_ada_loc)
    (g2,) = _all_gather([mod_cols], "gather_mod", pltpu.VMEM)
    mod_mine = lax.dynamic_index_in_dim(g2, my, axis=1, keepdims=False)
    mod = mod_mine.reshape(3, d)

    wg_in, wg_out, wg_pw = _all_gather(
        [w_in[0].astype(BF16), w_out[0].astype(BF16), w_pw[0].astype(BF16)], "gather_weights", pl.ANY)

    h = _modulated_norm(x2, norm_g, mod)
    proj = _proj_matmul(h, wg_in)
    o, tot, ycat = _attention_fwd(proj, q_norm_g, k_norm_g, nh, d)
    a, hc = _conv_fwd(proj, wdw_all, b_dw, ln_g, ln_b, kc, cw)
    z, ycat = _pointwise_fwd(a, wg_pw, b_pw, proj, ycat, cw)
    dout, dy, out_sums = _out_matmul(ycat, wg_out, x2, tg2, mod)

    dycat = _matmul_nt_slabs(dy, wg_out, "dycat_matmul")
    p_wout = _weight_grad_row_slabs(ycat, dy, "w_out_grad")
    dproj, gains = _attention_bwd(proj, o, tot, dycat, q_norm_g, k_norm_g, nh)
    dz, dproj, dz_sums = _gate_bwd(dycat, z, proj, dproj, cw)
    da = _matmul_nt_slabs(dz, wg_pw, "da_matmul")
    p_wpw = _weight_grad_row_slabs(a, dz, "w_pw_grad")
    dhc, ln_sums = _ln_bwd(da, hc, ln_g, ln_b)
    dproj, dwdw = _conv_bwd(dhc, proj, wdw_all, dproj, kc, cw)
    p_win = _weight_grad_col_slabs(h, dproj, ns)
    dh = _dh_matmul(dproj, wg_in)
    grad_x, in_sums = _input_grad(dh, x2, dout, norm_g, mod)

    p_wdw = dwdw.reshape(wdw_rows, NDEV, cwl).transpose(1, 0, 2).astype(BF16)
    r_win, r_wout, r_wpw, r_wdw = _all_to_all([p_win, p_wout, p_wpw, p_wdw], "exchange_grads")

    dmod = jnp.concatenate([in_sums[0], in_sums[1], out_sums[0]])
    loss_part = 0.5 / d * jnp.sum(out_sums[1].reshape(-1, LANES), axis=0)
    small = [in_sums[2], dmod, jnp.sum(gains[:, 0], axis=0), jnp.sum(gains[:, 1], axis=0),
             ln_sums[2], ln_sums[0], ln_sums[1], dz_sums[0], loss_part]
    sizes = [s.shape[0] for s in small]
    packed = _rows128(jnp.concatenate(small))
    n_rows = -(-packed.shape[0] // 8) * 8
    (g3,) = _all_gather([_pad_rows(packed, n_rows)], "gather_small", pltpu.VMEM)

    def pack_state(names_vals):
        flat = jnp.concatenate([v.reshape(-1) for v in names_vals] + [jnp.zeros((LANES,), F32)])
        return _pad_rows(_rows128(flat), n_rows)

    small_w = pack_state([norm_g, b_ada, q_norm_g, k_norm_g, b_dw, ln_g, ln_b, b_pw])
    small_m = pack_state([m_norm_g, m_b_ada, m_q_norm_g, m_k_norm_g, m_b_dw, m_ln_g, m_ln_b, m_b_pw])
    small_v = pack_state([v_norm_g, v_b_ada, v_q_norm_g, v_k_norm_g, v_b_dw, v_ln_g, v_ln_b, v_b_pw])
    sg, sd, sm, sv = _sum_adam(g3, small_w, small_m, small_v, "small_adam")

    def unpack(p):
        flat = p.reshape(-1)
        outs, off = [], 0
        for n in sizes[:-1]:
            outs.append(flat[off:off + n].reshape(1, n))
            off += n
        return outs, flat[off:off + LANES]

    g_small, loss_lanes = unpack(sg)
    d_small, _ = unpack(sd)
    m_small, _ = unpack(sm)
    v_small, _ = unpack(sv)
    loss = jnp.sum(loss_lanes)

    off = sizes[0]
    dmod_all = g3.reshape(NDEV, -1)[:, off:off + 3 * d]
    dmod_loc = lax.dynamic_slice(dmod_all, (0, my * n_ada), (NDEV, n_ada))
    ada = _ada_grad_adam(_silu_t(c_all), dmod_loc, w_ada[0], m_w_ada[0], v_w_ada[0])
    win = _sum_adam(r_win.reshape(NDEV, d, ns), w_in[0], m_w_in[0], v_w_in[0], "w_in_adam")
    wout = _sum_adam(r_wout, w_out[0], m_w_out[0], v_w_out[0], "w_out_adam")
    wpw = _sum_adam(r_wpw, w_pw[0], m_w_pw[0], v_w_pw[0], "w_pw_adam")
    wdw_state = [_pad_rows(s[0], wdw_rows) for s in (w_dw, m_w_dw, v_w_dw)]
    wdw = [r[:kc] for r in _sum_adam(r_wdw, *wdw_state, "w_dw_adam")]

    def group(k, small_list):
        s = small_list
        return [s[0], ada[k][None], s[1], win[k][None], s[2], s[3], wdw[k][None], s[4], s[5], s[6],
                wpw[k][None], s[7], wout[k][None]]

    return (loss, grad_x[None], *group(0, g_small), *group(1, d_small), *group(2, m_small), *group(3, v_small))
```

```python
import functools

import jax
import jax.numpy as jnp
from jax import lax
from jax.experimental import pallas as pl
from jax.experimental.pallas import tpu as pltpu

F32 = jnp.float32
BF16 = jnp.bfloat16
NDEV = 8
HEAD_DIM = 128
LANES = 128
HALO = 32
EPS = 1e-6
DEAD_LOG_WEIGHT = -104.0
VMEM_LIMIT = 56 * 1024 * 1024
MESH = pl.DeviceIdType.MESH

ADAM_LR = 0.001
ADAM_B1 = 0.9
ADAM_B2 = 0.999
ADAM_EPS = 1e-08
ADAM_WD = 0.01
ADAM_STEP = 10


def _params(sem=None):
    return pltpu.CompilerParams(dimension_semantics=sem, vmem_limit_bytes=VMEM_LIMIT)


def _pick(n, pref, unit=LANES):
    best = None
    for d in range(unit, min(n, pref) + 1, unit):
        if n % d == 0:
            best = d
    return best if best is not None else n


def _sigmoid(z):
    return 1.0 / (1.0 + jnp.exp(-z))


def _silu(z):
    return z * _sigmoid(z)


def _dsilu(z):
    s = _sigmoid(z)
    return s * (1.0 + z * (1.0 - s))


def _softplus(z):
    return jnp.maximum(z, 0.0) + jnp.log(1.0 + jnp.exp(-jnp.abs(z)))


def _dot(a, b, dims):
    return lax.dot_general(a, b, (dims, ((), ())), preferred_element_type=F32)


NN = ((1,), (0,))
NT = ((1,), (1,))
TN = ((0,), (0,))


def _adam(w, g, m, v):
    m = ADAM_B1 * m + (1.0 - ADAM_B1) * g
    v = ADAM_B2 * v + (1.0 - ADAM_B2) * (g * g)
    m_hat = m / (1.0 - ADAM_B1 ** ADAM_STEP)
    v_hat = v / (1.0 - ADAM_B2 ** ADAM_STEP)
    delta = -ADAM_LR * (m_hat / (jnp.sqrt(v_hat) + ADAM_EPS) + ADAM_WD * w)
    return delta, m, v


def _place():
    x, y, c = lax.axis_index("x"), lax.axis_index("y"), lax.axis_index("c")
    return x, y, c


def _flip(v, bit):
    return 1 - v if bit else v


def _all_gather(arrs, name, space):
    n = len(arrs)

    def body(*refs):
        ins, outs = refs[:n], refs[n:2 * n]
        send_sems, recv_sems, local_sems = refs[2 * n:]
        x, y, c = _place()
        me, sibling = (x, y, c), (x, y, 1 - c)
        chips = [(1 - x, y), (x, 1 - y), (1 - x, 1 - y)]

        def rows(a, p):
            return outs[a].at[4 * p[0] + 2 * p[1] + p[2]]

        def copy(a, k, block, to, src=None):
            return pltpu.make_async_remote_copy(
                src_ref=rows(a, block) if src is None else src, dst_ref=rows(a, block),
                send_sem=send_sems.at[7 * a + k], recv_sem=recv_sems.at[7 * a + k],
                device_id=to, device_id_type=MESH)

        mine = [pltpu.make_async_copy(ins[a], rows(a, me), local_sems.at[a]) for a in range(n)]
        for cp in mine:
            cp.start()
        first = []
        for a in range(n):
            first.append(copy(a, 0, me, sibling, src=ins[a]))
            first += [copy(a, 1 + j, me, (*chip, c), src=ins[a]) for j, chip in enumerate(chips)]
        for cp in first:
            cp.start()
        passed = []
        for j, chip in enumerate(chips):
            for a in range(n):
                copy(a, 1 + j, (*chip, c), me).wait_recv()
                fwd = copy(a, 4 + j, (*chip, c), sibling)
                fwd.start()
                passed.append(fwd)
        for a in range(n):
            copy(a, 0, sibling, me).wait_recv()
            for j, chip in enumerate(chips):
                copy(a, 4 + j, (*chip, 1 - c), me).wait_recv()
        for cp in first + passed:
            cp.wait_send()
        for cp in mine:
            cp.wait()

    spec = pl.BlockSpec(memory_space=space)
    return pl.pallas_call(
        body, name=name,
        out_shape=[jax.ShapeDtypeStruct((NDEV,) + a.shape, a.dtype) for a in arrs],
        in_specs=[spec] * n, out_specs=[spec] * n,
        scratch_shapes=[pltpu.SemaphoreType.DMA((7 * n,)), pltpu.SemaphoreType.DMA((7 * n,)),
                        pltpu.SemaphoreType.DMA((n,))],
        compiler_params=pltpu.CompilerParams(vmem_limit_bytes=VMEM_LIMIT),
    )(*arrs)


def _all_to_all(arrs, name):
    n = len(arrs)

    def body(*refs):
        ins, outs = refs[:n], refs[n:2 * n]
        send_sems, recv_sems, local_sems = refs[2 * n:]
        x, y, c = _place()
        my = 4 * x + 2 * y + c
        mine = [pltpu.make_async_copy(ins[a].at[my], outs[a].at[my], local_sems.at[a]) for a in range(n)]
        for cp in mine:
            cp.start()
        sent = []
        for k in range(1, NDEV):
            peer = (_flip(x, k & 4), _flip(y, k & 2), _flip(c, k & 1))
            peer_idx = 4 * peer[0] + 2 * peer[1] + peer[2]
            for a in range(n):
                cp = pltpu.make_async_remote_copy(
                    src_ref=ins[a].at[peer_idx], dst_ref=outs[a].at[my],
                    send_sem=send_sems.at[7 * a + k - 1], recv_sem=recv_sems.at[7 * a + k - 1],
                    device_id=peer, device_id_type=MESH)
                cp.start()
                sent.append(cp)
        for k in range(1, NDEV):
            peer = (_flip(x, k & 4), _flip(y, k & 2), _flip(c, k & 1))
            peer_idx = 4 * peer[0] + 2 * peer[1] + peer[2]
            for a in range(n):
                pltpu.make_async_remote_copy(
                    src_ref=ins[a].at[my], dst_ref=outs[a].at[peer_idx],
                    send_sem=send_sems.at[7 * a + k - 1], recv_sem=recv_sems.at[7 * a + k - 1],
                    device_id=peer, device_id_type=MESH).wait_recv()
        for cp in sent:
            cp.wait_send()
        for cp in mine:
            cp.wait()

    spec = pl.BlockSpec(memory_space=pl.ANY)
    return pl.pallas_call(
        body, name=name,
        out_shape=[jax.ShapeDtypeStruct(a.shape, a.dtype) for a in arrs],
        in_specs=[spec] * n, out_specs=[spec] * n,
        scratch_shapes=[pltpu.SemaphoreType.DMA((7 * n,)), pltpu.SemaphoreType.DMA((7 * n,)),
                        pltpu.SemaphoreType.DMA((n,))],
        compiler_params=pltpu.CompilerParams(vmem_limit_bytes=VMEM_LIMIT),
    )(*arrs)


def _ada_matmul(c_all, w_loc, b_loc):
    d, n = w_loc.shape
    bn = _pick(n, 512)

    def body(c_ref, w_ref, b_ref, o_ref):
        s = _silu(c_ref[...]).astype(BF16)
        o_ref[...] = _dot(s, w_ref[...].astype(BF16), NN) + b_ref[...]

    return pl.pallas_call(
        body, name="ada_matmul", grid=(n // bn,),
        out_shape=jax.ShapeDtypeStruct((NDEV, n), F32),
        in_specs=[pl.BlockSpec((NDEV, d), lambda j: (0, 0)), pl.BlockSpec((d, bn), lambda j: (0, j)),
                  pl.BlockSpec((1, bn), lambda j: (0, j))],
        out_specs=pl.BlockSpec((NDEV, bn), lambda j: (0, j)),
        compiler_params=_params(("arbitrary",)),
    )(c_all, w_loc, b_loc)


def _modulated_norm(x, norm_g, mod):
    t, d = x.shape
    tt = _pick(t, 256, 16)

    def body(x_ref, g_ref, mod_ref, h_ref):
        xv = x_ref[...]
        r = lax.rsqrt(jnp.mean(xv * xv, axis=-1, keepdims=True) + EPS)
        h = (xv * r) * g_ref[...] * (1.0 + mod_ref[1:2, :]) + mod_ref[0:1, :]
        h_ref[...] = h.astype(BF16)

    return pl.pallas_call(
        body, name="modulated_norm", grid=(t // tt,),
        out_shape=jax.ShapeDtypeStruct((t, d), BF16),
        in_specs=[pl.BlockSpec((tt, d), lambda i: (i, 0)), pl.BlockSpec((1, d), lambda i: (0, 0)),
                  pl.BlockSpec((3, d), lambda i: (0, 0))],
        out_specs=pl.BlockSpec((tt, d), lambda i: (i, 0)),
        compiler_params=_params(("arbitrary",)),
    )(x, norm_g, mod)


def _proj_matmul(h, wg):
    t, d = h.shape
    ns = wg.shape[2]
    tm, tk = _pick(t, 512, 16), _pick(d, 1024)
    nk = d // tk

    def body(a_ref, b_ref, o_ref, acc):
        k = pl.program_id(2)

        @pl.when(k == 0)
        def _():
            acc[...] = jnp.zeros_like(acc)

        acc[...] += _dot(a_ref[...], b_ref[...], NN)

        @pl.when(k == nk - 1)
        def _():
            o_ref[...] = acc[...]

    return pl.pallas_call(
        body, name="proj_matmul", grid=(t // tm, NDEV, nk),
        out_shape=jax.ShapeDtypeStruct((t, NDEV * ns), F32),
        in_specs=[pl.BlockSpec((tm, tk), lambda m, j, k: (m, k)),
                  pl.BlockSpec((None, tk, ns), lambda m, j, k: (j, k, 0))],
        out_specs=pl.BlockSpec((tm, ns), lambda m, j, k: (m, j)),
        scratch_shapes=[pltpu.VMEM((tm, ns), F32)],
        compiler_params=_params(("arbitrary", "arbitrary", "arbitrary")),
    )(h, wg)


def _attention_fwd(proj, qg, kg, nh, d_model):
    t = proj.shape[0]
    tq = _pick(t, 256, 16)
    nq = t // tq
    assert nq <= LANES
    scale = HEAD_DIM ** -0.5

    def body(q_ref, k_ref, v_ref, g_ref, qg_ref, kg_ref, o_ref, tot_ref, y_ref, first_ref, qn, kn, vb):
        def norm(src, gain, dst):
            v = src[...]
            r = lax.rsqrt(jnp.mean(v * v, axis=-1, keepdims=True) + EPS)
            dst[...] = ((v * r) * gain[...]).astype(BF16)

        norm(q_ref, qg_ref, qn)
        norm(k_ref, kg_ref, kn)
        vb[...] = v_ref[...].astype(BF16)
        row = lax.broadcasted_iota(jnp.int32, (tq, tq), 0)
        col = lax.broadcasted_iota(jnp.int32, (tq, tq), 1)
        upper = (row > col).astype(BF16)
        causal = col < row

        def block(qi, j, carry, acc, masked):
            ks = pl.ds(pl.multiple_of(j * tq, tq), tq)
            z = _dot(qi, kn[ks, :], NT) * scale
            sp = _softplus(z)
            ls = -sp
            if masked:
                ls = jnp.where(causal, ls, 0.0)
            hi = ls.astype(BF16)
            lo = (ls - hi.astype(F32)).astype(BF16)
            after = _dot(hi, upper, NN) + _dot(lo, upper, NN)
            w = jnp.exp(z - sp + after + carry)
            if masked:
                w = jnp.where(causal, w, 0.0)
            acc = acc + _dot(w.astype(BF16), vb[ks, :], NN)
            carry = carry + jnp.sum(ls, axis=1, keepdims=True)
            return carry, acc

        lane = lax.broadcasted_iota(jnp.int32, (8, LANES), 1)

        def live(carry):
            return (jnp.max(carry) > DEAD_LOG_WEIGHT).astype(jnp.int32)

        def q_block(i, firsts):
            qs = pl.ds(pl.multiple_of(i * tq, tq), tq)
            qi = qn[qs, :]
            carry, acc = block(qi, i, jnp.zeros((tq, 1), F32), jnp.zeros((tq, HEAD_DIM), F32), True)

            def k_step(st):
                ca, ac = block(qi, i - 1 - st[0], st[1], st[2], False)
                return st[0] + 1, ca, ac, live(ca)

            done, carry, acc, _ = lax.while_loop(
                lambda st: (st[0] < i) & (st[3] > 0), k_step, (jnp.int32(0), carry, acc, live(carry)))
            o_ref[qs, :] = acc
            tot_ref[qs, :] = jnp.broadcast_to(carry, (tq, HEAD_DIM))
            y_ref[qs, :] = (acc * _silu(g_ref[qs, :])).astype(BF16)
            return jnp.where(lane == i, (i - done).astype(F32), firsts)

        first_ref[...] = lax.fori_loop(0, nq, q_block, jnp.zeros((8, LANES), F32))

    col_block = lambda off: pl.BlockSpec((t, HEAD_DIM), lambda h: (0, off + h))
    vec = pl.BlockSpec((1, HEAD_DIM), lambda h: (0, 0))
    return pl.pallas_call(
        body, name="attention_fwd", grid=(nh,),
        out_shape=[jax.ShapeDtypeStruct((t, nh * HEAD_DIM), F32), jax.ShapeDtypeStruct((t, nh * HEAD_DIM), F32),
                   jax.ShapeDtypeStruct((t, d_model), BF16), jax.ShapeDtypeStruct((nh, 8, LANES), F32)],
        in_specs=[col_block(0), col_block(nh), col_block(2 * nh), col_block(3 * nh), vec, vec],
        out_specs=[col_block(0), col_block(0), col_block(0), pl.BlockSpec((None, 8, LANES), lambda h: (h, 0, 0))],
        scratch_shapes=[pltpu.VMEM((t, HEAD_DIM), BF16)] * 3,
        compiler_params=_params(("arbitrary",)),
    )(proj, proj, proj, proj, qg, kg)


def _conv_taps(buf, w_ref, kc, lo, n_rows, cw, first_tap_row):
    acc = None
    for j in range(kc):
        term = w_ref[j:j + 1, lo:lo + cw] * buf[pl.ds(first_tap_row(j), n_rows), pl.ds(lo, cw)]
        acc = term if acc is None else acc + term
    return acc


def _glu_rows(u_ref, g_ref):
    return u_ref[...] * _sigmoid(g_ref[...])


def _conv_fwd(proj, w_dw, b_dw, ln_g, ln_b, kc, cw):
    t = proj.shape[0]
    tt = _pick(t, 128, HALO)
    per = tt // HALO
    chunk = _pick(cw, 256)

    def body(u_ref, g_ref, up_ref, gp_ref, w_ref, b_ref, lg_ref, lb_ref, a_ref, hc_ref, buf):
        i = pl.program_id(0)
        buf[pl.ds(HALO, tt), :] = _glu_rows(u_ref, g_ref)
        halo = _glu_rows(up_ref, gp_ref)
        buf[pl.ds(0, HALO), :] = jnp.where(i > 0, halo, 0.0)
        for lo in range(0, cw, chunk):
            conv = _conv_taps(buf, w_ref, kc, lo, tt, chunk, lambda j: HALO - (kc - 1) + j)
            hc_ref[:, lo:lo + chunk] = conv + b_ref[:, lo:lo + chunk]
        hc = hc_ref[...]
        mu = jnp.mean(hc, axis=-1, keepdims=True)
        xc = hc - mu
        var = jnp.mean(xc * xc, axis=-1, keepdims=True)
        ln = xc * lax.rsqrt(var + EPS) * lg_ref[...] + lb_ref[...]
        a_ref[...] = _silu(ln).astype(BF16)

    ncol = proj.shape[1] // cw
    tile = lambda g: pl.BlockSpec((tt, cw), lambda i: (i, g))
    prev = lambda g: pl.BlockSpec((HALO, cw), lambda i: (jnp.maximum(i * per - 1, 0), g))
    full = lambda r: pl.BlockSpec((r, cw), lambda i: (0, 0))
    return pl.pallas_call(
        body, name="conv_fwd", grid=(t // tt,),
        out_shape=[jax.ShapeDtypeStruct((t, cw), BF16), jax.ShapeDtypeStruct((t, cw), F32)],
        in_specs=[tile(ncol - 3), tile(ncol - 2), prev(ncol - 3), prev(ncol - 2),
                  full(w_dw.shape[0]), full(1), full(1), full(1)],
        out_specs=[pl.BlockSpec((tt, cw), lambda i: (i, 0))] * 2,
        scratch_shapes=[pltpu.VMEM((HALO + tt, cw), F32)],
        compiler_params=_params(("arbitrary",)),
    )(proj, proj, proj, proj, w_dw, b_dw, ln_g, ln_b)


def _pointwise_fwd(a, wpw, b_pw, proj, ycat, cw):
    t = a.shape[0]
    rows = wpw.shape[1]
    tm = _pick(t, 512, 16)
    ncol = proj.shape[1] // cw
    ycol = ycat.shape[1] // cw - 1

    def body(a_ref, w_ref, b_ref, g_ref, y_in, z_ref, y_ref, acc):
        k = pl.program_id(1)

        @pl.when(k == 0)
        def _():
            acc[...] = jnp.zeros_like(acc)

        acc[...] += _dot(a_ref[...], w_ref[...], NN)

        @pl.when(k == NDEV - 1)
        def _():
            z = acc[...] + b_ref[...]
            z_ref[...] = z
            y_ref[...] = (z * _silu(g_ref[...])).astype(BF16)

    return pl.pallas_call(
        body, name="pointwise_fwd", grid=(t // tm, NDEV),
        out_shape=[jax.ShapeDtypeStruct((t, cw), F32), jax.ShapeDtypeStruct(ycat.shape, ycat.dtype)],
        in_specs=[pl.BlockSpec((tm, rows), lambda m, k: (m, k)),
                  pl.BlockSpec((None, rows, cw), lambda m, k: (k, 0, 0)),
                  pl.BlockSpec((1, cw), lambda m, k: (0, 0)),
                  pl.BlockSpec((tm, cw), lambda m, k: (m, ncol - 1)),
                  pl.BlockSpec(memory_space=pl.ANY)],
        out_specs=[pl.BlockSpec((tm, cw), lambda m, k: (m, 0)), pl.BlockSpec((tm, cw), lambda m, k: (m, ycol))],
        scratch_shapes=[pltpu.VMEM((tm, cw), F32)],
        input_output_aliases={4: 1},
        compiler_params=_params(("arbitrary", "arbitrary")),
    )(a, wpw, b_pw, proj, ycat)


def _out_matmul(ycat, wout, x, target, mod):
    t, d = x.shape
    rows = wout.shape[1]
    tm, tn = _pick(t, 512, 16), _pick(d, 1024)
    inv_d = 1.0 / d

    def body(a_ref, w_ref, x_ref, tg_ref, mod_ref, dout_ref, dy_ref, sums_ref, acc):
        m, k = pl.program_id(1), pl.program_id(2)

        @pl.when(k == 0)
        def _():
            acc[...] = jnp.zeros_like(acc)

        acc[...] += _dot(a_ref[...], w_ref[...], NN)

        @pl.when((k == NDEV - 1) & (m == 0))
        def _():
            sums_ref[...] = jnp.zeros_like(sums_ref)

        @pl.when(k == NDEV - 1)
        def _():
            y = acc[...]
            gate = mod_ref[2:3, :]
            err = (x_ref[...] + gate * y) - tg_ref[...]
            dout = err * inv_d
            dout_ref[...] = dout
            dy_ref[...] = (dout * gate).astype(BF16)
            sums_ref[0:1, :] += jnp.sum(dout * y, axis=0, keepdims=True)
            sums_ref[1:2, :] += jnp.sum(err * err, axis=0, keepdims=True)

    mn = lambda n, m, k: (m, n)
    return pl.pallas_call(
        body, name="out_matmul", grid=(d // tn, t // tm, NDEV),
        out_shape=[jax.ShapeDtypeStruct((t, d), F32), jax.ShapeDtypeStruct((t, d), BF16),
                   jax.ShapeDtypeStruct((8, d), F32)],
        in_specs=[pl.BlockSpec((tm, rows), lambda n, m, k: (m, k)),
                  pl.BlockSpec((None, rows, tn), lambda n, m, k: (k, 0, n)),
                  pl.BlockSpec((tm, tn), mn), pl.BlockSpec((tm, tn), mn),
                  pl.BlockSpec((3, tn), lambda n, m, k: (0, n))],
        out_specs=[pl.BlockSpec((tm, tn), mn), pl.BlockSpec((tm, tn), mn),
                   pl.BlockSpec((8, tn), lambda n, m, k: (0, n))],
        scratch_shapes=[pltpu.VMEM((tm, tn), F32)],
        compiler_params=_params(("arbitrary", "arbitrary", "arbitrary")),
    )(ycat, wout, x, target, mod)


def _matmul_nt_slabs(a, wg, name, out_dtype=F32):
    t, kdim = a.shape
    r = wg.shape[1]
    tm, tk = _pick(t, 512, 16), _pick(kdim, 2048)
    nk = kdim // tk

    def body(a_ref, b_ref, o_ref, acc):
        k = pl.program_id(2)

        @pl.when(k == 0)
        def _():
            acc[...] = jnp.zeros_like(acc)

        acc[...] += _dot(a_ref[...], b_ref[...], NT)

        @pl.when(k == nk - 1)
        def _():
            o_ref[...] = acc[...].astype(out_dtype)

    return pl.pallas_call(
        body, name=name, grid=(t // tm, NDEV, nk),
        out_shape=jax.ShapeDtypeStruct((t, NDEV * r), out_dtype),
        in_specs=[pl.BlockSpec((tm, tk), lambda m, j, k: (m, k)),
                  pl.BlockSpec((None, r, tk), lambda m, j, k: (j, 0, k))],
        out_specs=pl.BlockSpec((tm, r), lambda m, j, k: (m, j)),
        scratch_shapes=[pltpu.VMEM((tm, r), F32)],
        compiler_params=_params(("arbitrary", "arbitrary", "arbitrary")),
    )(a, wg)


def _weight_grad_row_slabs(a, b, name):
    t = a.shape[0]
    r = a.shape[1] // NDEV
    n = b.shape[1]
    tn, tt = _pick(n, 2048), _pick(t, 512, 16)
    nt = t // tt

    def body(a_ref, b_ref, o_ref, acc):
        k = pl.program_id(2)

        @pl.when(k == 0)
        def _():
            acc[...] = jnp.zeros_like(acc)

        acc[...] += _dot(a_ref[...], b_ref[...], TN)

        @pl.when(k == nt - 1)
        def _():
            o_ref[...] = acc[...].astype(BF16)

    return pl.pallas_call(
        body, name=name, grid=(NDEV, n // tn, nt),
        out_shape=jax.ShapeDtypeStruct((NDEV, r, n), BF16),
        in_specs=[pl.BlockSpec((tt, r), lambda j, c, k: (k, j)), pl.BlockSpec((tt, tn), lambda j, c, k: (k, c))],
        out_specs=pl.BlockSpec((None, r, tn), lambda j, c, k: (j, 0, c)),
        scratch_shapes=[pltpu.VMEM((r, tn), F32)],
        compiler_params=_params(("arbitrary", "arbitrary", "arbitrary")),
    )(a, b)


def _weight_grad_col_slabs(h, dproj, ns):
    t, d = h.shape
    tm, tt = _pick(d, 512), _pick(t, 512, 16)
    nt = t // tt

    def body(a_ref, b_ref, o_ref, acc):
        k = pl.program_id(2)

        @pl.when(k == 0)
        def _():
            acc[...] = jnp.zeros_like(acc)

        acc[...] += _dot(a_ref[...], b_ref[...], TN)

        @pl.when(k == nt - 1)
        def _():
            o_ref[...] = acc[...].astype(BF16)

    return pl.pallas_call(
        body, name="w_in_grad", grid=(NDEV, d // tm, nt),
        out_shape=jax.ShapeDtypeStruct((NDEV, d, ns), BF16),
        in_specs=[pl.BlockSpec((tt, tm), lambda j, m, k: (k, m)), pl.BlockSpec((tt, ns), lambda j, m, k: (k, j))],
        out_specs=pl.BlockSpec((None, tm, ns), lambda j, m, k: (j, m, 0)),
        scratch_shapes=[pltpu.VMEM((tm, ns), F32)],
        compiler_params=_params(("arbitrary", "arbitrary", "arbitrary")),
    )(h, dproj)


def _dh_matmul(dproj, wg):
    t = dproj.shape[0]
    d, ns = wg.shape[1], wg.shape[2]
    tm, tn = _pick(t, 512, 16), _pick(d, 1024)

    def body(a_ref, b_ref, o_ref, acc):
        k = pl.program_id(2)

        @pl.when(k == 0)
        def _():
            acc[...] = jnp.zeros_like(acc)

        acc[...] += _dot(a_ref[...], b_ref[...], NT)

        @pl.when(k == NDEV - 1)
        def _():
            o_ref[...] = acc[...]

    return pl.pallas_call(
        body, name="dh_matmul", grid=(t // tm, d // tn, NDEV),
        out_shape=jax.ShapeDtypeStruct((t, d), F32),
        in_specs=[pl.BlockSpec((tm, ns), lambda m, n, k: (m, k)),
                  pl.BlockSpec((None, tn, ns), lambda m, n, k: (k, n, 0))],
        out_specs=pl.BlockSpec((tm, tn), lambda m, n, k: (m, n)),
        scratch_shapes=[pltpu.VMEM((tm, tn), F32)],
        compiler_params=_params(("arbitrary", "arbitrary", "arbitrary")),
    )(dproj, wg)


def _attention_bwd(proj, o, tot, firsts, dycat, qg, kg, nh):
    t, in_cols = proj.shape
    tq = _pick(t, 256, 16)
    nq = t // tq
    scale = HEAD_DIM ** -0.5

    def body(q_ref, k_ref, v_ref, g_ref, o_ref, tot_ref, first_ref, dy_ref, qg_ref, kg_ref, dproj_ref, gains_ref,
             qn, kn, vb, dob, dk_acc, dv_acc, dq_acc, outs, sems):
        h = pl.program_id(0)

        def norm(src, gain, dst):
            v = src[...]
            r = lax.rsqrt(jnp.mean(v * v, axis=-1, keepdims=True) + EPS)
            dst[...] = ((v * r) * gain[...]).astype(BF16)

        norm(q_ref, qg_ref, qn)
        norm(k_ref, kg_ref, kn)
        vb[...] = v_ref[...].astype(BF16)
        gs = g_ref[...]
        dyv = dy_ref[...]
        dob[...] = (dyv * _silu(gs)).astype(BF16)
        outs[3] = (dyv * o_ref[...] * _dsilu(gs)).astype(BF16)
        dk_acc[...] = jnp.zeros_like(dk_acc)
        dv_acc[...] = jnp.zeros_like(dv_acc)

        row = lax.broadcasted_iota(jnp.int32, (tq, tq), 0)
        col = lax.broadcasted_iota(jnp.int32, (tq, tq), 1)
        incl = (col <= row).astype(BF16)
        excl = (col < row).astype(BF16)
        causal = row < col
        lane = lax.broadcasted_iota(jnp.int32, (1, LANES), 1)

        def block(i, j, qi, doi, tot_row, p_left, g_left, dq, masked):
            ks = pl.ds(pl.multiple_of(j * tq, tq), tq)
            kj = kn[ks, :]
            z = _dot(kj, qi, NT) * scale
            sp = _softplus(z)
            ls = -sp
            if masked:
                ls = jnp.where(causal, ls, 0.0)
            hi = ls.astype(BF16)
            lo = (ls - hi.astype(F32)).astype(BF16)
            p_inc = _dot(incl, hi, NN) + _dot(incl, lo, NN) + p_left
            beta = jnp.exp(z - sp)
            w = beta * jnp.exp(tot_row - p_inc)
            if masked:
                w = jnp.where(causal, w, 0.0)
            dw = _dot(vb[ks, :], doi, NT)
            g = w * dw
            g_before = _dot(excl, g.astype(BF16), NN) + g_left
            dz = g * (1.0 - beta) - beta * g_before
            if masked:
                dz = jnp.where(causal, dz, 0.0)
            dzb = dz.astype(BF16)
            dv_acc[ks, :] += _dot(w.astype(BF16), doi, NN)
            dk_acc[ks, :] += _dot(dzb, qi, NN)
            dq = dq + _dot(dzb, kj, TN)
            p_left = p_left + jnp.sum(ls, axis=0, keepdims=True)
            g_left = g_left + jnp.sum(g, axis=0, keepdims=True)
            return p_left, g_left, dq

        def q_block(i, _):
            qs = pl.ds(pl.multiple_of(i * tq, tq), tq)
            qi = qn[qs, :]
            doi = dob[qs, :]
            tot_row = jnp.transpose(tot_ref[qs, :])[0:1, :]
            zero_row = jnp.zeros((1, tq), F32)

            def k_step(j, carry):
                return block(i, j, qi, doi, tot_row, carry[0], carry[1], carry[2], False)

            first = jnp.sum(jnp.where(lane == i, first_ref[0:1, :], 0.0)).astype(jnp.int32)
            first = jnp.clip(first, 0, i)
            carry = lax.fori_loop(first, i, k_step, (zero_row, zero_row, jnp.zeros((tq, HEAD_DIM), F32)))
            _, _, dq = block(i, i, qi, doi, tot_row, carry[0], carry[1], carry[2], True)
            dq_acc[qs, :] = dq * scale
            return 0

        lax.fori_loop(0, nq, q_block, 0)

        def norm_bwd(src, gain, dn, slot, gain_row):
            v = src[...]
            r = lax.rsqrt(jnp.mean(v * v, axis=-1, keepdims=True) + EPS)
            vhat = v * r
            gains_ref[gain_row:gain_row + 1, :] = jnp.sum(dn * vhat, axis=0, keepdims=True)
            dhat = dn * gain[...]
            outs[slot] = (r * (dhat - vhat * jnp.mean(dhat * vhat, axis=-1, keepdims=True))).astype(BF16)

        gains_ref[...] = jnp.zeros_like(gains_ref)
        norm_bwd(q_ref, qg_ref, dq_acc[...], 0, 0)
        norm_bwd(k_ref, kg_ref, dk_acc[...] * scale, 1, 1)
        outs[2] = dv_acc[...].astype(BF16)
        copies = [pltpu.make_async_copy(
            outs.at[s], dproj_ref.at[:, pl.ds(pl.multiple_of((s * nh + h) * HEAD_DIM, HEAD_DIM), HEAD_DIM)], sems.at[s])
            for s in range(4)]
        for cp in copies:
            cp.start()
        for cp in copies:
            cp.wait()

    col_block = lambda off: pl.BlockSpec((t, HEAD_DIM), lambda h: (0, off + h))
    vec = pl.BlockSpec((1, HEAD_DIM), lambda h: (0, 0))
    head_scr = lambda dt: pltpu.VMEM((t, HEAD_DIM), dt)
    return pl.pallas_call(
        body, name="attention_bwd", grid=(nh,),
        out_shape=[jax.ShapeDtypeStruct((t, in_cols), BF16), jax.ShapeDtypeStruct((nh, 8, HEAD_DIM), F32)],
        in_specs=[col_block(0), col_block(nh), col_block(2 * nh), col_block(3 * nh),
                  col_block(0), col_block(0), pl.BlockSpec((None, 8, LANES), lambda h: (h, 0, 0)), col_block(0), vec, vec],
        out_specs=[pl.BlockSpec(memory_space=pl.ANY), pl.BlockSpec((None, 8, HEAD_DIM), lambda h: (h, 0, 0))],
        scratch_shapes=[head_scr(BF16), head_scr(BF16), head_scr(BF16), head_scr(BF16),
                        head_scr(F32), head_scr(F32), head_scr(F32),
                        pltpu.VMEM((4, t, HEAD_DIM), BF16), pltpu.SemaphoreType.DMA((4,))],
        compiler_params=_params(("arbitrary",)),
    )(proj, proj, proj, proj, o, tot, firsts, dycat, qg, kg)


def _gate_bwd(dycat, z, proj, dproj, cw):
    t = z.shape[0]
    tt = _pick(t, 256, 16)
    ncol = proj.shape[1] // cw
    ycol = dycat.shape[1] // cw - 1

    def body(dy_ref, z_ref, g_ref, dp_in, dz_ref, dp_ref, sums_ref):
        i = pl.program_id(0)

        @pl.when(i == 0)
        def _():
            sums_ref[...] = jnp.zeros_like(sums_ref)

        g = g_ref[...]
        dy = dy_ref[...]
        dz = dy * _silu(g)
        dz_ref[...] = dz.astype(BF16)
        dp_ref[...] = (dy * z_ref[...] * _dsilu(g)).astype(BF16)
        sums_ref[0:1, :] += jnp.sum(dz, axis=0, keepdims=True)

    return pl.pallas_call(
        body, name="gate_bwd", grid=(t // tt,),
        out_shape=[jax.ShapeDtypeStruct((t, cw), BF16), jax.ShapeDtypeStruct(dproj.shape, dproj.dtype),
                   jax.ShapeDtypeStruct((8, cw), F32)],
        in_specs=[pl.BlockSpec((tt, cw), lambda i: (i, ycol)), pl.BlockSpec((tt, cw), lambda i: (i, 0)),
                  pl.BlockSpec((tt, cw), lambda i: (i, ncol - 1)), pl.BlockSpec(memory_space=pl.ANY)],
        out_specs=[pl.BlockSpec((tt, cw), lambda i: (i, 0)), pl.BlockSpec((tt, cw), lambda i: (i, ncol - 1)),
                   pl.BlockSpec((8, cw), lambda i: (0, 0))],
        input_output_aliases={3: 1},
        compiler_params=_params(("arbitrary",)),
    )(dycat, z, proj, dproj)


def _ln_bwd(da, hc, ln_g, ln_b):
    t, cw = hc.shape
    tt = _pick(t, 256, 16)

    def body(da_ref, hc_ref, lg_ref, lb_ref, dh_ref, sums_ref):
        i = pl.program_id(0)

        @pl.when(i == 0)
        def _():
            sums_ref[...] = jnp.zeros_like(sums_ref)

        hcv = hc_ref[...]
        mu = jnp.mean(hcv, axis=-1, keepdims=True)
        xc = hcv - mu
        r = lax.rsqrt(jnp.mean(xc * xc, axis=-1, keepdims=True) + EPS)
        xhat = xc * r
        ln = xhat * lg_ref[...] + lb_ref[...]
        dln = da_ref[...] * _dsilu(ln)
        dxhat = dln * lg_ref[...]
        dhc = r * (dxhat - jnp.mean(dxhat, axis=-1, keepdims=True)
                   - xhat * jnp.mean(dxhat * xhat, axis=-1, keepdims=True))
        dh_ref[...] = dhc
        sums_ref[0:1, :] += jnp.sum(dln * xhat, axis=0, keepdims=True)
        sums_ref[1:2, :] += jnp.sum(dln, axis=0, keepdims=True)
        sums_ref[2:3, :] += jnp.sum(dhc, axis=0, keepdims=True)

    tile = pl.BlockSpec((tt, cw), lambda i: (i, 0))
    vec = pl.BlockSpec((1, cw), lambda i: (0, 0))
    return pl.pallas_call(
        body, name="ln_bwd", grid=(t // tt,),
        out_shape=[jax.ShapeDtypeStruct((t, cw), F32), jax.ShapeDtypeStruct((8, cw), F32)],
        in_specs=[tile, tile, vec, vec],
        out_specs=[tile, pl.BlockSpec((8, cw), lambda i: (0, 0))],
        compiler_params=_params(("arbitrary",)),
    )(da, hc, ln_g, ln_b)


def _conv_bwd(dhc, proj, w_dw, dproj, kc, cw):
    t = proj.shape[0]
    tt = _pick(t, 128, HALO)
    per = tt // HALO
    nt = t // tt
    chunk = _pick(cw, 256)
    ncol = proj.shape[1] // cw
    wr = w_dw.shape[0]

    def body(d_ref, dn_ref, u_ref, g_ref, up_ref, gp_ref, w_ref, dp_in, dp_ref, dw_ref, dbuf, hbuf):
        i = pl.program_id(0)

        @pl.when(i == 0)
        def _():
            dw_ref[...] = jnp.zeros_like(dw_ref)

        dbuf[pl.ds(0, tt), :] = d_ref[...]
        dbuf[pl.ds(tt, HALO), :] = jnp.where(i < nt - 1, dn_ref[...], 0.0)
        hbuf[pl.ds(HALO, tt), :] = _glu_rows(u_ref, g_ref)
        hbuf[pl.ds(0, HALO), :] = jnp.where(i > 0, _glu_rows(up_ref, gp_ref), 0.0)
        for lo in range(0, cw, chunk):
            dhg = _conv_taps(dbuf, w_ref, kc, lo, tt, chunk, lambda j: (kc - 1) - j)
            u = u_ref[:, lo:lo + chunk]
            sg = _sigmoid(g_ref[:, lo:lo + chunk])
            dp_ref[:, lo:lo + chunk] = (dhg * sg).astype(BF16)
            dp_ref[:, cw + lo:cw + lo + chunk] = (dhg * u * sg * (1.0 - sg)).astype(BF16)
            dtile = d_ref[:, lo:lo + chunk]
            for j in range(kc):
                prod = dtile * hbuf[pl.ds(HALO - (kc - 1) + j, tt), pl.ds(lo, chunk)]
                dw_ref[j:j + 1, lo:lo + chunk] += jnp.sum(prod, axis=0, keepdims=True)

    tile = lambda g: pl.BlockSpec((tt, cw), lambda i: (i, g))
    prev = lambda g: pl.BlockSpec((HALO, cw), lambda i: (jnp.maximum(i * per - 1, 0), g))
    return pl.pallas_call(
        body, name="conv_bwd", grid=(nt,),
        out_shape=[jax.ShapeDtypeStruct(dproj.shape, dproj.dtype), jax.ShapeDtypeStruct((wr, cw), F32)],
        in_specs=[pl.BlockSpec((tt, cw), lambda i: (i, 0)),
                  pl.BlockSpec((HALO, cw), lambda i: (jnp.minimum((i + 1) * per, nt * per - 1), 0)),
                  tile(ncol - 3), tile(ncol - 2), prev(ncol - 3), prev(ncol - 2),
                  pl.BlockSpec((wr, cw), lambda i: (0, 0)), pl.BlockSpec(memory_space=pl.ANY)],
        out_specs=[pl.BlockSpec((tt, 2 * cw), lambda i: (i, (ncol - 3) // 2)),
                   pl.BlockSpec((wr, cw), lambda i: (0, 0))],
        scratch_shapes=[pltpu.VMEM((tt + HALO, cw), F32), pltpu.VMEM((HALO + tt, cw), F32)],
        input_output_aliases={7: 0},
        compiler_params=_params(("arbitrary",)),
    )(dhc, dhc, proj, proj, proj, proj, w_dw, dproj)


def _input_grad(dh, x, dout, norm_g, mod):
    t, d = x.shape
    tt = _pick(t, 128, 16)

    def body(dh_ref, x_ref, do_ref, g_ref, mod_ref, gx_ref, sums_ref):
        i = pl.program_id(0)

        @pl.when(i == 0)
        def _():
            sums_ref[...] = jnp.zeros_like(sums_ref)

        xv = x_ref[...]
        dhv = dh_ref[...]
        r = lax.rsqrt(jnp.mean(xv * xv, axis=-1, keepdims=True) + EPS)
        xn = xv * r
        g = g_ref[...]
        one_scale = 1.0 + mod_ref[1:2, :]
        dxn = dhv * g * one_scale
        gx_ref[...] = do_ref[...] + r * (dxn - xn * jnp.mean(dxn * xn, axis=-1, keepdims=True))
        sums_ref[0:1, :] += jnp.sum(dhv, axis=0, keepdims=True)
        sums_ref[1:2, :] += jnp.sum(dhv * (xn * g), axis=0, keepdims=True)
        sums_ref[2:3, :] += jnp.sum(dhv * one_scale * xn, axis=0, keepdims=True)

    tile = pl.BlockSpec((tt, d), lambda i: (i, 0))
    return pl.pallas_call(
        body, name="input_grad", grid=(t // tt,),
        out_shape=[jax.ShapeDtypeStruct((t, d), F32), jax.ShapeDtypeStruct((8, d), F32)],
        in_specs=[tile, tile, tile, pl.BlockSpec((1, d), lambda i: (0, 0)), pl.BlockSpec((3, d), lambda i: (0, 0))],
        out_specs=[tile, pl.BlockSpec((8, d), lambda i: (0, 0))],
        compiler_params=_params(("arbitrary",)),
    )(dh, x, dout, norm_g, mod)


def _sum_adam(parts, w, m, v, name):
    r, c = w.shape
    tr = _pick(r, 128, 16) if r % 16 == 0 else r
    tc = _pick(c, 2048)

    def body(p_ref, w_ref, m_ref, v_ref, g_ref, d_ref, nm_ref, nv_ref):
        g = p_ref[0].astype(F32)
        for i in range(1, NDEV):
            g = g + p_ref[i].astype(F32)
        d, nm, nv = _adam(w_ref[...], g, m_ref[...], v_ref[...])
        g_ref[...] = g
        d_ref[...] = d
        nm_ref[...] = nm
        nv_ref[...] = nv

    tile = pl.BlockSpec((tr, tc), lambda i, j: (i, j))
    out = jax.ShapeDtypeStruct((r, c), F32)
    return pl.pallas_call(
        body, name=name, grid=(r // tr, c // tc),
        out_shape=[out] * 4,
        in_specs=[pl.BlockSpec((NDEV, tr, tc), lambda i, j: (0, i, j)), tile, tile, tile],
        out_specs=[tile] * 4,
        compiler_params=_params(("arbitrary", "arbitrary")),
    )(parts, w, m, v)


def _ada_grad_adam(s_t, dm, w, m, v):
    d, n = w.shape
    tr = _pick(d, 256, 16)

    def body(s_ref, dm_ref, w_ref, m_ref, v_ref, g_ref, d_ref, nm_ref, nv_ref):
        g = lax.dot_general(s_ref[...], dm_ref[...], (NN, ((), ())), preferred_element_type=F32,
                            precision=lax.Precision.HIGHEST)
        dl, nm, nv = _adam(w_ref[...], g, m_ref[...], v_ref[...])
        g_ref[...] = g
        d_ref[...] = dl
        nm_ref[...] = nm
        nv_ref[...] = nv

    tile = pl.BlockSpec((tr, n), lambda i: (i, 0))
    out = jax.ShapeDtypeStruct((d, n), F32)
    return pl.pallas_call(
        body, name="ada_grad_adam", grid=(d // tr,),
        out_shape=[out] * 4,
        in_specs=[pl.BlockSpec((tr, NDEV), lambda i: (i, 0)), pl.BlockSpec((NDEV, n), lambda i: (0, 0)),
                  tile, tile, tile],
        out_specs=[tile] * 4,
        compiler_params=_params(("arbitrary",)),
    )(s_t, dm, w, m, v)


def _silu_t(c_all):
    n, d = c_all.shape

    def body(c_ref, o_ref):
        o_ref[...] = jnp.transpose(_silu(c_ref[...]))

    return pl.pallas_call(
        body, name="silu_t", out_shape=jax.ShapeDtypeStruct((d, n), F32),
        in_specs=[pl.BlockSpec(memory_space=pltpu.VMEM)], out_specs=pl.BlockSpec(memory_space=pltpu.VMEM),
        compiler_params=pltpu.CompilerParams(vmem_limit_bytes=VMEM_LIMIT),
    )(c_all)


def _rows128(v):
    return v.reshape(-1, LANES)


def _pad_rows(a, rows):
    return jnp.pad(a, ((0, rows - a.shape[0]), (0, 0)))


def kernel(x, c, norm_g, w_ada, b_ada, w_in, q_norm_g, k_norm_g, w_dw, b_dw, ln_g, ln_b, w_pw, b_pw, w_out, loss_target, m_norm_g, m_w_ada, m_b_ada, m_w_in, m_q_norm_g, m_k_norm_g, m_w_dw, m_b_dw, m_ln_g, m_ln_b, m_w_pw, m_b_pw, m_w_out, v_norm_g, v_w_ada, v_b_ada, v_w_in, v_q_norm_g, v_k_norm_g, v_w_dw, v_b_dw, v_ln_g, v_ln_b, v_w_pw, v_b_pw, v_w_out):
    _, t, d = x.shape
    n_ada = w_ada.shape[2]
    ns = w_in.shape[2]
    kc, cwl = w_dw.shape[1], w_dw.shape[2]
    cw = cwl * NDEV
    sb = d - cw
    nh = sb // HEAD_DIM
    assert sb == cw and kc - 1 <= HALO and NDEV * ns == 4 * sb + 3 * cw
    my = 4 * lax.axis_index("x") + 2 * lax.axis_index("y") + lax.axis_index("c")

    x2, tg2 = x[0], loss_target[0]

    wdw_rows = -(-kc // 8) * 8
    wdw_pad = _pad_rows(w_dw[0], wdw_rows)
    pay1 = jnp.concatenate([_rows128(c[0]), _rows128(wdw_pad.reshape(-1))], axis=0)
    (g1,) = _all_gather([pay1], "gather_cond", pltpu.VMEM)
    c_rows = d // LANES
    c_all = g1[:, :c_rows].reshape(NDEV, d)
    wdw_all = g1[:, c_rows:].reshape(NDEV, wdw_rows, cwl).transpose(1, 0, 2).reshape(wdw_rows, cw)

    b_ada_loc = lax.dynamic_slice(b_ada, (0, my * n_ada), (1, n_ada))
    mod_cols = _ada_matmul(c_all, w_ada[0], b_ada_loc)
    (g2,) = _all_gather([mod_cols], "gather_mod", pltpu.VMEM)
    mod_mine = lax.dynamic_index_in_dim(g2, my, axis=1, keepdims=False)
    mod = mod_mine.reshape(3, d)

    wg_in, wg_out, wg_pw = _all_gather(
        [w_in[0].astype(BF16), w_out[0].astype(BF16), w_pw[0].astype(BF16)], "gather_weights", pl.ANY)

    h = _modulated_norm(x2, norm_g, mod)
    proj = _proj_matmul(h, wg_in)
    o, tot, ycat, firsts = _attention_fwd(proj, q_norm_g, k_norm_g, nh, d)
    a, hc = _conv_fwd(proj, wdw_all, b_dw, ln_g, ln_b, kc, cw)
    z, ycat = _pointwise_fwd(a, wg_pw, b_pw, proj, ycat, cw)
    dout, dy, out_sums = _out_matmul(ycat, wg_out, x2, tg2, mod)

    dycat = _matmul_nt_slabs(dy, wg_out, "dycat_matmul")
    p_wout = _weight_grad_row_slabs(ycat, dy, "w_out_grad")
    dproj, gains = _attention_bwd(proj, o, tot, firsts, dycat, q_norm_g, k_norm_g, nh)
    dz, dproj, dz_sums = _gate_bwd(dycat, z, proj, dproj, cw)
    da = _matmul_nt_slabs(dz, wg_pw, "da_matmul")
    p_wpw = _weight_grad_row_slabs(a, dz, "w_pw_grad")
    dhc, ln_sums = _ln_bwd(da, hc, ln_g, ln_b)
    dproj, dwdw = _conv_bwd(dhc, proj, wdw_all, dproj, kc, cw)
    p_win = _weight_grad_col_slabs(h, dproj, ns)
    dh = _dh_matmul(dproj, wg_in)
    grad_x, in_sums = _input_grad(dh, x2, dout, norm_g, mod)

    p_wdw = dwdw.reshape(wdw_rows, NDEV, cwl).transpose(1, 0, 2).astype(BF16)
    r_win, r_wout, r_wpw, r_wdw = _all_to_all([p_win, p_wout, p_wpw, p_wdw], "exchange_grads")

    dmod = jnp.concatenate([in_sums[0], in_sums[1], out_sums[0]])
    loss_part = 0.5 / d * jnp.sum(out_sums[1].reshape(-1, LANES), axis=0)
    small = [in_sums[2], dmod, jnp.sum(gains[:, 0], axis=0), jnp.sum(gains[:, 1], axis=0),
             ln_sums[2], ln_sums[0], ln_sums[1], dz_sums[0], loss_part]
    sizes = [s.shape[0] for s in small]
    packed = _rows128(jnp.concatenate(small))
    n_rows = -(-packed.shape[0] // 8) * 8
    (g3,) = _all_gather([_pad_rows(packed, n_rows)], "gather_small", pltpu.VMEM)

    def pack_state(names_vals):
        flat = jnp.concatenate([v.reshape(-1) for v in names_vals] + [jnp.zeros((LANES,), F32)])
        return _pad_rows(_rows128(flat), n_rows)

    small_w = pack_state([norm_g, b_ada, q_norm_g, k_norm_g, b_dw, ln_g, ln_b, b_pw])
    small_m = pack_state([m_norm_g, m_b_ada, m_q_norm_g, m_k_norm_g, m_b_dw, m_ln_g, m_ln_b, m_b_pw])
    small_v = pack_state([v_norm_g, v_b_ada, v_q_norm_g, v_k_norm_g, v_b_dw, v_ln_g, v_ln_b, v_b_pw])
    sg, sd, sm, sv = _sum_adam(g3, small_w, small_m, small_v, "small_adam")

    def unpack(p):
        flat = p.reshape(-1)
        outs, off = [], 0
        for n in sizes[:-1]:
            outs.append(flat[off:off + n].reshape(1, n))
            off += n
        return outs, flat[off:off + LANES]

    g_small, loss_lanes = unpack(sg)
    d_small, _ = unpack(sd)
    m_small, _ = unpack(sm)
    v_small, _ = unpack(sv)
    loss = jnp.sum(loss_lanes)

    off = sizes[0]
    dmod_all = g3.reshape(NDEV, -1)[:, off:off + 3 * d]
    dmod_loc = lax.dynamic_slice(dmod_all, (0, my * n_ada), (NDEV, n_ada))
    ada = _ada_grad_adam(_silu_t(c_all), dmod_loc, w_ada[0], m_w_ada[0], v_w_ada[0])
    win = _sum_adam(r_win.reshape(NDEV, d, ns), w_in[0], m_w_in[0], v_w_in[0], "w_in_adam")
    wout = _sum_adam(r_wout, w_out[0], m_w_out[0], v_w_out[0], "w_out_adam")
    wpw = _sum_adam(r_wpw, w_pw[0], m_w_pw[0], v_w_pw[0], "w_pw_adam")
    wdw_state = [_pad_rows(s[0], wdw_rows) for s in (w_dw, m_w_dw, v_w_dw)]
    wdw = [r[:kc] for r in _sum_adam(r_wdw, *wdw_state, "w_dw_adam")]

    def group(k, small_list):
        s = small_list
        return [s[0], ada[k][None], s[1], win[k][None], s[2], s[3], wdw[k][None], s[4], s[5], s[6],
                wpw[k][None], s[7], wout[k][None]]

    return (loss, grad_x[None], *group(0, g_small), *group(1, d_small), *group(2, m_small), *group(3, v_small))
```

```python
import functools

import jax
import jax.numpy as jnp
from jax import lax
from jax.experimental import pallas as pl
from jax.experimental.pallas import tpu as pltpu

F32 = jnp.float32
BF16 = jnp.bfloat16
NDEV = 8
HEAD_DIM = 128
LANES = 128
HALO = 32
EPS = 1e-6
DEAD_LOG_WEIGHT = -104.0
VMEM_LIMIT = 56 * 1024 * 1024
MESH = pl.DeviceIdType.MESH

ADAM_LR = 0.001
ADAM_B1 = 0.9
ADAM_B2 = 0.999
ADAM_EPS = 1e-08
ADAM_WD = 0.01
ADAM_STEP = 10


def _params(sem=None):
    return pltpu.CompilerParams(dimension_semantics=sem, vmem_limit_bytes=VMEM_LIMIT)


def _pick(n, pref, unit=LANES):
    best = None
    for d in range(unit, min(n, pref) + 1, unit):
        if n % d == 0:
            best = d
    return best if best is not None else n


def _sigmoid(z):
    return 1.0 / (1.0 + jnp.exp(-z))


def _silu(z):
    return z * _sigmoid(z)


def _dsilu(z):
    s = _sigmoid(z)
    return s * (1.0 + z * (1.0 - s))


def _softplus(z):
    return jnp.maximum(z, 0.0) + jnp.log(1.0 + jnp.exp(-jnp.abs(z)))


def _dot(a, b, dims):
    return lax.dot_general(a, b, (dims, ((), ())), preferred_element_type=F32)


NN = ((1,), (0,))
NT = ((1,), (1,))
TN = ((0,), (0,))


def _adam(w, g, m, v):
    m = ADAM_B1 * m + (1.0 - ADAM_B1) * g
    v = ADAM_B2 * v + (1.0 - ADAM_B2) * (g * g)
    m_hat = m / (1.0 - ADAM_B1 ** ADAM_STEP)
    v_hat = v / (1.0 - ADAM_B2 ** ADAM_STEP)
    delta = -ADAM_LR * (m_hat / (jnp.sqrt(v_hat) + ADAM_EPS) + ADAM_WD * w)
    return delta, m, v


def _place():
    x, y, c = lax.axis_index("x"), lax.axis_index("y"), lax.axis_index("c")
    return x, y, c


def _flip(v, bit):
    return 1 - v if bit else v


def _all_gather(arrs, name, space):
    n = len(arrs)

    def body(*refs):
        ins, outs = refs[:n], refs[n:2 * n]
        send_sems, recv_sems, local_sems = refs[2 * n:]
        x, y, c = _place()
        me, sibling = (x, y, c), (x, y, 1 - c)
        chips = [(1 - x, y), (x, 1 - y), (1 - x, 1 - y)]

        def rows(a, p):
            return outs[a].at[4 * p[0] + 2 * p[1] + p[2]]

        def copy(a, k, block, to, src=None):
            return pltpu.make_async_remote_copy(
                src_ref=rows(a, block) if src is None else src, dst_ref=rows(a, block),
                send_sem=send_sems.at[7 * a + k], recv_sem=recv_sems.at[7 * a + k],
                device_id=to, device_id_type=MESH)

        mine = [pltpu.make_async_copy(ins[a], rows(a, me), local_sems.at[a]) for a in range(n)]
        for cp in mine:
            cp.start()
        first = []
        for a in range(n):
            first.append(copy(a, 0, me, sibling, src=ins[a]))
            first += [copy(a, 1 + j, me, (*chip, c), src=ins[a]) for j, chip in enumerate(chips)]
        for cp in first:
            cp.start()
        passed = []
        for j, chip in enumerate(chips):
            for a in range(n):
                copy(a, 1 + j, (*chip, c), me).wait_recv()
                fwd = copy(a, 4 + j, (*chip, c), sibling)
                fwd.start()
                passed.append(fwd)
        for a in range(n):
            copy(a, 0, sibling, me).wait_recv()
            for j, chip in enumerate(chips):
                copy(a, 4 + j, (*chip, 1 - c), me).wait_recv()
        for cp in first + passed:
            cp.wait_send()
        for cp in mine:
            cp.wait()

    spec = pl.BlockSpec(memory_space=space)
    return pl.pallas_call(
        body, name=name,
        out_shape=[jax.ShapeDtypeStruct((NDEV,) + a.shape, a.dtype) for a in arrs],
        in_specs=[spec] * n, out_specs=[spec] * n,
        scratch_shapes=[pltpu.SemaphoreType.DMA((7 * n,)), pltpu.SemaphoreType.DMA((7 * n,)),
                        pltpu.SemaphoreType.DMA((n,))],
        compiler_params=pltpu.CompilerParams(vmem_limit_bytes=VMEM_LIMIT),
    )(*arrs)


class _Ride:
    def __init__(self, ins, out_shapes, n_sems, start, finish):
        self.ins, self.out_shapes, self.n_sems, self.start, self.finish = ins, out_shapes, n_sems, start, finish


def _call(body, *, name, grid, out_shape, in_specs, out_specs, args, scratch_shapes=(), aliases=None, ride=None):
    sem = ("arbitrary",) * len(grid)
    if ride is None:
        return pl.pallas_call(
            body, name=name, grid=grid, out_shape=out_shape, in_specs=in_specs, out_specs=out_specs,
            scratch_shapes=list(scratch_shapes), input_output_aliases=aliases or {}, compiler_params=_params(sem))(*args)
    n_in, n_out, n_scr = len(in_specs), len(out_specs), len(scratch_shapes)
    r_in, r_out = len(ride.ins), len(ride.out_shapes)

    def carried(*refs):
        ins, rins = refs[:n_in], refs[n_in:n_in + r_in]
        o0 = n_in + r_in
        outs, routs = refs[o0:o0 + n_out], refs[o0 + n_out:o0 + n_out + r_out]
        scratch = refs[o0 + n_out + r_out:o0 + n_out + r_out + n_scr]
        sems = refs[o0 + n_out + r_out + n_scr:]
        first = functools.reduce(lambda a, b: a & b, [pl.program_id(i) == 0 for i in range(len(grid))])
        last = functools.reduce(lambda a, b: a & b, [pl.program_id(i) == grid[i] - 1 for i in range(len(grid))])

        @pl.when(first)
        def _():
            ride.start(rins, routs, *sems)

        body(*ins, *outs, *scratch)

        @pl.when(last)
        def _():
            ride.finish(rins, routs, *sems)

    hbm = pl.BlockSpec(memory_space=pl.ANY)
    res = pl.pallas_call(
        carried, name=name, grid=grid,
        out_shape=list(out_shape) + list(ride.out_shapes),
        in_specs=list(in_specs) + [hbm] * r_in, out_specs=list(out_specs) + [hbm] * r_out,
        scratch_shapes=list(scratch_shapes) + [pltpu.SemaphoreType.DMA((ride.n_sems,))] * 3,
        input_output_aliases=aliases or {}, compiler_params=_params(sem))(*args, *ride.ins)
    return res[:n_out], res[n_out:]


def _chips(x, y):
    return [(1 - x, y), (x, 1 - y), (1 - x, 1 - y)]


def _gather_ride(arrs):
    n = len(arrs)

    def copies(ins, outs, send_sems, recv_sems):
        x, y, c = _place()
        me = 4 * x + 2 * y + c
        peers = [(x, y, 1 - c)] + [(*chip, c) for chip in _chips(x, y)]
        return [pltpu.make_async_remote_copy(
            src_ref=ins[a], dst_ref=outs[a].at[me], send_sem=send_sems.at[4 * a + k], recv_sem=recv_sems.at[4 * a + k],
            device_id=p, device_id_type=MESH) for a in range(n) for k, p in enumerate(peers)], me

    def start(ins, outs, send_sems, recv_sems, local_sems):
        cps, me = copies(ins, outs, send_sems, recv_sems)
        for a in range(n):
            pltpu.make_async_copy(ins[a], outs[a].at[me], local_sems.at[a]).start()
        for cp in cps:
            cp.start()

    def finish(ins, outs, send_sems, recv_sems, local_sems):
        cps, me = copies(ins, outs, send_sems, recv_sems)
        for cp in cps:
            cp.wait_recv()
        for cp in cps:
            cp.wait_send()
        for a in range(n):
            pltpu.make_async_copy(ins[a], outs[a].at[me], local_sems.at[a]).wait()

    return _Ride(arrs, [jax.ShapeDtypeStruct((NDEV,) + a.shape, a.dtype) for a in arrs], 4 * n, start, finish)


def _gather_finish(arrs):
    n = len(arrs)

    def body(*refs):
        outs = refs[n:2 * n]
        send_sems, recv_sems = refs[2 * n:]
        x, y, c = _place()
        cps = []
        for a in range(n):
            for k, chip in enumerate(_chips(x, y)):
                blk = 4 * chip[0] + 2 * chip[1]
                cps.append((pltpu.make_async_remote_copy(
                    src_ref=outs[a].at[blk + c], dst_ref=outs[a].at[blk + c],
                    send_sem=send_sems.at[3 * a + k], recv_sem=recv_sems.at[3 * a + k],
                    device_id=(x, y, 1 - c), device_id_type=MESH),
                    pltpu.make_async_remote_copy(
                    src_ref=outs[a].at[blk + 1 - c], dst_ref=outs[a].at[blk + 1 - c],
                    send_sem=send_sems.at[3 * a + k], recv_sem=recv_sems.at[3 * a + k],
                    device_id=(x, y, 1 - c), device_id_type=MESH)))
        for send, _ in cps:
            send.start()
        for _, recv in cps:
            recv.wait_recv()
        for send, _ in cps:
            send.wait_send()

    spec = pl.BlockSpec(memory_space=pl.ANY)
    return pl.pallas_call(
        body, name="gather_finish",
        out_shape=[jax.ShapeDtypeStruct(a.shape, a.dtype) for a in arrs],
        in_specs=[spec] * n, out_specs=[spec] * n,
        scratch_shapes=[pltpu.SemaphoreType.DMA((3 * n,)), pltpu.SemaphoreType.DMA((3 * n,))],
        input_output_aliases={a: a for a in range(n)},
        compiler_params=pltpu.CompilerParams(vmem_limit_bytes=VMEM_LIMIT),
    )(*arrs)


def _sibling_exchange(arrs, name):
    n = len(arrs)

    def body(*refs):
        ins, outs = refs[:n], refs[n:2 * n]
        send_sems, recv_sems = refs[2 * n:]
        x, y, c = _place()
        cps = [pltpu.make_async_remote_copy(
            src_ref=ins[a].at[:, 1 - c], dst_ref=outs[a], send_sem=send_sems.at[a], recv_sem=recv_sems.at[a],
            device_id=(x, y, 1 - c), device_id_type=MESH) for a in range(n)]
        for cp in cps:
            cp.start()
        for cp in cps:
            cp.wait()

    spec = pl.BlockSpec(memory_space=pl.ANY)
    return pl.pallas_call(
        body, name=name,
        out_shape=[jax.ShapeDtypeStruct((4,) + a.shape[2:], a.dtype) for a in arrs],
        in_specs=[spec] * n, out_specs=[spec] * n,
        scratch_shapes=[pltpu.SemaphoreType.DMA((n,)), pltpu.SemaphoreType.DMA((n,))],
        compiler_params=pltpu.CompilerParams(vmem_limit_bytes=VMEM_LIMIT),
    )(*arrs)


def _pair_sum(mine, theirs, core, name):
    _, _, r, c = mine.shape
    tr = _pick(r, 512, 16)

    def body(core_ref, a_ref, b_ref, o_ref):
        o_ref[...] = (a_ref[...].astype(F32) + b_ref[...].astype(F32)).astype(BF16)

    return pl.pallas_call(
        body, name=name,
        grid_spec=pltpu.PrefetchScalarGridSpec(
            num_scalar_prefetch=1, grid=(4, r // tr),
            in_specs=[pl.BlockSpec((None, None, tr, c), lambda i, k, core_ref: (i, core_ref[0], k, 0)),
                      pl.BlockSpec((None, tr, c), lambda i, k, core_ref: (i, k, 0))],
            out_specs=pl.BlockSpec((None, tr, c), lambda i, k, core_ref: (i, k, 0))),
        out_shape=jax.ShapeDtypeStruct((4, r, c), BF16),
        compiler_params=_params(("arbitrary", "arbitrary")),
    )(core, mine, theirs)


def _chip_exchange_ride(arrs):
    n = len(arrs)

    def copies(ins, outs, send_sems, recv_sems):
        x, y, c = _place()
        mine = 2 * x + y
        return [pltpu.make_async_remote_copy(
            src_ref=ins[a].at[2 * chip[0] + chip[1]], dst_ref=outs[a].at[mine],
            send_sem=send_sems.at[3 * a + k], recv_sem=recv_sems.at[3 * a + k],
            device_id=(*chip, c), device_id_type=MESH) for a in range(n) for k, chip in enumerate(_chips(x, y))], mine

    def start(ins, outs, send_sems, recv_sems, local_sems):
        cps, mine = copies(ins, outs, send_sems, recv_sems)
        for a in range(n):
            pltpu.make_async_copy(ins[a].at[mine], outs[a].at[mine], local_sems.at[a]).start()
        for cp in cps:
            cp.start()

    def finish(ins, outs, send_sems, recv_sems, local_sems):
        cps, mine = copies(ins, outs, send_sems, recv_sems)
        for cp in cps:
            cp.wait_recv()
        for cp in cps:
            cp.wait_send()
        for a in range(n):
            pltpu.make_async_copy(ins[a].at[mine], outs[a].at[mine], local_sems.at[a]).wait()

    return _Ride(arrs, [jax.ShapeDtypeStruct(a.shape, a.dtype) for a in arrs], 3 * n, start, finish)


def _ada_matmul(c_all, w_loc, b_loc):
    d, n = w_loc.shape
    bn = _pick(n, 512)

    def body(c_ref, w_ref, b_ref, o_ref):
        s = _silu(c_ref[...]).astype(BF16)
        o_ref[...] = _dot(s, w_ref[...].astype(BF16), NN) + b_ref[...]

    return pl.pallas_call(
        body, name="ada_matmul", grid=(n // bn,),
        out_shape=jax.ShapeDtypeStruct((NDEV, n), F32),
        in_specs=[pl.BlockSpec((NDEV, d), lambda j: (0, 0)), pl.BlockSpec((d, bn), lambda j: (0, j)),
                  pl.BlockSpec((1, bn), lambda j: (0, j))],
        out_specs=pl.BlockSpec((NDEV, bn), lambda j: (0, j)),
        compiler_params=_params(("arbitrary",)),
    )(c_all, w_loc, b_loc)


def _modulated_norm(x, norm_g, mod):
    t, d = x.shape
    tt = _pick(t, 256, 16)

    def body(x_ref, g_ref, mod_ref, h_ref):
        xv = x_ref[...]
        r = lax.rsqrt(jnp.mean(xv * xv, axis=-1, keepdims=True) + EPS)
        h = (xv * r) * g_ref[...] * (1.0 + mod_ref[1:2, :]) + mod_ref[0:1, :]
        h_ref[...] = h.astype(BF16)

    return pl.pallas_call(
        body, name="modulated_norm", grid=(t // tt,),
        out_shape=jax.ShapeDtypeStruct((t, d), BF16),
        in_specs=[pl.BlockSpec((tt, d), lambda i: (i, 0)), pl.BlockSpec((1, d), lambda i: (0, 0)),
                  pl.BlockSpec((3, d), lambda i: (0, 0))],
        out_specs=pl.BlockSpec((tt, d), lambda i: (i, 0)),
        compiler_params=_params(("arbitrary",)),
    )(x, norm_g, mod)


def _gather_proj(h, w_loc):
    t, d = h.shape
    ns = w_loc.shape[1]
    tm = _pick(t, 512, 16)
    nm = t // tm
    x, y, c = _place()
    idx = lambda p: 4 * p[0] + 2 * p[1] + p[2]
    order = [(x, y, c), (x, y, 1 - c)]
    for chip in _chips(x, y):
        order += [(*chip, c), (*chip, 1 - c)]
    order = jnp.stack([idx(p) for p in order]).astype(jnp.int32)

    def body(order_ref, a_ref, w_ref, wg_ref, o_ref, slab, send_sems, recv_sems, local_sems):
        j, m = pl.program_id(0), pl.program_id(1)
        x, y, c = _place()
        me, sibling = (x, y, c), (x, y, 1 - c)
        chips = _chips(x, y)

        def rows(p):
            return wg_ref.at[idx(p)]

        def copy(k, block, to, src=None):
            return pltpu.make_async_remote_copy(
                src_ref=rows(block) if src is None else src, dst_ref=rows(block),
                send_sem=send_sems.at[k], recv_sem=recv_sems.at[k], device_id=to, device_id_type=MESH)

        def load(src):
            cp = pltpu.make_async_copy(src, slab, local_sems.at[1])
            cp.start()
            cp.wait()

        keep = pltpu.make_async_copy(w_ref, rows(me), local_sems.at[0])
        first = [copy(0, me, sibling, src=w_ref)] + [copy(1 + k, me, (*chip, c), src=w_ref)
                                                     for k, chip in enumerate(chips)]
        passed = [copy(4 + k, (*chip, c), sibling) for k, chip in enumerate(chips)]

        @pl.when((j == 0) & (m == 0))
        def _():
            keep.start()
            for cp in first:
                cp.start()
            load(w_ref)

        @pl.when((j == 1) & (m == 0))
        def _():
            copy(0, sibling, me).wait_recv()
            load(rows(sibling))

        for k, chip in enumerate(chips):
            @pl.when((j == 2 + 2 * k) & (m == 0))
            def _(k=k, chip=chip):
                copy(1 + k, (*chip, c), me).wait_recv()
                passed[k].start()
                load(rows((*chip, c)))

            @pl.when((j == 3 + 2 * k) & (m == 0))
            def _(k=k, chip=chip):
                copy(4 + k, (*chip, 1 - c), me).wait_recv()
                load(rows((*chip, 1 - c)))

        o_ref[...] = _dot(a_ref[...], slab[...], NN)

        @pl.when((j == NDEV - 1) & (m == nm - 1))
        def _():
            for cp in first + passed:
                cp.wait_send()
            keep.wait()

    hbm = pl.BlockSpec(memory_space=pl.ANY)
    return pl.pallas_call(
        body, name="gather_proj",
        grid_spec=pltpu.PrefetchScalarGridSpec(
            num_scalar_prefetch=1, grid=(NDEV, nm),
            in_specs=[pl.BlockSpec((tm, d), lambda j, m, order_ref: (m, 0)), hbm],
            out_specs=[hbm, pl.BlockSpec((tm, ns), lambda j, m, order_ref: (m, order_ref[j]))],
            scratch_shapes=[pltpu.VMEM((d, ns), BF16), pltpu.SemaphoreType.DMA((7,)), pltpu.SemaphoreType.DMA((7,)),
                            pltpu.SemaphoreType.DMA((2,))]),
        out_shape=[jax.ShapeDtypeStruct((NDEV, d, ns), BF16), jax.ShapeDtypeStruct((t, NDEV * ns), F32)],
        compiler_params=_params(("arbitrary", "arbitrary")),
    )(order, h, w_loc)


def _attention_fwd(proj, qg, kg, nh, d_model):
    t = proj.shape[0]
    tq = _pick(t, 256, 16)
    nq = t // tq
    assert nq <= LANES
    scale = HEAD_DIM ** -0.5

    def body(q_ref, k_ref, v_ref, g_ref, qg_ref, kg_ref, o_ref, tot_ref, y_ref, first_ref, qn, kn, vb):
        def norm(src, gain, dst):
            v = src[...]
            r = lax.rsqrt(jnp.mean(v * v, axis=-1, keepdims=True) + EPS)
            dst[...] = ((v * r) * gain[...]).astype(BF16)

        norm(q_ref, qg_ref, qn)
        norm(k_ref, kg_ref, kn)
        vb[...] = v_ref[...].astype(BF16)
        row = lax.broadcasted_iota(jnp.int32, (tq, tq), 0)
        col = lax.broadcasted_iota(jnp.int32, (tq, tq), 1)
        upper = (row > col).astype(BF16)
        causal = col < row

        def block(qi, j, carry, acc, masked):
            ks = pl.ds(pl.multiple_of(j * tq, tq), tq)
            z = _dot(qi, kn[ks, :], NT) * scale
            sp = _softplus(z)
            ls = -sp
            if masked:
                ls = jnp.where(causal, ls, 0.0)
            hi = ls.astype(BF16)
            lo = (ls - hi.astype(F32)).astype(BF16)
            after = _dot(hi, upper, NN) + _dot(lo, upper, NN)
            w = jnp.exp(z - sp + after + carry)
            if masked:
                w = jnp.where(causal, w, 0.0)
            acc = acc + _dot(w.astype(BF16), vb[ks, :], NN)
            carry = carry + jnp.sum(ls, axis=1, keepdims=True)
            return carry, acc

        lane = lax.broadcasted_iota(jnp.int32, (8, LANES), 1)

        def live(carry):
            return (jnp.max(carry) > DEAD_LOG_WEIGHT).astype(jnp.int32)

        def q_block(i, firsts):
            qs = pl.ds(pl.multiple_of(i * tq, tq), tq)
            qi = qn[qs, :]
            carry, acc = block(qi, i, jnp.zeros((tq, 1), F32), jnp.zeros((tq, HEAD_DIM), F32), True)

            def k_step(st):
                ca, ac = block(qi, i - 1 - st[0], st[1], st[2], False)
                return st[0] + 1, ca, ac, live(ca)

            done, carry, acc, _ = lax.while_loop(
                lambda st: (st[0] < i) & (st[3] > 0), k_step, (jnp.int32(0), carry, acc, live(carry)))
            o_ref[qs, :] = acc
            tot_ref[qs, :] = jnp.broadcast_to(carry, (tq, HEAD_DIM))
            y_ref[qs, :] = (acc * _silu(g_ref[qs, :])).astype(BF16)
            return jnp.where(lane == i, (i - done).astype(F32), firsts)

        first_ref[...] = lax.fori_loop(0, nq, q_block, jnp.zeros((8, LANES), F32))

    col_block = lambda off: pl.BlockSpec((t, HEAD_DIM), lambda h: (0, off + h))
    vec = pl.BlockSpec((1, HEAD_DIM), lambda h: (0, 0))
    return pl.pallas_call(
        body, name="attention_fwd", grid=(nh,),
        out_shape=[jax.ShapeDtypeStruct((t, nh * HEAD_DIM), F32), jax.ShapeDtypeStruct((t, nh * HEAD_DIM), F32),
                   jax.ShapeDtypeStruct((t, d_model), BF16), jax.ShapeDtypeStruct((nh, 8, LANES), F32)],
        in_specs=[col_block(0), col_block(nh), col_block(2 * nh), col_block(3 * nh), vec, vec],
        out_specs=[col_block(0), col_block(0), col_block(0), pl.BlockSpec((None, 8, LANES), lambda h: (h, 0, 0))],
        scratch_shapes=[pltpu.VMEM((t, HEAD_DIM), BF16)] * 3,
        compiler_params=_params(("arbitrary",)),
    )(proj, proj, proj, proj, qg, kg)


def _conv_taps(buf, w_ref, kc, lo, n_rows, cw, first_tap_row):
    acc = None
    for j in range(kc):
        term = w_ref[j:j + 1, lo:lo + cw] * buf[pl.ds(first_tap_row(j), n_rows), pl.ds(lo, cw)]
        acc = term if acc is None else acc + term
    return acc


def _glu_rows(u_ref, g_ref):
    return u_ref[...] * _sigmoid(g_ref[...])


def _conv_fwd(proj, w_dw, b_dw, ln_g, ln_b, kc, cw, ride):
    t = proj.shape[0]
    tt = _pick(t, 128, HALO)
    per = tt // HALO
    chunk = _pick(cw, 256)

    def body(u_ref, g_ref, up_ref, gp_ref, w_ref, b_ref, lg_ref, lb_ref, a_ref, hc_ref, buf):
        i = pl.program_id(0)
        buf[pl.ds(HALO, tt), :] = _glu_rows(u_ref, g_ref)
        halo = _glu_rows(up_ref, gp_ref)
        buf[pl.ds(0, HALO), :] = jnp.where(i > 0, halo, 0.0)
        for lo in range(0, cw, chunk):
            conv = _conv_taps(buf, w_ref, kc, lo, tt, chunk, lambda j: HALO - (kc - 1) + j)
            hc_ref[:, lo:lo + chunk] = conv + b_ref[:, lo:lo + chunk]
        hc = hc_ref[...]
        mu = jnp.mean(hc, axis=-1, keepdims=True)
        xc = hc - mu
        var = jnp.mean(xc * xc, axis=-1, keepdims=True)
        ln = xc * lax.rsqrt(var + EPS) * lg_ref[...] + lb_ref[...]
        a_ref[...] = _silu(ln).astype(BF16)

    ncol = proj.shape[1] // cw
    tile = lambda g: pl.BlockSpec((tt, cw), lambda i: (i, g))
    prev = lambda g: pl.BlockSpec((HALO, cw), lambda i: (jnp.maximum(i * per - 1, 0), g))
    full = lambda r: pl.BlockSpec((r, cw), lambda i: (0, 0))
    return _call(
        body, name="conv_fwd", grid=(t // tt,),
        out_shape=[jax.ShapeDtypeStruct((t, cw), BF16), jax.ShapeDtypeStruct((t, cw), F32)],
        in_specs=[tile(ncol - 3), tile(ncol - 2), prev(ncol - 3), prev(ncol - 2),
                  full(w_dw.shape[0]), full(1), full(1), full(1)],
        out_specs=[pl.BlockSpec((tt, cw), lambda i: (i, 0))] * 2,
        scratch_shapes=[pltpu.VMEM((HALO + tt, cw), F32)],
        args=(proj, proj, proj, proj, w_dw, b_dw, ln_g, ln_b), ride=ride)


def _pointwise_fwd(a, wpw, b_pw, proj, ycat, cw):
    t = a.shape[0]
    rows = wpw.shape[1]
    tm = _pick(t, 512, 16)
    ncol = proj.shape[1] // cw
    ycol = ycat.shape[1] // cw - 1

    def body(a_ref, w_ref, b_ref, g_ref, y_in, z_ref, y_ref, acc):
        k = pl.program_id(1)

        @pl.when(k == 0)
        def _():
            acc[...] = jnp.zeros_like(acc)

        acc[...] += _dot(a_ref[...], w_ref[...], NN)

        @pl.when(k == NDEV - 1)
        def _():
            z = acc[...] + b_ref[...]
            z_ref[...] = z
            y_ref[...] = (z * _silu(g_ref[...])).astype(BF16)

    return pl.pallas_call(
        body, name="pointwise_fwd", grid=(t // tm, NDEV),
        out_shape=[jax.ShapeDtypeStruct((t, cw), F32), jax.ShapeDtypeStruct(ycat.shape, ycat.dtype)],
        in_specs=[pl.BlockSpec((tm, rows), lambda m, k: (m, k)),
                  pl.BlockSpec((None, rows, cw), lambda m, k: (k, 0, 0)),
                  pl.BlockSpec((1, cw), lambda m, k: (0, 0)),
                  pl.BlockSpec((tm, cw), lambda m, k: (m, ncol - 1)),
                  pl.BlockSpec(memory_space=pl.ANY)],
        out_specs=[pl.BlockSpec((tm, cw), lambda m, k: (m, 0)), pl.BlockSpec((tm, cw), lambda m, k: (m, ycol))],
        scratch_shapes=[pltpu.VMEM((tm, cw), F32)],
        input_output_aliases={4: 1},
        compiler_params=_params(("arbitrary", "arbitrary")),
    )(a, wpw, b_pw, proj, ycat)


def _out_matmul(ycat, wout, x, target, mod):
    t, d = x.shape
    rows = wout.shape[1]
    tm, tn = _pick(t, 512, 16), _pick(d, 1024)
    inv_d = 1.0 / d

    def body(a_ref, w_ref, x_ref, tg_ref, mod_ref, dout_ref, dy_ref, sums_ref, acc):
        m, k = pl.program_id(1), pl.program_id(2)

        @pl.when(k == 0)
        def _():
            acc[...] = jnp.zeros_like(acc)

        acc[...] += _dot(a_ref[...], w_ref[...], NN)

        @pl.when((k == NDEV - 1) & (m == 0))
        def _():
            sums_ref[...] = jnp.zeros_like(sums_ref)

        @pl.when(k == NDEV - 1)
        def _():
            y = acc[...]
            gate = mod_ref[2:3, :]
            err = (x_ref[...] + gate * y) - tg_ref[...]
            dout = err * inv_d
            dout_ref[...] = dout
            dy_ref[...] = (dout * gate).astype(BF16)
            sums_ref[0:1, :] += jnp.sum(dout * y, axis=0, keepdims=True)
            sums_ref[1:2, :] += jnp.sum(err * err, axis=0, keepdims=True)

    mn = lambda n, m, k: (m, n)
    return pl.pallas_call(
        body, name="out_matmul", grid=(d // tn, t // tm, NDEV),
        out_shape=[jax.ShapeDtypeStruct((t, d), F32), jax.ShapeDtypeStruct((t, d), BF16),
                   jax.ShapeDtypeStruct((8, d), F32)],
        in_specs=[pl.BlockSpec((tm, rows), lambda n, m, k: (m, k)),
                  pl.BlockSpec((None, rows, tn), lambda n, m, k: (k, 0, n)),
                  pl.BlockSpec((tm, tn), mn), pl.BlockSpec((tm, tn), mn),
                  pl.BlockSpec((3, tn), lambda n, m, k: (0, n))],
        out_specs=[pl.BlockSpec((tm, tn), mn), pl.BlockSpec((tm, tn), mn),
                   pl.BlockSpec((8, tn), lambda n, m, k: (0, n))],
        scratch_shapes=[pltpu.VMEM((tm, tn), F32)],
        compiler_params=_params(("arbitrary", "arbitrary", "arbitrary")),
    )(ycat, wout, x, target, mod)


def _matmul_nt_slabs(a, wg, name, out_dtype=F32):
    t, kdim = a.shape
    r = wg.shape[1]
    tm, tk = _pick(t, 512, 16), _pick(kdim, 2048)
    nk = kdim // tk

    def body(a_ref, b_ref, o_ref, acc):
        k = pl.program_id(2)

        @pl.when(k == 0)
        def _():
            acc[...] = jnp.zeros_like(acc)

        acc[...] += _dot(a_ref[...], b_ref[...], NT)

        @pl.when(k == nk - 1)
        def _():
            o_ref[...] = acc[...].astype(out_dtype)

    return pl.pallas_call(
        body, name=name, grid=(t // tm, NDEV, nk),
        out_shape=jax.ShapeDtypeStruct((t, NDEV * r), out_dtype),
        in_specs=[pl.BlockSpec((tm, tk), lambda m, j, k: (m, k)),
                  pl.BlockSpec((None, r, tk), lambda m, j, k: (j, 0, k))],
        out_specs=pl.BlockSpec((tm, r), lambda m, j, k: (m, j)),
        scratch_shapes=[pltpu.VMEM((tm, r), F32)],
        compiler_params=_params(("arbitrary", "arbitrary", "arbitrary")),
    )(a, wg)


def _weight_grad_row_slabs(a, b, name):
    t = a.shape[0]
    r = a.shape[1] // NDEV
    n = b.shape[1]
    tn, tt = _pick(n, 2048), _pick(t, 512, 16)
    nt = t // tt

    def body(a_ref, b_ref, o_ref, acc):
        k = pl.program_id(2)

        @pl.when(k == 0)
        def _():
            acc[...] = jnp.zeros_like(acc)

        acc[...] += _dot(a_ref[...], b_ref[...], TN)

        @pl.when(k == nt - 1)
        def _():
            o_ref[...] = acc[...].astype(BF16)

    return pl.pallas_call(
        body, name=name, grid=(NDEV, n // tn, nt),
        out_shape=jax.ShapeDtypeStruct((NDEV, r, n), BF16),
        in_specs=[pl.BlockSpec((tt, r), lambda j, c, k: (k, j)), pl.BlockSpec((tt, tn), lambda j, c, k: (k, c))],
        out_specs=pl.BlockSpec((None, r, tn), lambda j, c, k: (j, 0, c)),
        scratch_shapes=[pltpu.VMEM((r, tn), F32)],
        compiler_params=_params(("arbitrary", "arbitrary", "arbitrary")),
    )(a, b)


def _weight_grad_col_slabs(h, dproj, ns, ride):
    t, d = h.shape
    tm, tt = _pick(d, 512), _pick(t, 512, 16)
    nt = t // tt

    def body(a_ref, b_ref, o_ref, acc):
        k = pl.program_id(2)

        @pl.when(k == 0)
        def _():
            acc[...] = jnp.zeros_like(acc)

        acc[...] += _dot(a_ref[...], b_ref[...], TN)

        @pl.when(k == nt - 1)
        def _():
            o_ref[...] = acc[...].astype(BF16)

    return _call(
        body, name="w_in_grad", grid=(NDEV, d // tm, nt),
        out_shape=[jax.ShapeDtypeStruct((NDEV, d, ns), BF16)],
        in_specs=[pl.BlockSpec((tt, tm), lambda j, m, k: (k, m)), pl.BlockSpec((tt, ns), lambda j, m, k: (k, j))],
        out_specs=[pl.BlockSpec((None, tm, ns), lambda j, m, k: (j, m, 0))],
        scratch_shapes=[pltpu.VMEM((tm, ns), F32)],
        args=(h, dproj), ride=ride)


def _dh_matmul(dproj, wg, ride):
    t = dproj.shape[0]
    d, ns = wg.shape[1], wg.shape[2]
    tm, tn = _pick(t, 512, 16), _pick(d, 1024)

    def body(a_ref, b_ref, o_ref, acc):
        k = pl.program_id(2)

        @pl.when(k == 0)
        def _():
            acc[...] = jnp.zeros_like(acc)

        acc[...] += _dot(a_ref[...], b_ref[...], NT)

        @pl.when(k == NDEV - 1)
        def _():
            o_ref[...] = acc[...]

    return _call(
        body, name="dh_matmul", grid=(t // tm, d // tn, NDEV),
        out_shape=[jax.ShapeDtypeStruct((t, d), F32)],
        in_specs=[pl.BlockSpec((tm, ns), lambda m, n, k: (m, k)),
                  pl.BlockSpec((None, tn, ns), lambda m, n, k: (k, n, 0))],
        out_specs=[pl.BlockSpec((tm, tn), lambda m, n, k: (m, n))],
        scratch_shapes=[pltpu.VMEM((tm, tn), F32)],
        args=(dproj, wg), ride=ride)


def _attention_bwd(proj, o, tot, firsts, dycat, qg, kg, nh):
    t, in_cols = proj.shape
    tq = _pick(t, 256, 16)
    nq = t // tq
    scale = HEAD_DIM ** -0.5

    def body(q_ref, k_ref, v_ref, g_ref, o_ref, tot_ref, first_ref, dy_ref, qg_ref, kg_ref, dproj_ref, gains_ref,
             qn, kn, vb, dob, dk_acc, dv_acc, dq_acc, outs, sems):
        h = pl.program_id(0)

        def norm(src, gain, dst):
            v = src[...]
            r = lax.rsqrt(jnp.mean(v * v, axis=-1, keepdims=True) + EPS)
            dst[...] = ((v * r) * gain[...]).astype(BF16)

        norm(q_ref, qg_ref, qn)
        norm(k_ref, kg_ref, kn)
        vb[...] = v_ref[...].astype(BF16)
        gs = g_ref[...]
        dyv = dy_ref[...]
        dob[...] = (dyv * _silu(gs)).astype(BF16)
        outs[3] = (dyv * o_ref[...] * _dsilu(gs)).astype(BF16)
        dk_acc[...] = jnp.zeros_like(dk_acc)
        dv_acc[...] = jnp.zeros_like(dv_acc)

        row = lax.broadcasted_iota(jnp.int32, (tq, tq), 0)
        col = lax.broadcasted_iota(jnp.int32, (tq, tq), 1)
        incl = (col <= row).astype(BF16)
        excl = (col < row).astype(BF16)
        causal = row < col
        lane = lax.broadcasted_iota(jnp.int32, (1, LANES), 1)

        def block(i, j, qi, doi, tot_row, p_left, g_left, dq, masked):
            ks = pl.ds(pl.multiple_of(j * tq, tq), tq)
            kj = kn[ks, :]
            z = _dot(kj, qi, NT) * scale
            sp = _softplus(z)
            ls = -sp
            if masked:
                ls = jnp.where(causal, ls, 0.0)
            hi = ls.astype(BF16)
            lo = (ls - hi.astype(F32)).astype(BF16)
            p_inc = _dot(incl, hi, NN) + _dot(incl, lo, NN) + p_left
            beta = jnp.exp(z - sp)
            w = beta * jnp.exp(tot_row - p_inc)
            if masked:
                w = jnp.where(causal, w, 0.0)
            dw = _dot(vb[ks, :], doi, NT)
            g = w * dw
            g_before = _dot(excl, g.astype(BF16), NN) + g_left
            dz = g * (1.0 - beta) - beta * g_before
            if masked:
                dz = jnp.where(causal, dz, 0.0)
            dzb = dz.astype(BF16)
            dv_acc[ks, :] += _dot(w.astype(BF16), doi, NN)
            dk_acc[ks, :] += _dot(dzb, qi, NN)
            dq = dq + _dot(dzb, kj, TN)
            p_left = p_left + jnp.sum(ls, axis=0, keepdims=True)
            g_left = g_left + jnp.sum(g, axis=0, keepdims=True)
            return p_left, g_left, dq

        def q_block(i, _):
            qs = pl.ds(pl.multiple_of(i * tq, tq), tq)
            qi = qn[qs, :]
            doi = dob[qs, :]
            tot_row = jnp.transpose(tot_ref[qs, :])[0:1, :]
            zero_row = jnp.zeros((1, tq), F32)

            def k_step(j, carry):
                return block(i, j, qi, doi, tot_row, carry[0], carry[1], carry[2], False)

            first = jnp.sum(jnp.where(lane == i, first_ref[0:1, :], 0.0)).astype(jnp.int32)
            first = jnp.clip(first, 0, i)
            carry = lax.fori_loop(first, i, k_step, (zero_row, zero_row, jnp.zeros((tq, HEAD_DIM), F32)))
            _, _, dq = block(i, i, qi, doi, tot_row, carry[0], carry[1], carry[2], True)
            dq_acc[qs, :] = dq * scale
            return 0

        lax.fori_loop(0, nq, q_block, 0)

        def norm_bwd(src, gain, dn, slot, gain_row):
            v = src[...]
            r = lax.rsqrt(jnp.mean(v * v, axis=-1, keepdims=True) + EPS)
            vhat = v * r
            gains_ref[gain_row:gain_row + 1, :] = jnp.sum(dn * vhat, axis=0, keepdims=True)
            dhat = dn * gain[...]
            outs[slot] = (r * (dhat - vhat * jnp.mean(dhat * vhat, axis=-1, keepdims=True))).astype(BF16)

        gains_ref[...] = jnp.zeros_like(gains_ref)
        norm_bwd(q_ref, qg_ref, dq_acc[...], 0, 0)
        norm_bwd(k_ref, kg_ref, dk_acc[...] * scale, 1, 1)
        outs[2] = dv_acc[...].astype(BF16)
        copies = [pltpu.make_async_copy(
            outs.at[s], dproj_ref.at[:, pl.ds(pl.multiple_of((s * nh + h) * HEAD_DIM, HEAD_DIM), HEAD_DIM)], sems.at[s])
            for s in range(4)]
        for cp in copies:
            cp.start()
        for cp in copies:
            cp.wait()

    col_block = lambda off: pl.BlockSpec((t, HEAD_DIM), lambda h: (0, off + h))
    vec = pl.BlockSpec((1, HEAD_DIM), lambda h: (0, 0))
    head_scr = lambda dt: pltpu.VMEM((t, HEAD_DIM), dt)
    return pl.pallas_call(
        body, name="attention_bwd", grid=(nh,),
        out_shape=[jax.ShapeDtypeStruct((t, in_cols), BF16), jax.ShapeDtypeStruct((nh, 8, HEAD_DIM), F32)],
        in_specs=[col_block(0), col_block(nh), col_block(2 * nh), col_block(3 * nh),
                  col_block(0), col_block(0), pl.BlockSpec((None, 8, LANES), lambda h: (h, 0, 0)), col_block(0), vec, vec],
        out_specs=[pl.BlockSpec(memory_space=pl.ANY), pl.BlockSpec((None, 8, HEAD_DIM), lambda h: (h, 0, 0))],
        scratch_shapes=[head_scr(BF16), head_scr(BF16), head_scr(BF16), head_scr(BF16),
                        head_scr(F32), head_scr(F32), head_scr(F32),
                        pltpu.VMEM((4, t, HEAD_DIM), BF16), pltpu.SemaphoreType.DMA((4,))],
        compiler_params=_params(("arbitrary",)),
    )(proj, proj, proj, proj, o, tot, firsts, dycat, qg, kg)


def _gate_bwd(dycat, z, proj, dproj, cw):
    t = z.shape[0]
    tt = _pick(t, 256, 16)
    ncol = proj.shape[1] // cw
    ycol = dycat.shape[1] // cw - 1

    def body(dy_ref, z_ref, g_ref, dp_in, dz_ref, dp_ref, sums_ref):
        i = pl.program_id(0)

        @pl.when(i == 0)
        def _():
            sums_ref[...] = jnp.zeros_like(sums_ref)

        g = g_ref[...]
        dy = dy_ref[...]
        dz = dy * _silu(g)
        dz_ref[...] = dz.astype(BF16)
        dp_ref[...] = (dy * z_ref[...] * _dsilu(g)).astype(BF16)
        sums_ref[0:1, :] += jnp.sum(dz, axis=0, keepdims=True)

    return pl.pallas_call(
        body, name="gate_bwd", grid=(t // tt,),
        out_shape=[jax.ShapeDtypeStruct((t, cw), BF16), jax.ShapeDtypeStruct(dproj.shape, dproj.dtype),
                   jax.ShapeDtypeStruct((8, cw), F32)],
        in_specs=[pl.BlockSpec((tt, cw), lambda i: (i, ycol)), pl.BlockSpec((tt, cw), lambda i: (i, 0)),
                  pl.BlockSpec((tt, cw), lambda i: (i, ncol - 1)), pl.BlockSpec(memory_space=pl.ANY)],
        out_specs=[pl.BlockSpec((tt, cw), lambda i: (i, 0)), pl.BlockSpec((tt, cw), lambda i: (i, ncol - 1)),
                   pl.BlockSpec((8, cw), lambda i: (0, 0))],
        input_output_aliases={3: 1},
        compiler_params=_params(("arbitrary",)),
    )(dycat, z, proj, dproj)


def _ln_bwd(da, hc, ln_g, ln_b):
    t, cw = hc.shape
    tt = _pick(t, 256, 16)

    def body(da_ref, hc_ref, lg_ref, lb_ref, dh_ref, sums_ref):
        i = pl.program_id(0)

        @pl.when(i == 0)
        def _():
            sums_ref[...] = jnp.zeros_like(sums_ref)

        hcv = hc_ref[...]
        mu = jnp.mean(hcv, axis=-1, keepdims=True)
        xc = hcv - mu
        r = lax.rsqrt(jnp.mean(xc * xc, axis=-1, keepdims=True) + EPS)
        xhat = xc * r
        ln = xhat * lg_ref[...] + lb_ref[...]
        dln = da_ref[...] * _dsilu(ln)
        dxhat = dln * lg_ref[...]
        dhc = r * (dxhat - jnp.mean(dxhat, axis=-1, keepdims=True)
                   - xhat * jnp.mean(dxhat * xhat, axis=-1, keepdims=True))
        dh_ref[...] = dhc
        sums_ref[0:1, :] += jnp.sum(dln * xhat, axis=0, keepdims=True)
        sums_ref[1:2, :] += jnp.sum(dln, axis=0, keepdims=True)
        sums_ref[2:3, :] += jnp.sum(dhc, axis=0, keepdims=True)

    tile = pl.BlockSpec((tt, cw), lambda i: (i, 0))
    vec = pl.BlockSpec((1, cw), lambda i: (0, 0))
    return pl.pallas_call(
        body, name="ln_bwd", grid=(t // tt,),
        out_shape=[jax.ShapeDtypeStruct((t, cw), F32), jax.ShapeDtypeStruct((8, cw), F32)],
        in_specs=[tile, tile, vec, vec],
        out_specs=[tile, pl.BlockSpec((8, cw), lambda i: (0, 0))],
        compiler_params=_params(("arbitrary",)),
    )(da, hc, ln_g, ln_b)


def _conv_bwd(dhc, proj, w_dw, dproj, kc, cw):
    t = proj.shape[0]
    tt = _pick(t, 128, HALO)
    per = tt // HALO
    nt = t // tt
    chunk = _pick(cw, 256)
    ncol = proj.shape[1] // cw
    wr = w_dw.shape[0]

    def body(d_ref, dn_ref, u_ref, g_ref, up_ref, gp_ref, w_ref, dp_in, dp_ref, dw_ref, dbuf, hbuf):
        i = pl.program_id(0)

        @pl.when(i == 0)
        def _():
            dw_ref[...] = jnp.zeros_like(dw_ref)

        dbuf[pl.ds(0, tt), :] = d_ref[...]
        dbuf[pl.ds(tt, HALO), :] = jnp.where(i < nt - 1, dn_ref[...], 0.0)
        hbuf[pl.ds(HALO, tt), :] = _glu_rows(u_ref, g_ref)
        hbuf[pl.ds(0, HALO), :] = jnp.where(i > 0, _glu_rows(up_ref, gp_ref), 0.0)
        for lo in range(0, cw, chunk):
            dhg = _conv_taps(dbuf, w_ref, kc, lo, tt, chunk, lambda j: (kc - 1) - j)
            u = u_ref[:, lo:lo + chunk]
            sg = _sigmoid(g_ref[:, lo:lo + chunk])
            dp_ref[:, lo:lo + chunk] = (dhg * sg).astype(BF16)
            dp_ref[:, cw + lo:cw + lo + chunk] = (dhg * u * sg * (1.0 - sg)).astype(BF16)
            dtile = d_ref[:, lo:lo + chunk]
            for j in range(kc):
                prod = dtile * hbuf[pl.ds(HALO - (kc - 1) + j, tt), pl.ds(lo, chunk)]
                dw_ref[j:j + 1, lo:lo + chunk] += jnp.sum(prod, axis=0, keepdims=True)

    tile = lambda g: pl.BlockSpec((tt, cw), lambda i: (i, g))
    prev = lambda g: pl.BlockSpec((HALO, cw), lambda i: (jnp.maximum(i * per - 1, 0), g))
    return pl.pallas_call(
        body, name="conv_bwd", grid=(nt,),
        out_shape=[jax.ShapeDtypeStruct(dproj.shape, dproj.dtype), jax.ShapeDtypeStruct((wr, cw), F32)],
        in_specs=[pl.BlockSpec((tt, cw), lambda i: (i, 0)),
                  pl.BlockSpec((HALO, cw), lambda i: (jnp.minimum((i + 1) * per, nt * per - 1), 0)),
                  tile(ncol - 3), tile(ncol - 2), prev(ncol - 3), prev(ncol - 2),
                  pl.BlockSpec((wr, cw), lambda i: (0, 0)), pl.BlockSpec(memory_space=pl.ANY)],
        out_specs=[pl.BlockSpec((tt, 2 * cw), lambda i: (i, (ncol - 3) // 2)),
                   pl.BlockSpec((wr, cw), lambda i: (0, 0))],
        scratch_shapes=[pltpu.VMEM((tt + HALO, cw), F32), pltpu.VMEM((HALO + tt, cw), F32)],
        input_output_aliases={7: 0},
        compiler_params=_params(("arbitrary",)),
    )(dhc, dhc, proj, proj, proj, proj, w_dw, dproj)


def _input_grad(dh, x, dout, norm_g, mod):
    t, d = x.shape
    tt = _pick(t, 128, 16)

    def body(dh_ref, x_ref, do_ref, g_ref, mod_ref, gx_ref, sums_ref):
        i = pl.program_id(0)

        @pl.when(i == 0)
        def _():
            sums_ref[...] = jnp.zeros_like(sums_ref)

        xv = x_ref[...]
        dhv = dh_ref[...]
        r = lax.rsqrt(jnp.mean(xv * xv, axis=-1, keepdims=True) + EPS)
        xn = xv * r
        g = g_ref[...]
        one_scale = 1.0 + mod_ref[1:2, :]
        dxn = dhv * g * one_scale
        gx_ref[...] = do_ref[...] + r * (dxn - xn * jnp.mean(dxn * xn, axis=-1, keepdims=True))
        sums_ref[0:1, :] += jnp.sum(dhv, axis=0, keepdims=True)
        sums_ref[1:2, :] += jnp.sum(dhv * (xn * g), axis=0, keepdims=True)
        sums_ref[2:3, :] += jnp.sum(dhv * one_scale * xn, axis=0, keepdims=True)

    tile = pl.BlockSpec((tt, d), lambda i: (i, 0))
    return pl.pallas_call(
        body, name="input_grad", grid=(t // tt,),
        out_shape=[jax.ShapeDtypeStruct((t, d), F32), jax.ShapeDtypeStruct((8, d), F32)],
        in_specs=[tile, tile, tile, pl.BlockSpec((1, d), lambda i: (0, 0)), pl.BlockSpec((3, d), lambda i: (0, 0))],
        out_specs=[tile, pl.BlockSpec((8, d), lambda i: (0, 0))],
        compiler_params=_params(("arbitrary",)),
    )(dh, x, dout, norm_g, mod)


def _sum_adam(parts, w, m, v, name):
    r, c = w.shape
    n_parts = parts.shape[0]
    tr = _pick(r, 128, 16) if r % 16 == 0 else r
    tc = _pick(c, 2048)

    def body(p_ref, w_ref, m_ref, v_ref, g_ref, d_ref, nm_ref, nv_ref):
        g = p_ref[0].astype(F32)
        for i in range(1, n_parts):
            g = g + p_ref[i].astype(F32)
        d, nm, nv = _adam(w_ref[...], g, m_ref[...], v_ref[...])
        g_ref[...] = g
        d_ref[...] = d
        nm_ref[...] = nm
        nv_ref[...] = nv

    tile = pl.BlockSpec((tr, tc), lambda i, j: (i, j))
    out = jax.ShapeDtypeStruct((r, c), F32)
    return pl.pallas_call(
        body, name=name, grid=(r // tr, c // tc),
        out_shape=[out] * 4,
        in_specs=[pl.BlockSpec((n_parts, tr, tc), lambda i, j: (0, i, j)), tile, tile, tile],
        out_specs=[tile] * 4,
        compiler_params=_params(("arbitrary", "arbitrary")),
    )(parts, w, m, v)


def _ada_grad_adam(s_t, dm, w, m, v):
    d, n = w.shape
    tr = _pick(d, 256, 16)

    def body(s_ref, dm_ref, w_ref, m_ref, v_ref, g_ref, d_ref, nm_ref, nv_ref):
        g = lax.dot_general(s_ref[...], dm_ref[...], (NN, ((), ())), preferred_element_type=F32,
                            precision=lax.Precision.HIGHEST)
        dl, nm, nv = _adam(w_ref[...], g, m_ref[...], v_ref[...])
        g_ref[...] = g
        d_ref[...] = dl
        nm_ref[...] = nm
        nv_ref[...] = nv

    tile = pl.BlockSpec((tr, n), lambda i: (i, 0))
    out = jax.ShapeDtypeStruct((d, n), F32)
    return pl.pallas_call(
        body, name="ada_grad_adam", grid=(d // tr,),
        out_shape=[out] * 4,
        in_specs=[pl.BlockSpec((tr, NDEV), lambda i: (i, 0)), pl.BlockSpec((NDEV, n), lambda i: (0, 0)),
                  tile, tile, tile],
        out_specs=[tile] * 4,
        compiler_params=_params(("arbitrary",)),
    )(s_t, dm, w, m, v)


def _silu_t(c_all):
    n, d = c_all.shape

    def body(c_ref, o_ref):
        o_ref[...] = jnp.transpose(_silu(c_ref[...]))

    return pl.pallas_call(
        body, name="silu_t", out_shape=jax.ShapeDtypeStruct((d, n), F32),
        in_specs=[pl.BlockSpec(memory_space=pltpu.VMEM)], out_specs=pl.BlockSpec(memory_space=pltpu.VMEM),
        compiler_params=pltpu.CompilerParams(vmem_limit_bytes=VMEM_LIMIT),
    )(c_all)


def _rows128(v):
    return v.reshape(-1, LANES)


def _pad_rows(a, rows):
    return jnp.pad(a, ((0, rows - a.shape[0]), (0, 0)))


def kernel(x, c, norm_g, w_ada, b_ada, w_in, q_norm_g, k_norm_g, w_dw, b_dw, ln_g, ln_b, w_pw, b_pw, w_out, loss_target, m_norm_g, m_w_ada, m_b_ada, m_w_in, m_q_norm_g, m_k_norm_g, m_w_dw, m_b_dw, m_ln_g, m_ln_b, m_w_pw, m_b_pw, m_w_out, v_norm_g, v_w_ada, v_b_ada, v_w_in, v_q_norm_g, v_k_norm_g, v_w_dw, v_b_dw, v_ln_g, v_ln_b, v_w_pw, v_b_pw, v_w_out):
    _, t, d = x.shape
    n_ada = w_ada.shape[2]
    ns = w_in.shape[2]
    kc, cwl = w_dw.shape[1], w_dw.shape[2]
    cw = cwl * NDEV
    sb = d - cw
    nh = sb // HEAD_DIM
    assert sb == cw and kc - 1 <= HALO and NDEV * ns == 4 * sb + 3 * cw
    my = 4 * lax.axis_index("x") + 2 * lax.axis_index("y") + lax.axis_index("c")

    x2, tg2 = x[0], loss_target[0]

    wdw_rows = -(-kc // 8) * 8
    wdw_pad = _pad_rows(w_dw[0], wdw_rows)
    pay1 = jnp.concatenate([_rows128(c[0]), _rows128(wdw_pad.reshape(-1))], axis=0)
    (g1,) = _all_gather([pay1], "gather_cond", pltpu.VMEM)
    c_rows = d // LANES
    c_all = g1[:, :c_rows].reshape(NDEV, d)
    wdw_all = g1[:, c_rows:].reshape(NDEV, wdw_rows, cwl).transpose(1, 0, 2).reshape(wdw_rows, cw)

    b_ada_loc = lax.dynamic_slice(b_ada, (0, my * n_ada), (1, n_ada))
    mod_cols = _ada_matmul(c_all, w_ada[0], b_ada_loc)
    (g2,) = _all_gather([mod_cols], "gather_mod", pltpu.VMEM)
    mod_mine = lax.dynamic_index_in_dim(g2, my, axis=1, keepdims=False)
    mod = mod_mine.reshape(3, d)

    core = lax.axis_index("c").astype(jnp.int32).reshape(1)
    h = _modulated_norm(x2, norm_g, mod)
    wg_in, proj = _gather_proj(h, w_in[0].astype(BF16))
    (a, hc), partly = _conv_fwd(proj, wdw_all, b_dw, ln_g, ln_b, kc, cw,
                                _gather_ride([w_out[0].astype(BF16), w_pw[0].astype(BF16)]))
    wg_out, wg_pw = _gather_finish(partly)
    o, tot, ycat, firsts = _attention_fwd(proj, q_norm_g, k_norm_g, nh, d)
    z, ycat = _pointwise_fwd(a, wg_pw, b_pw, proj, ycat, cw)
    dout, dy, out_sums = _out_matmul(ycat, wg_out, x2, tg2, mod)

    dycat = _matmul_nt_slabs(dy, wg_out, "dycat_matmul")
    p_wout = _weight_grad_row_slabs(ycat, dy, "w_out_grad")
    dproj, gains = _attention_bwd(proj, o, tot, firsts, dycat, q_norm_g, k_norm_g, nh)
    dz, dproj, dz_sums = _gate_bwd(dycat, z, proj, dproj, cw)
    da = _matmul_nt_slabs(dz, wg_pw, "da_matmul")
    p_wpw = _weight_grad_row_slabs(a, dz, "w_pw_grad")
    dhc, ln_sums = _ln_bwd(da, hc, ln_g, ln_b)
    dproj, dwdw = _conv_bwd(dhc, proj, wdw_all, dproj, kc, cw)
    p_wdw = dwdw.reshape(wdw_rows, NDEV, cwl).transpose(1, 0, 2).astype(BF16)

    def chip_sums(parts, name):
        split = [p.reshape(4, 2, *p.shape[1:]) for p in parts]
        theirs = _sibling_exchange(split, name + "_sibling")
        return [_pair_sum(m, s, core, f"{name}_pair_sum_{i}") for i, (m, s) in enumerate(zip(split, theirs))]

    q_small = chip_sums([p_wout, p_wpw, p_wdw], "small_grads")
    (p_win,), (r_wout, r_wpw, r_wdw) = _weight_grad_col_slabs(h, dproj, ns, _chip_exchange_ride(q_small))
    q_win = chip_sums([p_win], "w_in_grad")
    (dh,), (r_win,) = _dh_matmul(dproj, wg_in, _chip_exchange_ride(q_win))
    grad_x, in_sums = _input_grad(dh, x2, dout, norm_g, mod)

    dmod = jnp.concatenate([in_sums[0], in_sums[1], out_sums[0]])
    loss_part = 0.5 / d * jnp.sum(out_sums[1].reshape(-1, LANES), axis=0)
    small = [in_sums[2], dmod, jnp.sum(gains[:, 0], axis=0), jnp.sum(gains[:, 1], axis=0),
             ln_sums[2], ln_sums[0], ln_sums[1], dz_sums[0], loss_part]
    sizes = [s.shape[0] for s in small]
    packed = _rows128(jnp.concatenate(small))
    n_rows = -(-packed.shape[0] // 8) * 8
    (g3,) = _all_gather([_pad_rows(packed, n_rows)], "gather_small", pltpu.VMEM)

    def pack_state(names_vals):
        flat = jnp.concatenate([v.reshape(-1) for v in names_vals] + [jnp.zeros((LANES,), F32)])
        return _pad_rows(_rows128(flat), n_rows)

    small_w = pack_state([norm_g, b_ada, q_norm_g, k_norm_g, b_dw, ln_g, ln_b, b_pw])
    small_m = pack_state([m_norm_g, m_b_ada, m_q_norm_g, m_k_norm_g, m_b_dw, m_ln_g, m_ln_b, m_b_pw])
    small_v = pack_state([v_norm_g, v_b_ada, v_q_norm_g, v_k_norm_g, v_b_dw, v_ln_g, v_ln_b, v_b_pw])
    sg, sd, sm, sv = _sum_adam(g3, small_w, small_m, small_v, "small_adam")

    def unpack(p):
        flat = p.reshape(-1)
        outs, off = [], 0
        for n in sizes[:-1]:
            outs.append(flat[off:off + n].reshape(1, n))
            off += n
        return outs, flat[off:off + LANES]

    g_small, loss_lanes = unpack(sg)
    d_small, _ = unpack(sd)
    m_small, _ = unpack(sm)
    v_small, _ = unpack(sv)
    loss = jnp.sum(loss_lanes)

    off = sizes[0]
    dmod_all = g3.reshape(NDEV, -1)[:, off:off + 3 * d]
    dmod_loc = lax.dynamic_slice(dmod_all, (0, my * n_ada), (NDEV, n_ada))
    ada = _ada_grad_adam(_silu_t(c_all), dmod_loc, w_ada[0], m_w_ada[0], v_w_ada[0])
    win = _sum_adam(r_win, w_in[0], m_w_in[0], v_w_in[0], "w_in_adam")
    wout = _sum_adam(r_wout, w_out[0], m_w_out[0], v_w_out[0], "w_out_adam")
    wpw = _sum_adam(r_wpw, w_pw[0], m_w_pw[0], v_w_pw[0], "w_pw_adam")
    wdw_state = [_pad_rows(s[0], wdw_rows) for s in (w_dw, m_w_dw, v_w_dw)]
    wdw = [r[:kc] for r in _sum_adam(r_wdw, *wdw_state, "w_dw_adam")]

    def group(k, small_list):
        s = small_list
        return [s[0], ada[k][None], s[1], win[k][None], s[2], s[3], wdw[k][None], s[4], s[5], s[6],
                wpw[k][None], s[7], wout[k][None]]

    return (loss, grad_x[None], *group(0, g_small), *group(1, d_small), *group(2, m_small), *group(3, v_small))
```

```python
import functools

import jax
import jax.numpy as jnp
from jax import lax
from jax.experimental import pallas as pl
from jax.experimental.pallas import tpu as pltpu

F32 = jnp.float32
BF16 = jnp.bfloat16
NDEV = 8
HEAD_DIM = 128
LANES = 128
HALO = 32
EPS = 1e-6
DEAD_LOG_WEIGHT = -104.0
VMEM_LIMIT = 56 * 1024 * 1024
MESH = pl.DeviceIdType.MESH

ADAM_LR = 0.001
ADAM_B1 = 0.9
ADAM_B2 = 0.999
ADAM_EPS = 1e-08
ADAM_WD = 0.01
ADAM_STEP = 10


def _params(sem=None):
    return pltpu.CompilerParams(dimension_semantics=sem, vmem_limit_bytes=VMEM_LIMIT)


def _pick(n, pref, unit=LANES):
    best = None
    for d in range(unit, min(n, pref) + 1, unit):
        if n % d == 0:
            best = d
    return best if best is not None else n


def _sigmoid(z):
    return 1.0 / (1.0 + jnp.exp(-z))


def _silu(z):
    return z * _sigmoid(z)


def _dsilu(z):
    s = _sigmoid(z)
    return s * (1.0 + z * (1.0 - s))


def _softplus(z):
    return jnp.maximum(z, 0.0) + jnp.log(1.0 + jnp.exp(-jnp.abs(z)))


def _dot(a, b, dims):
    return lax.dot_general(a, b, (dims, ((), ())), preferred_element_type=F32)


NN = ((1,), (0,))
NT = ((1,), (1,))
TN = ((0,), (0,))


def _adam(w, g, m, v):
    m = ADAM_B1 * m + (1.0 - ADAM_B1) * g
    v = ADAM_B2 * v + (1.0 - ADAM_B2) * (g * g)
    m_hat = m / (1.0 - ADAM_B1 ** ADAM_STEP)
    v_hat = v / (1.0 - ADAM_B2 ** ADAM_STEP)
    delta = -ADAM_LR * (m_hat / (jnp.sqrt(v_hat) + ADAM_EPS) + ADAM_WD * w)
    return delta, m, v


def _place():
    x, y, c = lax.axis_index("x"), lax.axis_index("y"), lax.axis_index("c")
    return x, y, c


def _flip(v, bit):
    return 1 - v if bit else v


def _all_gather(arrs, name, space):
    n = len(arrs)

    def body(*refs):
        ins, outs = refs[:n], refs[n:2 * n]
        send_sems, recv_sems, local_sems = refs[2 * n:]
        x, y, c = _place()
        me, sibling = (x, y, c), (x, y, 1 - c)
        chips = [(1 - x, y), (x, 1 - y), (1 - x, 1 - y)]

        def rows(a, p):
            return outs[a].at[4 * p[0] + 2 * p[1] + p[2]]

        def copy(a, k, block, to, src=None):
            return pltpu.make_async_remote_copy(
                src_ref=rows(a, block) if src is None else src, dst_ref=rows(a, block),
                send_sem=send_sems.at[7 * a + k], recv_sem=recv_sems.at[7 * a + k],
                device_id=to, device_id_type=MESH)

        mine = [pltpu.make_async_copy(ins[a], rows(a, me), local_sems.at[a]) for a in range(n)]
        for cp in mine:
            cp.start()
        first = []
        for a in range(n):
            first.append(copy(a, 0, me, sibling, src=ins[a]))
            first += [copy(a, 1 + j, me, (*chip, c), src=ins[a]) for j, chip in enumerate(chips)]
        for cp in first:
            cp.start()
        passed = []
        for j, chip in enumerate(chips):
            for a in range(n):
                copy(a, 1 + j, (*chip, c), me).wait_recv()
                fwd = copy(a, 4 + j, (*chip, c), sibling)
                fwd.start()
                passed.append(fwd)
        for a in range(n):
            copy(a, 0, sibling, me).wait_recv()
            for j, chip in enumerate(chips):
                copy(a, 4 + j, (*chip, 1 - c), me).wait_recv()
        for cp in first + passed:
            cp.wait_send()
        for cp in mine:
            cp.wait()

    spec = pl.BlockSpec(memory_space=space)
    return pl.pallas_call(
        body, name=name,
        out_shape=[jax.ShapeDtypeStruct((NDEV,) + a.shape, a.dtype) for a in arrs],
        in_specs=[spec] * n, out_specs=[spec] * n,
        scratch_shapes=[pltpu.SemaphoreType.DMA((7 * n,)), pltpu.SemaphoreType.DMA((7 * n,)),
                        pltpu.SemaphoreType.DMA((n,))],
        compiler_params=pltpu.CompilerParams(vmem_limit_bytes=VMEM_LIMIT),
    )(*arrs)


class _Ride:
    def __init__(self, ins, out_shapes, n_sems, start, finish):
        self.ins, self.out_shapes, self.n_sems, self.start, self.finish = ins, out_shapes, n_sems, start, finish


def _call(body, *, name, grid, out_shape, in_specs, out_specs, args, scratch_shapes=(), aliases=None, ride=None):
    sem = ("arbitrary",) * len(grid)
    if ride is None:
        return pl.pallas_call(
            body, name=name, grid=grid, out_shape=out_shape, in_specs=in_specs, out_specs=out_specs,
            scratch_shapes=list(scratch_shapes), input_output_aliases=aliases or {},
            compiler_params=_params(sem))(*args), []
    n_in, n_out, n_scr = len(in_specs), len(out_specs), len(scratch_shapes)
    r_in, r_out = len(ride.ins), len(ride.out_shapes)

    def carried(*refs):
        ins, rins = refs[:n_in], refs[n_in:n_in + r_in]
        o0 = n_in + r_in
        outs, routs = refs[o0:o0 + n_out], refs[o0 + n_out:o0 + n_out + r_out]
        scratch = refs[o0 + n_out + r_out:o0 + n_out + r_out + n_scr]
        sems = refs[o0 + n_out + r_out + n_scr:]
        first = functools.reduce(lambda a, b: a & b, [pl.program_id(i) == 0 for i in range(len(grid))])
        last = functools.reduce(lambda a, b: a & b, [pl.program_id(i) == grid[i] - 1 for i in range(len(grid))])

        @pl.when(first)
        def _():
            ride.start(rins, routs, *sems)

        body(*ins, *outs, *scratch)

        @pl.when(last)
        def _():
            ride.finish(rins, routs, *sems)

    hbm = pl.BlockSpec(memory_space=pl.ANY)
    res = pl.pallas_call(
        carried, name=name, grid=grid,
        out_shape=list(out_shape) + list(ride.out_shapes),
        in_specs=list(in_specs) + [hbm] * r_in, out_specs=list(out_specs) + [hbm] * r_out,
        scratch_shapes=list(scratch_shapes) + [pltpu.SemaphoreType.DMA((ride.n_sems,))] * 3,
        input_output_aliases=aliases or {}, compiler_params=_params(sem))(*args, *ride.ins)
    return res[:n_out], res[n_out:]


def _chips(x, y):
    return [(1 - x, y), (x, 1 - y), (1 - x, 1 - y)]


def _gather_ride(arrs):
    n = len(arrs)

    def copies(ins, outs, send_sems, recv_sems):
        x, y, c = _place()
        me = 4 * x + 2 * y + c
        peers = [(x, y, 1 - c)] + [(*chip, c) for chip in _chips(x, y)]
        return [pltpu.make_async_remote_copy(
            src_ref=ins[a], dst_ref=outs[a].at[me], send_sem=send_sems.at[4 * a + k], recv_sem=recv_sems.at[4 * a + k],
            device_id=p, device_id_type=MESH) for a in range(n) for k, p in enumerate(peers)], me

    def start(ins, outs, send_sems, recv_sems, local_sems):
        cps, me = copies(ins, outs, send_sems, recv_sems)
        for a in range(n):
            pltpu.make_async_copy(ins[a], outs[a].at[me], local_sems.at[a]).start()
        for cp in cps:
            cp.start()

    def finish(ins, outs, send_sems, recv_sems, local_sems):
        cps, me = copies(ins, outs, send_sems, recv_sems)
        for cp in cps:
            cp.wait_recv()
        for cp in cps:
            cp.wait_send()
        for a in range(n):
            pltpu.make_async_copy(ins[a], outs[a].at[me], local_sems.at[a]).wait()

    return _Ride(arrs, [jax.ShapeDtypeStruct((NDEV,) + a.shape, a.dtype) for a in arrs], 4 * n, start, finish)


def _gather_finish(arrs):
    n = len(arrs)

    def body(*refs):
        outs = refs[n:2 * n]
        send_sems, recv_sems = refs[2 * n:]
        x, y, c = _place()
        cps = []
        for a in range(n):
            for k, chip in enumerate(_chips(x, y)):
                blk = 4 * chip[0] + 2 * chip[1]
                cps.append((pltpu.make_async_remote_copy(
                    src_ref=outs[a].at[blk + c], dst_ref=outs[a].at[blk + c],
                    send_sem=send_sems.at[3 * a + k], recv_sem=recv_sems.at[3 * a + k],
                    device_id=(x, y, 1 - c), device_id_type=MESH),
                    pltpu.make_async_remote_copy(
                    src_ref=outs[a].at[blk + 1 - c], dst_ref=outs[a].at[blk + 1 - c],
                    send_sem=send_sems.at[3 * a + k], recv_sem=recv_sems.at[3 * a + k],
                    device_id=(x, y, 1 - c), device_id_type=MESH)))
        for send, _ in cps:
            send.start()
        for _, recv in cps:
            recv.wait_recv()
        for send, _ in cps:
            send.wait_send()

    spec = pl.BlockSpec(memory_space=pl.ANY)
    return pl.pallas_call(
        body, name="gather_finish",
        out_shape=[jax.ShapeDtypeStruct(a.shape, a.dtype) for a in arrs],
        in_specs=[spec] * n, out_specs=[spec] * n,
        scratch_shapes=[pltpu.SemaphoreType.DMA((3 * n,)), pltpu.SemaphoreType.DMA((3 * n,))],
        input_output_aliases={a: a for a in range(n)},
        compiler_params=pltpu.CompilerParams(vmem_limit_bytes=VMEM_LIMIT),
    )(*arrs)


def _sibling_exchange(arrs, name):
    n = len(arrs)

    def body(*refs):
        ins, outs = refs[:n], refs[n:2 * n]
        send_sems, recv_sems = refs[2 * n:]
        x, y, c = _place()
        cps = [pltpu.make_async_remote_copy(
            src_ref=ins[a].at[:, 1 - c], dst_ref=outs[a], send_sem=send_sems.at[a], recv_sem=recv_sems.at[a],
            device_id=(x, y, 1 - c), device_id_type=MESH) for a in range(n)]
        for cp in cps:
            cp.start()
        for cp in cps:
            cp.wait()

    spec = pl.BlockSpec(memory_space=pl.ANY)
    return pl.pallas_call(
        body, name=name,
        out_shape=[jax.ShapeDtypeStruct((4,) + a.shape[2:], a.dtype) for a in arrs],
        in_specs=[spec] * n, out_specs=[spec] * n,
        scratch_shapes=[pltpu.SemaphoreType.DMA((n,)), pltpu.SemaphoreType.DMA((n,))],
        compiler_params=pltpu.CompilerParams(vmem_limit_bytes=VMEM_LIMIT),
    )(*arrs)


def _pair_sum(mine, theirs, core, name):
    _, _, r, c = mine.shape
    tr = _pick(r, 512, 16)

    def body(core_ref, a_ref, b_ref, o_ref):
        o_ref[...] = (a_ref[...].astype(F32) + b_ref[...].astype(F32)).astype(BF16)

    return pl.pallas_call(
        body, name=name,
        grid_spec=pltpu.PrefetchScalarGridSpec(
            num_scalar_prefetch=1, grid=(4, r // tr),
            in_specs=[pl.BlockSpec((None, None, tr, c), lambda i, k, core_ref: (i, core_ref[0], k, 0)),
                      pl.BlockSpec((None, tr, c), lambda i, k, core_ref: (i, k, 0))],
            out_specs=pl.BlockSpec((None, tr, c), lambda i, k, core_ref: (i, k, 0))),
        out_shape=jax.ShapeDtypeStruct((4, r, c), BF16),
        compiler_params=_params(("arbitrary", "arbitrary")),
    )(core, mine, theirs)


def _chip_exchange_ride(arrs):
    n = len(arrs)

    def copies(ins, outs, send_sems, recv_sems):
        x, y, c = _place()
        mine = 2 * x + y
        return [pltpu.make_async_remote_copy(
            src_ref=ins[a].at[2 * chip[0] + chip[1]], dst_ref=outs[a].at[mine],
            send_sem=send_sems.at[3 * a + k], recv_sem=recv_sems.at[3 * a + k],
            device_id=(*chip, c), device_id_type=MESH) for a in range(n) for k, chip in enumerate(_chips(x, y))], mine

    def start(ins, outs, send_sems, recv_sems, local_sems):
        cps, mine = copies(ins, outs, send_sems, recv_sems)
        for a in range(n):
            pltpu.make_async_copy(ins[a].at[mine], outs[a].at[mine], local_sems.at[a]).start()
        for cp in cps:
            cp.start()

    def finish(ins, outs, send_sems, recv_sems, local_sems):
        cps, mine = copies(ins, outs, send_sems, recv_sems)
        for cp in cps:
            cp.wait_recv()
        for cp in cps:
            cp.wait_send()
        for a in range(n):
            pltpu.make_async_copy(ins[a].at[mine], outs[a].at[mine], local_sems.at[a]).wait()

    return _Ride(arrs, [jax.ShapeDtypeStruct(a.shape, a.dtype) for a in arrs], 3 * n, start, finish)


def _ada_matmul(c_all, w_loc, b_loc):
    d, n = w_loc.shape
    bn = _pick(n, 512)

    def body(c_ref, w_ref, b_ref, o_ref):
        s = _silu(c_ref[...]).astype(BF16)
        o_ref[...] = _dot(s, w_ref[...].astype(BF16), NN) + b_ref[...]

    return pl.pallas_call(
        body, name="ada_matmul", grid=(n // bn,),
        out_shape=jax.ShapeDtypeStruct((NDEV, n), F32),
        in_specs=[pl.BlockSpec((NDEV, d), lambda j: (0, 0)), pl.BlockSpec((d, bn), lambda j: (0, j)),
                  pl.BlockSpec((1, bn), lambda j: (0, j))],
        out_specs=pl.BlockSpec((NDEV, bn), lambda j: (0, j)),
        compiler_params=_params(("arbitrary",)),
    )(c_all, w_loc, b_loc)


def _modulated_norm(x, norm_g, mod):
    t, d = x.shape
    tt = _pick(t, 256, 16)

    def body(x_ref, g_ref, mod_ref, h_ref):
        xv = x_ref[...]
        r = lax.rsqrt(jnp.mean(xv * xv, axis=-1, keepdims=True) + EPS)
        h = (xv * r) * g_ref[...] * (1.0 + mod_ref[1:2, :]) + mod_ref[0:1, :]
        h_ref[...] = h.astype(BF16)

    return pl.pallas_call(
        body, name="modulated_norm", grid=(t // tt,),
        out_shape=jax.ShapeDtypeStruct((t, d), BF16),
        in_specs=[pl.BlockSpec((tt, d), lambda i: (i, 0)), pl.BlockSpec((1, d), lambda i: (0, 0)),
                  pl.BlockSpec((3, d), lambda i: (0, 0))],
        out_specs=pl.BlockSpec((tt, d), lambda i: (i, 0)),
        compiler_params=_params(("arbitrary",)),
    )(x, norm_g, mod)


def _gather_proj(h, w_loc):
    t, d = h.shape
    ns = w_loc.shape[1]
    tm = _pick(t, 512, 16)
    nm = t // tm
    x, y, c = _place()
    idx = lambda p: 4 * p[0] + 2 * p[1] + p[2]
    order = [(x, y, c), (x, y, 1 - c)]
    for chip in _chips(x, y):
        order += [(*chip, c), (*chip, 1 - c)]
    order = jnp.stack([idx(p) for p in order]).astype(jnp.int32)

    def body(order_ref, a_ref, w_ref, wg_ref, o_ref, slab, send_sems, recv_sems, local_sems):
        j, m = pl.program_id(0), pl.program_id(1)
        x, y, c = _place()
        me, sibling = (x, y, c), (x, y, 1 - c)
        chips = _chips(x, y)

        def rows(p):
            return wg_ref.at[:, pl.ds(pl.multiple_of(idx(p) * ns, LANES), ns)]

        def copy(k, block, to, src=None):
            return pltpu.make_async_remote_copy(
                src_ref=rows(block) if src is None else src, dst_ref=rows(block),
                send_sem=send_sems.at[k], recv_sem=recv_sems.at[k], device_id=to, device_id_type=MESH)

        def load(src):
            cp = pltpu.make_async_copy(src, slab, local_sems.at[1])
            cp.start()
            cp.wait()

        keep = pltpu.make_async_copy(w_ref, rows(me), local_sems.at[0])
        first = [copy(0, me, sibling, src=w_ref)] + [copy(1 + k, me, (*chip, c), src=w_ref)
                                                     for k, chip in enumerate(chips)]
        passed = [copy(4 + k, (*chip, c), sibling) for k, chip in enumerate(chips)]

        @pl.when((j == 0) & (m == 0))
        def _():
            keep.start()
            for cp in first:
                cp.start()
            load(w_ref)

        @pl.when((j == 1) & (m == 0))
        def _():
            copy(0, sibling, me).wait_recv()
            load(rows(sibling))

        for k, chip in enumerate(chips):
            @pl.when((j == 2 + 2 * k) & (m == 0))
            def _(k=k, chip=chip):
                copy(1 + k, (*chip, c), me).wait_recv()
                passed[k].start()
                load(rows((*chip, c)))

            @pl.when((j == 3 + 2 * k) & (m == 0))
            def _(k=k, chip=chip):
                copy(4 + k, (*chip, 1 - c), me).wait_recv()
                load(rows((*chip, 1 - c)))

        o_ref[...] = _dot(a_ref[...], slab[...], NN)

        @pl.when((j == NDEV - 1) & (m == nm - 1))
        def _():
            for cp in first + passed:
                cp.wait_send()
            keep.wait()

    hbm = pl.BlockSpec(memory_space=pl.ANY)
    return pl.pallas_call(
        body, name="gather_proj",
        grid_spec=pltpu.PrefetchScalarGridSpec(
            num_scalar_prefetch=1, grid=(NDEV, nm),
            in_specs=[pl.BlockSpec((tm, d), lambda j, m, order_ref: (m, 0)), hbm],
            out_specs=[hbm, pl.BlockSpec((tm, ns), lambda j, m, order_ref: (m, order_ref[j]))],
            scratch_shapes=[pltpu.VMEM((d, ns), BF16), pltpu.SemaphoreType.DMA((7,)), pltpu.SemaphoreType.DMA((7,)),
                            pltpu.SemaphoreType.DMA((2,))]),
        out_shape=[jax.ShapeDtypeStruct((d, NDEV * ns), BF16), jax.ShapeDtypeStruct((t, NDEV * ns), F32)],
        compiler_params=_params(("arbitrary", "arbitrary")),
    )(order, h, w_loc)


def _attention_fwd(proj, qg, kg, nh, d_model, ride):
    t = proj.shape[0]
    tq = _pick(t, 256, 16)
    nq = t // tq
    assert nq <= LANES
    scale = HEAD_DIM ** -0.5

    def body(q_ref, k_ref, v_ref, g_ref, qg_ref, kg_ref, o_ref, tot_ref, y_ref, first_ref, qn, kn, vb):
        def norm(src, gain, dst):
            v = src[...]
            r = lax.rsqrt(jnp.mean(v * v, axis=-1, keepdims=True) + EPS)
            dst[...] = ((v * r) * gain[...]).astype(BF16)

        norm(q_ref, qg_ref, qn)
        norm(k_ref, kg_ref, kn)
        vb[...] = v_ref[...].astype(BF16)
        row = lax.broadcasted_iota(jnp.int32, (tq, tq), 0)
        col = lax.broadcasted_iota(jnp.int32, (tq, tq), 1)
        upper = (row > col).astype(BF16)
        causal = col < row

        def block(qi, j, carry, acc, masked):
            ks = pl.ds(pl.multiple_of(j * tq, tq), tq)
            z = _dot(qi, kn[ks, :], NT) * scale
            sp = _softplus(z)
            ls = -sp
            if masked:
                ls = jnp.where(causal, ls, 0.0)
            hi = ls.astype(BF16)
            lo = (ls - hi.astype(F32)).astype(BF16)
            after = _dot(hi, upper, NN) + _dot(lo, upper, NN)
            w = jnp.exp(z - sp + after + carry)
            if masked:
                w = jnp.where(causal, w, 0.0)
            acc = acc + _dot(w.astype(BF16), vb[ks, :], NN)
            carry = carry + jnp.sum(ls, axis=1, keepdims=True)
            return carry, acc

        lane = lax.broadcasted_iota(jnp.int32, (8, LANES), 1)

        def live(carry):
            return (jnp.max(carry) > DEAD_LOG_WEIGHT).astype(jnp.int32)

        def q_block(i, firsts):
            qs = pl.ds(pl.multiple_of(i * tq, tq), tq)
            qi = qn[qs, :]
            carry, acc = block(qi, i, jnp.zeros((tq, 1), F32), jnp.zeros((tq, HEAD_DIM), F32), True)

            def k_step(st):
                ca, ac = block(qi, i - 1 - st[0], st[1], st[2], False)
                return st[0] + 1, ca, ac, live(ca)

            done, carry, acc, _ = lax.while_loop(
                lambda st: (st[0] < i) & (st[3] > 0), k_step, (jnp.int32(0), carry, acc, live(carry)))
            o_ref[qs, :] = acc
            tot_ref[qs, :] = jnp.broadcast_to(carry, (tq, HEAD_DIM))
            y_ref[qs, :] = (acc * _silu(g_ref[qs, :])).astype(BF16)
            return jnp.where(lane == i, (i - done).astype(F32), firsts)

        first_ref[...] = lax.fori_loop(0, nq, q_block, jnp.zeros((8, LANES), F32))

    col_block = lambda off: pl.BlockSpec((t, HEAD_DIM), lambda h: (0, off + h))
    vec = pl.BlockSpec((1, HEAD_DIM), lambda h: (0, 0))
    return _call(
        body, name="attention_fwd", grid=(nh,),
        out_shape=[jax.ShapeDtypeStruct((t, nh * HEAD_DIM), F32), jax.ShapeDtypeStruct((t, nh * HEAD_DIM), F32),
                   jax.ShapeDtypeStruct((t, d_model), BF16), jax.ShapeDtypeStruct((nh, 8, LANES), F32)],
        in_specs=[col_block(0), col_block(nh), col_block(2 * nh), col_block(3 * nh), vec, vec],
        out_specs=[col_block(0), col_block(0), col_block(0), pl.BlockSpec((None, 8, LANES), lambda h: (h, 0, 0))],
        scratch_shapes=[pltpu.VMEM((t, HEAD_DIM), BF16)] * 3,
        args=(proj, proj, proj, proj, qg, kg), ride=ride)


def _conv_taps(buf, w_ref, kc, lo, n_rows, cw, first_tap_row):
    acc = None
    for j in range(kc):
        term = w_ref[j:j + 1, lo:lo + cw] * buf[pl.ds(first_tap_row(j), n_rows), pl.ds(lo, cw)]
        acc = term if acc is None else acc + term
    return acc


def _glu_rows(u_ref, g_ref):
    return u_ref[...] * _sigmoid(g_ref[...])


def _conv_fwd(proj, w_dw, b_dw, ln_g, ln_b, kc, cw, ride):
    t = proj.shape[0]
    tt = _pick(t, 128, HALO)
    per = tt // HALO
    chunk = _pick(cw, 256)

    def body(u_ref, g_ref, up_ref, gp_ref, w_ref, b_ref, lg_ref, lb_ref, a_ref, hc_ref, buf):
        i = pl.program_id(0)
        buf[pl.ds(HALO, tt), :] = _glu_rows(u_ref, g_ref)
        halo = _glu_rows(up_ref, gp_ref)
        buf[pl.ds(0, HALO), :] = jnp.where(i > 0, halo, 0.0)
        for lo in range(0, cw, chunk):
            conv = _conv_taps(buf, w_ref, kc, lo, tt, chunk, lambda j: HALO - (kc - 1) + j)
            hc_ref[:, lo:lo + chunk] = conv + b_ref[:, lo:lo + chunk]
        hc = hc_ref[...]
        mu = jnp.mean(hc, axis=-1, keepdims=True)
        xc = hc - mu
        var = jnp.mean(xc * xc, axis=-1, keepdims=True)
        ln = xc * lax.rsqrt(var + EPS) * lg_ref[...] + lb_ref[...]
        a_ref[...] = _silu(ln).astype(BF16)

    ncol = proj.shape[1] // cw
    tile = lambda g: pl.BlockSpec((tt, cw), lambda i: (i, g))
    prev = lambda g: pl.BlockSpec((HALO, cw), lambda i: (jnp.maximum(i * per - 1, 0), g))
    full = lambda r: pl.BlockSpec((r, cw), lambda i: (0, 0))
    return _call(
        body, name="conv_fwd", grid=(t // tt,),
        out_shape=[jax.ShapeDtypeStruct((t, cw), BF16), jax.ShapeDtypeStruct((t, cw), F32)],
        in_specs=[tile(ncol - 3), tile(ncol - 2), prev(ncol - 3), prev(ncol - 2),
                  full(w_dw.shape[0]), full(1), full(1), full(1)],
        out_specs=[pl.BlockSpec((tt, cw), lambda i: (i, 0))] * 2,
        scratch_shapes=[pltpu.VMEM((HALO + tt, cw), F32)],
        args=(proj, proj, proj, proj, w_dw, b_dw, ln_g, ln_b), ride=ride)


def _pointwise_fwd(a, wpw, b_pw, proj, ycat, cw):
    t = a.shape[0]
    tm = _pick(t, 256, 16)
    ncol = proj.shape[1] // cw
    ycol = ycat.shape[1] // cw - 1

    def body(a_ref, w_ref, b_ref, g_ref, y_in, z_ref, y_ref):
        z = _dot(a_ref[...], w_ref[...], NN) + b_ref[...]
        z_ref[...] = z
        y_ref[...] = (z * _silu(g_ref[...])).astype(BF16)

    return pl.pallas_call(
        body, name="pointwise_fwd", grid=(t // tm,),
        out_shape=[jax.ShapeDtypeStruct((t, cw), F32), jax.ShapeDtypeStruct(ycat.shape, ycat.dtype)],
        in_specs=[pl.BlockSpec((tm, cw), lambda m: (m, 0)),
                  pl.BlockSpec((cw, cw), lambda m: (0, 0)),
                  pl.BlockSpec((1, cw), lambda m: (0, 0)),
                  pl.BlockSpec((tm, cw), lambda m: (m, ncol - 1)),
                  pl.BlockSpec(memory_space=pl.ANY)],
        out_specs=[pl.BlockSpec((tm, cw), lambda m: (m, 0)), pl.BlockSpec((tm, cw), lambda m: (m, ycol))],
        input_output_aliases={4: 1},
        compiler_params=_params(("arbitrary",)),
    )(a, wpw, b_pw, proj, ycat)


def _out_matmul(ycat, wout, x, target, mod):
    t, d = x.shape
    kdim = wout.shape[0]
    tm, tn = _pick(t, 512, 16), _pick(d, 512)
    inv_d = 1.0 / d

    def body(a_ref, w_ref, x_ref, tg_ref, mod_ref, dout_ref, dy_ref, sums_ref):
        @pl.when(pl.program_id(1) == 0)
        def _():
            sums_ref[...] = jnp.zeros_like(sums_ref)

        y = _dot(a_ref[...], w_ref[...], NN)
        gate = mod_ref[2:3, :]
        err = (x_ref[...] + gate * y) - tg_ref[...]
        dout = err * inv_d
        dout_ref[...] = dout
        dy_ref[...] = (dout * gate).astype(BF16)
        sums_ref[0:1, :] += jnp.sum(dout * y, axis=0, keepdims=True)
        sums_ref[1:2, :] += jnp.sum(err * err, axis=0, keepdims=True)

    mn = lambda n, m: (m, n)
    return pl.pallas_call(
        body, name="out_matmul", grid=(d // tn, t // tm),
        out_shape=[jax.ShapeDtypeStruct((t, d), F32), jax.ShapeDtypeStruct((t, d), BF16),
                   jax.ShapeDtypeStruct((8, d), F32)],
        in_specs=[pl.BlockSpec((tm, kdim), lambda n, m: (m, 0)),
                  pl.BlockSpec((kdim, tn), lambda n, m: (0, n)),
                  pl.BlockSpec((tm, tn), mn), pl.BlockSpec((tm, tn), mn),
                  pl.BlockSpec((3, tn), lambda n, m: (0, n))],
        out_specs=[pl.BlockSpec((tm, tn), mn), pl.BlockSpec((tm, tn), mn),
                   pl.BlockSpec((8, tn), lambda n, m: (0, n))],
        compiler_params=_params(("arbitrary", "arbitrary")),
    )(ycat, wout, x, target, mod)


def _mm(a, b, form, name, tm, tn, out_dtype, *, slabs=False, n_outer=False, ksplit=1, ride=None):
    if form == TN:
        kdim, m_dim = a.shape
    else:
        m_dim, kdim = a.shape
    n_dim = b.shape[0] if form == NT else b.shape[1]
    tk = kdim // ksplit
    gm, gn = m_dim // tm, n_dim // tn
    mn = (lambda g: (g[1], g[0])) if n_outer else (lambda g: (g[0], g[1]))
    a_map = (lambda *g: (g[2], mn(g)[0])) if form == TN else (lambda *g: (mn(g)[0], g[2]))
    b_map = (lambda *g: (mn(g)[1], g[2])) if form == NT else (lambda *g: (g[2], mn(g)[1]))
    a_blk = (tk, tm) if form == TN else (tm, tk)
    b_blk = (tn, tk) if form == NT else (tk, tn)
    if slabs:
        out_shape = jax.ShapeDtypeStruct((gn, m_dim, tn), out_dtype)
        out_spec = pl.BlockSpec((None, tm, tn), lambda *g: (mn(g)[1], mn(g)[0], 0))
    else:
        out_shape = jax.ShapeDtypeStruct((m_dim, n_dim), out_dtype)
        out_spec = pl.BlockSpec((tm, tn), lambda *g: mn(g))

    def body(a_ref, b_ref, o_ref, *acc):
        part = _dot(a_ref[...], b_ref[...], form)
        if ksplit == 1:
            o_ref[...] = part.astype(out_dtype)
            return
        k = pl.program_id(2)

        @pl.when(k == 0)
        def _():
            acc[0][...] = part

        @pl.when((k > 0) & (k < ksplit - 1))
        def _():
            acc[0][...] += part

        @pl.when(k == ksplit - 1)
        def _():
            o_ref[...] = (acc[0][...] + part).astype(out_dtype)

    return _call(
        body, name=name, grid=((gn, gm) if n_outer else (gm, gn)) + (ksplit,),
        out_shape=[out_shape], in_specs=[pl.BlockSpec(a_blk, a_map), pl.BlockSpec(b_blk, b_map)],
        out_specs=[out_spec], scratch_shapes=[pltpu.VMEM((tm, tn), F32)] if ksplit > 1 else [],
        args=(a, b), ride=ride)


def _attention_bwd(proj, o, tot, firsts, dycat, qg, kg, nh):
    t, in_cols = proj.shape
    tq = _pick(t, 256, 16)
    nq = t // tq
    scale = HEAD_DIM ** -0.5

    def body(q_ref, k_ref, v_ref, g_ref, o_ref, tot_ref, first_ref, dy_ref, qg_ref, kg_ref, dproj_ref, gains_ref,
             qn, kn, vb, dob, dk_acc, dv_acc, dq_acc, outs, sems):
        h = pl.program_id(0)

        def norm(src, gain, dst):
            v = src[...]
            r = lax.rsqrt(jnp.mean(v * v, axis=-1, keepdims=True) + EPS)
            dst[...] = ((v * r) * gain[...]).astype(BF16)

        norm(q_ref, qg_ref, qn)
        norm(k_ref, kg_ref, kn)
        vb[...] = v_ref[...].astype(BF16)
        gs = g_ref[...]
        dyv = dy_ref[...]
        dob[...] = (dyv * _silu(gs)).astype(BF16)
        outs[3] = (dyv * o_ref[...] * _dsilu(gs)).astype(BF16)
        dk_acc[...] = jnp.zeros_like(dk_acc)
        dv_acc[...] = jnp.zeros_like(dv_acc)

        row = lax.broadcasted_iota(jnp.int32, (tq, tq), 0)
        col = lax.broadcasted_iota(jnp.int32, (tq, tq), 1)
        incl = (col <= row).astype(BF16)
        excl = (col < row).astype(BF16)
        causal = row < col
        lane = lax.broadcasted_iota(jnp.int32, (1, LANES), 1)

        def block(i, j, qi, doi, tot_row, p_left, g_left, dq, masked):
            ks = pl.ds(pl.multiple_of(j * tq, tq), tq)
            kj = kn[ks, :]
            z = _dot(kj, qi, NT) * scale
            sp = _softplus(z)
            ls = -sp
            if masked:
                ls = jnp.where(causal, ls, 0.0)
            hi = ls.astype(BF16)
            lo = (ls - hi.astype(F32)).astype(BF16)
            p_inc = _dot(incl, hi, NN) + _dot(incl, lo, NN) + p_left
            beta = jnp.exp(z - sp)
            w = beta * jnp.exp(tot_row - p_inc)
            if masked:
                w = jnp.where(causal, w, 0.0)
            dw = _dot(vb[ks, :], doi, NT)
            g = w * dw
            g_before = _dot(excl, g.astype(BF16), NN) + g_left
            dz = g * (1.0 - beta) - beta * g_before
            if masked:
                dz = jnp.where(causal, dz, 0.0)
            dzb = dz.astype(BF16)
            dv_acc[ks, :] += _dot(w.astype(BF16), doi, NN)
            dk_acc[ks, :] += _dot(dzb, qi, NN)
            dq = dq + _dot(dzb, kj, TN)
            p_left = p_left + jnp.sum(ls, axis=0, keepdims=True)
            g_left = g_left + jnp.sum(g, axis=0, keepdims=True)
            return p_left, g_left, dq

        def q_block(i, _):
            qs = pl.ds(pl.multiple_of(i * tq, tq), tq)
            qi = qn[qs, :]
            doi = dob[qs, :]
            tot_row = jnp.transpose(tot_ref[qs, :])[0:1, :]
            zero_row = jnp.zeros((1, tq), F32)

            def k_step(j, carry):
                return block(i, j, qi, doi, tot_row, carry[0], carry[1], carry[2], False)

            first = jnp.sum(jnp.where(lane == i, first_ref[0:1, :], 0.0)).astype(jnp.int32)
            first = jnp.clip(first, 0, i)
            carry = lax.fori_loop(first, i, k_step, (zero_row, zero_row, jnp.zeros((tq, HEAD_DIM), F32)))
            _, _, dq = block(i, i, qi, doi, tot_row, carry[0], carry[1], carry[2], True)
            dq_acc[qs, :] = dq * scale
            return 0

        lax.fori_loop(0, nq, q_block, 0)

        def norm_bwd(src, gain, dn, slot, gain_row):
            v = src[...]
            r = lax.rsqrt(jnp.mean(v * v, axis=-1, keepdims=True) + EPS)
            vhat = v * r
            gains_ref[gain_row:gain_row + 1, :] = jnp.sum(dn * vhat, axis=0, keepdims=True)
            dhat = dn * gain[...]
            outs[slot] = (r * (dhat - vhat * jnp.mean(dhat * vhat, axis=-1, keepdims=True))).astype(BF16)

        gains_ref[...] = jnp.zeros_like(gains_ref)
        norm_bwd(q_ref, qg_ref, dq_acc[...], 0, 0)
        norm_bwd(k_ref, kg_ref, dk_acc[...] * scale, 1, 1)
        outs[2] = dv_acc[...].astype(BF16)
        copies = [pltpu.make_async_copy(
            outs.at[s], dproj_ref.at[:, pl.ds(pl.multiple_of((s * nh + h) * HEAD_DIM, HEAD_DIM), HEAD_DIM)], sems.at[s])
            for s in range(4)]
        for cp in copies:
            cp.start()
        for cp in copies:
            cp.wait()

    col_block = lambda off: pl.BlockSpec((t, HEAD_DIM), lambda h: (0, off + h))
    vec = pl.BlockSpec((1, HEAD_DIM), lambda h: (0, 0))
    head_scr = lambda dt: pltpu.VMEM((t, HEAD_DIM), dt)
    return pl.pallas_call(
        body, name="attention_bwd", grid=(nh,),
        out_shape=[jax.ShapeDtypeStruct((t, in_cols), BF16), jax.ShapeDtypeStruct((nh, 8, HEAD_DIM), F32)],
        in_specs=[col_block(0), col_block(nh), col_block(2 * nh), col_block(3 * nh),
                  col_block(0), col_block(0), pl.BlockSpec((None, 8, LANES), lambda h: (h, 0, 0)), col_block(0), vec, vec],
        out_specs=[pl.BlockSpec(memory_space=pl.ANY), pl.BlockSpec((None, 8, HEAD_DIM), lambda h: (h, 0, 0))],
        scratch_shapes=[head_scr(BF16), head_scr(BF16), head_scr(BF16), head_scr(BF16),
                        head_scr(F32), head_scr(F32), head_scr(F32),
                        pltpu.VMEM((4, t, HEAD_DIM), BF16), pltpu.SemaphoreType.DMA((4,))],
        compiler_params=_params(("arbitrary",)),
    )(proj, proj, proj, proj, o, tot, firsts, dycat, qg, kg)


def _gate_bwd(dycat, z, proj, dproj, cw):
    t = z.shape[0]
    tt = _pick(t, 256, 16)
    ncol = proj.shape[1] // cw
    ycol = dycat.shape[1] // cw - 1

    def body(dy_ref, z_ref, g_ref, dp_in, dz_ref, dp_ref, sums_ref):
        i = pl.program_id(0)

        @pl.when(i == 0)
        def _():
            sums_ref[...] = jnp.zeros_like(sums_ref)

        g = g_ref[...]
        dy = dy_ref[...]
        dz = dy * _silu(g)
        dz_ref[...] = dz.astype(BF16)
        dp_ref[...] = (dy * z_ref[...] * _dsilu(g)).astype(BF16)
        sums_ref[0:1, :] += jnp.sum(dz, axis=0, keepdims=True)

    return pl.pallas_call(
        body, name="gate_bwd", grid=(t // tt,),
        out_shape=[jax.ShapeDtypeStruct((t, cw), BF16), jax.ShapeDtypeStruct(dproj.shape, dproj.dtype),
                   jax.ShapeDtypeStruct((8, cw), F32)],
        in_specs=[pl.BlockSpec((tt, cw), lambda i: (i, ycol)), pl.BlockSpec((tt, cw), lambda i: (i, 0)),
                  pl.BlockSpec((tt, cw), lambda i: (i, ncol - 1)), pl.BlockSpec(memory_space=pl.ANY)],
        out_specs=[pl.BlockSpec((tt, cw), lambda i: (i, 0)), pl.BlockSpec((tt, cw), lambda i: (i, ncol - 1)),
                   pl.BlockSpec((8, cw), lambda i: (0, 0))],
        input_output_aliases={3: 1},
        compiler_params=_params(("arbitrary",)),
    )(dycat, z, proj, dproj)


def _ln_bwd(da, hc, ln_g, ln_b):
    t, cw = hc.shape
    tt = _pick(t, 256, 16)

    def body(da_ref, hc_ref, lg_ref, lb_ref, dh_ref, sums_ref):
        i = pl.program_id(0)

        @pl.when(i == 0)
        def _():
            sums_ref[...] = jnp.zeros_like(sums_ref)

        hcv = hc_ref[...]
        mu = jnp.mean(hcv, axis=-1, keepdims=True)
        xc = hcv - mu
        r = lax.rsqrt(jnp.mean(xc * xc, axis=-1, keepdims=True) + EPS)
        xhat = xc * r
        ln = xhat * lg_ref[...] + lb_ref[...]
        dln = da_ref[...] * _dsilu(ln)
        dxhat = dln * lg_ref[...]
        dhc = r * (dxhat - jnp.mean(dxhat, axis=-1, keepdims=True)
                   - xhat * jnp.mean(dxhat * xhat, axis=-1, keepdims=True))
        dh_ref[...] = dhc
        sums_ref[0:1, :] += jnp.sum(dln * xhat, axis=0, keepdims=True)
        sums_ref[1:2, :] += jnp.sum(dln, axis=0, keepdims=True)
        sums_ref[2:3, :] += jnp.sum(dhc, axis=0, keepdims=True)

    tile = pl.BlockSpec((tt, cw), lambda i: (i, 0))
    vec = pl.BlockSpec((1, cw), lambda i: (0, 0))
    return pl.pallas_call(
        body, name="ln_bwd", grid=(t // tt,),
        out_shape=[jax.ShapeDtypeStruct((t, cw), F32), jax.ShapeDtypeStruct((8, cw), F32)],
        in_specs=[tile, tile, vec, vec],
        out_specs=[tile, pl.BlockSpec((8, cw), lambda i: (0, 0))],
        compiler_params=_params(("arbitrary",)),
    )(da, hc, ln_g, ln_b)


def _conv_bwd(dhc, proj, w_dw, dproj, kc, cw):
    t = proj.shape[0]
    tt = _pick(t, 128, HALO)
    per = tt // HALO
    nt = t // tt
    chunk = _pick(cw, 256)
    ncol = proj.shape[1] // cw
    wr = w_dw.shape[0]

    def body(d_ref, dn_ref, u_ref, g_ref, up_ref, gp_ref, w_ref, dp_in, dp_ref, dw_ref, dbuf, hbuf):
        i = pl.program_id(0)

        @pl.when(i == 0)
        def _():
            dw_ref[...] = jnp.zeros_like(dw_ref)

        dbuf[pl.ds(0, tt), :] = d_ref[...]
        dbuf[pl.ds(tt, HALO), :] = jnp.where(i < nt - 1, dn_ref[...], 0.0)
        hbuf[pl.ds(HALO, tt), :] = _glu_rows(u_ref, g_ref)
        hbuf[pl.ds(0, HALO), :] = jnp.where(i > 0, _glu_rows(up_ref, gp_ref), 0.0)
        for lo in range(0, cw, chunk):
            dhg = _conv_taps(dbuf, w_ref, kc, lo, tt, chunk, lambda j: (kc - 1) - j)
            u = u_ref[:, lo:lo + chunk]
            sg = _sigmoid(g_ref[:, lo:lo + chunk])
            dp_ref[:, lo:lo + chunk] = (dhg * sg).astype(BF16)
            dp_ref[:, cw + lo:cw + lo + chunk] = (dhg * u * sg * (1.0 - sg)).astype(BF16)
            dtile = d_ref[:, lo:lo + chunk]
            for j in range(kc):
                prod = dtile * hbuf[pl.ds(HALO - (kc - 1) + j, tt), pl.ds(lo, chunk)]
                dw_ref[j:j + 1, lo:lo + chunk] += jnp.sum(prod, axis=0, keepdims=True)

    tile = lambda g: pl.BlockSpec((tt, cw), lambda i: (i, g))
    prev = lambda g: pl.BlockSpec((HALO, cw), lambda i: (jnp.maximum(i * per - 1, 0), g))
    return pl.pallas_call(
        body, name="conv_bwd", grid=(nt,),
        out_shape=[jax.ShapeDtypeStruct(dproj.shape, dproj.dtype), jax.ShapeDtypeStruct((wr, cw), F32)],
        in_specs=[pl.BlockSpec((tt, cw), lambda i: (i, 0)),
                  pl.BlockSpec((HALO, cw), lambda i: (jnp.minimum((i + 1) * per, nt * per - 1), 0)),
                  tile(ncol - 3), tile(ncol - 2), prev(ncol - 3), prev(ncol - 2),
                  pl.BlockSpec((wr, cw), lambda i: (0, 0)), pl.BlockSpec(memory_space=pl.ANY)],
        out_specs=[pl.BlockSpec((tt, 2 * cw), lambda i: (i, (ncol - 3) // 2)),
                   pl.BlockSpec((wr, cw), lambda i: (0, 0))],
        scratch_shapes=[pltpu.VMEM((tt + HALO, cw), F32), pltpu.VMEM((HALO + tt, cw), F32)],
        input_output_aliases={7: 0},
        compiler_params=_params(("arbitrary",)),
    )(dhc, dhc, proj, proj, proj, proj, w_dw, dproj)


def _input_grad(dh, x, dout, norm_g, mod):
    t, d = x.shape
    tt = _pick(t, 128, 16)

    def body(dh_ref, x_ref, do_ref, g_ref, mod_ref, gx_ref, sums_ref):
        i = pl.program_id(0)

        @pl.when(i == 0)
        def _():
            sums_ref[...] = jnp.zeros_like(sums_ref)

        xv = x_ref[...]
        dhv = dh_ref[...]
        r = lax.rsqrt(jnp.mean(xv * xv, axis=-1, keepdims=True) + EPS)
        xn = xv * r
        g = g_ref[...]
        one_scale = 1.0 + mod_ref[1:2, :]
        dxn = dhv * g * one_scale
        gx_ref[...] = do_ref[...] + r * (dxn - xn * jnp.mean(dxn * xn, axis=-1, keepdims=True))
        sums_ref[0:1, :] += jnp.sum(dhv, axis=0, keepdims=True)
        sums_ref[1:2, :] += jnp.sum(dhv * (xn * g), axis=0, keepdims=True)
        sums_ref[2:3, :] += jnp.sum(dhv * one_scale * xn, axis=0, keepdims=True)

    tile = pl.BlockSpec((tt, d), lambda i: (i, 0))
    return pl.pallas_call(
        body, name="input_grad", grid=(t // tt,),
        out_shape=[jax.ShapeDtypeStruct((t, d), F32), jax.ShapeDtypeStruct((8, d), F32)],
        in_specs=[tile, tile, tile, pl.BlockSpec((1, d), lambda i: (0, 0)), pl.BlockSpec((3, d), lambda i: (0, 0))],
        out_specs=[tile, pl.BlockSpec((8, d), lambda i: (0, 0))],
        compiler_params=_params(("arbitrary",)),
    )(dh, x, dout, norm_g, mod)


def _sum_adam(parts, w, m, v, name):
    r, c = w.shape
    n_parts = parts.shape[0]
    tr = _pick(r, 128, 16) if r % 16 == 0 else r
    tc = _pick(c, 2048)

    def body(p_ref, w_ref, m_ref, v_ref, g_ref, d_ref, nm_ref, nv_ref):
        g = p_ref[0].astype(F32)
        for i in range(1, n_parts):
            g = g + p_ref[i].astype(F32)
        d, nm, nv = _adam(w_ref[...], g, m_ref[...], v_ref[...])
        g_ref[...] = g
        d_ref[...] = d
        nm_ref[...] = nm
        nv_ref[...] = nv

    tile = pl.BlockSpec((tr, tc), lambda i, j: (i, j))
    out = jax.ShapeDtypeStruct((r, c), F32)
    return pl.pallas_call(
        body, name=name, grid=(r // tr, c // tc),
        out_shape=[out] * 4,
        in_specs=[pl.BlockSpec((n_parts, tr, tc), lambda i, j: (0, i, j)), tile, tile, tile],
        out_specs=[tile] * 4,
        compiler_params=_params(("arbitrary", "arbitrary")),
    )(parts, w, m, v)


def _ada_grad_adam(s_t, dm, w, m, v):
    d, n = w.shape
    tr = _pick(d, 256, 16)

    def body(s_ref, dm_ref, w_ref, m_ref, v_ref, g_ref, d_ref, nm_ref, nv_ref):
        g = lax.dot_general(s_ref[...], dm_ref[...], (NN, ((), ())), preferred_element_type=F32,
                            precision=lax.Precision.HIGHEST)
        dl, nm, nv = _adam(w_ref[...], g, m_ref[...], v_ref[...])
        g_ref[...] = g
        d_ref[...] = dl
        nm_ref[...] = nm
        nv_ref[...] = nv

    tile = pl.BlockSpec((tr, n), lambda i: (i, 0))
    out = jax.ShapeDtypeStruct((d, n), F32)
    return pl.pallas_call(
        body, name="ada_grad_adam", grid=(d // tr,),
        out_shape=[out] * 4,
        in_specs=[pl.BlockSpec((tr, NDEV), lambda i: (i, 0)), pl.BlockSpec((NDEV, n), lambda i: (0, 0)),
                  tile, tile, tile],
        out_specs=[tile] * 4,
        compiler_params=_params(("arbitrary",)),
    )(s_t, dm, w, m, v)


def _silu_t(c_all):
    n, d = c_all.shape

    def body(c_ref, o_ref):
        o_ref[...] = jnp.transpose(_silu(c_ref[...]))

    return pl.pallas_call(
        body, name="silu_t", out_shape=jax.ShapeDtypeStruct((d, n), F32),
        in_specs=[pl.BlockSpec(memory_space=pltpu.VMEM)], out_specs=pl.BlockSpec(memory_space=pltpu.VMEM),
        compiler_params=pltpu.CompilerParams(vmem_limit_bytes=VMEM_LIMIT),
    )(c_all)


def _rows128(v):
    return v.reshape(-1, LANES)


def _pad_rows(a, rows):
    return jnp.pad(a, ((0, rows - a.shape[0]), (0, 0)))


def kernel(x, c, norm_g, w_ada, b_ada, w_in, q_norm_g, k_norm_g, w_dw, b_dw, ln_g, ln_b, w_pw, b_pw, w_out, loss_target, m_norm_g, m_w_ada, m_b_ada, m_w_in, m_q_norm_g, m_k_norm_g, m_w_dw, m_b_dw, m_ln_g, m_ln_b, m_w_pw, m_b_pw, m_w_out, v_norm_g, v_w_ada, v_b_ada, v_w_in, v_q_norm_g, v_k_norm_g, v_w_dw, v_b_dw, v_ln_g, v_ln_b, v_w_pw, v_b_pw, v_w_out):
    _, t, d = x.shape
    n_ada = w_ada.shape[2]
    ns = w_in.shape[2]
    kc, cwl = w_dw.shape[1], w_dw.shape[2]
    cw = cwl * NDEV
    sb = d - cw
    nh = sb // HEAD_DIM
    assert sb == cw and kc - 1 <= HALO and NDEV * ns == 4 * sb + 3 * cw
    my = 4 * lax.axis_index("x") + 2 * lax.axis_index("y") + lax.axis_index("c")

    x2, tg2 = x[0], loss_target[0]

    wdw_rows = -(-kc // 8) * 8
    wdw_pad = _pad_rows(w_dw[0], wdw_rows)
    pay1 = jnp.concatenate([_rows128(c[0]), _rows128(wdw_pad.reshape(-1))], axis=0)
    (g1,) = _all_gather([pay1], "gather_cond", pltpu.VMEM)
    c_rows = d // LANES
    c_all = g1[:, :c_rows].reshape(NDEV, d)
    wdw_all = g1[:, c_rows:].reshape(NDEV, wdw_rows, cwl).transpose(1, 0, 2).reshape(wdw_rows, cw)

    b_ada_loc = lax.dynamic_slice(b_ada, (0, my * n_ada), (1, n_ada))
    mod_cols = _ada_matmul(c_all, w_ada[0], b_ada_loc)
    (g2,) = _all_gather([mod_cols], "gather_mod", pltpu.VMEM)
    mod_mine = lax.dynamic_index_in_dim(g2, my, axis=1, keepdims=False)
    mod = mod_mine.reshape(3, d)

    core = lax.axis_index("c").astype(jnp.int32).reshape(1)
    h = _modulated_norm(x2, norm_g, mod)
    wfull_in, proj = _gather_proj(h, w_in[0].astype(BF16))
    (o, tot, ycat, firsts), partly = _attention_fwd(
        proj, q_norm_g, k_norm_g, nh, d, _gather_ride([w_out[0].astype(BF16), w_pw[0].astype(BF16)]))
    wg_out, wg_pw = _gather_finish(partly)
    wfull_out, wfull_pw = wg_out.reshape(d, d), wg_pw.reshape(cw, cw)
    (a, hc), _ = _conv_fwd(proj, wdw_all, b_dw, ln_g, ln_b, kc, cw, None)
    z, ycat = _pointwise_fwd(a, wfull_pw, b_pw, proj, ycat, cw)
    dout, dy, out_sums = _out_matmul(ycat, wfull_out, x2, tg2, mod)

    tile = _pick(t, 512, 16)
    (dycat,), _ = _mm(dy, wfull_out, NT, "dycat_matmul", tile, _pick(d, 512), F32)
    (p_wout,), _ = _mm(ycat, dy, TN, "w_out_grad", _pick(d, 512), _pick(d, 1024), BF16)
    p_wout = p_wout.reshape(NDEV, d // NDEV, d)
    dproj, gains = _attention_bwd(proj, o, tot, firsts, dycat, q_norm_g, k_norm_g, nh)
    dz, dproj, dz_sums = _gate_bwd(dycat, z, proj, dproj, cw)
    (da,), _ = _mm(dz, wfull_pw, NT, "da_matmul", tile, cw, F32)
    (p_wpw,), _ = _mm(a, dz, TN, "w_pw_grad", _pick(cw, 512), _pick(cw, 1024), BF16)
    p_wpw = p_wpw.reshape(NDEV, cw // NDEV, cw)
    dhc, ln_sums = _ln_bwd(da, hc, ln_g, ln_b)
    dproj, dwdw = _conv_bwd(dhc, proj, wdw_all, dproj, kc, cw)
    p_wdw = dwdw.reshape(wdw_rows, NDEV, cwl).transpose(1, 0, 2).astype(BF16)

    def chip_sums(parts, name):
        split = [p.reshape(4, 2, *p.shape[1:]) for p in parts]
        theirs = _sibling_exchange(split, name + "_sibling")
        return [_pair_sum(m, s, core, f"{name}_pair_sum_{i}") for i, (m, s) in enumerate(zip(split, theirs))]

    q_small = chip_sums([p_wout, p_wpw, p_wdw], "small_grads")
    (p_win,), (r_wout, r_wpw, r_wdw) = _mm(h, dproj, TN, "w_in_grad", _pick(d, 256), ns, BF16, slabs=True,
                                           n_outer=True, ride=_chip_exchange_ride(q_small))
    q_win = chip_sums([p_win], "w_in_grad")
    (dh,), (r_win,) = _mm(dproj, wfull_in, NT, "dh_matmul", tile, _pick(d, 512), F32, ksplit=2,
                          ride=_chip_exchange_ride(q_win))
    grad_x, in_sums = _input_grad(dh, x2, dout, norm_g, mod)

    dmod = jnp.concatenate([in_sums[0], in_sums[1], out_sums[0]])
    loss_part = 0.5 / d * jnp.sum(out_sums[1].reshape(-1, LANES), axis=0)
    small = [in_sums[2], dmod, jnp.sum(gains[:, 0], axis=0), jnp.sum(gains[:, 1], axis=0),
             ln_sums[2], ln_sums[0], ln_sums[1], dz_sums[0], loss_part]
    sizes = [s.shape[0] for s in small]
    packed = _rows128(jnp.concatenate(small))
    n_rows = -(-packed.shape[0] // 8) * 8
    (g3,) = _all_gather([_pad_rows(packed, n_rows)], "gather_small", pltpu.VMEM)

    def pack_state(names_vals):
        flat = jnp.concatenate([v.reshape(-1) for v in names_vals] + [jnp.zeros((LANES,), F32)])
        return _pad_rows(_rows128(flat), n_rows)

    small_w = pack_state([norm_g, b_ada, q_norm_g, k_norm_g, b_dw, ln_g, ln_b, b_pw])
    small_m = pack_state([m_norm_g, m_b_ada, m_q_norm_g, m_k_norm_g, m_b_dw, m_ln_g, m_ln_b, m_b_pw])
    small_v = pack_state([v_norm_g, v_b_ada, v_q_norm_g, v_k_norm_g, v_b_dw, v_ln_g, v_ln_b, v_b_pw])
    sg, sd, sm, sv = _sum_adam(g3, small_w, small_m, small_v, "small_adam")

    def unpack(p):
        flat = p.reshape(-1)
        outs, off = [], 0
        for n in sizes[:-1]:
            outs.append(flat[off:off + n].reshape(1, n))
            off += n
        return outs, flat[off:off + LANES]

    g_small, loss_lanes = unpack(sg)
    d_small, _ = unpack(sd)
    m_small, _ = unpack(sm)
    v_small, _ = unpack(sv)
    loss = jnp.sum(loss_lanes)

    off = sizes[0]
    dmod_all = g3.reshape(NDEV, -1)[:, off:off + 3 * d]
    dmod_loc = lax.dynamic_slice(dmod_all, (0, my * n_ada), (NDEV, n_ada))
    ada = _ada_grad_adam(_silu_t(c_all), dmod_loc, w_ada[0], m_w_ada[0], v_w_ada[0])
    win = _sum_adam(r_win, w_in[0], m_w_in[0], v_w_in[0], "w_in_adam")
    wout = _sum_adam(r_wout, w_out[0], m_w_out[0], v_w_out[0], "w_out_adam")
    wpw = _sum_adam(r_wpw, w_pw[0], m_w_pw[0], v_w_pw[0], "w_pw_adam")
    wdw_state = [_pad_rows(s[0], wdw_rows) for s in (w_dw, m_w_dw, v_w_dw)]
    wdw = [r[:kc] for r in _sum_adam(r_wdw, *wdw_state, "w_dw_adam")]

    def group(k, small_list):
        s = small_list
        return [s[0], ada[k][None], s[1], win[k][None], s[2], s[3], wdw[k][None], s[4], s[5], s[6],
                wpw[k][None], s[7], wout[k][None]]

    return (loss, grad_x[None], *group(0, g_small), *group(1, d_small), *group(2, m_small), *group(3, v_small))
```

```python
import functools

import jax
import jax.numpy as jnp
from jax import lax
from jax.experimental import pallas as pl
from jax.experimental.pallas import tpu as pltpu

F32 = jnp.float32
BF16 = jnp.bfloat16
NDEV = 8
HEAD_DIM = 128
LANES = 128
HALO = 32
EPS = 1e-6
DEAD_LOG_WEIGHT = -104.0
VMEM_LIMIT = 56 * 1024 * 1024
MESH = pl.DeviceIdType.MESH

ADAM_LR = 0.001
ADAM_B1 = 0.9
ADAM_B2 = 0.999
ADAM_EPS = 1e-08
ADAM_WD = 0.01
ADAM_STEP = 10


def _params(sem=None):
    return pltpu.CompilerParams(dimension_semantics=sem, vmem_limit_bytes=VMEM_LIMIT)


def _pick(n, pref, unit=LANES):
    best = None
    for d in range(unit, min(n, pref) + 1, unit):
        if n % d == 0:
            best = d
    return best if best is not None else n


def _sigmoid(z):
    return 1.0 / (1.0 + jnp.exp(-z))


def _silu(z):
    return z * _sigmoid(z)


def _dsilu(z):
    s = _sigmoid(z)
    return s * (1.0 + z * (1.0 - s))


def _softplus(z):
    return jnp.maximum(z, 0.0) + jnp.log(1.0 + jnp.exp(-jnp.abs(z)))


def _dot(a, b, dims):
    return lax.dot_general(a, b, (dims, ((), ())), preferred_element_type=F32)


NN = ((1,), (0,))
NT = ((1,), (1,))
TN = ((0,), (0,))


def _adam(w, g, m, v):
    m = ADAM_B1 * m + (1.0 - ADAM_B1) * g
    v = ADAM_B2 * v + (1.0 - ADAM_B2) * (g * g)
    m_hat = m / (1.0 - ADAM_B1 ** ADAM_STEP)
    v_hat = v / (1.0 - ADAM_B2 ** ADAM_STEP)
    delta = -ADAM_LR * (m_hat / (jnp.sqrt(v_hat) + ADAM_EPS) + ADAM_WD * w)
    return delta, m, v


def _place():
    x, y, c = lax.axis_index("x"), lax.axis_index("y"), lax.axis_index("c")
    return x, y, c


def _flip(v, bit):
    return 1 - v if bit else v


def _all_gather(arrs, name, space):
    n = len(arrs)

    def body(*refs):
        ins, outs = refs[:n], refs[n:2 * n]
        send_sems, recv_sems, local_sems = refs[2 * n:]
        x, y, c = _place()
        me, sibling = (x, y, c), (x, y, 1 - c)
        chips = [(1 - x, y), (x, 1 - y), (1 - x, 1 - y)]

        def rows(a, p):
            return outs[a].at[4 * p[0] + 2 * p[1] + p[2]]

        def copy(a, k, block, to, src=None):
            return pltpu.make_async_remote_copy(
                src_ref=rows(a, block) if src is None else src, dst_ref=rows(a, block),
                send_sem=send_sems.at[7 * a + k], recv_sem=recv_sems.at[7 * a + k],
                device_id=to, device_id_type=MESH)

        mine = [pltpu.make_async_copy(ins[a], rows(a, me), local_sems.at[a]) for a in range(n)]
        for cp in mine:
            cp.start()
        first = []
        for a in range(n):
            first.append(copy(a, 0, me, sibling, src=ins[a]))
            first += [copy(a, 1 + j, me, (*chip, c), src=ins[a]) for j, chip in enumerate(chips)]
        for cp in first:
            cp.start()
        passed = []
        for j, chip in enumerate(chips):
            for a in range(n):
                copy(a, 1 + j, (*chip, c), me).wait_recv()
                fwd = copy(a, 4 + j, (*chip, c), sibling)
                fwd.start()
                passed.append(fwd)
        for a in range(n):
            copy(a, 0, sibling, me).wait_recv()
            for j, chip in enumerate(chips):
                copy(a, 4 + j, (*chip, 1 - c), me).wait_recv()
        for cp in first + passed:
            cp.wait_send()
        for cp in mine:
            cp.wait()

    spec = pl.BlockSpec(memory_space=space)
    return pl.pallas_call(
        body, name=name,
        out_shape=[jax.ShapeDtypeStruct((NDEV,) + a.shape, a.dtype) for a in arrs],
        in_specs=[spec] * n, out_specs=[spec] * n,
        scratch_shapes=[pltpu.SemaphoreType.DMA((7 * n,)), pltpu.SemaphoreType.DMA((7 * n,)),
                        pltpu.SemaphoreType.DMA((n,))],
        compiler_params=pltpu.CompilerParams(vmem_limit_bytes=VMEM_LIMIT),
    )(*arrs)


class _Ride:
    def __init__(self, ins, out_shapes, n_sems, start, finish):
        self.ins, self.out_shapes, self.n_sems, self.start, self.finish = ins, out_shapes, n_sems, start, finish


def _call(body, *, name, grid, out_shape, in_specs, out_specs, args, scratch_shapes=(), aliases=None, ride=None):
    sem = ("arbitrary",) * len(grid)
    if ride is None:
        return pl.pallas_call(
            body, name=name, grid=grid, out_shape=out_shape, in_specs=in_specs, out_specs=out_specs,
            scratch_shapes=list(scratch_shapes), input_output_aliases=aliases or {},
            compiler_params=_params(sem))(*args), []
    n_in, n_out, n_scr = len(in_specs), len(out_specs), len(scratch_shapes)
    r_in, r_out = len(ride.ins), len(ride.out_shapes)

    def carried(*refs):
        ins, rins = refs[:n_in], refs[n_in:n_in + r_in]
        o0 = n_in + r_in
        outs, routs = refs[o0:o0 + n_out], refs[o0 + n_out:o0 + n_out + r_out]
        scratch = refs[o0 + n_out + r_out:o0 + n_out + r_out + n_scr]
        sems = refs[o0 + n_out + r_out + n_scr:]
        first = functools.reduce(lambda a, b: a & b, [pl.program_id(i) == 0 for i in range(len(grid))])
        last = functools.reduce(lambda a, b: a & b, [pl.program_id(i) == grid[i] - 1 for i in range(len(grid))])

        @pl.when(first)
        def _():
            ride.start(rins, routs, *sems)

        body(*ins, *outs, *scratch)

        @pl.when(last)
        def _():
            ride.finish(rins, routs, *sems)

    hbm = pl.BlockSpec(memory_space=pl.ANY)
    res = pl.pallas_call(
        carried, name=name, grid=grid,
        out_shape=list(out_shape) + list(ride.out_shapes),
        in_specs=list(in_specs) + [hbm] * r_in, out_specs=list(out_specs) + [hbm] * r_out,
        scratch_shapes=list(scratch_shapes) + [pltpu.SemaphoreType.DMA((ride.n_sems,))] * 3,
        input_output_aliases=aliases or {}, compiler_params=_params(sem))(*args, *ride.ins)
    return res[:n_out], res[n_out:]


def _chips(x, y):
    return [(1 - x, y), (x, 1 - y), (1 - x, 1 - y)]


def _gather_ride(arrs):
    n = len(arrs)

    def copies(ins, outs, send_sems, recv_sems):
        x, y, c = _place()
        me = 4 * x + 2 * y + c
        peers = [(x, y, 1 - c)] + [(*chip, c) for chip in _chips(x, y)]
        return [pltpu.make_async_remote_copy(
            src_ref=ins[a], dst_ref=outs[a].at[me], send_sem=send_sems.at[4 * a + k], recv_sem=recv_sems.at[4 * a + k],
            device_id=p, device_id_type=MESH) for a in range(n) for k, p in enumerate(peers)], me

    def start(ins, outs, send_sems, recv_sems, local_sems):
        cps, me = copies(ins, outs, send_sems, recv_sems)
        for a in range(n):
            pltpu.make_async_copy(ins[a], outs[a].at[me], local_sems.at[a]).start()
        for cp in cps:
            cp.start()

    def finish(ins, outs, send_sems, recv_sems, local_sems):
        cps, me = copies(ins, outs, send_sems, recv_sems)
        for cp in cps:
            cp.wait_recv()
        for cp in cps:
            cp.wait_send()
        for a in range(n):
            pltpu.make_async_copy(ins[a], outs[a].at[me], local_sems.at[a]).wait()

    return _Ride(arrs, [jax.ShapeDtypeStruct((NDEV,) + a.shape, a.dtype) for a in arrs], 4 * n, start, finish)


def _gather_finish(arrs):
    n = len(arrs)

    def body(*refs):
        outs = refs[n:2 * n]
        send_sems, recv_sems = refs[2 * n:]
        x, y, c = _place()
        cps = []
        for a in range(n):
            for k, chip in enumerate(_chips(x, y)):
                blk = 4 * chip[0] + 2 * chip[1]
                cps.append((pltpu.make_async_remote_copy(
                    src_ref=outs[a].at[blk + c], dst_ref=outs[a].at[blk + c],
                    send_sem=send_sems.at[3 * a + k], recv_sem=recv_sems.at[3 * a + k],
                    device_id=(x, y, 1 - c), device_id_type=MESH),
                    pltpu.make_async_remote_copy(
                    src_ref=outs[a].at[blk + 1 - c], dst_ref=outs[a].at[blk + 1 - c],
                    send_sem=send_sems.at[3 * a + k], recv_sem=recv_sems.at[3 * a + k],
                    device_id=(x, y, 1 - c), device_id_type=MESH)))
        for send, _ in cps:
            send.start()
        for _, recv in cps:
            recv.wait_recv()
        for send, _ in cps:
            send.wait_send()

    spec = pl.BlockSpec(memory_space=pl.ANY)
    return pl.pallas_call(
        body, name="gather_finish",
        out_shape=[jax.ShapeDtypeStruct(a.shape, a.dtype) for a in arrs],
        in_specs=[spec] * n, out_specs=[spec] * n,
        scratch_shapes=[pltpu.SemaphoreType.DMA((3 * n,)), pltpu.SemaphoreType.DMA((3 * n,))],
        input_output_aliases={a: a for a in range(n)},
        compiler_params=pltpu.CompilerParams(vmem_limit_bytes=VMEM_LIMIT),
    )(*arrs)


def _sibling_exchange(arrs, name):
    n = len(arrs)

    def body(*refs):
        ins, outs = refs[:n], refs[n:2 * n]
        send_sems, recv_sems = refs[2 * n:]
        x, y, c = _place()
        cps = [pltpu.make_async_remote_copy(
            src_ref=ins[a].at[:, 1 - c], dst_ref=outs[a], send_sem=send_sems.at[a], recv_sem=recv_sems.at[a],
            device_id=(x, y, 1 - c), device_id_type=MESH) for a in range(n)]
        for cp in cps:
            cp.start()
        for cp in cps:
            cp.wait()

    spec = pl.BlockSpec(memory_space=pl.ANY)
    return pl.pallas_call(
        body, name=name,
        out_shape=[jax.ShapeDtypeStruct((4,) + a.shape[2:], a.dtype) for a in arrs],
        in_specs=[spec] * n, out_specs=[spec] * n,
        scratch_shapes=[pltpu.SemaphoreType.DMA((n,)), pltpu.SemaphoreType.DMA((n,))],
        compiler_params=pltpu.CompilerParams(vmem_limit_bytes=VMEM_LIMIT),
    )(*arrs)


def _pair_sum(mine, theirs, core, name):
    _, _, r, c = mine.shape
    tr = _pick(r, 512, 16)

    def body(core_ref, a_ref, b_ref, o_ref):
        o_ref[...] = (a_ref[...].astype(F32) + b_ref[...].astype(F32)).astype(BF16)

    return pl.pallas_call(
        body, name=name,
        grid_spec=pltpu.PrefetchScalarGridSpec(
            num_scalar_prefetch=1, grid=(4, r // tr),
            in_specs=[pl.BlockSpec((None, None, tr, c), lambda i, k, core_ref: (i, core_ref[0], k, 0)),
                      pl.BlockSpec((None, tr, c), lambda i, k, core_ref: (i, k, 0))],
            out_specs=pl.BlockSpec((None, tr, c), lambda i, k, core_ref: (i, k, 0))),
        out_shape=jax.ShapeDtypeStruct((4, r, c), BF16),
        compiler_params=_params(("arbitrary", "arbitrary")),
    )(core, mine, theirs)


def _chip_exchange_ride(arrs):
    n = len(arrs)

    def copies(ins, outs, send_sems, recv_sems):
        x, y, c = _place()
        mine = 2 * x + y
        return [pltpu.make_async_remote_copy(
            src_ref=ins[a].at[2 * chip[0] + chip[1]], dst_ref=outs[a].at[mine],
            send_sem=send_sems.at[3 * a + k], recv_sem=recv_sems.at[3 * a + k],
            device_id=(*chip, c), device_id_type=MESH) for a in range(n) for k, chip in enumerate(_chips(x, y))], mine

    def start(ins, outs, send_sems, recv_sems, local_sems):
        cps, mine = copies(ins, outs, send_sems, recv_sems)
        for a in range(n):
            pltpu.make_async_copy(ins[a].at[mine], outs[a].at[mine], local_sems.at[a]).start()
        for cp in cps:
            cp.start()

    def finish(ins, outs, send_sems, recv_sems, local_sems):
        cps, mine = copies(ins, outs, send_sems, recv_sems)
        for cp in cps:
            cp.wait_recv()
        for cp in cps:
            cp.wait_send()
        for a in range(n):
            pltpu.make_async_copy(ins[a].at[mine], outs[a].at[mine], local_sems.at[a]).wait()

    return _Ride(arrs, [jax.ShapeDtypeStruct(a.shape, a.dtype) for a in arrs], 3 * n, start, finish)


def _ada_matmul(c_all, w_loc, b_loc):
    d, n = w_loc.shape
    bn = _pick(n, 512)

    def body(c_ref, w_ref, b_ref, o_ref):
        s = _silu(c_ref[...]).astype(BF16)
        o_ref[...] = _dot(s, w_ref[...].astype(BF16), NN) + b_ref[...]

    return pl.pallas_call(
        body, name="ada_matmul", grid=(n // bn,),
        out_shape=jax.ShapeDtypeStruct((NDEV, n), F32),
        in_specs=[pl.BlockSpec((NDEV, d), lambda j: (0, 0)), pl.BlockSpec((d, bn), lambda j: (0, j)),
                  pl.BlockSpec((1, bn), lambda j: (0, j))],
        out_specs=pl.BlockSpec((NDEV, bn), lambda j: (0, j)),
        compiler_params=_params(("arbitrary",)),
    )(c_all, w_loc, b_loc)


def _modulated_norm(x, norm_g, mod):
    t, d = x.shape
    tt = _pick(t, 256, 16)

    def body(x_ref, g_ref, mod_ref, h_ref):
        xv = x_ref[...]
        r = lax.rsqrt(jnp.mean(xv * xv, axis=-1, keepdims=True) + EPS)
        h = (xv * r) * g_ref[...] * (1.0 + mod_ref[1:2, :]) + mod_ref[0:1, :]
        h_ref[...] = h.astype(BF16)

    return pl.pallas_call(
        body, name="modulated_norm", grid=(t // tt,),
        out_shape=jax.ShapeDtypeStruct((t, d), BF16),
        in_specs=[pl.BlockSpec((tt, d), lambda i: (i, 0)), pl.BlockSpec((1, d), lambda i: (0, 0)),
                  pl.BlockSpec((3, d), lambda i: (0, 0))],
        out_specs=pl.BlockSpec((tt, d), lambda i: (i, 0)),
        compiler_params=_params(("arbitrary",)),
    )(x, norm_g, mod)


def _gather_proj(h, w_loc):
    t, d = h.shape
    ns = w_loc.shape[1]
    tm = _pick(t, 512, 16)
    nm = t // tm
    idx = lambda p: 4 * p[0] + 2 * p[1] + p[2]

    def peers():
        x, y, c = _place()
        flip = lambda a, b: a + b - 2 * a * b
        near, far = (flip(x, c), flip(y, 1 - c)), (flip(x, 1 - c), flip(y, c))
        return (x, y, c), (x, y, 1 - c), near, far, (1 - x, 1 - y), c

    me, sibling, near, far, diag, c = peers()
    order = [me, sibling, (*near, c), (*far, 1 - c), (*far, c), (*near, 1 - c), (*diag, c), (*diag, 1 - c)]
    order = jnp.stack([idx(p) for p in order]).astype(jnp.int32)

    def body(order_ref, a_ref, w_ref, wg_ref, o_ref, slab, send_sems, recv_sems, local_sems):
        j, m = pl.program_id(0), pl.program_id(1)
        me, sibling, near, far, diag, c = peers()

        def rows(p):
            return wg_ref.at[:, pl.ds(pl.multiple_of(idx(p) * ns, LANES), ns)]

        def copy(k, block, to, src=None):
            return pltpu.make_async_remote_copy(
                src_ref=rows(block) if src is None else src, dst_ref=rows(block),
                send_sem=send_sems.at[k], recv_sem=recv_sems.at[k], device_id=to, device_id_type=MESH)

        def load(src):
            cp = pltpu.make_async_copy(src, slab, local_sems.at[1])
            cp.start()
            cp.wait()

        keep = pltpu.make_async_copy(w_ref, rows(me), local_sems.at[0])
        own = [copy(0, me, sibling, src=w_ref), copy(1, me, (*near, c), src=w_ref), copy(2, me, (*far, c), src=w_ref)]
        relay = copy(3, (*near, c), (*far, c))
        passed = [copy(4, (*near, c), sibling), copy(5, (*far, c), sibling), copy(6, (*diag, c), sibling)]
        arrivals = [(1, 0, sibling, []), (2, 1, (*near, c), [passed[0], relay]), (3, 4, (*far, 1 - c), []),
                    (4, 2, (*far, c), [passed[1]]), (5, 5, (*near, 1 - c), []),
                    (6, 3, (*diag, c), [passed[2]]), (7, 6, (*diag, 1 - c), [])]

        @pl.when((j == 0) & (m == 0))
        def _():
            keep.start()
            for cp in own:
                cp.start()
            load(w_ref)

        for step, sem, block, onward in arrivals:
            @pl.when((j == step) & (m == 0))
            def _(sem=sem, block=block, onward=onward):
                copy(sem, block, me).wait_recv()
                for cp in onward:
                    cp.start()
                load(rows(block))

        o_ref[...] = _dot(a_ref[...], slab[...], NN)

        @pl.when((j == NDEV - 1) & (m == nm - 1))
        def _():
            for cp in own + [relay] + passed:
                cp.wait_send()
            keep.wait()

    hbm = pl.BlockSpec(memory_space=pl.ANY)
    return pl.pallas_call(
        body, name="gather_proj",
        grid_spec=pltpu.PrefetchScalarGridSpec(
            num_scalar_prefetch=1, grid=(NDEV, nm),
            in_specs=[pl.BlockSpec((tm, d), lambda j, m, order_ref: (m, 0)), hbm],
            out_specs=[hbm, pl.BlockSpec((tm, ns), lambda j, m, order_ref: (m, order_ref[j]))],
            scratch_shapes=[pltpu.VMEM((d, ns), BF16), pltpu.SemaphoreType.DMA((7,)), pltpu.SemaphoreType.DMA((7,)),
                            pltpu.SemaphoreType.DMA((2,))]),
        out_shape=[jax.ShapeDtypeStruct((d, NDEV * ns), BF16), jax.ShapeDtypeStruct((t, NDEV * ns), F32)],
        compiler_params=_params(("arbitrary", "arbitrary")),
    )(order, h, w_loc)


def _attention_fwd(proj, qg, kg, nh, d_model, ride):
    t = proj.shape[0]
    tq = _pick(t, 256, 16)
    nq = t // tq
    assert 2 <= nq <= LANES
    scale = HEAD_DIM ** -0.5

    def body(q_ref, k_ref, v_ref, g_ref, qg_ref, kg_ref, o_ref, tot_ref, y_ref, first_ref, qn, kn, vb):
        def norm(src, gain, dst):
            v = src[...]
            r = lax.rsqrt(jnp.mean(v * v, axis=-1, keepdims=True) + EPS)
            dst[...] = ((v * r) * gain[...]).astype(BF16)

        norm(q_ref, qg_ref, qn)
        norm(k_ref, kg_ref, kn)
        vb[...] = v_ref[...].astype(BF16)
        def after_matrix(n):
            return (lax.broadcasted_iota(jnp.int32, (n, n), 0) > lax.broadcasted_iota(jnp.int32, (n, n), 1)).astype(BF16)

        upper = {tq: after_matrix(tq), 2 * tq: after_matrix(2 * tq)}

        def block(qi, start, width, carry, acc, q_start=None):
            ks = pl.ds(pl.multiple_of(start, tq), width)
            z = _dot(qi, kn[ks, :], NT) * scale
            sp = _softplus(z)
            ls = -sp
            if q_start is not None:
                causal = (start + lax.broadcasted_iota(jnp.int32, (tq, width), 1)
                          < q_start + lax.broadcasted_iota(jnp.int32, (tq, width), 0))
                ls = jnp.where(causal, ls, 0.0)
            hi = ls.astype(BF16)
            lo = (ls - hi.astype(F32)).astype(BF16)
            after = _dot(hi, upper[width], NN) + _dot(lo, upper[width], NN)
            w = jnp.exp(z - sp + after + carry)
            if q_start is not None:
                w = jnp.where(causal, w, 0.0)
            acc = acc + _dot(w.astype(BF16), vb[ks, :], NN)
            carry = carry + jnp.sum(ls, axis=1, keepdims=True)
            return carry, acc

        lane = lax.broadcasted_iota(jnp.int32, (8, LANES), 1)

        def live(carry):
            return (jnp.max(carry) > DEAD_LOG_WEIGHT).astype(jnp.int32)

        def q_block(i, firsts):
            qs = pl.ds(pl.multiple_of(i * tq, tq), tq)
            qi = qn[qs, :]
            left = jnp.maximum(i - 1, 0)
            carry, acc = block(qi, left * tq, 2 * tq, jnp.zeros((tq, 1), F32), jnp.zeros((tq, HEAD_DIM), F32), i * tq)

            def k_step(st):
                ca, ac = block(qi, (left - 1 - st[0]) * tq, tq, st[1], st[2])
                return st[0] + 1, ca, ac, live(ca)

            done, carry, acc, _ = lax.while_loop(
                lambda st: (st[0] < left) & (st[3] > 0), k_step, (jnp.int32(0), carry, acc, live(carry)))
            o_ref[qs, :] = acc
            tot_ref[qs, :] = jnp.broadcast_to(carry, (tq, HEAD_DIM))
            y_ref[qs, :] = (acc * _silu(g_ref[qs, :])).astype(BF16)
            return jnp.where(lane == i, (left - done).astype(F32), firsts)

        first_ref[...] = lax.fori_loop(0, nq, q_block, jnp.zeros((8, LANES), F32))

    col_block = lambda off: pl.BlockSpec((t, HEAD_DIM), lambda h: (0, off + h))
    vec = pl.BlockSpec((1, HEAD_DIM), lambda h: (0, 0))
    return _call(
        body, name="attention_fwd", grid=(nh,),
        out_shape=[jax.ShapeDtypeStruct((t, nh * HEAD_DIM), F32), jax.ShapeDtypeStruct((t, nh * HEAD_DIM), F32),
                   jax.ShapeDtypeStruct((t, d_model), BF16), jax.ShapeDtypeStruct((nh, 8, LANES), F32)],
        in_specs=[col_block(0), col_block(nh), col_block(2 * nh), col_block(3 * nh), vec, vec],
        out_specs=[col_block(0), col_block(0), col_block(0), pl.BlockSpec((None, 8, LANES), lambda h: (h, 0, 0))],
        scratch_shapes=[pltpu.VMEM((t, HEAD_DIM), BF16)] * 3,
        args=(proj, proj, proj, proj, qg, kg), ride=ride)


def _conv_taps(buf, w_ref, kc, lo, n_rows, cw, first_tap_row):
    acc = None
    for j in range(kc):
        term = w_ref[j:j + 1, lo:lo + cw] * buf[pl.ds(first_tap_row(j), n_rows), pl.ds(lo, cw)]
        acc = term if acc is None else acc + term
    return acc


def _glu_rows(u_ref, g_ref):
    return u_ref[...] * _sigmoid(g_ref[...])


def _conv_fwd(proj, w_dw, b_dw, ln_g, ln_b, kc, cw, ride):
    t = proj.shape[0]
    tt = _pick(t, 128, HALO)
    per = tt // HALO
    chunk = _pick(cw, 256)

    def body(u_ref, g_ref, up_ref, gp_ref, w_ref, b_ref, lg_ref, lb_ref, a_ref, hc_ref, buf):
        i = pl.program_id(0)
        buf[pl.ds(HALO, tt), :] = _glu_rows(u_ref, g_ref)
        halo = _glu_rows(up_ref, gp_ref)
        buf[pl.ds(0, HALO), :] = jnp.where(i > 0, halo, 0.0)
        for lo in range(0, cw, chunk):
            conv = _conv_taps(buf, w_ref, kc, lo, tt, chunk, lambda j: HALO - (kc - 1) + j)
            hc_ref[:, lo:lo + chunk] = conv + b_ref[:, lo:lo + chunk]
        hc = hc_ref[...]
        mu = jnp.mean(hc, axis=-1, keepdims=True)
        xc = hc - mu
        var = jnp.mean(xc * xc, axis=-1, keepdims=True)
        ln = xc * lax.rsqrt(var + EPS) * lg_ref[...] + lb_ref[...]
        a_ref[...] = _silu(ln).astype(BF16)

    ncol = proj.shape[1] // cw
    tile = lambda g: pl.BlockSpec((tt, cw), lambda i: (i, g))
    prev = lambda g: pl.BlockSpec((HALO, cw), lambda i: (jnp.maximum(i * per - 1, 0), g))
    full = lambda r: pl.BlockSpec((r, cw), lambda i: (0, 0))
    return _call(
        body, name="conv_fwd", grid=(t // tt,),
        out_shape=[jax.ShapeDtypeStruct((t, cw), BF16), jax.ShapeDtypeStruct((t, cw), F32)],
        in_specs=[tile(ncol - 3), tile(ncol - 2), prev(ncol - 3), prev(ncol - 2),
                  full(w_dw.shape[0]), full(1), full(1), full(1)],
        out_specs=[pl.BlockSpec((tt, cw), lambda i: (i, 0))] * 2,
        scratch_shapes=[pltpu.VMEM((HALO + tt, cw), F32)],
        args=(proj, proj, proj, proj, w_dw, b_dw, ln_g, ln_b), ride=ride)


def _pointwise_fwd(a, wpw, b_pw, proj, ycat, cw):
    t = a.shape[0]
    tm = _pick(t, 256, 16)
    ncol = proj.shape[1] // cw
    ycol = ycat.shape[1] // cw - 1

    def body(a_ref, w_ref, b_ref, g_ref, y_in, z_ref, y_ref):
        z = _dot(a_ref[...], w_ref[...], NN) + b_ref[...]
        z_ref[...] = z
        y_ref[...] = (z * _silu(g_ref[...])).astype(BF16)

    return pl.pallas_call(
        body, name="pointwise_fwd", grid=(t // tm,),
        out_shape=[jax.ShapeDtypeStruct((t, cw), F32), jax.ShapeDtypeStruct(ycat.shape, ycat.dtype)],
        in_specs=[pl.BlockSpec((tm, cw), lambda m: (m, 0)),
                  pl.BlockSpec((cw, cw), lambda m: (0, 0)),
                  pl.BlockSpec((1, cw), lambda m: (0, 0)),
                  pl.BlockSpec((tm, cw), lambda m: (m, ncol - 1)),
                  pl.BlockSpec(memory_space=pl.ANY)],
        out_specs=[pl.BlockSpec((tm, cw), lambda m: (m, 0)), pl.BlockSpec((tm, cw), lambda m: (m, ycol))],
        input_output_aliases={4: 1},
        compiler_params=_params(("arbitrary",)),
    )(a, wpw, b_pw, proj, ycat)


def _out_matmul(ycat, wout, x, target, mod):
    t, d = x.shape
    kdim = wout.shape[0]
    tm, tn = _pick(t, 512, 16), _pick(d, 512)
    inv_d = 1.0 / d

    def body(a_ref, w_ref, x_ref, tg_ref, mod_ref, dout_ref, dy_ref, sums_ref):
        @pl.when(pl.program_id(1) == 0)
        def _():
            sums_ref[...] = jnp.zeros_like(sums_ref)

        y = _dot(a_ref[...], w_ref[...], NN)
        gate = mod_ref[2:3, :]
        err = (x_ref[...] + gate * y) - tg_ref[...]
        dout = err * inv_d
        dout_ref[...] = dout
        dy_ref[...] = (dout * gate).astype(BF16)
        sums_ref[0:1, :] += jnp.sum(dout * y, axis=0, keepdims=True)
        sums_ref[1:2, :] += jnp.sum(err * err, axis=0, keepdims=True)

    mn = lambda n, m: (m, n)
    return pl.pallas_call(
        body, name="out_matmul", grid=(d // tn, t // tm),
        out_shape=[jax.ShapeDtypeStruct((t, d), F32), jax.ShapeDtypeStruct((t, d), BF16),
                   jax.ShapeDtypeStruct((8, d), F32)],
        in_specs=[pl.BlockSpec((tm, kdim), lambda n, m: (m, 0)),
                  pl.BlockSpec((kdim, tn), lambda n, m: (0, n)),
                  pl.BlockSpec((tm, tn), mn), pl.BlockSpec((tm, tn), mn),
                  pl.BlockSpec((3, tn), lambda n, m: (0, n))],
        out_specs=[pl.BlockSpec((tm, tn), mn), pl.BlockSpec((tm, tn), mn),
                   pl.BlockSpec((8, tn), lambda n, m: (0, n))],
        compiler_params=_params(("arbitrary", "arbitrary")),
    )(ycat, wout, x, target, mod)


def _mm(a, b, form, name, tm, tn, out_dtype, *, slabs=False, n_outer=False, ksplit=1, ride=None):
    if form == TN:
        kdim, m_dim = a.shape
    else:
        m_dim, kdim = a.shape
    n_dim = b.shape[0] if form == NT else b.shape[1]
    tk = kdim // ksplit
    gm, gn = m_dim // tm, n_dim // tn
    mn = (lambda g: (g[1], g[0])) if n_outer else (lambda g: (g[0], g[1]))
    a_map = (lambda *g: (g[2], mn(g)[0])) if form == TN else (lambda *g: (mn(g)[0], g[2]))
    b_map = (lambda *g: (mn(g)[1], g[2])) if form == NT else (lambda *g: (g[2], mn(g)[1]))
    a_blk = (tk, tm) if form == TN else (tm, tk)
    b_blk = (tn, tk) if form == NT else (tk, tn)
    if slabs:
        out_shape = jax.ShapeDtypeStruct((gn, m_dim, tn), out_dtype)
        out_spec = pl.BlockSpec((None, tm, tn), lambda *g: (mn(g)[1], mn(g)[0], 0))
    else:
        out_shape = jax.ShapeDtypeStruct((m_dim, n_dim), out_dtype)
        out_spec = pl.BlockSpec((tm, tn), lambda *g: mn(g))

    def body(a_ref, b_ref, o_ref, *acc):
        part = _dot(a_ref[...], b_ref[...], form)
        if ksplit == 1:
            o_ref[...] = part.astype(out_dtype)
            return
        k = pl.program_id(2)

        @pl.when(k == 0)
        def _():
            acc[0][...] = part

        @pl.when((k > 0) & (k < ksplit - 1))
        def _():
            acc[0][...] += part

        @pl.when(k == ksplit - 1)
        def _():
            o_ref[...] = (acc[0][...] + part).astype(out_dtype)

    return _call(
        body, name=name, grid=((gn, gm) if n_outer else (gm, gn)) + (ksplit,),
        out_shape=[out_shape], in_specs=[pl.BlockSpec(a_blk, a_map), pl.BlockSpec(b_blk, b_map)],
        out_specs=[out_spec], scratch_shapes=[pltpu.VMEM((tm, tn), F32)] if ksplit > 1 else [],
        args=(a, b), ride=ride)


def _attention_bwd(proj, o, tot, firsts, dycat, qg, kg, nh):
    t, in_cols = proj.shape
    tq = _pick(t, 256, 16)
    nq = t // tq
    scale = HEAD_DIM ** -0.5

    def body(q_ref, k_ref, v_ref, g_ref, o_ref, tot_ref, first_ref, dy_ref, qg_ref, kg_ref, dproj_ref, gains_ref,
             qn, kn, vb, dob, dk_acc, dv_acc, dq_acc, outs, sems):
        h = pl.program_id(0)

        def norm(src, gain, dst):
            v = src[...]
            r = lax.rsqrt(jnp.mean(v * v, axis=-1, keepdims=True) + EPS)
            dst[...] = ((v * r) * gain[...]).astype(BF16)

        norm(q_ref, qg_ref, qn)
        norm(k_ref, kg_ref, kn)
        vb[...] = v_ref[...].astype(BF16)
        gs = g_ref[...]
        dyv = dy_ref[...]
        dob[...] = (dyv * _silu(gs)).astype(BF16)
        outs[3] = (dyv * o_ref[...] * _dsilu(gs)).astype(BF16)
        dk_acc[...] = jnp.zeros_like(dk_acc)
        dv_acc[...] = jnp.zeros_like(dv_acc)

        def before_matrix(n, strict):
            r = lax.broadcasted_iota(jnp.int32, (n, n), 0)
            c = lax.broadcasted_iota(jnp.int32, (n, n), 1)
            return ((c < r) if strict else (c <= r)).astype(BF16)

        incl = {n: before_matrix(n, False) for n in (tq, 2 * tq)}
        excl = {n: before_matrix(n, True) for n in (tq, 2 * tq)}
        lane = lax.broadcasted_iota(jnp.int32, (1, LANES), 1)

        def block(start, width, qi, doi, tot_row, p_left, g_left, dq, q_start=None):
            ks = pl.ds(pl.multiple_of(start, tq), width)
            kj = kn[ks, :]
            z = _dot(kj, qi, NT) * scale
            sp = _softplus(z)
            ls = -sp
            if q_start is not None:
                causal = (start + lax.broadcasted_iota(jnp.int32, (width, tq), 0)
                          < q_start + lax.broadcasted_iota(jnp.int32, (width, tq), 1))
                ls = jnp.where(causal, ls, 0.0)
            hi = ls.astype(BF16)
            lo = (ls - hi.astype(F32)).astype(BF16)
            p_inc = _dot(incl[width], hi, NN) + _dot(incl[width], lo, NN) + p_left
            beta = jnp.exp(z - sp)
            w = beta * jnp.exp(tot_row - p_inc)
            if q_start is not None:
                w = jnp.where(causal, w, 0.0)
            dw = _dot(vb[ks, :], doi, NT)
            g = w * dw
            g_before = _dot(excl[width], g.astype(BF16), NN) + g_left
            dz = g * (1.0 - beta) - beta * g_before
            if q_start is not None:
                dz = jnp.where(causal, dz, 0.0)
            dzb = dz.astype(BF16)
            dv_acc[ks, :] += _dot(w.astype(BF16), doi, NN)
            dk_acc[ks, :] += _dot(dzb, qi, NN)
            dq = dq + _dot(dzb, kj, TN)
            p_left = p_left + jnp.sum(ls, axis=0, keepdims=True)
            g_left = g_left + jnp.sum(g, axis=0, keepdims=True)
            return p_left, g_left, dq

        def q_block(i, _):
            qs = pl.ds(pl.multiple_of(i * tq, tq), tq)
            qi = qn[qs, :]
            doi = dob[qs, :]
            tot_row = jnp.transpose(tot_ref[qs, :])[0:1, :]
            zero_row = jnp.zeros((1, tq), F32)

            def k_step(j, carry):
                return block(j * tq, tq, qi, doi, tot_row, carry[0], carry[1], carry[2])

            left = jnp.maximum(i - 1, 0)
            first = jnp.sum(jnp.where(lane == i, first_ref[0:1, :], 0.0)).astype(jnp.int32)
            first = jnp.clip(first, 0, left)
            carry = lax.fori_loop(first, left, k_step, (zero_row, zero_row, jnp.zeros((tq, HEAD_DIM), F32)))
            _, _, dq = block(left * tq, 2 * tq, qi, doi, tot_row, carry[0], carry[1], carry[2], i * tq)
            dq_acc[qs, :] = dq * scale
            return 0

        lax.fori_loop(0, nq, q_block, 0)

        def norm_bwd(src, gain, dn, slot, gain_row):
            v = src[...]
            r = lax.rsqrt(jnp.mean(v * v, axis=-1, keepdims=True) + EPS)
            vhat = v * r
            gains_ref[gain_row:gain_row + 1, :] = jnp.sum(dn * vhat, axis=0, keepdims=True)
            dhat = dn * gain[...]
            outs[slot] = (r * (dhat - vhat * jnp.mean(dhat * vhat, axis=-1, keepdims=True))).astype(BF16)

        gains_ref[...] = jnp.zeros_like(gains_ref)
        norm_bwd(q_ref, qg_ref, dq_acc[...], 0, 0)
        norm_bwd(k_ref, kg_ref, dk_acc[...] * scale, 1, 1)
        outs[2] = dv_acc[...].astype(BF16)
        copies = [pltpu.make_async_copy(
            outs.at[s], dproj_ref.at[:, pl.ds(pl.multiple_of((s * nh + h) * HEAD_DIM, HEAD_DIM), HEAD_DIM)], sems.at[s])
            for s in range(4)]
        for cp in copies:
            cp.start()
        for cp in copies:
            cp.wait()

    col_block = lambda off: pl.BlockSpec((t, HEAD_DIM), lambda h: (0, off + h))
    vec = pl.BlockSpec((1, HEAD_DIM), lambda h: (0, 0))
    head_scr = lambda dt: pltpu.VMEM((t, HEAD_DIM), dt)
    return pl.pallas_call(
        body, name="attention_bwd", grid=(nh,),
        out_shape=[jax.ShapeDtypeStruct((t, in_cols), BF16), jax.ShapeDtypeStruct((nh, 8, HEAD_DIM), F32)],
        in_specs=[col_block(0), col_block(nh), col_block(2 * nh), col_block(3 * nh),
                  col_block(0), col_block(0), pl.BlockSpec((None, 8, LANES), lambda h: (h, 0, 0)), col_block(0), vec, vec],
        out_specs=[pl.BlockSpec(memory_space=pl.ANY), pl.BlockSpec((None, 8, HEAD_DIM), lambda h: (h, 0, 0))],
        scratch_shapes=[head_scr(BF16), head_scr(BF16), head_scr(BF16), head_scr(BF16),
                        head_scr(F32), head_scr(F32), head_scr(F32),
                        pltpu.VMEM((4, t, HEAD_DIM), BF16), pltpu.SemaphoreType.DMA((4,))],
        compiler_params=_params(("arbitrary",)),
    )(proj, proj, proj, proj, o, tot, firsts, dycat, qg, kg)


def _gate_bwd(dycat, z, proj, dproj, cw):
    t = z.shape[0]
    tt = _pick(t, 256, 16)
    ncol = proj.shape[1] // cw
    ycol = dycat.shape[1] // cw - 1

    def body(dy_ref, z_ref, g_ref, dp_in, dz_ref, dp_ref, sums_ref):
        i = pl.program_id(0)

        @pl.when(i == 0)
        def _():
            sums_ref[...] = jnp.zeros_like(sums_ref)

        g = g_ref[...]
        dy = dy_ref[...]
        dz = dy * _silu(g)
        dz_ref[...] = dz.astype(BF16)
        dp_ref[...] = (dy * z_ref[...] * _dsilu(g)).astype(BF16)
        sums_ref[0:1, :] += jnp.sum(dz, axis=0, keepdims=True)

    return pl.pallas_call(
        body, name="gate_bwd", grid=(t // tt,),
        out_shape=[jax.ShapeDtypeStruct((t, cw), BF16), jax.ShapeDtypeStruct(dproj.shape, dproj.dtype),
                   jax.ShapeDtypeStruct((8, cw), F32)],
        in_specs=[pl.BlockSpec((tt, cw), lambda i: (i, ycol)), pl.BlockSpec((tt, cw), lambda i: (i, 0)),
                  pl.BlockSpec((tt, cw), lambda i: (i, ncol - 1)), pl.BlockSpec(memory_space=pl.ANY)],
        out_specs=[pl.BlockSpec((tt, cw), lambda i: (i, 0)), pl.BlockSpec((tt, cw), lambda i: (i, ncol - 1)),
                   pl.BlockSpec((8, cw), lambda i: (0, 0))],
        input_output_aliases={3: 1},
        compiler_params=_params(("arbitrary",)),
    )(dycat, z, proj, dproj)


def _ln_bwd(da, hc, ln_g, ln_b):
    t, cw = hc.shape
    tt = _pick(t, 256, 16)

    def body(da_ref, hc_ref, lg_ref, lb_ref, dh_ref, sums_ref):
        i = pl.program_id(0)

        @pl.when(i == 0)
        def _():
            sums_ref[...] = jnp.zeros_like(sums_ref)

        hcv = hc_ref[...]
        mu = jnp.mean(hcv, axis=-1, keepdims=True)
        xc = hcv - mu
        r = lax.rsqrt(jnp.mean(xc * xc, axis=-1, keepdims=True) + EPS)
        xhat = xc * r
        ln = xhat * lg_ref[...] + lb_ref[...]
        dln = da_ref[...] * _dsilu(ln)
        dxhat = dln * lg_ref[...]
        dhc = r * (dxhat - jnp.mean(dxhat, axis=-1, keepdims=True)
                   - xhat * jnp.mean(dxhat * xhat, axis=-1, keepdims=True))
        dh_ref[...] = dhc
        sums_ref[0:1, :] += jnp.sum(dln * xhat, axis=0, keepdims=True)
        sums_ref[1:2, :] += jnp.sum(dln, axis=0, keepdims=True)
        sums_ref[2:3, :] += jnp.sum(dhc, axis=0, keepdims=True)

    tile = pl.BlockSpec((tt, cw), lambda i: (i, 0))
    vec = pl.BlockSpec((1, cw), lambda i: (0, 0))
    return pl.pallas_call(
        body, name="ln_bwd", grid=(t // tt,),
        out_shape=[jax.ShapeDtypeStruct((t, cw), F32), jax.ShapeDtypeStruct((8, cw), F32)],
        in_specs=[tile, tile, vec, vec],
        out_specs=[tile, pl.BlockSpec((8, cw), lambda i: (0, 0))],
        compiler_params=_params(("arbitrary",)),
    )(da, hc, ln_g, ln_b)


def _conv_bwd(dhc, proj, w_dw, dproj, kc, cw):
    t = proj.shape[0]
    tt = _pick(t, 128, HALO)
    per = tt // HALO
    nt = t // tt
    chunk = _pick(cw, 256)
    ncol = proj.shape[1] // cw
    wr = w_dw.shape[0]

    def body(d_ref, dn_ref, u_ref, g_ref, up_ref, gp_ref, w_ref, dp_in, dp_ref, dw_ref, dbuf, hbuf):
        i = pl.program_id(0)

        @pl.when(i == 0)
        def _():
            dw_ref[...] = jnp.zeros_like(dw_ref)

        dbuf[pl.ds(0, tt), :] = d_ref[...]
        dbuf[pl.ds(tt, HALO), :] = jnp.where(i < nt - 1, dn_ref[...], 0.0)
        hbuf[pl.ds(HALO, tt), :] = _glu_rows(u_ref, g_ref)
        hbuf[pl.ds(0, HALO), :] = jnp.where(i > 0, _glu_rows(up_ref, gp_ref), 0.0)
        for lo in range(0, cw, chunk):
            dhg = _conv_taps(dbuf, w_ref, kc, lo, tt, chunk, lambda j: (kc - 1) - j)
            u = u_ref[:, lo:lo + chunk]
            sg = _sigmoid(g_ref[:, lo:lo + chunk])
            dp_ref[:, lo:lo + chunk] = (dhg * sg).astype(BF16)
            dp_ref[:, cw + lo:cw + lo + chunk] = (dhg * u * sg * (1.0 - sg)).astype(BF16)
            dtile = d_ref[:, lo:lo + chunk]
            for j in range(kc):
                prod = dtile * hbuf[pl.ds(HALO - (kc - 1) + j, tt), pl.ds(lo, chunk)]
                dw_ref[j:j + 1, lo:lo + chunk] += jnp.sum(prod, axis=0, keepdims=True)

    tile = lambda g: pl.BlockSpec((tt, cw), lambda i: (i, g))
    prev = lambda g: pl.BlockSpec((HALO, cw), lambda i: (jnp.maximum(i * per - 1, 0), g))
    return pl.pallas_call(
        body, name="conv_bwd", grid=(nt,),
        out_shape=[jax.ShapeDtypeStruct(dproj.shape, dproj.dtype), jax.ShapeDtypeStruct((wr, cw), F32)],
        in_specs=[pl.BlockSpec((tt, cw), lambda i: (i, 0)),
                  pl.BlockSpec((HALO, cw), lambda i: (jnp.minimum((i + 1) * per, nt * per - 1), 0)),
                  tile(ncol - 3), tile(ncol - 2), prev(ncol - 3), prev(ncol - 2),
                  pl.BlockSpec((wr, cw), lambda i: (0, 0)), pl.BlockSpec(memory_space=pl.ANY)],
        out_specs=[pl.BlockSpec((tt, 2 * cw), lambda i: (i, (ncol - 3) // 2)),
                   pl.BlockSpec((wr, cw), lambda i: (0, 0))],
        scratch_shapes=[pltpu.VMEM((tt + HALO, cw), F32), pltpu.VMEM((HALO + tt, cw), F32)],
        input_output_aliases={7: 0},
        compiler_params=_params(("arbitrary",)),
    )(dhc, dhc, proj, proj, proj, proj, w_dw, dproj)


def _input_grad(dh, x, dout, norm_g, mod):
    t, d = x.shape
    tt = _pick(t, 128, 16)

    def body(dh_ref, x_ref, do_ref, g_ref, mod_ref, gx_ref, sums_ref):
        i = pl.program_id(0)

        @pl.when(i == 0)
        def _():
            sums_ref[...] = jnp.zeros_like(sums_ref)

        xv = x_ref[...]
        dhv = dh_ref[...]
        r = lax.rsqrt(jnp.mean(xv * xv, axis=-1, keepdims=True) + EPS)
        xn = xv * r
        g = g_ref[...]
        one_scale = 1.0 + mod_ref[1:2, :]
        dxn = dhv * g * one_scale
        gx_ref[...] = do_ref[...] + r * (dxn - xn * jnp.mean(dxn * xn, axis=-1, keepdims=True))
        sums_ref[0:1, :] += jnp.sum(dhv, axis=0, keepdims=True)
        sums_ref[1:2, :] += jnp.sum(dhv * (xn * g), axis=0, keepdims=True)
        sums_ref[2:3, :] += jnp.sum(dhv * one_scale * xn, axis=0, keepdims=True)

    tile = pl.BlockSpec((tt, d), lambda i: (i, 0))
    return pl.pallas_call(
        body, name="input_grad", grid=(t // tt,),
        out_shape=[jax.ShapeDtypeStruct((t, d), F32), jax.ShapeDtypeStruct((8, d), F32)],
        in_specs=[tile, tile, tile, pl.BlockSpec((1, d), lambda i: (0, 0)), pl.BlockSpec((3, d), lambda i: (0, 0))],
        out_specs=[tile, pl.BlockSpec((8, d), lambda i: (0, 0))],
        compiler_params=_params(("arbitrary",)),
    )(dh, x, dout, norm_g, mod)


def _sum_adam(parts, w, m, v, name):
    r, c = w.shape
    n_parts = parts.shape[0]
    tr = _pick(r, 128, 16) if r % 16 == 0 else r
    tc = _pick(c, 2048)

    def body(p_ref, w_ref, m_ref, v_ref, g_ref, d_ref, nm_ref, nv_ref):
        g = p_ref[0].astype(F32)
        for i in range(1, n_parts):
            g = g + p_ref[i].astype(F32)
        d, nm, nv = _adam(w_ref[...], g, m_ref[...], v_ref[...])
        g_ref[...] = g
        d_ref[...] = d
        nm_ref[...] = nm
        nv_ref[...] = nv

    tile = pl.BlockSpec((tr, tc), lambda i, j: (i, j))
    out = jax.ShapeDtypeStruct((r, c), F32)
    return pl.pallas_call(
        body, name=name, grid=(r // tr, c // tc),
        out_shape=[out] * 4,
        in_specs=[pl.BlockSpec((n_parts, tr, tc), lambda i, j: (0, i, j)), tile, tile, tile],
        out_specs=[tile] * 4,
        compiler_params=_params(("arbitrary", "arbitrary")),
    )(parts, w, m, v)


def _ada_grad_adam(s_t, dm, w, m, v):
    d, n = w.shape
    tr = _pick(d, 256, 16)

    def body(s_ref, dm_ref, w_ref, m_ref, v_ref, g_ref, d_ref, nm_ref, nv_ref):
        g = lax.dot_general(s_ref[...], dm_ref[...], (NN, ((), ())), preferred_element_type=F32,
                            precision=lax.Precision.HIGHEST)
        dl, nm, nv = _adam(w_ref[...], g, m_ref[...], v_ref[...])
        g_ref[...] = g
        d_ref[...] = dl
        nm_ref[...] = nm
        nv_ref[...] = nv

    tile = pl.BlockSpec((tr, n), lambda i: (i, 0))
    out = jax.ShapeDtypeStruct((d, n), F32)
    return pl.pallas_call(
        body, name="ada_grad_adam", grid=(d // tr,),
        out_shape=[out] * 4,
        in_specs=[pl.BlockSpec((tr, NDEV), lambda i: (i, 0)), pl.BlockSpec((NDEV, n), lambda i: (0, 0)),
                  tile, tile, tile],
        out_specs=[tile] * 4,
        compiler_params=_params(("arbitrary",)),
    )(s_t, dm, w, m, v)


def _silu_t(c_all):
    n, d = c_all.shape

    def body(c_ref, o_ref):
        o_ref[...] = jnp.transpose(_silu(c_ref[...]))

    return pl.pallas_call(
        body, name="silu_t", out_shape=jax.ShapeDtypeStruct((d, n), F32),
        in_specs=[pl.BlockSpec(memory_space=pltpu.VMEM)], out_specs=pl.BlockSpec(memory_space=pltpu.VMEM),
        compiler_params=pltpu.CompilerParams(vmem_limit_bytes=VMEM_LIMIT),
    )(c_all)


def _rows128(v):
    return v.reshape(-1, LANES)


def _pad_rows(a, rows):
    return jnp.pad(a, ((0, rows - a.shape[0]), (0, 0)))


def kernel(x, c, norm_g, w_ada, b_ada, w_in, q_norm_g, k_norm_g, w_dw, b_dw, ln_g, ln_b, w_pw, b_pw, w_out, loss_target, m_norm_g, m_w_ada, m_b_ada, m_w_in, m_q_norm_g, m_k_norm_g, m_w_dw, m_b_dw, m_ln_g, m_ln_b, m_w_pw, m_b_pw, m_w_out, v_norm_g, v_w_ada, v_b_ada, v_w_in, v_q_norm_g, v_k_norm_g, v_w_dw, v_b_dw, v_ln_g, v_ln_b, v_w_pw, v_b_pw, v_w_out):
    _, t, d = x.shape
    n_ada = w_ada.shape[2]
    ns = w_in.shape[2]
    kc, cwl = w_dw.shape[1], w_dw.shape[2]
    cw = cwl * NDEV
    sb = d - cw
    nh = sb // HEAD_DIM
    assert sb == cw and kc - 1 <= HALO and NDEV * ns == 4 * sb + 3 * cw
    my = 4 * lax.axis_index("x") + 2 * lax.axis_index("y") + lax.axis_index("c")

    x2, tg2 = x[0], loss_target[0]

    wdw_rows = -(-kc // 8) * 8
    wdw_pad = _pad_rows(w_dw[0], wdw_rows)
    pay1 = jnp.concatenate([_rows128(c[0]), _rows128(wdw_pad.reshape(-1))], axis=0)
    (g1,) = _all_gather([pay1], "gather_cond", pltpu.VMEM)
    c_rows = d // LANES
    c_all = g1[:, :c_rows].reshape(NDEV, d)
    wdw_all = g1[:, c_rows:].reshape(NDEV, wdw_rows, cwl).transpose(1, 0, 2).reshape(wdw_rows, cw)

    b_ada_loc = lax.dynamic_slice(b_ada, (0, my * n_ada), (1, n_ada))
    mod_cols = _ada_matmul(c_all, w_ada[0], b_ada_loc)
    (g2,) = _all_gather([mod_cols], "gather_mod", pltpu.VMEM)
    mod_mine = lax.dynamic_index_in_dim(g2, my, axis=1, keepdims=False)
    mod = mod_mine.reshape(3, d)

    core = lax.axis_index("c").astype(jnp.int32).reshape(1)
    h = _modulated_norm(x2, norm_g, mod)
    wfull_in, proj = _gather_proj(h, w_in[0].astype(BF16))
    (o, tot, ycat, firsts), partly = _attention_fwd(
        proj, q_norm_g, k_norm_g, nh, d, _gather_ride([w_out[0].astype(BF16), w_pw[0].astype(BF16)]))
    wg_out, wg_pw = _gather_finish(partly)
    wfull_out, wfull_pw = wg_out.reshape(d, d), wg_pw.reshape(cw, cw)
    (a, hc), _ = _conv_fwd(proj, wdw_all, b_dw, ln_g, ln_b, kc, cw, None)
    z, ycat = _pointwise_fwd(a, wfull_pw, b_pw, proj, ycat, cw)
    dout, dy, out_sums = _out_matmul(ycat, wfull_out, x2, tg2, mod)

    tile = _pick(t, 512, 16)
    (dycat,), _ = _mm(dy, wfull_out, NT, "dycat_matmul", tile, _pick(d, 512), F32)
    (p_wout,), _ = _mm(ycat, dy, TN, "w_out_grad", _pick(d, 512), _pick(d, 1024), BF16)
    p_wout = p_wout.reshape(NDEV, d // NDEV, d)
    dproj, gains = _attention_bwd(proj, o, tot, firsts, dycat, q_norm_g, k_norm_g, nh)
    dz, dproj, dz_sums = _gate_bwd(dycat, z, proj, dproj, cw)
    (da,), _ = _mm(dz, wfull_pw, NT, "da_matmul", tile, cw, F32)
    (p_wpw,), _ = _mm(a, dz, TN, "w_pw_grad", _pick(cw, 512), _pick(cw, 1024), BF16)
    p_wpw = p_wpw.reshape(NDEV, cw // NDEV, cw)
    dhc, ln_sums = _ln_bwd(da, hc, ln_g, ln_b)
    dproj, dwdw = _conv_bwd(dhc, proj, wdw_all, dproj, kc, cw)
    p_wdw = dwdw.reshape(wdw_rows, NDEV, cwl).transpose(1, 0, 2).astype(BF16)

    def chip_sums(parts, name):
        split = [p.reshape(4, 2, *p.shape[1:]) for p in parts]
        theirs = _sibling_exchange(split, name + "_sibling")
        return [_pair_sum(m, s, core, f"{name}_pair_sum_{i}") for i, (m, s) in enumerate(zip(split, theirs))]

    q_small = chip_sums([p_wout, p_wpw, p_wdw], "small_grads")
    (p_win,), (r_wout, r_wpw, r_wdw) = _mm(h, dproj, TN, "w_in_grad", _pick(d, 256), ns, BF16, slabs=True,
                                           n_outer=True, ride=_chip_exchange_ride(q_small))
    q_win = chip_sums([p_win], "w_in_grad")
    (dh,), (r_win,) = _mm(dproj, wfull_in, NT, "dh_matmul", tile, _pick(d, 512), F32, ksplit=2,
                          ride=_chip_exchange_ride(q_win))
    grad_x, in_sums = _input_grad(dh, x2, dout, norm_g, mod)

    dmod = jnp.concatenate([in_sums[0], in_sums[1], out_sums[0]])
    loss_part = 0.5 / d * jnp.sum(out_sums[1].reshape(-1, LANES), axis=0)
    small = [in_sums[2], dmod, jnp.sum(gains[:, 0], axis=0), jnp.sum(gains[:, 1], axis=0),
             ln_sums[2], ln_sums[0], ln_sums[1], dz_sums[0], loss_part]
    sizes = [s.shape[0] for s in small]
    packed = _rows128(jnp.concatenate(small))
    n_rows = -(-packed.shape[0] // 8) * 8
    (g3,) = _all_gather([_pad_rows(packed, n_rows)], "gather_small", pltpu.VMEM)

    def pack_state(names_vals):
        flat = jnp.concatenate([v.reshape(-1) for v in names_vals] + [jnp.zeros((LANES,), F32)])
        return _pad_rows(_rows128(flat), n_rows)

    small_w = pack_state([norm_g, b_ada, q_norm_g, k_norm_g, b_dw, ln_g, ln_b, b_pw])
    small_m = pack_state([m_norm_g, m_b_ada, m_q_norm_g, m_k_norm_g, m_b_dw, m_ln_g, m_ln_b, m_b_pw])
    small_v = pack_state([v_norm_g, v_b_ada, v_q_norm_g, v_k_norm_g, v_b_dw, v_ln_g, v_ln_b, v_b_pw])
    sg, sd, sm, sv = _sum_adam(g3, small_w, small_m, small_v, "small_adam")

    def unpack(p):
        flat = p.reshape(-1)
        outs, off = [], 0
        for n in sizes[:-1]:
            outs.append(flat[off:off + n].reshape(1, n))
            off += n
        return outs, flat[off:off + LANES]

    g_small, loss_lanes = unpack(sg)
    d_small, _ = unpack(sd)
    m_small, _ = unpack(sm)
    v_small, _ = unpack(sv)
    loss = jnp.sum(loss_lanes)

    off = sizes[0]
    dmod_all = g3.reshape(NDEV, -1)[:, off:off + 3 * d]
    dmod_loc = lax.dynamic_slice(dmod_all, (0, my * n_ada), (NDEV, n_ada))
    ada = _ada_grad_adam(_silu_t(c_all), dmod_loc, w_ada[0], m_w_ada[0], v_w_ada[0])
    win = _sum_adam(r_win, w_in[0], m_w_in[0], v_w_in[0], "w_in_adam")
    wout = _sum_adam(r_wout, w_out[0], m_w_out[0], v_w_out[0], "w_out_adam")
    wpw = _sum_adam(r_wpw, w_pw[0], m_w_pw[0], v_w_pw[0], "w_pw_adam")
    wdw_state = [_pad_rows(s[0], wdw_rows) for s in (w_dw, m_w_dw, v_w_dw)]
    wdw = [r[:kc] for r in _sum_adam(r_wdw, *wdw_state, "w_dw_adam")]

    def group(k, small_list):
        s = small_list
        return [s[0], ada[k][None], s[1], win[k][None], s[2], s[3], wdw[k][None], s[4], s[5], s[6],
                wpw[k][None], s[7], wout[k][None]]

    return (loss, grad_x[None], *group(0, g_small), *group(1, d_small), *group(2, m_small), *group(3, v_small))
```

```python
import functools

import jax
import jax.numpy as jnp
from jax import lax
from jax.experimental import pallas as pl
from jax.experimental.pallas import tpu as pltpu

F32 = jnp.float32
BF16 = jnp.bfloat16
NDEV = 8
HEAD_DIM = 128
LANES = 128
SUBLANES = 8
HALO = 32
EPS = 1e-6
DEAD_LOG_WEIGHT = -104.0
VMEM_LIMIT = 56 * 1024 * 1024
MESH = pl.DeviceIdType.MESH

ADAM_LR = 0.001
ADAM_B1 = 0.9
ADAM_B2 = 0.999
ADAM_EPS = 1e-08
ADAM_WD = 0.01
ADAM_STEP = 10


def _params(sem=None):
    return pltpu.CompilerParams(dimension_semantics=sem, vmem_limit_bytes=VMEM_LIMIT)


def _pick(n, pref, unit=LANES):
    best = None
    for d in range(unit, min(n, pref) + 1, unit):
        if n % d == 0:
            best = d
    return best if best is not None else n


def _sigmoid(z):
    return 1.0 / (1.0 + jnp.exp(-z))


def _silu(z):
    return z * _sigmoid(z)


def _dsilu(z):
    s = _sigmoid(z)
    return s * (1.0 + z * (1.0 - s))


def _softplus(z):
    return jnp.maximum(z, 0.0) + jnp.log(1.0 + jnp.exp(-jnp.abs(z)))


def _dot(a, b, dims):
    return lax.dot_general(a, b, (dims, ((), ())), preferred_element_type=F32)


NN = ((1,), (0,))
NT = ((1,), (1,))
TN = ((0,), (0,))


def _adam(w, g, m, v):
    m = ADAM_B1 * m + (1.0 - ADAM_B1) * g
    v = ADAM_B2 * v + (1.0 - ADAM_B2) * (g * g)
    m_hat = m / (1.0 - ADAM_B1 ** ADAM_STEP)
    v_hat = v / (1.0 - ADAM_B2 ** ADAM_STEP)
    delta = -ADAM_LR * (m_hat / (jnp.sqrt(v_hat) + ADAM_EPS) + ADAM_WD * w)
    return delta, m, v


def _place():
    x, y, c = lax.axis_index("x"), lax.axis_index("y"), lax.axis_index("c")
    return x, y, c


def _flip(v, bit):
    return 1 - v if bit else v


def _all_gather(arrs, name, space):
    n = len(arrs)

    def body(*refs):
        ins, outs = refs[:n], refs[n:2 * n]
        send_sems, recv_sems, local_sems = refs[2 * n:]
        x, y, c = _place()
        me, sibling = (x, y, c), (x, y, 1 - c)
        chips = [(1 - x, y), (x, 1 - y), (1 - x, 1 - y)]

        def rows(a, p):
            return outs[a].at[4 * p[0] + 2 * p[1] + p[2]]

        def copy(a, k, block, to, src=None):
            return pltpu.make_async_remote_copy(
                src_ref=rows(a, block) if src is None else src, dst_ref=rows(a, block),
                send_sem=send_sems.at[7 * a + k], recv_sem=recv_sems.at[7 * a + k],
                device_id=to, device_id_type=MESH)

        mine = [pltpu.make_async_copy(ins[a], rows(a, me), local_sems.at[a]) for a in range(n)]
        for cp in mine:
            cp.start()
        first = []
        for a in range(n):
            first.append(copy(a, 0, me, sibling, src=ins[a]))
            first += [copy(a, 1 + j, me, (*chip, c), src=ins[a]) for j, chip in enumerate(chips)]
        for cp in first:
            cp.start()
        passed = []
        for j, chip in enumerate(chips):
            for a in range(n):
                copy(a, 1 + j, (*chip, c), me).wait_recv()
                fwd = copy(a, 4 + j, (*chip, c), sibling)
                fwd.start()
                passed.append(fwd)
        for a in range(n):
            copy(a, 0, sibling, me).wait_recv()
            for j, chip in enumerate(chips):
                copy(a, 4 + j, (*chip, 1 - c), me).wait_recv()
        for cp in first + passed:
            cp.wait_send()
        for cp in mine:
            cp.wait()

    spec = pl.BlockSpec(memory_space=space)
    return pl.pallas_call(
        body, name=name,
        out_shape=[jax.ShapeDtypeStruct((NDEV,) + a.shape, a.dtype) for a in arrs],
        in_specs=[spec] * n, out_specs=[spec] * n,
        scratch_shapes=[pltpu.SemaphoreType.DMA((7 * n,)), pltpu.SemaphoreType.DMA((7 * n,)),
                        pltpu.SemaphoreType.DMA((n,))],
        compiler_params=pltpu.CompilerParams(vmem_limit_bytes=VMEM_LIMIT),
    )(*arrs)


class _Ride:
    def __init__(self, ins, out_shapes, n_sems, start, finish):
        self.ins, self.out_shapes, self.n_sems, self.start, self.finish = ins, out_shapes, n_sems, start, finish


def _call(body, *, name, grid, out_shape, in_specs, out_specs, args, scratch_shapes=(), aliases=None, ride=None):
    sem = ("arbitrary",) * len(grid)
    if ride is None:
        return pl.pallas_call(
            body, name=name, grid=grid, out_shape=out_shape, in_specs=in_specs, out_specs=out_specs,
            scratch_shapes=list(scratch_shapes), input_output_aliases=aliases or {},
            compiler_params=_params(sem))(*args), []
    n_in, n_out, n_scr = len(in_specs), len(out_specs), len(scratch_shapes)
    r_in, r_out = len(ride.ins), len(ride.out_shapes)

    def carried(*refs):
        ins, rins = refs[:n_in], refs[n_in:n_in + r_in]
        o0 = n_in + r_in
        outs, routs = refs[o0:o0 + n_out], refs[o0 + n_out:o0 + n_out + r_out]
        scratch = refs[o0 + n_out + r_out:o0 + n_out + r_out + n_scr]
        sems = refs[o0 + n_out + r_out + n_scr:]
        first = functools.reduce(lambda a, b: a & b, [pl.program_id(i) == 0 for i in range(len(grid))])
        last = functools.reduce(lambda a, b: a & b, [pl.program_id(i) == grid[i] - 1 for i in range(len(grid))])

        @pl.when(first)
        def _():
            ride.start(rins, routs, *sems)

        body(*ins, *outs, *scratch)

        @pl.when(last)
        def _():
            ride.finish(rins, routs, *sems)

    hbm = pl.BlockSpec(memory_space=pl.ANY)
    res = pl.pallas_call(
        carried, name=name, grid=grid,
        out_shape=list(out_shape) + list(ride.out_shapes),
        in_specs=list(in_specs) + [hbm] * r_in, out_specs=list(out_specs) + [hbm] * r_out,
        scratch_shapes=list(scratch_shapes) + [pltpu.SemaphoreType.DMA((ride.n_sems,))] * 3,
        input_output_aliases=aliases or {}, compiler_params=_params(sem))(*args, *ride.ins)
    return res[:n_out], res[n_out:]


def _chips(x, y):
    return [(1 - x, y), (x, 1 - y), (1 - x, 1 - y)]


def _gather_ride(arrs):
    n = len(arrs)

    def copies(ins, outs, send_sems, recv_sems):
        x, y, c = _place()
        me = 4 * x + 2 * y + c
        peers = [(x, y, 1 - c)] + [(*chip, c) for chip in _chips(x, y)]
        return [pltpu.make_async_remote_copy(
            src_ref=ins[a], dst_ref=outs[a].at[me], send_sem=send_sems.at[4 * a + k], recv_sem=recv_sems.at[4 * a + k],
            device_id=p, device_id_type=MESH) for a in range(n) for k, p in enumerate(peers)], me

    def start(ins, outs, send_sems, recv_sems, local_sems):
        cps, me = copies(ins, outs, send_sems, recv_sems)
        for a in range(n):
            pltpu.make_async_copy(ins[a], outs[a].at[me], local_sems.at[a]).start()
        for cp in cps:
            cp.start()

    def finish(ins, outs, send_sems, recv_sems, local_sems):
        cps, me = copies(ins, outs, send_sems, recv_sems)
        for cp in cps:
            cp.wait_recv()
        for cp in cps:
            cp.wait_send()
        for a in range(n):
            pltpu.make_async_copy(ins[a], outs[a].at[me], local_sems.at[a]).wait()

    return _Ride(arrs, [jax.ShapeDtypeStruct((NDEV,) + a.shape, a.dtype) for a in arrs], 4 * n, start, finish)


def _gather_finish(arrs):
    n = len(arrs)

    def body(*refs):
        outs = refs[n:2 * n]
        send_sems, recv_sems = refs[2 * n:]
        x, y, c = _place()
        cps = []
        for a in range(n):
            for k, chip in enumerate(_chips(x, y)):
                blk = 4 * chip[0] + 2 * chip[1]
                cps.append((pltpu.make_async_remote_copy(
                    src_ref=outs[a].at[blk + c], dst_ref=outs[a].at[blk + c],
                    send_sem=send_sems.at[3 * a + k], recv_sem=recv_sems.at[3 * a + k],
                    device_id=(x, y, 1 - c), device_id_type=MESH),
                    pltpu.make_async_remote_copy(
                    src_ref=outs[a].at[blk + 1 - c], dst_ref=outs[a].at[blk + 1 - c],
                    send_sem=send_sems.at[3 * a + k], recv_sem=recv_sems.at[3 * a + k],
                    device_id=(x, y, 1 - c), device_id_type=MESH)))
        for send, _ in cps:
            send.start()
        for _, recv in cps:
            recv.wait_recv()
        for send, _ in cps:
            send.wait_send()

    spec = pl.BlockSpec(memory_space=pl.ANY)
    return pl.pallas_call(
        body, name="gather_finish",
        out_shape=[jax.ShapeDtypeStruct(a.shape, a.dtype) for a in arrs],
        in_specs=[spec] * n, out_specs=[spec] * n,
        scratch_shapes=[pltpu.SemaphoreType.DMA((3 * n,)), pltpu.SemaphoreType.DMA((3 * n,))],
        input_output_aliases={a: a for a in range(n)},
        compiler_params=pltpu.CompilerParams(vmem_limit_bytes=VMEM_LIMIT),
    )(*arrs)


def _sibling_exchange(arrs, name):
    n = len(arrs)

    def body(*refs):
        ins, outs = refs[:n], refs[n:2 * n]
        send_sems, recv_sems = refs[2 * n:]
        x, y, c = _place()
        cps = [pltpu.make_async_remote_copy(
            src_ref=ins[a].at[:, 1 - c], dst_ref=outs[a], send_sem=send_sems.at[a], recv_sem=recv_sems.at[a],
            device_id=(x, y, 1 - c), device_id_type=MESH) for a in range(n)]
        for cp in cps:
            cp.start()
        for cp in cps:
            cp.wait()

    spec = pl.BlockSpec(memory_space=pl.ANY)
    return pl.pallas_call(
        body, name=name,
        out_shape=[jax.ShapeDtypeStruct((4,) + a.shape[2:], a.dtype) for a in arrs],
        in_specs=[spec] * n, out_specs=[spec] * n,
        scratch_shapes=[pltpu.SemaphoreType.DMA((n,)), pltpu.SemaphoreType.DMA((n,))],
        compiler_params=pltpu.CompilerParams(vmem_limit_bytes=VMEM_LIMIT),
    )(*arrs)


def _pair_sum(mine, theirs, core, name):
    _, _, r, c = mine.shape
    tr = _pick(r, 512, 16)

    def body(core_ref, a_ref, b_ref, o_ref):
        o_ref[...] = (a_ref[...].astype(F32) + b_ref[...].astype(F32)).astype(BF16)

    return pl.pallas_call(
        body, name=name,
        grid_spec=pltpu.PrefetchScalarGridSpec(
            num_scalar_prefetch=1, grid=(4, r // tr),
            in_specs=[pl.BlockSpec((None, None, tr, c), lambda i, k, core_ref: (i, core_ref[0], k, 0)),
                      pl.BlockSpec((None, tr, c), lambda i, k, core_ref: (i, k, 0))],
            out_specs=pl.BlockSpec((None, tr, c), lambda i, k, core_ref: (i, k, 0))),
        out_shape=jax.ShapeDtypeStruct((4, r, c), BF16),
        compiler_params=_params(("arbitrary", "arbitrary")),
    )(core, mine, theirs)


def _chip_exchange_ride(arrs):
    n = len(arrs)

    def copies(ins, outs, send_sems, recv_sems):
        x, y, c = _place()
        mine = 2 * x + y
        return [pltpu.make_async_remote_copy(
            src_ref=ins[a].at[2 * chip[0] + chip[1]], dst_ref=outs[a].at[mine],
            send_sem=send_sems.at[3 * a + k], recv_sem=recv_sems.at[3 * a + k],
            device_id=(*chip, c), device_id_type=MESH) for a in range(n) for k, chip in enumerate(_chips(x, y))], mine

    def start(ins, outs, send_sems, recv_sems, local_sems):
        cps, mine = copies(ins, outs, send_sems, recv_sems)
        for a in range(n):
            pltpu.make_async_copy(ins[a].at[mine], outs[a].at[mine], local_sems.at[a]).start()
        for cp in cps:
            cp.start()

    def finish(ins, outs, send_sems, recv_sems, local_sems):
        cps, mine = copies(ins, outs, send_sems, recv_sems)
        for cp in cps:
            cp.wait_recv()
        for cp in cps:
            cp.wait_send()
        for a in range(n):
            pltpu.make_async_copy(ins[a].at[mine], outs[a].at[mine], local_sems.at[a]).wait()

    return _Ride(arrs, [jax.ShapeDtypeStruct(a.shape, a.dtype) for a in arrs], 3 * n, start, finish)


def _ada_matmul(c_all, w_loc, b_loc):
    d, n = w_loc.shape
    bn = _pick(n, 512)

    def body(c_ref, w_ref, b_ref, o_ref):
        s = _silu(c_ref[...]).astype(BF16)
        o_ref[...] = _dot(s, w_ref[...].astype(BF16), NN) + b_ref[...]

    return pl.pallas_call(
        body, name="ada_matmul", grid=(n // bn,),
        out_shape=jax.ShapeDtypeStruct((NDEV, n), F32),
        in_specs=[pl.BlockSpec((NDEV, d), lambda j: (0, 0)), pl.BlockSpec((d, bn), lambda j: (0, j)),
                  pl.BlockSpec((1, bn), lambda j: (0, j))],
        out_specs=pl.BlockSpec((NDEV, bn), lambda j: (0, j)),
        compiler_params=_params(("arbitrary",)),
    )(c_all, w_loc, b_loc)


def _modulated_norm(x, norm_g, mod):
    t, d = x.shape
    tt = _pick(t, 256, 16)

    def body(x_ref, g_ref, mod_ref, h_ref):
        xv = x_ref[...]
        r = lax.rsqrt(jnp.mean(xv * xv, axis=-1, keepdims=True) + EPS)
        h = (xv * r) * g_ref[...] * (1.0 + mod_ref[1:2, :]) + mod_ref[0:1, :]
        h_ref[...] = h.astype(BF16)

    return pl.pallas_call(
        body, name="modulated_norm", grid=(t // tt,),
        out_shape=jax.ShapeDtypeStruct((t, d), BF16),
        in_specs=[pl.BlockSpec((tt, d), lambda i: (i, 0)), pl.BlockSpec((1, d), lambda i: (0, 0)),
                  pl.BlockSpec((3, d), lambda i: (0, 0))],
        out_specs=pl.BlockSpec((tt, d), lambda i: (i, 0)),
        compiler_params=_params(("arbitrary",)),
    )(x, norm_g, mod)


def _gather_proj(h, w_loc):
    t, d = h.shape
    ns = w_loc.shape[1]
    tm = _pick(t, 512, 16)
    nm = t // tm
    idx = lambda p: 4 * p[0] + 2 * p[1] + p[2]

    def peers():
        x, y, c = _place()
        flip = lambda a, b: a + b - 2 * a * b
        near, far = (flip(x, c), flip(y, 1 - c)), (flip(x, 1 - c), flip(y, c))
        return (x, y, c), (x, y, 1 - c), near, far, (1 - x, 1 - y), c

    me, sibling, near, far, diag, c = peers()
    order = [me, sibling, (*near, c), (*far, 1 - c), (*far, c), (*near, 1 - c), (*diag, c), (*diag, 1 - c)]
    order = jnp.stack([idx(p) for p in order]).astype(jnp.int32)

    def body(order_ref, a_ref, w_ref, wg_ref, o_ref, slab, send_sems, recv_sems, local_sems):
        j, m = pl.program_id(0), pl.program_id(1)
        me, sibling, near, far, diag, c = peers()

        def rows(p):
            return wg_ref.at[:, pl.ds(pl.multiple_of(idx(p) * ns, LANES), ns)]

        def copy(k, block, to, src=None):
            return pltpu.make_async_remote_copy(
                src_ref=rows(block) if src is None else src, dst_ref=rows(block),
                send_sem=send_sems.at[k], recv_sem=recv_sems.at[k], device_id=to, device_id_type=MESH)

        def load(src):
            cp = pltpu.make_async_copy(src, slab, local_sems.at[1])
            cp.start()
            cp.wait()

        keep = pltpu.make_async_copy(w_ref, rows(me), local_sems.at[0])
        own = [copy(0, me, sibling, src=w_ref), copy(1, me, (*near, c), src=w_ref), copy(2, me, (*far, c), src=w_ref)]
        relay = copy(3, (*near, c), (*far, c))
        passed = [copy(4, (*near, c), sibling), copy(5, (*far, c), sibling), copy(6, (*diag, c), sibling)]
        arrivals = [(1, 0, sibling, []), (2, 1, (*near, c), [passed[0], own[2], relay]), (3, 4, (*far, 1 - c), []),
                    (4, 2, (*far, c), [passed[1]]), (5, 5, (*near, 1 - c), []),
                    (6, 3, (*diag, c), [passed[2]]), (7, 6, (*diag, 1 - c), [])]

        @pl.when((j == 0) & (m == 0))
        def _():
            keep.start()
            for cp in own[:2]:
                cp.start()
            load(w_ref)

        for step, sem, block, onward in arrivals:
            @pl.when((j == step) & (m == 0))
            def _(sem=sem, block=block, onward=onward):
                copy(sem, block, me).wait_recv()
                for cp in onward:
                    cp.start()
                load(rows(block))

        o_ref[...] = _dot(a_ref[...], slab[...], NN)

        @pl.when((j == NDEV - 1) & (m == nm - 1))
        def _():
            for cp in own + [relay] + passed:
                cp.wait_send()
            keep.wait()

    hbm = pl.BlockSpec(memory_space=pl.ANY)
    return pl.pallas_call(
        body, name="gather_proj",
        grid_spec=pltpu.PrefetchScalarGridSpec(
            num_scalar_prefetch=1, grid=(NDEV, nm),
            in_specs=[pl.BlockSpec((tm, d), lambda j, m, order_ref: (m, 0)), hbm],
            out_specs=[hbm, pl.BlockSpec((tm, ns), lambda j, m, order_ref: (m, order_ref[j]))],
            scratch_shapes=[pltpu.VMEM((d, ns), BF16), pltpu.SemaphoreType.DMA((7,)), pltpu.SemaphoreType.DMA((7,)),
                            pltpu.SemaphoreType.DMA((2,))]),
        out_shape=[jax.ShapeDtypeStruct((d, NDEV * ns), BF16), jax.ShapeDtypeStruct((t, NDEV * ns), F32)],
        compiler_params=_params(("arbitrary", "arbitrary")),
    )(order, h, w_loc)


def _attention_fwd(proj, qg, kg, nh, d_model, ride):
    t = proj.shape[0]
    tq = _pick(t, 256, 16)
    nq = t // tq
    assert 2 <= nq <= LANES
    scale = HEAD_DIM ** -0.5

    def body(q_ref, k_ref, v_ref, g_ref, qg_ref, kg_ref, o_ref, tot_ref, y_ref, first_ref, qn, kn, vb):
        def norm(src, gain, dst):
            v = src[...]
            r = lax.rsqrt(jnp.mean(v * v, axis=-1, keepdims=True) + EPS)
            dst[...] = ((v * r) * gain[...]).astype(BF16)

        norm(q_ref, qg_ref, qn)
        norm(k_ref, kg_ref, kn)
        vb[...] = v_ref[...].astype(BF16)
        def after_matrix(n):
            return (lax.broadcasted_iota(jnp.int32, (n, n), 0) > lax.broadcasted_iota(jnp.int32, (n, n), 1)).astype(BF16)

        upper = {tq: after_matrix(tq), 2 * tq: after_matrix(2 * tq)}

        def block(qi, start, width, carry, acc, q_start=None):
            ks = pl.ds(pl.multiple_of(start, tq), width)
            z = _dot(qi, kn[ks, :], NT) * scale
            sp = _softplus(z)
            ls = -sp
            if q_start is not None:
                causal = (start + lax.broadcasted_iota(jnp.int32, (tq, width), 1)
                          < q_start + lax.broadcasted_iota(jnp.int32, (tq, width), 0))
                ls = jnp.where(causal, ls, 0.0)
            hi = ls.astype(BF16)
            lo = (ls - hi.astype(F32)).astype(BF16)
            after = _dot(hi, upper[width], NN) + _dot(lo, upper[width], NN)
            w = jnp.exp(z - sp + after + carry)
            if q_start is not None:
                w = jnp.where(causal, w, 0.0)
            acc = acc + _dot(w.astype(BF16), vb[ks, :], NN)
            carry = carry + jnp.sum(ls, axis=1, keepdims=True)
            return carry, acc

        lane = lax.broadcasted_iota(jnp.int32, (8, LANES), 1)

        def live(carry):
            return (jnp.max(carry) > DEAD_LOG_WEIGHT).astype(jnp.int32)

        def q_block(i, firsts):
            qs = pl.ds(pl.multiple_of(i * tq, tq), tq)
            qi = qn[qs, :]
            left = jnp.maximum(i - 1, 0)
            carry, acc = block(qi, left * tq, 2 * tq, jnp.zeros((tq, 1), F32), jnp.zeros((tq, HEAD_DIM), F32), i * tq)

            def k_step(st):
                ca, ac = block(qi, (left - 1 - st[0]) * tq, tq, st[1], st[2])
                return st[0] + 1, ca, ac, live(ca)

            done, carry, acc, _ = lax.while_loop(
                lambda st: (st[0] < left) & (st[3] > 0), k_step, (jnp.int32(0), carry, acc, live(carry)))
            o_ref[qs, :] = acc
            tot_ref[qs, :] = jnp.broadcast_to(carry, (tq, HEAD_DIM))
            y_ref[qs, :] = (acc * _silu(g_ref[qs, :])).astype(BF16)
            return jnp.where(lane == i, (left - done).astype(F32), firsts)

        first_ref[...] = lax.fori_loop(0, nq, q_block, jnp.zeros((8, LANES), F32))

    col_block = lambda off: pl.BlockSpec((t, HEAD_DIM), lambda h: (0, off + h))
    vec = pl.BlockSpec((1, HEAD_DIM), lambda h: (0, 0))
    return _call(
        body, name="attention_fwd", grid=(nh,),
        out_shape=[jax.ShapeDtypeStruct((t, nh * HEAD_DIM), F32), jax.ShapeDtypeStruct((t, nh * HEAD_DIM), F32),
                   jax.ShapeDtypeStruct((t, d_model), BF16), jax.ShapeDtypeStruct((nh, 8, LANES), F32)],
        in_specs=[col_block(0), col_block(nh), col_block(2 * nh), col_block(3 * nh), vec, vec],
        out_specs=[col_block(0), col_block(0), col_block(0), pl.BlockSpec((None, 8, LANES), lambda h: (h, 0, 0))],
        scratch_shapes=[pltpu.VMEM((t, HEAD_DIM), BF16)] * 3,
        args=(proj, proj, proj, proj, qg, kg), ride=ride)


def _fill_shifts(shifted, n_rows):
    for b in range(1, SUBLANES):
        shifted[b, pl.ds(0, n_rows), :] = shifted[0, pl.ds(b, n_rows), :]


def _shifted_rows(shifted, offset, n_rows, lo, cw):
    a, b = divmod(offset, SUBLANES)
    return shifted[b, pl.ds(SUBLANES * a, n_rows), pl.ds(lo, cw)]


def _conv_taps(shifted, w_ref, kc, lo, n_rows, cw, first_tap_row):
    acc = None
    for j in range(kc):
        term = w_ref[j:j + 1, lo:lo + cw] * _shifted_rows(shifted, first_tap_row(j), n_rows, lo, cw)
        acc = term if acc is None else acc + term
    return acc


def _glu_rows(u_ref, g_ref):
    return u_ref[...] * _sigmoid(g_ref[...])


def _conv_fwd(proj, w_dw, b_dw, ln_g, ln_b, kc, cw, ride):
    t = proj.shape[0]
    tt = _pick(t, 128, HALO)
    per = tt // HALO
    chunk = _pick(cw, 256)

    def body(u_ref, g_ref, up_ref, gp_ref, w_ref, b_ref, lg_ref, lb_ref, a_ref, hc_ref, buf):
        i = pl.program_id(0)
        buf[0, pl.ds(HALO, tt), :] = _glu_rows(u_ref, g_ref)
        halo = _glu_rows(up_ref, gp_ref)
        buf[0, pl.ds(0, HALO), :] = jnp.where(i > 0, halo, 0.0)
        _fill_shifts(buf, tt + HALO - SUBLANES)
        for lo in range(0, cw, chunk):
            conv = _conv_taps(buf, w_ref, kc, lo, tt, chunk, lambda j: HALO - (kc - 1) + j)
            hc_ref[:, lo:lo + chunk] = conv + b_ref[:, lo:lo + chunk]
        hc = hc_ref[...]
        mu = jnp.mean(hc, axis=-1, keepdims=True)
        xc = hc - mu
        var = jnp.mean(xc * xc, axis=-1, keepdims=True)
        ln = xc * lax.rsqrt(var + EPS) * lg_ref[...] + lb_ref[...]
        a_ref[...] = _silu(ln).astype(BF16)

    ncol = proj.shape[1] // cw
    tile = lambda g: pl.BlockSpec((tt, cw), lambda i: (i, g))
    prev = lambda g: pl.BlockSpec((HALO, cw), lambda i: (jnp.maximum(i * per - 1, 0), g))
    full = lambda r: pl.BlockSpec((r, cw), lambda i: (0, 0))
    return _call(
        body, name="conv_fwd", grid=(t // tt,),
        out_shape=[jax.ShapeDtypeStruct((t, cw), BF16), jax.ShapeDtypeStruct((t, cw), F32)],
        in_specs=[tile(ncol - 3), tile(ncol - 2), prev(ncol - 3), prev(ncol - 2),
                  full(w_dw.shape[0]), full(1), full(1), full(1)],
        out_specs=[pl.BlockSpec((tt, cw), lambda i: (i, 0))] * 2,
        scratch_shapes=[pltpu.VMEM((SUBLANES, HALO + tt, cw), F32)],
        args=(proj, proj, proj, proj, w_dw, b_dw, ln_g, ln_b), ride=ride)


def _pointwise_fwd(a, wpw, b_pw, proj, ycat, cw):
    t = a.shape[0]
    tm = _pick(t, 256, 16)
    ncol = proj.shape[1] // cw
    ycol = ycat.shape[1] // cw - 1

    def body(a_ref, w_ref, b_ref, g_ref, y_in, z_ref, y_ref):
        z = _dot(a_ref[...], w_ref[...], NN) + b_ref[...]
        z_ref[...] = z
        y_ref[...] = (z * _silu(g_ref[...])).astype(BF16)

    return pl.pallas_call(
        body, name="pointwise_fwd", grid=(t // tm,),
        out_shape=[jax.ShapeDtypeStruct((t, cw), F32), jax.ShapeDtypeStruct(ycat.shape, ycat.dtype)],
        in_specs=[pl.BlockSpec((tm, cw), lambda m: (m, 0)),
                  pl.BlockSpec((cw, cw), lambda m: (0, 0)),
                  pl.BlockSpec((1, cw), lambda m: (0, 0)),
                  pl.BlockSpec((tm, cw), lambda m: (m, ncol - 1)),
                  pl.BlockSpec(memory_space=pl.ANY)],
        out_specs=[pl.BlockSpec((tm, cw), lambda m: (m, 0)), pl.BlockSpec((tm, cw), lambda m: (m, ycol))],
        input_output_aliases={4: 1},
        compiler_params=_params(("arbitrary",)),
    )(a, wpw, b_pw, proj, ycat)


def _out_matmul(ycat, wout, x, target, mod):
    t, d = x.shape
    kdim = wout.shape[0]
    tm, tn = _pick(t, 512, 16), _pick(d, 512)
    inv_d = 1.0 / d

    def body(a_ref, w_ref, x_ref, tg_ref, mod_ref, dout_ref, dy_ref, sums_ref):
        @pl.when(pl.program_id(1) == 0)
        def _():
            sums_ref[...] = jnp.zeros_like(sums_ref)

        y = _dot(a_ref[...], w_ref[...], NN)
        gate = mod_ref[2:3, :]
        err = (x_ref[...] + gate * y) - tg_ref[...]
        dout = err * inv_d
        dout_ref[...] = dout
        dy_ref[...] = (dout * gate).astype(BF16)
        sums_ref[0:1, :] += jnp.sum(dout * y, axis=0, keepdims=True)
        sums_ref[1:2, :] += jnp.sum(err * err, axis=0, keepdims=True)

    mn = lambda n, m: (m, n)
    return pl.pallas_call(
        body, name="out_matmul", grid=(d // tn, t // tm),
        out_shape=[jax.ShapeDtypeStruct((t, d), F32), jax.ShapeDtypeStruct((t, d), BF16),
                   jax.ShapeDtypeStruct((8, d), F32)],
        in_specs=[pl.BlockSpec((tm, kdim), lambda n, m: (m, 0)),
                  pl.BlockSpec((kdim, tn), lambda n, m: (0, n)),
                  pl.BlockSpec((tm, tn), mn), pl.BlockSpec((tm, tn), mn),
                  pl.BlockSpec((3, tn), lambda n, m: (0, n))],
        out_specs=[pl.BlockSpec((tm, tn), mn), pl.BlockSpec((tm, tn), mn),
                   pl.BlockSpec((8, tn), lambda n, m: (0, n))],
        compiler_params=_params(("arbitrary", "arbitrary")),
    )(ycat, wout, x, target, mod)


def _mm(a, b, form, name, tm, tn, out_dtype, *, slabs=False, n_outer=False, ksplit=1, ride=None):
    if form == TN:
        kdim, m_dim = a.shape
    else:
        m_dim, kdim = a.shape
    n_dim = b.shape[0] if form == NT else b.shape[1]
    tk = kdim // ksplit
    gm, gn = m_dim // tm, n_dim // tn
    mn = (lambda g: (g[1], g[0])) if n_outer else (lambda g: (g[0], g[1]))
    a_map = (lambda *g: (g[2], mn(g)[0])) if form == TN else (lambda *g: (mn(g)[0], g[2]))
    b_map = (lambda *g: (mn(g)[1], g[2])) if form == NT else (lambda *g: (g[2], mn(g)[1]))
    a_blk = (tk, tm) if form == TN else (tm, tk)
    b_blk = (tn, tk) if form == NT else (tk, tn)
    if slabs:
        out_shape = jax.ShapeDtypeStruct((gn, m_dim, tn), out_dtype)
        out_spec = pl.BlockSpec((None, tm, tn), lambda *g: (mn(g)[1], mn(g)[0], 0))
    else:
        out_shape = jax.ShapeDtypeStruct((m_dim, n_dim), out_dtype)
        out_spec = pl.BlockSpec((tm, tn), lambda *g: mn(g))

    def body(a_ref, b_ref, o_ref, *acc):
        part = _dot(a_ref[...], b_ref[...], form)
        if ksplit == 1:
            o_ref[...] = part.astype(out_dtype)
            return
        k = pl.program_id(2)

        @pl.when(k == 0)
        def _():
            acc[0][...] = part

        @pl.when((k > 0) & (k < ksplit - 1))
        def _():
            acc[0][...] += part

        @pl.when(k == ksplit - 1)
        def _():
            o_ref[...] = (acc[0][...] + part).astype(out_dtype)

    return _call(
        body, name=name, grid=((gn, gm) if n_outer else (gm, gn)) + (ksplit,),
        out_shape=[out_shape], in_specs=[pl.BlockSpec(a_blk, a_map), pl.BlockSpec(b_blk, b_map)],
        out_specs=[out_spec], scratch_shapes=[pltpu.VMEM((tm, tn), F32)] if ksplit > 1 else [],
        args=(a, b), ride=ride)


def _attention_bwd(proj, o, tot, firsts, dycat, qg, kg, nh):
    t, in_cols = proj.shape
    tq = _pick(t, 256, 16)
    nq = t // tq
    scale = HEAD_DIM ** -0.5

    def body(q_ref, k_ref, v_ref, g_ref, o_ref, tot_ref, first_ref, dy_ref, qg_ref, kg_ref, dproj_ref, gains_ref,
             qn, kn, vb, dob, dk_acc, dv_acc, dq_acc, outs, sems):
        h = pl.program_id(0)

        def norm(src, gain, dst):
            v = src[...]
            r = lax.rsqrt(jnp.mean(v * v, axis=-1, keepdims=True) + EPS)
            dst[...] = ((v * r) * gain[...]).astype(BF16)

        norm(q_ref, qg_ref, qn)
        norm(k_ref, kg_ref, kn)
        vb[...] = v_ref[...].astype(BF16)
        gs = g_ref[...]
        dyv = dy_ref[...]
        dob[...] = (dyv * _silu(gs)).astype(BF16)
        outs[3] = (dyv * o_ref[...] * _dsilu(gs)).astype(BF16)
        dk_acc[...] = jnp.zeros_like(dk_acc)
        dv_acc[...] = jnp.zeros_like(dv_acc)

        def before_matrix(n, strict):
            r = lax.broadcasted_iota(jnp.int32, (n, n), 0)
            c = lax.broadcasted_iota(jnp.int32, (n, n), 1)
            return ((c < r) if strict else (c <= r)).astype(BF16)

        incl = {n: before_matrix(n, False) for n in (tq, 2 * tq)}
        excl = {n: before_matrix(n, True) for n in (tq, 2 * tq)}
        lane = lax.broadcasted_iota(jnp.int32, (1, LANES), 1)

        def block(start, width, qi, doi, tot_row, p_left, g_left, dq, q_start=None):
            ks = pl.ds(pl.multiple_of(start, tq), width)
            kj = kn[ks, :]
            z = _dot(kj, qi, NT) * scale
            sp = _softplus(z)
            ls = -sp
            if q_start is not None:
                causal = (start + lax.broadcasted_iota(jnp.int32, (width, tq), 0)
                          < q_start + lax.broadcasted_iota(jnp.int32, (width, tq), 1))
                ls = jnp.where(causal, ls, 0.0)
            hi = ls.astype(BF16)
            lo = (ls - hi.astype(F32)).astype(BF16)
            p_inc = _dot(incl[width], hi, NN) + _dot(incl[width], lo, NN) + p_left
            beta = jnp.exp(z - sp)
            w = beta * jnp.exp(tot_row - p_inc)
            if q_start is not None:
                w = jnp.where(causal, w, 0.0)
            dw = _dot(vb[ks, :], doi, NT)
            g = w * dw
            g_before = _dot(excl[width], g.astype(BF16), NN) + g_left
            dz = g * (1.0 - beta) - beta * g_before
            if q_start is not None:
                dz = jnp.where(causal, dz, 0.0)
            dzb = dz.astype(BF16)
            dv_acc[ks, :] += _dot(w.astype(BF16), doi, NN)
            dk_acc[ks, :] += _dot(dzb, qi, NN)
            dq = dq + _dot(dzb, kj, TN)
            p_left = p_left + jnp.sum(ls, axis=0, keepdims=True)
            g_left = g_left + jnp.sum(g, axis=0, keepdims=True)
            return p_left, g_left, dq

        def q_block(i, _):
            qs = pl.ds(pl.multiple_of(i * tq, tq), tq)
            qi = qn[qs, :]
            doi = dob[qs, :]
            tot_row = jnp.transpose(tot_ref[qs, :])[0:1, :]
            zero_row = jnp.zeros((1, tq), F32)

            def k_step(j, carry):
                return block(j * tq, tq, qi, doi, tot_row, carry[0], carry[1], carry[2])

            left = jnp.maximum(i - 1, 0)
            first = jnp.sum(jnp.where(lane == i, first_ref[0:1, :], 0.0)).astype(jnp.int32)
            first = jnp.clip(first, 0, left)
            carry = lax.fori_loop(first, left, k_step, (zero_row, zero_row, jnp.zeros((tq, HEAD_DIM), F32)))
            _, _, dq = block(left * tq, 2 * tq, qi, doi, tot_row, carry[0], carry[1], carry[2], i * tq)
            dq_acc[qs, :] = dq * scale
            return 0

        lax.fori_loop(0, nq, q_block, 0)

        def norm_bwd(src, gain, dn, slot, gain_row):
            v = src[...]
            r = lax.rsqrt(jnp.mean(v * v, axis=-1, keepdims=True) + EPS)
            vhat = v * r
            gains_ref[gain_row:gain_row + 1, :] = jnp.sum(dn * vhat, axis=0, keepdims=True)
            dhat = dn * gain[...]
            outs[slot] = (r * (dhat - vhat * jnp.mean(dhat * vhat, axis=-1, keepdims=True))).astype(BF16)

        gains_ref[...] = jnp.zeros_like(gains_ref)
        norm_bwd(q_ref, qg_ref, dq_acc[...], 0, 0)
        norm_bwd(k_ref, kg_ref, dk_acc[...] * scale, 1, 1)
        outs[2] = dv_acc[...].astype(BF16)
        copies = [pltpu.make_async_copy(
            outs.at[s], dproj_ref.at[:, pl.ds(pl.multiple_of((s * nh + h) * HEAD_DIM, HEAD_DIM), HEAD_DIM)], sems.at[s])
            for s in range(4)]
        for cp in copies:
            cp.start()
        for cp in copies:
            cp.wait()

    col_block = lambda off: pl.BlockSpec((t, HEAD_DIM), lambda h: (0, off + h))
    vec = pl.BlockSpec((1, HEAD_DIM), lambda h: (0, 0))
    head_scr = lambda dt: pltpu.VMEM((t, HEAD_DIM), dt)
    return pl.pallas_call(
        body, name="attention_bwd", grid=(nh,),
        out_shape=[jax.ShapeDtypeStruct((t, in_cols), BF16), jax.ShapeDtypeStruct((nh, 8, HEAD_DIM), F32)],
        in_specs=[col_block(0), col_block(nh), col_block(2 * nh), col_block(3 * nh),
                  col_block(0), col_block(0), pl.BlockSpec((None, 8, LANES), lambda h: (h, 0, 0)), col_block(0), vec, vec],
        out_specs=[pl.BlockSpec(memory_space=pl.ANY), pl.BlockSpec((None, 8, HEAD_DIM), lambda h: (h, 0, 0))],
        scratch_shapes=[head_scr(BF16), head_scr(BF16), head_scr(BF16), head_scr(BF16),
                        head_scr(F32), head_scr(F32), head_scr(F32),
                        pltpu.VMEM((4, t, HEAD_DIM), BF16), pltpu.SemaphoreType.DMA((4,))],
        compiler_params=_params(("arbitrary",)),
    )(proj, proj, proj, proj, o, tot, firsts, dycat, qg, kg)


def _gate_bwd(dycat, z, proj, dproj, cw):
    t = z.shape[0]
    tt = _pick(t, 256, 16)
    ncol = proj.shape[1] // cw
    ycol = dycat.shape[1] // cw - 1

    def body(dy_ref, z_ref, g_ref, dp_in, dz_ref, dp_ref, sums_ref):
        i = pl.program_id(0)

        @pl.when(i == 0)
        def _():
            sums_ref[...] = jnp.zeros_like(sums_ref)

        g = g_ref[...]
        dy = dy_ref[...]
        dz = dy * _silu(g)
        dz_ref[...] = dz.astype(BF16)
        dp_ref[...] = (dy * z_ref[...] * _dsilu(g)).astype(BF16)
        sums_ref[0:1, :] += jnp.sum(dz, axis=0, keepdims=True)

    return pl.pallas_call(
        body, name="gate_bwd", grid=(t // tt,),
        out_shape=[jax.ShapeDtypeStruct((t, cw), BF16), jax.ShapeDtypeStruct(dproj.shape, dproj.dtype),
                   jax.ShapeDtypeStruct((8, cw), F32)],
        in_specs=[pl.BlockSpec((tt, cw), lambda i: (i, ycol)), pl.BlockSpec((tt, cw), lambda i: (i, 0)),
                  pl.BlockSpec((tt, cw), lambda i: (i, ncol - 1)), pl.BlockSpec(memory_space=pl.ANY)],
        out_specs=[pl.BlockSpec((tt, cw), lambda i: (i, 0)), pl.BlockSpec((tt, cw), lambda i: (i, ncol - 1)),
                   pl.BlockSpec((8, cw), lambda i: (0, 0))],
        input_output_aliases={3: 1},
        compiler_params=_params(("arbitrary",)),
    )(dycat, z, proj, dproj)


def _ln_bwd(da, hc, ln_g, ln_b):
    t, cw = hc.shape
    tt = _pick(t, 256, 16)

    def body(da_ref, hc_ref, lg_ref, lb_ref, dh_ref, sums_ref):
        i = pl.program_id(0)

        @pl.when(i == 0)
        def _():
            sums_ref[...] = jnp.zeros_like(sums_ref)

        hcv = hc_ref[...]
        mu = jnp.mean(hcv, axis=-1, keepdims=True)
        xc = hcv - mu
        r = lax.rsqrt(jnp.mean(xc * xc, axis=-1, keepdims=True) + EPS)
        xhat = xc * r
        ln = xhat * lg_ref[...] + lb_ref[...]
        dln = da_ref[...] * _dsilu(ln)
        dxhat = dln * lg_ref[...]
        dhc = r * (dxhat - jnp.mean(dxhat, axis=-1, keepdims=True)
                   - xhat * jnp.mean(dxhat * xhat, axis=-1, keepdims=True))
        dh_ref[...] = dhc
        sums_ref[0:1, :] += jnp.sum(dln * xhat, axis=0, keepdims=True)
        sums_ref[1:2, :] += jnp.sum(dln, axis=0, keepdims=True)
        sums_ref[2:3, :] += jnp.sum(dhc, axis=0, keepdims=True)

    tile = pl.BlockSpec((tt, cw), lambda i: (i, 0))
    vec = pl.BlockSpec((1, cw), lambda i: (0, 0))
    return pl.pallas_call(
        body, name="ln_bwd", grid=(t // tt,),
        out_shape=[jax.ShapeDtypeStruct((t, cw), F32), jax.ShapeDtypeStruct((8, cw), F32)],
        in_specs=[tile, tile, vec, vec],
        out_specs=[tile, pl.BlockSpec((8, cw), lambda i: (0, 0))],
        compiler_params=_params(("arbitrary",)),
    )(da, hc, ln_g, ln_b)


def _conv_bwd(dhc, proj, w_dw, dproj, kc, cw):
    t = proj.shape[0]
    tt = _pick(t, 128, HALO)
    per = tt // HALO
    nt = t // tt
    chunk = _pick(cw, 256)
    ncol = proj.shape[1] // cw
    wr = w_dw.shape[0]

    def body(d_ref, dn_ref, u_ref, g_ref, up_ref, gp_ref, w_ref, dp_in, dp_ref, dw_ref, dbuf, hbuf, dw_acc):
        i = pl.program_id(0)

        @pl.when(i == 0)
        def _():
            dw_acc[...] = jnp.zeros_like(dw_acc)

        dbuf[0, pl.ds(0, tt), :] = d_ref[...]
        dbuf[0, pl.ds(tt, HALO), :] = jnp.where(i < nt - 1, dn_ref[...], 0.0)
        hbuf[0, pl.ds(HALO, tt), :] = _glu_rows(u_ref, g_ref)
        hbuf[0, pl.ds(0, HALO), :] = jnp.where(i > 0, _glu_rows(up_ref, gp_ref), 0.0)
        _fill_shifts(dbuf, tt + HALO - SUBLANES)
        _fill_shifts(hbuf, tt + HALO - SUBLANES)
        for lo in range(0, cw, chunk):
            dhg = _conv_taps(dbuf, w_ref, kc, lo, tt, chunk, lambda j: (kc - 1) - j)
            u = u_ref[:, lo:lo + chunk]
            sg = _sigmoid(g_ref[:, lo:lo + chunk])
            dp_ref[:, lo:lo + chunk] = (dhg * sg).astype(BF16)
            dp_ref[:, cw + lo:cw + lo + chunk] = (dhg * u * sg * (1.0 - sg)).astype(BF16)
            dtile = d_ref[:, lo:lo + chunk]
            for j in range(kc):
                prod = dtile * _shifted_rows(hbuf, HALO - (kc - 1) + j, tt, lo, chunk)
                dw_acc[j, :, lo:lo + chunk] += jnp.sum(prod.reshape(tt // SUBLANES, SUBLANES, chunk), axis=0)

        @pl.when(i == nt - 1)
        def _():
            dw_ref[...] = jnp.sum(dw_acc[...], axis=1)

    tile = lambda g: pl.BlockSpec((tt, cw), lambda i: (i, g))
    prev = lambda g: pl.BlockSpec((HALO, cw), lambda i: (jnp.maximum(i * per - 1, 0), g))
    return pl.pallas_call(
        body, name="conv_bwd", grid=(nt,),
        out_shape=[jax.ShapeDtypeStruct(dproj.shape, dproj.dtype), jax.ShapeDtypeStruct((wr, cw), F32)],
        in_specs=[pl.BlockSpec((tt, cw), lambda i: (i, 0)),
                  pl.BlockSpec((HALO, cw), lambda i: (jnp.minimum((i + 1) * per, nt * per - 1), 0)),
                  tile(ncol - 3), tile(ncol - 2), prev(ncol - 3), prev(ncol - 2),
                  pl.BlockSpec((wr, cw), lambda i: (0, 0)), pl.BlockSpec(memory_space=pl.ANY)],
        out_specs=[pl.BlockSpec((tt, 2 * cw), lambda i: (i, (ncol - 3) // 2)),
                   pl.BlockSpec((wr, cw), lambda i: (0, 0))],
        scratch_shapes=[pltpu.VMEM((SUBLANES, tt + HALO, cw), F32), pltpu.VMEM((SUBLANES, HALO + tt, cw), F32),
                        pltpu.VMEM((wr, SUBLANES, cw), F32)],
        input_output_aliases={7: 0},
        compiler_params=_params(("arbitrary",)),
    )(dhc, dhc, proj, proj, proj, proj, w_dw, dproj)


def _input_grad(dh, x, dout, norm_g, mod):
    t, d = x.shape
    tt = _pick(t, 128, 16)

    def body(dh_ref, x_ref, do_ref, g_ref, mod_ref, gx_ref, sums_ref):
        i = pl.program_id(0)

        @pl.when(i == 0)
        def _():
            sums_ref[...] = jnp.zeros_like(sums_ref)

        xv = x_ref[...]
        dhv = dh_ref[...]
        r = lax.rsqrt(jnp.mean(xv * xv, axis=-1, keepdims=True) + EPS)
        xn = xv * r
        g = g_ref[...]
        one_scale = 1.0 + mod_ref[1:2, :]
        dxn = dhv * g * one_scale
        gx_ref[...] = do_ref[...] + r * (dxn - xn * jnp.mean(dxn * xn, axis=-1, keepdims=True))
        sums_ref[0:1, :] += jnp.sum(dhv, axis=0, keepdims=True)
        sums_ref[1:2, :] += jnp.sum(dhv * (xn * g), axis=0, keepdims=True)
        sums_ref[2:3, :] += jnp.sum(dhv * one_scale * xn, axis=0, keepdims=True)

    tile = pl.BlockSpec((tt, d), lambda i: (i, 0))
    return pl.pallas_call(
        body, name="input_grad", grid=(t // tt,),
        out_shape=[jax.ShapeDtypeStruct((t, d), F32), jax.ShapeDtypeStruct((8, d), F32)],
        in_specs=[tile, tile, tile, pl.BlockSpec((1, d), lambda i: (0, 0)), pl.BlockSpec((3, d), lambda i: (0, 0))],
        out_specs=[tile, pl.BlockSpec((8, d), lambda i: (0, 0))],
        compiler_params=_params(("arbitrary",)),
    )(dh, x, dout, norm_g, mod)


def _sum_adam(parts, w, m, v, name):
    r, c = w.shape
    n_parts = parts.shape[0]
    tr = _pick(r, 128, 16) if r % 16 == 0 else r
    tc = _pick(c, 2048)

    def body(p_ref, w_ref, m_ref, v_ref, g_ref, d_ref, nm_ref, nv_ref):
        g = p_ref[0].astype(F32)
        for i in range(1, n_parts):
            g = g + p_ref[i].astype(F32)
        d, nm, nv = _adam(w_ref[...], g, m_ref[...], v_ref[...])
        g_ref[...] = g
        d_ref[...] = d
        nm_ref[...] = nm
        nv_ref[...] = nv

    tile = pl.BlockSpec((tr, tc), lambda i, j: (i, j))
    out = jax.ShapeDtypeStruct((r, c), F32)
    return pl.pallas_call(
        body, name=name, grid=(r // tr, c // tc),
        out_shape=[out] * 4,
        in_specs=[pl.BlockSpec((n_parts, tr, tc), lambda i, j: (0, i, j)), tile, tile, tile],
        out_specs=[tile] * 4,
        compiler_params=_params(("arbitrary", "arbitrary")),
    )(parts, w, m, v)


def _ada_grad_adam(s_t, dm, w, m, v):
    d, n = w.shape
    tr = _pick(d, 256, 16)

    def body(s_ref, dm_ref, w_ref, m_ref, v_ref, g_ref, d_ref, nm_ref, nv_ref):
        g = lax.dot_general(s_ref[...], dm_ref[...], (NN, ((), ())), preferred_element_type=F32,
                            precision=lax.Precision.HIGHEST)
        dl, nm, nv = _adam(w_ref[...], g, m_ref[...], v_ref[...])
        g_ref[...] = g
        d_ref[...] = dl
        nm_ref[...] = nm
        nv_ref[...] = nv

    tile = pl.BlockSpec((tr, n), lambda i: (i, 0))
    out = jax.ShapeDtypeStruct((d, n), F32)
    return pl.pallas_call(
        body, name="ada_grad_adam", grid=(d // tr,),
        out_shape=[out] * 4,
        in_specs=[pl.BlockSpec((tr, NDEV), lambda i: (i, 0)), pl.BlockSpec((NDEV, n), lambda i: (0, 0)),
                  tile, tile, tile],
        out_specs=[tile] * 4,
        compiler_params=_params(("arbitrary",)),
    )(s_t, dm, w, m, v)


def _silu_t(c_all):
    n, d = c_all.shape

    def body(c_ref, o_ref):
        o_ref[...] = jnp.transpose(_silu(c_ref[...]))

    return pl.pallas_call(
        body, name="silu_t", out_shape=jax.ShapeDtypeStruct((d, n), F32),
        in_specs=[pl.BlockSpec(memory_space=pltpu.VMEM)], out_specs=pl.BlockSpec(memory_space=pltpu.VMEM),
        compiler_params=pltpu.CompilerParams(vmem_limit_bytes=VMEM_LIMIT),
    )(c_all)


def _rows128(v):
    return v.reshape(-1, LANES)


def _pad_rows(a, rows):
    return jnp.pad(a, ((0, rows - a.shape[0]), (0, 0)))


def kernel(x, c, norm_g, w_ada, b_ada, w_in, q_norm_g, k_norm_g, w_dw, b_dw, ln_g, ln_b, w_pw, b_pw, w_out, loss_target, m_norm_g, m_w_ada, m_b_ada, m_w_in, m_q_norm_g, m_k_norm_g, m_w_dw, m_b_dw, m_ln_g, m_ln_b, m_w_pw, m_b_pw, m_w_out, v_norm_g, v_w_ada, v_b_ada, v_w_in, v_q_norm_g, v_k_norm_g, v_w_dw, v_b_dw, v_ln_g, v_ln_b, v_w_pw, v_b_pw, v_w_out):
    _, t, d = x.shape
    n_ada = w_ada.shape[2]
    ns = w_in.shape[2]
    kc, cwl = w_dw.shape[1], w_dw.shape[2]
    cw = cwl * NDEV
    sb = d - cw
    nh = sb // HEAD_DIM
    assert sb == cw and kc - 1 <= HALO and NDEV * ns == 4 * sb + 3 * cw
    my = 4 * lax.axis_index("x") + 2 * lax.axis_index("y") + lax.axis_index("c")

    x2, tg2 = x[0], loss_target[0]

    wdw_rows = -(-kc // 8) * 8
    wdw_pad = _pad_rows(w_dw[0], wdw_rows)
    pay1 = jnp.concatenate([_rows128(c[0]), _rows128(wdw_pad.reshape(-1))], axis=0)
    (g1,) = _all_gather([pay1], "gather_cond", pltpu.VMEM)
    c_rows = d // LANES
    c_all = g1[:, :c_rows].reshape(NDEV, d)
    wdw_all = g1[:, c_rows:].reshape(NDEV, wdw_rows, cwl).transpose(1, 0, 2).reshape(wdw_rows, cw)

    b_ada_loc = lax.dynamic_slice(b_ada, (0, my * n_ada), (1, n_ada))
    mod_cols = _ada_matmul(c_all, w_ada[0], b_ada_loc)
    (g2,) = _all_gather([mod_cols], "gather_mod", pltpu.VMEM)
    mod_mine = lax.dynamic_index_in_dim(g2, my, axis=1, keepdims=False)
    mod = mod_mine.reshape(3, d)

    core = lax.axis_index("c").astype(jnp.int32).reshape(1)
    h = _modulated_norm(x2, norm_g, mod)
    wfull_in, proj = _gather_proj(h, w_in[0].astype(BF16))
    (o, tot, ycat, firsts), partly = _attention_fwd(
        proj, q_norm_g, k_norm_g, nh, d, _gather_ride([w_out[0].astype(BF16), w_pw[0].astype(BF16)]))
    wg_out, wg_pw = _gather_finish(partly)
    wfull_out, wfull_pw = wg_out.reshape(d, d), wg_pw.reshape(cw, cw)
    (a, hc), _ = _conv_fwd(proj, wdw_all, b_dw, ln_g, ln_b, kc, cw, None)
    z, ycat = _pointwise_fwd(a, wfull_pw, b_pw, proj, ycat, cw)
    dout, dy, out_sums = _out_matmul(ycat, wfull_out, x2, tg2, mod)

    tile = _pick(t, 512, 16)
    (dycat,), _ = _mm(dy, wfull_out, NT, "dycat_matmul", tile, _pick(d, 512), F32)
    (p_wout,), _ = _mm(ycat, dy, TN, "w_out_grad", _pick(d, 512), _pick(d, 1024), BF16)
    p_wout = p_wout.reshape(NDEV, d // NDEV, d)
    dproj, gains = _attention_bwd(proj, o, tot, firsts, dycat, q_norm_g, k_norm_g, nh)
    dz, dproj, dz_sums = _gate_bwd(dycat, z, proj, dproj, cw)
    (da,), _ = _mm(dz, wfull_pw, NT, "da_matmul", tile, cw, F32)
    (p_wpw,), _ = _mm(a, dz, TN, "w_pw_grad", _pick(cw, 512), _pick(cw, 1024), BF16)
    p_wpw = p_wpw.reshape(NDEV, cw // NDEV, cw)
    dhc, ln_sums = _ln_bwd(da, hc, ln_g, ln_b)
    dproj, dwdw = _conv_bwd(dhc, proj, wdw_all, dproj, kc, cw)
    p_wdw = dwdw.reshape(wdw_rows, NDEV, cwl).transpose(1, 0, 2).astype(BF16)

    def chip_sums(parts, name):
        split = [p.reshape(4, 2, *p.shape[1:]) for p in parts]
        theirs = _sibling_exchange(split, name + "_sibling")
        return [_pair_sum(m, s, core, f"{name}_pair_sum_{i}") for i, (m, s) in enumerate(zip(split, theirs))]

    q_small = chip_sums([p_wout, p_wpw, p_wdw], "small_grads")
    (p_win,), (r_wout, r_wpw, r_wdw) = _mm(h, dproj, TN, "w_in_grad", _pick(d, 256), ns, BF16, slabs=True,
                                           n_outer=True, ride=_chip_exchange_ride(q_small))
    q_win = chip_sums([p_win], "w_in_grad")
    (dh,), (r_win,) = _mm(dproj, wfull_in, NT, "dh_matmul", tile, _pick(d, 512), F32, ksplit=2,
                          ride=_chip_exchange_ride(q_win))
    grad_x, in_sums = _input_grad(dh, x2, dout, norm_g, mod)

    dmod = jnp.concatenate([in_sums[0], in_sums[1], out_sums[0]])
    loss_part = 0.5 / d * jnp.sum(out_sums[1].reshape(-1, LANES), axis=0)
    small = [in_sums[2], dmod, jnp.sum(gains[:, 0], axis=0), jnp.sum(gains[:, 1], axis=0),
             ln_sums[2], ln_sums[0], ln_sums[1], dz_sums[0], loss_part]
    sizes = [s.shape[0] for s in small]
    packed = _rows128(jnp.concatenate(small))
    n_rows = -(-packed.shape[0] // 8) * 8
    (g3,) = _all_gather([_pad_rows(packed, n_rows)], "gather_small", pltpu.VMEM)

    def pack_state(names_vals):
        flat = jnp.concatenate([v.reshape(-1) for v in names_vals] + [jnp.zeros((LANES,), F32)])
        return _pad_rows(_rows128(flat), n_rows)

    small_w = pack_state([norm_g, b_ada, q_norm_g, k_norm_g, b_dw, ln_g, ln_b, b_pw])
    small_m = pack_state([m_norm_g, m_b_ada, m_q_norm_g, m_k_norm_g, m_b_dw, m_ln_g, m_ln_b, m_b_pw])
    small_v = pack_state([v_norm_g, v_b_ada, v_q_norm_g, v_k_norm_g, v_b_dw, v_ln_g, v_ln_b, v_b_pw])
    sg, sd, sm, sv = _sum_adam(g3, small_w, small_m, small_v, "small_adam")

    def unpack(p):
        flat = p.reshape(-1)
        outs, off = [], 0
        for n in sizes[:-1]:
            outs.append(flat[off:off + n].reshape(1, n))
            off += n
        return outs, flat[off:off + LANES]

    g_small, loss_lanes = unpack(sg)
    d_small, _ = unpack(sd)
    m_small, _ = unpack(sm)
    v_small, _ = unpack(sv)
    loss = jnp.sum(loss_lanes)

    off = sizes[0]
    dmod_all = g3.reshape(NDEV, -1)[:, off:off + 3 * d]
    dmod_loc = lax.dynamic_slice(dmod_all, (0, my * n_ada), (NDEV, n_ada))
    ada = _ada_grad_adam(_silu_t(c_all), dmod_loc, w_ada[0], m_w_ada[0], v_w_ada[0])
    win = _sum_adam(r_win, w_in[0], m_w_in[0], v_w_in[0], "w_in_adam")
    wout = _sum_adam(r_wout, w_out[0], m_w_out[0], v_w_out[0], "w_out_adam")
    wpw = _sum_adam(r_wpw, w_pw[0], m_w_pw[0], v_w_pw[0], "w_pw_adam")
    wdw_state = [_pad_rows(s[0], wdw_rows) for s in (w_dw, m_w_dw, v_w_dw)]
    wdw = [r[:kc] for r in _sum_adam(r_wdw, *wdw_state, "w_dw_adam")]

    def group(k, small_list):
        s = small_list
        return [s[0], ada[k][None], s[1], win[k][None], s[2], s[3], wdw[k][None], s[4], s[5], s[6],
                wpw[k][None], s[7], wout[k][None]]

    return (loss, grad_x[None], *group(0, g_small), *group(1, d_small), *group(2, m_small), *group(3, v_small))
```

```python
import functools

import jax
import jax.numpy as jnp
from jax import lax
from jax.experimental import pallas as pl
from jax.experimental.pallas import tpu as pltpu

F32 = jnp.float32
BF16 = jnp.bfloat16
NDEV = 8
HEAD_DIM = 128
LANES = 128
SUBLANES = 8
HALO = 32
EPS = 1e-6
DEAD_LOG_WEIGHT = -104.0
VMEM_LIMIT = 56 * 1024 * 1024
MESH = pl.DeviceIdType.MESH

ADAM_LR = 0.001
ADAM_B1 = 0.9
ADAM_B2 = 0.999
ADAM_EPS = 1e-08
ADAM_WD = 0.01
ADAM_STEP = 10


def _params(sem=None):
    return pltpu.CompilerParams(dimension_semantics=sem, vmem_limit_bytes=VMEM_LIMIT)


def _pick(n, pref, unit=LANES):
    best = None
    for d in range(unit, min(n, pref) + 1, unit):
        if n % d == 0:
            best = d
    return best if best is not None else n


def _sigmoid(z):
    return 1.0 / (1.0 + jnp.exp(-z))


def _silu(z):
    return z * _sigmoid(z)


def _dsilu(z):
    s = _sigmoid(z)
    return s * (1.0 + z * (1.0 - s))


def _softplus(z):
    return jnp.maximum(z, 0.0) + jnp.log(1.0 + jnp.exp(-jnp.abs(z)))


def _dot(a, b, dims):
    return lax.dot_general(a, b, (dims, ((), ())), preferred_element_type=F32)


NN = ((1,), (0,))
NT = ((1,), (1,))
TN = ((0,), (0,))


def _adam(w, g, m, v):
    m = ADAM_B1 * m + (1.0 - ADAM_B1) * g
    v = ADAM_B2 * v + (1.0 - ADAM_B2) * (g * g)
    m_hat = m / (1.0 - ADAM_B1 ** ADAM_STEP)
    v_hat = v / (1.0 - ADAM_B2 ** ADAM_STEP)
    delta = -ADAM_LR * (m_hat / (jnp.sqrt(v_hat) + ADAM_EPS) + ADAM_WD * w)
    return delta, m, v


def _place():
    x, y, c = lax.axis_index("x"), lax.axis_index("y"), lax.axis_index("c")
    return x, y, c


def _flip(v, bit):
    return 1 - v if bit else v


def _all_gather(arrs, name, space):
    n = len(arrs)

    def body(*refs):
        ins, outs = refs[:n], refs[n:2 * n]
        send_sems, recv_sems, local_sems = refs[2 * n:]
        x, y, c = _place()
        me, sibling = (x, y, c), (x, y, 1 - c)
        chips = [(1 - x, y), (x, 1 - y), (1 - x, 1 - y)]

        def rows(a, p):
            return outs[a].at[4 * p[0] + 2 * p[1] + p[2]]

        def copy(a, k, block, to, src=None):
            return pltpu.make_async_remote_copy(
                src_ref=rows(a, block) if src is None else src, dst_ref=rows(a, block),
                send_sem=send_sems.at[7 * a + k], recv_sem=recv_sems.at[7 * a + k],
                device_id=to, device_id_type=MESH)

        mine = [pltpu.make_async_copy(ins[a], rows(a, me), local_sems.at[a]) for a in range(n)]
        for cp in mine:
            cp.start()
        first = []
        for a in range(n):
            first.append(copy(a, 0, me, sibling, src=ins[a]))
            first += [copy(a, 1 + j, me, (*chip, c), src=ins[a]) for j, chip in enumerate(chips)]
        for cp in first:
            cp.start()
        passed = []
        for j, chip in enumerate(chips):
            for a in range(n):
                copy(a, 1 + j, (*chip, c), me).wait_recv()
                fwd = copy(a, 4 + j, (*chip, c), sibling)
                fwd.start()
                passed.append(fwd)
        for a in range(n):
            copy(a, 0, sibling, me).wait_recv()
            for j, chip in enumerate(chips):
                copy(a, 4 + j, (*chip, 1 - c), me).wait_recv()
        for cp in first + passed:
            cp.wait_send()
        for cp in mine:
            cp.wait()

    spec = pl.BlockSpec(memory_space=space)
    return pl.pallas_call(
        body, name=name,
        out_shape=[jax.ShapeDtypeStruct((NDEV,) + a.shape, a.dtype) for a in arrs],
        in_specs=[spec] * n, out_specs=[spec] * n,
        scratch_shapes=[pltpu.SemaphoreType.DMA((7 * n,)), pltpu.SemaphoreType.DMA((7 * n,)),
                        pltpu.SemaphoreType.DMA((n,))],
        compiler_params=pltpu.CompilerParams(vmem_limit_bytes=VMEM_LIMIT),
    )(*arrs)


class _Ride:
    def __init__(self, ins, out_shapes, n_sems, start, finish, in_place=False):
        self.ins, self.out_shapes, self.n_sems, self.start, self.finish = ins, out_shapes, n_sems, start, finish
        self.in_place = in_place


def _call(body, *, name, grid, out_shape, in_specs, out_specs, args, scratch_shapes=(), rides=(), prefetch=None):
    sem = ("arbitrary",) * len(grid)
    rides = [r for r in rides if r is not None]
    n_pre = 0 if prefetch is None else 1
    n_in, n_out, n_scr = len(in_specs), len(out_specs), len(scratch_shapes)
    r_ins = [len(r.ins) for r in rides]
    r_outs = [len(r.out_shapes) for r in rides]

    def carried(*refs):
        pre, refs = refs[:n_pre], refs[n_pre:]
        ins, pos = refs[:n_in], n_in
        rins = []
        for k in r_ins:
            rins.append(refs[pos:pos + k])
            pos += k
        outs, pos = refs[pos:pos + n_out], pos + n_out
        routs = []
        for k in r_outs:
            routs.append(refs[pos:pos + k])
            pos += k
        scratch, pos = refs[pos:pos + n_scr], pos + n_scr
        sems = [refs[pos + 3 * i:pos + 3 * i + 3] for i in range(len(rides))]
        first = functools.reduce(lambda a, b: a & b, [pl.program_id(i) == 0 for i in range(len(grid))])
        last = functools.reduce(lambda a, b: a & b, [pl.program_id(i) == grid[i] - 1 for i in range(len(grid))])

        @pl.when(first)
        def _():
            for ride, ri, ro, s in zip(rides, rins, routs, sems):
                ride.start(ri, ro, *s)

        body(*pre, *ins, *outs, *scratch)

        @pl.when(last)
        def _():
            for ride, ri, ro, s in zip(rides, rins, routs, sems):
                ride.finish(ri, ro, *s)

    hbm = pl.BlockSpec(memory_space=pl.ANY)
    aliases, in_pos, out_pos = {}, n_pre + n_in, n_out
    for ride, ki, ko in zip(rides, r_ins, r_outs):
        if ride.in_place:
            aliases.update({in_pos + k: out_pos + k for k in range(ki)})
        in_pos, out_pos = in_pos + ki, out_pos + ko
    all_scratch = list(scratch_shapes)
    for ride in rides:
        all_scratch += [pltpu.SemaphoreType.DMA((ride.n_sems,))] * 3
    all_in = list(in_specs) + [hbm] * sum(r_ins)
    all_out = list(out_specs) + [hbm] * sum(r_outs)
    shapes = list(out_shape) + [s for r in rides for s in r.out_shapes]
    operands = list(args) + [a for r in rides for a in r.ins]
    if prefetch is None:
        res = pl.pallas_call(
            carried, name=name, grid=grid, out_shape=shapes, in_specs=all_in, out_specs=all_out,
            scratch_shapes=all_scratch, input_output_aliases=aliases, compiler_params=_params(sem))(*operands)
    else:
        res = pl.pallas_call(
            carried, name=name, out_shape=shapes,
            grid_spec=pltpu.PrefetchScalarGridSpec(num_scalar_prefetch=1, grid=grid, in_specs=all_in,
                                                   out_specs=all_out, scratch_shapes=all_scratch),
            input_output_aliases=aliases, compiler_params=_params(sem))(prefetch, *operands)
    split, pos = [], n_out
    for k in r_outs:
        split.append(res[pos:pos + k])
        pos += k
    return res[:n_out], split


def _chips(x, y):
    return [(1 - x, y), (x, 1 - y), (1 - x, 1 - y)]


def _gather_ride(arrs):
    n = len(arrs)

    def copies(ins, outs, send_sems, recv_sems):
        x, y, c = _place()
        me = 4 * x + 2 * y + c
        peers = [(x, y, 1 - c)] + [(*chip, c) for chip in _chips(x, y)]
        return [pltpu.make_async_remote_copy(
            src_ref=ins[a], dst_ref=outs[a].at[me], send_sem=send_sems.at[4 * a + k], recv_sem=recv_sems.at[4 * a + k],
            device_id=p, device_id_type=MESH) for a in range(n) for k, p in enumerate(peers)], me

    def start(ins, outs, send_sems, recv_sems, local_sems):
        cps, me = copies(ins, outs, send_sems, recv_sems)
        for a in range(n):
            pltpu.make_async_copy(ins[a], outs[a].at[me], local_sems.at[a]).start()
        for cp in cps:
            cp.start()

    def finish(ins, outs, send_sems, recv_sems, local_sems):
        cps, me = copies(ins, outs, send_sems, recv_sems)
        for cp in cps:
            cp.wait_recv()
        for cp in cps:
            cp.wait_send()
        for a in range(n):
            pltpu.make_async_copy(ins[a], outs[a].at[me], local_sems.at[a]).wait()

    return _Ride(arrs, [jax.ShapeDtypeStruct((NDEV,) + a.shape, a.dtype) for a in arrs], 4 * n, start, finish)


def _gather_finish_ride(arrs):
    n = len(arrs)

    def copies(outs, send_sems, recv_sems):
        x, y, c = _place()
        cps = []
        for a in range(n):
            for k, chip in enumerate(_chips(x, y)):
                blk = 4 * chip[0] + 2 * chip[1]
                cps.append((pltpu.make_async_remote_copy(
                    src_ref=outs[a].at[blk + c], dst_ref=outs[a].at[blk + c],
                    send_sem=send_sems.at[3 * a + k], recv_sem=recv_sems.at[3 * a + k],
                    device_id=(x, y, 1 - c), device_id_type=MESH),
                    pltpu.make_async_remote_copy(
                    src_ref=outs[a].at[blk + 1 - c], dst_ref=outs[a].at[blk + 1 - c],
                    send_sem=send_sems.at[3 * a + k], recv_sem=recv_sems.at[3 * a + k],
                    device_id=(x, y, 1 - c), device_id_type=MESH)))
        return cps

    def start(ins, outs, send_sems, recv_sems, local_sems):
        for send, _ in copies(outs, send_sems, recv_sems):
            send.start()

    def finish(ins, outs, send_sems, recv_sems, local_sems):
        cps = copies(outs, send_sems, recv_sems)
        for _, recv in cps:
            recv.wait_recv()
        for send, _ in cps:
            send.wait_send()

    return _Ride(arrs, [jax.ShapeDtypeStruct(a.shape, a.dtype) for a in arrs], 3 * n, start, finish, in_place=True)


def _sibling_ride(arrs):
    n = len(arrs)

    def copies(ins, outs, send_sems, recv_sems):
        x, y, c = _place()
        return [pltpu.make_async_remote_copy(
            src_ref=ins[a].at[:, 1 - c] if arrs[a].ndim == 4 else ins[a], dst_ref=outs[a],
            send_sem=send_sems.at[a], recv_sem=recv_sems.at[a],
            device_id=(x, y, 1 - c), device_id_type=MESH) for a in range(n)]

    def start(ins, outs, send_sems, recv_sems, local_sems):
        for cp in copies(ins, outs, send_sems, recv_sems):
            cp.start()

    def finish(ins, outs, send_sems, recv_sems, local_sems):
        for cp in copies(ins, outs, send_sems, recv_sems):
            cp.wait()

    return _Ride(arrs, [jax.ShapeDtypeStruct((4,) + a.shape[-2:], a.dtype) for a in arrs], n, start, finish)


def _pair_sum(mine, theirs, core, name):
    _, pick, r, c = mine.shape
    if pick == 1:
        core = jnp.zeros_like(core)
    tr = _pick(r, 512, 16)

    def body(core_ref, a_ref, b_ref, o_ref):
        o_ref[...] = (a_ref[...].astype(F32) + b_ref[...].astype(F32)).astype(BF16)

    return pl.pallas_call(
        body, name=name,
        grid_spec=pltpu.PrefetchScalarGridSpec(
            num_scalar_prefetch=1, grid=(4, r // tr),
            in_specs=[pl.BlockSpec((None, None, tr, c), lambda i, k, core_ref: (i, core_ref[0], k, 0)),
                      pl.BlockSpec((None, tr, c), lambda i, k, core_ref: (i, k, 0))],
            out_specs=pl.BlockSpec((None, tr, c), lambda i, k, core_ref: (i, k, 0))),
        out_shape=jax.ShapeDtypeStruct((4, r, c), BF16),
        compiler_params=_params(("arbitrary", "arbitrary")),
    )(core, mine, theirs)


def _chip_exchange_ride(arrs):
    n = len(arrs)

    def copies(ins, outs, send_sems, recv_sems):
        x, y, c = _place()
        mine = 2 * x + y
        return [pltpu.make_async_remote_copy(
            src_ref=ins[a].at[2 * chip[0] + chip[1]], dst_ref=outs[a].at[mine],
            send_sem=send_sems.at[3 * a + k], recv_sem=recv_sems.at[3 * a + k],
            device_id=(*chip, c), device_id_type=MESH) for a in range(n) for k, chip in enumerate(_chips(x, y))], mine

    def start(ins, outs, send_sems, recv_sems, local_sems):
        cps, mine = copies(ins, outs, send_sems, recv_sems)
        for a in range(n):
            pltpu.make_async_copy(ins[a].at[mine], outs[a].at[mine], local_sems.at[a]).start()
        for cp in cps:
            cp.start()

    def finish(ins, outs, send_sems, recv_sems, local_sems):
        cps, mine = copies(ins, outs, send_sems, recv_sems)
        for cp in cps:
            cp.wait_recv()
        for cp in cps:
            cp.wait_send()
        for a in range(n):
            pltpu.make_async_copy(ins[a].at[mine], outs[a].at[mine], local_sems.at[a]).wait()

    return _Ride(arrs, [jax.ShapeDtypeStruct(a.shape, a.dtype) for a in arrs], 3 * n, start, finish)


def _ada_matmul(c_all, w_loc, b_loc):
    d, n = w_loc.shape
    bn = _pick(n, 512)

    def body(c_ref, w_ref, b_ref, o_ref):
        s = _silu(c_ref[...]).astype(BF16)
        o_ref[...] = _dot(s, w_ref[...].astype(BF16), NN) + b_ref[...]

    return pl.pallas_call(
        body, name="ada_matmul", grid=(n // bn,),
        out_shape=jax.ShapeDtypeStruct((NDEV, n), F32),
        in_specs=[pl.BlockSpec((NDEV, d), lambda j: (0, 0)), pl.BlockSpec((d, bn), lambda j: (0, j)),
                  pl.BlockSpec((1, bn), lambda j: (0, j))],
        out_specs=pl.BlockSpec((NDEV, bn), lambda j: (0, j)),
        compiler_params=_params(("arbitrary",)),
    )(c_all, w_loc, b_loc)


def _modulated_norm(x, norm_g, mod):
    t, d = x.shape
    tt = _pick(t, 256, 16)

    def body(x_ref, g_ref, mod_ref, h_ref):
        xv = x_ref[...]
        r = lax.rsqrt(jnp.mean(xv * xv, axis=-1, keepdims=True) + EPS)
        h = (xv * r) * g_ref[...] * (1.0 + mod_ref[1:2, :]) + mod_ref[0:1, :]
        h_ref[...] = h.astype(BF16)

    return pl.pallas_call(
        body, name="modulated_norm", grid=(t // tt,),
        out_shape=jax.ShapeDtypeStruct((t, d), BF16),
        in_specs=[pl.BlockSpec((tt, d), lambda i: (i, 0)), pl.BlockSpec((1, d), lambda i: (0, 0)),
                  pl.BlockSpec((3, d), lambda i: (0, 0))],
        out_specs=pl.BlockSpec((tt, d), lambda i: (i, 0)),
        compiler_params=_params(("arbitrary",)),
    )(x, norm_g, mod)


def _gather_proj(h, w_loc):
    t, d = h.shape
    ns = w_loc.shape[1]
    tm = _pick(t, 512, 16)
    nm = t // tm
    idx = lambda p: 4 * p[0] + 2 * p[1] + p[2]

    def peers():
        x, y, c = _place()
        flip = lambda a, b: a + b - 2 * a * b
        near, far = (flip(x, c), flip(y, 1 - c)), (flip(x, 1 - c), flip(y, c))
        return (x, y, c), (x, y, 1 - c), near, far, (1 - x, 1 - y), c

    me, sibling, near, far, diag, c = peers()
    order = [me, sibling, (*near, c), (*far, 1 - c), (*far, c), (*near, 1 - c), (*diag, c), (*diag, 1 - c)]
    order = jnp.stack([idx(p) for p in order]).astype(jnp.int32)

    def body(order_ref, a_ref, w_ref, wg_ref, o_ref, slab, send_sems, recv_sems, local_sems):
        j, m = pl.program_id(0), pl.program_id(1)
        me, sibling, near, far, diag, c = peers()

        def rows(p):
            return wg_ref.at[:, pl.ds(pl.multiple_of(idx(p) * ns, LANES), ns)]

        def copy(k, block, to, src=None):
            return pltpu.make_async_remote_copy(
                src_ref=rows(block) if src is None else src, dst_ref=rows(block),
                send_sem=send_sems.at[k], recv_sem=recv_sems.at[k], device_id=to, device_id_type=MESH)

        def load(src):
            cp = pltpu.make_async_copy(src, slab, local_sems.at[1])
            cp.start()
            cp.wait()

        keep = pltpu.make_async_copy(w_ref, rows(me), local_sems.at[0])
        own = [copy(0, me, sibling, src=w_ref), copy(1, me, (*near, c), src=w_ref), copy(2, me, (*far, c), src=w_ref)]
        relay = copy(3, (*near, c), (*far, c))
        passed = [copy(4, (*near, c), sibling), copy(5, (*far, c), sibling), copy(6, (*diag, c), sibling)]
        arrivals = [(1, 0, sibling, []), (2, 1, (*near, c), [passed[0], own[2], relay]), (3, 4, (*far, 1 - c), []),
                    (4, 2, (*far, c), [passed[1]]), (5, 5, (*near, 1 - c), []),
                    (6, 3, (*diag, c), [passed[2]]), (7, 6, (*diag, 1 - c), [])]

        @pl.when((j == 0) & (m == 0))
        def _():
            keep.start()
            for cp in own[:2]:
                cp.start()
            load(w_ref)

        for step, sem, block, onward in arrivals:
            @pl.when((j == step) & (m == 0))
            def _(sem=sem, block=block, onward=onward):
                copy(sem, block, me).wait_recv()
                for cp in onward:
                    cp.start()
                load(rows(block))

        o_ref[...] = _dot(a_ref[...], slab[...], NN)

        @pl.when((j == NDEV - 1) & (m == nm - 1))
        def _():
            for cp in own + [relay] + passed:
                cp.wait_send()
            keep.wait()

    hbm = pl.BlockSpec(memory_space=pl.ANY)
    return pl.pallas_call(
        body, name="gather_proj",
        grid_spec=pltpu.PrefetchScalarGridSpec(
            num_scalar_prefetch=1, grid=(NDEV, nm),
            in_specs=[pl.BlockSpec((tm, d), lambda j, m, order_ref: (m, 0)), hbm],
            out_specs=[hbm, pl.BlockSpec((tm, ns), lambda j, m, order_ref: (m, order_ref[j]))],
            scratch_shapes=[pltpu.VMEM((d, ns), BF16), pltpu.SemaphoreType.DMA((7,)), pltpu.SemaphoreType.DMA((7,)),
                            pltpu.SemaphoreType.DMA((2,))]),
        out_shape=[jax.ShapeDtypeStruct((d, NDEV * ns), BF16), jax.ShapeDtypeStruct((t, NDEV * ns), F32)],
        compiler_params=_params(("arbitrary", "arbitrary")),
    )(order, h, w_loc)


def _attention_fwd(proj, qg, kg, nh, d_model, ride):
    t = proj.shape[0]
    tq = _pick(t, 256, 16)
    nq = t // tq
    assert 2 <= nq <= LANES
    group = 2 if nq % 2 == 0 else 1
    scale = HEAD_DIM ** -0.5

    def body(q_ref, k_ref, v_ref, g_ref, qg_ref, kg_ref, o_ref, tot_ref, y_ref, first_ref, qn, kn, vb):
        def norm(src, gain, dst):
            v = src[...]
            r = lax.rsqrt(jnp.mean(v * v, axis=-1, keepdims=True) + EPS)
            dst[...] = ((v * r) * gain[...]).astype(BF16)

        norm(q_ref, qg_ref, qn)
        norm(k_ref, kg_ref, kn)
        vb[...] = v_ref[...].astype(BF16)
        def after_matrix(n):
            return (lax.broadcasted_iota(jnp.int32, (n, n), 0) > lax.broadcasted_iota(jnp.int32, (n, n), 1)).astype(BF16)

        upper = {tq: after_matrix(tq), 2 * tq: after_matrix(2 * tq)}

        def block(qi, start, width, carry, acc, q_start=None):
            ks = pl.ds(pl.multiple_of(start, tq), width)
            z = _dot(qi, kn[ks, :], NT) * scale
            sp = _softplus(z)
            ls = -sp
            if q_start is not None:
                causal = (start + lax.broadcasted_iota(jnp.int32, (tq, width), 1)
                          < q_start + lax.broadcasted_iota(jnp.int32, (tq, width), 0))
                ls = jnp.where(causal, ls, 0.0)
            hi = ls.astype(BF16)
            lo = (ls - hi.astype(F32)).astype(BF16)
            after = _dot(hi, upper[width], NN) + _dot(lo, upper[width], NN)
            w = jnp.exp(z - sp + after + carry)
            if q_start is not None:
                w = jnp.where(causal, w, 0.0)
            acc = acc + _dot(w.astype(BF16), vb[ks, :], NN)
            carry = carry + jnp.sum(ls, axis=1, keepdims=True)
            return carry, acc

        lane = lax.broadcasted_iota(jnp.int32, (8, LANES), 1)

        def live(carry):
            return (jnp.max(carry) > DEAD_LOG_WEIGHT).astype(jnp.int32)

        def wide_step(i):
            qi = qn[pl.ds(pl.multiple_of(i * tq, tq), tq), :]
            left = jnp.maximum(i - 1, 0)
            return block(qi, left * tq, 2 * tq, jnp.zeros((tq, 1), F32), jnp.zeros((tq, HEAD_DIM), F32), i * tq)

        def finish(i, carry, acc, firsts):
            qs = pl.ds(pl.multiple_of(i * tq, tq), tq)
            qi = qn[qs, :]
            left = jnp.maximum(i - 1, 0)

            def k_step(st):
                ca, ac = block(qi, (left - 1 - st[0]) * tq, tq, st[1], st[2])
                return st[0] + 1, ca, ac, live(ca)

            done, carry, acc, _ = lax.while_loop(
                lambda st: (st[0] < left) & (st[3] > 0), k_step, (jnp.int32(0), carry, acc, live(carry)))
            o_ref[qs, :] = acc
            tot_ref[qs, :] = jnp.broadcast_to(carry, (tq, HEAD_DIM))
            y_ref[qs, :] = (acc * _silu(g_ref[qs, :])).astype(BF16)
            return jnp.where(lane == i, (left - done).astype(F32), firsts)

        def q_group(p, firsts):
            blocks = [p + b * (nq // group) for b in range(group)]
            swept = [wide_step(i) for i in blocks]
            for i, (carry, acc) in zip(blocks, swept):
                firsts = finish(i, carry, acc, firsts)
            return firsts

        first_ref[...] = lax.fori_loop(0, nq // group, q_group, jnp.zeros((8, LANES), F32))

    col_block = lambda off: pl.BlockSpec((t, HEAD_DIM), lambda h: (0, off + h))
    vec = pl.BlockSpec((1, HEAD_DIM), lambda h: (0, 0))
    return _call(
        body, name="attention_fwd", grid=(nh,),
        out_shape=[jax.ShapeDtypeStruct((t, nh * HEAD_DIM), F32), jax.ShapeDtypeStruct((t, nh * HEAD_DIM), F32),
                   jax.ShapeDtypeStruct((t, d_model), BF16), jax.ShapeDtypeStruct((nh, 8, LANES), F32)],
        in_specs=[col_block(0), col_block(nh), col_block(2 * nh), col_block(3 * nh), vec, vec],
        out_specs=[col_block(0), col_block(0), col_block(0), pl.BlockSpec((None, 8, LANES), lambda h: (h, 0, 0))],
        scratch_shapes=[pltpu.VMEM((t, HEAD_DIM), BF16)] * 3,
        args=(proj, proj, proj, proj, qg, kg), rides=[ride])


def _fill_shifts(shifted, n_rows):
    for b in range(1, SUBLANES):
        shifted[b, pl.ds(0, n_rows), :] = shifted[0, pl.ds(b, n_rows), :]


def _shifted_rows(shifted, offset, n_rows, lo, cw):
    a, b = divmod(offset, SUBLANES)
    return shifted[b, pl.ds(SUBLANES * a, n_rows), pl.ds(lo, cw)]


def _conv_taps(shifted, w_ref, kc, lo, n_rows, cw, first_tap_row):
    acc = None
    for j in range(kc):
        term = w_ref[j:j + 1, lo:lo + cw] * _shifted_rows(shifted, first_tap_row(j), n_rows, lo, cw)
        acc = term if acc is None else acc + term
    return acc


def _glu_rows(u_ref, g_ref):
    return u_ref[...] * _sigmoid(g_ref[...])


def _conv_fwd(proj, w_dw, b_dw, ln_g, ln_b, kc, cw, ride):
    t = proj.shape[0]
    tt = _pick(t, 128, HALO)
    per = tt // HALO
    chunk = _pick(cw, 256)

    def body(u_ref, g_ref, up_ref, gp_ref, w_ref, b_ref, lg_ref, lb_ref, a_ref, hc_ref, buf):
        i = pl.program_id(0)
        buf[0, pl.ds(HALO, tt), :] = _glu_rows(u_ref, g_ref)
        halo = _glu_rows(up_ref, gp_ref)
        buf[0, pl.ds(0, HALO), :] = jnp.where(i > 0, halo, 0.0)
        _fill_shifts(buf, tt + HALO - SUBLANES)
        for lo in range(0, cw, chunk):
            conv = _conv_taps(buf, w_ref, kc, lo, tt, chunk, lambda j: HALO - (kc - 1) + j)
            hc_ref[:, lo:lo + chunk] = conv + b_ref[:, lo:lo + chunk]
        hc = hc_ref[...]
        mu = jnp.mean(hc, axis=-1, keepdims=True)
        xc = hc - mu
        var = jnp.mean(xc * xc, axis=-1, keepdims=True)
        ln = xc * lax.rsqrt(var + EPS) * lg_ref[...] + lb_ref[...]
        a_ref[...] = _silu(ln).astype(BF16)

    ncol = proj.shape[1] // cw
    tile = lambda g: pl.BlockSpec((tt, cw), lambda i: (i, g))
    prev = lambda g: pl.BlockSpec((HALO, cw), lambda i: (jnp.maximum(i * per - 1, 0), g))
    full = lambda r: pl.BlockSpec((r, cw), lambda i: (0, 0))
    return _call(
        body, name="conv_fwd", grid=(t // tt,),
        out_shape=[jax.ShapeDtypeStruct((t, cw), BF16), jax.ShapeDtypeStruct((t, cw), F32)],
        in_specs=[tile(ncol - 3), tile(ncol - 2), prev(ncol - 3), prev(ncol - 2),
                  full(w_dw.shape[0]), full(1), full(1), full(1)],
        out_specs=[pl.BlockSpec((tt, cw), lambda i: (i, 0))] * 2,
        scratch_shapes=[pltpu.VMEM((SUBLANES, HALO + tt, cw), F32)],
        args=(proj, proj, proj, proj, w_dw, b_dw, ln_g, ln_b), rides=[ride])


def _pointwise_fwd(a, wpw, b_pw, proj, ycat, cw):
    t = a.shape[0]
    tm = _pick(t, 256, 16)
    ncol = proj.shape[1] // cw
    ycol = ycat.shape[1] // cw - 1

    def body(a_ref, w_ref, b_ref, g_ref, y_in, z_ref, y_ref):
        z = _dot(a_ref[...], w_ref[...], NN) + b_ref[...]
        z_ref[...] = z
        y_ref[...] = (z * _silu(g_ref[...])).astype(BF16)

    return pl.pallas_call(
        body, name="pointwise_fwd", grid=(t // tm,),
        out_shape=[jax.ShapeDtypeStruct((t, cw), F32), jax.ShapeDtypeStruct(ycat.shape, ycat.dtype)],
        in_specs=[pl.BlockSpec((tm, cw), lambda m: (m, 0)),
                  pl.BlockSpec((cw, cw), lambda m: (0, 0)),
                  pl.BlockSpec((1, cw), lambda m: (0, 0)),
                  pl.BlockSpec((tm, cw), lambda m: (m, ncol - 1)),
                  pl.BlockSpec(memory_space=pl.ANY)],
        out_specs=[pl.BlockSpec((tm, cw), lambda m: (m, 0)), pl.BlockSpec((tm, cw), lambda m: (m, ycol))],
        input_output_aliases={4: 1},
        compiler_params=_params(("arbitrary",)),
    )(a, wpw, b_pw, proj, ycat)


def _out_matmul(ycat, wout, x, target, mod):
    t, d = x.shape
    kdim = wout.shape[0]
    tm, tn = _pick(t, 512, 16), _pick(d, 512)
    inv_d = 1.0 / d

    def body(a_ref, w_ref, x_ref, tg_ref, mod_ref, dout_ref, dy_ref, sums_ref):
        @pl.when(pl.program_id(1) == 0)
        def _():
            sums_ref[...] = jnp.zeros_like(sums_ref)

        y = _dot(a_ref[...], w_ref[...], NN)
        gate = mod_ref[2:3, :]
        err = (x_ref[...] + gate * y) - tg_ref[...]
        dout = err * inv_d
        dout_ref[...] = dout
        dy_ref[...] = (dout * gate).astype(BF16)
        sums_ref[0:1, :] += jnp.sum(dout * y, axis=0, keepdims=True)
        sums_ref[1:2, :] += jnp.sum(err * err, axis=0, keepdims=True)

    mn = lambda n, m: (m, n)
    return pl.pallas_call(
        body, name="out_matmul", grid=(d // tn, t // tm),
        out_shape=[jax.ShapeDtypeStruct((t, d), F32), jax.ShapeDtypeStruct((t, d), BF16),
                   jax.ShapeDtypeStruct((8, d), F32)],
        in_specs=[pl.BlockSpec((tm, kdim), lambda n, m: (m, 0)),
                  pl.BlockSpec((kdim, tn), lambda n, m: (0, n)),
                  pl.BlockSpec((tm, tn), mn), pl.BlockSpec((tm, tn), mn),
                  pl.BlockSpec((3, tn), lambda n, m: (0, n))],
        out_specs=[pl.BlockSpec((tm, tn), mn), pl.BlockSpec((tm, tn), mn),
                   pl.BlockSpec((8, tn), lambda n, m: (0, n))],
        compiler_params=_params(("arbitrary", "arbitrary")),
    )(ycat, wout, x, target, mod)


def _mm(a, b, form, name, tm, tn, out_dtype, *, slabs=False, n_outer=False, ksplit=1, every_other=None, rides=()):
    if form == TN:
        kdim, m_dim = a.shape
    else:
        m_dim, kdim = a.shape
    n_dim = (b.shape[0] if form == NT else b.shape[1]) // (1 if every_other is None else 2)
    tk = kdim // ksplit
    gm, gn = m_dim // tm, n_dim // tn
    mn = (lambda g: (g[1], g[0])) if n_outer else (lambda g: (g[0], g[1]))
    b_col = (lambda g: mn(g)[1]) if every_other is None else (lambda g: 2 * mn(g)[1] + g[3][0])
    a_map = (lambda *g: (g[2], mn(g)[0])) if form == TN else (lambda *g: (mn(g)[0], g[2]))
    b_map = (lambda *g: (b_col(g), g[2])) if form == NT else (lambda *g: (g[2], b_col(g)))
    a_blk = (tk, tm) if form == TN else (tm, tk)
    b_blk = (tn, tk) if form == NT else (tk, tn)
    if slabs:
        out_shape = jax.ShapeDtypeStruct((gn, m_dim, tn), out_dtype)
        out_spec = pl.BlockSpec((None, tm, tn), lambda *g: (mn(g)[1], mn(g)[0], 0))
    else:
        out_shape = jax.ShapeDtypeStruct((m_dim, n_dim), out_dtype)
        out_spec = pl.BlockSpec((tm, tn), lambda *g: mn(g))

    def body(*refs):
        a_ref, b_ref, o_ref, *acc = refs if every_other is None else refs[1:]
        part = _dot(a_ref[...], b_ref[...], form)
        if ksplit == 1:
            o_ref[...] = part.astype(out_dtype)
            return
        k = pl.program_id(2)

        @pl.when(k == 0)
        def _():
            acc[0][...] = part

        @pl.when((k > 0) & (k < ksplit - 1))
        def _():
            acc[0][...] += part

        @pl.when(k == ksplit - 1)
        def _():
            o_ref[...] = (acc[0][...] + part).astype(out_dtype)

    return _call(
        body, name=name, grid=((gn, gm) if n_outer else (gm, gn)) + (ksplit,),
        out_shape=[out_shape], in_specs=[pl.BlockSpec(a_blk, a_map), pl.BlockSpec(b_blk, b_map)],
        out_specs=[out_spec], scratch_shapes=[pltpu.VMEM((tm, tn), F32)] if ksplit > 1 else [],
        args=(a, b), rides=rides, prefetch=every_other)


def _attention_bwd(proj, o, tot, firsts, dycat, qg, kg, nh):
    t, in_cols = proj.shape
    tq = _pick(t, 256, 16)
    nq = t // tq
    group = 2 if nq % 2 == 0 else 1
    scale = HEAD_DIM ** -0.5

    def body(q_ref, k_ref, v_ref, g_ref, o_ref, tot_ref, first_ref, dy_ref, qg_ref, kg_ref, dproj_ref, gains_ref,
             qn, kn, vb, dob, dk_acc, dv_acc, dq_acc, outs, sems):
        h = pl.program_id(0)

        def norm(src, gain, dst):
            v = src[...]
            r = lax.rsqrt(jnp.mean(v * v, axis=-1, keepdims=True) + EPS)
            dst[...] = ((v * r) * gain[...]).astype(BF16)

        norm(q_ref, qg_ref, qn)
        norm(k_ref, kg_ref, kn)
        vb[...] = v_ref[...].astype(BF16)
        gs = g_ref[...]
        dyv = dy_ref[...]
        dob[...] = (dyv * _silu(gs)).astype(BF16)
        outs[3] = (dyv * o_ref[...] * _dsilu(gs)).astype(BF16)
        dk_acc[...] = jnp.zeros_like(dk_acc)
        dv_acc[...] = jnp.zeros_like(dv_acc)

        def before_matrix(n, strict):
            r = lax.broadcasted_iota(jnp.int32, (n, n), 0)
            c = lax.broadcasted_iota(jnp.int32, (n, n), 1)
            return ((c < r) if strict else (c <= r)).astype(BF16)

        incl = {n: before_matrix(n, False) for n in (tq, 2 * tq)}
        excl = {n: before_matrix(n, True) for n in (tq, 2 * tq)}
        lane = lax.broadcasted_iota(jnp.int32, (1, LANES), 1)

        def block(start, width, qi, doi, tot_row, p_left, g_left, dq, q_start=None):
            ks = pl.ds(pl.multiple_of(start, tq), width)
            kj = kn[ks, :]
            z = _dot(kj, qi, NT) * scale
            sp = _softplus(z)
            ls = -sp
            if q_start is not None:
                causal = (start + lax.broadcasted_iota(jnp.int32, (width, tq), 0)
                          < q_start + lax.broadcasted_iota(jnp.int32, (width, tq), 1))
                ls = jnp.where(causal, ls, 0.0)
            hi = ls.astype(BF16)
            lo = (ls - hi.astype(F32)).astype(BF16)
            p_inc = _dot(incl[width], hi, NN) + _dot(incl[width], lo, NN) + p_left
            beta = jnp.exp(z - sp)
            w = beta * jnp.exp(tot_row - p_inc)
            if q_start is not None:
                w = jnp.where(causal, w, 0.0)
            dw = _dot(vb[ks, :], doi, NT)
            g = w * dw
            g_before = _dot(excl[width], g.astype(BF16), NN) + g_left
            dz = g * (1.0 - beta) - beta * g_before
            if q_start is not None:
                dz = jnp.where(causal, dz, 0.0)
            dzb = dz.astype(BF16)
            dv_acc[ks, :] += _dot(w.astype(BF16), doi, NN)
            dk_acc[ks, :] += _dot(dzb, qi, NN)
            dq = dq + _dot(dzb, kj, TN)
            p_left = p_left + jnp.sum(ls, axis=0, keepdims=True)
            g_left = g_left + jnp.sum(g, axis=0, keepdims=True)
            return p_left, g_left, dq

        def operands(i):
            qs = pl.ds(pl.multiple_of(i * tq, tq), tq)
            return qn[qs, :], dob[qs, :], jnp.transpose(tot_ref[qs, :])[0:1, :]

        def singles(i):
            qi, doi, tot_row = operands(i)
            zero_row = jnp.zeros((1, tq), F32)

            def k_step(j, carry):
                return block(j * tq, tq, qi, doi, tot_row, carry[0], carry[1], carry[2])

            left = jnp.maximum(i - 1, 0)
            first = jnp.sum(jnp.where(lane == i, first_ref[0:1, :], 0.0)).astype(jnp.int32)
            first = jnp.clip(first, 0, left)
            return lax.fori_loop(first, left, k_step, (zero_row, zero_row, jnp.zeros((tq, HEAD_DIM), F32)))

        def wide_step(i, carry):
            qi, doi, tot_row = operands(i)
            left = jnp.maximum(i - 1, 0)
            _, _, dq = block(left * tq, 2 * tq, qi, doi, tot_row, carry[0], carry[1], carry[2], i * tq)
            dq_acc[pl.ds(pl.multiple_of(i * tq, tq), tq), :] = dq * scale

        def q_group(p, _):
            blocks = [p + b * (nq // group) for b in range(group)]
            carries = [singles(i) for i in blocks]
            for i, carry in zip(blocks, carries):
                wide_step(i, carry)
            return 0

        lax.fori_loop(0, nq // group, q_group, 0)

        def norm_bwd(src, gain, dn, slot, gain_row):
            v = src[...]
            r = lax.rsqrt(jnp.mean(v * v, axis=-1, keepdims=True) + EPS)
            vhat = v * r
            gains_ref[gain_row:gain_row + 1, :] = jnp.sum(dn * vhat, axis=0, keepdims=True)
            dhat = dn * gain[...]
            outs[slot] = (r * (dhat - vhat * jnp.mean(dhat * vhat, axis=-1, keepdims=True))).astype(BF16)

        gains_ref[...] = jnp.zeros_like(gains_ref)
        norm_bwd(q_ref, qg_ref, dq_acc[...], 0, 0)
        norm_bwd(k_ref, kg_ref, dk_acc[...] * scale, 1, 1)
        outs[2] = dv_acc[...].astype(BF16)
        copies = [pltpu.make_async_copy(
            outs.at[s], dproj_ref.at[:, pl.ds(pl.multiple_of((s * nh + h) * HEAD_DIM, HEAD_DIM), HEAD_DIM)], sems.at[s])
            for s in range(4)]
        for cp in copies:
            cp.start()
        for cp in copies:
            cp.wait()

    col_block = lambda off: pl.BlockSpec((t, HEAD_DIM), lambda h: (0, off + h))
    vec = pl.BlockSpec((1, HEAD_DIM), lambda h: (0, 0))
    head_scr = lambda dt: pltpu.VMEM((t, HEAD_DIM), dt)
    return pl.pallas_call(
        body, name="attention_bwd", grid=(nh,),
        out_shape=[jax.ShapeDtypeStruct((t, in_cols), BF16), jax.ShapeDtypeStruct((nh, 8, HEAD_DIM), F32)],
        in_specs=[col_block(0), col_block(nh), col_block(2 * nh), col_block(3 * nh),
                  col_block(0), col_block(0), pl.BlockSpec((None, 8, LANES), lambda h: (h, 0, 0)), col_block(0), vec, vec],
        out_specs=[pl.BlockSpec(memory_space=pl.ANY), pl.BlockSpec((None, 8, HEAD_DIM), lambda h: (h, 0, 0))],
        scratch_shapes=[head_scr(BF16), head_scr(BF16), head_scr(BF16), head_scr(BF16),
                        head_scr(F32), head_scr(F32), head_scr(F32),
                        pltpu.VMEM((4, t, HEAD_DIM), BF16), pltpu.SemaphoreType.DMA((4,))],
        compiler_params=_params(("arbitrary",)),
    )(proj, proj, proj, proj, o, tot, firsts, dycat, qg, kg)


def _gate_bwd(dycat, z, proj, dproj, cw):
    t = z.shape[0]
    tt = _pick(t, 256, 16)
    ncol = proj.shape[1] // cw
    ycol = dycat.shape[1] // cw - 1

    def body(dy_ref, z_ref, g_ref, dp_in, dz_ref, dp_ref, sums_ref):
        i = pl.program_id(0)

        @pl.when(i == 0)
        def _():
            sums_ref[...] = jnp.zeros_like(sums_ref)

        g = g_ref[...]
        dy = dy_ref[...]
        dz = dy * _silu(g)
        dz_ref[...] = dz.astype(BF16)
        dp_ref[...] = (dy * z_ref[...] * _dsilu(g)).astype(BF16)
        sums_ref[0:1, :] += jnp.sum(dz, axis=0, keepdims=True)

    return pl.pallas_call(
        body, name="gate_bwd", grid=(t // tt,),
        out_shape=[jax.ShapeDtypeStruct((t, cw), BF16), jax.ShapeDtypeStruct(dproj.shape, dproj.dtype),
                   jax.ShapeDtypeStruct((8, cw), F32)],
        in_specs=[pl.BlockSpec((tt, cw), lambda i: (i, ycol)), pl.BlockSpec((tt, cw), lambda i: (i, 0)),
                  pl.BlockSpec((tt, cw), lambda i: (i, ncol - 1)), pl.BlockSpec(memory_space=pl.ANY)],
        out_specs=[pl.BlockSpec((tt, cw), lambda i: (i, 0)), pl.BlockSpec((tt, cw), lambda i: (i, ncol - 1)),
                   pl.BlockSpec((8, cw), lambda i: (0, 0))],
        input_output_aliases={3: 1},
        compiler_params=_params(("arbitrary",)),
    )(dycat, z, proj, dproj)


def _ln_bwd(da, hc, ln_g, ln_b):
    t, cw = hc.shape
    tt = _pick(t, 256, 16)

    def body(da_ref, hc_ref, lg_ref, lb_ref, dh_ref, sums_ref):
        i = pl.program_id(0)

        @pl.when(i == 0)
        def _():
            sums_ref[...] = jnp.zeros_like(sums_ref)

        hcv = hc_ref[...]
        mu = jnp.mean(hcv, axis=-1, keepdims=True)
        xc = hcv - mu
        r = lax.rsqrt(jnp.mean(xc * xc, axis=-1, keepdims=True) + EPS)
        xhat = xc * r
        ln = xhat * lg_ref[...] + lb_ref[...]
        dln = da_ref[...] * _dsilu(ln)
        dxhat = dln * lg_ref[...]
        dhc = r * (dxhat - jnp.mean(dxhat, axis=-1, keepdims=True)
                   - xhat * jnp.mean(dxhat * xhat, axis=-1, keepdims=True))
        dh_ref[...] = dhc
        sums_ref[0:1, :] += jnp.sum(dln * xhat, axis=0, keepdims=True)
        sums_ref[1:2, :] += jnp.sum(dln, axis=0, keepdims=True)
        sums_ref[2:3, :] += jnp.sum(dhc, axis=0, keepdims=True)

    tile = pl.BlockSpec((tt, cw), lambda i: (i, 0))
    vec = pl.BlockSpec((1, cw), lambda i: (0, 0))
    return pl.pallas_call(
        body, name="ln_bwd", grid=(t // tt,),
        out_shape=[jax.ShapeDtypeStruct((t, cw), F32), jax.ShapeDtypeStruct((8, cw), F32)],
        in_specs=[tile, tile, vec, vec],
        out_specs=[tile, pl.BlockSpec((8, cw), lambda i: (0, 0))],
        compiler_params=_params(("arbitrary",)),
    )(da, hc, ln_g, ln_b)


def _conv_bwd(dhc, proj, w_dw, dproj, kc, cw):
    t = proj.shape[0]
    tt = _pick(t, 128, HALO)
    per = tt // HALO
    nt = t // tt
    chunk = _pick(cw, 256)
    ncol = proj.shape[1] // cw
    wr = w_dw.shape[0]

    def body(d_ref, dn_ref, u_ref, g_ref, up_ref, gp_ref, w_ref, dp_in, dp_ref, dw_ref, dbuf, hbuf, dw_acc):
        i = pl.program_id(0)

        @pl.when(i == 0)
        def _():
            dw_acc[...] = jnp.zeros_like(dw_acc)

        dbuf[0, pl.ds(0, tt), :] = d_ref[...]
        dbuf[0, pl.ds(tt, HALO), :] = jnp.where(i < nt - 1, dn_ref[...], 0.0)
        hbuf[0, pl.ds(HALO, tt), :] = _glu_rows(u_ref, g_ref)
        hbuf[0, pl.ds(0, HALO), :] = jnp.where(i > 0, _glu_rows(up_ref, gp_ref), 0.0)
        _fill_shifts(dbuf, tt + HALO - SUBLANES)
        _fill_shifts(hbuf, tt + HALO - SUBLANES)
        for lo in range(0, cw, chunk):
            dhg = _conv_taps(dbuf, w_ref, kc, lo, tt, chunk, lambda j: (kc - 1) - j)
            u = u_ref[:, lo:lo + chunk]
            sg = _sigmoid(g_ref[:, lo:lo + chunk])
            dp_ref[:, lo:lo + chunk] = (dhg * sg).astype(BF16)
            dp_ref[:, cw + lo:cw + lo + chunk] = (dhg * u * sg * (1.0 - sg)).astype(BF16)
            dtile = d_ref[:, lo:lo + chunk]
            for j in range(kc):
                prod = dtile * _shifted_rows(hbuf, HALO - (kc - 1) + j, tt, lo, chunk)
                dw_acc[j, :, lo:lo + chunk] += jnp.sum(prod.reshape(tt // SUBLANES, SUBLANES, chunk), axis=0)

        @pl.when(i == nt - 1)
        def _():
            dw_ref[...] = jnp.sum(dw_acc[...], axis=1)

    tile = lambda g: pl.BlockSpec((tt, cw), lambda i: (i, g))
    prev = lambda g: pl.BlockSpec((HALO, cw), lambda i: (jnp.maximum(i * per - 1, 0), g))
    return pl.pallas_call(
        body, name="conv_bwd", grid=(nt,),
        out_shape=[jax.ShapeDtypeStruct(dproj.shape, dproj.dtype), jax.ShapeDtypeStruct((wr, cw), F32)],
        in_specs=[pl.BlockSpec((tt, cw), lambda i: (i, 0)),
                  pl.BlockSpec((HALO, cw), lambda i: (jnp.minimum((i + 1) * per, nt * per - 1), 0)),
                  tile(ncol - 3), tile(ncol - 2), prev(ncol - 3), prev(ncol - 2),
                  pl.BlockSpec((wr, cw), lambda i: (0, 0)), pl.BlockSpec(memory_space=pl.ANY)],
        out_specs=[pl.BlockSpec((tt, 2 * cw), lambda i: (i, (ncol - 3) // 2)),
                   pl.BlockSpec((wr, cw), lambda i: (0, 0))],
        scratch_shapes=[pltpu.VMEM((SUBLANES, tt + HALO, cw), F32), pltpu.VMEM((SUBLANES, HALO + tt, cw), F32),
                        pltpu.VMEM((wr, SUBLANES, cw), F32)],
        input_output_aliases={7: 0},
        compiler_params=_params(("arbitrary",)),
    )(dhc, dhc, proj, proj, proj, proj, w_dw, dproj)


def _input_grad(dh, x, dout, norm_g, mod):
    t, d = x.shape
    tt = _pick(t, 128, 16)

    def body(dh_ref, x_ref, do_ref, g_ref, mod_ref, gx_ref, sums_ref):
        i = pl.program_id(0)

        @pl.when(i == 0)
        def _():
            sums_ref[...] = jnp.zeros_like(sums_ref)

        xv = x_ref[...]
        dhv = dh_ref[...]
        r = lax.rsqrt(jnp.mean(xv * xv, axis=-1, keepdims=True) + EPS)
        xn = xv * r
        g = g_ref[...]
        one_scale = 1.0 + mod_ref[1:2, :]
        dxn = dhv * g * one_scale
        gx_ref[...] = do_ref[...] + r * (dxn - xn * jnp.mean(dxn * xn, axis=-1, keepdims=True))
        sums_ref[0:1, :] += jnp.sum(dhv, axis=0, keepdims=True)
        sums_ref[1:2, :] += jnp.sum(dhv * (xn * g), axis=0, keepdims=True)
        sums_ref[2:3, :] += jnp.sum(dhv * one_scale * xn, axis=0, keepdims=True)

    tile = pl.BlockSpec((tt, d), lambda i: (i, 0))
    return pl.pallas_call(
        body, name="input_grad", grid=(t // tt,),
        out_shape=[jax.ShapeDtypeStruct((t, d), F32), jax.ShapeDtypeStruct((8, d), F32)],
        in_specs=[tile, tile, tile, pl.BlockSpec((1, d), lambda i: (0, 0)), pl.BlockSpec((3, d), lambda i: (0, 0))],
        out_specs=[tile, pl.BlockSpec((8, d), lambda i: (0, 0))],
        compiler_params=_params(("arbitrary",)),
    )(dh, x, dout, norm_g, mod)


def _sum_adam(parts, w, m, v, name):
    r, c = w.shape
    n_parts = parts.shape[0]
    tr = _pick(r, 128, 16) if r % 16 == 0 else r
    tc = _pick(c, 2048)

    def body(p_ref, w_ref, m_ref, v_ref, g_ref, d_ref, nm_ref, nv_ref):
        g = p_ref[0].astype(F32)
        for i in range(1, n_parts):
            g = g + p_ref[i].astype(F32)
        d, nm, nv = _adam(w_ref[...], g, m_ref[...], v_ref[...])
        g_ref[...] = g
        d_ref[...] = d
        nm_ref[...] = nm
        nv_ref[...] = nv

    tile = pl.BlockSpec((tr, tc), lambda i, j: (i, j))
    out = jax.ShapeDtypeStruct((r, c), F32)
    return pl.pallas_call(
        body, name=name, grid=(r // tr, c // tc),
        out_shape=[out] * 4,
        in_specs=[pl.BlockSpec((n_parts, tr, tc), lambda i, j: (0, i, j)), tile, tile, tile],
        out_specs=[tile] * 4,
        compiler_params=_params(("arbitrary", "arbitrary")),
    )(parts, w, m, v)


def _ada_grad_adam(s_t, dm, w, m, v):
    d, n = w.shape
    tr = _pick(d, 256, 16)

    def body(s_ref, dm_ref, w_ref, m_ref, v_ref, g_ref, d_ref, nm_ref, nv_ref):
        g = lax.dot_general(s_ref[...], dm_ref[...], (NN, ((), ())), preferred_element_type=F32,
                            precision=lax.Precision.HIGHEST)
        dl, nm, nv = _adam(w_ref[...], g, m_ref[...], v_ref[...])
        g_ref[...] = g
        d_ref[...] = dl
        nm_ref[...] = nm
        nv_ref[...] = nv

    tile = pl.BlockSpec((tr, n), lambda i: (i, 0))
    out = jax.ShapeDtypeStruct((d, n), F32)
    return pl.pallas_call(
        body, name="ada_grad_adam", grid=(d // tr,),
        out_shape=[out] * 4,
        in_specs=[pl.BlockSpec((tr, NDEV), lambda i: (i, 0)), pl.BlockSpec((NDEV, n), lambda i: (0, 0)),
                  tile, tile, tile],
        out_specs=[tile] * 4,
        compiler_params=_params(("arbitrary",)),
    )(s_t, dm, w, m, v)


def _silu_t(c_all):
    n, d = c_all.shape

    def body(c_ref, o_ref):
        o_ref[...] = jnp.transpose(_silu(c_ref[...]))

    return pl.pallas_call(
        body, name="silu_t", out_shape=jax.ShapeDtypeStruct((d, n), F32),
        in_specs=[pl.BlockSpec(memory_space=pltpu.VMEM)], out_specs=pl.BlockSpec(memory_space=pltpu.VMEM),
        compiler_params=pltpu.CompilerParams(vmem_limit_bytes=VMEM_LIMIT),
    )(c_all)


def _rows128(v):
    return v.reshape(-1, LANES)


def _pad_rows(a, rows):
    return jnp.pad(a, ((0, rows - a.shape[0]), (0, 0)))


def kernel(x, c, norm_g, w_ada, b_ada, w_in, q_norm_g, k_norm_g, w_dw, b_dw, ln_g, ln_b, w_pw, b_pw, w_out, loss_target, m_norm_g, m_w_ada, m_b_ada, m_w_in, m_q_norm_g, m_k_norm_g, m_w_dw, m_b_dw, m_ln_g, m_ln_b, m_w_pw, m_b_pw, m_w_out, v_norm_g, v_w_ada, v_b_ada, v_w_in, v_q_norm_g, v_k_norm_g, v_w_dw, v_b_dw, v_ln_g, v_ln_b, v_w_pw, v_b_pw, v_w_out):
    _, t, d = x.shape
    n_ada = w_ada.shape[2]
    ns = w_in.shape[2]
    kc, cwl = w_dw.shape[1], w_dw.shape[2]
    cw = cwl * NDEV
    sb = d - cw
    nh = sb // HEAD_DIM
    assert sb == cw and kc - 1 <= HALO and NDEV * ns == 4 * sb + 3 * cw
    my = 4 * lax.axis_index("x") + 2 * lax.axis_index("y") + lax.axis_index("c")

    x2, tg2 = x[0], loss_target[0]

    wdw_rows = -(-kc // 8) * 8
    wdw_pad = _pad_rows(w_dw[0], wdw_rows)
    pay1 = jnp.concatenate([_rows128(c[0]), _rows128(wdw_pad.reshape(-1))], axis=0)
    (g1,) = _all_gather([pay1], "gather_cond", pltpu.VMEM)
    c_rows = d // LANES
    c_all = g1[:, :c_rows].reshape(NDEV, d)
    wdw_all = g1[:, c_rows:].reshape(NDEV, wdw_rows, cwl).transpose(1, 0, 2).reshape(wdw_rows, cw)

    b_ada_loc = lax.dynamic_slice(b_ada, (0, my * n_ada), (1, n_ada))
    mod_cols = _ada_matmul(c_all, w_ada[0], b_ada_loc)
    (g2,) = _all_gather([mod_cols], "gather_mod", pltpu.VMEM)
    mod_mine = lax.dynamic_index_in_dim(g2, my, axis=1, keepdims=False)
    mod = mod_mine.reshape(3, d)

    core = lax.axis_index("c").astype(jnp.int32).reshape(1)
    h = _modulated_norm(x2, norm_g, mod)
    wfull_in, proj = _gather_proj(h, w_in[0].astype(BF16))
    (o, tot, ycat, firsts), (partly,) = _attention_fwd(
        proj, q_norm_g, k_norm_g, nh, d, _gather_ride([w_out[0].astype(BF16), w_pw[0].astype(BF16)]))
    (a, hc), ((wg_out, wg_pw),) = _conv_fwd(proj, wdw_all, b_dw, ln_g, ln_b, kc, cw, _gather_finish_ride(partly))
    wfull_out, wfull_pw = wg_out.reshape(d, d), wg_pw.reshape(cw, cw)
    z, ycat = _pointwise_fwd(a, wfull_pw, b_pw, proj, ycat, cw)
    dout, dy, out_sums = _out_matmul(ycat, wfull_out, x2, tg2, mod)

    tile = _pick(t, 512, 16)
    (dycat,), _ = _mm(dy, wfull_out, NT, "dycat_matmul", tile, _pick(d, 512), F32)
    (p_wout,), _ = _mm(ycat, dy, TN, "w_out_grad", _pick(d, 512), _pick(d, 1024), BF16)
    p_wout = p_wout.reshape(NDEV, d // NDEV, d)
    dproj, gains = _attention_bwd(proj, o, tot, firsts, dycat, q_norm_g, k_norm_g, nh)
    dz, dproj, dz_sums = _gate_bwd(dycat, z, proj, dproj, cw)
    (da,), _ = _mm(dz, wfull_pw, NT, "da_matmul", tile, cw, F32)
    (p_wpw,), _ = _mm(a, dz, TN, "w_pw_grad", _pick(cw, 512), _pick(cw, 1024), BF16)
    p_wpw = p_wpw.reshape(NDEV, cw // NDEV, cw)
    dhc, ln_sums = _ln_bwd(da, hc, ln_g, ln_b)
    dproj, dwdw = _conv_bwd(dhc, proj, wdw_all, dproj, kc, cw)
    p_wdw = dwdw.reshape(wdw_rows, NDEV, cwl).transpose(1, 0, 2).astype(BF16)

    def pair_sums(mine, theirs, name):
        return [_pair_sum(m, s, core, f"{name}_pair_sum_{i}") for i, (m, s) in enumerate(zip(mine, theirs))]

    lesser = [p.reshape(4, 2, *p.shape[1:]) for p in (p_wout, p_wpw, p_wdw)]
    (p_theirs,), (small_theirs,) = _mm(h, dproj, TN, "w_in_grad_sibling", _pick(d, 256), ns, BF16, slabs=True,
                                       n_outer=True, every_other=1 - core, rides=[_sibling_ride(lesser)])
    q_small = pair_sums(lesser, small_theirs, "small_grads")
    (p_mine,), (win_theirs, (r_wout, r_wpw, r_wdw)) = _mm(
        h, dproj, TN, "w_in_grad_own", _pick(d, 256), ns, BF16, slabs=True, n_outer=True, every_other=core,
        rides=[_sibling_ride([p_theirs]), _chip_exchange_ride(q_small)])
    q_win = pair_sums([p_mine[:, None]], win_theirs, "w_in_grad")
    (dh,), ((r_win,),) = _mm(dproj, wfull_in, NT, "dh_matmul", tile, _pick(d, 512), F32, ksplit=2,
                             rides=[_chip_exchange_ride(q_win)])
    grad_x, in_sums = _input_grad(dh, x2, dout, norm_g, mod)

    dmod = jnp.concatenate([in_sums[0], in_sums[1], out_sums[0]])
    loss_part = 0.5 / d * jnp.sum(out_sums[1].reshape(-1, LANES), axis=0)
    small = [in_sums[2], dmod, jnp.sum(gains[:, 0], axis=0), jnp.sum(gains[:, 1], axis=0),
             ln_sums[2], ln_sums[0], ln_sums[1], dz_sums[0], loss_part]
    sizes = [s.shape[0] for s in small]
    packed = _rows128(jnp.concatenate(small))
    n_rows = -(-packed.shape[0] // 8) * 8
    (g3,) = _all_gather([_pad_rows(packed, n_rows)], "gather_small", pltpu.VMEM)

    def pack_state(names_vals):
        flat = jnp.concatenate([v.reshape(-1) for v in names_vals] + [jnp.zeros((LANES,), F32)])
        return _pad_rows(_rows128(flat), n_rows)

    small_w = pack_state([norm_g, b_ada, q_norm_g, k_norm_g, b_dw, ln_g, ln_b, b_pw])
    small_m = pack_state([m_norm_g, m_b_ada, m_q_norm_g, m_k_norm_g, m_b_dw, m_ln_g, m_ln_b, m_b_pw])
    small_v = pack_state([v_norm_g, v_b_ada, v_q_norm_g, v_k_norm_g, v_b_dw, v_ln_g, v_ln_b, v_b_pw])
    sg, sd, sm, sv = _sum_adam(g3, small_w, small_m, small_v, "small_adam")

    def unpack(p):
        flat = p.reshape(-1)
        outs, off = [], 0
        for n in sizes[:-1]:
            outs.append(flat[off:off + n].reshape(1, n))
            off += n
        return outs, flat[off:off + LANES]

    g_small, loss_lanes = unpack(sg)
    d_small, _ = unpack(sd)
    m_small, _ = unpack(sm)
    v_small, _ = unpack(sv)
    loss = jnp.sum(loss_lanes)

    off = sizes[0]
    dmod_all = g3.reshape(NDEV, -1)[:, off:off + 3 * d]
    dmod_loc = lax.dynamic_slice(dmod_all, (0, my * n_ada), (NDEV, n_ada))
    ada = _ada_grad_adam(_silu_t(c_all), dmod_loc, w_ada[0], m_w_ada[0], v_w_ada[0])
    win = _sum_adam(r_win, w_in[0], m_w_in[0], v_w_in[0], "w_in_adam")
    wout = _sum_adam(r_wout, w_out[0], m_w_out[0], v_w_out[0], "w_out_adam")
    wpw = _sum_adam(r_wpw, w_pw[0], m_w_pw[0], v_w_pw[0], "w_pw_adam")
    wdw_state = [_pad_rows(s[0], wdw_rows) for s in (w_dw, m_w_dw, v_w_dw)]
    wdw = [r[:kc] for r in _sum_adam(r_wdw, *wdw_state, "w_dw_adam")]

    def group(k, small_list):
        s = small_list
        return [s[0], ada[k][None], s[1], win[k][None], s[2], s[3], wdw[k][None], s[4], s[5], s[6],
                wpw[k][None], s[7], wout[k][None]]

    return (loss, grad_x[None], *group(0, g_small), *group(1, d_small), *group(2, m_small), *group(3, v_small))
```

```python
import functools

import jax
import jax.numpy as jnp
from jax import lax
from jax.experimental import pallas as pl
from jax.experimental.pallas import tpu as pltpu

F32 = jnp.float32
BF16 = jnp.bfloat16
NDEV = 8
HEAD_DIM = 128
LANES = 128
SUBLANES = 8
HALO = 32
EPS = 1e-6
DEAD_LOG_WEIGHT = -104.0
VMEM_LIMIT = 56 * 1024 * 1024
MESH = pl.DeviceIdType.MESH

ADAM_LR = 0.001
ADAM_B1 = 0.9
ADAM_B2 = 0.999
ADAM_EPS = 1e-08
ADAM_WD = 0.01
ADAM_STEP = 10


def _params(sem=None):
    return pltpu.CompilerParams(dimension_semantics=sem, vmem_limit_bytes=VMEM_LIMIT)


def _pick(n, pref, unit=LANES):
    best = None
    for d in range(unit, min(n, pref) + 1, unit):
        if n % d == 0:
            best = d
    return best if best is not None else n


def _sigmoid(z):
    return 1.0 / (1.0 + jnp.exp(-z))


def _silu(z):
    return z * _sigmoid(z)


def _dsilu(z):
    s = _sigmoid(z)
    return s * (1.0 + z * (1.0 - s))


def _softplus(z):
    return jnp.maximum(z, 0.0) + jnp.log(1.0 + jnp.exp(-jnp.abs(z)))


def _dot(a, b, dims):
    return lax.dot_general(a, b, (dims, ((), ())), preferred_element_type=F32)


NN = ((1,), (0,))
NT = ((1,), (1,))
TN = ((0,), (0,))


def _adam(w, g, m, v):
    m = ADAM_B1 * m + (1.0 - ADAM_B1) * g
    v = ADAM_B2 * v + (1.0 - ADAM_B2) * (g * g)
    m_hat = m / (1.0 - ADAM_B1 ** ADAM_STEP)
    v_hat = v / (1.0 - ADAM_B2 ** ADAM_STEP)
    delta = -ADAM_LR * (m_hat / (jnp.sqrt(v_hat) + ADAM_EPS) + ADAM_WD * w)
    return delta, m, v


def _place():
    x, y, c = lax.axis_index("x"), lax.axis_index("y"), lax.axis_index("c")
    return x, y, c


def _flip(v, bit):
    return 1 - v if bit else v


def _all_gather(arrs, name, space):
    n = len(arrs)

    def body(*refs):
        ins, outs = refs[:n], refs[n:2 * n]
        send_sems, recv_sems, local_sems = refs[2 * n:]
        x, y, c = _place()
        me, sibling = (x, y, c), (x, y, 1 - c)
        chips = [(1 - x, y), (x, 1 - y), (1 - x, 1 - y)]

        def rows(a, p):
            return outs[a].at[4 * p[0] + 2 * p[1] + p[2]]

        def copy(a, k, block, to, src=None):
            return pltpu.make_async_remote_copy(
                src_ref=rows(a, block) if src is None else src, dst_ref=rows(a, block),
                send_sem=send_sems.at[7 * a + k], recv_sem=recv_sems.at[7 * a + k],
                device_id=to, device_id_type=MESH)

        mine = [pltpu.make_async_copy(ins[a], rows(a, me), local_sems.at[a]) for a in range(n)]
        for cp in mine:
            cp.start()
        first = []
        for a in range(n):
            first.append(copy(a, 0, me, sibling, src=ins[a]))
            first += [copy(a, 1 + j, me, (*chip, c), src=ins[a]) for j, chip in enumerate(chips)]
        for cp in first:
            cp.start()
        passed = []
        for j, chip in enumerate(chips):
            for a in range(n):
                copy(a, 1 + j, (*chip, c), me).wait_recv()
                fwd = copy(a, 4 + j, (*chip, c), sibling)
                fwd.start()
                passed.append(fwd)
        for a in range(n):
            copy(a, 0, sibling, me).wait_recv()
            for j, chip in enumerate(chips):
                copy(a, 4 + j, (*chip, 1 - c), me).wait_recv()
        for cp in first + passed:
            cp.wait_send()
        for cp in mine:
            cp.wait()

    spec = pl.BlockSpec(memory_space=space)
    return pl.pallas_call(
        body, name=name,
        out_shape=[jax.ShapeDtypeStruct((NDEV,) + a.shape, a.dtype) for a in arrs],
        in_specs=[spec] * n, out_specs=[spec] * n,
        scratch_shapes=[pltpu.SemaphoreType.DMA((7 * n,)), pltpu.SemaphoreType.DMA((7 * n,)),
                        pltpu.SemaphoreType.DMA((n,))],
        compiler_params=pltpu.CompilerParams(vmem_limit_bytes=VMEM_LIMIT),
    )(*arrs)


class _Ride:
    def __init__(self, ins, out_shapes, n_sems, start, finish, in_place=False):
        self.ins, self.out_shapes, self.n_sems, self.start, self.finish = ins, out_shapes, n_sems, start, finish
        self.in_place = in_place


def _call(body, *, name, grid, out_shape, in_specs, out_specs, args, scratch_shapes=(), rides=(), prefetch=None):
    sem = ("arbitrary",) * len(grid)
    rides = [r for r in rides if r is not None]
    n_pre = 0 if prefetch is None else 1
    n_in, n_out, n_scr = len(in_specs), len(out_specs), len(scratch_shapes)
    r_ins = [len(r.ins) for r in rides]
    r_outs = [len(r.out_shapes) for r in rides]

    def carried(*refs):
        pre, refs = refs[:n_pre], refs[n_pre:]
        ins, pos = refs[:n_in], n_in
        rins = []
        for k in r_ins:
            rins.append(refs[pos:pos + k])
            pos += k
        outs, pos = refs[pos:pos + n_out], pos + n_out
        routs = []
        for k in r_outs:
            routs.append(refs[pos:pos + k])
            pos += k
        scratch, pos = refs[pos:pos + n_scr], pos + n_scr
        sems = [refs[pos + 3 * i:pos + 3 * i + 3] for i in range(len(rides))]
        first = functools.reduce(lambda a, b: a & b, [pl.program_id(i) == 0 for i in range(len(grid))])
        last = functools.reduce(lambda a, b: a & b, [pl.program_id(i) == grid[i] - 1 for i in range(len(grid))])

        @pl.when(first)
        def _():
            for ride, ri, ro, s in zip(rides, rins, routs, sems):
                ride.start(ri, ro, *s)

        body(*pre, *ins, *outs, *scratch)

        @pl.when(last)
        def _():
            for ride, ri, ro, s in zip(rides, rins, routs, sems):
                ride.finish(ri, ro, *s)

    hbm = pl.BlockSpec(memory_space=pl.ANY)
    aliases, in_pos, out_pos = {}, n_pre + n_in, n_out
    for ride, ki, ko in zip(rides, r_ins, r_outs):
        if ride.in_place:
            aliases.update({in_pos + k: out_pos + k for k in range(ki)})
        in_pos, out_pos = in_pos + ki, out_pos + ko
    all_scratch = list(scratch_shapes)
    for ride in rides:
        all_scratch += [pltpu.SemaphoreType.DMA((ride.n_sems,))] * 3
    all_in = list(in_specs) + [hbm] * sum(r_ins)
    all_out = list(out_specs) + [hbm] * sum(r_outs)
    shapes = list(out_shape) + [s for r in rides for s in r.out_shapes]
    operands = list(args) + [a for r in rides for a in r.ins]
    if prefetch is None:
        res = pl.pallas_call(
            carried, name=name, grid=grid, out_shape=shapes, in_specs=all_in, out_specs=all_out,
            scratch_shapes=all_scratch, input_output_aliases=aliases, compiler_params=_params(sem))(*operands)
    else:
        res = pl.pallas_call(
            carried, name=name, out_shape=shapes,
            grid_spec=pltpu.PrefetchScalarGridSpec(num_scalar_prefetch=1, grid=grid, in_specs=all_in,
                                                   out_specs=all_out, scratch_shapes=all_scratch),
            input_output_aliases=aliases, compiler_params=_params(sem))(prefetch, *operands)
    split, pos = [], n_out
    for k in r_outs:
        split.append(res[pos:pos + k])
        pos += k
    return res[:n_out], split


def _chips(x, y):
    return [(1 - x, y), (x, 1 - y), (1 - x, 1 - y)]


def _gather_ride(arrs):
    n = len(arrs)

    def copies(ins, outs, send_sems, recv_sems):
        x, y, c = _place()
        me = 4 * x + 2 * y + c
        peers = [(x, y, 1 - c)] + [(*chip, c) for chip in _chips(x, y)]
        return [pltpu.make_async_remote_copy(
            src_ref=ins[a], dst_ref=outs[a].at[me], send_sem=send_sems.at[4 * a + k], recv_sem=recv_sems.at[4 * a + k],
            device_id=p, device_id_type=MESH) for a in range(n) for k, p in enumerate(peers)], me

    def start(ins, outs, send_sems, recv_sems, local_sems):
        cps, me = copies(ins, outs, send_sems, recv_sems)
        for a in range(n):
            pltpu.make_async_copy(ins[a], outs[a].at[me], local_sems.at[a]).start()
        for cp in cps:
            cp.start()

    def finish(ins, outs, send_sems, recv_sems, local_sems):
        cps, me = copies(ins, outs, send_sems, recv_sems)
        for cp in cps:
            cp.wait_recv()
        for cp in cps:
            cp.wait_send()
        for a in range(n):
            pltpu.make_async_copy(ins[a], outs[a].at[me], local_sems.at[a]).wait()

    return _Ride(arrs, [jax.ShapeDtypeStruct((NDEV,) + a.shape, a.dtype) for a in arrs], 4 * n, start, finish)


def _gather_finish_ride(arrs):
    n = len(arrs)

    def copies(outs, send_sems, recv_sems):
        x, y, c = _place()
        cps = []
        for a in range(n):
            for k, chip in enumerate(_chips(x, y)):
                blk = 4 * chip[0] + 2 * chip[1]
                cps.append((pltpu.make_async_remote_copy(
                    src_ref=outs[a].at[blk + c], dst_ref=outs[a].at[blk + c],
                    send_sem=send_sems.at[3 * a + k], recv_sem=recv_sems.at[3 * a + k],
                    device_id=(x, y, 1 - c), device_id_type=MESH),
                    pltpu.make_async_remote_copy(
                    src_ref=outs[a].at[blk + 1 - c], dst_ref=outs[a].at[blk + 1 - c],
                    send_sem=send_sems.at[3 * a + k], recv_sem=recv_sems.at[3 * a + k],
                    device_id=(x, y, 1 - c), device_id_type=MESH)))
        return cps

    def start(ins, outs, send_sems, recv_sems, local_sems):
        for send, _ in copies(outs, send_sems, recv_sems):
            send.start()

    def finish(ins, outs, send_sems, recv_sems, local_sems):
        cps = copies(outs, send_sems, recv_sems)
        for _, recv in cps:
            recv.wait_recv()
        for send, _ in cps:
            send.wait_send()

    return _Ride(arrs, [jax.ShapeDtypeStruct(a.shape, a.dtype) for a in arrs], 3 * n, start, finish, in_place=True)


def _sibling_ride(arrs):
    n = len(arrs)

    def copies(ins, outs, send_sems, recv_sems):
        x, y, c = _place()
        return [pltpu.make_async_remote_copy(
            src_ref=ins[a].at[:, 1 - c] if arrs[a].ndim == 4 else ins[a], dst_ref=outs[a],
            send_sem=send_sems.at[a], recv_sem=recv_sems.at[a],
            device_id=(x, y, 1 - c), device_id_type=MESH) for a in range(n)]

    def start(ins, outs, send_sems, recv_sems, local_sems):
        for cp in copies(ins, outs, send_sems, recv_sems):
            cp.start()

    def finish(ins, outs, send_sems, recv_sems, local_sems):
        for cp in copies(ins, outs, send_sems, recv_sems):
            cp.wait()

    return _Ride(arrs, [jax.ShapeDtypeStruct((4,) + a.shape[-2:], a.dtype) for a in arrs], n, start, finish)


def _pair_sum(mine, theirs, core, name):
    _, pick, r, c = mine.shape
    if pick == 1:
        core = jnp.zeros_like(core)
    tr = _pick(r, 512, 16)

    def body(core_ref, a_ref, b_ref, o_ref):
        o_ref[...] = (a_ref[...].astype(F32) + b_ref[...].astype(F32)).astype(BF16)

    return pl.pallas_call(
        body, name=name,
        grid_spec=pltpu.PrefetchScalarGridSpec(
            num_scalar_prefetch=1, grid=(4, r // tr),
            in_specs=[pl.BlockSpec((None, None, tr, c), lambda i, k, core_ref: (i, core_ref[0], k, 0)),
                      pl.BlockSpec((None, tr, c), lambda i, k, core_ref: (i, k, 0))],
            out_specs=pl.BlockSpec((None, tr, c), lambda i, k, core_ref: (i, k, 0))),
        out_shape=jax.ShapeDtypeStruct((4, r, c), BF16),
        compiler_params=_params(("arbitrary", "arbitrary")),
    )(core, mine, theirs)


def _chip_exchange_ride(arrs):
    n = len(arrs)

    def copies(ins, outs, send_sems, recv_sems):
        x, y, c = _place()
        mine = 2 * x + y
        return [pltpu.make_async_remote_copy(
            src_ref=ins[a].at[2 * chip[0] + chip[1]], dst_ref=outs[a].at[mine],
            send_sem=send_sems.at[3 * a + k], recv_sem=recv_sems.at[3 * a + k],
            device_id=(*chip, c), device_id_type=MESH) for a in range(n) for k, chip in enumerate(_chips(x, y))], mine

    def start(ins, outs, send_sems, recv_sems, local_sems):
        cps, mine = copies(ins, outs, send_sems, recv_sems)
        for a in range(n):
            pltpu.make_async_copy(ins[a].at[mine], outs[a].at[mine], local_sems.at[a]).start()
        for cp in cps:
            cp.start()

    def finish(ins, outs, send_sems, recv_sems, local_sems):
        cps, mine = copies(ins, outs, send_sems, recv_sems)
        for cp in cps:
            cp.wait_recv()
        for cp in cps:
            cp.wait_send()
        for a in range(n):
            pltpu.make_async_copy(ins[a].at[mine], outs[a].at[mine], local_sems.at[a]).wait()

    return _Ride(arrs, [jax.ShapeDtypeStruct(a.shape, a.dtype) for a in arrs], 3 * n, start, finish)


def _ada_matmul(c_all, w_loc, b_loc):
    d, n = w_loc.shape
    bn = _pick(n, 512)

    def body(c_ref, w_ref, b_ref, o_ref):
        s = _silu(c_ref[...]).astype(BF16)
        o_ref[...] = _dot(s, w_ref[...].astype(BF16), NN) + b_ref[...]

    return pl.pallas_call(
        body, name="ada_matmul", grid=(n // bn,),
        out_shape=jax.ShapeDtypeStruct((NDEV, n), F32),
        in_specs=[pl.BlockSpec((NDEV, d), lambda j: (0, 0)), pl.BlockSpec((d, bn), lambda j: (0, j)),
                  pl.BlockSpec((1, bn), lambda j: (0, j))],
        out_specs=pl.BlockSpec((NDEV, bn), lambda j: (0, j)),
        compiler_params=_params(("arbitrary",)),
    )(c_all, w_loc, b_loc)


def _modulated_norm(x, norm_g, mod):
    t, d = x.shape
    tt = _pick(t, 256, 16)

    def body(x_ref, g_ref, mod_ref, h_ref):
        xv = x_ref[...]
        r = lax.rsqrt(jnp.mean(xv * xv, axis=-1, keepdims=True) + EPS)
        h = (xv * r) * g_ref[...] * (1.0 + mod_ref[1:2, :]) + mod_ref[0:1, :]
        h_ref[...] = h.astype(BF16)

    return pl.pallas_call(
        body, name="modulated_norm", grid=(t // tt,),
        out_shape=jax.ShapeDtypeStruct((t, d), BF16),
        in_specs=[pl.BlockSpec((tt, d), lambda i: (i, 0)), pl.BlockSpec((1, d), lambda i: (0, 0)),
                  pl.BlockSpec((3, d), lambda i: (0, 0))],
        out_specs=pl.BlockSpec((tt, d), lambda i: (i, 0)),
        compiler_params=_params(("arbitrary",)),
    )(x, norm_g, mod)


def _gather_proj(h, w_loc):
    t, d = h.shape
    ns = w_loc.shape[1]
    tm = _pick(t, 512, 16)
    nm = t // tm
    idx = lambda p: 4 * p[0] + 2 * p[1] + p[2]

    def peers():
        x, y, c = _place()
        flip = lambda a, b: a + b - 2 * a * b
        near, far = (flip(x, c), flip(y, 1 - c)), (flip(x, 1 - c), flip(y, c))
        return (x, y, c), (x, y, 1 - c), near, far, (1 - x, 1 - y), c

    me, sibling, near, far, diag, c = peers()
    order = [me, sibling, (*near, c), (*far, 1 - c), (*far, c), (*near, 1 - c), (*diag, c), (*diag, 1 - c)]
    order = jnp.stack([idx(p) for p in order]).astype(jnp.int32)

    def body(order_ref, a_ref, w_ref, wg_ref, o_ref, slab, send_sems, recv_sems, local_sems):
        j, m = pl.program_id(0), pl.program_id(1)
        me, sibling, near, far, diag, c = peers()

        def rows(p):
            return wg_ref.at[:, pl.ds(pl.multiple_of(idx(p) * ns, LANES), ns)]

        def copy(k, block, to, src=None):
            return pltpu.make_async_remote_copy(
                src_ref=rows(block) if src is None else src, dst_ref=rows(block),
                send_sem=send_sems.at[k], recv_sem=recv_sems.at[k], device_id=to, device_id_type=MESH)

        def load(src):
            cp = pltpu.make_async_copy(src, slab, local_sems.at[1])
            cp.start()
            cp.wait()

        keep = pltpu.make_async_copy(w_ref, rows(me), local_sems.at[0])
        own = [copy(0, me, sibling, src=w_ref), copy(1, me, (*near, c), src=w_ref), copy(2, me, (*far, c), src=w_ref)]
        relay = copy(3, (*near, c), (*far, c))
        passed = [copy(4, (*near, c), sibling), copy(5, (*far, c), sibling), copy(6, (*diag, c), sibling)]
        arrivals = [(1, 0, sibling, []), (2, 1, (*near, c), [passed[0], own[2], relay]), (3, 4, (*far, 1 - c), []),
                    (4, 2, (*far, c), [passed[1]]), (5, 5, (*near, 1 - c), []),
                    (6, 3, (*diag, c), [passed[2]]), (7, 6, (*diag, 1 - c), [])]

        @pl.when((j == 0) & (m == 0))
        def _():
            keep.start()
            for cp in own[:2]:
                cp.start()
            load(w_ref)

        for step, sem, block, onward in arrivals:
            @pl.when((j == step) & (m == 0))
            def _(sem=sem, block=block, onward=onward):
                copy(sem, block, me).wait_recv()
                for cp in onward:
                    cp.start()
                load(rows(block))

        o_ref[...] = _dot(a_ref[...], slab[...], NN)

        @pl.when((j == NDEV - 1) & (m == nm - 1))
        def _():
            for cp in own + [relay] + passed:
                cp.wait_send()
            keep.wait()

    hbm = pl.BlockSpec(memory_space=pl.ANY)
    return pl.pallas_call(
        body, name="gather_proj",
        grid_spec=pltpu.PrefetchScalarGridSpec(
            num_scalar_prefetch=1, grid=(NDEV, nm),
            in_specs=[pl.BlockSpec((tm, d), lambda j, m, order_ref: (m, 0)), hbm],
            out_specs=[hbm, pl.BlockSpec((tm, ns), lambda j, m, order_ref: (m, order_ref[j]))],
            scratch_shapes=[pltpu.VMEM((d, ns), BF16), pltpu.SemaphoreType.DMA((7,)), pltpu.SemaphoreType.DMA((7,)),
                            pltpu.SemaphoreType.DMA((2,))]),
        out_shape=[jax.ShapeDtypeStruct((d, NDEV * ns), BF16), jax.ShapeDtypeStruct((t, NDEV * ns), F32)],
        compiler_params=_params(("arbitrary", "arbitrary")),
    )(order, h, w_loc)


def _attention_fwd(proj, qg, kg, nh, d_model, ride):
    t = proj.shape[0]
    tq = _pick(t, 256, 16)
    nq = t // tq
    assert 2 <= nq <= LANES
    group = 2 if nq % 2 == 0 else 1
    scale = HEAD_DIM ** -0.5

    def body(q_ref, k_ref, v_ref, g_ref, qg_ref, kg_ref, o_ref, tot_ref, y_ref, first_ref, qn, kn, vb):
        def norm(src, gain, dst):
            v = src[...]
            r = lax.rsqrt(jnp.mean(v * v, axis=-1, keepdims=True) + EPS)
            dst[...] = ((v * r) * gain[...]).astype(BF16)

        norm(q_ref, qg_ref, qn)
        norm(k_ref, kg_ref, kn)
        vb[...] = v_ref[...].astype(BF16)
        def after_matrix(n):
            return (lax.broadcasted_iota(jnp.int32, (n, n), 0) > lax.broadcasted_iota(jnp.int32, (n, n), 1)).astype(BF16)

        upper = {tq: after_matrix(tq), 2 * tq: after_matrix(2 * tq)}

        def block(qi, start, width, carry, acc, q_start=None):
            ks = pl.ds(pl.multiple_of(start, tq), width)
            z = _dot(qi, kn[ks, :], NT) * scale
            sp = _softplus(z)
            ls = -sp
            if q_start is not None:
                causal = (start + lax.broadcasted_iota(jnp.int32, (tq, width), 1)
                          < q_start + lax.broadcasted_iota(jnp.int32, (tq, width), 0))
                ls = jnp.where(causal, ls, 0.0)
            hi = ls.astype(BF16)
            lo = (ls - hi.astype(F32)).astype(BF16)
            after = _dot(hi, upper[width], NN) + _dot(lo, upper[width], NN)
            w = jnp.exp(z - sp + after + carry)
            if q_start is not None:
                w = jnp.where(causal, w, 0.0)
            acc = acc + _dot(w.astype(BF16), vb[ks, :], NN)
            carry = carry + jnp.sum(ls, axis=1, keepdims=True)
            return carry, acc

        lane = lax.broadcasted_iota(jnp.int32, (8, LANES), 1)

        def live(carry):
            return (jnp.max(carry) > DEAD_LOG_WEIGHT).astype(jnp.int32)

        def wide_step(i):
            qi = qn[pl.ds(pl.multiple_of(i * tq, tq), tq), :]
            left = jnp.maximum(i - 1, 0)
            return block(qi, left * tq, 2 * tq, jnp.zeros((tq, 1), F32), jnp.zeros((tq, HEAD_DIM), F32), i * tq)

        def finish(i, carry, acc, firsts):
            qs = pl.ds(pl.multiple_of(i * tq, tq), tq)
            qi = qn[qs, :]
            left = jnp.maximum(i - 1, 0)

            def k_step(st):
                ca, ac = block(qi, (left - 1 - st[0]) * tq, tq, st[1], st[2])
                return st[0] + 1, ca, ac, live(ca)

            done, carry, acc, _ = lax.while_loop(
                lambda st: (st[0] < left) & (st[3] > 0), k_step, (jnp.int32(0), carry, acc, live(carry)))
            o_ref[qs, :] = acc
            tot_ref[qs, :] = jnp.broadcast_to(carry, (tq, HEAD_DIM))
            y_ref[qs, :] = (acc * _silu(g_ref[qs, :])).astype(BF16)
            return jnp.where(lane == i, (left - done).astype(F32), firsts)

        def q_group(p, firsts):
            blocks = [p + b * (nq // group) for b in range(group)]
            swept = [wide_step(i) for i in blocks]
            for i, (carry, acc) in zip(blocks, swept):
                firsts = finish(i, carry, acc, firsts)
            return firsts

        first_ref[...] = lax.fori_loop(0, nq // group, q_group, jnp.zeros((8, LANES), F32))

    col_block = lambda off: pl.BlockSpec((t, HEAD_DIM), lambda h: (0, off + h))
    vec = pl.BlockSpec((1, HEAD_DIM), lambda h: (0, 0))
    return _call(
        body, name="attention_fwd", grid=(nh,),
        out_shape=[jax.ShapeDtypeStruct((t, nh * HEAD_DIM), F32), jax.ShapeDtypeStruct((t, nh * HEAD_DIM), F32),
                   jax.ShapeDtypeStruct((t, d_model), BF16), jax.ShapeDtypeStruct((nh, 8, LANES), F32)],
        in_specs=[col_block(0), col_block(nh), col_block(2 * nh), col_block(3 * nh), vec, vec],
        out_specs=[col_block(0), col_block(0), col_block(0), pl.BlockSpec((None, 8, LANES), lambda h: (h, 0, 0))],
        scratch_shapes=[pltpu.VMEM((t, HEAD_DIM), BF16)] * 3,
        args=(proj, proj, proj, proj, qg, kg), rides=[ride])


def _fill_shifts(shifted, n_rows):
    for b in range(1, SUBLANES):
        shifted[b, pl.ds(0, n_rows), :] = shifted[0, pl.ds(b, n_rows), :]


def _shifted_rows(shifted, offset, n_rows, lo, cw):
    a, b = divmod(offset, SUBLANES)
    return shifted[b, pl.ds(SUBLANES * a, n_rows), pl.ds(lo, cw)]


def _conv_taps(shifted, w_ref, kc, lo, n_rows, cw, first_tap_row):
    acc = None
    for j in range(kc):
        term = w_ref[j:j + 1, lo:lo + cw] * _shifted_rows(shifted, first_tap_row(j), n_rows, lo, cw)
        acc = term if acc is None else acc + term
    return acc


def _glu_rows(u_ref, g_ref):
    return u_ref[...] * _sigmoid(g_ref[...])


def _conv_fwd(proj, w_dw, b_dw, ln_g, ln_b, kc, cw, ride):
    t = proj.shape[0]
    tt = _pick(t, 128, HALO)
    per = tt // HALO
    chunk = _pick(cw, 256)

    def body(u_ref, g_ref, up_ref, gp_ref, w_ref, b_ref, lg_ref, lb_ref, a_ref, hc_ref, buf):
        i = pl.program_id(0)
        buf[0, pl.ds(HALO, tt), :] = _glu_rows(u_ref, g_ref)
        halo = _glu_rows(up_ref, gp_ref)
        buf[0, pl.ds(0, HALO), :] = jnp.where(i > 0, halo, 0.0)
        _fill_shifts(buf, tt + HALO - SUBLANES)
        for lo in range(0, cw, chunk):
            conv = _conv_taps(buf, w_ref, kc, lo, tt, chunk, lambda j: HALO - (kc - 1) + j)
            hc_ref[:, lo:lo + chunk] = conv + b_ref[:, lo:lo + chunk]
        hc = hc_ref[...]
        mu = jnp.mean(hc, axis=-1, keepdims=True)
        xc = hc - mu
        var = jnp.mean(xc * xc, axis=-1, keepdims=True)
        ln = xc * lax.rsqrt(var + EPS) * lg_ref[...] + lb_ref[...]
        a_ref[...] = _silu(ln).astype(BF16)

    ncol = proj.shape[1] // cw
    tile = lambda g: pl.BlockSpec((tt, cw), lambda i: (i, g))
    prev = lambda g: pl.BlockSpec((HALO, cw), lambda i: (jnp.maximum(i * per - 1, 0), g))
    full = lambda r: pl.BlockSpec((r, cw), lambda i: (0, 0))
    return _call(
        body, name="conv_fwd", grid=(t // tt,),
        out_shape=[jax.ShapeDtypeStruct((t, cw), BF16), jax.ShapeDtypeStruct((t, cw), F32)],
        in_specs=[tile(ncol - 3), tile(ncol - 2), prev(ncol - 3), prev(ncol - 2),
                  full(w_dw.shape[0]), full(1), full(1), full(1)],
        out_specs=[pl.BlockSpec((tt, cw), lambda i: (i, 0))] * 2,
        scratch_shapes=[pltpu.VMEM((SUBLANES, HALO + tt, cw), F32)],
        args=(proj, proj, proj, proj, w_dw, b_dw, ln_g, ln_b), rides=[ride])


def _pointwise_fwd(a, wpw, b_pw, proj, ycat, cw):
    t = a.shape[0]
    tm = _pick(t, 256, 16)
    ncol = proj.shape[1] // cw
    ycol = ycat.shape[1] // cw - 1

    def body(a_ref, w_ref, b_ref, g_ref, y_in, z_ref, y_ref):
        z = _dot(a_ref[...], w_ref[...], NN) + b_ref[...]
        z_ref[...] = z
        y_ref[...] = (z * _silu(g_ref[...])).astype(BF16)

    return pl.pallas_call(
        body, name="pointwise_fwd", grid=(t // tm,),
        out_shape=[jax.ShapeDtypeStruct((t, cw), F32), jax.ShapeDtypeStruct(ycat.shape, ycat.dtype)],
        in_specs=[pl.BlockSpec((tm, cw), lambda m: (m, 0)),
                  pl.BlockSpec((cw, cw), lambda m: (0, 0)),
                  pl.BlockSpec((1, cw), lambda m: (0, 0)),
                  pl.BlockSpec((tm, cw), lambda m: (m, ncol - 1)),
                  pl.BlockSpec(memory_space=pl.ANY)],
        out_specs=[pl.BlockSpec((tm, cw), lambda m: (m, 0)), pl.BlockSpec((tm, cw), lambda m: (m, ycol))],
        input_output_aliases={4: 1},
        compiler_params=_params(("arbitrary",)),
    )(a, wpw, b_pw, proj, ycat)


def _out_matmul(ycat, wout, x, target, mod):
    t, d = x.shape
    kdim = wout.shape[0]
    tm, tn = _pick(t, 512, 16), _pick(d, 512)
    inv_d = 1.0 / d

    def body(a_ref, w_ref, x_ref, tg_ref, mod_ref, dout_ref, dy_ref, sums_ref):
        @pl.when(pl.program_id(1) == 0)
        def _():
            sums_ref[...] = jnp.zeros_like(sums_ref)

        y = _dot(a_ref[...], w_ref[...], NN)
        gate = mod_ref[2:3, :]
        err = (x_ref[...] + gate * y) - tg_ref[...]
        dout = err * inv_d
        dout_ref[...] = dout
        dy_ref[...] = (dout * gate).astype(BF16)
        sums_ref[0:1, :] += jnp.sum(dout * y, axis=0, keepdims=True)
        sums_ref[1:2, :] += jnp.sum(err * err, axis=0, keepdims=True)

    mn = lambda n, m: (m, n)
    return pl.pallas_call(
        body, name="out_matmul", grid=(d // tn, t // tm),
        out_shape=[jax.ShapeDtypeStruct((t, d), F32), jax.ShapeDtypeStruct((t, d), BF16),
                   jax.ShapeDtypeStruct((8, d), F32)],
        in_specs=[pl.BlockSpec((tm, kdim), lambda n, m: (m, 0)),
                  pl.BlockSpec((kdim, tn), lambda n, m: (0, n)),
                  pl.BlockSpec((tm, tn), mn), pl.BlockSpec((tm, tn), mn),
                  pl.BlockSpec((3, tn), lambda n, m: (0, n))],
        out_specs=[pl.BlockSpec((tm, tn), mn), pl.BlockSpec((tm, tn), mn),
                   pl.BlockSpec((8, tn), lambda n, m: (0, n))],
        compiler_params=_params(("arbitrary", "arbitrary")),
    )(ycat, wout, x, target, mod)


def _mm(a, b, form, name, tm, tn, out_dtype, *, slabs=False, n_outer=False, ksplit=1, every_other=None, rides=()):
    if form == TN:
        kdim, m_dim = a.shape
    else:
        m_dim, kdim = a.shape
    n_dim = (b.shape[0] if form == NT else b.shape[1]) // (1 if every_other is None else 2)
    tk = kdim // ksplit
    gm, gn = m_dim // tm, n_dim // tn
    mn = (lambda g: (g[1], g[0])) if n_outer else (lambda g: (g[0], g[1]))
    b_col = (lambda g: mn(g)[1]) if every_other is None else (lambda g: 2 * mn(g)[1] + g[3][0])
    a_map = (lambda *g: (g[2], mn(g)[0])) if form == TN else (lambda *g: (mn(g)[0], g[2]))
    b_map = (lambda *g: (b_col(g), g[2])) if form == NT else (lambda *g: (g[2], b_col(g)))
    a_blk = (tk, tm) if form == TN else (tm, tk)
    b_blk = (tn, tk) if form == NT else (tk, tn)
    if slabs:
        out_shape = jax.ShapeDtypeStruct((gn, m_dim, tn), out_dtype)
        out_spec = pl.BlockSpec((None, tm, tn), lambda *g: (mn(g)[1], mn(g)[0], 0))
    else:
        out_shape = jax.ShapeDtypeStruct((m_dim, n_dim), out_dtype)
        out_spec = pl.BlockSpec((tm, tn), lambda *g: mn(g))

    def body(*refs):
        a_ref, b_ref, o_ref, *acc = refs if every_other is None else refs[1:]
        part = _dot(a_ref[...], b_ref[...], form)
        if ksplit == 1:
            o_ref[...] = part.astype(out_dtype)
            return
        k = pl.program_id(2)

        @pl.when(k == 0)
        def _():
            acc[0][...] = part

        @pl.when((k > 0) & (k < ksplit - 1))
        def _():
            acc[0][...] += part

        @pl.when(k == ksplit - 1)
        def _():
            o_ref[...] = (acc[0][...] + part).astype(out_dtype)

    return _call(
        body, name=name, grid=((gn, gm) if n_outer else (gm, gn)) + (ksplit,),
        out_shape=[out_shape], in_specs=[pl.BlockSpec(a_blk, a_map), pl.BlockSpec(b_blk, b_map)],
        out_specs=[out_spec], scratch_shapes=[pltpu.VMEM((tm, tn), F32)] if ksplit > 1 else [],
        args=(a, b), rides=rides, prefetch=every_other)


def _attention_bwd(proj, o, tot, firsts, dycat, qg, kg, nh):
    t, in_cols = proj.shape
    tq = _pick(t, 256, 16)
    nq = t // tq
    group = 2 if nq % 2 == 0 else 1
    scale = HEAD_DIM ** -0.5

    def body(q_ref, k_ref, v_ref, g_ref, o_ref, tot_ref, first_ref, dy_ref, qg_ref, kg_ref, dproj_ref, gains_ref,
             qn, kn, vb, dob, dk_acc, dv_acc, dq_acc, outs, sems):
        h = pl.program_id(0)

        def norm(src, gain, dst):
            v = src[...]
            r = lax.rsqrt(jnp.mean(v * v, axis=-1, keepdims=True) + EPS)
            dst[...] = ((v * r) * gain[...]).astype(BF16)

        norm(q_ref, qg_ref, qn)
        norm(k_ref, kg_ref, kn)
        vb[...] = v_ref[...].astype(BF16)
        gs = g_ref[...]
        dyv = dy_ref[...]
        dob[...] = (dyv * _silu(gs)).astype(BF16)
        outs[3] = (dyv * o_ref[...] * _dsilu(gs)).astype(BF16)
        dk_acc[...] = jnp.zeros_like(dk_acc)
        dv_acc[...] = jnp.zeros_like(dv_acc)

        def before_matrix(n, strict):
            r = lax.broadcasted_iota(jnp.int32, (n, n), 0)
            c = lax.broadcasted_iota(jnp.int32, (n, n), 1)
            return ((c < r) if strict else (c <= r)).astype(BF16)

        incl = {n: before_matrix(n, False) for n in (tq, 2 * tq)}
        excl = {n: before_matrix(n, True) for n in (tq, 2 * tq)}
        lane = lax.broadcasted_iota(jnp.int32, (1, LANES), 1)

        def block(start, width, qi, doi, tot_row, p_left, g_left, dq, q_start=None):
            ks = pl.ds(pl.multiple_of(start, tq), width)
            kj = kn[ks, :]
            z = _dot(kj, qi, NT) * scale
            sp = _softplus(z)
            ls = -sp
            if q_start is not None:
                causal = (start + lax.broadcasted_iota(jnp.int32, (width, tq), 0)
                          < q_start + lax.broadcasted_iota(jnp.int32, (width, tq), 1))
                ls = jnp.where(causal, ls, 0.0)
            hi = ls.astype(BF16)
            lo = (ls - hi.astype(F32)).astype(BF16)
            p_inc = _dot(incl[width], hi, NN) + _dot(incl[width], lo, NN) + p_left
            beta = jnp.exp(z - sp)
            w = beta * jnp.exp(tot_row - p_inc)
            if q_start is not None:
                w = jnp.where(causal, w, 0.0)
            dw = _dot(vb[ks, :], doi, NT)
            g = w * dw
            g_before = _dot(excl[width], g.astype(BF16), NN) + g_left
            dz = g * (1.0 - beta) - beta * g_before
            if q_start is not None:
                dz = jnp.where(causal, dz, 0.0)
            dzb = dz.astype(BF16)
            dv_acc[ks, :] += _dot(w.astype(BF16), doi, NN)
            dk_acc[ks, :] += _dot(dzb, qi, NN)
            dq = dq + _dot(dzb, kj, TN)
            p_left = p_left + jnp.sum(ls, axis=0, keepdims=True)
            g_left = g_left + jnp.sum(g, axis=0, keepdims=True)
            return p_left, g_left, dq

        def operands(i):
            qs = pl.ds(pl.multiple_of(i * tq, tq), tq)
            return qn[qs, :], dob[qs, :], jnp.transpose(tot_ref[qs, :])[0:1, :]

        def singles(i):
            qi, doi, tot_row = operands(i)
            zero_row = jnp.zeros((1, tq), F32)

            def k_step(j, carry):
                return block(j * tq, tq, qi, doi, tot_row, carry[0], carry[1], carry[2])

            left = jnp.maximum(i - 1, 0)
            first = jnp.sum(jnp.where(lane == i, first_ref[0:1, :], 0.0)).astype(jnp.int32)
            first = jnp.clip(first, 0, left)
            return lax.fori_loop(first, left, k_step, (zero_row, zero_row, jnp.zeros((tq, HEAD_DIM), F32)))

        def wide_step(i, carry):
            qi, doi, tot_row = operands(i)
            left = jnp.maximum(i - 1, 0)
            _, _, dq = block(left * tq, 2 * tq, qi, doi, tot_row, carry[0], carry[1], carry[2], i * tq)
            dq_acc[pl.ds(pl.multiple_of(i * tq, tq), tq), :] = dq * scale

        def q_group(p, _):
            blocks = [p + b * (nq // group) for b in range(group)]
            carries = [singles(i) for i in blocks]
            for i, carry in zip(blocks, carries):
                wide_step(i, carry)
            return 0

        lax.fori_loop(0, nq // group, q_group, 0)

        def norm_bwd(src, gain, dn, slot, gain_row):
            v = src[...]
            r = lax.rsqrt(jnp.mean(v * v, axis=-1, keepdims=True) + EPS)
            vhat = v * r
            gains_ref[gain_row:gain_row + 1, :] = jnp.sum(dn * vhat, axis=0, keepdims=True)
            dhat = dn * gain[...]
            outs[slot] = (r * (dhat - vhat * jnp.mean(dhat * vhat, axis=-1, keepdims=True))).astype(BF16)

        gains_ref[...] = jnp.zeros_like(gains_ref)
        norm_bwd(q_ref, qg_ref, dq_acc[...], 0, 0)
        norm_bwd(k_ref, kg_ref, dk_acc[...] * scale, 1, 1)
        outs[2] = dv_acc[...].astype(BF16)
        copies = [pltpu.make_async_copy(
            outs.at[s], dproj_ref.at[:, pl.ds(pl.multiple_of((s * nh + h) * HEAD_DIM, HEAD_DIM), HEAD_DIM)], sems.at[s])
            for s in range(4)]
        for cp in copies:
            cp.start()
        for cp in copies:
            cp.wait()

    col_block = lambda off: pl.BlockSpec((t, HEAD_DIM), lambda h: (0, off + h))
    vec = pl.BlockSpec((1, HEAD_DIM), lambda h: (0, 0))
    head_scr = lambda dt: pltpu.VMEM((t, HEAD_DIM), dt)
    return pl.pallas_call(
        body, name="attention_bwd", grid=(nh,),
        out_shape=[jax.ShapeDtypeStruct((t, in_cols), BF16), jax.ShapeDtypeStruct((nh, 8, HEAD_DIM), F32)],
        in_specs=[col_block(0), col_block(nh), col_block(2 * nh), col_block(3 * nh),
                  col_block(0), col_block(0), pl.BlockSpec((None, 8, LANES), lambda h: (h, 0, 0)), col_block(0), vec, vec],
        out_specs=[pl.BlockSpec(memory_space=pl.ANY), pl.BlockSpec((None, 8, HEAD_DIM), lambda h: (h, 0, 0))],
        scratch_shapes=[head_scr(BF16), head_scr(BF16), head_scr(BF16), head_scr(BF16),
                        head_scr(F32), head_scr(F32), head_scr(F32),
                        pltpu.VMEM((4, t, HEAD_DIM), BF16), pltpu.SemaphoreType.DMA((4,))],
        compiler_params=_params(("arbitrary",)),
    )(proj, proj, proj, proj, o, tot, firsts, dycat, qg, kg)


def _pointwise_bwd(dycat, z, proj, dproj, wpw, hc, ln_g, ln_b, cw):
    t = z.shape[0]
    tt = _pick(t, 256, 16)
    ncol = proj.shape[1] // cw
    ycol = dycat.shape[1] // cw - 1

    def body(dy_ref, z_ref, g_ref, w_ref, hc_ref, lg_ref, lb_ref, dp_in, dz_ref, dp_ref, dh_ref, sums_ref):
        i = pl.program_id(0)

        @pl.when(i == 0)
        def _():
            sums_ref[...] = jnp.zeros_like(sums_ref)

        g = g_ref[...]
        dy = dy_ref[...]
        dz = dy * _silu(g)
        dzb = dz.astype(BF16)
        dz_ref[...] = dzb
        dp_ref[...] = (dy * z_ref[...] * _dsilu(g)).astype(BF16)
        da = _dot(dzb, w_ref[...], NT)
        hcv = hc_ref[...]
        mu = jnp.mean(hcv, axis=-1, keepdims=True)
        xc = hcv - mu
        r = lax.rsqrt(jnp.mean(xc * xc, axis=-1, keepdims=True) + EPS)
        xhat = xc * r
        ln = xhat * lg_ref[...] + lb_ref[...]
        dln = da * _dsilu(ln)
        dxhat = dln * lg_ref[...]
        dhc = r * (dxhat - jnp.mean(dxhat, axis=-1, keepdims=True)
                   - xhat * jnp.mean(dxhat * xhat, axis=-1, keepdims=True))
        dh_ref[...] = dhc
        sums_ref[0:1, :] += jnp.sum(dz, axis=0, keepdims=True)
        sums_ref[1:2, :] += jnp.sum(dln * xhat, axis=0, keepdims=True)
        sums_ref[2:3, :] += jnp.sum(dln, axis=0, keepdims=True)
        sums_ref[3:4, :] += jnp.sum(dhc, axis=0, keepdims=True)

    tile = lambda col: pl.BlockSpec((tt, cw), lambda i: (i, col))
    vec = pl.BlockSpec((1, cw), lambda i: (0, 0))
    return pl.pallas_call(
        body, name="pointwise_bwd", grid=(t // tt,),
        out_shape=[jax.ShapeDtypeStruct((t, cw), BF16), jax.ShapeDtypeStruct(dproj.shape, dproj.dtype),
                   jax.ShapeDtypeStruct((t, cw), F32), jax.ShapeDtypeStruct((8, cw), F32)],
        in_specs=[tile(ycol), tile(0), tile(ncol - 1), pl.BlockSpec((cw, cw), lambda i: (0, 0)), tile(0), vec, vec,
                  pl.BlockSpec(memory_space=pl.ANY)],
        out_specs=[tile(0), tile(ncol - 1), tile(0), pl.BlockSpec((8, cw), lambda i: (0, 0))],
        input_output_aliases={7: 1},
        compiler_params=_params(("arbitrary",)),
    )(dycat, z, proj, wpw, hc, ln_g, ln_b, dproj)


def _conv_bwd(dhc, proj, w_dw, dproj, kc, cw):
    t = proj.shape[0]
    tt = _pick(t, 128, HALO)
    per = tt // HALO
    nt = t // tt
    chunk = _pick(cw, 256)
    ncol = proj.shape[1] // cw
    wr = w_dw.shape[0]

    def body(d_ref, dn_ref, u_ref, g_ref, up_ref, gp_ref, w_ref, dp_in, dp_ref, dw_ref, dbuf, hbuf, dw_acc):
        i = pl.program_id(0)

        @pl.when(i == 0)
        def _():
            dw_acc[...] = jnp.zeros_like(dw_acc)

        dbuf[0, pl.ds(0, tt), :] = d_ref[...]
        dbuf[0, pl.ds(tt, HALO), :] = jnp.where(i < nt - 1, dn_ref[...], 0.0)
        hbuf[0, pl.ds(HALO, tt), :] = _glu_rows(u_ref, g_ref)
        hbuf[0, pl.ds(0, HALO), :] = jnp.where(i > 0, _glu_rows(up_ref, gp_ref), 0.0)
        _fill_shifts(dbuf, tt + HALO - SUBLANES)
        _fill_shifts(hbuf, tt + HALO - SUBLANES)
        for lo in range(0, cw, chunk):
            dhg = _conv_taps(dbuf, w_ref, kc, lo, tt, chunk, lambda j: (kc - 1) - j)
            u = u_ref[:, lo:lo + chunk]
            sg = _sigmoid(g_ref[:, lo:lo + chunk])
            dp_ref[:, lo:lo + chunk] = (dhg * sg).astype(BF16)
            dp_ref[:, cw + lo:cw + lo + chunk] = (dhg * u * sg * (1.0 - sg)).astype(BF16)
            dtile = d_ref[:, lo:lo + chunk]
            for j in range(kc):
                prod = dtile * _shifted_rows(hbuf, HALO - (kc - 1) + j, tt, lo, chunk)
                dw_acc[j, :, lo:lo + chunk] += jnp.sum(prod.reshape(tt // SUBLANES, SUBLANES, chunk), axis=0)

        @pl.when(i == nt - 1)
        def _():
            dw_ref[...] = jnp.sum(dw_acc[...], axis=1)

    tile = lambda g: pl.BlockSpec((tt, cw), lambda i: (i, g))
    prev = lambda g: pl.BlockSpec((HALO, cw), lambda i: (jnp.maximum(i * per - 1, 0), g))
    return pl.pallas_call(
        body, name="conv_bwd", grid=(nt,),
        out_shape=[jax.ShapeDtypeStruct(dproj.shape, dproj.dtype), jax.ShapeDtypeStruct((wr, cw), F32)],
        in_specs=[pl.BlockSpec((tt, cw), lambda i: (i, 0)),
                  pl.BlockSpec((HALO, cw), lambda i: (jnp.minimum((i + 1) * per, nt * per - 1), 0)),
                  tile(ncol - 3), tile(ncol - 2), prev(ncol - 3), prev(ncol - 2),
                  pl.BlockSpec((wr, cw), lambda i: (0, 0)), pl.BlockSpec(memory_space=pl.ANY)],
        out_specs=[pl.BlockSpec((tt, 2 * cw), lambda i: (i, (ncol - 3) // 2)),
                   pl.BlockSpec((wr, cw), lambda i: (0, 0))],
        scratch_shapes=[pltpu.VMEM((SUBLANES, tt + HALO, cw), F32), pltpu.VMEM((SUBLANES, HALO + tt, cw), F32),
                        pltpu.VMEM((wr, SUBLANES, cw), F32)],
        input_output_aliases={7: 0},
        compiler_params=_params(("arbitrary",)),
    )(dhc, dhc, proj, proj, proj, proj, w_dw, dproj)


def _input_grad(dh, x, dout, norm_g, mod):
    t, d = x.shape
    tt = _pick(t, 128, 16)

    def body(dh_ref, x_ref, do_ref, g_ref, mod_ref, gx_ref, sums_ref):
        i = pl.program_id(0)

        @pl.when(i == 0)
        def _():
            sums_ref[...] = jnp.zeros_like(sums_ref)

        xv = x_ref[...]
        dhv = dh_ref[...]
        r = lax.rsqrt(jnp.mean(xv * xv, axis=-1, keepdims=True) + EPS)
        xn = xv * r
        g = g_ref[...]
        one_scale = 1.0 + mod_ref[1:2, :]
        dxn = dhv * g * one_scale
        gx_ref[...] = do_ref[...] + r * (dxn - xn * jnp.mean(dxn * xn, axis=-1, keepdims=True))
        sums_ref[0:1, :] += jnp.sum(dhv, axis=0, keepdims=True)
        sums_ref[1:2, :] += jnp.sum(dhv * (xn * g), axis=0, keepdims=True)
        sums_ref[2:3, :] += jnp.sum(dhv * one_scale * xn, axis=0, keepdims=True)

    tile = pl.BlockSpec((tt, d), lambda i: (i, 0))
    return pl.pallas_call(
        body, name="input_grad", grid=(t // tt,),
        out_shape=[jax.ShapeDtypeStruct((t, d), F32), jax.ShapeDtypeStruct((8, d), F32)],
        in_specs=[tile, tile, tile, pl.BlockSpec((1, d), lambda i: (0, 0)), pl.BlockSpec((3, d), lambda i: (0, 0))],
        out_specs=[tile, pl.BlockSpec((8, d), lambda i: (0, 0))],
        compiler_params=_params(("arbitrary",)),
    )(dh, x, dout, norm_g, mod)


def _sum_adam(parts, w, m, v, name):
    r, c = w.shape
    n_parts = parts.shape[0]
    tr = _pick(r, 128, 16) if r % 16 == 0 else r
    tc = _pick(c, 2048)

    def body(p_ref, w_ref, m_ref, v_ref, g_ref, d_ref, nm_ref, nv_ref):
        g = p_ref[0].astype(F32)
        for i in range(1, n_parts):
            g = g + p_ref[i].astype(F32)
        d, nm, nv = _adam(w_ref[...], g, m_ref[...], v_ref[...])
        g_ref[...] = g
        d_ref[...] = d
        nm_ref[...] = nm
        nv_ref[...] = nv

    tile = pl.BlockSpec((tr, tc), lambda i, j: (i, j))
    out = jax.ShapeDtypeStruct((r, c), F32)
    return pl.pallas_call(
        body, name=name, grid=(r // tr, c // tc),
        out_shape=[out] * 4,
        in_specs=[pl.BlockSpec((n_parts, tr, tc), lambda i, j: (0, i, j)), tile, tile, tile],
        out_specs=[tile] * 4,
        compiler_params=_params(("arbitrary", "arbitrary")),
    )(parts, w, m, v)


def _ada_grad_adam(s_t, dm, w, m, v):
    d, n = w.shape
    tr = _pick(d, 256, 16)

    def body(s_ref, dm_ref, w_ref, m_ref, v_ref, g_ref, d_ref, nm_ref, nv_ref):
        g = lax.dot_general(s_ref[...], dm_ref[...], (NN, ((), ())), preferred_element_type=F32,
                            precision=lax.Precision.HIGHEST)
        dl, nm, nv = _adam(w_ref[...], g, m_ref[...], v_ref[...])
        g_ref[...] = g
        d_ref[...] = dl
        nm_ref[...] = nm
        nv_ref[...] = nv

    tile = pl.BlockSpec((tr, n), lambda i: (i, 0))
    out = jax.ShapeDtypeStruct((d, n), F32)
    return pl.pallas_call(
        body, name="ada_grad_adam", grid=(d // tr,),
        out_shape=[out] * 4,
        in_specs=[pl.BlockSpec((tr, NDEV), lambda i: (i, 0)), pl.BlockSpec((NDEV, n), lambda i: (0, 0)),
                  tile, tile, tile],
        out_specs=[tile] * 4,
        compiler_params=_params(("arbitrary",)),
    )(s_t, dm, w, m, v)


def _silu_t(c_all):
    n, d = c_all.shape

    def body(c_ref, o_ref):
        o_ref[...] = jnp.transpose(_silu(c_ref[...]))

    return pl.pallas_call(
        body, name="silu_t", out_shape=jax.ShapeDtypeStruct((d, n), F32),
        in_specs=[pl.BlockSpec(memory_space=pltpu.VMEM)], out_specs=pl.BlockSpec(memory_space=pltpu.VMEM),
        compiler_params=pltpu.CompilerParams(vmem_limit_bytes=VMEM_LIMIT),
    )(c_all)


def _rows128(v):
    return v.reshape(-1, LANES)


def _pad_rows(a, rows):
    return jnp.pad(a, ((0, rows - a.shape[0]), (0, 0)))


def kernel(x, c, norm_g, w_ada, b_ada, w_in, q_norm_g, k_norm_g, w_dw, b_dw, ln_g, ln_b, w_pw, b_pw, w_out, loss_target, m_norm_g, m_w_ada, m_b_ada, m_w_in, m_q_norm_g, m_k_norm_g, m_w_dw, m_b_dw, m_ln_g, m_ln_b, m_w_pw, m_b_pw, m_w_out, v_norm_g, v_w_ada, v_b_ada, v_w_in, v_q_norm_g, v_k_norm_g, v_w_dw, v_b_dw, v_ln_g, v_ln_b, v_w_pw, v_b_pw, v_w_out):
    _, t, d = x.shape
    n_ada = w_ada.shape[2]
    ns = w_in.shape[2]
    kc, cwl = w_dw.shape[1], w_dw.shape[2]
    cw = cwl * NDEV
    sb = d - cw
    nh = sb // HEAD_DIM
    assert sb == cw and kc - 1 <= HALO and NDEV * ns == 4 * sb + 3 * cw
    my = 4 * lax.axis_index("x") + 2 * lax.axis_index("y") + lax.axis_index("c")

    x2, tg2 = x[0], loss_target[0]

    wdw_rows = -(-kc // 8) * 8
    wdw_pad = _pad_rows(w_dw[0], wdw_rows)
    pay1 = jnp.concatenate([_rows128(c[0]), _rows128(wdw_pad.reshape(-1))], axis=0)
    (g1,) = _all_gather([pay1], "gather_cond", pltpu.VMEM)
    c_rows = d // LANES
    c_all = g1[:, :c_rows].reshape(NDEV, d)
    wdw_all = g1[:, c_rows:].reshape(NDEV, wdw_rows, cwl).transpose(1, 0, 2).reshape(wdw_rows, cw)

    b_ada_loc = lax.dynamic_slice(b_ada, (0, my * n_ada), (1, n_ada))
    mod_cols = _ada_matmul(c_all, w_ada[0], b_ada_loc)
    (g2,) = _all_gather([mod_cols], "gather_mod", pltpu.VMEM)
    mod_mine = lax.dynamic_index_in_dim(g2, my, axis=1, keepdims=False)
    mod = mod_mine.reshape(3, d)

    core = lax.axis_index("c").astype(jnp.int32).reshape(1)
    h = _modulated_norm(x2, norm_g, mod)
    wfull_in, proj = _gather_proj(h, w_in[0].astype(BF16))
    (o, tot, ycat, firsts), (partly,) = _attention_fwd(
        proj, q_norm_g, k_norm_g, nh, d, _gather_ride([w_out[0].astype(BF16), w_pw[0].astype(BF16)]))
    (a, hc), ((wg_out, wg_pw),) = _conv_fwd(proj, wdw_all, b_dw, ln_g, ln_b, kc, cw, _gather_finish_ride(partly))
    wfull_out, wfull_pw = wg_out.reshape(d, d), wg_pw.reshape(cw, cw)
    z, ycat = _pointwise_fwd(a, wfull_pw, b_pw, proj, ycat, cw)
    dout, dy, out_sums = _out_matmul(ycat, wfull_out, x2, tg2, mod)

    tile = _pick(t, 512, 16)
    (dycat,), _ = _mm(dy, wfull_out, NT, "dycat_matmul", tile, _pick(d, 512), F32)
    (p_wout,), _ = _mm(ycat, dy, TN, "w_out_grad", _pick(d, 512), _pick(d, 1024), BF16)
    p_wout = p_wout.reshape(NDEV, d // NDEV, d)
    dproj, gains = _attention_bwd(proj, o, tot, firsts, dycat, q_norm_g, k_norm_g, nh)
    dz, dproj, dhc, pw_sums = _pointwise_bwd(dycat, z, proj, dproj, wfull_pw, hc, ln_g, ln_b, cw)
    (p_wpw,), _ = _mm(a, dz, TN, "w_pw_grad", _pick(cw, 512), _pick(cw, 1024), BF16)
    p_wpw = p_wpw.reshape(NDEV, cw // NDEV, cw)
    dproj, dwdw = _conv_bwd(dhc, proj, wdw_all, dproj, kc, cw)
    p_wdw = dwdw.reshape(wdw_rows, NDEV, cwl).transpose(1, 0, 2).astype(BF16)

    def pair_sums(mine, theirs, name):
        return [_pair_sum(m, s, core, f"{name}_pair_sum_{i}") for i, (m, s) in enumerate(zip(mine, theirs))]

    lesser = [p.reshape(4, 2, *p.shape[1:]) for p in (p_wout, p_wpw, p_wdw)]
    (p_theirs,), (small_theirs,) = _mm(h, dproj, TN, "w_in_grad_sibling", _pick(d, 256), ns, BF16, slabs=True,
                                       n_outer=True, every_other=1 - core, rides=[_sibling_ride(lesser)])
    q_small = pair_sums(lesser, small_theirs, "small_grads")
    (p_mine,), (win_theirs, (r_wout, r_wpw, r_wdw)) = _mm(
        h, dproj, TN, "w_in_grad_own", _pick(d, 256), ns, BF16, slabs=True, n_outer=True, every_other=core,
        rides=[_sibling_ride([p_theirs]), _chip_exchange_ride(q_small)])
    q_win = pair_sums([p_mine[:, None]], win_theirs, "w_in_grad")
    (dh,), ((r_win,),) = _mm(dproj, wfull_in, NT, "dh_matmul", tile, _pick(d, 512), F32, ksplit=2,
                             rides=[_chip_exchange_ride(q_win)])
    grad_x, in_sums = _input_grad(dh, x2, dout, norm_g, mod)

    dmod = jnp.concatenate([in_sums[0], in_sums[1], out_sums[0]])
    loss_part = 0.5 / d * jnp.sum(out_sums[1].reshape(-1, LANES), axis=0)
    small = [in_sums[2], dmod, jnp.sum(gains[:, 0], axis=0), jnp.sum(gains[:, 1], axis=0),
             pw_sums[3], pw_sums[1], pw_sums[2], pw_sums[0], loss_part]
    sizes = [s.shape[0] for s in small]
    packed = _rows128(jnp.concatenate(small))
    n_rows = -(-packed.shape[0] // 8) * 8
    (g3,) = _all_gather([_pad_rows(packed, n_rows)], "gather_small", pltpu.VMEM)

    def pack_state(names_vals):
        flat = jnp.concatenate([v.reshape(-1) for v in names_vals] + [jnp.zeros((LANES,), F32)])
        return _pad_rows(_rows128(flat), n_rows)

    small_w = pack_state([norm_g, b_ada, q_norm_g, k_norm_g, b_dw, ln_g, ln_b, b_pw])
    small_m = pack_state([m_norm_g, m_b_ada, m_q_norm_g, m_k_norm_g, m_b_dw, m_ln_g, m_ln_b, m_b_pw])
    small_v = pack_state([v_norm_g, v_b_ada, v_q_norm_g, v_k_norm_g, v_b_dw, v_ln_g, v_ln_b, v_b_pw])
    sg, sd, sm, sv = _sum_adam(g3, small_w, small_m, small_v, "small_adam")

    def unpack(p):
        flat = p.reshape(-1)
        outs, off = [], 0
        for n in sizes[:-1]:
            outs.append(flat[off:off + n].reshape(1, n))
            off += n
        return outs, flat[off:off + LANES]

    g_small, loss_lanes = unpack(sg)
    d_small, _ = unpack(sd)
    m_small, _ = unpack(sm)
    v_small, _ = unpack(sv)
    loss = jnp.sum(loss_lanes)

    off = sizes[0]
    dmod_all = g3.reshape(NDEV, -1)[:, off:off + 3 * d]
    dmod_loc = lax.dynamic_slice(dmod_all, (0, my * n_ada), (NDEV, n_ada))
    ada = _ada_grad_adam(_silu_t(c_all), dmod_loc, w_ada[0], m_w_ada[0], v_w_ada[0])
    win = _sum_adam(r_win, w_in[0], m_w_in[0], v_w_in[0], "w_in_adam")
    wout = _sum_adam(r_wout, w_out[0], m_w_out[0], v_w_out[0], "w_out_adam")
    wpw = _sum_adam(r_wpw, w_pw[0], m_w_pw[0], v_w_pw[0], "w_pw_adam")
    wdw_state = [_pad_rows(s[0], wdw_rows) for s in (w_dw, m_w_dw, v_w_dw)]
    wdw = [r[:kc] for r in _sum_adam(r_wdw, *wdw_state, "w_dw_adam")]

    def group(k, small_list):
        s = small_list
        return [s[0], ada[k][None], s[1], win[k][None], s[2], s[3], wdw[k][None], s[4], s[5], s[6],
                wpw[k][None], s[7], wout[k][None]]

    return (loss, grad_x[None], *group(0, g_small), *group(1, d_small), *group(2, m_small), *group(3, v_small))
```

```python
import functools

import jax
import jax.numpy as jnp
from jax import lax
from jax.experimental import pallas as pl
from jax.experimental.pallas import tpu as pltpu

F32 = jnp.float32
BF16 = jnp.bfloat16
NDEV = 8
HEAD_DIM = 128
LANES = 128
SUBLANES = 8
HALO = 32
EPS = 1e-6
DEAD_LOG_WEIGHT = -104.0
VMEM_LIMIT = 56 * 1024 * 1024
MESH = pl.DeviceIdType.MESH

ADAM_LR = 0.001
ADAM_B1 = 0.9
ADAM_B2 = 0.999
ADAM_EPS = 1e-08
ADAM_WD = 0.01
ADAM_STEP = 10


def _params(sem=None):
    return pltpu.CompilerParams(dimension_semantics=sem, vmem_limit_bytes=VMEM_LIMIT)


def _pick(n, pref, unit=LANES):
    best = None
    for d in range(unit, min(n, pref) + 1, unit):
        if n % d == 0:
            best = d
    return best if best is not None else n


def _sigmoid(z):
    return 1.0 / (1.0 + jnp.exp(-z))


def _silu(z):
    return z * _sigmoid(z)


def _dsilu(z):
    s = _sigmoid(z)
    return s * (1.0 + z * (1.0 - s))


def _softplus(z):
    return jnp.maximum(z, 0.0) + jnp.log(1.0 + jnp.exp(-jnp.abs(z)))


def _dot(a, b, dims):
    return lax.dot_general(a, b, (dims, ((), ())), preferred_element_type=F32)


NN = ((1,), (0,))
NT = ((1,), (1,))
TN = ((0,), (0,))


def _adam(w, g, m, v):
    m = ADAM_B1 * m + (1.0 - ADAM_B1) * g
    v = ADAM_B2 * v + (1.0 - ADAM_B2) * (g * g)
    m_hat = m / (1.0 - ADAM_B1 ** ADAM_STEP)
    v_hat = v / (1.0 - ADAM_B2 ** ADAM_STEP)
    delta = -ADAM_LR * (m_hat / (jnp.sqrt(v_hat) + ADAM_EPS) + ADAM_WD * w)
    return delta, m, v


def _place():
    x, y, c = lax.axis_index("x"), lax.axis_index("y"), lax.axis_index("c")
    return x, y, c


def _flip(v, bit):
    return 1 - v if bit else v


def _all_gather(arrs, name, space):
    n = len(arrs)

    def body(*refs):
        ins, outs = refs[:n], refs[n:2 * n]
        send_sems, recv_sems, local_sems = refs[2 * n:]
        x, y, c = _place()
        me, sibling = (x, y, c), (x, y, 1 - c)
        chips = [(1 - x, y), (x, 1 - y), (1 - x, 1 - y)]

        def rows(a, p):
            return outs[a].at[4 * p[0] + 2 * p[1] + p[2]]

        def copy(a, k, block, to, src=None):
            return pltpu.make_async_remote_copy(
                src_ref=rows(a, block) if src is None else src, dst_ref=rows(a, block),
                send_sem=send_sems.at[7 * a + k], recv_sem=recv_sems.at[7 * a + k],
                device_id=to, device_id_type=MESH)

        mine = [pltpu.make_async_copy(ins[a], rows(a, me), local_sems.at[a]) for a in range(n)]
        for cp in mine:
            cp.start()
        first = []
        for a in range(n):
            first.append(copy(a, 0, me, sibling, src=ins[a]))
            first += [copy(a, 1 + j, me, (*chip, c), src=ins[a]) for j, chip in enumerate(chips)]
        for cp in first:
            cp.start()
        passed = []
        for j, chip in enumerate(chips):
            for a in range(n):
                copy(a, 1 + j, (*chip, c), me).wait_recv()
                fwd = copy(a, 4 + j, (*chip, c), sibling)
                fwd.start()
                passed.append(fwd)
        for a in range(n):
            copy(a, 0, sibling, me).wait_recv()
            for j, chip in enumerate(chips):
                copy(a, 4 + j, (*chip, 1 - c), me).wait_recv()
        for cp in first + passed:
            cp.wait_send()
        for cp in mine:
            cp.wait()

    spec = pl.BlockSpec(memory_space=space)
    return pl.pallas_call(
        body, name=name,
        out_shape=[jax.ShapeDtypeStruct((NDEV,) + a.shape, a.dtype) for a in arrs],
        in_specs=[spec] * n, out_specs=[spec] * n,
        scratch_shapes=[pltpu.SemaphoreType.DMA((7 * n,)), pltpu.SemaphoreType.DMA((7 * n,)),
                        pltpu.SemaphoreType.DMA((n,))],
        compiler_params=pltpu.CompilerParams(vmem_limit_bytes=VMEM_LIMIT),
    )(*arrs)


class _Ride:
    def __init__(self, ins, out_shapes, n_sems, start, finish, in_place=False):
        self.ins, self.out_shapes, self.n_sems, self.start, self.finish = ins, out_shapes, n_sems, start, finish
        self.in_place = in_place


def _call(body, *, name, grid, out_shape, in_specs, out_specs, args, scratch_shapes=(), rides=(), prefetch=None):
    sem = ("arbitrary",) * len(grid)
    rides = [r for r in rides if r is not None]
    n_pre = 0 if prefetch is None else 1
    n_in, n_out, n_scr = len(in_specs), len(out_specs), len(scratch_shapes)
    r_ins = [len(r.ins) for r in rides]
    r_outs = [len(r.out_shapes) for r in rides]

    def carried(*refs):
        pre, refs = refs[:n_pre], refs[n_pre:]
        ins, pos = refs[:n_in], n_in
        rins = []
        for k in r_ins:
            rins.append(refs[pos:pos + k])
            pos += k
        outs, pos = refs[pos:pos + n_out], pos + n_out
        routs = []
        for k in r_outs:
            routs.append(refs[pos:pos + k])
            pos += k
        scratch, pos = refs[pos:pos + n_scr], pos + n_scr
        sems = [refs[pos + 3 * i:pos + 3 * i + 3] for i in range(len(rides))]
        first = functools.reduce(lambda a, b: a & b, [pl.program_id(i) == 0 for i in range(len(grid))])
        last = functools.reduce(lambda a, b: a & b, [pl.program_id(i) == grid[i] - 1 for i in range(len(grid))])

        @pl.when(first)
        def _():
            for ride, ri, ro, s in zip(rides, rins, routs, sems):
                ride.start(ri, ro, *s)

        body(*pre, *ins, *outs, *scratch)

        @pl.when(last)
        def _():
            for ride, ri, ro, s in zip(rides, rins, routs, sems):
                ride.finish(ri, ro, *s)

    hbm = pl.BlockSpec(memory_space=pl.ANY)
    aliases, in_pos, out_pos = {}, n_pre + n_in, n_out
    for ride, ki, ko in zip(rides, r_ins, r_outs):
        if ride.in_place:
            aliases.update({in_pos + k: out_pos + k for k in range(ki)})
        in_pos, out_pos = in_pos + ki, out_pos + ko
    all_scratch = list(scratch_shapes)
    for ride in rides:
        all_scratch += [pltpu.SemaphoreType.DMA((ride.n_sems,))] * 3
    all_in = list(in_specs) + [hbm] * sum(r_ins)
    all_out = list(out_specs) + [hbm] * sum(r_outs)
    shapes = list(out_shape) + [s for r in rides for s in r.out_shapes]
    operands = list(args) + [a for r in rides for a in r.ins]
    if prefetch is None:
        res = pl.pallas_call(
            carried, name=name, grid=grid, out_shape=shapes, in_specs=all_in, out_specs=all_out,
            scratch_shapes=all_scratch, input_output_aliases=aliases, compiler_params=_params(sem))(*operands)
    else:
        res = pl.pallas_call(
            carried, name=name, out_shape=shapes,
            grid_spec=pltpu.PrefetchScalarGridSpec(num_scalar_prefetch=1, grid=grid, in_specs=all_in,
                                                   out_specs=all_out, scratch_shapes=all_scratch),
            input_output_aliases=aliases, compiler_params=_params(sem))(prefetch, *operands)
    split, pos = [], n_out
    for k in r_outs:
        split.append(res[pos:pos + k])
        pos += k
    return res[:n_out], split


def _chips(x, y):
    return [(1 - x, y), (x, 1 - y), (1 - x, 1 - y)]


def _gather_ride(arrs):
    n = len(arrs)

    def copies(ins, outs, send_sems, recv_sems):
        x, y, c = _place()
        me = 4 * x + 2 * y + c
        peers = [(x, y, 1 - c)] + [(*chip, c) for chip in _chips(x, y)]
        return [pltpu.make_async_remote_copy(
            src_ref=ins[a], dst_ref=outs[a].at[me], send_sem=send_sems.at[4 * a + k], recv_sem=recv_sems.at[4 * a + k],
            device_id=p, device_id_type=MESH) for a in range(n) for k, p in enumerate(peers)], me

    def start(ins, outs, send_sems, recv_sems, local_sems):
        cps, me = copies(ins, outs, send_sems, recv_sems)
        for a in range(n):
            pltpu.make_async_copy(ins[a], outs[a].at[me], local_sems.at[a]).start()
        for cp in cps:
            cp.start()

    def finish(ins, outs, send_sems, recv_sems, local_sems):
        cps, me = copies(ins, outs, send_sems, recv_sems)
        for cp in cps:
            cp.wait_recv()
        for cp in cps:
            cp.wait_send()
        for a in range(n):
            pltpu.make_async_copy(ins[a], outs[a].at[me], local_sems.at[a]).wait()

    return _Ride(arrs, [jax.ShapeDtypeStruct((NDEV,) + a.shape, a.dtype) for a in arrs], 4 * n, start, finish)


def _gather_finish_ride(arrs):
    n = len(arrs)

    def copies(outs, send_sems, recv_sems):
        x, y, c = _place()
        cps = []
        for a in range(n):
            for k, chip in enumerate(_chips(x, y)):
                blk = 4 * chip[0] + 2 * chip[1]
                cps.append((pltpu.make_async_remote_copy(
                    src_ref=outs[a].at[blk + c], dst_ref=outs[a].at[blk + c],
                    send_sem=send_sems.at[3 * a + k], recv_sem=recv_sems.at[3 * a + k],
                    device_id=(x, y, 1 - c), device_id_type=MESH),
                    pltpu.make_async_remote_copy(
                    src_ref=outs[a].at[blk + 1 - c], dst_ref=outs[a].at[blk + 1 - c],
                    send_sem=send_sems.at[3 * a + k], recv_sem=recv_sems.at[3 * a + k],
                    device_id=(x, y, 1 - c), device_id_type=MESH)))
        return cps

    def start(ins, outs, send_sems, recv_sems, local_sems):
        for send, _ in copies(outs, send_sems, recv_sems):
            send.start()

    def finish(ins, outs, send_sems, recv_sems, local_sems):
        cps = copies(outs, send_sems, recv_sems)
        for _, recv in cps:
            recv.wait_recv()
        for send, _ in cps:
            send.wait_send()

    return _Ride(arrs, [jax.ShapeDtypeStruct(a.shape, a.dtype) for a in arrs], 3 * n, start, finish, in_place=True)


def _sibling_ride(arrs):
    n = len(arrs)

    def copies(ins, outs, send_sems, recv_sems):
        x, y, c = _place()
        return [pltpu.make_async_remote_copy(
            src_ref=ins[a].at[:, 1 - c] if arrs[a].ndim == 4 else ins[a], dst_ref=outs[a],
            send_sem=send_sems.at[a], recv_sem=recv_sems.at[a],
            device_id=(x, y, 1 - c), device_id_type=MESH) for a in range(n)]

    def start(ins, outs, send_sems, recv_sems, local_sems):
        for cp in copies(ins, outs, send_sems, recv_sems):
            cp.start()

    def finish(ins, outs, send_sems, recv_sems, local_sems):
        for cp in copies(ins, outs, send_sems, recv_sems):
            cp.wait()

    return _Ride(arrs, [jax.ShapeDtypeStruct((4,) + a.shape[-2:], a.dtype) for a in arrs], n, start, finish)


def _pair_sum(mine, theirs, core, name):
    _, pick, r, c = mine.shape
    if pick == 1:
        core = jnp.zeros_like(core)
    tr = _pick(r, 512, 16)

    def body(core_ref, a_ref, b_ref, o_ref):
        o_ref[...] = (a_ref[...].astype(F32) + b_ref[...].astype(F32)).astype(BF16)

    return pl.pallas_call(
        body, name=name,
        grid_spec=pltpu.PrefetchScalarGridSpec(
            num_scalar_prefetch=1, grid=(4, r // tr),
            in_specs=[pl.BlockSpec((None, None, tr, c), lambda i, k, core_ref: (i, core_ref[0], k, 0)),
                      pl.BlockSpec((None, tr, c), lambda i, k, core_ref: (i, k, 0))],
            out_specs=pl.BlockSpec((None, tr, c), lambda i, k, core_ref: (i, k, 0))),
        out_shape=jax.ShapeDtypeStruct((4, r, c), BF16),
        compiler_params=_params(("arbitrary", "arbitrary")),
    )(core, mine, theirs)


def _chip_exchange_ride(arrs):
    n = len(arrs)

    def copies(ins, outs, send_sems, recv_sems):
        x, y, c = _place()
        mine = 2 * x + y
        return [pltpu.make_async_remote_copy(
            src_ref=ins[a].at[2 * chip[0] + chip[1]], dst_ref=outs[a].at[mine],
            send_sem=send_sems.at[3 * a + k], recv_sem=recv_sems.at[3 * a + k],
            device_id=(*chip, c), device_id_type=MESH) for a in range(n) for k, chip in enumerate(_chips(x, y))], mine

    def start(ins, outs, send_sems, recv_sems, local_sems):
        cps, mine = copies(ins, outs, send_sems, recv_sems)
        for a in range(n):
            pltpu.make_async_copy(ins[a].at[mine], outs[a].at[mine], local_sems.at[a]).start()
        for cp in cps:
            cp.start()

    def finish(ins, outs, send_sems, recv_sems, local_sems):
        cps, mine = copies(ins, outs, send_sems, recv_sems)
        for cp in cps:
            cp.wait_recv()
        for cp in cps:
            cp.wait_send()
        for a in range(n):
            pltpu.make_async_copy(ins[a].at[mine], outs[a].at[mine], local_sems.at[a]).wait()

    return _Ride(arrs, [jax.ShapeDtypeStruct(a.shape, a.dtype) for a in arrs], 3 * n, start, finish)


def _ada_matmul(c_all, w_loc, b_loc):
    d, n = w_loc.shape
    bn = _pick(n, 512)

    def body(c_ref, w_ref, b_ref, o_ref):
        s = _silu(c_ref[...]).astype(BF16)
        o_ref[...] = _dot(s, w_ref[...].astype(BF16), NN) + b_ref[...]

    return pl.pallas_call(
        body, name="ada_matmul", grid=(n // bn,),
        out_shape=jax.ShapeDtypeStruct((NDEV, n), F32),
        in_specs=[pl.BlockSpec((NDEV, d), lambda j: (0, 0)), pl.BlockSpec((d, bn), lambda j: (0, j)),
                  pl.BlockSpec((1, bn), lambda j: (0, j))],
        out_specs=pl.BlockSpec((NDEV, bn), lambda j: (0, j)),
        compiler_params=_params(("arbitrary",)),
    )(c_all, w_loc, b_loc)


def _modulated_norm(x, norm_g, mod):
    t, d = x.shape
    tt = _pick(t, 256, 16)

    def body(x_ref, g_ref, mod_ref, h_ref):
        xv = x_ref[...]
        r = lax.rsqrt(jnp.mean(xv * xv, axis=-1, keepdims=True) + EPS)
        h = (xv * r) * g_ref[...] * (1.0 + mod_ref[1:2, :]) + mod_ref[0:1, :]
        h_ref[...] = h.astype(BF16)

    return pl.pallas_call(
        body, name="modulated_norm", grid=(t // tt,),
        out_shape=jax.ShapeDtypeStruct((t, d), BF16),
        in_specs=[pl.BlockSpec((tt, d), lambda i: (i, 0)), pl.BlockSpec((1, d), lambda i: (0, 0)),
                  pl.BlockSpec((3, d), lambda i: (0, 0))],
        out_specs=pl.BlockSpec((tt, d), lambda i: (i, 0)),
        compiler_params=_params(("arbitrary",)),
    )(x, norm_g, mod)


def _gather_proj(h, w_loc):
    t, d = h.shape
    ns = w_loc.shape[1]
    tm = _pick(t, 512, 16)
    nm = t // tm
    idx = lambda p: 4 * p[0] + 2 * p[1] + p[2]

    def peers():
        x, y, c = _place()
        flip = lambda a, b: a + b - 2 * a * b
        near, far = (flip(x, c), flip(y, 1 - c)), (flip(x, 1 - c), flip(y, c))
        return (x, y, c), (x, y, 1 - c), near, far, (1 - x, 1 - y), c

    me, sibling, near, far, diag, c = peers()
    order = [me, sibling, (*near, c), (*far, 1 - c), (*far, c), (*near, 1 - c), (*diag, c), (*diag, 1 - c)]
    order = jnp.stack([idx(p) for p in order]).astype(jnp.int32)

    def body(order_ref, a_ref, w_ref, wg_ref, o_ref, slab, send_sems, recv_sems, local_sems):
        j, m = pl.program_id(0), pl.program_id(1)
        me, sibling, near, far, diag, c = peers()

        def rows(p):
            return wg_ref.at[:, pl.ds(pl.multiple_of(idx(p) * ns, LANES), ns)]

        def copy(k, block, to, src=None):
            return pltpu.make_async_remote_copy(
                src_ref=rows(block) if src is None else src, dst_ref=rows(block),
                send_sem=send_sems.at[k], recv_sem=recv_sems.at[k], device_id=to, device_id_type=MESH)

        def load(src):
            cp = pltpu.make_async_copy(src, slab, local_sems.at[1])
            cp.start()
            cp.wait()

        keep = pltpu.make_async_copy(w_ref, rows(me), local_sems.at[0])
        own = [copy(0, me, sibling, src=w_ref), copy(1, me, (*near, c), src=w_ref), copy(2, me, (*far, c), src=w_ref)]
        relay = copy(3, (*near, c), (*far, c))
        passed = [copy(4, (*near, c), sibling), copy(5, (*far, c), sibling), copy(6, (*diag, c), sibling)]
        arrivals = [(1, 0, sibling, []), (2, 1, (*near, c), [passed[0], own[2], relay]), (3, 4, (*far, 1 - c), []),
                    (4, 2, (*far, c), [passed[1]]), (5, 5, (*near, 1 - c), []),
                    (6, 3, (*diag, c), [passed[2]]), (7, 6, (*diag, 1 - c), [])]

        @pl.when((j == 0) & (m == 0))
        def _():
            keep.start()
            for cp in own[:2]:
                cp.start()
            load(w_ref)

        for step, sem, block, onward in arrivals:
            @pl.when((j == step) & (m == 0))
            def _(sem=sem, block=block, onward=onward):
                copy(sem, block, me).wait_recv()
                for cp in onward:
                    cp.start()
                load(rows(block))

        o_ref[...] = _dot(a_ref[...], slab[...], NN)

        @pl.when((j == NDEV - 1) & (m == nm - 1))
        def _():
            for cp in own + [relay] + passed:
                cp.wait_send()
            keep.wait()

    hbm = pl.BlockSpec(memory_space=pl.ANY)
    return pl.pallas_call(
        body, name="gather_proj",
        grid_spec=pltpu.PrefetchScalarGridSpec(
            num_scalar_prefetch=1, grid=(NDEV, nm),
            in_specs=[pl.BlockSpec((tm, d), lambda j, m, order_ref: (m, 0)), hbm],
            out_specs=[hbm, pl.BlockSpec((tm, ns), lambda j, m, order_ref: (m, order_ref[j]))],
            scratch_shapes=[pltpu.VMEM((d, ns), BF16), pltpu.SemaphoreType.DMA((7,)), pltpu.SemaphoreType.DMA((7,)),
                            pltpu.SemaphoreType.DMA((2,))]),
        out_shape=[jax.ShapeDtypeStruct((d, NDEV * ns), BF16), jax.ShapeDtypeStruct((t, NDEV * ns), F32)],
        compiler_params=_params(("arbitrary", "arbitrary")),
    )(order, h, w_loc)


def _attention_fwd(proj, qg, kg, nh, d_model, ride):
    t = proj.shape[0]
    tq = _pick(t, 256, 16)
    nq = t // tq
    assert 2 <= nq <= LANES
    group = 2 if nq % 2 == 0 else 1
    scale = HEAD_DIM ** -0.5

    def body(q_ref, k_ref, v_ref, g_ref, qg_ref, kg_ref, o_ref, tot_ref, y_ref, first_ref, qn, kn, vb):
        def norm(src, gain, dst):
            v = src[...]
            r = lax.rsqrt(jnp.mean(v * v, axis=-1, keepdims=True) + EPS)
            dst[...] = ((v * r) * gain[...]).astype(BF16)

        norm(q_ref, qg_ref, qn)
        norm(k_ref, kg_ref, kn)
        vb[...] = v_ref[...].astype(BF16)
        def after_matrix(n):
            return (lax.broadcasted_iota(jnp.int32, (n, n), 0) > lax.broadcasted_iota(jnp.int32, (n, n), 1)).astype(BF16)

        upper = {tq: after_matrix(tq), 2 * tq: after_matrix(2 * tq)}
        key_minus_query = (lax.broadcasted_iota(jnp.int32, (tq, 2 * tq), 1)
                           - lax.broadcasted_iota(jnp.int32, (tq, 2 * tq), 0))

        def block(qi, start, width, carry, acc, q_start=None):
            ks = pl.ds(pl.multiple_of(start, tq), width)
            z = _dot(qi, kn[ks, :], NT) * scale
            sp = _softplus(z)
            ls = -sp
            if q_start is not None:
                causal = key_minus_query < q_start - start
                ls = jnp.where(causal, ls, 0.0)
            hi = ls.astype(BF16)
            lo = (ls - hi.astype(F32)).astype(BF16)
            after = _dot(hi, upper[width], NN) + _dot(lo, upper[width], NN)
            w = jnp.exp(z - sp + after + carry)
            if q_start is not None:
                w = jnp.where(causal, w, 0.0)
            acc = acc + _dot(w.astype(BF16), vb[ks, :], NN)
            carry = carry + jnp.sum(ls, axis=1, keepdims=True)
            return carry, acc

        lane = lax.broadcasted_iota(jnp.int32, (8, LANES), 1)

        def live(carry):
            return (jnp.max(carry) > DEAD_LOG_WEIGHT).astype(jnp.int32)

        def wide_step(i):
            qi = qn[pl.ds(pl.multiple_of(i * tq, tq), tq), :]
            left = jnp.maximum(i - 1, 0)
            return block(qi, left * tq, 2 * tq, jnp.zeros((tq, 1), F32), jnp.zeros((tq, HEAD_DIM), F32), i * tq)

        def finish(i, carry, acc, firsts):
            qs = pl.ds(pl.multiple_of(i * tq, tq), tq)
            qi = qn[qs, :]
            left = jnp.maximum(i - 1, 0)

            def k_step(st):
                ca, ac = block(qi, (left - 1 - st[0]) * tq, tq, st[1], st[2])
                return st[0] + 1, ca, ac, live(ca)

            done, carry, acc, _ = lax.while_loop(
                lambda st: (st[0] < left) & (st[3] > 0), k_step, (jnp.int32(0), carry, acc, live(carry)))
            o_ref[qs, :] = acc
            tot_ref[qs, :] = jnp.broadcast_to(carry, (tq, HEAD_DIM))
            y_ref[qs, :] = (acc * _silu(g_ref[qs, :])).astype(BF16)
            return jnp.where(lane == i, (left - done).astype(F32), firsts)

        def q_group(p, firsts):
            blocks = [p + b * (nq // group) for b in range(group)]
            swept = [wide_step(i) for i in blocks]
            for i, (carry, acc) in zip(blocks, swept):
                firsts = finish(i, carry, acc, firsts)
            return firsts

        first_ref[...] = lax.fori_loop(0, nq // group, q_group, jnp.zeros((8, LANES), F32))

    col_block = lambda off: pl.BlockSpec((t, HEAD_DIM), lambda h: (0, off + h))
    vec = pl.BlockSpec((1, HEAD_DIM), lambda h: (0, 0))
    return _call(
        body, name="attention_fwd", grid=(nh,),
        out_shape=[jax.ShapeDtypeStruct((t, nh * HEAD_DIM), F32), jax.ShapeDtypeStruct((t, nh * HEAD_DIM), F32),
                   jax.ShapeDtypeStruct((t, d_model), BF16), jax.ShapeDtypeStruct((nh, 8, LANES), F32)],
        in_specs=[col_block(0), col_block(nh), col_block(2 * nh), col_block(3 * nh), vec, vec],
        out_specs=[col_block(0), col_block(0), col_block(0), pl.BlockSpec((None, 8, LANES), lambda h: (h, 0, 0))],
        scratch_shapes=[pltpu.VMEM((t, HEAD_DIM), BF16)] * 3,
        args=(proj, proj, proj, proj, qg, kg), rides=[ride])


def _fill_shifts(shifted, n_rows):
    for b in range(1, SUBLANES):
        shifted[b, pl.ds(0, n_rows), :] = shifted[0, pl.ds(b, n_rows), :]


def _shifted_rows(shifted, offset, n_rows, lo, cw):
    a, b = divmod(offset, SUBLANES)
    return shifted[b, pl.ds(SUBLANES * a, n_rows), pl.ds(lo, cw)]


def _conv_taps(shifted, w_ref, kc, lo, n_rows, cw, first_tap_row):
    acc = None
    for j in range(kc):
        term = w_ref[j:j + 1, lo:lo + cw] * _shifted_rows(shifted, first_tap_row(j), n_rows, lo, cw)
        acc = term if acc is None else acc + term
    return acc


def _glu_rows(u_ref, g_ref):
    return u_ref[...] * _sigmoid(g_ref[...])


def _conv_fwd(proj, w_dw, b_dw, ln_g, ln_b, kc, cw, ride):
    t = proj.shape[0]
    tt = _pick(t, 128, HALO)
    per = tt // HALO
    chunk = _pick(cw, 256)

    def body(u_ref, g_ref, up_ref, gp_ref, w_ref, b_ref, lg_ref, lb_ref, a_ref, hc_ref, buf):
        i = pl.program_id(0)
        buf[0, pl.ds(HALO, tt), :] = _glu_rows(u_ref, g_ref)
        halo = _glu_rows(up_ref, gp_ref)
        buf[0, pl.ds(0, HALO), :] = jnp.where(i > 0, halo, 0.0)
        _fill_shifts(buf, tt + HALO - SUBLANES)
        for lo in range(0, cw, chunk):
            conv = _conv_taps(buf, w_ref, kc, lo, tt, chunk, lambda j: HALO - (kc - 1) + j)
            hc_ref[:, lo:lo + chunk] = conv + b_ref[:, lo:lo + chunk]
        hc = hc_ref[...]
        mu = jnp.mean(hc, axis=-1, keepdims=True)
        xc = hc - mu
        var = jnp.mean(xc * xc, axis=-1, keepdims=True)
        ln = xc * lax.rsqrt(var + EPS) * lg_ref[...] + lb_ref[...]
        a_ref[...] = _silu(ln).astype(BF16)

    ncol = proj.shape[1] // cw
    tile = lambda g: pl.BlockSpec((tt, cw), lambda i: (i, g))
    prev = lambda g: pl.BlockSpec((HALO, cw), lambda i: (jnp.maximum(i * per - 1, 0), g))
    full = lambda r: pl.BlockSpec((r, cw), lambda i: (0, 0))
    return _call(
        body, name="conv_fwd", grid=(t // tt,),
        out_shape=[jax.ShapeDtypeStruct((t, cw), BF16), jax.ShapeDtypeStruct((t, cw), F32)],
        in_specs=[tile(ncol - 3), tile(ncol - 2), prev(ncol - 3), prev(ncol - 2),
                  full(w_dw.shape[0]), full(1), full(1), full(1)],
        out_specs=[pl.BlockSpec((tt, cw), lambda i: (i, 0))] * 2,
        scratch_shapes=[pltpu.VMEM((SUBLANES, HALO + tt, cw), F32)],
        args=(proj, proj, proj, proj, w_dw, b_dw, ln_g, ln_b), rides=[ride])


def _pointwise_fwd(a, wpw, b_pw, proj, ycat, cw):
    t = a.shape[0]
    tm = _pick(t, 256, 16)
    ncol = proj.shape[1] // cw
    ycol = ycat.shape[1] // cw - 1

    def body(a_ref, w_ref, b_ref, g_ref, y_in, z_ref, y_ref):
        z = _dot(a_ref[...], w_ref[...], NN) + b_ref[...]
        z_ref[...] = z
        y_ref[...] = (z * _silu(g_ref[...])).astype(BF16)

    return pl.pallas_call(
        body, name="pointwise_fwd", grid=(t // tm,),
        out_shape=[jax.ShapeDtypeStruct((t, cw), F32), jax.ShapeDtypeStruct(ycat.shape, ycat.dtype)],
        in_specs=[pl.BlockSpec((tm, cw), lambda m: (m, 0)),
                  pl.BlockSpec((cw, cw), lambda m: (0, 0)),
                  pl.BlockSpec((1, cw), lambda m: (0, 0)),
                  pl.BlockSpec((tm, cw), lambda m: (m, ncol - 1)),
                  pl.BlockSpec(memory_space=pl.ANY)],
        out_specs=[pl.BlockSpec((tm, cw), lambda m: (m, 0)), pl.BlockSpec((tm, cw), lambda m: (m, ycol))],
        input_output_aliases={4: 1},
        compiler_params=_params(("arbitrary",)),
    )(a, wpw, b_pw, proj, ycat)


def _out_matmul(ycat, wout, x, target, mod):
    t, d = x.shape
    kdim = wout.shape[0]
    tm, tn = _pick(t, 512, 16), _pick(d, 512)
    inv_d = 1.0 / d

    def body(a_ref, w_ref, x_ref, tg_ref, mod_ref, dout_ref, dy_ref, sums_ref):
        @pl.when(pl.program_id(1) == 0)
        def _():
            sums_ref[...] = jnp.zeros_like(sums_ref)

        y = _dot(a_ref[...], w_ref[...], NN)
        gate = mod_ref[2:3, :]
        err = (x_ref[...] + gate * y) - tg_ref[...]
        dout = err * inv_d
        dout_ref[...] = dout
        dy_ref[...] = (dout * gate).astype(BF16)
        sums_ref[0:1, :] += jnp.sum(dout * y, axis=0, keepdims=True)
        sums_ref[1:2, :] += jnp.sum(err * err, axis=0, keepdims=True)

    mn = lambda n, m: (m, n)
    return pl.pallas_call(
        body, name="out_matmul", grid=(d // tn, t // tm),
        out_shape=[jax.ShapeDtypeStruct((t, d), F32), jax.ShapeDtypeStruct((t, d), BF16),
                   jax.ShapeDtypeStruct((8, d), F32)],
        in_specs=[pl.BlockSpec((tm, kdim), lambda n, m: (m, 0)),
                  pl.BlockSpec((kdim, tn), lambda n, m: (0, n)),
                  pl.BlockSpec((tm, tn), mn), pl.BlockSpec((tm, tn), mn),
                  pl.BlockSpec((3, tn), lambda n, m: (0, n))],
        out_specs=[pl.BlockSpec((tm, tn), mn), pl.BlockSpec((tm, tn), mn),
                   pl.BlockSpec((8, tn), lambda n, m: (0, n))],
        compiler_params=_params(("arbitrary", "arbitrary")),
    )(ycat, wout, x, target, mod)


def _mm(a, b, form, name, tm, tn, out_dtype, *, slabs=False, n_outer=False, ksplit=1, every_other=None, rides=()):
    if form == TN:
        kdim, m_dim = a.shape
    else:
        m_dim, kdim = a.shape
    n_dim = (b.shape[0] if form == NT else b.shape[1]) // (1 if every_other is None else 2)
    tk = kdim // ksplit
    gm, gn = m_dim // tm, n_dim // tn
    mn = (lambda g: (g[1], g[0])) if n_outer else (lambda g: (g[0], g[1]))
    b_col = (lambda g: mn(g)[1]) if every_other is None else (lambda g: 2 * mn(g)[1] + g[3][0])
    a_map = (lambda *g: (g[2], mn(g)[0])) if form == TN else (lambda *g: (mn(g)[0], g[2]))
    b_map = (lambda *g: (b_col(g), g[2])) if form == NT else (lambda *g: (g[2], b_col(g)))
    a_blk = (tk, tm) if form == TN else (tm, tk)
    b_blk = (tn, tk) if form == NT else (tk, tn)
    if slabs:
        out_shape = jax.ShapeDtypeStruct((gn, m_dim, tn), out_dtype)
        out_spec = pl.BlockSpec((None, tm, tn), lambda *g: (mn(g)[1], mn(g)[0], 0))
    else:
        out_shape = jax.ShapeDtypeStruct((m_dim, n_dim), out_dtype)
        out_spec = pl.BlockSpec((tm, tn), lambda *g: mn(g))

    def body(*refs):
        a_ref, b_ref, o_ref, *acc = refs if every_other is None else refs[1:]
        part = _dot(a_ref[...], b_ref[...], form)
        if ksplit == 1:
            o_ref[...] = part.astype(out_dtype)
            return
        k = pl.program_id(2)

        @pl.when(k == 0)
        def _():
            acc[0][...] = part

        @pl.when((k > 0) & (k < ksplit - 1))
        def _():
            acc[0][...] += part

        @pl.when(k == ksplit - 1)
        def _():
            o_ref[...] = (acc[0][...] + part).astype(out_dtype)

    return _call(
        body, name=name, grid=((gn, gm) if n_outer else (gm, gn)) + (ksplit,),
        out_shape=[out_shape], in_specs=[pl.BlockSpec(a_blk, a_map), pl.BlockSpec(b_blk, b_map)],
        out_specs=[out_spec], scratch_shapes=[pltpu.VMEM((tm, tn), F32)] if ksplit > 1 else [],
        args=(a, b), rides=rides, prefetch=every_other)


def _attention_bwd(proj, o, tot, firsts, dycat, qg, kg, nh):
    t, in_cols = proj.shape
    tq = _pick(t, 256, 16)
    nq = t // tq
    group = 2 if nq % 2 == 0 else 1
    scale = HEAD_DIM ** -0.5

    def body(q_ref, k_ref, v_ref, g_ref, o_ref, tot_ref, first_ref, dy_ref, qg_ref, kg_ref, dproj_ref, gains_ref,
             qn, kn, vb, dob, dk_acc, dv_acc, dq_acc, outs, sems):
        h = pl.program_id(0)

        def norm(src, gain, dst):
            v = src[...]
            r = lax.rsqrt(jnp.mean(v * v, axis=-1, keepdims=True) + EPS)
            dst[...] = ((v * r) * gain[...]).astype(BF16)

        norm(q_ref, qg_ref, qn)
        norm(k_ref, kg_ref, kn)
        vb[...] = v_ref[...].astype(BF16)
        gs = g_ref[...]
        dyv = dy_ref[...]
        dob[...] = (dyv * _silu(gs)).astype(BF16)
        outs[3] = (dyv * o_ref[...] * _dsilu(gs)).astype(BF16)
        dk_acc[...] = jnp.zeros_like(dk_acc)
        dv_acc[...] = jnp.zeros_like(dv_acc)

        def before_matrix(n, strict):
            r = lax.broadcasted_iota(jnp.int32, (n, n), 0)
            c = lax.broadcasted_iota(jnp.int32, (n, n), 1)
            return ((c < r) if strict else (c <= r)).astype(BF16)

        incl = {n: before_matrix(n, False) for n in (tq, 2 * tq)}
        excl = {n: before_matrix(n, True) for n in (tq, 2 * tq)}
        lane = lax.broadcasted_iota(jnp.int32, (1, LANES), 1)
        key_minus_query = (lax.broadcasted_iota(jnp.int32, (2 * tq, tq), 0)
                           - lax.broadcasted_iota(jnp.int32, (2 * tq, tq), 1))

        def block(start, width, qi, doi, tot_row, p_left, g_left, dq, q_start=None):
            ks = pl.ds(pl.multiple_of(start, tq), width)
            kj = kn[ks, :]
            z = _dot(kj, qi, NT) * scale
            sp = _softplus(z)
            ls = -sp
            if q_start is not None:
                causal = key_minus_query < q_start - start
                ls = jnp.where(causal, ls, 0.0)
            hi = ls.astype(BF16)
            lo = (ls - hi.astype(F32)).astype(BF16)
            p_inc = _dot(incl[width], hi, NN) + _dot(incl[width], lo, NN) + p_left
            beta = jnp.exp(z - sp)
            w = beta * jnp.exp(tot_row - p_inc)
            if q_start is not None:
                w = jnp.where(causal, w, 0.0)
            dw = _dot(vb[ks, :], doi, NT)
            g = w * dw
            g_before = _dot(excl[width], g.astype(BF16), NN) + g_left
            dz = g * (1.0 - beta) - beta * g_before
            if q_start is not None:
                dz = jnp.where(causal, dz, 0.0)
            dzb = dz.astype(BF16)
            dv_acc[ks, :] += _dot(w.astype(BF16), doi, NN)
            dk_acc[ks, :] += _dot(dzb, qi, NN)
            dq = dq + _dot(dzb, kj, TN)
            p_left = p_left + jnp.sum(ls, axis=0, keepdims=True)
            g_left = g_left + jnp.sum(g, axis=0, keepdims=True)
            return p_left, g_left, dq

        def operands(i):
            qs = pl.ds(pl.multiple_of(i * tq, tq), tq)
            return qn[qs, :], dob[qs, :], jnp.transpose(tot_ref[qs, :])[0:1, :]

        def singles(i):
            qi, doi, tot_row = operands(i)
            zero_row = jnp.zeros((1, tq), F32)

            def k_step(j, carry):
                return block(j * tq, tq, qi, doi, tot_row, carry[0], carry[1], carry[2])

            left = jnp.maximum(i - 1, 0)
            first = jnp.sum(jnp.where(lane == i, first_ref[0:1, :], 0.0)).astype(jnp.int32)
            first = jnp.clip(first, 0, left)
            return lax.fori_loop(first, left, k_step, (zero_row, zero_row, jnp.zeros((tq, HEAD_DIM), F32)))

        def wide_step(i, carry):
            qi, doi, tot_row = operands(i)
            left = jnp.maximum(i - 1, 0)
            _, _, dq = block(left * tq, 2 * tq, qi, doi, tot_row, carry[0], carry[1], carry[2], i * tq)
            dq_acc[pl.ds(pl.multiple_of(i * tq, tq), tq), :] = dq * scale

        def q_group(p, _):
            blocks = [p + b * (nq // group) for b in range(group)]
            carries = [singles(i) for i in blocks]
            for i, carry in zip(blocks, carries):
                wide_step(i, carry)
            return 0

        lax.fori_loop(0, nq // group, q_group, 0)

        def norm_bwd(src, gain, dn, slot, gain_row):
            v = src[...]
            r = lax.rsqrt(jnp.mean(v * v, axis=-1, keepdims=True) + EPS)
            vhat = v * r
            gains_ref[gain_row:gain_row + 1, :] = jnp.sum(dn * vhat, axis=0, keepdims=True)
            dhat = dn * gain[...]
            outs[slot] = (r * (dhat - vhat * jnp.mean(dhat * vhat, axis=-1, keepdims=True))).astype(BF16)

        gains_ref[...] = jnp.zeros_like(gains_ref)
        norm_bwd(q_ref, qg_ref, dq_acc[...], 0, 0)
        norm_bwd(k_ref, kg_ref, dk_acc[...] * scale, 1, 1)
        outs[2] = dv_acc[...].astype(BF16)
        copies = [pltpu.make_async_copy(
            outs.at[s], dproj_ref.at[:, pl.ds(pl.multiple_of((s * nh + h) * HEAD_DIM, HEAD_DIM), HEAD_DIM)], sems.at[s])
            for s in range(4)]
        for cp in copies:
            cp.start()
        for cp in copies:
            cp.wait()

    col_block = lambda off: pl.BlockSpec((t, HEAD_DIM), lambda h: (0, off + h))
    vec = pl.BlockSpec((1, HEAD_DIM), lambda h: (0, 0))
    head_scr = lambda dt: pltpu.VMEM((t, HEAD_DIM), dt)
    return pl.pallas_call(
        body, name="attention_bwd", grid=(nh,),
        out_shape=[jax.ShapeDtypeStruct((t, in_cols), BF16), jax.ShapeDtypeStruct((nh, 8, HEAD_DIM), F32)],
        in_specs=[col_block(0), col_block(nh), col_block(2 * nh), col_block(3 * nh),
                  col_block(0), col_block(0), pl.BlockSpec((None, 8, LANES), lambda h: (h, 0, 0)), col_block(0), vec, vec],
        out_specs=[pl.BlockSpec(memory_space=pl.ANY), pl.BlockSpec((None, 8, HEAD_DIM), lambda h: (h, 0, 0))],
        scratch_shapes=[head_scr(BF16), head_scr(BF16), head_scr(BF16), head_scr(BF16),
                        head_scr(F32), head_scr(F32), head_scr(F32),
                        pltpu.VMEM((4, t, HEAD_DIM), BF16), pltpu.SemaphoreType.DMA((4,))],
        compiler_params=_params(("arbitrary",)),
    )(proj, proj, proj, proj, o, tot, firsts, dycat, qg, kg)


def _pointwise_bwd(dycat, z, proj, dproj, wpw, hc, ln_g, ln_b, cw):
    t = z.shape[0]
    tt = _pick(t, 256, 16)
    ncol = proj.shape[1] // cw
    ycol = dycat.shape[1] // cw - 1

    def body(dy_ref, z_ref, g_ref, w_ref, hc_ref, lg_ref, lb_ref, dp_in, dz_ref, dp_ref, dh_ref, sums_ref):
        i = pl.program_id(0)

        @pl.when(i == 0)
        def _():
            sums_ref[...] = jnp.zeros_like(sums_ref)

        g = g_ref[...]
        dy = dy_ref[...]
        dz = dy * _silu(g)
        dzb = dz.astype(BF16)
        dz_ref[...] = dzb
        dp_ref[...] = (dy * z_ref[...] * _dsilu(g)).astype(BF16)
        da = _dot(dzb, w_ref[...], NT)
        hcv = hc_ref[...]
        mu = jnp.mean(hcv, axis=-1, keepdims=True)
        xc = hcv - mu
        r = lax.rsqrt(jnp.mean(xc * xc, axis=-1, keepdims=True) + EPS)
        xhat = xc * r
        ln = xhat * lg_ref[...] + lb_ref[...]
        dln = da * _dsilu(ln)
        dxhat = dln * lg_ref[...]
        dhc = r * (dxhat - jnp.mean(dxhat, axis=-1, keepdims=True)
                   - xhat * jnp.mean(dxhat * xhat, axis=-1, keepdims=True))
        dh_ref[...] = dhc
        sums_ref[0:1, :] += jnp.sum(dz, axis=0, keepdims=True)
        sums_ref[1:2, :] += jnp.sum(dln * xhat, axis=0, keepdims=True)
        sums_ref[2:3, :] += jnp.sum(dln, axis=0, keepdims=True)
        sums_ref[3:4, :] += jnp.sum(dhc, axis=0, keepdims=True)

    tile = lambda col: pl.BlockSpec((tt, cw), lambda i: (i, col))
    vec = pl.BlockSpec((1, cw), lambda i: (0, 0))
    return pl.pallas_call(
        body, name="pointwise_bwd", grid=(t // tt,),
        out_shape=[jax.ShapeDtypeStruct((t, cw), BF16), jax.ShapeDtypeStruct(dproj.shape, dproj.dtype),
                   jax.ShapeDtypeStruct((t, cw), F32), jax.ShapeDtypeStruct((8, cw), F32)],
        in_specs=[tile(ycol), tile(0), tile(ncol - 1), pl.BlockSpec((cw, cw), lambda i: (0, 0)), tile(0), vec, vec,
                  pl.BlockSpec(memory_space=pl.ANY)],
        out_specs=[tile(0), tile(ncol - 1), tile(0), pl.BlockSpec((8, cw), lambda i: (0, 0))],
        input_output_aliases={7: 1},
        compiler_params=_params(("arbitrary",)),
    )(dycat, z, proj, wpw, hc, ln_g, ln_b, dproj)


def _conv_bwd(dhc, proj, w_dw, dproj, kc, cw):
    t = proj.shape[0]
    tt = _pick(t, 128, HALO)
    per = tt // HALO
    nt = t // tt
    chunk = _pick(cw, 256)
    ncol = proj.shape[1] // cw
    wr = w_dw.shape[0]

    def body(d_ref, dn_ref, u_ref, g_ref, up_ref, gp_ref, w_ref, dp_in, dp_ref, dw_ref, dbuf, hbuf, dw_acc):
        i = pl.program_id(0)

        @pl.when(i == 0)
        def _():
            dw_acc[...] = jnp.zeros_like(dw_acc)

        dbuf[0, pl.ds(0, tt), :] = d_ref[...]
        dbuf[0, pl.ds(tt, HALO), :] = jnp.where(i < nt - 1, dn_ref[...], 0.0)
        hbuf[0, pl.ds(HALO, tt), :] = _glu_rows(u_ref, g_ref)
        hbuf[0, pl.ds(0, HALO), :] = jnp.where(i > 0, _glu_rows(up_ref, gp_ref), 0.0)
        _fill_shifts(dbuf, tt + HALO - SUBLANES)
        _fill_shifts(hbuf, tt + HALO - SUBLANES)
        for lo in range(0, cw, chunk):
            dhg = _conv_taps(dbuf, w_ref, kc, lo, tt, chunk, lambda j: (kc - 1) - j)
            u = u_ref[:, lo:lo + chunk]
            sg = _sigmoid(g_ref[:, lo:lo + chunk])
            dp_ref[:, lo:lo + chunk] = (dhg * sg).astype(BF16)
            dp_ref[:, cw + lo:cw + lo + chunk] = (dhg * u * sg * (1.0 - sg)).astype(BF16)
            dtile = d_ref[:, lo:lo + chunk]
            for j in range(kc):
                prod = dtile * _shifted_rows(hbuf, HALO - (kc - 1) + j, tt, lo, chunk)
                dw_acc[j, :, lo:lo + chunk] += jnp.sum(prod.reshape(tt // SUBLANES, SUBLANES, chunk), axis=0)

        @pl.when(i == nt - 1)
        def _():
            dw_ref[...] = jnp.sum(dw_acc[...], axis=1)

    tile = lambda g: pl.BlockSpec((tt, cw), lambda i: (i, g))
    prev = lambda g: pl.BlockSpec((HALO, cw), lambda i: (jnp.maximum(i * per - 1, 0), g))
    return pl.pallas_call(
        body, name="conv_bwd", grid=(nt,),
        out_shape=[jax.ShapeDtypeStruct(dproj.shape, dproj.dtype), jax.ShapeDtypeStruct((wr, cw), F32)],
        in_specs=[pl.BlockSpec((tt, cw), lambda i: (i, 0)),
                  pl.BlockSpec((HALO, cw), lambda i: (jnp.minimum((i + 1) * per, nt * per - 1), 0)),
                  tile(ncol - 3), tile(ncol - 2), prev(ncol - 3), prev(ncol - 2),
                  pl.BlockSpec((wr, cw), lambda i: (0, 0)), pl.BlockSpec(memory_space=pl.ANY)],
        out_specs=[pl.BlockSpec((tt, 2 * cw), lambda i: (i, (ncol - 3) // 2)),
                   pl.BlockSpec((wr, cw), lambda i: (0, 0))],
        scratch_shapes=[pltpu.VMEM((SUBLANES, tt + HALO, cw), F32), pltpu.VMEM((SUBLANES, HALO + tt, cw), F32),
                        pltpu.VMEM((wr, SUBLANES, cw), F32)],
        input_output_aliases={7: 0},
        compiler_params=_params(("arbitrary",)),
    )(dhc, dhc, proj, proj, proj, proj, w_dw, dproj)


def _input_grad(dh, x, dout, norm_g, mod):
    t, d = x.shape
    tt = _pick(t, 128, 16)

    def body(dh_ref, x_ref, do_ref, g_ref, mod_ref, gx_ref, sums_ref):
        i = pl.program_id(0)

        @pl.when(i == 0)
        def _():
            sums_ref[...] = jnp.zeros_like(sums_ref)

        xv = x_ref[...]
        dhv = dh_ref[...]
        r = lax.rsqrt(jnp.mean(xv * xv, axis=-1, keepdims=True) + EPS)
        xn = xv * r
        g = g_ref[...]
        one_scale = 1.0 + mod_ref[1:2, :]
        dxn = dhv * g * one_scale
        gx_ref[...] = do_ref[...] + r * (dxn - xn * jnp.mean(dxn * xn, axis=-1, keepdims=True))
        sums_ref[0:1, :] += jnp.sum(dhv, axis=0, keepdims=True)
        sums_ref[1:2, :] += jnp.sum(dhv * (xn * g), axis=0, keepdims=True)
        sums_ref[2:3, :] += jnp.sum(dhv * one_scale * xn, axis=0, keepdims=True)

    tile = pl.BlockSpec((tt, d), lambda i: (i, 0))
    return pl.pallas_call(
        body, name="input_grad", grid=(t // tt,),
        out_shape=[jax.ShapeDtypeStruct((t, d), F32), jax.ShapeDtypeStruct((8, d), F32)],
        in_specs=[tile, tile, tile, pl.BlockSpec((1, d), lambda i: (0, 0)), pl.BlockSpec((3, d), lambda i: (0, 0))],
        out_specs=[tile, pl.BlockSpec((8, d), lambda i: (0, 0))],
        compiler_params=_params(("arbitrary",)),
    )(dh, x, dout, norm_g, mod)


def _sum_adam(parts, w, m, v, name):
    r, c = w.shape
    n_parts = parts.shape[0]
    tr = _pick(r, 128, 16) if r % 16 == 0 else r
    tc = _pick(c, 2048)

    def body(p_ref, w_ref, m_ref, v_ref, g_ref, d_ref, nm_ref, nv_ref):
        g = p_ref[0].astype(F32)
        for i in range(1, n_parts):
            g = g + p_ref[i].astype(F32)
        d, nm, nv = _adam(w_ref[...], g, m_ref[...], v_ref[...])
        g_ref[...] = g
        d_ref[...] = d
        nm_ref[...] = nm
        nv_ref[...] = nv

    tile = pl.BlockSpec((tr, tc), lambda i, j: (i, j))
    out = jax.ShapeDtypeStruct((r, c), F32)
    return pl.pallas_call(
        body, name=name, grid=(r // tr, c // tc),
        out_shape=[out] * 4,
        in_specs=[pl.BlockSpec((n_parts, tr, tc), lambda i, j: (0, i, j)), tile, tile, tile],
        out_specs=[tile] * 4,
        compiler_params=_params(("arbitrary", "arbitrary")),
    )(parts, w, m, v)


def _small_adam(parts, piece_rows, states):
    n_leaf = len(states)
    offsets = [sum(piece_rows[:i]) for i in range(len(piece_rows))]

    def total(p_ref, off, r):
        g = p_ref[0, off:off + r, :]
        for k in range(1, NDEV):
            g = g + p_ref[k, off:off + r, :]
        return g

    def body(p_ref, *refs):
        ins, outs = refs[:3 * n_leaf], refs[3 * n_leaf:]
        for i in range(n_leaf):
            w_ref, m_ref, v_ref = ins[3 * i:3 * i + 3]
            g = total(p_ref, offsets[i], w_ref.shape[0])
            d, nm, nv = _adam(w_ref[...], g, m_ref[...], v_ref[...])
            for o_ref, val in zip(outs[4 * i:4 * i + 4], (g, d, nm, nv)):
                o_ref[...] = val
        outs[4 * n_leaf][...] = total(p_ref, offsets[n_leaf], piece_rows[n_leaf])

    vmem = pl.BlockSpec(memory_space=pltpu.VMEM)
    flat = [a for leaf in states for a in leaf]
    out_shape = [jax.ShapeDtypeStruct(leaf[0].shape, F32) for leaf in states for _ in range(4)]
    out_shape.append(jax.ShapeDtypeStruct((piece_rows[n_leaf], LANES), F32))
    res = pl.pallas_call(
        body, name="small_adam", out_shape=out_shape,
        in_specs=[vmem] * (1 + len(flat)), out_specs=[vmem] * len(out_shape),
        compiler_params=pltpu.CompilerParams(vmem_limit_bytes=VMEM_LIMIT),
    )(parts, *flat)
    return [res[4 * i:4 * i + 4] for i in range(n_leaf)], res[4 * n_leaf]


def _ada_grad_adam(s_t, dm, w, m, v):
    d, n = w.shape
    tr = _pick(d, 256, 16)

    def body(s_ref, dm_ref, w_ref, m_ref, v_ref, g_ref, d_ref, nm_ref, nv_ref):
        g = lax.dot_general(s_ref[...], dm_ref[...], (NN, ((), ())), preferred_element_type=F32,
                            precision=lax.Precision.HIGHEST)
        dl, nm, nv = _adam(w_ref[...], g, m_ref[...], v_ref[...])
        g_ref[...] = g
        d_ref[...] = dl
        nm_ref[...] = nm
        nv_ref[...] = nv

    tile = pl.BlockSpec((tr, n), lambda i: (i, 0))
    out = jax.ShapeDtypeStruct((d, n), F32)
    return pl.pallas_call(
        body, name="ada_grad_adam", grid=(d // tr,),
        out_shape=[out] * 4,
        in_specs=[pl.BlockSpec((tr, NDEV), lambda i: (i, 0)), pl.BlockSpec((NDEV, n), lambda i: (0, 0)),
                  tile, tile, tile],
        out_specs=[tile] * 4,
        compiler_params=_params(("arbitrary",)),
    )(s_t, dm, w, m, v)


def _silu_t(c_all):
    n, d = c_all.shape

    def body(c_ref, o_ref):
        o_ref[...] = jnp.transpose(_silu(c_ref[...]))

    return pl.pallas_call(
        body, name="silu_t", out_shape=jax.ShapeDtypeStruct((d, n), F32),
        in_specs=[pl.BlockSpec(memory_space=pltpu.VMEM)], out_specs=pl.BlockSpec(memory_space=pltpu.VMEM),
        compiler_params=pltpu.CompilerParams(vmem_limit_bytes=VMEM_LIMIT),
    )(c_all)


def _rows128(v):
    return v.reshape(-1, LANES)


def _pad_rows(a, rows):
    return jnp.pad(a, ((0, rows - a.shape[0]), (0, 0)))


def kernel(x, c, norm_g, w_ada, b_ada, w_in, q_norm_g, k_norm_g, w_dw, b_dw, ln_g, ln_b, w_pw, b_pw, w_out, loss_target, m_norm_g, m_w_ada, m_b_ada, m_w_in, m_q_norm_g, m_k_norm_g, m_w_dw, m_b_dw, m_ln_g, m_ln_b, m_w_pw, m_b_pw, m_w_out, v_norm_g, v_w_ada, v_b_ada, v_w_in, v_q_norm_g, v_k_norm_g, v_w_dw, v_b_dw, v_ln_g, v_ln_b, v_w_pw, v_b_pw, v_w_out):
    _, t, d = x.shape
    n_ada = w_ada.shape[2]
    ns = w_in.shape[2]
    kc, cwl = w_dw.shape[1], w_dw.shape[2]
    cw = cwl * NDEV
    sb = d - cw
    nh = sb // HEAD_DIM
    assert sb == cw and kc - 1 <= HALO and NDEV * ns == 4 * sb + 3 * cw
    my = 4 * lax.axis_index("x") + 2 * lax.axis_index("y") + lax.axis_index("c")

    x2, tg2 = x[0], loss_target[0]

    wdw_rows = -(-kc // 8) * 8
    wdw_pad = _pad_rows(w_dw[0], wdw_rows)
    pay1 = jnp.concatenate([_rows128(c[0]), _rows128(wdw_pad.reshape(-1))], axis=0)
    (g1,) = _all_gather([pay1], "gather_cond", pltpu.VMEM)
    c_rows = d // LANES
    c_all = g1[:, :c_rows].reshape(NDEV, d)
    wdw_all = g1[:, c_rows:].reshape(NDEV, wdw_rows, cwl).transpose(1, 0, 2).reshape(wdw_rows, cw)

    b_ada_loc = lax.dynamic_slice(b_ada, (0, my * n_ada), (1, n_ada))
    mod_cols = _ada_matmul(c_all, w_ada[0], b_ada_loc)
    (g2,) = _all_gather([mod_cols], "gather_mod", pltpu.VMEM)
    mod_mine = lax.dynamic_index_in_dim(g2, my, axis=1, keepdims=False)
    mod = mod_mine.reshape(3, d)

    core = lax.axis_index("c").astype(jnp.int32).reshape(1)
    h = _modulated_norm(x2, norm_g, mod)
    wfull_in, proj = _gather_proj(h, w_in[0].astype(BF16))
    (o, tot, ycat, firsts), (partly,) = _attention_fwd(
        proj, q_norm_g, k_norm_g, nh, d, _gather_ride([w_out[0].astype(BF16), w_pw[0].astype(BF16)]))
    (a, hc), ((wg_out, wg_pw),) = _conv_fwd(proj, wdw_all, b_dw, ln_g, ln_b, kc, cw, _gather_finish_ride(partly))
    wfull_out, wfull_pw = wg_out.reshape(d, d), wg_pw.reshape(cw, cw)
    z, ycat = _pointwise_fwd(a, wfull_pw, b_pw, proj, ycat, cw)
    dout, dy, out_sums = _out_matmul(ycat, wfull_out, x2, tg2, mod)

    tile = _pick(t, 512, 16)
    (dycat,), _ = _mm(dy, wfull_out, NT, "dycat_matmul", tile, _pick(d, 512), F32)
    (p_wout,), _ = _mm(ycat, dy, TN, "w_out_grad", _pick(d, 512), _pick(d, 1024), BF16)
    p_wout = p_wout.reshape(NDEV, d // NDEV, d)
    dproj, gains = _attention_bwd(proj, o, tot, firsts, dycat, q_norm_g, k_norm_g, nh)
    dz, dproj, dhc, pw_sums = _pointwise_bwd(dycat, z, proj, dproj, wfull_pw, hc, ln_g, ln_b, cw)
    (p_wpw,), _ = _mm(a, dz, TN, "w_pw_grad", _pick(cw, 512), _pick(cw, 1024), BF16)
    p_wpw = p_wpw.reshape(NDEV, cw // NDEV, cw)
    dproj, dwdw = _conv_bwd(dhc, proj, wdw_all, dproj, kc, cw)
    p_wdw = dwdw.reshape(wdw_rows, NDEV, cwl).transpose(1, 0, 2).astype(BF16)

    def pair_sums(mine, theirs, name):
        return [_pair_sum(m, s, core, f"{name}_pair_sum_{i}") for i, (m, s) in enumerate(zip(mine, theirs))]

    lesser = [p.reshape(4, 2, *p.shape[1:]) for p in (p_wout, p_wpw, p_wdw)]
    (p_theirs,), (small_theirs,) = _mm(h, dproj, TN, "w_in_grad_sibling", _pick(d, 256), ns, BF16, slabs=True,
                                       n_outer=True, every_other=1 - core, rides=[_sibling_ride(lesser)])
    q_small = pair_sums(lesser, small_theirs, "small_grads")
    (p_mine,), (win_theirs, (r_wout, r_wpw, r_wdw)) = _mm(
        h, dproj, TN, "w_in_grad_own", _pick(d, 256), ns, BF16, slabs=True, n_outer=True, every_other=core,
        rides=[_sibling_ride([p_theirs]), _chip_exchange_ride(q_small)])
    q_win = pair_sums([p_mine[:, None]], win_theirs, "w_in_grad")
    (dh,), ((r_win,),) = _mm(dproj, wfull_in, NT, "dh_matmul", tile, _pick(d, 512), F32, ksplit=2,
                             rides=[_chip_exchange_ride(q_win)])
    grad_x, in_sums = _input_grad(dh, x2, dout, norm_g, mod)

    dmod = jnp.concatenate([in_sums[0], in_sums[1], out_sums[0]])
    loss_part = 0.5 / d * jnp.sum(out_sums[1].reshape(-1, LANES), axis=0)
    small = [in_sums[2], dmod, jnp.sum(gains[:, 0], axis=0), jnp.sum(gains[:, 1], axis=0),
             pw_sums[3], pw_sums[1], pw_sums[2], pw_sums[0], loss_part]
    pieces = [_rows128(s) for s in small]
    pieces = [_pad_rows(p, -(-p.shape[0] // 8) * 8) for p in pieces]
    (g3,) = _all_gather([jnp.concatenate(pieces, axis=0)], "gather_small", pltpu.VMEM)
    states = [[_rows128(s.reshape(-1)) for s in leaf] for leaf in (
        (norm_g, m_norm_g, v_norm_g), (b_ada, m_b_ada, v_b_ada), (q_norm_g, m_q_norm_g, v_q_norm_g),
        (k_norm_g, m_k_norm_g, v_k_norm_g), (b_dw, m_b_dw, v_b_dw), (ln_g, m_ln_g, v_ln_g),
        (ln_b, m_ln_b, v_ln_b), (b_pw, m_b_pw, v_b_pw))]
    small_out, loss_rows = _small_adam(g3, [p.shape[0] for p in pieces], states)
    g_small, d_small, m_small, v_small = [[leaf[k].reshape(1, -1) for leaf in small_out] for k in range(4)]
    loss = jnp.sum(loss_rows[0])

    off = pieces[0].shape[0]
    dmod_all = g3[:, off:off + pieces[1].shape[0]].reshape(NDEV, 3 * d)
    dmod_loc = lax.dynamic_slice(dmod_all, (0, my * n_ada), (NDEV, n_ada))
    ada = _ada_grad_adam(_silu_t(c_all), dmod_loc, w_ada[0], m_w_ada[0], v_w_ada[0])
    win = _sum_adam(r_win, w_in[0], m_w_in[0], v_w_in[0], "w_in_adam")
    wout = _sum_adam(r_wout, w_out[0], m_w_out[0], v_w_out[0], "w_out_adam")
    wpw = _sum_adam(r_wpw, w_pw[0], m_w_pw[0], v_w_pw[0], "w_pw_adam")
    wdw_state = [_pad_rows(s[0], wdw_rows) for s in (w_dw, m_w_dw, v_w_dw)]
    wdw = [r[:kc] for r in _sum_adam(r_wdw, *wdw_state, "w_dw_adam")]

    def group(k, small_list):
        s = small_list
        return [s[0], ada[k][None], s[1], win[k][None], s[2], s[3], wdw[k][None], s[4], s[5], s[6],
                wpw[k][None], s[7], wout[k][None]]

    return (loss, grad_x[None], *group(0, g_small), *group(1, d_small), *group(2, m_small), *group(3, v_small))
```

```python
import functools

import jax
import jax.numpy as jnp
from jax import lax
from jax.experimental import pallas as pl
from jax.experimental.pallas import tpu as pltpu

F32 = jnp.float32
BF16 = jnp.bfloat16
NDEV = 8
HEAD_DIM = 128
LANES = 128
SUBLANES = 8
HALO = 32
EPS = 1e-6
DEAD_LOG_WEIGHT = -104.0
VMEM_LIMIT = 56 * 1024 * 1024
MESH = pl.DeviceIdType.MESH

ADAM_LR = 0.001
ADAM_B1 = 0.9
ADAM_B2 = 0.999
ADAM_EPS = 1e-08
ADAM_WD = 0.01
ADAM_STEP = 10


def _params(sem=None):
    return pltpu.CompilerParams(dimension_semantics=sem, vmem_limit_bytes=VMEM_LIMIT)


def _pick(n, pref, unit=LANES):
    best = None
    for d in range(unit, min(n, pref) + 1, unit):
        if n % d == 0:
            best = d
    return best if best is not None else n


def _sigmoid(z):
    return 1.0 / (1.0 + jnp.exp(-z))


def _silu(z):
    return z * _sigmoid(z)


def _dsilu(z):
    s = _sigmoid(z)
    return s * (1.0 + z * (1.0 - s))


def _softplus(z):
    return jnp.maximum(z, 0.0) + jnp.log(1.0 + jnp.exp(-jnp.abs(z)))


def _dot(a, b, dims):
    return lax.dot_general(a, b, (dims, ((), ())), preferred_element_type=F32)


NN = ((1,), (0,))
NT = ((1,), (1,))
TN = ((0,), (0,))


def _adam(w, g, m, v):
    m = ADAM_B1 * m + (1.0 - ADAM_B1) * g
    v = ADAM_B2 * v + (1.0 - ADAM_B2) * (g * g)
    m_hat = m / (1.0 - ADAM_B1 ** ADAM_STEP)
    v_hat = v / (1.0 - ADAM_B2 ** ADAM_STEP)
    delta = -ADAM_LR * (m_hat / (jnp.sqrt(v_hat) + ADAM_EPS) + ADAM_WD * w)
    return delta, m, v


def _place():
    x, y, c = lax.axis_index("x"), lax.axis_index("y"), lax.axis_index("c")
    return x, y, c


def _flip(v, bit):
    return 1 - v if bit else v


def _all_gather(arrs, name, space):
    n = len(arrs)

    def body(*refs):
        ins, outs = refs[:n], refs[n:2 * n]
        send_sems, recv_sems, local_sems = refs[2 * n:]
        x, y, c = _place()
        me, sibling = (x, y, c), (x, y, 1 - c)
        chips = [(1 - x, y), (x, 1 - y), (1 - x, 1 - y)]

        def rows(a, p):
            return outs[a].at[4 * p[0] + 2 * p[1] + p[2]]

        def copy(a, k, block, to, src=None):
            return pltpu.make_async_remote_copy(
                src_ref=rows(a, block) if src is None else src, dst_ref=rows(a, block),
                send_sem=send_sems.at[7 * a + k], recv_sem=recv_sems.at[7 * a + k],
                device_id=to, device_id_type=MESH)

        mine = [pltpu.make_async_copy(ins[a], rows(a, me), local_sems.at[a]) for a in range(n)]
        for cp in mine:
            cp.start()
        first = []
        for a in range(n):
            first.append(copy(a, 0, me, sibling, src=ins[a]))
            first += [copy(a, 1 + j, me, (*chip, c), src=ins[a]) for j, chip in enumerate(chips)]
        for cp in first:
            cp.start()
        passed = []
        for j, chip in enumerate(chips):
            for a in range(n):
                copy(a, 1 + j, (*chip, c), me).wait_recv()
                fwd = copy(a, 4 + j, (*chip, c), sibling)
                fwd.start()
                passed.append(fwd)
        for a in range(n):
            copy(a, 0, sibling, me).wait_recv()
            for j, chip in enumerate(chips):
                copy(a, 4 + j, (*chip, 1 - c), me).wait_recv()
        for cp in first + passed:
            cp.wait_send()
        for cp in mine:
            cp.wait()

    spec = pl.BlockSpec(memory_space=space)
    return pl.pallas_call(
        body, name=name,
        out_shape=[jax.ShapeDtypeStruct((NDEV,) + a.shape, a.dtype) for a in arrs],
        in_specs=[spec] * n, out_specs=[spec] * n,
        scratch_shapes=[pltpu.SemaphoreType.DMA((7 * n,)), pltpu.SemaphoreType.DMA((7 * n,)),
                        pltpu.SemaphoreType.DMA((n,))],
        compiler_params=pltpu.CompilerParams(vmem_limit_bytes=VMEM_LIMIT),
    )(*arrs)


class _Ride:
    def __init__(self, ins, out_shapes, n_sems, start, finish, alias=None):
        self.ins, self.out_shapes, self.n_sems, self.start, self.finish = ins, out_shapes, n_sems, start, finish
        self.alias = alias or {}


def _call(body, *, name, grid, out_shape, in_specs, out_specs, args, scratch_shapes=(), rides=(), prefetch=None):
    sem = ("arbitrary",) * len(grid)
    rides = [r for r in rides if r is not None]
    n_pre = 0 if prefetch is None else 1
    n_in, n_out, n_scr = len(in_specs), len(out_specs), len(scratch_shapes)
    r_ins = [len(r.ins) for r in rides]
    r_outs = [len(r.out_shapes) for r in rides]

    def carried(*refs):
        pre, refs = refs[:n_pre], refs[n_pre:]
        ins, pos = refs[:n_in], n_in
        rins = []
        for k in r_ins:
            rins.append(refs[pos:pos + k])
            pos += k
        outs, pos = refs[pos:pos + n_out], pos + n_out
        routs = []
        for k in r_outs:
            routs.append(refs[pos:pos + k])
            pos += k
        scratch, pos = refs[pos:pos + n_scr], pos + n_scr
        sems = [refs[pos + 3 * i:pos + 3 * i + 3] for i in range(len(rides))]
        first = functools.reduce(lambda a, b: a & b, [pl.program_id(i) == 0 for i in range(len(grid))])
        last = functools.reduce(lambda a, b: a & b, [pl.program_id(i) == grid[i] - 1 for i in range(len(grid))])

        @pl.when(first)
        def _():
            for ride, ri, ro, s in zip(rides, rins, routs, sems):
                ride.start(ri, ro, *s)

        body(*pre, *ins, *outs, *scratch)

        @pl.when(last)
        def _():
            for ride, ri, ro, s in zip(rides, rins, routs, sems):
                ride.finish(ri, ro, *s)

    hbm = pl.BlockSpec(memory_space=pl.ANY)
    aliases, in_pos, out_pos = {}, n_pre + n_in, n_out
    for ride, ki, ko in zip(rides, r_ins, r_outs):
        aliases.update({in_pos + k: out_pos + o for k, o in ride.alias.items()})
        in_pos, out_pos = in_pos + ki, out_pos + ko
    all_scratch = list(scratch_shapes)
    for ride in rides:
        all_scratch += [pltpu.SemaphoreType.DMA((ride.n_sems,))] * 3
    all_in = list(in_specs) + [hbm] * sum(r_ins)
    all_out = list(out_specs) + [hbm] * sum(r_outs)
    shapes = list(out_shape) + [s for r in rides for s in r.out_shapes]
    operands = list(args) + [a for r in rides for a in r.ins]
    if prefetch is None:
        res = pl.pallas_call(
            carried, name=name, grid=grid, out_shape=shapes, in_specs=all_in, out_specs=all_out,
            scratch_shapes=all_scratch, input_output_aliases=aliases, compiler_params=_params(sem))(*operands)
    else:
        res = pl.pallas_call(
            carried, name=name, out_shape=shapes,
            grid_spec=pltpu.PrefetchScalarGridSpec(num_scalar_prefetch=1, grid=grid, in_specs=all_in,
                                                   out_specs=all_out, scratch_shapes=all_scratch),
            input_output_aliases=aliases, compiler_params=_params(sem))(prefetch, *operands)
    split, pos = [], n_out
    for k in r_outs:
        split.append(res[pos:pos + k])
        pos += k
    return res[:n_out], split


def _chips(x, y):
    return [(1 - x, y), (x, 1 - y), (1 - x, 1 - y)]


def _gather_ride(arrs):
    n = len(arrs)

    def copies(ins, outs, send_sems, recv_sems):
        x, y, c = _place()
        me = 4 * x + 2 * y + c
        peers = [(x, y, 1 - c)] + [(*chip, c) for chip in _chips(x, y)]
        return [pltpu.make_async_remote_copy(
            src_ref=ins[a], dst_ref=outs[a].at[me], send_sem=send_sems.at[4 * a + k], recv_sem=recv_sems.at[4 * a + k],
            device_id=p, device_id_type=MESH) for a in range(n) for k, p in enumerate(peers)], me

    def start(ins, outs, send_sems, recv_sems, local_sems):
        cps, me = copies(ins, outs, send_sems, recv_sems)
        for a in range(n):
            pltpu.make_async_copy(ins[a], outs[a].at[me], local_sems.at[a]).start()
        for cp in cps:
            cp.start()

    def finish(ins, outs, send_sems, recv_sems, local_sems):
        cps, me = copies(ins, outs, send_sems, recv_sems)
        for cp in cps:
            cp.wait_recv()
        for cp in cps:
            cp.wait_send()
        for a in range(n):
            pltpu.make_async_copy(ins[a], outs[a].at[me], local_sems.at[a]).wait()

    return _Ride(arrs, [jax.ShapeDtypeStruct((NDEV,) + a.shape, a.dtype) for a in arrs], 4 * n, start, finish)


def _gather_finish_ride(arrs):
    n = len(arrs)

    def copies(outs, send_sems, recv_sems):
        x, y, c = _place()
        cps = []
        for a in range(n):
            for k, chip in enumerate(_chips(x, y)):
                blk = 4 * chip[0] + 2 * chip[1]
                cps.append((pltpu.make_async_remote_copy(
                    src_ref=outs[a].at[blk + c], dst_ref=outs[a].at[blk + c],
                    send_sem=send_sems.at[3 * a + k], recv_sem=recv_sems.at[3 * a + k],
                    device_id=(x, y, 1 - c), device_id_type=MESH),
                    pltpu.make_async_remote_copy(
                    src_ref=outs[a].at[blk + 1 - c], dst_ref=outs[a].at[blk + 1 - c],
                    send_sem=send_sems.at[3 * a + k], recv_sem=recv_sems.at[3 * a + k],
                    device_id=(x, y, 1 - c), device_id_type=MESH)))
        return cps

    def start(ins, outs, send_sems, recv_sems, local_sems):
        for send, _ in copies(outs, send_sems, recv_sems):
            send.start()

    def finish(ins, outs, send_sems, recv_sems, local_sems):
        cps = copies(outs, send_sems, recv_sems)
        for _, recv in cps:
            recv.wait_recv()
        for send, _ in cps:
            send.wait_send()

    return _Ride(arrs, [jax.ShapeDtypeStruct(a.shape, a.dtype) for a in arrs], 3 * n, start, finish,
                 alias={a: a for a in range(n)})


def _sibling_ride(arrs):
    n = len(arrs)

    def copies(ins, outs, send_sems, recv_sems):
        x, y, c = _place()
        return [pltpu.make_async_remote_copy(
            src_ref=ins[a].at[:, 1 - c] if arrs[a].ndim == 4 else ins[a], dst_ref=outs[a],
            send_sem=send_sems.at[a], recv_sem=recv_sems.at[a],
            device_id=(x, y, 1 - c), device_id_type=MESH) for a in range(n)]

    def start(ins, outs, send_sems, recv_sems, local_sems):
        for cp in copies(ins, outs, send_sems, recv_sems):
            cp.start()

    def finish(ins, outs, send_sems, recv_sems, local_sems):
        for cp in copies(ins, outs, send_sems, recv_sems):
            cp.wait()

    return _Ride(arrs, [jax.ShapeDtypeStruct((4,) + a.shape[-2:], a.dtype) for a in arrs], n, start, finish)


def _pair_sum(mine, theirs, core, name):
    _, pick, r, c = mine.shape
    if pick == 1:
        core = jnp.zeros_like(core)
    tr = _pick(r, 512, 16)

    def body(core_ref, a_ref, b_ref, o_ref):
        o_ref[...] = (a_ref[...].astype(F32) + b_ref[...].astype(F32)).astype(BF16)

    return pl.pallas_call(
        body, name=name,
        grid_spec=pltpu.PrefetchScalarGridSpec(
            num_scalar_prefetch=1, grid=(4, r // tr),
            in_specs=[pl.BlockSpec((None, None, tr, c), lambda i, k, core_ref: (i, core_ref[0], k, 0)),
                      pl.BlockSpec((None, tr, c), lambda i, k, core_ref: (i, k, 0))],
            out_specs=pl.BlockSpec((None, tr, c), lambda i, k, core_ref: (i, k, 0))),
        out_shape=jax.ShapeDtypeStruct((4, r, c), BF16),
        compiler_params=_params(("arbitrary", "arbitrary")),
    )(core, mine, theirs)


def _chip_exchange_ride(arrs, rows=None, into=None):
    n = len(arrs)

    def part(ref):
        return ref if rows is None else ref.at[pl.ds(rows[0], rows[1]), :]

    def copies(ins, outs, send_sems, recv_sems):
        x, y, c = _place()
        mine = 2 * x + y
        return [pltpu.make_async_remote_copy(
            src_ref=part(ins[a].at[2 * chip[0] + chip[1]]), dst_ref=part(outs[a].at[mine]),
            send_sem=send_sems.at[3 * a + k], recv_sem=recv_sems.at[3 * a + k],
            device_id=(*chip, c), device_id_type=MESH) for a in range(n) for k, chip in enumerate(_chips(x, y))], mine

    def start(ins, outs, send_sems, recv_sems, local_sems):
        cps, mine = copies(ins, outs, send_sems, recv_sems)
        if into is None:
            for a in range(n):
                pltpu.make_async_copy(ins[a].at[mine], outs[a].at[mine], local_sems.at[a]).start()
        for cp in cps:
            cp.start()

    def finish(ins, outs, send_sems, recv_sems, local_sems):
        cps, mine = copies(ins, outs, send_sems, recv_sems)
        for cp in cps:
            cp.wait_recv()
        for cp in cps:
            cp.wait_send()
        if into is None:
            for a in range(n):
                pltpu.make_async_copy(ins[a].at[mine], outs[a].at[mine], local_sems.at[a]).wait()

    shapes = [jax.ShapeDtypeStruct(a.shape, a.dtype) for a in arrs]
    if into is None:
        return _Ride(arrs, shapes, 3 * n, start, finish)
    return _Ride(list(arrs) + list(into), shapes, 3 * n, start, finish, alias={n + a: a for a in range(n)})


def _ada_matmul(c_all, w_loc, b_loc):
    d, n = w_loc.shape
    bn = _pick(n, 512)

    def body(c_ref, w_ref, b_ref, o_ref):
        s = _silu(c_ref[...]).astype(BF16)
        o_ref[...] = _dot(s, w_ref[...].astype(BF16), NN) + b_ref[...]

    return pl.pallas_call(
        body, name="ada_matmul", grid=(n // bn,),
        out_shape=jax.ShapeDtypeStruct((NDEV, n), F32),
        in_specs=[pl.BlockSpec((NDEV, d), lambda j: (0, 0)), pl.BlockSpec((d, bn), lambda j: (0, j)),
                  pl.BlockSpec((1, bn), lambda j: (0, j))],
        out_specs=pl.BlockSpec((NDEV, bn), lambda j: (0, j)),
        compiler_params=_params(("arbitrary",)),
    )(c_all, w_loc, b_loc)


def _modulated_norm(x, norm_g, mod):
    t, d = x.shape
    tt = _pick(t, 256, 16)

    def body(x_ref, g_ref, mod_ref, h_ref):
        xv = x_ref[...]
        r = lax.rsqrt(jnp.mean(xv * xv, axis=-1, keepdims=True) + EPS)
        h = (xv * r) * g_ref[...] * (1.0 + mod_ref[1:2, :]) + mod_ref[0:1, :]
        h_ref[...] = h.astype(BF16)

    return pl.pallas_call(
        body, name="modulated_norm", grid=(t // tt,),
        out_shape=jax.ShapeDtypeStruct((t, d), BF16),
        in_specs=[pl.BlockSpec((tt, d), lambda i: (i, 0)), pl.BlockSpec((1, d), lambda i: (0, 0)),
                  pl.BlockSpec((3, d), lambda i: (0, 0))],
        out_specs=pl.BlockSpec((tt, d), lambda i: (i, 0)),
        compiler_params=_params(("arbitrary",)),
    )(x, norm_g, mod)


def _gather_proj(h, w_loc):
    t, d = h.shape
    ns = w_loc.shape[1]
    tm = _pick(t, 512, 16)
    nm = t // tm
    idx = lambda p: 4 * p[0] + 2 * p[1] + p[2]

    def peers():
        x, y, c = _place()
        flip = lambda a, b: a + b - 2 * a * b
        near, far = (flip(x, c), flip(y, 1 - c)), (flip(x, 1 - c), flip(y, c))
        return (x, y, c), (x, y, 1 - c), near, far, (1 - x, 1 - y), c

    me, sibling, near, far, diag, c = peers()
    order = [me, sibling, (*near, c), (*far, 1 - c), (*far, c), (*near, 1 - c), (*diag, c), (*diag, 1 - c)]
    order = jnp.stack([idx(p) for p in order]).astype(jnp.int32)

    def body(order_ref, a_ref, w_ref, wg_ref, o_ref, slab, send_sems, recv_sems, local_sems):
        j, m = pl.program_id(0), pl.program_id(1)
        me, sibling, near, far, diag, c = peers()

        def rows(p):
            return wg_ref.at[:, pl.ds(pl.multiple_of(idx(p) * ns, LANES), ns)]

        def copy(k, block, to, src=None):
            return pltpu.make_async_remote_copy(
                src_ref=rows(block) if src is None else src, dst_ref=rows(block),
                send_sem=send_sems.at[k], recv_sem=recv_sems.at[k], device_id=to, device_id_type=MESH)

        def load(src):
            cp = pltpu.make_async_copy(src, slab, local_sems.at[1])
            cp.start()
            cp.wait()

        keep = pltpu.make_async_copy(w_ref, rows(me), local_sems.at[0])
        own = [copy(0, me, sibling, src=w_ref), copy(1, me, (*near, c), src=w_ref), copy(2, me, (*far, c), src=w_ref)]
        relay = copy(3, (*near, c), (*far, c))
        passed = [copy(4, (*near, c), sibling), copy(5, (*far, c), sibling), copy(6, (*diag, c), sibling)]
        arrivals = [(1, 0, sibling, []), (2, 1, (*near, c), [passed[0], own[2], relay]), (3, 4, (*far, 1 - c), []),
                    (4, 2, (*far, c), [passed[1]]), (5, 5, (*near, 1 - c), []),
                    (6, 3, (*diag, c), [passed[2]]), (7, 6, (*diag, 1 - c), [])]

        @pl.when((j == 0) & (m == 0))
        def _():
            keep.start()
            for cp in own[:2]:
                cp.start()
            load(w_ref)

        for step, sem, block, onward in arrivals:
            @pl.when((j == step) & (m == 0))
            def _(sem=sem, block=block, onward=onward):
                copy(sem, block, me).wait_recv()
                for cp in onward:
                    cp.start()
                load(rows(block))

        o_ref[...] = _dot(a_ref[...], slab[...], NN)

        @pl.when((j == NDEV - 1) & (m == nm - 1))
        def _():
            for cp in own + [relay] + passed:
                cp.wait_send()
            keep.wait()

    hbm = pl.BlockSpec(memory_space=pl.ANY)
    return pl.pallas_call(
        body, name="gather_proj",
        grid_spec=pltpu.PrefetchScalarGridSpec(
            num_scalar_prefetch=1, grid=(NDEV, nm),
            in_specs=[pl.BlockSpec((tm, d), lambda j, m, order_ref: (m, 0)), hbm],
            out_specs=[hbm, pl.BlockSpec((tm, ns), lambda j, m, order_ref: (m, order_ref[j]))],
            scratch_shapes=[pltpu.VMEM((d, ns), BF16), pltpu.SemaphoreType.DMA((7,)), pltpu.SemaphoreType.DMA((7,)),
                            pltpu.SemaphoreType.DMA((2,))]),
        out_shape=[jax.ShapeDtypeStruct((d, NDEV * ns), BF16), jax.ShapeDtypeStruct((t, NDEV * ns), F32)],
        compiler_params=_params(("arbitrary", "arbitrary")),
    )(order, h, w_loc)


def _attention_fwd(proj, qg, kg, nh, d_model, ride):
    t = proj.shape[0]
    tq = _pick(t, 256, 16)
    nq = t // tq
    assert 2 <= nq <= LANES
    group = 2 if nq % 2 == 0 else 1
    scale = HEAD_DIM ** -0.5

    def body(q_ref, k_ref, v_ref, g_ref, qg_ref, kg_ref, o_ref, tot_ref, y_ref, first_ref, qn, kn, vb):
        def norm(src, gain, dst):
            v = src[...]
            r = lax.rsqrt(jnp.mean(v * v, axis=-1, keepdims=True) + EPS)
            dst[...] = ((v * r) * gain[...]).astype(BF16)

        norm(q_ref, qg_ref, qn)
        norm(k_ref, kg_ref, kn)
        vb[...] = v_ref[...].astype(BF16)
        def after_matrix(n):
            return (lax.broadcasted_iota(jnp.int32, (n, n), 0) > lax.broadcasted_iota(jnp.int32, (n, n), 1)).astype(BF16)

        upper = {tq: after_matrix(tq), 2 * tq: after_matrix(2 * tq)}
        key_minus_query = (lax.broadcasted_iota(jnp.int32, (tq, 2 * tq), 1)
                           - lax.broadcasted_iota(jnp.int32, (tq, 2 * tq), 0))

        def block(qi, start, width, carry, acc, q_start=None):
            ks = pl.ds(pl.multiple_of(start, tq), width)
            z = _dot(qi, kn[ks, :], NT) * scale
            sp = _softplus(z)
            ls = -sp
            if q_start is not None:
                causal = key_minus_query < q_start - start
                ls = jnp.where(causal, ls, 0.0)
            hi = ls.astype(BF16)
            lo = (ls - hi.astype(F32)).astype(BF16)
            after = _dot(hi, upper[width], NN) + _dot(lo, upper[width], NN)
            w = jnp.exp(z - sp + after + carry)
            if q_start is not None:
                w = jnp.where(causal, w, 0.0)
            acc = acc + _dot(w.astype(BF16), vb[ks, :], NN)
            carry = carry + jnp.sum(ls, axis=1, keepdims=True)
            return carry, acc

        lane = lax.broadcasted_iota(jnp.int32, (8, LANES), 1)

        def live(carry):
            return (jnp.max(carry) > DEAD_LOG_WEIGHT).astype(jnp.int32)

        def wide_step(i):
            qi = qn[pl.ds(pl.multiple_of(i * tq, tq), tq), :]
            left = jnp.maximum(i - 1, 0)
            return block(qi, left * tq, 2 * tq, jnp.zeros((tq, 1), F32), jnp.zeros((tq, HEAD_DIM), F32), i * tq)

        def finish(i, carry, acc, firsts):
            qs = pl.ds(pl.multiple_of(i * tq, tq), tq)
            qi = qn[qs, :]
            left = jnp.maximum(i - 1, 0)

            def k_step(st):
                ca, ac = block(qi, (left - 1 - st[0]) * tq, tq, st[1], st[2])
                return st[0] + 1, ca, ac, live(ca)

            done, carry, acc, _ = lax.while_loop(
                lambda st: (st[0] < left) & (st[3] > 0), k_step, (jnp.int32(0), carry, acc, live(carry)))
            o_ref[qs, :] = acc
            tot_ref[qs, :] = jnp.broadcast_to(carry, (tq, HEAD_DIM))
            y_ref[qs, :] = (acc * _silu(g_ref[qs, :])).astype(BF16)
            return jnp.where(lane == i, (left - done).astype(F32), firsts)

        def q_group(p, firsts):
            blocks = [p + b * (nq // group) for b in range(group)]
            swept = [wide_step(i) for i in blocks]
            for i, (carry, acc) in zip(blocks, swept):
                firsts = finish(i, carry, acc, firsts)
            return firsts

        first_ref[...] = lax.fori_loop(0, nq // group, q_group, jnp.zeros((8, LANES), F32))

    col_block = lambda off: pl.BlockSpec((t, HEAD_DIM), lambda h: (0, off + h))
    vec = pl.BlockSpec((1, HEAD_DIM), lambda h: (0, 0))
    return _call(
        body, name="attention_fwd", grid=(nh,),
        out_shape=[jax.ShapeDtypeStruct((t, nh * HEAD_DIM), F32), jax.ShapeDtypeStruct((t, nh * HEAD_DIM), F32),
                   jax.ShapeDtypeStruct((t, d_model), BF16), jax.ShapeDtypeStruct((nh, 8, LANES), F32)],
        in_specs=[col_block(0), col_block(nh), col_block(2 * nh), col_block(3 * nh), vec, vec],
        out_specs=[col_block(0), col_block(0), col_block(0), pl.BlockSpec((None, 8, LANES), lambda h: (h, 0, 0))],
        scratch_shapes=[pltpu.VMEM((t, HEAD_DIM), BF16)] * 3,
        args=(proj, proj, proj, proj, qg, kg), rides=[ride])


def _fill_shifts(shifted, n_rows):
    for b in range(1, SUBLANES):
        shifted[b, pl.ds(0, n_rows), :] = shifted[0, pl.ds(b, n_rows), :]


def _shifted_rows(shifted, offset, n_rows, lo, cw):
    a, b = divmod(offset, SUBLANES)
    return shifted[b, pl.ds(SUBLANES * a, n_rows), pl.ds(lo, cw)]


def _conv_taps(shifted, w_ref, kc, lo, n_rows, cw, first_tap_row):
    acc = None
    for j in range(kc):
        term = w_ref[j:j + 1, lo:lo + cw] * _shifted_rows(shifted, first_tap_row(j), n_rows, lo, cw)
        acc = term if acc is None else acc + term
    return acc


def _glu_rows(u_ref, g_ref):
    return u_ref[...] * _sigmoid(g_ref[...])


def _conv_fwd(proj, w_dw, b_dw, ln_g, ln_b, kc, cw, ride):
    t = proj.shape[0]
    tt = _pick(t, 128, HALO)
    per = tt // HALO
    chunk = _pick(cw, 256)

    def body(u_ref, g_ref, up_ref, gp_ref, w_ref, b_ref, lg_ref, lb_ref, a_ref, hc_ref, buf):
        i = pl.program_id(0)
        buf[0, pl.ds(HALO, tt), :] = _glu_rows(u_ref, g_ref)
        halo = _glu_rows(up_ref, gp_ref)
        buf[0, pl.ds(0, HALO), :] = jnp.where(i > 0, halo, 0.0)
        _fill_shifts(buf, tt + HALO - SUBLANES)
        for lo in range(0, cw, chunk):
            conv = _conv_taps(buf, w_ref, kc, lo, tt, chunk, lambda j: HALO - (kc - 1) + j)
            hc_ref[:, lo:lo + chunk] = conv + b_ref[:, lo:lo + chunk]
        hc = hc_ref[...]
        mu = jnp.mean(hc, axis=-1, keepdims=True)
        xc = hc - mu
        var = jnp.mean(xc * xc, axis=-1, keepdims=True)
        ln = xc * lax.rsqrt(var + EPS) * lg_ref[...] + lb_ref[...]
        a_ref[...] = _silu(ln).astype(BF16)

    ncol = proj.shape[1] // cw
    tile = lambda g: pl.BlockSpec((tt, cw), lambda i: (i, g))
    prev = lambda g: pl.BlockSpec((HALO, cw), lambda i: (jnp.maximum(i * per - 1, 0), g))
    full = lambda r: pl.BlockSpec((r, cw), lambda i: (0, 0))
    return _call(
        body, name="conv_fwd", grid=(t // tt,),
        out_shape=[jax.ShapeDtypeStruct((t, cw), BF16), jax.ShapeDtypeStruct((t, cw), F32)],
        in_specs=[tile(ncol - 3), tile(ncol - 2), prev(ncol - 3), prev(ncol - 2),
                  full(w_dw.shape[0]), full(1), full(1), full(1)],
        out_specs=[pl.BlockSpec((tt, cw), lambda i: (i, 0))] * 2,
        scratch_shapes=[pltpu.VMEM((SUBLANES, HALO + tt, cw), F32)],
        args=(proj, proj, proj, proj, w_dw, b_dw, ln_g, ln_b), rides=[ride])


def _pointwise_fwd(a, wpw, b_pw, proj, ycat, cw):
    t = a.shape[0]
    tm = _pick(t, 256, 16)
    ncol = proj.shape[1] // cw
    ycol = ycat.shape[1] // cw - 1

    def body(a_ref, w_ref, b_ref, g_ref, y_in, z_ref, y_ref):
        z = _dot(a_ref[...], w_ref[...], NN) + b_ref[...]
        z_ref[...] = z
        y_ref[...] = (z * _silu(g_ref[...])).astype(BF16)

    return pl.pallas_call(
        body, name="pointwise_fwd", grid=(t // tm,),
        out_shape=[jax.ShapeDtypeStruct((t, cw), F32), jax.ShapeDtypeStruct(ycat.shape, ycat.dtype)],
        in_specs=[pl.BlockSpec((tm, cw), lambda m: (m, 0)),
                  pl.BlockSpec((cw, cw), lambda m: (0, 0)),
                  pl.BlockSpec((1, cw), lambda m: (0, 0)),
                  pl.BlockSpec((tm, cw), lambda m: (m, ncol - 1)),
                  pl.BlockSpec(memory_space=pl.ANY)],
        out_specs=[pl.BlockSpec((tm, cw), lambda m: (m, 0)), pl.BlockSpec((tm, cw), lambda m: (m, ycol))],
        input_output_aliases={4: 1},
        compiler_params=_params(("arbitrary",)),
    )(a, wpw, b_pw, proj, ycat)


def _out_matmul(ycat, wout, x, target, mod):
    t, d = x.shape
    kdim = wout.shape[0]
    tm, tn = _pick(t, 512, 16), _pick(d, 512)
    inv_d = 1.0 / d

    def body(a_ref, w_ref, x_ref, tg_ref, mod_ref, dout_ref, dy_ref, sums_ref):
        @pl.when(pl.program_id(1) == 0)
        def _():
            sums_ref[...] = jnp.zeros_like(sums_ref)

        y = _dot(a_ref[...], w_ref[...], NN)
        gate = mod_ref[2:3, :]
        err = (x_ref[...] + gate * y) - tg_ref[...]
        dout = err * inv_d
        dout_ref[...] = dout
        dy_ref[...] = (dout * gate).astype(BF16)
        sums_ref[0:1, :] += jnp.sum(dout * y, axis=0, keepdims=True)
        sums_ref[1:2, :] += jnp.sum(err * err, axis=0, keepdims=True)

    mn = lambda n, m: (m, n)
    return pl.pallas_call(
        body, name="out_matmul", grid=(d // tn, t // tm),
        out_shape=[jax.ShapeDtypeStruct((t, d), F32), jax.ShapeDtypeStruct((t, d), BF16),
                   jax.ShapeDtypeStruct((8, d), F32)],
        in_specs=[pl.BlockSpec((tm, kdim), lambda n, m: (m, 0)),
                  pl.BlockSpec((kdim, tn), lambda n, m: (0, n)),
                  pl.BlockSpec((tm, tn), mn), pl.BlockSpec((tm, tn), mn),
                  pl.BlockSpec((3, tn), lambda n, m: (0, n))],
        out_specs=[pl.BlockSpec((tm, tn), mn), pl.BlockSpec((tm, tn), mn),
                   pl.BlockSpec((8, tn), lambda n, m: (0, n))],
        compiler_params=_params(("arbitrary", "arbitrary")),
    )(ycat, wout, x, target, mod)


def _mm(a, b, form, name, tm, tn, out_dtype, *, slabs=False, n_outer=False, ksplit=1, every_other=None, rides=()):
    if form == TN:
        kdim, m_dim = a.shape
    else:
        m_dim, kdim = a.shape
    n_dim = (b.shape[0] if form == NT else b.shape[1]) // (1 if every_other is None else 2)
    tk = kdim // ksplit
    gm, gn = m_dim // tm, n_dim // tn
    mn = (lambda g: (g[1], g[0])) if n_outer else (lambda g: (g[0], g[1]))
    b_col = (lambda g: mn(g)[1]) if every_other is None else (lambda g: 2 * mn(g)[1] + g[3][0])
    a_map = (lambda *g: (g[2], mn(g)[0])) if form == TN else (lambda *g: (mn(g)[0], g[2]))
    b_map = (lambda *g: (b_col(g), g[2])) if form == NT else (lambda *g: (g[2], b_col(g)))
    a_blk = (tk, tm) if form == TN else (tm, tk)
    b_blk = (tn, tk) if form == NT else (tk, tn)
    if slabs:
        out_shape = jax.ShapeDtypeStruct((gn, m_dim, tn), out_dtype)
        out_spec = pl.BlockSpec((None, tm, tn), lambda *g: (mn(g)[1], mn(g)[0], 0))
    else:
        out_shape = jax.ShapeDtypeStruct((m_dim, n_dim), out_dtype)
        out_spec = pl.BlockSpec((tm, tn), lambda *g: mn(g))

    def body(*refs):
        a_ref, b_ref, o_ref, *acc = refs if every_other is None else refs[1:]
        part = _dot(a_ref[...], b_ref[...], form)
        if ksplit == 1:
            o_ref[...] = part.astype(out_dtype)
            return
        k = pl.program_id(2)

        @pl.when(k == 0)
        def _():
            acc[0][...] = part

        @pl.when((k > 0) & (k < ksplit - 1))
        def _():
            acc[0][...] += part

        @pl.when(k == ksplit - 1)
        def _():
            o_ref[...] = (acc[0][...] + part).astype(out_dtype)

    return _call(
        body, name=name, grid=((gn, gm) if n_outer else (gm, gn)) + (ksplit,),
        out_shape=[out_shape], in_specs=[pl.BlockSpec(a_blk, a_map), pl.BlockSpec(b_blk, b_map)],
        out_specs=[out_spec], scratch_shapes=[pltpu.VMEM((tm, tn), F32)] if ksplit > 1 else [],
        args=(a, b), rides=rides, prefetch=every_other)


def _attention_bwd(proj, o, tot, firsts, dycat, qg, kg, nh):
    t, in_cols = proj.shape
    tq = _pick(t, 256, 16)
    nq = t // tq
    group = 2 if nq % 2 == 0 else 1
    scale = HEAD_DIM ** -0.5

    def body(q_ref, k_ref, v_ref, g_ref, o_ref, tot_ref, first_ref, dy_ref, qg_ref, kg_ref, dproj_ref, gains_ref,
             qn, kn, vb, dob, dk_acc, dv_acc, dq_acc, outs, sems):
        h = pl.program_id(0)

        def norm(src, gain, dst):
            v = src[...]
            r = lax.rsqrt(jnp.mean(v * v, axis=-1, keepdims=True) + EPS)
            dst[...] = ((v * r) * gain[...]).astype(BF16)

        norm(q_ref, qg_ref, qn)
        norm(k_ref, kg_ref, kn)
        vb[...] = v_ref[...].astype(BF16)
        gs = g_ref[...]
        dyv = dy_ref[...]
        dob[...] = (dyv * _silu(gs)).astype(BF16)
        outs[3] = (dyv * o_ref[...] * _dsilu(gs)).astype(BF16)
        dk_acc[...] = jnp.zeros_like(dk_acc)
        dv_acc[...] = jnp.zeros_like(dv_acc)

        def before_matrix(n, strict):
            r = lax.broadcasted_iota(jnp.int32, (n, n), 0)
            c = lax.broadcasted_iota(jnp.int32, (n, n), 1)
            return ((c < r) if strict else (c <= r)).astype(BF16)

        incl = {n: before_matrix(n, False) for n in (tq, 2 * tq)}
        excl = {n: before_matrix(n, True) for n in (tq, 2 * tq)}
        lane = lax.broadcasted_iota(jnp.int32, (1, LANES), 1)
        key_minus_query = (lax.broadcasted_iota(jnp.int32, (2 * tq, tq), 0)
                           - lax.broadcasted_iota(jnp.int32, (2 * tq, tq), 1))

        def block(start, width, qi, doi, tot_row, p_left, g_left, dq, q_start=None):
            ks = pl.ds(pl.multiple_of(start, tq), width)
            kj = kn[ks, :]
            z = _dot(kj, qi, NT) * scale
            sp = _softplus(z)
            ls = -sp
            if q_start is not None:
                causal = key_minus_query < q_start - start
                ls = jnp.where(causal, ls, 0.0)
            hi = ls.astype(BF16)
            lo = (ls - hi.astype(F32)).astype(BF16)
            p_inc = _dot(incl[width], hi, NN) + _dot(incl[width], lo, NN) + p_left
            beta = jnp.exp(z - sp)
            w = beta * jnp.exp(tot_row - p_inc)
            if q_start is not None:
                w = jnp.where(causal, w, 0.0)
            dw = _dot(vb[ks, :], doi, NT)
            g = w * dw
            g_before = _dot(excl[width], g.astype(BF16), NN) + g_left
            dz = g * (1.0 - beta) - beta * g_before
            if q_start is not None:
                dz = jnp.where(causal, dz, 0.0)
            dzb = dz.astype(BF16)
            dv_acc[ks, :] += _dot(w.astype(BF16), doi, NN)
            dk_acc[ks, :] += _dot(dzb, qi, NN)
            dq = dq + _dot(dzb, kj, TN)
            p_left = p_left + jnp.sum(ls, axis=0, keepdims=True)
            g_left = g_left + jnp.sum(g, axis=0, keepdims=True)
            return p_left, g_left, dq

        def operands(i):
            qs = pl.ds(pl.multiple_of(i * tq, tq), tq)
            return qn[qs, :], dob[qs, :], jnp.transpose(tot_ref[qs, :])[0:1, :]

        def singles(i):
            qi, doi, tot_row = operands(i)
            zero_row = jnp.zeros((1, tq), F32)

            def k_step(j, carry):
                return block(j * tq, tq, qi, doi, tot_row, carry[0], carry[1], carry[2])

            left = jnp.maximum(i - 1, 0)
            first = jnp.sum(jnp.where(lane == i, first_ref[0:1, :], 0.0)).astype(jnp.int32)
            first = jnp.clip(first, 0, left)
            return lax.fori_loop(first, left, k_step, (zero_row, zero_row, jnp.zeros((tq, HEAD_DIM), F32)))

        def wide_step(i, carry):
            qi, doi, tot_row = operands(i)
            left = jnp.maximum(i - 1, 0)
            _, _, dq = block(left * tq, 2 * tq, qi, doi, tot_row, carry[0], carry[1], carry[2], i * tq)
            dq_acc[pl.ds(pl.multiple_of(i * tq, tq), tq), :] = dq * scale

        def q_group(p, _):
            blocks = [p + b * (nq // group) for b in range(group)]
            carries = [singles(i) for i in blocks]
            for i, carry in zip(blocks, carries):
                wide_step(i, carry)
            return 0

        lax.fori_loop(0, nq // group, q_group, 0)

        def norm_bwd(src, gain, dn, slot, gain_row):
            v = src[...]
            r = lax.rsqrt(jnp.mean(v * v, axis=-1, keepdims=True) + EPS)
            vhat = v * r
            gains_ref[gain_row:gain_row + 1, :] = jnp.sum(dn * vhat, axis=0, keepdims=True)
            dhat = dn * gain[...]
            outs[slot] = (r * (dhat - vhat * jnp.mean(dhat * vhat, axis=-1, keepdims=True))).astype(BF16)

        gains_ref[...] = jnp.zeros_like(gains_ref)
        norm_bwd(q_ref, qg_ref, dq_acc[...], 0, 0)
        norm_bwd(k_ref, kg_ref, dk_acc[...] * scale, 1, 1)
        outs[2] = dv_acc[...].astype(BF16)
        copies = [pltpu.make_async_copy(
            outs.at[s], dproj_ref.at[:, pl.ds(pl.multiple_of((s * nh + h) * HEAD_DIM, HEAD_DIM), HEAD_DIM)], sems.at[s])
            for s in range(4)]
        for cp in copies:
            cp.start()
        for cp in copies:
            cp.wait()

    col_block = lambda off: pl.BlockSpec((t, HEAD_DIM), lambda h: (0, off + h))
    vec = pl.BlockSpec((1, HEAD_DIM), lambda h: (0, 0))
    head_scr = lambda dt: pltpu.VMEM((t, HEAD_DIM), dt)
    return pl.pallas_call(
        body, name="attention_bwd", grid=(nh,),
        out_shape=[jax.ShapeDtypeStruct((t, in_cols), BF16), jax.ShapeDtypeStruct((nh, 8, HEAD_DIM), F32)],
        in_specs=[col_block(0), col_block(nh), col_block(2 * nh), col_block(3 * nh),
                  col_block(0), col_block(0), pl.BlockSpec((None, 8, LANES), lambda h: (h, 0, 0)), col_block(0), vec, vec],
        out_specs=[pl.BlockSpec(memory_space=pl.ANY), pl.BlockSpec((None, 8, HEAD_DIM), lambda h: (h, 0, 0))],
        scratch_shapes=[head_scr(BF16), head_scr(BF16), head_scr(BF16), head_scr(BF16),
                        head_scr(F32), head_scr(F32), head_scr(F32),
                        pltpu.VMEM((4, t, HEAD_DIM), BF16), pltpu.SemaphoreType.DMA((4,))],
        compiler_params=_params(("arbitrary",)),
    )(proj, proj, proj, proj, o, tot, firsts, dycat, qg, kg)


def _pointwise_bwd(dycat, z, proj, dproj, wpw, hc, ln_g, ln_b, cw):
    t = z.shape[0]
    tt = _pick(t, 256, 16)
    ncol = proj.shape[1] // cw
    ycol = dycat.shape[1] // cw - 1

    def body(dy_ref, z_ref, g_ref, w_ref, hc_ref, lg_ref, lb_ref, dp_in, dz_ref, dp_ref, dh_ref, sums_ref):
        i = pl.program_id(0)

        @pl.when(i == 0)
        def _():
            sums_ref[...] = jnp.zeros_like(sums_ref)

        g = g_ref[...]
        dy = dy_ref[...]
        dz = dy * _silu(g)
        dzb = dz.astype(BF16)
        dz_ref[...] = dzb
        dp_ref[...] = (dy * z_ref[...] * _dsilu(g)).astype(BF16)
        da = _dot(dzb, w_ref[...], NT)
        hcv = hc_ref[...]
        mu = jnp.mean(hcv, axis=-1, keepdims=True)
        xc = hcv - mu
        r = lax.rsqrt(jnp.mean(xc * xc, axis=-1, keepdims=True) + EPS)
        xhat = xc * r
        ln = xhat * lg_ref[...] + lb_ref[...]
        dln = da * _dsilu(ln)
        dxhat = dln * lg_ref[...]
        dhc = r * (dxhat - jnp.mean(dxhat, axis=-1, keepdims=True)
                   - xhat * jnp.mean(dxhat * xhat, axis=-1, keepdims=True))
        dh_ref[...] = dhc
        sums_ref[0:1, :] += jnp.sum(dz, axis=0, keepdims=True)
        sums_ref[1:2, :] += jnp.sum(dln * xhat, axis=0, keepdims=True)
        sums_ref[2:3, :] += jnp.sum(dln, axis=0, keepdims=True)
        sums_ref[3:4, :] += jnp.sum(dhc, axis=0, keepdims=True)

    tile = lambda col: pl.BlockSpec((tt, cw), lambda i: (i, col))
    vec = pl.BlockSpec((1, cw), lambda i: (0, 0))
    return pl.pallas_call(
        body, name="pointwise_bwd", grid=(t // tt,),
        out_shape=[jax.ShapeDtypeStruct((t, cw), BF16), jax.ShapeDtypeStruct(dproj.shape, dproj.dtype),
                   jax.ShapeDtypeStruct((t, cw), F32), jax.ShapeDtypeStruct((8, cw), F32)],
        in_specs=[tile(ycol), tile(0), tile(ncol - 1), pl.BlockSpec((cw, cw), lambda i: (0, 0)), tile(0), vec, vec,
                  pl.BlockSpec(memory_space=pl.ANY)],
        out_specs=[tile(0), tile(ncol - 1), tile(0), pl.BlockSpec((8, cw), lambda i: (0, 0))],
        input_output_aliases={7: 1},
        compiler_params=_params(("arbitrary",)),
    )(dycat, z, proj, wpw, hc, ln_g, ln_b, dproj)


def _conv_bwd(dhc, proj, w_dw, dproj, kc, cw):
    t = proj.shape[0]
    tt = _pick(t, 128, HALO)
    per = tt // HALO
    nt = t // tt
    chunk = _pick(cw, 256)
    ncol = proj.shape[1] // cw
    wr = w_dw.shape[0]

    def body(d_ref, dn_ref, u_ref, g_ref, up_ref, gp_ref, w_ref, dp_in, dp_ref, dw_ref, dbuf, hbuf, dw_acc):
        i = pl.program_id(0)

        @pl.when(i == 0)
        def _():
            dw_acc[...] = jnp.zeros_like(dw_acc)

        dbuf[0, pl.ds(0, tt), :] = d_ref[...]
        dbuf[0, pl.ds(tt, HALO), :] = jnp.where(i < nt - 1, dn_ref[...], 0.0)
        hbuf[0, pl.ds(HALO, tt), :] = _glu_rows(u_ref, g_ref)
        hbuf[0, pl.ds(0, HALO), :] = jnp.where(i > 0, _glu_rows(up_ref, gp_ref), 0.0)
        _fill_shifts(dbuf, tt + HALO - SUBLANES)
        _fill_shifts(hbuf, tt + HALO - SUBLANES)
        for lo in range(0, cw, chunk):
            dhg = _conv_taps(dbuf, w_ref, kc, lo, tt, chunk, lambda j: (kc - 1) - j)
            u = u_ref[:, lo:lo + chunk]
            sg = _sigmoid(g_ref[:, lo:lo + chunk])
            dp_ref[:, lo:lo + chunk] = (dhg * sg).astype(BF16)
            dp_ref[:, cw + lo:cw + lo + chunk] = (dhg * u * sg * (1.0 - sg)).astype(BF16)
            dtile = d_ref[:, lo:lo + chunk]
            for j in range(kc):
                prod = dtile * _shifted_rows(hbuf, HALO - (kc - 1) + j, tt, lo, chunk)
                dw_acc[j, :, lo:lo + chunk] += jnp.sum(prod.reshape(tt // SUBLANES, SUBLANES, chunk), axis=0)

        @pl.when(i == nt - 1)
        def _():
            dw_ref[...] = jnp.sum(dw_acc[...], axis=1)

    tile = lambda g: pl.BlockSpec((tt, cw), lambda i: (i, g))
    prev = lambda g: pl.BlockSpec((HALO, cw), lambda i: (jnp.maximum(i * per - 1, 0), g))
    return pl.pallas_call(
        body, name="conv_bwd", grid=(nt,),
        out_shape=[jax.ShapeDtypeStruct(dproj.shape, dproj.dtype), jax.ShapeDtypeStruct((wr, cw), F32)],
        in_specs=[pl.BlockSpec((tt, cw), lambda i: (i, 0)),
                  pl.BlockSpec((HALO, cw), lambda i: (jnp.minimum((i + 1) * per, nt * per - 1), 0)),
                  tile(ncol - 3), tile(ncol - 2), prev(ncol - 3), prev(ncol - 2),
                  pl.BlockSpec((wr, cw), lambda i: (0, 0)), pl.BlockSpec(memory_space=pl.ANY)],
        out_specs=[pl.BlockSpec((tt, 2 * cw), lambda i: (i, (ncol - 3) // 2)),
                   pl.BlockSpec((wr, cw), lambda i: (0, 0))],
        scratch_shapes=[pltpu.VMEM((SUBLANES, tt + HALO, cw), F32), pltpu.VMEM((SUBLANES, HALO + tt, cw), F32),
                        pltpu.VMEM((wr, SUBLANES, cw), F32)],
        input_output_aliases={7: 0},
        compiler_params=_params(("arbitrary",)),
    )(dhc, dhc, proj, proj, proj, proj, w_dw, dproj)


def _input_grad(dh, x, dout, norm_g, mod, rides=()):
    t, d = x.shape
    tt = _pick(t, 128, 16)

    def body(dh_ref, x_ref, do_ref, g_ref, mod_ref, gx_ref, sums_ref):
        i = pl.program_id(0)

        @pl.when(i == 0)
        def _():
            sums_ref[...] = jnp.zeros_like(sums_ref)

        xv = x_ref[...]
        dhv = dh_ref[...]
        r = lax.rsqrt(jnp.mean(xv * xv, axis=-1, keepdims=True) + EPS)
        xn = xv * r
        g = g_ref[...]
        one_scale = 1.0 + mod_ref[1:2, :]
        dxn = dhv * g * one_scale
        gx_ref[...] = do_ref[...] + r * (dxn - xn * jnp.mean(dxn * xn, axis=-1, keepdims=True))
        sums_ref[0:1, :] += jnp.sum(dhv, axis=0, keepdims=True)
        sums_ref[1:2, :] += jnp.sum(dhv * (xn * g), axis=0, keepdims=True)
        sums_ref[2:3, :] += jnp.sum(dhv * one_scale * xn, axis=0, keepdims=True)

    tile = pl.BlockSpec((tt, d), lambda i: (i, 0))
    return _call(
        body, name="input_grad", grid=(t // tt,),
        out_shape=[jax.ShapeDtypeStruct((t, d), F32), jax.ShapeDtypeStruct((8, d), F32)],
        in_specs=[tile, tile, tile, pl.BlockSpec((1, d), lambda i: (0, 0)), pl.BlockSpec((3, d), lambda i: (0, 0))],
        out_specs=[tile, pl.BlockSpec((8, d), lambda i: (0, 0))],
        args=(dh, x, dout, norm_g, mod), rides=rides)


def _sum_adam(parts, w, m, v, name):
    r, c = w.shape
    n_parts = parts.shape[0]
    tr = _pick(r, 128, 16) if r % 16 == 0 else r
    tc = _pick(c, 2048)

    def body(p_ref, w_ref, m_ref, v_ref, g_ref, d_ref, nm_ref, nv_ref):
        g = p_ref[0].astype(F32)
        for i in range(1, n_parts):
            g = g + p_ref[i].astype(F32)
        d, nm, nv = _adam(w_ref[...], g, m_ref[...], v_ref[...])
        g_ref[...] = g
        d_ref[...] = d
        nm_ref[...] = nm
        nv_ref[...] = nv

    tile = pl.BlockSpec((tr, tc), lambda i, j: (i, j))
    out = jax.ShapeDtypeStruct((r, c), F32)
    return pl.pallas_call(
        body, name=name, grid=(r // tr, c // tc),
        out_shape=[out] * 4,
        in_specs=[pl.BlockSpec((n_parts, tr, tc), lambda i, j: (0, i, j)), tile, tile, tile],
        out_specs=[tile] * 4,
        compiler_params=_params(("arbitrary", "arbitrary")),
    )(parts, w, m, v)


def _small_adam(parts, piece_rows, states):
    n_leaf = len(states)
    offsets = [sum(piece_rows[:i]) for i in range(len(piece_rows))]

    def total(p_ref, off, r):
        g = p_ref[0, off:off + r, :]
        for k in range(1, NDEV):
            g = g + p_ref[k, off:off + r, :]
        return g

    def body(p_ref, *refs):
        ins, outs = refs[:3 * n_leaf], refs[3 * n_leaf:]
        for i in range(n_leaf):
            w_ref, m_ref, v_ref = ins[3 * i:3 * i + 3]
            g = total(p_ref, offsets[i], w_ref.shape[0])
            d, nm, nv = _adam(w_ref[...], g, m_ref[...], v_ref[...])
            for o_ref, val in zip(outs[4 * i:4 * i + 4], (g, d, nm, nv)):
                o_ref[...] = val
        outs[4 * n_leaf][...] = total(p_ref, offsets[n_leaf], piece_rows[n_leaf])

    vmem = pl.BlockSpec(memory_space=pltpu.VMEM)
    flat = [a for leaf in states for a in leaf]
    out_shape = [jax.ShapeDtypeStruct(leaf[0].shape, F32) for leaf in states for _ in range(4)]
    out_shape.append(jax.ShapeDtypeStruct((piece_rows[n_leaf], LANES), F32))
    res = pl.pallas_call(
        body, name="small_adam", out_shape=out_shape,
        in_specs=[vmem] * (1 + len(flat)), out_specs=[vmem] * len(out_shape),
        compiler_params=pltpu.CompilerParams(vmem_limit_bytes=VMEM_LIMIT),
    )(parts, *flat)
    return [res[4 * i:4 * i + 4] for i in range(n_leaf)], res[4 * n_leaf]


def _ada_grad_adam(s_t, dm, w, m, v):
    d, n = w.shape
    tr = _pick(d, 256, 16)

    def body(s_ref, dm_ref, w_ref, m_ref, v_ref, g_ref, d_ref, nm_ref, nv_ref):
        g = lax.dot_general(s_ref[...], dm_ref[...], (NN, ((), ())), preferred_element_type=F32,
                            precision=lax.Precision.HIGHEST)
        dl, nm, nv = _adam(w_ref[...], g, m_ref[...], v_ref[...])
        g_ref[...] = g
        d_ref[...] = dl
        nm_ref[...] = nm
        nv_ref[...] = nv

    tile = pl.BlockSpec((tr, n), lambda i: (i, 0))
    out = jax.ShapeDtypeStruct((d, n), F32)
    return pl.pallas_call(
        body, name="ada_grad_adam", grid=(d // tr,),
        out_shape=[out] * 4,
        in_specs=[pl.BlockSpec((tr, NDEV), lambda i: (i, 0)), pl.BlockSpec((NDEV, n), lambda i: (0, 0)),
                  tile, tile, tile],
        out_specs=[tile] * 4,
        compiler_params=_params(("arbitrary",)),
    )(s_t, dm, w, m, v)


def _silu_t(c_all):
    n, d = c_all.shape

    def body(c_ref, o_ref):
        o_ref[...] = jnp.transpose(_silu(c_ref[...]))

    return pl.pallas_call(
        body, name="silu_t", out_shape=jax.ShapeDtypeStruct((d, n), F32),
        in_specs=[pl.BlockSpec(memory_space=pltpu.VMEM)], out_specs=pl.BlockSpec(memory_space=pltpu.VMEM),
        compiler_params=pltpu.CompilerParams(vmem_limit_bytes=VMEM_LIMIT),
    )(c_all)


def _rows128(v):
    return v.reshape(-1, LANES)


def _pad_rows(a, rows):
    return jnp.pad(a, ((0, rows - a.shape[0]), (0, 0)))


def kernel(x, c, norm_g, w_ada, b_ada, w_in, q_norm_g, k_norm_g, w_dw, b_dw, ln_g, ln_b, w_pw, b_pw, w_out, loss_target, m_norm_g, m_w_ada, m_b_ada, m_w_in, m_q_norm_g, m_k_norm_g, m_w_dw, m_b_dw, m_ln_g, m_ln_b, m_w_pw, m_b_pw, m_w_out, v_norm_g, v_w_ada, v_b_ada, v_w_in, v_q_norm_g, v_k_norm_g, v_w_dw, v_b_dw, v_ln_g, v_ln_b, v_w_pw, v_b_pw, v_w_out):
    _, t, d = x.shape
    n_ada = w_ada.shape[2]
    ns = w_in.shape[2]
    kc, cwl = w_dw.shape[1], w_dw.shape[2]
    cw = cwl * NDEV
    sb = d - cw
    nh = sb // HEAD_DIM
    assert sb == cw and kc - 1 <= HALO and NDEV * ns == 4 * sb + 3 * cw
    my = 4 * lax.axis_index("x") + 2 * lax.axis_index("y") + lax.axis_index("c")

    x2, tg2 = x[0], loss_target[0]

    wdw_rows = -(-kc // 8) * 8
    wdw_pad = _pad_rows(w_dw[0], wdw_rows)
    pay1 = jnp.concatenate([_rows128(c[0]), _rows128(wdw_pad.reshape(-1))], axis=0)
    (g1,) = _all_gather([pay1], "gather_cond", pltpu.VMEM)
    c_rows = d // LANES
    c_all = g1[:, :c_rows].reshape(NDEV, d)
    wdw_all = g1[:, c_rows:].reshape(NDEV, wdw_rows, cwl).transpose(1, 0, 2).reshape(wdw_rows, cw)

    b_ada_loc = lax.dynamic_slice(b_ada, (0, my * n_ada), (1, n_ada))
    mod_cols = _ada_matmul(c_all, w_ada[0], b_ada_loc)
    (g2,) = _all_gather([mod_cols], "gather_mod", pltpu.VMEM)
    mod_mine = lax.dynamic_index_in_dim(g2, my, axis=1, keepdims=False)
    mod = mod_mine.reshape(3, d)

    core = lax.axis_index("c").astype(jnp.int32).reshape(1)
    h = _modulated_norm(x2, norm_g, mod)
    wfull_in, proj = _gather_proj(h, w_in[0].astype(BF16))
    (o, tot, ycat, firsts), (partly,) = _attention_fwd(
        proj, q_norm_g, k_norm_g, nh, d, _gather_ride([w_out[0].astype(BF16), w_pw[0].astype(BF16)]))
    (a, hc), ((wg_out, wg_pw),) = _conv_fwd(proj, wdw_all, b_dw, ln_g, ln_b, kc, cw, _gather_finish_ride(partly))
    wfull_out, wfull_pw = wg_out.reshape(d, d), wg_pw.reshape(cw, cw)
    z, ycat = _pointwise_fwd(a, wfull_pw, b_pw, proj, ycat, cw)
    dout, dy, out_sums = _out_matmul(ycat, wfull_out, x2, tg2, mod)

    tile = _pick(t, 512, 16)
    (dycat,), _ = _mm(dy, wfull_out, NT, "dycat_matmul", tile, _pick(d, 512), F32)
    (p_wout,), _ = _mm(ycat, dy, TN, "w_out_grad", _pick(d, 512), _pick(d, 1024), BF16)
    p_wout = p_wout.reshape(NDEV, d // NDEV, d)
    dproj, gains = _attention_bwd(proj, o, tot, firsts, dycat, q_norm_g, k_norm_g, nh)
    dz, dproj, dhc, pw_sums = _pointwise_bwd(dycat, z, proj, dproj, wfull_pw, hc, ln_g, ln_b, cw)
    (p_wpw,), _ = _mm(a, dz, TN, "w_pw_grad", _pick(cw, 512), _pick(cw, 1024), BF16)
    p_wpw = p_wpw.reshape(NDEV, cw // NDEV, cw)
    dproj, dwdw = _conv_bwd(dhc, proj, wdw_all, dproj, kc, cw)
    p_wdw = dwdw.reshape(wdw_rows, NDEV, cwl).transpose(1, 0, 2).astype(BF16)

    def pair_sums(mine, theirs, name):
        return [_pair_sum(m, s, core, f"{name}_pair_sum_{i}") for i, (m, s) in enumerate(zip(mine, theirs))]

    lesser = [p.reshape(4, 2, *p.shape[1:]) for p in (p_wout, p_wpw, p_wdw)]
    (p_theirs,), (small_theirs,) = _mm(h, dproj, TN, "w_in_grad_sibling", _pick(d, 256), ns, BF16, slabs=True,
                                       n_outer=True, every_other=1 - core, rides=[_sibling_ride(lesser)])
    q_small = pair_sums(lesser, small_theirs, "small_grads")
    (p_mine,), (win_theirs, (r_wout, r_wpw, r_wdw)) = _mm(
        h, dproj, TN, "w_in_grad_own", _pick(d, 256), ns, BF16, slabs=True, n_outer=True, every_other=core,
        rides=[_sibling_ride([p_theirs]), _chip_exchange_ride(q_small)])
    q_win = pair_sums([p_mine[:, None]], win_theirs, "w_in_grad")
    head = d - d // 8
    (dh,), ((r_head,),) = _mm(dproj, wfull_in, NT, "dh_matmul", tile, _pick(d, 512), F32, ksplit=2,
                              rides=[_chip_exchange_ride(q_win, rows=(0, head))])
    (grad_x, in_sums), ((r_win,),) = _input_grad(
        dh, x2, dout, norm_g, mod, rides=[_chip_exchange_ride(q_win, rows=(head, d - head), into=[r_head])])

    dmod = jnp.concatenate([in_sums[0], in_sums[1], out_sums[0]])
    loss_part = 0.5 / d * jnp.sum(out_sums[1].reshape(-1, LANES), axis=0)
    small = [in_sums[2], dmod, jnp.sum(gains[:, 0], axis=0), jnp.sum(gains[:, 1], axis=0),
             pw_sums[3], pw_sums[1], pw_sums[2], pw_sums[0], loss_part]
    pieces = [_rows128(s) for s in small]
    pieces = [_pad_rows(p, -(-p.shape[0] // 8) * 8) for p in pieces]
    (g3,) = _all_gather([jnp.concatenate(pieces, axis=0)], "gather_small", pltpu.VMEM)
    states = [[_rows128(s.reshape(-1)) for s in leaf] for leaf in (
        (norm_g, m_norm_g, v_norm_g), (b_ada, m_b_ada, v_b_ada), (q_norm_g, m_q_norm_g, v_q_norm_g),
        (k_norm_g, m_k_norm_g, v_k_norm_g), (b_dw, m_b_dw, v_b_dw), (ln_g, m_ln_g, v_ln_g),
        (ln_b, m_ln_b, v_ln_b), (b_pw, m_b_pw, v_b_pw))]
    small_out, loss_rows = _small_adam(g3, [p.shape[0] for p in pieces], states)
    g_small, d_small, m_small, v_small = [[leaf[k].reshape(1, -1) for leaf in small_out] for k in range(4)]
    loss = jnp.sum(loss_rows[0])

    off = pieces[0].shape[0]
    dmod_all = g3[:, off:off + pieces[1].shape[0]].reshape(NDEV, 3 * d)
    dmod_loc = lax.dynamic_slice(dmod_all, (0, my * n_ada), (NDEV, n_ada))
    ada = _ada_grad_adam(_silu_t(c_all), dmod_loc, w_ada[0], m_w_ada[0], v_w_ada[0])
    win = _sum_adam(r_win, w_in[0], m_w_in[0], v_w_in[0], "w_in_adam")
    wout = _sum_adam(r_wout, w_out[0], m_w_out[0], v_w_out[0], "w_out_adam")
    wpw = _sum_adam(r_wpw, w_pw[0], m_w_pw[0], v_w_pw[0], "w_pw_adam")
    wdw_state = [_pad_rows(s[0], wdw_rows) for s in (w_dw, m_w_dw, v_w_dw)]
    wdw = [r[:kc] for r in _sum_adam(r_wdw, *wdw_state, "w_dw_adam")]

    def group(k, small_list):
        s = small_list
        return [s[0], ada[k][None], s[1], win[k][None], s[2], s[3], wdw[k][None], s[4], s[5], s[6],
                wpw[k][None], s[7], wout[k][None]]

    return (loss, grad_x[None], *group(0, g_small), *group(1, d_small), *group(2, m_small), *group(3, v_small))
```

```python
import functools

import jax
import jax.numpy as jnp
from jax import lax
from jax.experimental import pallas as pl
from jax.experimental.pallas import tpu as pltpu

F32 = jnp.float32
BF16 = jnp.bfloat16
NDEV = 8
HEAD_DIM = 128
LANES = 128
SUBLANES = 8
HALO = 32
EPS = 1e-6
DEAD_LOG_WEIGHT = -104.0
VMEM_LIMIT = 56 * 1024 * 1024
MESH = pl.DeviceIdType.MESH

ADAM_LR = 0.001
ADAM_B1 = 0.9
ADAM_B2 = 0.999
ADAM_EPS = 1e-08
ADAM_WD = 0.01
ADAM_STEP = 10


def _params(sem=None):
    return pltpu.CompilerParams(dimension_semantics=sem, vmem_limit_bytes=VMEM_LIMIT)


def _pick(n, pref, unit=LANES):
    best = None
    for d in range(unit, min(n, pref) + 1, unit):
        if n % d == 0:
            best = d
    return best if best is not None else n


def _sigmoid(z):
    return 1.0 / (1.0 + jnp.exp(-z))


def _silu(z):
    return z * _sigmoid(z)


def _dsilu(z):
    s = _sigmoid(z)
    return s * (1.0 + z * (1.0 - s))


def _softplus(z):
    return jnp.maximum(z, 0.0) + jnp.log(1.0 + jnp.exp(-jnp.abs(z)))


def _dot(a, b, dims):
    return lax.dot_general(a, b, (dims, ((), ())), preferred_element_type=F32)


NN = ((1,), (0,))
NT = ((1,), (1,))
TN = ((0,), (0,))


def _adam(w, g, m, v):
    m = ADAM_B1 * m + (1.0 - ADAM_B1) * g
    v = ADAM_B2 * v + (1.0 - ADAM_B2) * (g * g)
    m_hat = m / (1.0 - ADAM_B1 ** ADAM_STEP)
    v_hat = v / (1.0 - ADAM_B2 ** ADAM_STEP)
    delta = -ADAM_LR * (m_hat / (jnp.sqrt(v_hat) + ADAM_EPS) + ADAM_WD * w)
    return delta, m, v


def _place():
    x, y, c = lax.axis_index("x"), lax.axis_index("y"), lax.axis_index("c")
    return x, y, c


def _flip(v, bit):
    return 1 - v if bit else v


def _all_gather(arrs, name, space):
    n = len(arrs)

    def body(*refs):
        ins, outs = refs[:n], refs[n:2 * n]
        send_sems, recv_sems, local_sems = refs[2 * n:]
        x, y, c = _place()
        me, sibling = (x, y, c), (x, y, 1 - c)
        chips = [(1 - x, y), (x, 1 - y), (1 - x, 1 - y)]

        def rows(a, p):
            return outs[a].at[4 * p[0] + 2 * p[1] + p[2]]

        def copy(a, k, block, to, src=None):
            return pltpu.make_async_remote_copy(
                src_ref=rows(a, block) if src is None else src, dst_ref=rows(a, block),
                send_sem=send_sems.at[7 * a + k], recv_sem=recv_sems.at[7 * a + k],
                device_id=to, device_id_type=MESH)

        mine = [pltpu.make_async_copy(ins[a], rows(a, me), local_sems.at[a]) for a in range(n)]
        for cp in mine:
            cp.start()
        first = []
        for a in range(n):
            first.append(copy(a, 0, me, sibling, src=ins[a]))
            first += [copy(a, 1 + j, me, (*chip, c), src=ins[a]) for j, chip in enumerate(chips)]
        for cp in first:
            cp.start()
        passed = []
        for j, chip in enumerate(chips):
            for a in range(n):
                copy(a, 1 + j, (*chip, c), me).wait_recv()
                fwd = copy(a, 4 + j, (*chip, c), sibling)
                fwd.start()
                passed.append(fwd)
        for a in range(n):
            copy(a, 0, sibling, me).wait_recv()
            for j, chip in enumerate(chips):
                copy(a, 4 + j, (*chip, 1 - c), me).wait_recv()
        for cp in first + passed:
            cp.wait_send()
        for cp in mine:
            cp.wait()

    spec = pl.BlockSpec(memory_space=space)
    return pl.pallas_call(
        body, name=name,
        out_shape=[jax.ShapeDtypeStruct((NDEV,) + a.shape, a.dtype) for a in arrs],
        in_specs=[spec] * n, out_specs=[spec] * n,
        scratch_shapes=[pltpu.SemaphoreType.DMA((7 * n,)), pltpu.SemaphoreType.DMA((7 * n,)),
                        pltpu.SemaphoreType.DMA((n,))],
        compiler_params=pltpu.CompilerParams(vmem_limit_bytes=VMEM_LIMIT),
    )(*arrs)


class _Ride:
    def __init__(self, ins, out_shapes, n_sems, start, finish, alias=None):
        self.ins, self.out_shapes, self.n_sems, self.start, self.finish = ins, out_shapes, n_sems, start, finish
        self.alias = alias or {}


def _call(body, *, name, grid, out_shape, in_specs, out_specs, args, scratch_shapes=(), rides=(), prefetch=None):
    sem = ("arbitrary",) * len(grid)
    rides = [r for r in rides if r is not None]
    n_pre = 0 if prefetch is None else 1
    n_in, n_out, n_scr = len(in_specs), len(out_specs), len(scratch_shapes)
    r_ins = [len(r.ins) for r in rides]
    r_outs = [len(r.out_shapes) for r in rides]

    def carried(*refs):
        pre, refs = refs[:n_pre], refs[n_pre:]
        ins, pos = refs[:n_in], n_in
        rins = []
        for k in r_ins:
            rins.append(refs[pos:pos + k])
            pos += k
        outs, pos = refs[pos:pos + n_out], pos + n_out
        routs = []
        for k in r_outs:
            routs.append(refs[pos:pos + k])
            pos += k
        scratch, pos = refs[pos:pos + n_scr], pos + n_scr
        sems = [refs[pos + 3 * i:pos + 3 * i + 3] for i in range(len(rides))]
        first = functools.reduce(lambda a, b: a & b, [pl.program_id(i) == 0 for i in range(len(grid))])
        last = functools.reduce(lambda a, b: a & b, [pl.program_id(i) == grid[i] - 1 for i in range(len(grid))])

        @pl.when(first)
        def _():
            for ride, ri, ro, s in zip(rides, rins, routs, sems):
                ride.start(ri, ro, *s)

        body(*pre, *ins, *outs, *scratch)

        @pl.when(last)
        def _():
            for ride, ri, ro, s in zip(rides, rins, routs, sems):
                ride.finish(ri, ro, *s)

    hbm = pl.BlockSpec(memory_space=pl.ANY)
    aliases, in_pos, out_pos = {}, n_pre + n_in, n_out
    for ride, ki, ko in zip(rides, r_ins, r_outs):
        aliases.update({in_pos + k: out_pos + o for k, o in ride.alias.items()})
        in_pos, out_pos = in_pos + ki, out_pos + ko
    all_scratch = list(scratch_shapes)
    for ride in rides:
        all_scratch += [pltpu.SemaphoreType.DMA((ride.n_sems,))] * 3
    all_in = list(in_specs) + [hbm] * sum(r_ins)
    all_out = list(out_specs) + [hbm] * sum(r_outs)
    shapes = list(out_shape) + [s for r in rides for s in r.out_shapes]
    operands = list(args) + [a for r in rides for a in r.ins]
    if prefetch is None:
        res = pl.pallas_call(
            carried, name=name, grid=grid, out_shape=shapes, in_specs=all_in, out_specs=all_out,
            scratch_shapes=all_scratch, input_output_aliases=aliases, compiler_params=_params(sem))(*operands)
    else:
        res = pl.pallas_call(
            carried, name=name, out_shape=shapes,
            grid_spec=pltpu.PrefetchScalarGridSpec(num_scalar_prefetch=1, grid=grid, in_specs=all_in,
                                                   out_specs=all_out, scratch_shapes=all_scratch),
            input_output_aliases=aliases, compiler_params=_params(sem))(prefetch, *operands)
    split, pos = [], n_out
    for k in r_outs:
        split.append(res[pos:pos + k])
        pos += k
    return res[:n_out], split


def _chips(x, y):
    return [(1 - x, y), (x, 1 - y), (1 - x, 1 - y)]


def _gather_ride(arrs):
    n = len(arrs)

    def copies(ins, outs, send_sems, recv_sems):
        x, y, c = _place()
        me = 4 * x + 2 * y + c
        peers = [(x, y, 1 - c)] + [(*chip, c) for chip in _chips(x, y)]
        return [pltpu.make_async_remote_copy(
            src_ref=ins[a], dst_ref=outs[a].at[me], send_sem=send_sems.at[4 * a + k], recv_sem=recv_sems.at[4 * a + k],
            device_id=p, device_id_type=MESH) for a in range(n) for k, p in enumerate(peers)], me

    def start(ins, outs, send_sems, recv_sems, local_sems):
        cps, me = copies(ins, outs, send_sems, recv_sems)
        for a in range(n):
            pltpu.make_async_copy(ins[a], outs[a].at[me], local_sems.at[a]).start()
        for cp in cps:
            cp.start()

    def finish(ins, outs, send_sems, recv_sems, local_sems):
        cps, me = copies(ins, outs, send_sems, recv_sems)
        for cp in cps:
            cp.wait_recv()
        for cp in cps:
            cp.wait_send()
        for a in range(n):
            pltpu.make_async_copy(ins[a], outs[a].at[me], local_sems.at[a]).wait()

    return _Ride(arrs, [jax.ShapeDtypeStruct((NDEV,) + a.shape, a.dtype) for a in arrs], 4 * n, start, finish)


def _gather_finish_ride(arrs):
    n = len(arrs)

    def copies(outs, send_sems, recv_sems):
        x, y, c = _place()
        cps = []
        for a in range(n):
            for k, chip in enumerate(_chips(x, y)):
                blk = 4 * chip[0] + 2 * chip[1]
                cps.append((pltpu.make_async_remote_copy(
                    src_ref=outs[a].at[blk + c], dst_ref=outs[a].at[blk + c],
                    send_sem=send_sems.at[3 * a + k], recv_sem=recv_sems.at[3 * a + k],
                    device_id=(x, y, 1 - c), device_id_type=MESH),
                    pltpu.make_async_remote_copy(
                    src_ref=outs[a].at[blk + 1 - c], dst_ref=outs[a].at[blk + 1 - c],
                    send_sem=send_sems.at[3 * a + k], recv_sem=recv_sems.at[3 * a + k],
                    device_id=(x, y, 1 - c), device_id_type=MESH)))
        return cps

    def start(ins, outs, send_sems, recv_sems, local_sems):
        for send, _ in copies(outs, send_sems, recv_sems):
            send.start()

    def finish(ins, outs, send_sems, recv_sems, local_sems):
        cps = copies(outs, send_sems, recv_sems)
        for _, recv in cps:
            recv.wait_recv()
        for send, _ in cps:
            send.wait_send()

    return _Ride(arrs, [jax.ShapeDtypeStruct(a.shape, a.dtype) for a in arrs], 3 * n, start, finish,
                 alias={a: a for a in range(n)})


def _sibling_ride(arrs):
    n = len(arrs)

    def copies(ins, outs, send_sems, recv_sems):
        x, y, c = _place()
        return [pltpu.make_async_remote_copy(
            src_ref=ins[a].at[:, 1 - c] if arrs[a].ndim == 4 else ins[a], dst_ref=outs[a],
            send_sem=send_sems.at[a], recv_sem=recv_sems.at[a],
            device_id=(x, y, 1 - c), device_id_type=MESH) for a in range(n)]

    def start(ins, outs, send_sems, recv_sems, local_sems):
        for cp in copies(ins, outs, send_sems, recv_sems):
            cp.start()

    def finish(ins, outs, send_sems, recv_sems, local_sems):
        for cp in copies(ins, outs, send_sems, recv_sems):
            cp.wait()

    return _Ride(arrs, [jax.ShapeDtypeStruct((4,) + a.shape[-2:], a.dtype) for a in arrs], n, start, finish)


def _pair_sum(mine, theirs, core, name):
    _, pick, r, c = mine.shape
    if pick == 1:
        core = jnp.zeros_like(core)
    tr = _pick(r, 512, 16)

    def body(core_ref, a_ref, b_ref, o_ref):
        o_ref[...] = (a_ref[...].astype(F32) + b_ref[...].astype(F32)).astype(BF16)

    return pl.pallas_call(
        body, name=name,
        grid_spec=pltpu.PrefetchScalarGridSpec(
            num_scalar_prefetch=1, grid=(4, r // tr),
            in_specs=[pl.BlockSpec((None, None, tr, c), lambda i, k, core_ref: (i, core_ref[0], k, 0)),
                      pl.BlockSpec((None, tr, c), lambda i, k, core_ref: (i, k, 0))],
            out_specs=pl.BlockSpec((None, tr, c), lambda i, k, core_ref: (i, k, 0))),
        out_shape=jax.ShapeDtypeStruct((4, r, c), BF16),
        compiler_params=_params(("arbitrary", "arbitrary")),
    )(core, mine, theirs)


def _chip_exchange_ride(arrs, rows=None, into=None):
    n = len(arrs)

    def part(ref):
        return ref if rows is None else ref.at[pl.ds(rows[0], rows[1]), :]

    def copies(ins, outs, send_sems, recv_sems):
        x, y, c = _place()
        mine = 2 * x + y
        return [pltpu.make_async_remote_copy(
            src_ref=part(ins[a].at[2 * chip[0] + chip[1]]), dst_ref=part(outs[a].at[mine]),
            send_sem=send_sems.at[3 * a + k], recv_sem=recv_sems.at[3 * a + k],
            device_id=(*chip, c), device_id_type=MESH) for a in range(n) for k, chip in enumerate(_chips(x, y))], mine

    def start(ins, outs, send_sems, recv_sems, local_sems):
        cps, mine = copies(ins, outs, send_sems, recv_sems)
        if into is None:
            for a in range(n):
                pltpu.make_async_copy(ins[a].at[mine], outs[a].at[mine], local_sems.at[a]).start()
        for cp in cps:
            cp.start()

    def finish(ins, outs, send_sems, recv_sems, local_sems):
        cps, mine = copies(ins, outs, send_sems, recv_sems)
        for cp in cps:
            cp.wait_recv()
        for cp in cps:
            cp.wait_send()
        if into is None:
            for a in range(n):
                pltpu.make_async_copy(ins[a].at[mine], outs[a].at[mine], local_sems.at[a]).wait()

    shapes = [jax.ShapeDtypeStruct(a.shape, a.dtype) for a in arrs]
    if into is None:
        return _Ride(arrs, shapes, 3 * n, start, finish)
    return _Ride(list(arrs) + list(into), shapes, 3 * n, start, finish, alias={n + a: a for a in range(n)})


def _ada_matmul(c_all, w_loc, b_loc):
    d, n = w_loc.shape
    bn = _pick(n, 512)

    def body(c_ref, w_ref, b_ref, o_ref):
        s = _silu(c_ref[...]).astype(BF16)
        o_ref[...] = _dot(s, w_ref[...].astype(BF16), NN) + b_ref[...]

    return pl.pallas_call(
        body, name="ada_matmul", grid=(n // bn,),
        out_shape=jax.ShapeDtypeStruct((NDEV, n), F32),
        in_specs=[pl.BlockSpec((NDEV, d), lambda j: (0, 0)), pl.BlockSpec((d, bn), lambda j: (0, j)),
                  pl.BlockSpec((1, bn), lambda j: (0, j))],
        out_specs=pl.BlockSpec((NDEV, bn), lambda j: (0, j)),
        compiler_params=_params(("arbitrary",)),
    )(c_all, w_loc, b_loc)


def _modulated_norm(x, norm_g, mod):
    t, d = x.shape
    tt = _pick(t, 256, 16)

    def body(x_ref, g_ref, mod_ref, h_ref):
        xv = x_ref[...]
        r = lax.rsqrt(jnp.mean(xv * xv, axis=-1, keepdims=True) + EPS)
        h = (xv * r) * g_ref[...] * (1.0 + mod_ref[1:2, :]) + mod_ref[0:1, :]
        h_ref[...] = h.astype(BF16)

    return pl.pallas_call(
        body, name="modulated_norm", grid=(t // tt,),
        out_shape=jax.ShapeDtypeStruct((t, d), BF16),
        in_specs=[pl.BlockSpec((tt, d), lambda i: (i, 0)), pl.BlockSpec((1, d), lambda i: (0, 0)),
                  pl.BlockSpec((3, d), lambda i: (0, 0))],
        out_specs=pl.BlockSpec((tt, d), lambda i: (i, 0)),
        compiler_params=_params(("arbitrary",)),
    )(x, norm_g, mod)


def _gather_proj(h, w_loc):
    t, d = h.shape
    ns = w_loc.shape[1]
    tm = _pick(t, 512, 16)
    nm = t // tm
    idx = lambda p: 4 * p[0] + 2 * p[1] + p[2]

    def peers():
        x, y, c = _place()
        flip = lambda a, b: a + b - 2 * a * b
        near, far = (flip(x, c), flip(y, 1 - c)), (flip(x, 1 - c), flip(y, c))
        return (x, y, c), (x, y, 1 - c), near, far, (1 - x, 1 - y), c

    me, sibling, near, far, diag, c = peers()
    order = [me, sibling, (*near, c), (*far, 1 - c), (*far, c), (*near, 1 - c), (*diag, c), (*diag, 1 - c)]
    order = jnp.stack([idx(p) for p in order]).astype(jnp.int32)

    def body(order_ref, a_ref, w_ref, wg_ref, o_ref, slab, send_sems, recv_sems, local_sems):
        j, m = pl.program_id(0), pl.program_id(1)
        me, sibling, near, far, diag, c = peers()

        def rows(p):
            return wg_ref.at[:, pl.ds(pl.multiple_of(idx(p) * ns, LANES), ns)]

        def copy(k, block, to, src=None):
            return pltpu.make_async_remote_copy(
                src_ref=rows(block) if src is None else src, dst_ref=rows(block),
                send_sem=send_sems.at[k], recv_sem=recv_sems.at[k], device_id=to, device_id_type=MESH)

        def load(src):
            cp = pltpu.make_async_copy(src, slab, local_sems.at[1])
            cp.start()
            cp.wait()

        keep = pltpu.make_async_copy(w_ref, rows(me), local_sems.at[0])
        own = [copy(0, me, sibling, src=w_ref), copy(1, me, (*near, c), src=w_ref), copy(2, me, (*far, c), src=w_ref)]
        relay = copy(3, (*near, c), (*far, c))
        passed = [copy(4, (*near, c), sibling), copy(5, (*far, c), sibling), copy(6, (*diag, c), sibling)]
        arrivals = [(1, 0, sibling, []), (2, 1, (*near, c), [passed[0], own[2], relay]), (3, 4, (*far, 1 - c), []),
                    (4, 2, (*far, c), [passed[1]]), (5, 5, (*near, 1 - c), []),
                    (6, 3, (*diag, c), [passed[2]]), (7, 6, (*diag, 1 - c), [])]

        @pl.when((j == 0) & (m == 0))
        def _():
            keep.start()
            for cp in own[:2]:
                cp.start()
            load(w_ref)

        for step, sem, block, onward in arrivals:
            @pl.when((j == step) & (m == 0))
            def _(sem=sem, block=block, onward=onward):
                copy(sem, block, me).wait_recv()
                for cp in onward:
                    cp.start()
                load(rows(block))

        o_ref[...] = _dot(a_ref[...], slab[...], NN)

        @pl.when((j == NDEV - 1) & (m == nm - 1))
        def _():
            for cp in own + [relay] + passed:
                cp.wait_send()
            keep.wait()

    hbm = pl.BlockSpec(memory_space=pl.ANY)
    return pl.pallas_call(
        body, name="gather_proj",
        grid_spec=pltpu.PrefetchScalarGridSpec(
            num_scalar_prefetch=1, grid=(NDEV, nm),
            in_specs=[pl.BlockSpec((tm, d), lambda j, m, order_ref: (m, 0)), hbm],
            out_specs=[hbm, pl.BlockSpec((tm, ns), lambda j, m, order_ref: (m, order_ref[j]))],
            scratch_shapes=[pltpu.VMEM((d, ns), BF16), pltpu.SemaphoreType.DMA((7,)), pltpu.SemaphoreType.DMA((7,)),
                            pltpu.SemaphoreType.DMA((2,))]),
        out_shape=[jax.ShapeDtypeStruct((d, NDEV * ns), BF16), jax.ShapeDtypeStruct((t, NDEV * ns), F32)],
        compiler_params=_params(("arbitrary", "arbitrary")),
    )(order, h, w_loc)


def _attention_fwd(proj, qg, kg, nh, d_model, ride):
    t = proj.shape[0]
    tq = _pick(t, 256, 16)
    nq = t // tq
    assert 2 <= nq <= LANES
    group = 2 if nq % 2 == 0 else 1
    scale = HEAD_DIM ** -0.5

    def body(q_ref, k_ref, v_ref, g_ref, qg_ref, kg_ref, o_ref, tot_ref, y_ref, first_ref, qn, kn, vb):
        def norm(src, gain, dst):
            v = src[...]
            r = lax.rsqrt(jnp.mean(v * v, axis=-1, keepdims=True) + EPS)
            dst[...] = ((v * r) * gain[...]).astype(BF16)

        norm(q_ref, qg_ref, qn)
        norm(k_ref, kg_ref, kn)
        vb[...] = v_ref[...].astype(BF16)
        def after_matrix(n):
            return (lax.broadcasted_iota(jnp.int32, (n, n), 0) > lax.broadcasted_iota(jnp.int32, (n, n), 1)).astype(BF16)

        upper = {tq: after_matrix(tq), 2 * tq: after_matrix(2 * tq)}
        key_minus_query = (lax.broadcasted_iota(jnp.int32, (tq, 2 * tq), 1)
                           - lax.broadcasted_iota(jnp.int32, (tq, 2 * tq), 0))

        def block(qi, start, width, carry, acc, q_start=None):
            ks = pl.ds(pl.multiple_of(start, tq), width)
            z = _dot(qi, kn[ks, :], NT) * scale
            sp = _softplus(z)
            ls = -sp
            if q_start is not None:
                causal = key_minus_query < q_start - start
                ls = jnp.where(causal, ls, 0.0)
            hi = ls.astype(BF16)
            lo = (ls - hi.astype(F32)).astype(BF16)
            after = _dot(hi, upper[width], NN) + _dot(lo, upper[width], NN)
            w = jnp.exp(z - sp + after + carry)
            if q_start is not None:
                w = jnp.where(causal, w, 0.0)
            acc = acc + _dot(w.astype(BF16), vb[ks, :], NN)
            carry = carry + jnp.sum(ls, axis=1, keepdims=True)
            return carry, acc

        lane = lax.broadcasted_iota(jnp.int32, (8, LANES), 1)

        def live(carry):
            return (jnp.max(carry) > DEAD_LOG_WEIGHT).astype(jnp.int32)

        def wide_step(i):
            qi = qn[pl.ds(pl.multiple_of(i * tq, tq), tq), :]
            left = jnp.maximum(i - 1, 0)
            return block(qi, left * tq, 2 * tq, jnp.zeros((tq, 1), F32), jnp.zeros((tq, HEAD_DIM), F32), i * tq)

        def finish(i, carry, acc, firsts):
            qs = pl.ds(pl.multiple_of(i * tq, tq), tq)
            qi = qn[qs, :]
            left = jnp.maximum(i - 1, 0)

            def k_step(st):
                ca, ac = block(qi, (left - 1 - st[0]) * tq, tq, st[1], st[2])
                return st[0] + 1, ca, ac, live(ca)

            done, carry, acc, _ = lax.while_loop(
                lambda st: (st[0] < left) & (st[3] > 0), k_step, (jnp.int32(0), carry, acc, live(carry)))
            o_ref[qs, :] = acc
            tot_ref[qs, :] = jnp.broadcast_to(carry, (tq, HEAD_DIM))
            y_ref[qs, :] = (acc * _silu(g_ref[qs, :])).astype(BF16)
            return jnp.where(lane == i, (left - done).astype(F32), firsts)

        def q_group(p, firsts):
            blocks = [p + b * (nq // group) for b in range(group)]
            swept = [wide_step(i) for i in blocks]
            for i, (carry, acc) in zip(blocks, swept):
                firsts = finish(i, carry, acc, firsts)
            return firsts

        first_ref[...] = lax.fori_loop(0, nq // group, q_group, jnp.zeros((8, LANES), F32))

    col_block = lambda off: pl.BlockSpec((t, HEAD_DIM), lambda h: (0, off + h))
    vec = pl.BlockSpec((1, HEAD_DIM), lambda h: (0, 0))
    return _call(
        body, name="attention_fwd", grid=(nh,),
        out_shape=[jax.ShapeDtypeStruct((t, nh * HEAD_DIM), F32), jax.ShapeDtypeStruct((t, nh * HEAD_DIM), F32),
                   jax.ShapeDtypeStruct((t, d_model), BF16), jax.ShapeDtypeStruct((nh, 8, LANES), F32)],
        in_specs=[col_block(0), col_block(nh), col_block(2 * nh), col_block(3 * nh), vec, vec],
        out_specs=[col_block(0), col_block(0), col_block(0), pl.BlockSpec((None, 8, LANES), lambda h: (h, 0, 0))],
        scratch_shapes=[pltpu.VMEM((t, HEAD_DIM), BF16)] * 3,
        args=(proj, proj, proj, proj, qg, kg), rides=[ride])


def _fill_shifts(shifted, n_rows):
    for b in range(1, SUBLANES):
        shifted[b, pl.ds(0, n_rows), :] = shifted[0, pl.ds(b, n_rows), :]


def _shifted_rows(shifted, offset, n_rows, lo, cw):
    a, b = divmod(offset, SUBLANES)
    return shifted[b, pl.ds(SUBLANES * a, n_rows), pl.ds(lo, cw)]


def _conv_taps(shifted, w_ref, kc, lo, n_rows, cw, first_tap_row):
    acc = None
    for j in range(kc):
        term = w_ref[j:j + 1, lo:lo + cw] * _shifted_rows(shifted, first_tap_row(j), n_rows, lo, cw)
        acc = term if acc is None else acc + term
    return acc


def _glu_rows(u_ref, g_ref):
    return u_ref[...] * _sigmoid(g_ref[...])


def _conv_fwd(proj, w_dw, b_dw, ln_g, ln_b, kc, cw, ride):
    t = proj.shape[0]
    tt = _pick(t, 128, HALO)
    per = tt // HALO
    chunk = _pick(cw, 256)

    def body(u_ref, g_ref, up_ref, gp_ref, w_ref, b_ref, lg_ref, lb_ref, a_ref, hc_ref, buf):
        i = pl.program_id(0)
        buf[0, pl.ds(HALO, tt), :] = _glu_rows(u_ref, g_ref)
        halo = _glu_rows(up_ref, gp_ref)
        buf[0, pl.ds(0, HALO), :] = jnp.where(i > 0, halo, 0.0)
        _fill_shifts(buf, tt + HALO - SUBLANES)
        for lo in range(0, cw, chunk):
            conv = _conv_taps(buf, w_ref, kc, lo, tt, chunk, lambda j: HALO - (kc - 1) + j)
            hc_ref[:, lo:lo + chunk] = conv + b_ref[:, lo:lo + chunk]
        hc = hc_ref[...]
        mu = jnp.mean(hc, axis=-1, keepdims=True)
        xc = hc - mu
        var = jnp.mean(xc * xc, axis=-1, keepdims=True)
        ln = xc * lax.rsqrt(var + EPS) * lg_ref[...] + lb_ref[...]
        a_ref[...] = _silu(ln).astype(BF16)

    ncol = proj.shape[1] // cw
    tile = lambda g: pl.BlockSpec((tt, cw), lambda i: (i, g))
    prev = lambda g: pl.BlockSpec((HALO, cw), lambda i: (jnp.maximum(i * per - 1, 0), g))
    full = lambda r: pl.BlockSpec((r, cw), lambda i: (0, 0))
    return _call(
        body, name="conv_fwd", grid=(t // tt,),
        out_shape=[jax.ShapeDtypeStruct((t, cw), BF16), jax.ShapeDtypeStruct((t, cw), F32)],
        in_specs=[tile(ncol - 3), tile(ncol - 2), prev(ncol - 3), prev(ncol - 2),
                  full(w_dw.shape[0]), full(1), full(1), full(1)],
        out_specs=[pl.BlockSpec((tt, cw), lambda i: (i, 0))] * 2,
        scratch_shapes=[pltpu.VMEM((SUBLANES, HALO + tt, cw), F32)],
        args=(proj, proj, proj, proj, w_dw, b_dw, ln_g, ln_b), rides=[ride])


def _pointwise_fwd(a, wpw, b_pw, proj, ycat, cw):
    t = a.shape[0]
    tm = _pick(t, 256, 16)
    ncol = proj.shape[1] // cw
    ycol = ycat.shape[1] // cw - 1

    def body(a_ref, w_ref, b_ref, g_ref, y_in, z_ref, y_ref):
        z = _dot(a_ref[...], w_ref[...], NN) + b_ref[...]
        z_ref[...] = z
        y_ref[...] = (z * _silu(g_ref[...])).astype(BF16)

    return pl.pallas_call(
        body, name="pointwise_fwd", grid=(t // tm,),
        out_shape=[jax.ShapeDtypeStruct((t, cw), F32), jax.ShapeDtypeStruct(ycat.shape, ycat.dtype)],
        in_specs=[pl.BlockSpec((tm, cw), lambda m: (m, 0)),
                  pl.BlockSpec((cw, cw), lambda m: (0, 0)),
                  pl.BlockSpec((1, cw), lambda m: (0, 0)),
                  pl.BlockSpec((tm, cw), lambda m: (m, ncol - 1)),
                  pl.BlockSpec(memory_space=pl.ANY)],
        out_specs=[pl.BlockSpec((tm, cw), lambda m: (m, 0)), pl.BlockSpec((tm, cw), lambda m: (m, ycol))],
        input_output_aliases={4: 1},
        compiler_params=_params(("arbitrary",)),
    )(a, wpw, b_pw, proj, ycat)


def _out_matmul(ycat, wout, x, target, mod):
    t, d = x.shape
    kdim = wout.shape[0]
    tm, tn = _pick(t, 512, 16), _pick(d, 1024)
    inv_d = 1.0 / d

    def body(a_ref, w_ref, x_ref, tg_ref, mod_ref, dout_ref, dy_ref, sums_ref):
        @pl.when(pl.program_id(1) == 0)
        def _():
            sums_ref[...] = jnp.zeros_like(sums_ref)

        y = _dot(a_ref[...], w_ref[...], NN)
        gate = mod_ref[2:3, :]
        err = (x_ref[...] + gate * y) - tg_ref[...]
        dout = err * inv_d
        dout_ref[...] = dout
        dy_ref[...] = (dout * gate).astype(BF16)
        sums_ref[0:1, :] += jnp.sum(dout * y, axis=0, keepdims=True)
        sums_ref[1:2, :] += jnp.sum(err * err, axis=0, keepdims=True)

    mn = lambda n, m: (m, n)
    return pl.pallas_call(
        body, name="out_matmul", grid=(d // tn, t // tm),
        out_shape=[jax.ShapeDtypeStruct((t, d), F32), jax.ShapeDtypeStruct((t, d), BF16),
                   jax.ShapeDtypeStruct((8, d), F32)],
        in_specs=[pl.BlockSpec((tm, kdim), lambda n, m: (m, 0)),
                  pl.BlockSpec((kdim, tn), lambda n, m: (0, n)),
                  pl.BlockSpec((tm, tn), mn), pl.BlockSpec((tm, tn), mn),
                  pl.BlockSpec((3, tn), lambda n, m: (0, n))],
        out_specs=[pl.BlockSpec((tm, tn), mn), pl.BlockSpec((tm, tn), mn),
                   pl.BlockSpec((8, tn), lambda n, m: (0, n))],
        compiler_params=_params(("arbitrary", "arbitrary")),
    )(ycat, wout, x, target, mod)


def _mm(a, b, form, name, tm, tn, out_dtype, *, slabs=False, n_outer=False, ksplit=1, every_other=None, rides=()):
    if form == TN:
        kdim, m_dim = a.shape
    else:
        m_dim, kdim = a.shape
    n_dim = (b.shape[0] if form == NT else b.shape[1]) // (1 if every_other is None else 2)
    tk = kdim // ksplit
    gm, gn = m_dim // tm, n_dim // tn
    mn = (lambda g: (g[1], g[0])) if n_outer else (lambda g: (g[0], g[1]))
    b_col = (lambda g: mn(g)[1]) if every_other is None else (lambda g: 2 * mn(g)[1] + g[3][0])
    a_map = (lambda *g: (g[2], mn(g)[0])) if form == TN else (lambda *g: (mn(g)[0], g[2]))
    b_map = (lambda *g: (b_col(g), g[2])) if form == NT else (lambda *g: (g[2], b_col(g)))
    a_blk = (tk, tm) if form == TN else (tm, tk)
    b_blk = (tn, tk) if form == NT else (tk, tn)
    if slabs:
        out_shape = jax.ShapeDtypeStruct((gn, m_dim, tn), out_dtype)
        out_spec = pl.BlockSpec((None, tm, tn), lambda *g: (mn(g)[1], mn(g)[0], 0))
    else:
        out_shape = jax.ShapeDtypeStruct((m_dim, n_dim), out_dtype)
        out_spec = pl.BlockSpec((tm, tn), lambda *g: mn(g))

    def body(*refs):
        a_ref, b_ref, o_ref, *acc = refs if every_other is None else refs[1:]
        part = _dot(a_ref[...], b_ref[...], form)
        if ksplit == 1:
            o_ref[...] = part.astype(out_dtype)
            return
        k = pl.program_id(2)

        @pl.when(k == 0)
        def _():
            acc[0][...] = part

        @pl.when((k > 0) & (k < ksplit - 1))
        def _():
            acc[0][...] += part

        @pl.when(k == ksplit - 1)
        def _():
            o_ref[...] = (acc[0][...] + part).astype(out_dtype)

    return _call(
        body, name=name, grid=((gn, gm) if n_outer else (gm, gn)) + (ksplit,),
        out_shape=[out_shape], in_specs=[pl.BlockSpec(a_blk, a_map), pl.BlockSpec(b_blk, b_map)],
        out_specs=[out_spec], scratch_shapes=[pltpu.VMEM((tm, tn), F32)] if ksplit > 1 else [],
        args=(a, b), rides=rides, prefetch=every_other)


def _dh_matmul(dproj, w, tm, tn, ksplit, rides):
    t, kdim = dproj.shape
    d = w.shape[0]
    tk = kdim // ksplit

    def body(a_ref, b_ref, o_ref):
        k, n = pl.program_id(1), pl.program_id(2)
        cols = pl.ds(pl.multiple_of(n * tn, tn), tn)
        part = _dot(a_ref[...], b_ref[...], NT)

        @pl.when(k == 0)
        def _():
            o_ref[:, cols] = part

        @pl.when(k > 0)
        def _():
            o_ref[:, cols] += part

    return _call(
        body, name="dh_matmul", grid=(t // tm, ksplit, d // tn),
        out_shape=[jax.ShapeDtypeStruct((t, d), F32)],
        in_specs=[pl.BlockSpec((tm, tk), lambda m, k, n: (m, k)), pl.BlockSpec((tn, tk), lambda m, k, n: (n, k))],
        out_specs=[pl.BlockSpec((tm, d), lambda m, k, n: (m, 0))],
        args=(dproj, w), rides=rides)


def _attention_bwd(proj, o, tot, firsts, dycat, qg, kg, nh):
    t, in_cols = proj.shape
    tq = _pick(t, 256, 16)
    nq = t // tq
    group = 2 if nq % 2 == 0 else 1
    scale = HEAD_DIM ** -0.5

    def body(q_ref, k_ref, v_ref, g_ref, o_ref, tot_ref, first_ref, dy_ref, qg_ref, kg_ref, dproj_ref, gains_ref,
             qn, kn, vb, dob, dk_acc, dv_acc, dq_acc, outs, sems):
        h = pl.program_id(0)

        def norm(src, gain, dst):
            v = src[...]
            r = lax.rsqrt(jnp.mean(v * v, axis=-1, keepdims=True) + EPS)
            dst[...] = ((v * r) * gain[...]).astype(BF16)

        norm(q_ref, qg_ref, qn)
        norm(k_ref, kg_ref, kn)
        vb[...] = v_ref[...].astype(BF16)
        gs = g_ref[...]
        dyv = dy_ref[...]
        dob[...] = (dyv * _silu(gs)).astype(BF16)
        outs[3] = (dyv * o_ref[...] * _dsilu(gs)).astype(BF16)
        dk_acc[...] = jnp.zeros_like(dk_acc)
        dv_acc[...] = jnp.zeros_like(dv_acc)

        def before_matrix(n, strict):
            r = lax.broadcasted_iota(jnp.int32, (n, n), 0)
            c = lax.broadcasted_iota(jnp.int32, (n, n), 1)
            return ((c < r) if strict else (c <= r)).astype(BF16)

        incl = {n: before_matrix(n, False) for n in (tq, 2 * tq)}
        excl = {n: before_matrix(n, True) for n in (tq, 2 * tq)}
        lane = lax.broadcasted_iota(jnp.int32, (1, LANES), 1)
        key_minus_query = (lax.broadcasted_iota(jnp.int32, (2 * tq, tq), 0)
                           - lax.broadcasted_iota(jnp.int32, (2 * tq, tq), 1))

        def block(start, width, qi, doi, tot_row, p_left, g_left, dq, q_start=None):
            ks = pl.ds(pl.multiple_of(start, tq), width)
            kj = kn[ks, :]
            z = _dot(kj, qi, NT) * scale
            sp = _softplus(z)
            ls = -sp
            if q_start is not None:
                causal = key_minus_query < q_start - start
                ls = jnp.where(causal, ls, 0.0)
            hi = ls.astype(BF16)
            lo = (ls - hi.astype(F32)).astype(BF16)
            p_inc = _dot(incl[width], hi, NN) + _dot(incl[width], lo, NN) + p_left
            beta = jnp.exp(z - sp)
            w = beta * jnp.exp(tot_row - p_inc)
            if q_start is not None:
                w = jnp.where(causal, w, 0.0)
            dw = _dot(vb[ks, :], doi, NT)
            g = w * dw
            g_before = _dot(excl[width], g.astype(BF16), NN) + g_left
            dz = g * (1.0 - beta) - beta * g_before
            if q_start is not None:
                dz = jnp.where(causal, dz, 0.0)
            dzb = dz.astype(BF16)
            dv_acc[ks, :] += _dot(w.astype(BF16), doi, NN)
            dk_acc[ks, :] += _dot(dzb, qi, NN)
            dq = dq + _dot(dzb, kj, TN)
            p_left = p_left + jnp.sum(ls, axis=0, keepdims=True)
            g_left = g_left + jnp.sum(g, axis=0, keepdims=True)
            return p_left, g_left, dq

        def operands(i):
            qs = pl.ds(pl.multiple_of(i * tq, tq), tq)
            return qn[qs, :], dob[qs, :], jnp.transpose(tot_ref[qs, :])[0:1, :]

        def singles(i):
            qi, doi, tot_row = operands(i)
            zero_row = jnp.zeros((1, tq), F32)

            def k_step(j, carry):
                return block(j * tq, tq, qi, doi, tot_row, carry[0], carry[1], carry[2])

            left = jnp.maximum(i - 1, 0)
            first = jnp.sum(jnp.where(lane == i, first_ref[0:1, :], 0.0)).astype(jnp.int32)
            first = jnp.clip(first, 0, left)
            return lax.fori_loop(first, left, k_step, (zero_row, zero_row, jnp.zeros((tq, HEAD_DIM), F32)))

        def wide_step(i, carry):
            qi, doi, tot_row = operands(i)
            left = jnp.maximum(i - 1, 0)
            _, _, dq = block(left * tq, 2 * tq, qi, doi, tot_row, carry[0], carry[1], carry[2], i * tq)
            dq_acc[pl.ds(pl.multiple_of(i * tq, tq), tq), :] = dq * scale

        def q_group(p, _):
            blocks = [p + b * (nq // group) for b in range(group)]
            carries = [singles(i) for i in blocks]
            for i, carry in zip(blocks, carries):
                wide_step(i, carry)
            return 0

        lax.fori_loop(0, nq // group, q_group, 0)

        def norm_bwd(src, gain, dn, slot, gain_row):
            v = src[...]
            r = lax.rsqrt(jnp.mean(v * v, axis=-1, keepdims=True) + EPS)
            vhat = v * r
            gains_ref[gain_row:gain_row + 1, :] = jnp.sum(dn * vhat, axis=0, keepdims=True)
            dhat = dn * gain[...]
            outs[slot] = (r * (dhat - vhat * jnp.mean(dhat * vhat, axis=-1, keepdims=True))).astype(BF16)

        gains_ref[...] = jnp.zeros_like(gains_ref)
        norm_bwd(q_ref, qg_ref, dq_acc[...], 0, 0)
        norm_bwd(k_ref, kg_ref, dk_acc[...] * scale, 1, 1)
        outs[2] = dv_acc[...].astype(BF16)
        copies = [pltpu.make_async_copy(
            outs.at[s], dproj_ref.at[:, pl.ds(pl.multiple_of((s * nh + h) * HEAD_DIM, HEAD_DIM), HEAD_DIM)], sems.at[s])
            for s in range(4)]
        for cp in copies:
            cp.start()
        for cp in copies:
            cp.wait()

    col_block = lambda off: pl.BlockSpec((t, HEAD_DIM), lambda h: (0, off + h))
    vec = pl.BlockSpec((1, HEAD_DIM), lambda h: (0, 0))
    head_scr = lambda dt: pltpu.VMEM((t, HEAD_DIM), dt)
    return pl.pallas_call(
        body, name="attention_bwd", grid=(nh,),
        out_shape=[jax.ShapeDtypeStruct((t, in_cols), BF16), jax.ShapeDtypeStruct((nh, 8, HEAD_DIM), F32)],
        in_specs=[col_block(0), col_block(nh), col_block(2 * nh), col_block(3 * nh),
                  col_block(0), col_block(0), pl.BlockSpec((None, 8, LANES), lambda h: (h, 0, 0)), col_block(0), vec, vec],
        out_specs=[pl.BlockSpec(memory_space=pl.ANY), pl.BlockSpec((None, 8, HEAD_DIM), lambda h: (h, 0, 0))],
        scratch_shapes=[head_scr(BF16), head_scr(BF16), head_scr(BF16), head_scr(BF16),
                        head_scr(F32), head_scr(F32), head_scr(F32),
                        pltpu.VMEM((4, t, HEAD_DIM), BF16), pltpu.SemaphoreType.DMA((4,))],
        compiler_params=_params(("arbitrary",)),
    )(proj, proj, proj, proj, o, tot, firsts, dycat, qg, kg)


def _pointwise_bwd(dycat, z, proj, dproj, wpw, hc, ln_g, ln_b, cw):
    t = z.shape[0]
    tt = _pick(t, 256, 16)
    ncol = proj.shape[1] // cw
    ycol = dycat.shape[1] // cw - 1

    def body(dy_ref, z_ref, g_ref, w_ref, hc_ref, lg_ref, lb_ref, dp_in, dz_ref, dp_ref, dh_ref, sums_ref):
        i = pl.program_id(0)

        @pl.when(i == 0)
        def _():
            sums_ref[...] = jnp.zeros_like(sums_ref)

        g = g_ref[...]
        dy = dy_ref[...]
        dz = dy * _silu(g)
        dzb = dz.astype(BF16)
        dz_ref[...] = dzb
        dp_ref[...] = (dy * z_ref[...] * _dsilu(g)).astype(BF16)
        da = _dot(dzb, w_ref[...], NT)
        hcv = hc_ref[...]
        mu = jnp.mean(hcv, axis=-1, keepdims=True)
        xc = hcv - mu
        r = lax.rsqrt(jnp.mean(xc * xc, axis=-1, keepdims=True) + EPS)
        xhat = xc * r
        ln = xhat * lg_ref[...] + lb_ref[...]
        dln = da * _dsilu(ln)
        dxhat = dln * lg_ref[...]
        dhc = r * (dxhat - jnp.mean(dxhat, axis=-1, keepdims=True)
                   - xhat * jnp.mean(dxhat * xhat, axis=-1, keepdims=True))
        dh_ref[...] = dhc
        sums_ref[0:1, :] += jnp.sum(dz, axis=0, keepdims=True)
        sums_ref[1:2, :] += jnp.sum(dln * xhat, axis=0, keepdims=True)
        sums_ref[2:3, :] += jnp.sum(dln, axis=0, keepdims=True)
        sums_ref[3:4, :] += jnp.sum(dhc, axis=0, keepdims=True)

    tile = lambda col: pl.BlockSpec((tt, cw), lambda i: (i, col))
    vec = pl.BlockSpec((1, cw), lambda i: (0, 0))
    return pl.pallas_call(
        body, name="pointwise_bwd", grid=(t // tt,),
        out_shape=[jax.ShapeDtypeStruct((t, cw), BF16), jax.ShapeDtypeStruct(dproj.shape, dproj.dtype),
                   jax.ShapeDtypeStruct((t, cw), F32), jax.ShapeDtypeStruct((8, cw), F32)],
        in_specs=[tile(ycol), tile(0), tile(ncol - 1), pl.BlockSpec((cw, cw), lambda i: (0, 0)), tile(0), vec, vec,
                  pl.BlockSpec(memory_space=pl.ANY)],
        out_specs=[tile(0), tile(ncol - 1), tile(0), pl.BlockSpec((8, cw), lambda i: (0, 0))],
        input_output_aliases={7: 1},
        compiler_params=_params(("arbitrary",)),
    )(dycat, z, proj, wpw, hc, ln_g, ln_b, dproj)


def _conv_bwd(dhc, proj, w_dw, dproj, kc, cw):
    t = proj.shape[0]
    tt = _pick(t, 128, HALO)
    per = tt // HALO
    nt = t // tt
    chunk = _pick(cw, 256)
    ncol = proj.shape[1] // cw
    wr = w_dw.shape[0]

    def body(d_ref, dn_ref, u_ref, g_ref, up_ref, gp_ref, w_ref, dp_in, dp_ref, dw_ref, dbuf, hbuf, dw_acc):
        i = pl.program_id(0)

        @pl.when(i == 0)
        def _():
            dw_acc[...] = jnp.zeros_like(dw_acc)

        dbuf[0, pl.ds(0, tt), :] = d_ref[...]
        dbuf[0, pl.ds(tt, HALO), :] = jnp.where(i < nt - 1, dn_ref[...], 0.0)
        hbuf[0, pl.ds(HALO, tt), :] = _glu_rows(u_ref, g_ref)
        hbuf[0, pl.ds(0, HALO), :] = jnp.where(i > 0, _glu_rows(up_ref, gp_ref), 0.0)
        _fill_shifts(dbuf, tt + HALO - SUBLANES)
        _fill_shifts(hbuf, tt + HALO - SUBLANES)
        for lo in range(0, cw, chunk):
            dhg = _conv_taps(dbuf, w_ref, kc, lo, tt, chunk, lambda j: (kc - 1) - j)
            u = u_ref[:, lo:lo + chunk]
            sg = _sigmoid(g_ref[:, lo:lo + chunk])
            dp_ref[:, lo:lo + chunk] = (dhg * sg).astype(BF16)
            dp_ref[:, cw + lo:cw + lo + chunk] = (dhg * u * sg * (1.0 - sg)).astype(BF16)
            dtile = d_ref[:, lo:lo + chunk]
            for j in range(kc):
                prod = dtile * _shifted_rows(hbuf, HALO - (kc - 1) + j, tt, lo, chunk)
                dw_acc[j, :, lo:lo + chunk] += jnp.sum(prod.reshape(tt // SUBLANES, SUBLANES, chunk), axis=0)

        @pl.when(i == nt - 1)
        def _():
            dw_ref[...] = jnp.sum(dw_acc[...], axis=1)

    tile = lambda g: pl.BlockSpec((tt, cw), lambda i: (i, g))
    prev = lambda g: pl.BlockSpec((HALO, cw), lambda i: (jnp.maximum(i * per - 1, 0), g))
    return pl.pallas_call(
        body, name="conv_bwd", grid=(nt,),
        out_shape=[jax.ShapeDtypeStruct(dproj.shape, dproj.dtype), jax.ShapeDtypeStruct((wr, cw), F32)],
        in_specs=[pl.BlockSpec((tt, cw), lambda i: (i, 0)),
                  pl.BlockSpec((HALO, cw), lambda i: (jnp.minimum((i + 1) * per, nt * per - 1), 0)),
                  tile(ncol - 3), tile(ncol - 2), prev(ncol - 3), prev(ncol - 2),
                  pl.BlockSpec((wr, cw), lambda i: (0, 0)), pl.BlockSpec(memory_space=pl.ANY)],
        out_specs=[pl.BlockSpec((tt, 2 * cw), lambda i: (i, (ncol - 3) // 2)),
                   pl.BlockSpec((wr, cw), lambda i: (0, 0))],
        scratch_shapes=[pltpu.VMEM((SUBLANES, tt + HALO, cw), F32), pltpu.VMEM((SUBLANES, HALO + tt, cw), F32),
                        pltpu.VMEM((wr, SUBLANES, cw), F32)],
        input_output_aliases={7: 0},
        compiler_params=_params(("arbitrary",)),
    )(dhc, dhc, proj, proj, proj, proj, w_dw, dproj)


def _input_grad(dh, x, dout, norm_g, mod, rides=()):
    t, d = x.shape
    tt = _pick(t, 128, 16)

    def body(dh_ref, x_ref, do_ref, g_ref, mod_ref, gx_ref, sums_ref):
        i = pl.program_id(0)

        @pl.when(i == 0)
        def _():
            sums_ref[...] = jnp.zeros_like(sums_ref)

        xv = x_ref[...]
        dhv = dh_ref[...]
        r = lax.rsqrt(jnp.mean(xv * xv, axis=-1, keepdims=True) + EPS)
        xn = xv * r
        g = g_ref[...]
        one_scale = 1.0 + mod_ref[1:2, :]
        dxn = dhv * g * one_scale
        gx_ref[...] = do_ref[...] + r * (dxn - xn * jnp.mean(dxn * xn, axis=-1, keepdims=True))
        sums_ref[0:1, :] += jnp.sum(dhv, axis=0, keepdims=True)
        sums_ref[1:2, :] += jnp.sum(dhv * (xn * g), axis=0, keepdims=True)
        sums_ref[2:3, :] += jnp.sum(dhv * one_scale * xn, axis=0, keepdims=True)

    tile = pl.BlockSpec((tt, d), lambda i: (i, 0))
    return _call(
        body, name="input_grad", grid=(t // tt,),
        out_shape=[jax.ShapeDtypeStruct((t, d), F32), jax.ShapeDtypeStruct((8, d), F32)],
        in_specs=[tile, tile, tile, pl.BlockSpec((1, d), lambda i: (0, 0)), pl.BlockSpec((3, d), lambda i: (0, 0))],
        out_specs=[tile, pl.BlockSpec((8, d), lambda i: (0, 0))],
        args=(dh, x, dout, norm_g, mod), rides=rides)


def _sum_adam(parts, w, m, v, name):
    r, c = w.shape
    n_parts = parts.shape[0]
    tr = _pick(r, 128, 16) if r % 16 == 0 else r
    tc = _pick(c, 2048)

    def body(p_ref, w_ref, m_ref, v_ref, g_ref, d_ref, nm_ref, nv_ref):
        g = p_ref[0].astype(F32)
        for i in range(1, n_parts):
            g = g + p_ref[i].astype(F32)
        d, nm, nv = _adam(w_ref[...], g, m_ref[...], v_ref[...])
        g_ref[...] = g
        d_ref[...] = d
        nm_ref[...] = nm
        nv_ref[...] = nv

    tile = pl.BlockSpec((tr, tc), lambda i, j: (i, j))
    out = jax.ShapeDtypeStruct((r, c), F32)
    return pl.pallas_call(
        body, name=name, grid=(r // tr, c // tc),
        out_shape=[out] * 4,
        in_specs=[pl.BlockSpec((n_parts, tr, tc), lambda i, j: (0, i, j)), tile, tile, tile],
        out_specs=[tile] * 4,
        compiler_params=_params(("arbitrary", "arbitrary")),
    )(parts, w, m, v)


def _small_adam(parts, piece_rows, states):
    n_leaf = len(states)
    offsets = [sum(piece_rows[:i]) for i in range(len(piece_rows))]

    def total(p_ref, off, r):
        g = p_ref[0, off:off + r, :]
        for k in range(1, NDEV):
            g = g + p_ref[k, off:off + r, :]
        return g

    def body(p_ref, *refs):
        ins, outs = refs[:3 * n_leaf], refs[3 * n_leaf:]
        for i in range(n_leaf):
            w_ref, m_ref, v_ref = ins[3 * i:3 * i + 3]
            g = total(p_ref, offsets[i], w_ref.shape[0])
            d, nm, nv = _adam(w_ref[...], g, m_ref[...], v_ref[...])
            for o_ref, val in zip(outs[4 * i:4 * i + 4], (g, d, nm, nv)):
                o_ref[...] = val
        outs[4 * n_leaf][...] = total(p_ref, offsets[n_leaf], piece_rows[n_leaf])

    vmem = pl.BlockSpec(memory_space=pltpu.VMEM)
    flat = [a for leaf in states for a in leaf]
    out_shape = [jax.ShapeDtypeStruct(leaf[0].shape, F32) for leaf in states for _ in range(4)]
    out_shape.append(jax.ShapeDtypeStruct((piece_rows[n_leaf], LANES), F32))
    res = pl.pallas_call(
        body, name="small_adam", out_shape=out_shape,
        in_specs=[vmem] * (1 + len(flat)), out_specs=[vmem] * len(out_shape),
        compiler_params=pltpu.CompilerParams(vmem_limit_bytes=VMEM_LIMIT),
    )(parts, *flat)
    return [res[4 * i:4 * i + 4] for i in range(n_leaf)], res[4 * n_leaf]


def _ada_grad_adam(s_t, dm, w, m, v):
    d, n = w.shape
    tr = _pick(d, 256, 16)

    def body(s_ref, dm_ref, w_ref, m_ref, v_ref, g_ref, d_ref, nm_ref, nv_ref):
        g = lax.dot_general(s_ref[...], dm_ref[...], (NN, ((), ())), preferred_element_type=F32,
                            precision=lax.Precision.HIGHEST)
        dl, nm, nv = _adam(w_ref[...], g, m_ref[...], v_ref[...])
        g_ref[...] = g
        d_ref[...] = dl
        nm_ref[...] = nm
        nv_ref[...] = nv

    tile = pl.BlockSpec((tr, n), lambda i: (i, 0))
    out = jax.ShapeDtypeStruct((d, n), F32)
    return pl.pallas_call(
        body, name="ada_grad_adam", grid=(d // tr,),
        out_shape=[out] * 4,
        in_specs=[pl.BlockSpec((tr, NDEV), lambda i: (i, 0)), pl.BlockSpec((NDEV, n), lambda i: (0, 0)),
                  tile, tile, tile],
        out_specs=[tile] * 4,
        compiler_params=_params(("arbitrary",)),
    )(s_t, dm, w, m, v)


def _silu_t(c_all):
    n, d = c_all.shape

    def body(c_ref, o_ref):
        o_ref[...] = jnp.transpose(_silu(c_ref[...]))

    return pl.pallas_call(
        body, name="silu_t", out_shape=jax.ShapeDtypeStruct((d, n), F32),
        in_specs=[pl.BlockSpec(memory_space=pltpu.VMEM)], out_specs=pl.BlockSpec(memory_space=pltpu.VMEM),
        compiler_params=pltpu.CompilerParams(vmem_limit_bytes=VMEM_LIMIT),
    )(c_all)


def _rows128(v):
    return v.reshape(-1, LANES)


def _pad_rows(a, rows):
    return jnp.pad(a, ((0, rows - a.shape[0]), (0, 0)))


def kernel(x, c, norm_g, w_ada, b_ada, w_in, q_norm_g, k_norm_g, w_dw, b_dw, ln_g, ln_b, w_pw, b_pw, w_out, loss_target, m_norm_g, m_w_ada, m_b_ada, m_w_in, m_q_norm_g, m_k_norm_g, m_w_dw, m_b_dw, m_ln_g, m_ln_b, m_w_pw, m_b_pw, m_w_out, v_norm_g, v_w_ada, v_b_ada, v_w_in, v_q_norm_g, v_k_norm_g, v_w_dw, v_b_dw, v_ln_g, v_ln_b, v_w_pw, v_b_pw, v_w_out):
    _, t, d = x.shape
    n_ada = w_ada.shape[2]
    ns = w_in.shape[2]
    kc, cwl = w_dw.shape[1], w_dw.shape[2]
    cw = cwl * NDEV
    sb = d - cw
    nh = sb // HEAD_DIM
    assert sb == cw and kc - 1 <= HALO and NDEV * ns == 4 * sb + 3 * cw
    my = 4 * lax.axis_index("x") + 2 * lax.axis_index("y") + lax.axis_index("c")

    x2, tg2 = x[0], loss_target[0]

    wdw_rows = -(-kc // 8) * 8
    wdw_pad = _pad_rows(w_dw[0], wdw_rows)
    pay1 = jnp.concatenate([_rows128(c[0]), _rows128(wdw_pad.reshape(-1))], axis=0)
    (g1,) = _all_gather([pay1], "gather_cond", pltpu.VMEM)
    c_rows = d // LANES
    c_all = g1[:, :c_rows].reshape(NDEV, d)
    wdw_all = g1[:, c_rows:].reshape(NDEV, wdw_rows, cwl).transpose(1, 0, 2).reshape(wdw_rows, cw)

    b_ada_loc = lax.dynamic_slice(b_ada, (0, my * n_ada), (1, n_ada))
    mod_cols = _ada_matmul(c_all, w_ada[0], b_ada_loc)
    (g2,) = _all_gather([mod_cols], "gather_mod", pltpu.VMEM)
    mod_mine = lax.dynamic_index_in_dim(g2, my, axis=1, keepdims=False)
    mod = mod_mine.reshape(3, d)

    core = lax.axis_index("c").astype(jnp.int32).reshape(1)
    h = _modulated_norm(x2, norm_g, mod)
    wfull_in, proj = _gather_proj(h, w_in[0].astype(BF16))
    (o, tot, ycat, firsts), (partly,) = _attention_fwd(
        proj, q_norm_g, k_norm_g, nh, d, _gather_ride([w_out[0].astype(BF16), w_pw[0].astype(BF16)]))
    (a, hc), ((wg_out, wg_pw),) = _conv_fwd(proj, wdw_all, b_dw, ln_g, ln_b, kc, cw, _gather_finish_ride(partly))
    wfull_out, wfull_pw = wg_out.reshape(d, d), wg_pw.reshape(cw, cw)
    z, ycat = _pointwise_fwd(a, wfull_pw, b_pw, proj, ycat, cw)
    dout, dy, out_sums = _out_matmul(ycat, wfull_out, x2, tg2, mod)

    tile = _pick(t, 512, 16)
    (dycat,), _ = _mm(dy, wfull_out, NT, "dycat_matmul", tile, _pick(d, 512), F32)
    (p_wout,), _ = _mm(ycat, dy, TN, "w_out_grad", _pick(d, 512), _pick(d, 1024), BF16)
    p_wout = p_wout.reshape(NDEV, d // NDEV, d)
    dproj, gains = _attention_bwd(proj, o, tot, firsts, dycat, q_norm_g, k_norm_g, nh)
    dz, dproj, dhc, pw_sums = _pointwise_bwd(dycat, z, proj, dproj, wfull_pw, hc, ln_g, ln_b, cw)
    (p_wpw,), _ = _mm(a, dz, TN, "w_pw_grad", _pick(cw, 512), _pick(cw, 1024), BF16)
    p_wpw = p_wpw.reshape(NDEV, cw // NDEV, cw)
    dproj, dwdw = _conv_bwd(dhc, proj, wdw_all, dproj, kc, cw)
    p_wdw = dwdw.reshape(wdw_rows, NDEV, cwl).transpose(1, 0, 2).astype(BF16)

    def pair_sums(mine, theirs, name):
        return [_pair_sum(m, s, core, f"{name}_pair_sum_{i}") for i, (m, s) in enumerate(zip(mine, theirs))]

    lesser = [p.reshape(4, 2, *p.shape[1:]) for p in (p_wout, p_wpw, p_wdw)]
    (p_theirs,), (small_theirs,) = _mm(h, dproj, TN, "w_in_grad_sibling", _pick(d, 256), ns, BF16, slabs=True,
                                       n_outer=True, every_other=1 - core, rides=[_sibling_ride(lesser)])
    q_small = pair_sums(lesser, small_theirs, "small_grads")
    (p_mine,), (win_theirs, (r_wout, r_wpw, r_wdw)) = _mm(
        h, dproj, TN, "w_in_grad_own", _pick(d, 256), ns, BF16, slabs=True, n_outer=True, every_other=core,
        rides=[_sibling_ride([p_theirs]), _chip_exchange_ride(q_small)])
    q_win = pair_sums([p_mine[:, None]], win_theirs, "w_in_grad")
    head = d - d // 8
    (dh,), ((r_head,),) = _dh_matmul(dproj, wfull_in, tile, _pick(d, 512), 2,
                                     [_chip_exchange_ride(q_win, rows=(0, head))])
    (grad_x, in_sums), ((r_win,),) = _input_grad(
        dh, x2, dout, norm_g, mod, rides=[_chip_exchange_ride(q_win, rows=(head, d - head), into=[r_head])])

    dmod = jnp.concatenate([in_sums[0], in_sums[1], out_sums[0]])
    loss_part = 0.5 / d * jnp.sum(out_sums[1].reshape(-1, LANES), axis=0)
    small = [in_sums[2], dmod, jnp.sum(gains[:, 0], axis=0), jnp.sum(gains[:, 1], axis=0),
             pw_sums[3], pw_sums[1], pw_sums[2], pw_sums[0], loss_part]
    pieces = [_rows128(s) for s in small]
    pieces = [_pad_rows(p, -(-p.shape[0] // 8) * 8) for p in pieces]
    (g3,) = _all_gather([jnp.concatenate(pieces, axis=0)], "gather_small", pltpu.VMEM)
    states = [[_rows128(s.reshape(-1)) for s in leaf] for leaf in (
        (norm_g, m_norm_g, v_norm_g), (b_ada, m_b_ada, v_b_ada), (q_norm_g, m_q_norm_g, v_q_norm_g),
        (k_norm_g, m_k_norm_g, v_k_norm_g), (b_dw, m_b_dw, v_b_dw), (ln_g, m_ln_g, v_ln_g),
        (ln_b, m_ln_b, v_ln_b), (b_pw, m_b_pw, v_b_pw))]
    small_out, loss_rows = _small_adam(g3, [p.shape[0] for p in pieces], states)
    g_small, d_small, m_small, v_small = [[leaf[k].reshape(1, -1) for leaf in small_out] for k in range(4)]
    loss = jnp.sum(loss_rows[0])

    off = pieces[0].shape[0]
    dmod_all = g3[:, off:off + pieces[1].shape[0]].reshape(NDEV, 3 * d)
    dmod_loc = lax.dynamic_slice(dmod_all, (0, my * n_ada), (NDEV, n_ada))
    ada = _ada_grad_adam(_silu_t(c_all), dmod_loc, w_ada[0], m_w_ada[0], v_w_ada[0])
    win = _sum_adam(r_win, w_in[0], m_w_in[0], v_w_in[0], "w_in_adam")
    wout = _sum_adam(r_wout, w_out[0], m_w_out[0], v_w_out[0], "w_out_adam")
    wpw = _sum_adam(r_wpw, w_pw[0], m_w_pw[0], v_w_pw[0], "w_pw_adam")
    wdw_state = [_pad_rows(s[0], wdw_rows) for s in (w_dw, m_w_dw, v_w_dw)]
    wdw = [r[:kc] for r in _sum_adam(r_wdw, *wdw_state, "w_dw_adam")]

    def group(k, small_list):
        s = small_list
        return [s[0], ada[k][None], s[1], win[k][None], s[2], s[3], wdw[k][None], s[4], s[5], s[6],
                wpw[k][None], s[7], wout[k][None]]

    return (loss, grad_x[None], *group(0, g_small), *group(1, d_small), *group(2, m_small), *group(3, v_small))
```

```python
import functools

import jax
import jax.numpy as jnp
from jax import lax
from jax.experimental import pallas as pl
from jax.experimental.pallas import tpu as pltpu

F32 = jnp.float32
BF16 = jnp.bfloat16
NDEV = 8
HEAD_DIM = 128
LANES = 128
SUBLANES = 8
HALO = 32
EPS = 1e-6
DEAD_LOG_WEIGHT = -104.0
VMEM_LIMIT = 56 * 1024 * 1024
MESH = pl.DeviceIdType.MESH

ADAM_LR = 0.001
ADAM_B1 = 0.9
ADAM_B2 = 0.999
ADAM_EPS = 1e-08
ADAM_WD = 0.01
ADAM_STEP = 10


def _params(sem=None):
    return pltpu.CompilerParams(dimension_semantics=sem, vmem_limit_bytes=VMEM_LIMIT)


def _pick(n, pref, unit=LANES):
    best = None
    for d in range(unit, min(n, pref) + 1, unit):
        if n % d == 0:
            best = d
    return best if best is not None else n


def _sigmoid(z):
    return 1.0 / (1.0 + jnp.exp(-z))


def _silu(z):
    return z * _sigmoid(z)


def _dsilu(z):
    s = _sigmoid(z)
    return s * (1.0 + z * (1.0 - s))


def _softplus(z):
    return jnp.maximum(z, 0.0) + jnp.log(1.0 + jnp.exp(-jnp.abs(z)))


def _dot(a, b, dims):
    return lax.dot_general(a, b, (dims, ((), ())), preferred_element_type=F32)


NN = ((1,), (0,))
NT = ((1,), (1,))
TN = ((0,), (0,))


def _adam(w, g, m, v):
    m = ADAM_B1 * m + (1.0 - ADAM_B1) * g
    v = ADAM_B2 * v + (1.0 - ADAM_B2) * (g * g)
    m_hat = m / (1.0 - ADAM_B1 ** ADAM_STEP)
    v_hat = v / (1.0 - ADAM_B2 ** ADAM_STEP)
    delta = -ADAM_LR * (m_hat / (jnp.sqrt(v_hat) + ADAM_EPS) + ADAM_WD * w)
    return delta, m, v


def _place():
    x, y, c = lax.axis_index("x"), lax.axis_index("y"), lax.axis_index("c")
    return x, y, c


def _flip(v, bit):
    return 1 - v if bit else v


def _all_gather(arrs, name, space):
    n = len(arrs)

    def body(*refs):
        ins, outs = refs[:n], refs[n:2 * n]
        send_sems, recv_sems, local_sems = refs[2 * n:]
        x, y, c = _place()
        me, sibling = (x, y, c), (x, y, 1 - c)
        chips = [(1 - x, y), (x, 1 - y), (1 - x, 1 - y)]

        def rows(a, p):
            return outs[a].at[4 * p[0] + 2 * p[1] + p[2]]

        def copy(a, k, block, to, src=None):
            return pltpu.make_async_remote_copy(
                src_ref=rows(a, block) if src is None else src, dst_ref=rows(a, block),
                send_sem=send_sems.at[7 * a + k], recv_sem=recv_sems.at[7 * a + k],
                device_id=to, device_id_type=MESH)

        mine = [pltpu.make_async_copy(ins[a], rows(a, me), local_sems.at[a]) for a in range(n)]
        for cp in mine:
            cp.start()
        first = []
        for a in range(n):
            first.append(copy(a, 0, me, sibling, src=ins[a]))
            first += [copy(a, 1 + j, me, (*chip, c), src=ins[a]) for j, chip in enumerate(chips)]
        for cp in first:
            cp.start()
        passed = []
        for j, chip in enumerate(chips):
            for a in range(n):
                copy(a, 1 + j, (*chip, c), me).wait_recv()
                fwd = copy(a, 4 + j, (*chip, c), sibling)
                fwd.start()
                passed.append(fwd)
        for a in range(n):
            copy(a, 0, sibling, me).wait_recv()
            for j, chip in enumerate(chips):
                copy(a, 4 + j, (*chip, 1 - c), me).wait_recv()
        for cp in first + passed:
            cp.wait_send()
        for cp in mine:
            cp.wait()

    spec = pl.BlockSpec(memory_space=space)
    return pl.pallas_call(
        body, name=name,
        out_shape=[jax.ShapeDtypeStruct((NDEV,) + a.shape, a.dtype) for a in arrs],
        in_specs=[spec] * n, out_specs=[spec] * n,
        scratch_shapes=[pltpu.SemaphoreType.DMA((7 * n,)), pltpu.SemaphoreType.DMA((7 * n,)),
                        pltpu.SemaphoreType.DMA((n,))],
        compiler_params=pltpu.CompilerParams(vmem_limit_bytes=VMEM_LIMIT),
    )(*arrs)


class _Ride:
    def __init__(self, ins, out_shapes, n_sems, start, finish, alias=None):
        self.ins, self.out_shapes, self.n_sems, self.start, self.finish = ins, out_shapes, n_sems, start, finish
        self.alias = alias or {}


def _call(body, *, name, grid, out_shape, in_specs, out_specs, args, scratch_shapes=(), rides=(), prefetch=None):
    sem = ("arbitrary",) * len(grid)
    rides = [r for r in rides if r is not None]
    n_pre = 0 if prefetch is None else 1
    n_in, n_out, n_scr = len(in_specs), len(out_specs), len(scratch_shapes)
    r_ins = [len(r.ins) for r in rides]
    r_outs = [len(r.out_shapes) for r in rides]

    def carried(*refs):
        pre, refs = refs[:n_pre], refs[n_pre:]
        ins, pos = refs[:n_in], n_in
        rins = []
        for k in r_ins:
            rins.append(refs[pos:pos + k])
            pos += k
        outs, pos = refs[pos:pos + n_out], pos + n_out
        routs = []
        for k in r_outs:
            routs.append(refs[pos:pos + k])
            pos += k
        scratch, pos = refs[pos:pos + n_scr], pos + n_scr
        sems = [refs[pos + 3 * i:pos + 3 * i + 3] for i in range(len(rides))]
        first = functools.reduce(lambda a, b: a & b, [pl.program_id(i) == 0 for i in range(len(grid))])
        last = functools.reduce(lambda a, b: a & b, [pl.program_id(i) == grid[i] - 1 for i in range(len(grid))])

        @pl.when(first)
        def _():
            for ride, ri, ro, s in zip(rides, rins, routs, sems):
                ride.start(ri, ro, *s)

        body(*pre, *ins, *outs, *scratch)

        @pl.when(last)
        def _():
            for ride, ri, ro, s in zip(rides, rins, routs, sems):
                ride.finish(ri, ro, *s)

    hbm = pl.BlockSpec(memory_space=pl.ANY)
    aliases, in_pos, out_pos = {}, n_pre + n_in, n_out
    for ride, ki, ko in zip(rides, r_ins, r_outs):
        aliases.update({in_pos + k: out_pos + o for k, o in ride.alias.items()})
        in_pos, out_pos = in_pos + ki, out_pos + ko
    all_scratch = list(scratch_shapes)
    for ride in rides:
        all_scratch += [pltpu.SemaphoreType.DMA((ride.n_sems,))] * 3
    all_in = list(in_specs) + [hbm] * sum(r_ins)
    all_out = list(out_specs) + [hbm] * sum(r_outs)
    shapes = list(out_shape) + [s for r in rides for s in r.out_shapes]
    operands = list(args) + [a for r in rides for a in r.ins]
    if prefetch is None:
        res = pl.pallas_call(
            carried, name=name, grid=grid, out_shape=shapes, in_specs=all_in, out_specs=all_out,
            scratch_shapes=all_scratch, input_output_aliases=aliases, compiler_params=_params(sem))(*operands)
    else:
        res = pl.pallas_call(
            carried, name=name, out_shape=shapes,
            grid_spec=pltpu.PrefetchScalarGridSpec(num_scalar_prefetch=1, grid=grid, in_specs=all_in,
                                                   out_specs=all_out, scratch_shapes=all_scratch),
            input_output_aliases=aliases, compiler_params=_params(sem))(prefetch, *operands)
    split, pos = [], n_out
    for k in r_outs:
        split.append(res[pos:pos + k])
        pos += k
    return res[:n_out], split


def _chips(x, y):
    return [(1 - x, y), (x, 1 - y), (1 - x, 1 - y)]


def _gather_ride(arrs):
    n = len(arrs)

    def copies(ins, outs, send_sems, recv_sems):
        x, y, c = _place()
        me = 4 * x + 2 * y + c
        peers = [(x, y, 1 - c)] + [(*chip, c) for chip in _chips(x, y)]
        return [pltpu.make_async_remote_copy(
            src_ref=ins[a], dst_ref=outs[a].at[me], send_sem=send_sems.at[4 * a + k], recv_sem=recv_sems.at[4 * a + k],
            device_id=p, device_id_type=MESH) for a in range(n) for k, p in enumerate(peers)], me

    def start(ins, outs, send_sems, recv_sems, local_sems):
        cps, me = copies(ins, outs, send_sems, recv_sems)
        for a in range(n):
            pltpu.make_async_copy(ins[a], outs[a].at[me], local_sems.at[a]).start()
        for cp in cps:
            cp.start()

    def finish(ins, outs, send_sems, recv_sems, local_sems):
        cps, me = copies(ins, outs, send_sems, recv_sems)
        for cp in cps:
            cp.wait_recv()
        for cp in cps:
            cp.wait_send()
        for a in range(n):
            pltpu.make_async_copy(ins[a], outs[a].at[me], local_sems.at[a]).wait()

    return _Ride(arrs, [jax.ShapeDtypeStruct((NDEV,) + a.shape, a.dtype) for a in arrs], 4 * n, start, finish)


def _gather_finish_ride(arrs):
    n = len(arrs)

    def copies(outs, send_sems, recv_sems):
        x, y, c = _place()
        cps = []
        for a in range(n):
            for k, chip in enumerate(_chips(x, y)):
                blk = 4 * chip[0] + 2 * chip[1]
                cps.append((pltpu.make_async_remote_copy(
                    src_ref=outs[a].at[blk + c], dst_ref=outs[a].at[blk + c],
                    send_sem=send_sems.at[3 * a + k], recv_sem=recv_sems.at[3 * a + k],
                    device_id=(x, y, 1 - c), device_id_type=MESH),
                    pltpu.make_async_remote_copy(
                    src_ref=outs[a].at[blk + 1 - c], dst_ref=outs[a].at[blk + 1 - c],
                    send_sem=send_sems.at[3 * a + k], recv_sem=recv_sems.at[3 * a + k],
                    device_id=(x, y, 1 - c), device_id_type=MESH)))
        return cps

    def start(ins, outs, send_sems, recv_sems, local_sems):
        for send, _ in copies(outs, send_sems, recv_sems):
            send.start()

    def finish(ins, outs, send_sems, recv_sems, local_sems):
        cps = copies(outs, send_sems, recv_sems)
        for _, recv in cps:
            recv.wait_recv()
        for send, _ in cps:
            send.wait_send()

    return _Ride(arrs, [jax.ShapeDtypeStruct(a.shape, a.dtype) for a in arrs], 3 * n, start, finish,
                 alias={a: a for a in range(n)})


def _sibling_ride(arrs):
    n = len(arrs)

    def copies(ins, outs, send_sems, recv_sems):
        x, y, c = _place()
        return [pltpu.make_async_remote_copy(
            src_ref=ins[a].at[:, 1 - c] if arrs[a].ndim == 4 else ins[a], dst_ref=outs[a],
            send_sem=send_sems.at[a], recv_sem=recv_sems.at[a],
            device_id=(x, y, 1 - c), device_id_type=MESH) for a in range(n)]

    def start(ins, outs, send_sems, recv_sems, local_sems):
        for cp in copies(ins, outs, send_sems, recv_sems):
            cp.start()

    def finish(ins, outs, send_sems, recv_sems, local_sems):
        for cp in copies(ins, outs, send_sems, recv_sems):
            cp.wait()

    return _Ride(arrs, [jax.ShapeDtypeStruct((4,) + a.shape[-2:], a.dtype) for a in arrs], n, start, finish)


def _pair_sum(mine, theirs, core, name):
    _, pick, r, c = mine.shape
    if pick == 1:
        core = jnp.zeros_like(core)
    tr = _pick(r, 512, 16)

    def body(core_ref, a_ref, b_ref, o_ref):
        o_ref[...] = (a_ref[...].astype(F32) + b_ref[...].astype(F32)).astype(BF16)

    return pl.pallas_call(
        body, name=name,
        grid_spec=pltpu.PrefetchScalarGridSpec(
            num_scalar_prefetch=1, grid=(4, r // tr),
            in_specs=[pl.BlockSpec((None, None, tr, c), lambda i, k, core_ref: (i, core_ref[0], k, 0)),
                      pl.BlockSpec((None, tr, c), lambda i, k, core_ref: (i, k, 0))],
            out_specs=pl.BlockSpec((None, tr, c), lambda i, k, core_ref: (i, k, 0))),
        out_shape=jax.ShapeDtypeStruct((4, r, c), BF16),
        compiler_params=_params(("arbitrary", "arbitrary")),
    )(core, mine, theirs)


def _chip_exchange_ride(arrs, rows=None, into=None):
    n = len(arrs)

    def part(ref):
        return ref if rows is None else ref.at[pl.ds(rows[0], rows[1]), :]

    def copies(ins, outs, send_sems, recv_sems):
        x, y, c = _place()
        mine = 2 * x + y
        return [pltpu.make_async_remote_copy(
            src_ref=part(ins[a].at[2 * chip[0] + chip[1]]), dst_ref=part(outs[a].at[mine]),
            send_sem=send_sems.at[3 * a + k], recv_sem=recv_sems.at[3 * a + k],
            device_id=(*chip, c), device_id_type=MESH) for a in range(n) for k, chip in enumerate(_chips(x, y))], mine

    def start(ins, outs, send_sems, recv_sems, local_sems):
        cps, mine = copies(ins, outs, send_sems, recv_sems)
        if into is None:
            for a in range(n):
                pltpu.make_async_copy(ins[a].at[mine], outs[a].at[mine], local_sems.at[a]).start()
        for cp in cps:
            cp.start()

    def finish(ins, outs, send_sems, recv_sems, local_sems):
        cps, mine = copies(ins, outs, send_sems, recv_sems)
        for cp in cps:
            cp.wait_recv()
        for cp in cps:
            cp.wait_send()
        if into is None:
            for a in range(n):
                pltpu.make_async_copy(ins[a].at[mine], outs[a].at[mine], local_sems.at[a]).wait()

    shapes = [jax.ShapeDtypeStruct(a.shape, a.dtype) for a in arrs]
    if into is None:
        return _Ride(arrs, shapes, 3 * n, start, finish)
    return _Ride(list(arrs) + list(into), shapes, 3 * n, start, finish, alias={n + a: a for a in range(n)})


def _ada_matmul(c_all, w_loc, b_loc):
    d, n = w_loc.shape
    bn = _pick(n, 512)

    def body(c_ref, w_ref, b_ref, o_ref):
        s = _silu(c_ref[...]).astype(BF16)
        o_ref[...] = _dot(s, w_ref[...].astype(BF16), NN) + b_ref[...]

    return pl.pallas_call(
        body, name="ada_matmul", grid=(n // bn,),
        out_shape=jax.ShapeDtypeStruct((NDEV, n), F32),
        in_specs=[pl.BlockSpec((NDEV, d), lambda j: (0, 0)), pl.BlockSpec((d, bn), lambda j: (0, j)),
                  pl.BlockSpec((1, bn), lambda j: (0, j))],
        out_specs=pl.BlockSpec((NDEV, bn), lambda j: (0, j)),
        compiler_params=_params(("arbitrary",)),
    )(c_all, w_loc, b_loc)


def _modulated_norm(x, norm_g, mod):
    t, d = x.shape
    tt = _pick(t, 256, 16)

    def body(x_ref, g_ref, mod_ref, h_ref):
        xv = x_ref[...]
        r = lax.rsqrt(jnp.mean(xv * xv, axis=-1, keepdims=True) + EPS)
        h = (xv * r) * g_ref[...] * (1.0 + mod_ref[1:2, :]) + mod_ref[0:1, :]
        h_ref[...] = h.astype(BF16)

    return pl.pallas_call(
        body, name="modulated_norm", grid=(t // tt,),
        out_shape=jax.ShapeDtypeStruct((t, d), BF16),
        in_specs=[pl.BlockSpec((tt, d), lambda i: (i, 0)), pl.BlockSpec((1, d), lambda i: (0, 0)),
                  pl.BlockSpec((3, d), lambda i: (0, 0))],
        out_specs=pl.BlockSpec((tt, d), lambda i: (i, 0)),
        compiler_params=_params(("arbitrary",)),
    )(x, norm_g, mod)


def _gather_proj(h, w_loc):
    t, d = h.shape
    ns = w_loc.shape[1]
    tm = _pick(t, 512, 16)
    nm = t // tm
    idx = lambda p: 4 * p[0] + 2 * p[1] + p[2]

    def peers():
        x, y, c = _place()
        flip = lambda a, b: a + b - 2 * a * b
        near, far = (flip(x, c), flip(y, 1 - c)), (flip(x, 1 - c), flip(y, c))
        return (x, y, c), (x, y, 1 - c), near, far, (1 - x, 1 - y), c

    me, sibling, near, far, diag, c = peers()
    order = [me, sibling, (*near, c), (*far, 1 - c), (*far, c), (*near, 1 - c), (*diag, c), (*diag, 1 - c)]
    order = jnp.stack([idx(p) for p in order]).astype(jnp.int32)

    def body(order_ref, a_ref, w_ref, wg_ref, o_ref, slab, send_sems, recv_sems, local_sems):
        j, m = pl.program_id(0), pl.program_id(1)
        me, sibling, near, far, diag, c = peers()

        def rows(p):
            return wg_ref.at[:, pl.ds(pl.multiple_of(idx(p) * ns, LANES), ns)]

        def copy(k, block, to, src=None):
            return pltpu.make_async_remote_copy(
                src_ref=rows(block) if src is None else src, dst_ref=rows(block),
                send_sem=send_sems.at[k], recv_sem=recv_sems.at[k], device_id=to, device_id_type=MESH)

        def load(src):
            cp = pltpu.make_async_copy(src, slab, local_sems.at[1])
            cp.start()
            cp.wait()

        keep = pltpu.make_async_copy(w_ref, rows(me), local_sems.at[0])
        own = [copy(0, me, sibling, src=w_ref), copy(1, me, (*near, c), src=w_ref), copy(2, me, (*far, c), src=w_ref)]
        relay = copy(3, (*near, c), (*far, c))
        passed = [copy(4, (*near, c), sibling), copy(5, (*far, c), sibling), copy(6, (*diag, c), sibling)]
        arrivals = [(1, 0, sibling, []), (2, 1, (*near, c), [passed[0], own[2], relay]), (3, 4, (*far, 1 - c), []),
                    (4, 2, (*far, c), [passed[1]]), (5, 5, (*near, 1 - c), []),
                    (6, 3, (*diag, c), [passed[2]]), (7, 6, (*diag, 1 - c), [])]

        @pl.when((j == 0) & (m == 0))
        def _():
            keep.start()
            for cp in own[:2]:
                cp.start()
            load(w_ref)

        for step, sem, block, onward in arrivals:
            @pl.when((j == step) & (m == 0))
            def _(sem=sem, block=block, onward=onward):
                copy(sem, block, me).wait_recv()
                for cp in onward:
                    cp.start()
                load(rows(block))

        o_ref[...] = _dot(a_ref[...], slab[...], NN)

        @pl.when((j == NDEV - 1) & (m == nm - 1))
        def _():
            for cp in own + [relay] + passed:
                cp.wait_send()
            keep.wait()

    hbm = pl.BlockSpec(memory_space=pl.ANY)
    return pl.pallas_call(
        body, name="gather_proj",
        grid_spec=pltpu.PrefetchScalarGridSpec(
            num_scalar_prefetch=1, grid=(NDEV, nm),
            in_specs=[pl.BlockSpec((tm, d), lambda j, m, order_ref: (m, 0)), hbm],
            out_specs=[hbm, pl.BlockSpec((tm, ns), lambda j, m, order_ref: (m, order_ref[j]))],
            scratch_shapes=[pltpu.VMEM((d, ns), BF16), pltpu.SemaphoreType.DMA((7,)), pltpu.SemaphoreType.DMA((7,)),
                            pltpu.SemaphoreType.DMA((2,))]),
        out_shape=[jax.ShapeDtypeStruct((d, NDEV * ns), BF16), jax.ShapeDtypeStruct((t, NDEV * ns), F32)],
        compiler_params=_params(("arbitrary", "arbitrary")),
    )(order, h, w_loc)


def _attention_fwd(proj, qg, kg, nh, d_model, ride):
    t = proj.shape[0]
    tq = _pick(t, 256, 16)
    nq = t // tq
    assert 2 <= nq <= LANES
    group = 2 if nq % 2 == 0 else 1
    scale = HEAD_DIM ** -0.5

    def body(q_ref, k_ref, v_ref, g_ref, qg_ref, kg_ref, o_ref, tot_ref, y_ref, first_ref, qn, kn, vb):
        def norm(src, gain, dst):
            v = src[...]
            r = lax.rsqrt(jnp.mean(v * v, axis=-1, keepdims=True) + EPS)
            dst[...] = ((v * r) * gain[...]).astype(BF16)

        norm(q_ref, qg_ref, qn)
        norm(k_ref, kg_ref, kn)
        vb[...] = v_ref[...].astype(BF16)
        def after_matrix(n):
            return (lax.broadcasted_iota(jnp.int32, (n, n), 0) > lax.broadcasted_iota(jnp.int32, (n, n), 1)).astype(BF16)

        upper = {tq: after_matrix(tq), 2 * tq: after_matrix(2 * tq)}
        key_minus_query = (lax.broadcasted_iota(jnp.int32, (tq, 2 * tq), 1)
                           - lax.broadcasted_iota(jnp.int32, (tq, 2 * tq), 0))

        def block(qi, start, width, carry, acc, q_start=None):
            ks = pl.ds(pl.multiple_of(start, tq), width)
            z = _dot(qi, kn[ks, :], NT) * scale
            sp = _softplus(z)
            ls = -sp
            if q_start is not None:
                causal = key_minus_query < q_start - start
                ls = jnp.where(causal, ls, 0.0)
            hi = ls.astype(BF16)
            lo = (ls - hi.astype(F32)).astype(BF16)
            after = _dot(hi, upper[width], NN) + _dot(lo, upper[width], NN)
            w = jnp.exp(z - sp + after + carry)
            if q_start is not None:
                w = jnp.where(causal, w, 0.0)
            acc = acc + _dot(w.astype(BF16), vb[ks, :], NN)
            carry = carry + jnp.sum(ls, axis=1, keepdims=True)
            return carry, acc

        lane = lax.broadcasted_iota(jnp.int32, (8, LANES), 1)

        def live(carry):
            return (jnp.max(carry) > DEAD_LOG_WEIGHT).astype(jnp.int32)

        def wide_step(i):
            qi = qn[pl.ds(pl.multiple_of(i * tq, tq), tq), :]
            left = jnp.maximum(i - 1, 0)
            return block(qi, left * tq, 2 * tq, jnp.zeros((tq, 1), F32), jnp.zeros((tq, HEAD_DIM), F32), i * tq)

        def finish(i, carry, acc, firsts):
            qs = pl.ds(pl.multiple_of(i * tq, tq), tq)
            qi = qn[qs, :]
            left = jnp.maximum(i - 1, 0)

            def k_step(st):
                ca, ac = block(qi, (left - 1 - st[0]) * tq, tq, st[1], st[2])
                return st[0] + 1, ca, ac, live(ca)

            done, carry, acc, _ = lax.while_loop(
                lambda st: (st[0] < left) & (st[3] > 0), k_step, (jnp.int32(0), carry, acc, live(carry)))
            o_ref[qs, :] = acc
            tot_ref[qs, :] = jnp.broadcast_to(carry, (tq, HEAD_DIM))
            y_ref[qs, :] = (acc * _silu(g_ref[qs, :])).astype(BF16)
            return jnp.where(lane == i, (left - done).astype(F32), firsts)

        def q_group(p, firsts):
            blocks = [p + b * (nq // group) for b in range(group)]
            swept = [wide_step(i) for i in blocks]
            for i, (carry, acc) in zip(blocks, swept):
                firsts = finish(i, carry, acc, firsts)
            return firsts

        first_ref[...] = lax.fori_loop(0, nq // group, q_group, jnp.zeros((8, LANES), F32))

    col_block = lambda off: pl.BlockSpec((t, HEAD_DIM), lambda h: (0, off + h))
    vec = pl.BlockSpec((1, HEAD_DIM), lambda h: (0, 0))
    return _call(
        body, name="attention_fwd", grid=(nh,),
        out_shape=[jax.ShapeDtypeStruct((t, nh * HEAD_DIM), F32), jax.ShapeDtypeStruct((t, nh * HEAD_DIM), F32),
                   jax.ShapeDtypeStruct((t, d_model), BF16), jax.ShapeDtypeStruct((nh, 8, LANES), F32)],
        in_specs=[col_block(0), col_block(nh), col_block(2 * nh), col_block(3 * nh), vec, vec],
        out_specs=[col_block(0), col_block(0), col_block(0), pl.BlockSpec((None, 8, LANES), lambda h: (h, 0, 0))],
        scratch_shapes=[pltpu.VMEM((t, HEAD_DIM), BF16)] * 3,
        args=(proj, proj, proj, proj, qg, kg), rides=[ride])


def _fill_shifts(shifted, n_rows):
    for b in range(1, SUBLANES):
        shifted[b, pl.ds(0, n_rows), :] = shifted[0, pl.ds(b, n_rows), :]


def _shifted_rows(shifted, offset, n_rows, lo, cw):
    a, b = divmod(offset, SUBLANES)
    return shifted[b, pl.ds(SUBLANES * a, n_rows), pl.ds(lo, cw)]


def _conv_taps(shifted, w_ref, kc, lo, n_rows, cw, first_tap_row):
    acc = None
    for j in range(kc):
        term = w_ref[j:j + 1, lo:lo + cw] * _shifted_rows(shifted, first_tap_row(j), n_rows, lo, cw)
        acc = term if acc is None else acc + term
    return acc


def _glu_rows(u_ref, g_ref):
    return u_ref[...] * _sigmoid(g_ref[...])


def _conv_fwd(proj, w_dw, b_dw, ln_g, ln_b, kc, cw, ride):
    t = proj.shape[0]
    tt = _pick(t, 128, HALO)
    per = tt // HALO
    chunk = _pick(cw, 256)

    def body(u_ref, g_ref, up_ref, gp_ref, w_ref, b_ref, lg_ref, lb_ref, a_ref, hc_ref, buf):
        i = pl.program_id(0)
        buf[0, pl.ds(HALO, tt), :] = _glu_rows(u_ref, g_ref)
        halo = _glu_rows(up_ref, gp_ref)
        buf[0, pl.ds(0, HALO), :] = jnp.where(i > 0, halo, 0.0)
        _fill_shifts(buf, tt + HALO - SUBLANES)
        for lo in range(0, cw, chunk):
            conv = _conv_taps(buf, w_ref, kc, lo, tt, chunk, lambda j: HALO - (kc - 1) + j)
            hc_ref[:, lo:lo + chunk] = conv + b_ref[:, lo:lo + chunk]
        hc = hc_ref[...]
        mu = jnp.mean(hc, axis=-1, keepdims=True)
        xc = hc - mu
        var = jnp.mean(xc * xc, axis=-1, keepdims=True)
        ln = xc * lax.rsqrt(var + EPS) * lg_ref[...] + lb_ref[...]
        a_ref[...] = _silu(ln).astype(BF16)

    ncol = proj.shape[1] // cw
    tile = lambda g: pl.BlockSpec((tt, cw), lambda i: (i, g))
    prev = lambda g: pl.BlockSpec((HALO, cw), lambda i: (jnp.maximum(i * per - 1, 0), g))
    full = lambda r: pl.BlockSpec((r, cw), lambda i: (0, 0))
    return _call(
        body, name="conv_fwd", grid=(t // tt,),
        out_shape=[jax.ShapeDtypeStruct((t, cw), BF16), jax.ShapeDtypeStruct((t, cw), F32)],
        in_specs=[tile(ncol - 3), tile(ncol - 2), prev(ncol - 3), prev(ncol - 2),
                  full(w_dw.shape[0]), full(1), full(1), full(1)],
        out_specs=[pl.BlockSpec((tt, cw), lambda i: (i, 0))] * 2,
        scratch_shapes=[pltpu.VMEM((SUBLANES, HALO + tt, cw), F32)],
        args=(proj, proj, proj, proj, w_dw, b_dw, ln_g, ln_b), rides=[ride])


def _pointwise_fwd(a, wpw, b_pw, proj, ycat, cw):
    t = a.shape[0]
    tm = _pick(t, 256, 16)
    ncol = proj.shape[1] // cw
    ycol = ycat.shape[1] // cw - 1

    def body(a_ref, w_ref, b_ref, g_ref, y_in, z_ref, y_ref):
        z = _dot(a_ref[...], w_ref[...], NN) + b_ref[...]
        z_ref[...] = z
        y_ref[...] = (z * _silu(g_ref[...])).astype(BF16)

    return pl.pallas_call(
        body, name="pointwise_fwd", grid=(t // tm,),
        out_shape=[jax.ShapeDtypeStruct((t, cw), F32), jax.ShapeDtypeStruct(ycat.shape, ycat.dtype)],
        in_specs=[pl.BlockSpec((tm, cw), lambda m: (m, 0)),
                  pl.BlockSpec((cw, cw), lambda m: (0, 0)),
                  pl.BlockSpec((1, cw), lambda m: (0, 0)),
                  pl.BlockSpec((tm, cw), lambda m: (m, ncol - 1)),
                  pl.BlockSpec(memory_space=pl.ANY)],
        out_specs=[pl.BlockSpec((tm, cw), lambda m: (m, 0)), pl.BlockSpec((tm, cw), lambda m: (m, ycol))],
        input_output_aliases={4: 1},
        compiler_params=_params(("arbitrary",)),
    )(a, wpw, b_pw, proj, ycat)


def _out_matmul(ycat, wout, x, target, mod):
    t, d = x.shape
    kdim = wout.shape[0]
    tm, tn = _pick(t, 512, 16), _pick(d, 1024)
    inv_d = 1.0 / d

    def body(a_ref, w_ref, x_ref, tg_ref, mod_ref, dout_ref, dy_ref, sums_ref):
        @pl.when(pl.program_id(1) == 0)
        def _():
            sums_ref[...] = jnp.zeros_like(sums_ref)

        y = _dot(a_ref[...], w_ref[...], NN)
        gate = mod_ref[2:3, :]
        err = (x_ref[...] + gate * y) - tg_ref[...]
        dout = err * inv_d
        dout_ref[...] = dout
        dy_ref[...] = (dout * gate).astype(BF16)
        sums_ref[0:1, :] += jnp.sum(dout * y, axis=0, keepdims=True)
        sums_ref[1:2, :] += jnp.sum(err * err, axis=0, keepdims=True)

    mn = lambda n, m: (m, n)
    return pl.pallas_call(
        body, name="out_matmul", grid=(d // tn, t // tm),
        out_shape=[jax.ShapeDtypeStruct((t, d), F32), jax.ShapeDtypeStruct((t, d), BF16),
                   jax.ShapeDtypeStruct((8, d), F32)],
        in_specs=[pl.BlockSpec((tm, kdim), lambda n, m: (m, 0)),
                  pl.BlockSpec((kdim, tn), lambda n, m: (0, n)),
                  pl.BlockSpec((tm, tn), mn), pl.BlockSpec((tm, tn), mn),
                  pl.BlockSpec((3, tn), lambda n, m: (0, n))],
        out_specs=[pl.BlockSpec((tm, tn), mn), pl.BlockSpec((tm, tn), mn),
                   pl.BlockSpec((8, tn), lambda n, m: (0, n))],
        compiler_params=_params(("arbitrary", "arbitrary")),
    )(ycat, wout, x, target, mod)


def _mm(a, b, form, name, tm, tn, out_dtype, *, slabs=False, n_outer=False, ksplit=1, every_other=None, rides=()):
    if form == TN:
        kdim, m_dim = a.shape
    else:
        m_dim, kdim = a.shape
    n_dim = (b.shape[0] if form == NT else b.shape[1]) // (1 if every_other is None else 2)
    tk = kdim // ksplit
    gm, gn = m_dim // tm, n_dim // tn
    mn = (lambda g: (g[1], g[0])) if n_outer else (lambda g: (g[0], g[1]))
    b_col = (lambda g: mn(g)[1]) if every_other is None else (lambda g: 2 * mn(g)[1] + g[3][0])
    a_map = (lambda *g: (g[2], mn(g)[0])) if form == TN else (lambda *g: (mn(g)[0], g[2]))
    b_map = (lambda *g: (b_col(g), g[2])) if form == NT else (lambda *g: (g[2], b_col(g)))
    a_blk = (tk, tm) if form == TN else (tm, tk)
    b_blk = (tn, tk) if form == NT else (tk, tn)
    if slabs:
        out_shape = jax.ShapeDtypeStruct((gn, m_dim, tn), out_dtype)
        out_spec = pl.BlockSpec((None, tm, tn), lambda *g: (mn(g)[1], mn(g)[0], 0))
    else:
        out_shape = jax.ShapeDtypeStruct((m_dim, n_dim), out_dtype)
        out_spec = pl.BlockSpec((tm, tn), lambda *g: mn(g))

    def body(*refs):
        a_ref, b_ref, o_ref, *acc = refs if every_other is None else refs[1:]
        part = _dot(a_ref[...], b_ref[...], form)
        if ksplit == 1:
            o_ref[...] = part.astype(out_dtype)
            return
        k = pl.program_id(2)

        @pl.when(k == 0)
        def _():
            acc[0][...] = part

        @pl.when((k > 0) & (k < ksplit - 1))
        def _():
            acc[0][...] += part

        @pl.when(k == ksplit - 1)
        def _():
            o_ref[...] = (acc[0][...] + part).astype(out_dtype)

    return _call(
        body, name=name, grid=((gn, gm) if n_outer else (gm, gn)) + (ksplit,),
        out_shape=[out_shape], in_specs=[pl.BlockSpec(a_blk, a_map), pl.BlockSpec(b_blk, b_map)],
        out_specs=[out_spec], scratch_shapes=[pltpu.VMEM((tm, tn), F32)] if ksplit > 1 else [],
        args=(a, b), rides=rides, prefetch=every_other)


def _dh_matmul(dproj, w, tm, tn, ksplit, rides):
    t, kdim = dproj.shape
    d = w.shape[0]
    tk = kdim // ksplit

    def body(a_ref, b_ref, o_ref):
        k, n = pl.program_id(1), pl.program_id(2)
        cols = pl.ds(pl.multiple_of(n * tn, tn), tn)
        part = _dot(a_ref[...], b_ref[...], NT)

        @pl.when(k == 0)
        def _():
            o_ref[:, cols] = part

        @pl.when(k > 0)
        def _():
            o_ref[:, cols] += part

    return _call(
        body, name="dh_matmul", grid=(t // tm, ksplit, d // tn),
        out_shape=[jax.ShapeDtypeStruct((t, d), F32)],
        in_specs=[pl.BlockSpec((tm, tk), lambda m, k, n: (m, k)), pl.BlockSpec((tn, tk), lambda m, k, n: (n, k))],
        out_specs=[pl.BlockSpec((tm, d), lambda m, k, n: (m, 0))],
        args=(dproj, w), rides=rides)


def _attention_bwd(proj, o, tot, firsts, dycat, qg, kg, nh):
    t, in_cols = proj.shape
    tq = _pick(t, 256, 16)
    nq = t // tq
    group = 2 if nq % 2 == 0 else 1
    scale = HEAD_DIM ** -0.5

    def body(q_ref, k_ref, v_ref, g_ref, o_ref, tot_ref, first_ref, dy_ref, qg_ref, kg_ref, dproj_ref, gains_ref,
             qn, kn, vb, dob, dk_acc, dv_acc, dq_acc, outs, sems):
        h = pl.program_id(0)

        def norm(src, gain, dst):
            v = src[...]
            r = lax.rsqrt(jnp.mean(v * v, axis=-1, keepdims=True) + EPS)
            dst[...] = ((v * r) * gain[...]).astype(BF16)

        norm(q_ref, qg_ref, qn)
        norm(k_ref, kg_ref, kn)
        vb[...] = v_ref[...].astype(BF16)
        gs = g_ref[...]
        dyv = dy_ref[...]
        dob[...] = (dyv * _silu(gs)).astype(BF16)
        outs[3] = (dyv * o_ref[...] * _dsilu(gs)).astype(BF16)
        dk_acc[...] = jnp.zeros_like(dk_acc)
        dv_acc[...] = jnp.zeros_like(dv_acc)

        def before_matrix(n, strict):
            r = lax.broadcasted_iota(jnp.int32, (n, n), 0)
            c = lax.broadcasted_iota(jnp.int32, (n, n), 1)
            return ((c < r) if strict else (c <= r)).astype(BF16)

        incl = {n: before_matrix(n, False) for n in (tq, 2 * tq)}
        excl = {n: before_matrix(n, True) for n in (tq, 2 * tq)}
        lane = lax.broadcasted_iota(jnp.int32, (1, LANES), 1)
        key_minus_query = (lax.broadcasted_iota(jnp.int32, (2 * tq, tq), 0)
                           - lax.broadcasted_iota(jnp.int32, (2 * tq, tq), 1))

        def block(start, width, qi, doi, tot_row, p_left, g_left, dq, q_start=None):
            ks = pl.ds(pl.multiple_of(start, tq), width)
            kj = kn[ks, :]
            z = _dot(kj, qi, NT) * scale
            sp = _softplus(z)
            ls = -sp
            if q_start is not None:
                causal = key_minus_query < q_start - start
                ls = jnp.where(causal, ls, 0.0)
            hi = ls.astype(BF16)
            lo = (ls - hi.astype(F32)).astype(BF16)
            p_inc = _dot(incl[width], hi, NN) + _dot(incl[width], lo, NN) + p_left
            beta = jnp.exp(z - sp)
            w = beta * jnp.exp(tot_row - p_inc)
            if q_start is not None:
                w = jnp.where(causal, w, 0.0)
            dw = _dot(vb[ks, :], doi, NT)
            g = w * dw
            g_before = _dot(excl[width], g.astype(BF16), NN) + g_left
            dz = g * (1.0 - beta) - beta * g_before
            if q_start is not None:
                dz = jnp.where(causal, dz, 0.0)
            dzb = dz.astype(BF16)
            dv_acc[ks, :] += _dot(w.astype(BF16), doi, NN)
            dk_acc[ks, :] += _dot(dzb, qi, NN)
            dq = dq + _dot(dzb, kj, TN)
            p_left = p_left + jnp.sum(ls, axis=0, keepdims=True)
            g_left = g_left + jnp.sum(g, axis=0, keepdims=True)
            return p_left, g_left, dq

        def operands(i):
            qs = pl.ds(pl.multiple_of(i * tq, tq), tq)
            return qn[qs, :], dob[qs, :], jnp.transpose(tot_ref[qs, :])[0:1, :]

        def singles(i):
            qi, doi, tot_row = operands(i)
            zero_row = jnp.zeros((1, tq), F32)

            def k_step(j, carry):
                return block(j * tq, tq, qi, doi, tot_row, carry[0], carry[1], carry[2])

            left = jnp.maximum(i - 1, 0)
            first = jnp.sum(jnp.where(lane == i, first_ref[0:1, :], 0.0)).astype(jnp.int32)
            first = jnp.clip(first, 0, left)
            return lax.fori_loop(first, left, k_step, (zero_row, zero_row, jnp.zeros((tq, HEAD_DIM), F32)))

        def wide_step(i, carry):
            qi, doi, tot_row = operands(i)
            left = jnp.maximum(i - 1, 0)
            _, _, dq = block(left * tq, 2 * tq, qi, doi, tot_row, carry[0], carry[1], carry[2], i * tq)
            dq_acc[pl.ds(pl.multiple_of(i * tq, tq), tq), :] = dq * scale

        def q_group(p, _):
            blocks = [p + b * (nq // group) for b in range(group)]
            carries = [singles(i) for i in blocks]
            for i, carry in zip(blocks, carries):
                wide_step(i, carry)
            return 0

        lax.fori_loop(0, nq // group, q_group, 0)

        def norm_bwd(src, gain, dn, slot, gain_row):
            v = src[...]
            r = lax.rsqrt(jnp.mean(v * v, axis=-1, keepdims=True) + EPS)
            vhat = v * r
            gains_ref[gain_row:gain_row + 1, :] = jnp.sum(dn * vhat, axis=0, keepdims=True)
            dhat = dn * gain[...]
            outs[slot] = (r * (dhat - vhat * jnp.mean(dhat * vhat, axis=-1, keepdims=True))).astype(BF16)

        gains_ref[...] = jnp.zeros_like(gains_ref)
        norm_bwd(q_ref, qg_ref, dq_acc[...], 0, 0)
        norm_bwd(k_ref, kg_ref, dk_acc[...] * scale, 1, 1)
        outs[2] = dv_acc[...].astype(BF16)
        copies = [pltpu.make_async_copy(
            outs.at[s], dproj_ref.at[:, pl.ds(pl.multiple_of((s * nh + h) * HEAD_DIM, HEAD_DIM), HEAD_DIM)], sems.at[s])
            for s in range(4)]
        for cp in copies:
            cp.start()
        for cp in copies:
            cp.wait()

    col_block = lambda off: pl.BlockSpec((t, HEAD_DIM), lambda h: (0, off + h))
    vec = pl.BlockSpec((1, HEAD_DIM), lambda h: (0, 0))
    head_scr = lambda dt: pltpu.VMEM((t, HEAD_DIM), dt)
    return pl.pallas_call(
        body, name="attention_bwd", grid=(nh,),
        out_shape=[jax.ShapeDtypeStruct((t, in_cols), BF16), jax.ShapeDtypeStruct((nh, 8, HEAD_DIM), F32)],
        in_specs=[col_block(0), col_block(nh), col_block(2 * nh), col_block(3 * nh),
                  col_block(0), col_block(0), pl.BlockSpec((None, 8, LANES), lambda h: (h, 0, 0)), col_block(0), vec, vec],
        out_specs=[pl.BlockSpec(memory_space=pl.ANY), pl.BlockSpec((None, 8, HEAD_DIM), lambda h: (h, 0, 0))],
        scratch_shapes=[head_scr(BF16), head_scr(BF16), head_scr(BF16), head_scr(BF16),
                        head_scr(F32), head_scr(F32), head_scr(F32),
                        pltpu.VMEM((4, t, HEAD_DIM), BF16), pltpu.SemaphoreType.DMA((4,))],
        compiler_params=_params(("arbitrary",)),
    )(proj, proj, proj, proj, o, tot, firsts, dycat, qg, kg)


def _pointwise_bwd(dycat, z, proj, dproj, wpw, hc, ln_g, ln_b, cw):
    t = z.shape[0]
    tt = _pick(t, 256, 16)
    ncol = proj.shape[1] // cw
    ycol = dycat.shape[1] // cw - 1

    def body(dy_ref, z_ref, g_ref, w_ref, hc_ref, lg_ref, lb_ref, dp_in, dz_ref, dp_ref, dh_ref, sums_ref):
        i = pl.program_id(0)

        @pl.when(i == 0)
        def _():
            sums_ref[...] = jnp.zeros_like(sums_ref)

        g = g_ref[...]
        dy = dy_ref[...]
        dz = dy * _silu(g)
        dzb = dz.astype(BF16)
        dz_ref[...] = dzb
        dp_ref[...] = (dy * z_ref[...] * _dsilu(g)).astype(BF16)
        da = _dot(dzb, w_ref[...], NT)
        hcv = hc_ref[...]
        mu = jnp.mean(hcv, axis=-1, keepdims=True)
        xc = hcv - mu
        r = lax.rsqrt(jnp.mean(xc * xc, axis=-1, keepdims=True) + EPS)
        xhat = xc * r
        ln = xhat * lg_ref[...] + lb_ref[...]
        dln = da * _dsilu(ln)
        dxhat = dln * lg_ref[...]
        dhc = r * (dxhat - jnp.mean(dxhat, axis=-1, keepdims=True)
                   - xhat * jnp.mean(dxhat * xhat, axis=-1, keepdims=True))
        dh_ref[...] = dhc
        sums_ref[0:1, :] += jnp.sum(dz, axis=0, keepdims=True)
        sums_ref[1:2, :] += jnp.sum(dln * xhat, axis=0, keepdims=True)
        sums_ref[2:3, :] += jnp.sum(dln, axis=0, keepdims=True)
        sums_ref[3:4, :] += jnp.sum(dhc, axis=0, keepdims=True)

    tile = lambda col: pl.BlockSpec((tt, cw), lambda i: (i, col))
    vec = pl.BlockSpec((1, cw), lambda i: (0, 0))
    return pl.pallas_call(
        body, name="pointwise_bwd", grid=(t // tt,),
        out_shape=[jax.ShapeDtypeStruct((t, cw), BF16), jax.ShapeDtypeStruct(dproj.shape, dproj.dtype),
                   jax.ShapeDtypeStruct((t, cw), F32), jax.ShapeDtypeStruct((8, cw), F32)],
        in_specs=[tile(ycol), tile(0), tile(ncol - 1), pl.BlockSpec((cw, cw), lambda i: (0, 0)), tile(0), vec, vec,
                  pl.BlockSpec(memory_space=pl.ANY)],
        out_specs=[tile(0), tile(ncol - 1), tile(0), pl.BlockSpec((8, cw), lambda i: (0, 0))],
        input_output_aliases={7: 1},
        compiler_params=_params(("arbitrary",)),
    )(dycat, z, proj, wpw, hc, ln_g, ln_b, dproj)


def _conv_bwd(dhc, proj, w_dw, dproj, kc, cw):
    t = proj.shape[0]
    tt = _pick(t, 128, HALO)
    per = tt // HALO
    nt = t // tt
    chunk = _pick(cw, 256)
    ncol = proj.shape[1] // cw
    wr = w_dw.shape[0]

    def body(d_ref, dn_ref, u_ref, g_ref, up_ref, gp_ref, w_ref, dp_in, dp_ref, dw_ref, dbuf, hbuf, dw_acc):
        i = pl.program_id(0)

        @pl.when(i == 0)
        def _():
            dw_acc[...] = jnp.zeros_like(dw_acc)

        dbuf[0, pl.ds(0, tt), :] = d_ref[...]
        dbuf[0, pl.ds(tt, HALO), :] = jnp.where(i < nt - 1, dn_ref[...], 0.0)
        hbuf[0, pl.ds(HALO, tt), :] = _glu_rows(u_ref, g_ref)
        hbuf[0, pl.ds(0, HALO), :] = jnp.where(i > 0, _glu_rows(up_ref, gp_ref), 0.0)
        _fill_shifts(dbuf, tt + HALO - SUBLANES)
        _fill_shifts(hbuf, tt + HALO - SUBLANES)
        for lo in range(0, cw, chunk):
            dhg = _conv_taps(dbuf, w_ref, kc, lo, tt, chunk, lambda j: (kc - 1) - j)
            u = u_ref[:, lo:lo + chunk]
            sg = _sigmoid(g_ref[:, lo:lo + chunk])
            dp_ref[:, lo:lo + chunk] = (dhg * sg).astype(BF16)
            dp_ref[:, cw + lo:cw + lo + chunk] = (dhg * u * sg * (1.0 - sg)).astype(BF16)
            dtile = d_ref[:, lo:lo + chunk]
            for j in range(kc):
                prod = dtile * _shifted_rows(hbuf, HALO - (kc - 1) + j, tt, lo, chunk)
                dw_acc[j, :, lo:lo + chunk] += jnp.sum(prod.reshape(tt // SUBLANES, SUBLANES, chunk), axis=0)

        @pl.when(i == nt - 1)
        def _():
            dw_ref[...] = jnp.sum(dw_acc[...], axis=1)

    tile = lambda g: pl.BlockSpec((tt, cw), lambda i: (i, g))
    prev = lambda g: pl.BlockSpec((HALO, cw), lambda i: (jnp.maximum(i * per - 1, 0), g))
    return pl.pallas_call(
        body, name="conv_bwd", grid=(nt,),
        out_shape=[jax.ShapeDtypeStruct(dproj.shape, dproj.dtype), jax.ShapeDtypeStruct((wr, cw), F32)],
        in_specs=[pl.BlockSpec((tt, cw), lambda i: (i, 0)),
                  pl.BlockSpec((HALO, cw), lambda i: (jnp.minimum((i + 1) * per, nt * per - 1), 0)),
                  tile(ncol - 3), tile(ncol - 2), prev(ncol - 3), prev(ncol - 2),
                  pl.BlockSpec((wr, cw), lambda i: (0, 0)), pl.BlockSpec(memory_space=pl.ANY)],
        out_specs=[pl.BlockSpec((tt, 2 * cw), lambda i: (i, (ncol - 3) // 2)),
                   pl.BlockSpec((wr, cw), lambda i: (0, 0))],
        scratch_shapes=[pltpu.VMEM((SUBLANES, tt + HALO, cw), F32), pltpu.VMEM((SUBLANES, HALO + tt, cw), F32),
                        pltpu.VMEM((wr, SUBLANES, cw), F32)],
        input_output_aliases={7: 0},
        compiler_params=_params(("arbitrary",)),
    )(dhc, dhc, proj, proj, proj, proj, w_dw, dproj)


def _input_grad(dh, x, dout, norm_g, mod, rides=()):
    t, d = x.shape
    tt = _pick(t, 128, 16)

    def body(dh_ref, x_ref, do_ref, g_ref, mod_ref, gx_ref, sums_ref):
        i = pl.program_id(0)

        @pl.when(i == 0)
        def _():
            sums_ref[...] = jnp.zeros_like(sums_ref)

        xv = x_ref[...]
        dhv = dh_ref[...]
        r = lax.rsqrt(jnp.mean(xv * xv, axis=-1, keepdims=True) + EPS)
        xn = xv * r
        g = g_ref[...]
        one_scale = 1.0 + mod_ref[1:2, :]
        dxn = dhv * g * one_scale
        gx_ref[...] = do_ref[...] + r * (dxn - xn * jnp.mean(dxn * xn, axis=-1, keepdims=True))
        sums_ref[0:1, :] += jnp.sum(dhv, axis=0, keepdims=True)
        sums_ref[1:2, :] += jnp.sum(dhv * (xn * g), axis=0, keepdims=True)
        sums_ref[2:3, :] += jnp.sum(dhv * one_scale * xn, axis=0, keepdims=True)

    tile = pl.BlockSpec((tt, d), lambda i: (i, 0))
    return _call(
        body, name="input_grad", grid=(t // tt,),
        out_shape=[jax.ShapeDtypeStruct((t, d), F32), jax.ShapeDtypeStruct((8, d), F32)],
        in_specs=[tile, tile, tile, pl.BlockSpec((1, d), lambda i: (0, 0)), pl.BlockSpec((3, d), lambda i: (0, 0))],
        out_specs=[tile, pl.BlockSpec((8, d), lambda i: (0, 0))],
        args=(dh, x, dout, norm_g, mod), rides=rides)


def _sum_adam(parts, w, m, v, name):
    r, c = w.shape
    n_parts = parts.shape[0]
    tr = _pick(r, 128, 16) if r % 16 == 0 else r
    tc = _pick(c, 2048)

    def body(p_ref, w_ref, m_ref, v_ref, g_ref, d_ref, nm_ref, nv_ref):
        g = p_ref[0].astype(F32)
        for i in range(1, n_parts):
            g = g + p_ref[i].astype(F32)
        d, nm, nv = _adam(w_ref[...], g, m_ref[...], v_ref[...])
        g_ref[...] = g
        d_ref[...] = d
        nm_ref[...] = nm
        nv_ref[...] = nv

    tile = pl.BlockSpec((tr, tc), lambda i, j: (i, j))
    out = jax.ShapeDtypeStruct((r, c), F32)
    return pl.pallas_call(
        body, name=name, grid=(r // tr, c // tc),
        out_shape=[out] * 4,
        in_specs=[pl.BlockSpec((n_parts, tr, tc), lambda i, j: (0, i, j)), tile, tile, tile],
        out_specs=[tile] * 4,
        compiler_params=_params(("arbitrary", "arbitrary")),
    )(parts, w, m, v)


def _small_adam(parts, piece_rows, states):
    n_leaf = len(states)
    offsets = [sum(piece_rows[:i]) for i in range(len(piece_rows))]

    def total(p_ref, off, r):
        g = p_ref[0, off:off + r, :]
        for k in range(1, NDEV):
            g = g + p_ref[k, off:off + r, :]
        return g

    def body(p_ref, *refs):
        ins, outs = refs[:3 * n_leaf], refs[3 * n_leaf:]
        for i in range(n_leaf):
            w_ref, m_ref, v_ref = ins[3 * i:3 * i + 3]
            g = total(p_ref, offsets[i], w_ref.shape[0])
            d, nm, nv = _adam(w_ref[...], g, m_ref[...], v_ref[...])
            for o_ref, val in zip(outs[4 * i:4 * i + 4], (g, d, nm, nv)):
                o_ref[...] = val
        outs[4 * n_leaf][...] = total(p_ref, offsets[n_leaf], piece_rows[n_leaf])

    vmem = pl.BlockSpec(memory_space=pltpu.VMEM)
    flat = [a for leaf in states for a in leaf]
    out_shape = [jax.ShapeDtypeStruct(leaf[0].shape, F32) for leaf in states for _ in range(4)]
    out_shape.append(jax.ShapeDtypeStruct((piece_rows[n_leaf], LANES), F32))
    res = pl.pallas_call(
        body, name="small_adam", out_shape=out_shape,
        in_specs=[vmem] * (1 + len(flat)), out_specs=[vmem] * len(out_shape),
        compiler_params=pltpu.CompilerParams(vmem_limit_bytes=VMEM_LIMIT),
    )(parts, *flat)
    return [res[4 * i:4 * i + 4] for i in range(n_leaf)], res[4 * n_leaf]


def _ada_grad_adam(s_t, dm, w, m, v, rides=()):
    d, n = w.shape
    tr = _pick(d, 256, 16)

    def body(s_ref, dm_ref, w_ref, m_ref, v_ref, g_ref, d_ref, nm_ref, nv_ref):
        g = lax.dot_general(s_ref[...], dm_ref[...], (NN, ((), ())), preferred_element_type=F32,
                            precision=lax.Precision.HIGHEST)
        dl, nm, nv = _adam(w_ref[...], g, m_ref[...], v_ref[...])
        g_ref[...] = g
        d_ref[...] = dl
        nm_ref[...] = nm
        nv_ref[...] = nv

    tile = pl.BlockSpec((tr, n), lambda i: (i, 0))
    out = jax.ShapeDtypeStruct((d, n), F32)
    return _call(
        body, name="ada_grad_adam", grid=(d // tr,),
        out_shape=[out] * 4,
        in_specs=[pl.BlockSpec((tr, NDEV), lambda i: (i, 0)), pl.BlockSpec((NDEV, n), lambda i: (0, 0)),
                  tile, tile, tile],
        out_specs=[tile] * 4,
        args=(s_t, dm, w, m, v), rides=rides)


def _silu_t(c_all):
    n, d = c_all.shape

    def body(c_ref, o_ref):
        o_ref[...] = jnp.transpose(_silu(c_ref[...]))

    return pl.pallas_call(
        body, name="silu_t", out_shape=jax.ShapeDtypeStruct((d, n), F32),
        in_specs=[pl.BlockSpec(memory_space=pltpu.VMEM)], out_specs=pl.BlockSpec(memory_space=pltpu.VMEM),
        compiler_params=pltpu.CompilerParams(vmem_limit_bytes=VMEM_LIMIT),
    )(c_all)


def _rows128(v):
    return v.reshape(-1, LANES)


def _pad_rows(a, rows):
    return jnp.pad(a, ((0, rows - a.shape[0]), (0, 0)))


def kernel(x, c, norm_g, w_ada, b_ada, w_in, q_norm_g, k_norm_g, w_dw, b_dw, ln_g, ln_b, w_pw, b_pw, w_out, loss_target, m_norm_g, m_w_ada, m_b_ada, m_w_in, m_q_norm_g, m_k_norm_g, m_w_dw, m_b_dw, m_ln_g, m_ln_b, m_w_pw, m_b_pw, m_w_out, v_norm_g, v_w_ada, v_b_ada, v_w_in, v_q_norm_g, v_k_norm_g, v_w_dw, v_b_dw, v_ln_g, v_ln_b, v_w_pw, v_b_pw, v_w_out):
    _, t, d = x.shape
    n_ada = w_ada.shape[2]
    ns = w_in.shape[2]
    kc, cwl = w_dw.shape[1], w_dw.shape[2]
    cw = cwl * NDEV
    sb = d - cw
    nh = sb // HEAD_DIM
    assert sb == cw and kc - 1 <= HALO and NDEV * ns == 4 * sb + 3 * cw
    my = 4 * lax.axis_index("x") + 2 * lax.axis_index("y") + lax.axis_index("c")

    x2, tg2 = x[0], loss_target[0]

    wdw_rows = -(-kc // 8) * 8
    wdw_pad = _pad_rows(w_dw[0], wdw_rows)
    pay1 = jnp.concatenate([_rows128(c[0]), _rows128(wdw_pad.reshape(-1))], axis=0)
    (g1,) = _all_gather([pay1], "gather_cond", pltpu.VMEM)
    c_rows = d // LANES
    c_all = g1[:, :c_rows].reshape(NDEV, d)
    wdw_all = g1[:, c_rows:].reshape(NDEV, wdw_rows, cwl).transpose(1, 0, 2).reshape(wdw_rows, cw)

    b_ada_loc = lax.dynamic_slice(b_ada, (0, my * n_ada), (1, n_ada))
    mod_cols = _ada_matmul(c_all, w_ada[0], b_ada_loc)
    (g2,) = _all_gather([mod_cols], "gather_mod", pltpu.VMEM)
    mod_mine = lax.dynamic_index_in_dim(g2, my, axis=1, keepdims=False)
    mod = mod_mine.reshape(3, d)

    core = lax.axis_index("c").astype(jnp.int32).reshape(1)
    h = _modulated_norm(x2, norm_g, mod)
    wfull_in, proj = _gather_proj(h, w_in[0].astype(BF16))
    (o, tot, ycat, firsts), (partly,) = _attention_fwd(
        proj, q_norm_g, k_norm_g, nh, d, _gather_ride([w_out[0].astype(BF16), w_pw[0].astype(BF16)]))
    (a, hc), ((wg_out, wg_pw),) = _conv_fwd(proj, wdw_all, b_dw, ln_g, ln_b, kc, cw, _gather_finish_ride(partly))
    wfull_out, wfull_pw = wg_out.reshape(d, d), wg_pw.reshape(cw, cw)
    z, ycat = _pointwise_fwd(a, wfull_pw, b_pw, proj, ycat, cw)
    dout, dy, out_sums = _out_matmul(ycat, wfull_out, x2, tg2, mod)

    tile = _pick(t, 512, 16)
    (dycat,), _ = _mm(dy, wfull_out, NT, "dycat_matmul", tile, _pick(d, 512), F32)
    (p_wout,), _ = _mm(ycat, dy, TN, "w_out_grad", _pick(d, 512), _pick(d, 1024), BF16)
    p_wout = p_wout.reshape(NDEV, d // NDEV, d)
    dproj, gains = _attention_bwd(proj, o, tot, firsts, dycat, q_norm_g, k_norm_g, nh)
    dz, dproj, dhc, pw_sums = _pointwise_bwd(dycat, z, proj, dproj, wfull_pw, hc, ln_g, ln_b, cw)
    (p_wpw,), _ = _mm(a, dz, TN, "w_pw_grad", _pick(cw, 512), _pick(cw, 1024), BF16)
    p_wpw = p_wpw.reshape(NDEV, cw // NDEV, cw)
    dproj, dwdw = _conv_bwd(dhc, proj, wdw_all, dproj, kc, cw)
    p_wdw = dwdw.reshape(wdw_rows, NDEV, cwl).transpose(1, 0, 2).astype(BF16)

    def pair_sums(mine, theirs, name):
        return [_pair_sum(m, s, core, f"{name}_pair_sum_{i}") for i, (m, s) in enumerate(zip(mine, theirs))]

    lesser = [p.reshape(4, 2, *p.shape[1:]) for p in (p_wout, p_wpw, p_wdw)]
    (p_theirs,), (small_theirs,) = _mm(h, dproj, TN, "w_in_grad_sibling", _pick(d, 256), ns, BF16, slabs=True,
                                       n_outer=True, every_other=1 - core, rides=[_sibling_ride(lesser)])
    q_small = pair_sums(lesser, small_theirs, "small_grads")
    (p_mine,), (win_theirs, (r_wout, r_wpw, r_wdw)) = _mm(
        h, dproj, TN, "w_in_grad_own", _pick(d, 256), ns, BF16, slabs=True, n_outer=True, every_other=core,
        rides=[_sibling_ride([p_theirs]), _chip_exchange_ride(q_small)])
    q_win = pair_sums([p_mine[:, None]], win_theirs, "w_in_grad")
    cut1, cut2 = d * 13 // 16, d * 15 // 16
    (dh,), ((r_part,),) = _dh_matmul(dproj, wfull_in, tile, _pick(d, 512), 2,
                                     [_chip_exchange_ride(q_win, rows=(0, cut1))])
    (grad_x, in_sums), ((r_part,),) = _input_grad(
        dh, x2, dout, norm_g, mod, rides=[_chip_exchange_ride(q_win, rows=(cut1, cut2 - cut1), into=[r_part])])

    dmod = jnp.concatenate([in_sums[0], in_sums[1], out_sums[0]])
    loss_part = 0.5 / d * jnp.sum(out_sums[1].reshape(-1, LANES), axis=0)
    small = [in_sums[2], dmod, jnp.sum(gains[:, 0], axis=0), jnp.sum(gains[:, 1], axis=0),
             pw_sums[3], pw_sums[1], pw_sums[2], pw_sums[0], loss_part]
    pieces = [_rows128(s) for s in small]
    pieces = [_pad_rows(p, -(-p.shape[0] // 8) * 8) for p in pieces]
    (g3,) = _all_gather([jnp.concatenate(pieces, axis=0)], "gather_small", pltpu.VMEM)
    states = [[_rows128(s.reshape(-1)) for s in leaf] for leaf in (
        (norm_g, m_norm_g, v_norm_g), (b_ada, m_b_ada, v_b_ada), (q_norm_g, m_q_norm_g, v_q_norm_g),
        (k_norm_g, m_k_norm_g, v_k_norm_g), (b_dw, m_b_dw, v_b_dw), (ln_g, m_ln_g, v_ln_g),
        (ln_b, m_ln_b, v_ln_b), (b_pw, m_b_pw, v_b_pw))]
    small_out, loss_rows = _small_adam(g3, [p.shape[0] for p in pieces], states)
    g_small, d_small, m_small, v_small = [[leaf[k].reshape(1, -1) for leaf in small_out] for k in range(4)]
    loss = jnp.sum(loss_rows[0])

    off = pieces[0].shape[0]
    dmod_all = g3[:, off:off + pieces[1].shape[0]].reshape(NDEV, 3 * d)
    dmod_loc = lax.dynamic_slice(dmod_all, (0, my * n_ada), (NDEV, n_ada))
    ada, ((r_win,),) = _ada_grad_adam(_silu_t(c_all), dmod_loc, w_ada[0], m_w_ada[0], v_w_ada[0],
                                      rides=[_chip_exchange_ride(q_win, rows=(cut2, d - cut2), into=[r_part])])
    win = _sum_adam(r_win, w_in[0], m_w_in[0], v_w_in[0], "w_in_adam")
    wout = _sum_adam(r_wout, w_out[0], m_w_out[0], v_w_out[0], "w_out_adam")
    wpw = _sum_adam(r_wpw, w_pw[0], m_w_pw[0], v_w_pw[0], "w_pw_adam")
    wdw_state = [_pad_rows(s[0], wdw_rows) for s in (w_dw, m_w_dw, v_w_dw)]
    wdw = [r[:kc] for r in _sum_adam(r_wdw, *wdw_state, "w_dw_adam")]

    def group(k, small_list):
        s = small_list
        return [s[0], ada[k][None], s[1], win[k][None], s[2], s[3], wdw[k][None], s[4], s[5], s[6],
                wpw[k][None], s[7], wout[k][None]]

    return (loss, grad_x[None], *group(0, g_small), *group(1, d_small), *group(2, m_small), *group(3, v_small))
```

```python
import functools

import jax
import jax.numpy as jnp
from jax import lax
from jax.experimental import pallas as pl
from jax.experimental.pallas import tpu as pltpu

F32 = jnp.float32
BF16 = jnp.bfloat16
NDEV = 8
HEAD_DIM = 128
LANES = 128
SUBLANES = 8
HALO = 32
EPS = 1e-6
DEAD_LOG_WEIGHT = -104.0
VMEM_LIMIT = 56 * 1024 * 1024
MESH = pl.DeviceIdType.MESH

ADAM_LR = 0.001
ADAM_B1 = 0.9
ADAM_B2 = 0.999
ADAM_EPS = 1e-08
ADAM_WD = 0.01
ADAM_STEP = 10


def _params(sem=None):
    return pltpu.CompilerParams(dimension_semantics=sem, vmem_limit_bytes=VMEM_LIMIT)


def _pick(n, pref, unit=LANES):
    best = None
    for d in range(unit, min(n, pref) + 1, unit):
        if n % d == 0:
            best = d
    return best if best is not None else n


def _sigmoid(z):
    return 1.0 / (1.0 + jnp.exp(-z))


def _silu(z):
    return z * _sigmoid(z)


def _dsilu(z):
    s = _sigmoid(z)
    return s * (1.0 + z * (1.0 - s))


def _softplus(z):
    return jnp.maximum(z, 0.0) + jnp.log(1.0 + jnp.exp(-jnp.abs(z)))


def _dot(a, b, dims):
    return lax.dot_general(a, b, (dims, ((), ())), preferred_element_type=F32)


NN = ((1,), (0,))
NT = ((1,), (1,))
TN = ((0,), (0,))


def _adam(w, g, m, v):
    m = ADAM_B1 * m + (1.0 - ADAM_B1) * g
    v = ADAM_B2 * v + (1.0 - ADAM_B2) * (g * g)
    m_hat = m / (1.0 - ADAM_B1 ** ADAM_STEP)
    v_hat = v / (1.0 - ADAM_B2 ** ADAM_STEP)
    delta = -ADAM_LR * (m_hat / (jnp.sqrt(v_hat) + ADAM_EPS) + ADAM_WD * w)
    return delta, m, v


def _place():
    x, y, c = lax.axis_index("x"), lax.axis_index("y"), lax.axis_index("c")
    return x, y, c


def _flip(v, bit):
    return 1 - v if bit else v


def _all_gather(arrs, name, space):
    n = len(arrs)

    def body(*refs):
        ins, outs = refs[:n], refs[n:2 * n]
        send_sems, recv_sems, local_sems = refs[2 * n:]
        x, y, c = _place()
        me, sibling = (x, y, c), (x, y, 1 - c)
        chips = [(1 - x, y), (x, 1 - y), (1 - x, 1 - y)]

        def rows(a, p):
            return outs[a].at[4 * p[0] + 2 * p[1] + p[2]]

        def copy(a, k, block, to, src=None):
            return pltpu.make_async_remote_copy(
                src_ref=rows(a, block) if src is None else src, dst_ref=rows(a, block),
                send_sem=send_sems.at[7 * a + k], recv_sem=recv_sems.at[7 * a + k],
                device_id=to, device_id_type=MESH)

        mine = [pltpu.make_async_copy(ins[a], rows(a, me), local_sems.at[a]) for a in range(n)]
        for cp in mine:
            cp.start()
        first = []
        for a in range(n):
            first.append(copy(a, 0, me, sibling, src=ins[a]))
            first += [copy(a, 1 + j, me, (*chip, c), src=ins[a]) for j, chip in enumerate(chips)]
        for cp in first:
            cp.start()
        passed = []
        for j, chip in enumerate(chips):
            for a in range(n):
                copy(a, 1 + j, (*chip, c), me).wait_recv()
                fwd = copy(a, 4 + j, (*chip, c), sibling)
                fwd.start()
                passed.append(fwd)
        for a in range(n):
            copy(a, 0, sibling, me).wait_recv()
            for j, chip in enumerate(chips):
                copy(a, 4 + j, (*chip, 1 - c), me).wait_recv()
        for cp in first + passed:
            cp.wait_send()
        for cp in mine:
            cp.wait()

    spec = pl.BlockSpec(memory_space=space)
    return pl.pallas_call(
        body, name=name,
        out_shape=[jax.ShapeDtypeStruct((NDEV,) + a.shape, a.dtype) for a in arrs],
        in_specs=[spec] * n, out_specs=[spec] * n,
        scratch_shapes=[pltpu.SemaphoreType.DMA((7 * n,)), pltpu.SemaphoreType.DMA((7 * n,)),
                        pltpu.SemaphoreType.DMA((n,))],
        compiler_params=pltpu.CompilerParams(vmem_limit_bytes=VMEM_LIMIT),
    )(*arrs)


class _Ride:
    def __init__(self, ins, out_shapes, n_sems, start, finish, alias=None):
        self.ins, self.out_shapes, self.n_sems, self.start, self.finish = ins, out_shapes, n_sems, start, finish
        self.alias = alias or {}


def _call(body, *, name, grid, out_shape, in_specs, out_specs, args, scratch_shapes=(), rides=(), prefetch=None):
    sem = ("arbitrary",) * len(grid)
    rides = [r for r in rides if r is not None]
    n_pre = 0 if prefetch is None else 1
    n_in, n_out, n_scr = len(in_specs), len(out_specs), len(scratch_shapes)
    r_ins = [len(r.ins) for r in rides]
    r_outs = [len(r.out_shapes) for r in rides]

    def carried(*refs):
        pre, refs = refs[:n_pre], refs[n_pre:]
        ins, pos = refs[:n_in], n_in
        rins = []
        for k in r_ins:
            rins.append(refs[pos:pos + k])
            pos += k
        outs, pos = refs[pos:pos + n_out], pos + n_out
        routs = []
        for k in r_outs:
            routs.append(refs[pos:pos + k])
            pos += k
        scratch, pos = refs[pos:pos + n_scr], pos + n_scr
        sems = [refs[pos + 3 * i:pos + 3 * i + 3] for i in range(len(rides))]
        first = functools.reduce(lambda a, b: a & b, [pl.program_id(i) == 0 for i in range(len(grid))])
        last = functools.reduce(lambda a, b: a & b, [pl.program_id(i) == grid[i] - 1 for i in range(len(grid))])

        @pl.when(first)
        def _():
            for ride, ri, ro, s in zip(rides, rins, routs, sems):
                ride.start(ri, ro, *s)

        body(*pre, *ins, *outs, *scratch)

        @pl.when(last)
        def _():
            for ride, ri, ro, s in zip(rides, rins, routs, sems):
                ride.finish(ri, ro, *s)

    hbm = pl.BlockSpec(memory_space=pl.ANY)
    aliases, in_pos, out_pos = {}, n_pre + n_in, n_out
    for ride, ki, ko in zip(rides, r_ins, r_outs):
        aliases.update({in_pos + k: out_pos + o for k, o in ride.alias.items()})
        in_pos, out_pos = in_pos + ki, out_pos + ko
    all_scratch = list(scratch_shapes)
    for ride in rides:
        all_scratch += [pltpu.SemaphoreType.DMA((ride.n_sems,))] * 3
    all_in = list(in_specs) + [hbm] * sum(r_ins)
    all_out = list(out_specs) + [hbm] * sum(r_outs)
    shapes = list(out_shape) + [s for r in rides for s in r.out_shapes]
    operands = list(args) + [a for r in rides for a in r.ins]
    if prefetch is None:
        res = pl.pallas_call(
            carried, name=name, grid=grid, out_shape=shapes, in_specs=all_in, out_specs=all_out,
            scratch_shapes=all_scratch, input_output_aliases=aliases, compiler_params=_params(sem))(*operands)
    else:
        res = pl.pallas_call(
            carried, name=name, out_shape=shapes,
            grid_spec=pltpu.PrefetchScalarGridSpec(num_scalar_prefetch=1, grid=grid, in_specs=all_in,
                                                   out_specs=all_out, scratch_shapes=all_scratch),
            input_output_aliases=aliases, compiler_params=_params(sem))(prefetch, *operands)
    split, pos = [], n_out
    for k in r_outs:
        split.append(res[pos:pos + k])
        pos += k
    return res[:n_out], split


def _chips(x, y):
    return [(1 - x, y), (x, 1 - y), (1 - x, 1 - y)]


def _gather_ride(arrs):
    n = len(arrs)

    def copies(ins, outs, send_sems, recv_sems):
        x, y, c = _place()
        me = 4 * x + 2 * y + c
        peers = [(x, y, 1 - c)] + [(*chip, c) for chip in _chips(x, y)]
        return [pltpu.make_async_remote_copy(
            src_ref=ins[a], dst_ref=outs[a].at[me], send_sem=send_sems.at[4 * a + k], recv_sem=recv_sems.at[4 * a + k],
            device_id=p, device_id_type=MESH) for a in range(n) for k, p in enumerate(peers)], me

    def start(ins, outs, send_sems, recv_sems, local_sems):
        cps, me = copies(ins, outs, send_sems, recv_sems)
        for a in range(n):
            pltpu.make_async_copy(ins[a], outs[a].at[me], local_sems.at[a]).start()
        for cp in cps:
            cp.start()

    def finish(ins, outs, send_sems, recv_sems, local_sems):
        cps, me = copies(ins, outs, send_sems, recv_sems)
        for cp in cps:
            cp.wait_recv()
        for cp in cps:
            cp.wait_send()
        for a in range(n):
            pltpu.make_async_copy(ins[a], outs[a].at[me], local_sems.at[a]).wait()

    return _Ride(arrs, [jax.ShapeDtypeStruct((NDEV,) + a.shape, a.dtype) for a in arrs], 4 * n, start, finish)


def _gather_finish_ride(arrs):
    n = len(arrs)

    def copies(outs, send_sems, recv_sems):
        x, y, c = _place()
        cps = []
        for a in range(n):
            for k, chip in enumerate(_chips(x, y)):
                blk = 4 * chip[0] + 2 * chip[1]
                cps.append((pltpu.make_async_remote_copy(
                    src_ref=outs[a].at[blk + c], dst_ref=outs[a].at[blk + c],
                    send_sem=send_sems.at[3 * a + k], recv_sem=recv_sems.at[3 * a + k],
                    device_id=(x, y, 1 - c), device_id_type=MESH),
                    pltpu.make_async_remote_copy(
                    src_ref=outs[a].at[blk + 1 - c], dst_ref=outs[a].at[blk + 1 - c],
                    send_sem=send_sems.at[3 * a + k], recv_sem=recv_sems.at[3 * a + k],
                    device_id=(x, y, 1 - c), device_id_type=MESH)))
        return cps

    def start(ins, outs, send_sems, recv_sems, local_sems):
        for send, _ in copies(outs, send_sems, recv_sems):
            send.start()

    def finish(ins, outs, send_sems, recv_sems, local_sems):
        cps = copies(outs, send_sems, recv_sems)
        for _, recv in cps:
            recv.wait_recv()
        for send, _ in cps:
            send.wait_send()

    return _Ride(arrs, [jax.ShapeDtypeStruct(a.shape, a.dtype) for a in arrs], 3 * n, start, finish,
                 alias={a: a for a in range(n)})


def _sibling_ride(arrs):
    n = len(arrs)

    def copies(ins, outs, send_sems, recv_sems):
        x, y, c = _place()
        return [pltpu.make_async_remote_copy(
            src_ref=ins[a].at[:, 1 - c] if arrs[a].ndim == 4 else ins[a], dst_ref=outs[a],
            send_sem=send_sems.at[a], recv_sem=recv_sems.at[a],
            device_id=(x, y, 1 - c), device_id_type=MESH) for a in range(n)]

    def start(ins, outs, send_sems, recv_sems, local_sems):
        for cp in copies(ins, outs, send_sems, recv_sems):
            cp.start()

    def finish(ins, outs, send_sems, recv_sems, local_sems):
        for cp in copies(ins, outs, send_sems, recv_sems):
            cp.wait()

    return _Ride(arrs, [jax.ShapeDtypeStruct((4,) + a.shape[-2:], a.dtype) for a in arrs], n, start, finish)


def _pair_sum(mine, theirs, core, name):
    _, pick, r, c = mine.shape
    if pick == 1:
        core = jnp.zeros_like(core)
    tr = _pick(r, 512, 16)

    def body(core_ref, a_ref, b_ref, o_ref):
        o_ref[...] = (a_ref[...].astype(F32) + b_ref[...].astype(F32)).astype(BF16)

    return pl.pallas_call(
        body, name=name,
        grid_spec=pltpu.PrefetchScalarGridSpec(
            num_scalar_prefetch=1, grid=(4, r // tr),
            in_specs=[pl.BlockSpec((None, None, tr, c), lambda i, k, core_ref: (i, core_ref[0], k, 0)),
                      pl.BlockSpec((None, tr, c), lambda i, k, core_ref: (i, k, 0))],
            out_specs=pl.BlockSpec((None, tr, c), lambda i, k, core_ref: (i, k, 0))),
        out_shape=jax.ShapeDtypeStruct((4, r, c), BF16),
        compiler_params=_params(("arbitrary", "arbitrary")),
    )(core, mine, theirs)


def _chip_exchange_ride(arrs, rows=None, into=None):
    n = len(arrs)

    def part(ref):
        return ref if rows is None else ref.at[pl.ds(rows[0], rows[1]), :]

    def copies(ins, outs, send_sems, recv_sems):
        x, y, c = _place()
        mine = 2 * x + y
        return [pltpu.make_async_remote_copy(
            src_ref=part(ins[a].at[2 * chip[0] + chip[1]]), dst_ref=part(outs[a].at[mine]),
            send_sem=send_sems.at[3 * a + k], recv_sem=recv_sems.at[3 * a + k],
            device_id=(*chip, c), device_id_type=MESH) for a in range(n) for k, chip in enumerate(_chips(x, y))], mine

    def start(ins, outs, send_sems, recv_sems, local_sems):
        cps, mine = copies(ins, outs, send_sems, recv_sems)
        if into is None:
            for a in range(n):
                pltpu.make_async_copy(ins[a].at[mine], outs[a].at[mine], local_sems.at[a]).start()
        for cp in cps:
            cp.start()

    def finish(ins, outs, send_sems, recv_sems, local_sems):
        cps, mine = copies(ins, outs, send_sems, recv_sems)
        for cp in cps:
            cp.wait_recv()
        for cp in cps:
            cp.wait_send()
        if into is None:
            for a in range(n):
                pltpu.make_async_copy(ins[a].at[mine], outs[a].at[mine], local_sems.at[a]).wait()

    shapes = [jax.ShapeDtypeStruct(a.shape, a.dtype) for a in arrs]
    if into is None:
        return _Ride(arrs, shapes, 3 * n, start, finish)
    return _Ride(list(arrs) + list(into), shapes, 3 * n, start, finish, alias={n + a: a for a in range(n)})


def _ada_matmul(c_all, w_loc, b_loc):
    d, n = w_loc.shape
    bn = _pick(n, 512)

    def body(c_ref, w_ref, b_ref, o_ref):
        s = _silu(c_ref[...]).astype(BF16)
        o_ref[...] = _dot(s, w_ref[...].astype(BF16), NN) + b_ref[...]

    return pl.pallas_call(
        body, name="ada_matmul", grid=(n // bn,),
        out_shape=jax.ShapeDtypeStruct((NDEV, n), F32),
        in_specs=[pl.BlockSpec((NDEV, d), lambda j: (0, 0)), pl.BlockSpec((d, bn), lambda j: (0, j)),
                  pl.BlockSpec((1, bn), lambda j: (0, j))],
        out_specs=pl.BlockSpec((NDEV, bn), lambda j: (0, j)),
        compiler_params=_params(("arbitrary",)),
    )(c_all, w_loc, b_loc)


def _modulated_norm(x, norm_g, mod):
    t, d = x.shape
    tt = _pick(t, 256, 16)

    def body(x_ref, g_ref, mod_ref, h_ref):
        xv = x_ref[...]
        r = lax.rsqrt(jnp.mean(xv * xv, axis=-1, keepdims=True) + EPS)
        h = (xv * r) * g_ref[...] * (1.0 + mod_ref[1:2, :]) + mod_ref[0:1, :]
        h_ref[...] = h.astype(BF16)

    return pl.pallas_call(
        body, name="modulated_norm", grid=(t // tt,),
        out_shape=jax.ShapeDtypeStruct((t, d), BF16),
        in_specs=[pl.BlockSpec((tt, d), lambda i: (i, 0)), pl.BlockSpec((1, d), lambda i: (0, 0)),
                  pl.BlockSpec((3, d), lambda i: (0, 0))],
        out_specs=pl.BlockSpec((tt, d), lambda i: (i, 0)),
        compiler_params=_params(("arbitrary",)),
    )(x, norm_g, mod)


def _gather_proj(h, w_loc):
    t, d = h.shape
    ns = w_loc.shape[1]
    tm = _pick(t, 512, 16)
    nm = t // tm
    idx = lambda p: 4 * p[0] + 2 * p[1] + p[2]

    def peers():
        x, y, c = _place()
        flip = lambda a, b: a + b - 2 * a * b
        near, far = (flip(x, c), flip(y, 1 - c)), (flip(x, 1 - c), flip(y, c))
        return (x, y, c), (x, y, 1 - c), near, far, (1 - x, 1 - y), c

    me, sibling, near, far, diag, c = peers()
    order = [me, sibling, (*near, c), (*far, 1 - c), (*far, c), (*near, 1 - c), (*diag, c), (*diag, 1 - c)]
    order = jnp.stack([idx(p) for p in order]).astype(jnp.int32)

    def body(order_ref, a_ref, w_ref, wg_ref, o_ref, slab, send_sems, recv_sems, local_sems):
        j, m = pl.program_id(0), pl.program_id(1)
        me, sibling, near, far, diag, c = peers()

        def rows(p):
            return wg_ref.at[:, pl.ds(pl.multiple_of(idx(p) * ns, LANES), ns)]

        def copy(k, block, to, src=None):
            return pltpu.make_async_remote_copy(
                src_ref=rows(block) if src is None else src, dst_ref=rows(block),
                send_sem=send_sems.at[k], recv_sem=recv_sems.at[k], device_id=to, device_id_type=MESH)

        def load(src):
            cp = pltpu.make_async_copy(src, slab, local_sems.at[1])
            cp.start()
            cp.wait()

        keep = pltpu.make_async_copy(w_ref, rows(me), local_sems.at[0])
        own = [copy(0, me, sibling, src=w_ref), copy(1, me, (*near, c), src=w_ref), copy(2, me, (*far, c), src=w_ref)]
        relay = copy(3, (*near, c), (*far, c))
        passed = [copy(4, (*near, c), sibling), copy(5, (*far, c), sibling), copy(6, (*diag, c), sibling)]
        arrivals = [(1, 0, sibling, []), (2, 1, (*near, c), [passed[0], own[2], relay]), (3, 4, (*far, 1 - c), []),
                    (4, 2, (*far, c), [passed[1]]), (5, 5, (*near, 1 - c), []),
                    (6, 3, (*diag, c), [passed[2]]), (7, 6, (*diag, 1 - c), [])]

        @pl.when((j == 0) & (m == 0))
        def _():
            keep.start()
            for cp in own[:2]:
                cp.start()
            load(w_ref)

        for step, sem, block, onward in arrivals:
            @pl.when((j == step) & (m == 0))
            def _(sem=sem, block=block, onward=onward):
                copy(sem, block, me).wait_recv()
                for cp in onward:
                    cp.start()
                load(rows(block))

        o_ref[...] = _dot(a_ref[...], slab[...], NN)

        @pl.when((j == NDEV - 1) & (m == nm - 1))
        def _():
            for cp in own + [relay] + passed:
                cp.wait_send()
            keep.wait()

    hbm = pl.BlockSpec(memory_space=pl.ANY)
    return pl.pallas_call(
        body, name="gather_proj",
        grid_spec=pltpu.PrefetchScalarGridSpec(
            num_scalar_prefetch=1, grid=(NDEV, nm),
            in_specs=[pl.BlockSpec((tm, d), lambda j, m, order_ref: (m, 0)), hbm],
            out_specs=[hbm, pl.BlockSpec((tm, ns), lambda j, m, order_ref: (m, order_ref[j]))],
            scratch_shapes=[pltpu.VMEM((d, ns), BF16), pltpu.SemaphoreType.DMA((7,)), pltpu.SemaphoreType.DMA((7,)),
                            pltpu.SemaphoreType.DMA((2,))]),
        out_shape=[jax.ShapeDtypeStruct((d, NDEV * ns), BF16), jax.ShapeDtypeStruct((t, NDEV * ns), F32)],
        compiler_params=_params(("arbitrary", "arbitrary")),
    )(order, h, w_loc)


def _attention_fwd(proj, qg, kg, nh, d_model, ride):
    t = proj.shape[0]
    tq = _pick(t, 256, 16)
    nq = t // tq
    assert 2 <= nq <= LANES
    group = 4 if nq % 4 == 0 else 2 if nq % 2 == 0 else 1
    scale = HEAD_DIM ** -0.5

    def body(q_ref, k_ref, v_ref, g_ref, qg_ref, kg_ref, o_ref, tot_ref, y_ref, first_ref, qn, kn, vb):
        def norm(src, gain, dst):
            v = src[...]
            r = lax.rsqrt(jnp.mean(v * v, axis=-1, keepdims=True) + EPS)
            dst[...] = ((v * r) * gain[...]).astype(BF16)

        norm(q_ref, qg_ref, qn)
        norm(k_ref, kg_ref, kn)
        vb[...] = v_ref[...].astype(BF16)
        def after_matrix(n):
            return (lax.broadcasted_iota(jnp.int32, (n, n), 0) > lax.broadcasted_iota(jnp.int32, (n, n), 1)).astype(BF16)

        upper = {tq: after_matrix(tq), 2 * tq: after_matrix(2 * tq)}
        key_minus_query = (lax.broadcasted_iota(jnp.int32, (tq, 2 * tq), 1)
                           - lax.broadcasted_iota(jnp.int32, (tq, 2 * tq), 0))

        def block(qi, start, width, carry, acc, q_start=None):
            ks = pl.ds(pl.multiple_of(start, tq), width)
            z = _dot(qi, kn[ks, :], NT) * scale
            sp = _softplus(z)
            ls = -sp
            if q_start is not None:
                causal = key_minus_query < q_start - start
                ls = jnp.where(causal, ls, 0.0)
            hi = ls.astype(BF16)
            lo = (ls - hi.astype(F32)).astype(BF16)
            after = _dot(hi, upper[width], NN) + _dot(lo, upper[width], NN)
            w = jnp.exp(z - sp + after + carry)
            if q_start is not None:
                w = jnp.where(causal, w, 0.0)
            acc = acc + _dot(w.astype(BF16), vb[ks, :], NN)
            carry = carry + jnp.sum(ls, axis=1, keepdims=True)
            return carry, acc

        lane = lax.broadcasted_iota(jnp.int32, (8, LANES), 1)

        def live(carry):
            return (jnp.max(carry) > DEAD_LOG_WEIGHT).astype(jnp.int32)

        def wide_step(i):
            qi = qn[pl.ds(pl.multiple_of(i * tq, tq), tq), :]
            left = jnp.maximum(i - 1, 0)
            return block(qi, left * tq, 2 * tq, jnp.zeros((tq, 1), F32), jnp.zeros((tq, HEAD_DIM), F32), i * tq)

        def finish(i, carry, acc, firsts):
            qs = pl.ds(pl.multiple_of(i * tq, tq), tq)
            qi = qn[qs, :]
            left = jnp.maximum(i - 1, 0)

            def k_step(st):
                ca, ac = block(qi, (left - 1 - st[0]) * tq, tq, st[1], st[2])
                return st[0] + 1, ca, ac, live(ca)

            done, carry, acc, _ = lax.while_loop(
                lambda st: (st[0] < left) & (st[3] > 0), k_step, (jnp.int32(0), carry, acc, live(carry)))
            o_ref[qs, :] = acc
            tot_ref[qs, :] = jnp.broadcast_to(carry, (tq, HEAD_DIM))
            y_ref[qs, :] = (acc * _silu(g_ref[qs, :])).astype(BF16)
            return jnp.where(lane == i, (left - done).astype(F32), firsts)

        def q_group(p, firsts):
            blocks = [p + b * (nq // group) for b in range(group)]
            swept = [wide_step(i) for i in blocks]
            for i, (carry, acc) in zip(blocks, swept):
                firsts = finish(i, carry, acc, firsts)
            return firsts

        first_ref[...] = lax.fori_loop(0, nq // group, q_group, jnp.zeros((8, LANES), F32))

    col_block = lambda off: pl.BlockSpec((t, HEAD_DIM), lambda h: (0, off + h))
    vec = pl.BlockSpec((1, HEAD_DIM), lambda h: (0, 0))
    return _call(
        body, name="attention_fwd", grid=(nh,),
        out_shape=[jax.ShapeDtypeStruct((t, nh * HEAD_DIM), F32), jax.ShapeDtypeStruct((t, nh * HEAD_DIM), F32),
                   jax.ShapeDtypeStruct((t, d_model), BF16), jax.ShapeDtypeStruct((nh, 8, LANES), F32)],
        in_specs=[col_block(0), col_block(nh), col_block(2 * nh), col_block(3 * nh), vec, vec],
        out_specs=[col_block(0), col_block(0), col_block(0), pl.BlockSpec((None, 8, LANES), lambda h: (h, 0, 0))],
        scratch_shapes=[pltpu.VMEM((t, HEAD_DIM), BF16)] * 3,
        args=(proj, proj, proj, proj, qg, kg), rides=[ride])


def _fill_shifts(shifted, n_rows):
    for b in range(1, SUBLANES):
        shifted[b, pl.ds(0, n_rows), :] = shifted[0, pl.ds(b, n_rows), :]


def _shifted_rows(shifted, offset, n_rows, lo, cw):
    a, b = divmod(offset, SUBLANES)
    return shifted[b, pl.ds(SUBLANES * a, n_rows), pl.ds(lo, cw)]


def _conv_taps(shifted, w_ref, kc, lo, n_rows, cw, first_tap_row):
    acc = None
    for j in range(kc):
        term = w_ref[j:j + 1, lo:lo + cw] * _shifted_rows(shifted, first_tap_row(j), n_rows, lo, cw)
        acc = term if acc is None else acc + term
    return acc


def _glu_rows(u_ref, g_ref):
    return u_ref[...] * _sigmoid(g_ref[...])


def _conv_fwd(proj, w_dw, b_dw, ln_g, ln_b, kc, cw, ride):
    t = proj.shape[0]
    tt = _pick(t, 128, HALO)
    per = tt // HALO
    chunk = _pick(cw, 256)

    def body(u_ref, g_ref, up_ref, gp_ref, w_ref, b_ref, lg_ref, lb_ref, a_ref, hc_ref, buf):
        i = pl.program_id(0)
        buf[0, pl.ds(HALO, tt), :] = _glu_rows(u_ref, g_ref)
        halo = _glu_rows(up_ref, gp_ref)
        buf[0, pl.ds(0, HALO), :] = jnp.where(i > 0, halo, 0.0)
        _fill_shifts(buf, tt + HALO - SUBLANES)
        for lo in range(0, cw, chunk):
            conv = _conv_taps(buf, w_ref, kc, lo, tt, chunk, lambda j: HALO - (kc - 1) + j)
            hc_ref[:, lo:lo + chunk] = conv + b_ref[:, lo:lo + chunk]
        hc = hc_ref[...]
        mu = jnp.mean(hc, axis=-1, keepdims=True)
        xc = hc - mu
        var = jnp.mean(xc * xc, axis=-1, keepdims=True)
        ln = xc * lax.rsqrt(var + EPS) * lg_ref[...] + lb_ref[...]
        a_ref[...] = _silu(ln).astype(BF16)

    ncol = proj.shape[1] // cw
    tile = lambda g: pl.BlockSpec((tt, cw), lambda i: (i, g))
    prev = lambda g: pl.BlockSpec((HALO, cw), lambda i: (jnp.maximum(i * per - 1, 0), g))
    full = lambda r: pl.BlockSpec((r, cw), lambda i: (0, 0))
    return _call(
        body, name="conv_fwd", grid=(t // tt,),
        out_shape=[jax.ShapeDtypeStruct((t, cw), BF16), jax.ShapeDtypeStruct((t, cw), F32)],
        in_specs=[tile(ncol - 3), tile(ncol - 2), prev(ncol - 3), prev(ncol - 2),
                  full(w_dw.shape[0]), full(1), full(1), full(1)],
        out_specs=[pl.BlockSpec((tt, cw), lambda i: (i, 0))] * 2,
        scratch_shapes=[pltpu.VMEM((SUBLANES, HALO + tt, cw), F32)],
        args=(proj, proj, proj, proj, w_dw, b_dw, ln_g, ln_b), rides=[ride])


def _pointwise_fwd(a, wpw, b_pw, proj, ycat, cw):
    t = a.shape[0]
    tm = _pick(t, 256, 16)
    ncol = proj.shape[1] // cw
    ycol = ycat.shape[1] // cw - 1

    def body(a_ref, w_ref, b_ref, g_ref, y_in, z_ref, y_ref):
        z = _dot(a_ref[...], w_ref[...], NN) + b_ref[...]
        z_ref[...] = z
        y_ref[...] = (z * _silu(g_ref[...])).astype(BF16)

    return pl.pallas_call(
        body, name="pointwise_fwd", grid=(t // tm,),
        out_shape=[jax.ShapeDtypeStruct((t, cw), F32), jax.ShapeDtypeStruct(ycat.shape, ycat.dtype)],
        in_specs=[pl.BlockSpec((tm, cw), lambda m: (m, 0)),
                  pl.BlockSpec((cw, cw), lambda m: (0, 0)),
                  pl.BlockSpec((1, cw), lambda m: (0, 0)),
                  pl.BlockSpec((tm, cw), lambda m: (m, ncol - 1)),
                  pl.BlockSpec(memory_space=pl.ANY)],
        out_specs=[pl.BlockSpec((tm, cw), lambda m: (m, 0)), pl.BlockSpec((tm, cw), lambda m: (m, ycol))],
        input_output_aliases={4: 1},
        compiler_params=_params(("arbitrary",)),
    )(a, wpw, b_pw, proj, ycat)


def _out_matmul(ycat, wout, x, target, mod):
    t, d = x.shape
    kdim = wout.shape[0]
    tm, tn = _pick(t, 512, 16), _pick(d, 1024)
    inv_d = 1.0 / d

    def body(a_ref, w_ref, x_ref, tg_ref, mod_ref, dout_ref, dy_ref, sums_ref):
        @pl.when(pl.program_id(1) == 0)
        def _():
            sums_ref[...] = jnp.zeros_like(sums_ref)

        y = _dot(a_ref[...], w_ref[...], NN)
        gate = mod_ref[2:3, :]
        err = (x_ref[...] + gate * y) - tg_ref[...]
        dout = err * inv_d
        dout_ref[...] = dout
        dy_ref[...] = (dout * gate).astype(BF16)
        sums_ref[0:1, :] += jnp.sum(dout * y, axis=0, keepdims=True)
        sums_ref[1:2, :] += jnp.sum(err * err, axis=0, keepdims=True)

    mn = lambda n, m: (m, n)
    return pl.pallas_call(
        body, name="out_matmul", grid=(d // tn, t // tm),
        out_shape=[jax.ShapeDtypeStruct((t, d), F32), jax.ShapeDtypeStruct((t, d), BF16),
                   jax.ShapeDtypeStruct((8, d), F32)],
        in_specs=[pl.BlockSpec((tm, kdim), lambda n, m: (m, 0)),
                  pl.BlockSpec((kdim, tn), lambda n, m: (0, n)),
                  pl.BlockSpec((tm, tn), mn), pl.BlockSpec((tm, tn), mn),
                  pl.BlockSpec((3, tn), lambda n, m: (0, n))],
        out_specs=[pl.BlockSpec((tm, tn), mn), pl.BlockSpec((tm, tn), mn),
                   pl.BlockSpec((8, tn), lambda n, m: (0, n))],
        compiler_params=_params(("arbitrary", "arbitrary")),
    )(ycat, wout, x, target, mod)


def _mm(a, b, form, name, tm, tn, out_dtype, *, slabs=False, n_outer=False, ksplit=1, every_other=None, rides=()):
    if form == TN:
        kdim, m_dim = a.shape
    else:
        m_dim, kdim = a.shape
    n_dim = (b.shape[0] if form == NT else b.shape[1]) // (1 if every_other is None else 2)
    tk = kdim // ksplit
    gm, gn = m_dim // tm, n_dim // tn
    mn = (lambda g: (g[1], g[0])) if n_outer else (lambda g: (g[0], g[1]))
    b_col = (lambda g: mn(g)[1]) if every_other is None else (lambda g: 2 * mn(g)[1] + g[3][0])
    a_map = (lambda *g: (g[2], mn(g)[0])) if form == TN else (lambda *g: (mn(g)[0], g[2]))
    b_map = (lambda *g: (b_col(g), g[2])) if form == NT else (lambda *g: (g[2], b_col(g)))
    a_blk = (tk, tm) if form == TN else (tm, tk)
    b_blk = (tn, tk) if form == NT else (tk, tn)
    if slabs:
        out_shape = jax.ShapeDtypeStruct((gn, m_dim, tn), out_dtype)
        out_spec = pl.BlockSpec((None, tm, tn), lambda *g: (mn(g)[1], mn(g)[0], 0))
    else:
        out_shape = jax.ShapeDtypeStruct((m_dim, n_dim), out_dtype)
        out_spec = pl.BlockSpec((tm, tn), lambda *g: mn(g))

    def body(*refs):
        a_ref, b_ref, o_ref, *acc = refs if every_other is None else refs[1:]
        part = _dot(a_ref[...], b_ref[...], form)
        if ksplit == 1:
            o_ref[...] = part.astype(out_dtype)
            return
        k = pl.program_id(2)

        @pl.when(k == 0)
        def _():
            acc[0][...] = part

        @pl.when((k > 0) & (k < ksplit - 1))
        def _():
            acc[0][...] += part

        @pl.when(k == ksplit - 1)
        def _():
            o_ref[...] = (acc[0][...] + part).astype(out_dtype)

    return _call(
        body, name=name, grid=((gn, gm) if n_outer else (gm, gn)) + (ksplit,),
        out_shape=[out_shape], in_specs=[pl.BlockSpec(a_blk, a_map), pl.BlockSpec(b_blk, b_map)],
        out_specs=[out_spec], scratch_shapes=[pltpu.VMEM((tm, tn), F32)] if ksplit > 1 else [],
        args=(a, b), rides=rides, prefetch=every_other)


def _dh_matmul(dproj, w, tm, tn, ksplit, rides):
    t, kdim = dproj.shape
    d = w.shape[0]
    tk = kdim // ksplit

    def body(a_ref, b_ref, o_ref):
        k, n = pl.program_id(1), pl.program_id(2)
        cols = pl.ds(pl.multiple_of(n * tn, tn), tn)
        part = _dot(a_ref[...], b_ref[...], NT)

        @pl.when(k == 0)
        def _():
            o_ref[:, cols] = part

        @pl.when(k > 0)
        def _():
            o_ref[:, cols] += part

    return _call(
        body, name="dh_matmul", grid=(t // tm, ksplit, d // tn),
        out_shape=[jax.ShapeDtypeStruct((t, d), F32)],
        in_specs=[pl.BlockSpec((tm, tk), lambda m, k, n: (m, k)), pl.BlockSpec((tn, tk), lambda m, k, n: (n, k))],
        out_specs=[pl.BlockSpec((tm, d), lambda m, k, n: (m, 0))],
        args=(dproj, w), rides=rides)


def _attention_bwd(proj, o, tot, firsts, dycat, qg, kg, nh):
    t, in_cols = proj.shape
    tq = _pick(t, 256, 16)
    nq = t // tq
    group = 2 if nq % 2 == 0 else 1
    scale = HEAD_DIM ** -0.5

    def body(q_ref, k_ref, v_ref, g_ref, o_ref, tot_ref, first_ref, dy_ref, qg_ref, kg_ref, dproj_ref, gains_ref,
             qn, kn, vb, dob, dk_acc, dv_acc, dq_acc, outs, sems):
        h = pl.program_id(0)

        def norm(src, gain, dst):
            v = src[...]
            r = lax.rsqrt(jnp.mean(v * v, axis=-1, keepdims=True) + EPS)
            dst[...] = ((v * r) * gain[...]).astype(BF16)

        norm(q_ref, qg_ref, qn)
        norm(k_ref, kg_ref, kn)
        vb[...] = v_ref[...].astype(BF16)
        gs = g_ref[...]
        dyv = dy_ref[...]
        dob[...] = (dyv * _silu(gs)).astype(BF16)
        outs[3] = (dyv * o_ref[...] * _dsilu(gs)).astype(BF16)
        dk_acc[...] = jnp.zeros_like(dk_acc)
        dv_acc[...] = jnp.zeros_like(dv_acc)

        def before_matrix(n):
            r = lax.broadcasted_iota(jnp.int32, (n, n), 0)
            c = lax.broadcasted_iota(jnp.int32, (n, n), 1)
            return (c <= r).astype(BF16)

        incl = {n: before_matrix(n) for n in (tq, 2 * tq)}
        lane = lax.broadcasted_iota(jnp.int32, (1, LANES), 1)
        key_minus_query = (lax.broadcasted_iota(jnp.int32, (2 * tq, tq), 0)
                           - lax.broadcasted_iota(jnp.int32, (2 * tq, tq), 1))

        def block(start, width, qi, doi, tot_row, p_left, g_left, dq, q_start=None):
            ks = pl.ds(pl.multiple_of(start, tq), width)
            kj = kn[ks, :]
            z = _dot(kj, qi, NT) * scale
            sp = _softplus(z)
            ls = -sp
            if q_start is not None:
                causal = key_minus_query < q_start - start
                ls = jnp.where(causal, ls, 0.0)
            hi = ls.astype(BF16)
            lo = (ls - hi.astype(F32)).astype(BF16)
            p_inc = _dot(incl[width], hi, NN) + _dot(incl[width], lo, NN) + p_left
            beta = jnp.exp(z - sp)
            w = beta * jnp.exp(tot_row - p_inc)
            if q_start is not None:
                w = jnp.where(causal, w, 0.0)
            dw = _dot(vb[ks, :], doi, NT)
            g = w * dw
            g_upto = _dot(incl[width], g.astype(BF16), NN) + g_left
            dz = g - beta * g_upto
            if q_start is not None:
                dz = jnp.where(causal, dz, 0.0)
            dzb = dz.astype(BF16)
            dv_acc[ks, :] += _dot(w.astype(BF16), doi, NN)
            dk_acc[ks, :] += _dot(dzb, qi, NN)
            dq = dq + _dot(dzb, kj, TN)
            p_left = p_left + jnp.sum(ls, axis=0, keepdims=True)
            g_left = g_left + jnp.sum(g, axis=0, keepdims=True)
            return p_left, g_left, dq

        def operands(i):
            qs = pl.ds(pl.multiple_of(i * tq, tq), tq)
            return qn[qs, :], dob[qs, :], jnp.transpose(tot_ref[qs, :])[0:1, :]

        def singles(i):
            qi, doi, tot_row = operands(i)
            zero_row = jnp.zeros((1, tq), F32)

            def k_step(j, carry):
                return block(j * tq, tq, qi, doi, tot_row, carry[0], carry[1], carry[2])

            left = jnp.maximum(i - 1, 0)
            first = jnp.sum(jnp.where(lane == i, first_ref[0:1, :], 0.0)).astype(jnp.int32)
            first = jnp.clip(first, 0, left)
            return lax.fori_loop(first, left, k_step, (zero_row, zero_row, jnp.zeros((tq, HEAD_DIM), F32)))

        def wide_step(i, carry):
            qi, doi, tot_row = operands(i)
            left = jnp.maximum(i - 1, 0)
            _, _, dq = block(left * tq, 2 * tq, qi, doi, tot_row, carry[0], carry[1], carry[2], i * tq)
            dq_acc[pl.ds(pl.multiple_of(i * tq, tq), tq), :] = dq * scale

        def q_group(p, _):
            blocks = [p + b * (nq // group) for b in range(group)]
            carries = [singles(i) for i in blocks]
            for i, carry in zip(blocks, carries):
                wide_step(i, carry)
            return 0

        lax.fori_loop(0, nq // group, q_group, 0)

        def norm_bwd(src, gain, dn, slot, gain_row):
            v = src[...]
            r = lax.rsqrt(jnp.mean(v * v, axis=-1, keepdims=True) + EPS)
            vhat = v * r
            gains_ref[gain_row:gain_row + 1, :] = jnp.sum(dn * vhat, axis=0, keepdims=True)
            dhat = dn * gain[...]
            outs[slot] = (r * (dhat - vhat * jnp.mean(dhat * vhat, axis=-1, keepdims=True))).astype(BF16)

        gains_ref[...] = jnp.zeros_like(gains_ref)
        norm_bwd(q_ref, qg_ref, dq_acc[...], 0, 0)
        norm_bwd(k_ref, kg_ref, dk_acc[...] * scale, 1, 1)
        outs[2] = dv_acc[...].astype(BF16)
        copies = [pltpu.make_async_copy(
            outs.at[s], dproj_ref.at[:, pl.ds(pl.multiple_of((s * nh + h) * HEAD_DIM, HEAD_DIM), HEAD_DIM)], sems.at[s])
            for s in range(4)]
        for cp in copies:
            cp.start()
        for cp in copies:
            cp.wait()

    col_block = lambda off: pl.BlockSpec((t, HEAD_DIM), lambda h: (0, off + h))
    vec = pl.BlockSpec((1, HEAD_DIM), lambda h: (0, 0))
    head_scr = lambda dt: pltpu.VMEM((t, HEAD_DIM), dt)
    return pl.pallas_call(
        body, name="attention_bwd", grid=(nh,),
        out_shape=[jax.ShapeDtypeStruct((t, in_cols), BF16), jax.ShapeDtypeStruct((nh, 8, HEAD_DIM), F32)],
        in_specs=[col_block(0), col_block(nh), col_block(2 * nh), col_block(3 * nh),
                  col_block(0), col_block(0), pl.BlockSpec((None, 8, LANES), lambda h: (h, 0, 0)), col_block(0), vec, vec],
        out_specs=[pl.BlockSpec(memory_space=pl.ANY), pl.BlockSpec((None, 8, HEAD_DIM), lambda h: (h, 0, 0))],
        scratch_shapes=[head_scr(BF16), head_scr(BF16), head_scr(BF16), head_scr(BF16),
                        head_scr(F32), head_scr(F32), head_scr(F32),
                        pltpu.VMEM((4, t, HEAD_DIM), BF16), pltpu.SemaphoreType.DMA((4,))],
        compiler_params=_params(("arbitrary",)),
    )(proj, proj, proj, proj, o, tot, firsts, dycat, qg, kg)


def _pointwise_bwd(dycat, z, proj, dproj, wpw, hc, ln_g, ln_b, cw):
    t = z.shape[0]
    tt = _pick(t, 256, 16)
    ncol = proj.shape[1] // cw
    ycol = dycat.shape[1] // cw - 1

    def body(dy_ref, z_ref, g_ref, w_ref, hc_ref, lg_ref, lb_ref, dp_in, dz_ref, dp_ref, dh_ref, sums_ref):
        i = pl.program_id(0)

        @pl.when(i == 0)
        def _():
            sums_ref[...] = jnp.zeros_like(sums_ref)

        g = g_ref[...]
        dy = dy_ref[...]
        dz = dy * _silu(g)
        dzb = dz.astype(BF16)
        dz_ref[...] = dzb
        dp_ref[...] = (dy * z_ref[...] * _dsilu(g)).astype(BF16)
        da = _dot(dzb, w_ref[...], NT)
        hcv = hc_ref[...]
        mu = jnp.mean(hcv, axis=-1, keepdims=True)
        xc = hcv - mu
        r = lax.rsqrt(jnp.mean(xc * xc, axis=-1, keepdims=True) + EPS)
        xhat = xc * r
        ln = xhat * lg_ref[...] + lb_ref[...]
        dln = da * _dsilu(ln)
        dxhat = dln * lg_ref[...]
        dhc = r * (dxhat - jnp.mean(dxhat, axis=-1, keepdims=True)
                   - xhat * jnp.mean(dxhat * xhat, axis=-1, keepdims=True))
        dh_ref[...] = dhc
        sums_ref[0:1, :] += jnp.sum(dz, axis=0, keepdims=True)
        sums_ref[1:2, :] += jnp.sum(dln * xhat, axis=0, keepdims=True)
        sums_ref[2:3, :] += jnp.sum(dln, axis=0, keepdims=True)
        sums_ref[3:4, :] += jnp.sum(dhc, axis=0, keepdims=True)

    tile = lambda col: pl.BlockSpec((tt, cw), lambda i: (i, col))
    vec = pl.BlockSpec((1, cw), lambda i: (0, 0))
    return pl.pallas_call(
        body, name="pointwise_bwd", grid=(t // tt,),
        out_shape=[jax.ShapeDtypeStruct((t, cw), BF16), jax.ShapeDtypeStruct(dproj.shape, dproj.dtype),
                   jax.ShapeDtypeStruct((t, cw), F32), jax.ShapeDtypeStruct((8, cw), F32)],
        in_specs=[tile(ycol), tile(0), tile(ncol - 1), pl.BlockSpec((cw, cw), lambda i: (0, 0)), tile(0), vec, vec,
                  pl.BlockSpec(memory_space=pl.ANY)],
        out_specs=[tile(0), tile(ncol - 1), tile(0), pl.BlockSpec((8, cw), lambda i: (0, 0))],
        input_output_aliases={7: 1},
        compiler_params=_params(("arbitrary",)),
    )(dycat, z, proj, wpw, hc, ln_g, ln_b, dproj)


def _conv_bwd(dhc, proj, w_dw, dproj, kc, cw):
    t = proj.shape[0]
    tt = _pick(t, 128, HALO)
    per = tt // HALO
    nt = t // tt
    chunk = _pick(cw, 256)
    ncol = proj.shape[1] // cw
    wr = w_dw.shape[0]

    def body(d_ref, dn_ref, u_ref, g_ref, up_ref, gp_ref, w_ref, dp_in, dp_ref, dw_ref, dbuf, hbuf, dw_acc):
        i = pl.program_id(0)

        @pl.when(i == 0)
        def _():
            dw_acc[...] = jnp.zeros_like(dw_acc)

        dbuf[0, pl.ds(0, tt), :] = d_ref[...]
        dbuf[0, pl.ds(tt, HALO), :] = jnp.where(i < nt - 1, dn_ref[...], 0.0)
        hbuf[0, pl.ds(HALO, tt), :] = _glu_rows(u_ref, g_ref)
        hbuf[0, pl.ds(0, HALO), :] = jnp.where(i > 0, _glu_rows(up_ref, gp_ref), 0.0)
        _fill_shifts(dbuf, tt + HALO - SUBLANES)
        _fill_shifts(hbuf, tt + HALO - SUBLANES)
        for lo in range(0, cw, chunk):
            dhg = _conv_taps(dbuf, w_ref, kc, lo, tt, chunk, lambda j: (kc - 1) - j)
            u = u_ref[:, lo:lo + chunk]
            sg = _sigmoid(g_ref[:, lo:lo + chunk])
            dp_ref[:, lo:lo + chunk] = (dhg * sg).astype(BF16)
            dp_ref[:, cw + lo:cw + lo + chunk] = (dhg * u * sg * (1.0 - sg)).astype(BF16)
            dtile = d_ref[:, lo:lo + chunk]
            for j in range(kc):
                prod = dtile * _shifted_rows(hbuf, HALO - (kc - 1) + j, tt, lo, chunk)
                dw_acc[j, :, lo:lo + chunk] += jnp.sum(prod.reshape(tt // SUBLANES, SUBLANES, chunk), axis=0)

        @pl.when(i == nt - 1)
        def _():
            dw_ref[...] = jnp.sum(dw_acc[...], axis=1)

    tile = lambda g: pl.BlockSpec((tt, cw), lambda i: (i, g))
    prev = lambda g: pl.BlockSpec((HALO, cw), lambda i: (jnp.maximum(i * per - 1, 0), g))
    return pl.pallas_call(
        body, name="conv_bwd", grid=(nt,),
        out_shape=[jax.ShapeDtypeStruct(dproj.shape, dproj.dtype), jax.ShapeDtypeStruct((wr, cw), F32)],
        in_specs=[pl.BlockSpec((tt, cw), lambda i: (i, 0)),
                  pl.BlockSpec((HALO, cw), lambda i: (jnp.minimum((i + 1) * per, nt * per - 1), 0)),
                  tile(ncol - 3), tile(ncol - 2), prev(ncol - 3), prev(ncol - 2),
                  pl.BlockSpec((wr, cw), lambda i: (0, 0)), pl.BlockSpec(memory_space=pl.ANY)],
        out_specs=[pl.BlockSpec((tt, 2 * cw), lambda i: (i, (ncol - 3) // 2)),
                   pl.BlockSpec((wr, cw), lambda i: (0, 0))],
        scratch_shapes=[pltpu.VMEM((SUBLANES, tt + HALO, cw), F32), pltpu.VMEM((SUBLANES, HALO + tt, cw), F32),
                        pltpu.VMEM((wr, SUBLANES, cw), F32)],
        input_output_aliases={7: 0},
        compiler_params=_params(("arbitrary",)),
    )(dhc, dhc, proj, proj, proj, proj, w_dw, dproj)


def _input_grad(dh, x, dout, norm_g, mod, rides=()):
    t, d = x.shape
    tt = _pick(t, 128, 16)

    def body(dh_ref, x_ref, do_ref, g_ref, mod_ref, gx_ref, sums_ref):
        i = pl.program_id(0)

        @pl.when(i == 0)
        def _():
            sums_ref[...] = jnp.zeros_like(sums_ref)

        xv = x_ref[...]
        dhv = dh_ref[...]
        r = lax.rsqrt(jnp.mean(xv * xv, axis=-1, keepdims=True) + EPS)
        xn = xv * r
        g = g_ref[...]
        one_scale = 1.0 + mod_ref[1:2, :]
        dxn = dhv * g * one_scale
        gx_ref[...] = do_ref[...] + r * (dxn - xn * jnp.mean(dxn * xn, axis=-1, keepdims=True))
        sums_ref[0:1, :] += jnp.sum(dhv, axis=0, keepdims=True)
        sums_ref[1:2, :] += jnp.sum(dhv * (xn * g), axis=0, keepdims=True)
        sums_ref[2:3, :] += jnp.sum(dhv * one_scale * xn, axis=0, keepdims=True)

    tile = pl.BlockSpec((tt, d), lambda i: (i, 0))
    return _call(
        body, name="input_grad", grid=(t // tt,),
        out_shape=[jax.ShapeDtypeStruct((t, d), F32), jax.ShapeDtypeStruct((8, d), F32)],
        in_specs=[tile, tile, tile, pl.BlockSpec((1, d), lambda i: (0, 0)), pl.BlockSpec((3, d), lambda i: (0, 0))],
        out_specs=[tile, pl.BlockSpec((8, d), lambda i: (0, 0))],
        args=(dh, x, dout, norm_g, mod), rides=rides)


def _sum_adam(parts, w, m, v, name):
    r, c = w.shape
    n_parts = parts.shape[0]
    tr = _pick(r, 128, 16) if r % 16 == 0 else r
    tc = _pick(c, 2048)

    def body(p_ref, w_ref, m_ref, v_ref, g_ref, d_ref, nm_ref, nv_ref):
        g = p_ref[0].astype(F32)
        for i in range(1, n_parts):
            g = g + p_ref[i].astype(F32)
        d, nm, nv = _adam(w_ref[...], g, m_ref[...], v_ref[...])
        g_ref[...] = g
        d_ref[...] = d
        nm_ref[...] = nm
        nv_ref[...] = nv

    tile = pl.BlockSpec((tr, tc), lambda i, j: (i, j))
    out = jax.ShapeDtypeStruct((r, c), F32)
    return pl.pallas_call(
        body, name=name, grid=(r // tr, c // tc),
        out_shape=[out] * 4,
        in_specs=[pl.BlockSpec((n_parts, tr, tc), lambda i, j: (0, i, j)), tile, tile, tile],
        out_specs=[tile] * 4,
        compiler_params=_params(("arbitrary", "arbitrary")),
    )(parts, w, m, v)


def _small_adam(parts, piece_rows, states):
    n_leaf = len(states)
    offsets = [sum(piece_rows[:i]) for i in range(len(piece_rows))]

    def total(p_ref, off, r):
        g = p_ref[0, off:off + r, :]
        for k in range(1, NDEV):
            g = g + p_ref[k, off:off + r, :]
        return g

    def body(p_ref, *refs):
        ins, outs = refs[:3 * n_leaf], refs[3 * n_leaf:]
        for i in range(n_leaf):
            w_ref, m_ref, v_ref = ins[3 * i:3 * i + 3]
            g = total(p_ref, offsets[i], w_ref.shape[0])
            d, nm, nv = _adam(w_ref[...], g, m_ref[...], v_ref[...])
            for o_ref, val in zip(outs[4 * i:4 * i + 4], (g, d, nm, nv)):
                o_ref[...] = val
        outs[4 * n_leaf][...] = total(p_ref, offsets[n_leaf], piece_rows[n_leaf])

    vmem = pl.BlockSpec(memory_space=pltpu.VMEM)
    flat = [a for leaf in states for a in leaf]
    out_shape = [jax.ShapeDtypeStruct(leaf[0].shape, F32) for leaf in states for _ in range(4)]
    out_shape.append(jax.ShapeDtypeStruct((piece_rows[n_leaf], LANES), F32))
    res = pl.pallas_call(
        body, name="small_adam", out_shape=out_shape,
        in_specs=[vmem] * (1 + len(flat)), out_specs=[vmem] * len(out_shape),
        compiler_params=pltpu.CompilerParams(vmem_limit_bytes=VMEM_LIMIT),
    )(parts, *flat)
    return [res[4 * i:4 * i + 4] for i in range(n_leaf)], res[4 * n_leaf]


def _ada_grad_adam(s_t, dm, w, m, v):
    d, n = w.shape
    tr = _pick(d, 256, 16)

    def body(s_ref, dm_ref, w_ref, m_ref, v_ref, g_ref, d_ref, nm_ref, nv_ref):
        g = lax.dot_general(s_ref[...], dm_ref[...], (NN, ((), ())), preferred_element_type=F32,
                            precision=lax.Precision.HIGHEST)
        dl, nm, nv = _adam(w_ref[...], g, m_ref[...], v_ref[...])
        g_ref[...] = g
        d_ref[...] = dl
        nm_ref[...] = nm
        nv_ref[...] = nv

    tile = pl.BlockSpec((tr, n), lambda i: (i, 0))
    out = jax.ShapeDtypeStruct((d, n), F32)
    return pl.pallas_call(
        body, name="ada_grad_adam", grid=(d // tr,),
        out_shape=[out] * 4,
        in_specs=[pl.BlockSpec((tr, NDEV), lambda i: (i, 0)), pl.BlockSpec((NDEV, n), lambda i: (0, 0)),
                  tile, tile, tile],
        out_specs=[tile] * 4,
        compiler_params=_params(("arbitrary",)),
    )(s_t, dm, w, m, v)


def _silu_t(c_all):
    n, d = c_all.shape

    def body(c_ref, o_ref):
        o_ref[...] = jnp.transpose(_silu(c_ref[...]))

    return pl.pallas_call(
        body, name="silu_t", out_shape=jax.ShapeDtypeStruct((d, n), F32),
        in_specs=[pl.BlockSpec(memory_space=pltpu.VMEM)], out_specs=pl.BlockSpec(memory_space=pltpu.VMEM),
        compiler_params=pltpu.CompilerParams(vmem_limit_bytes=VMEM_LIMIT),
    )(c_all)


def _rows128(v):
    return v.reshape(-1, LANES)


def _pad_rows(a, rows):
    return jnp.pad(a, ((0, rows - a.shape[0]), (0, 0)))


def kernel(x, c, norm_g, w_ada, b_ada, w_in, q_norm_g, k_norm_g, w_dw, b_dw, ln_g, ln_b, w_pw, b_pw, w_out, loss_target, m_norm_g, m_w_ada, m_b_ada, m_w_in, m_q_norm_g, m_k_norm_g, m_w_dw, m_b_dw, m_ln_g, m_ln_b, m_w_pw, m_b_pw, m_w_out, v_norm_g, v_w_ada, v_b_ada, v_w_in, v_q_norm_g, v_k_norm_g, v_w_dw, v_b_dw, v_ln_g, v_ln_b, v_w_pw, v_b_pw, v_w_out):
    _, t, d = x.shape
    n_ada = w_ada.shape[2]
    ns = w_in.shape[2]
    kc, cwl = w_dw.shape[1], w_dw.shape[2]
    cw = cwl * NDEV
    sb = d - cw
    nh = sb // HEAD_DIM
    assert sb == cw and kc - 1 <= HALO and NDEV * ns == 4 * sb + 3 * cw
    my = 4 * lax.axis_index("x") + 2 * lax.axis_index("y") + lax.axis_index("c")

    x2, tg2 = x[0], loss_target[0]

    wdw_rows = -(-kc // 8) * 8
    wdw_pad = _pad_rows(w_dw[0], wdw_rows)
    pay1 = jnp.concatenate([_rows128(c[0]), _rows128(wdw_pad.reshape(-1))], axis=0)
    (g1,) = _all_gather([pay1], "gather_cond", pltpu.VMEM)
    c_rows = d // LANES
    c_all = g1[:, :c_rows].reshape(NDEV, d)
    wdw_all = g1[:, c_rows:].reshape(NDEV, wdw_rows, cwl).transpose(1, 0, 2).reshape(wdw_rows, cw)

    b_ada_loc = lax.dynamic_slice(b_ada, (0, my * n_ada), (1, n_ada))
    mod_cols = _ada_matmul(c_all, w_ada[0], b_ada_loc)
    (g2,) = _all_gather([mod_cols], "gather_mod", pltpu.VMEM)
    mod_mine = lax.dynamic_index_in_dim(g2, my, axis=1, keepdims=False)
    mod = mod_mine.reshape(3, d)

    core = lax.axis_index("c").astype(jnp.int32).reshape(1)
    h = _modulated_norm(x2, norm_g, mod)
    wfull_in, proj = _gather_proj(h, w_in[0].astype(BF16))
    (o, tot, ycat, firsts), (partly,) = _attention_fwd(
        proj, q_norm_g, k_norm_g, nh, d, _gather_ride([w_out[0].astype(BF16), w_pw[0].astype(BF16)]))
    (a, hc), ((wg_out, wg_pw),) = _conv_fwd(proj, wdw_all, b_dw, ln_g, ln_b, kc, cw, _gather_finish_ride(partly))
    wfull_out, wfull_pw = wg_out.reshape(d, d), wg_pw.reshape(cw, cw)
    z, ycat = _pointwise_fwd(a, wfull_pw, b_pw, proj, ycat, cw)
    dout, dy, out_sums = _out_matmul(ycat, wfull_out, x2, tg2, mod)

    tile = _pick(t, 512, 16)
    (dycat,), _ = _mm(dy, wfull_out, NT, "dycat_matmul", _pick(t, 1024, 16), _pick(d, 512), F32)
    (p_wout,), _ = _mm(ycat, dy, TN, "w_out_grad", _pick(d, 512), _pick(d, 1024), BF16)
    p_wout = p_wout.reshape(NDEV, d // NDEV, d)
    dproj, gains = _attention_bwd(proj, o, tot, firsts, dycat, q_norm_g, k_norm_g, nh)
    dz, dproj, dhc, pw_sums = _pointwise_bwd(dycat, z, proj, dproj, wfull_pw, hc, ln_g, ln_b, cw)
    (p_wpw,), _ = _mm(a, dz, TN, "w_pw_grad", _pick(cw, 512), _pick(cw, 1024), BF16)
    p_wpw = p_wpw.reshape(NDEV, cw // NDEV, cw)
    dproj, dwdw = _conv_bwd(dhc, proj, wdw_all, dproj, kc, cw)
    p_wdw = dwdw.reshape(wdw_rows, NDEV, cwl).transpose(1, 0, 2).astype(BF16)

    def pair_sums(mine, theirs, name):
        return [_pair_sum(m, s, core, f"{name}_pair_sum_{i}") for i, (m, s) in enumerate(zip(mine, theirs))]

    lesser = [p.reshape(4, 2, *p.shape[1:]) for p in (p_wout, p_wpw, p_wdw)]
    (p_theirs,), (small_theirs,) = _mm(h, dproj, TN, "w_in_grad_sibling", _pick(d, 256), ns, BF16, slabs=True,
                                       n_outer=True, every_other=1 - core, rides=[_sibling_ride(lesser)])
    q_small = pair_sums(lesser, small_theirs, "small_grads")
    (p_mine,), (win_theirs, (r_wout, r_wpw, r_wdw)) = _mm(
        h, dproj, TN, "w_in_grad_own", _pick(d, 256), ns, BF16, slabs=True, n_outer=True, every_other=core,
        rides=[_sibling_ride([p_theirs]), _chip_exchange_ride(q_small)])
    q_win = pair_sums([p_mine[:, None]], win_theirs, "w_in_grad")
    head = d - d // 8
    (dh,), ((r_head,),) = _dh_matmul(dproj, wfull_in, tile, _pick(d, 512), 2,
                                     [_chip_exchange_ride(q_win, rows=(0, head))])
    (grad_x, in_sums), ((r_win,),) = _input_grad(
        dh, x2, dout, norm_g, mod, rides=[_chip_exchange_ride(q_win, rows=(head, d - head), into=[r_head])])

    dmod = jnp.concatenate([in_sums[0], in_sums[1], out_sums[0]])
    loss_part = 0.5 / d * jnp.sum(out_sums[1].reshape(-1, LANES), axis=0)
    small = [in_sums[2], dmod, jnp.sum(gains[:, 0], axis=0), jnp.sum(gains[:, 1], axis=0),
             pw_sums[3], pw_sums[1], pw_sums[2], pw_sums[0], loss_part]
    pieces = [_rows128(s) for s in small]
    pieces = [_pad_rows(p, -(-p.shape[0] // 8) * 8) for p in pieces]
    (g3,) = _all_gather([jnp.concatenate(pieces, axis=0)], "gather_small", pltpu.VMEM)
    states = [[_rows128(s.reshape(-1)) for s in leaf] for leaf in (
        (norm_g, m_norm_g, v_norm_g), (b_ada, m_b_ada, v_b_ada), (q_norm_g, m_q_norm_g, v_q_norm_g),
        (k_norm_g, m_k_norm_g, v_k_norm_g), (b_dw, m_b_dw, v_b_dw), (ln_g, m_ln_g, v_ln_g),
        (ln_b, m_ln_b, v_ln_b), (b_pw, m_b_pw, v_b_pw))]
    small_out, loss_rows = _small_adam(g3, [p.shape[0] for p in pieces], states)
    g_small, d_small, m_small, v_small = [[leaf[k].reshape(1, -1) for leaf in small_out] for k in range(4)]
    loss = jnp.sum(loss_rows[0])

    off = pieces[0].shape[0]
    dmod_all = g3[:, off:off + pieces[1].shape[0]].reshape(NDEV, 3 * d)
    dmod_loc = lax.dynamic_slice(dmod_all, (0, my * n_ada), (NDEV, n_ada))
    ada = _ada_grad_adam(_silu_t(c_all), dmod_loc, w_ada[0], m_w_ada[0], v_w_ada[0])
    win = _sum_adam(r_win, w_in[0], m_w_in[0], v_w_in[0], "w_in_adam")
    wout = _sum_adam(r_wout, w_out[0], m_w_out[0], v_w_out[0], "w_out_adam")
    wpw = _sum_adam(r_wpw, w_pw[0], m_w_pw[0], v_w_pw[0], "w_pw_adam")
    wdw_state = [_pad_rows(s[0], wdw_rows) for s in (w_dw, m_w_dw, v_w_dw)]
    wdw = [r[:kc] for r in _sum_adam(r_wdw, *wdw_state, "w_dw_adam")]

    def group(k, small_list):
        s = small_list
        return [s[0], ada[k][None], s[1], win[k][None], s[2], s[3], wdw[k][None], s[4], s[5], s[6],
                wpw[k][None], s[7], wout[k][None]]

    return (loss, grad_x[None], *group(0, g_small), *group(1, d_small), *group(2, m_small), *group(3, v_small))
```

```python
import functools

import jax
import jax.numpy as jnp
from jax import lax
from jax.experimental import pallas as pl
from jax.experimental.pallas import tpu as pltpu

F32 = jnp.float32
BF16 = jnp.bfloat16
NDEV = 8
HEAD_DIM = 128
LANES = 128
SUBLANES = 8
HALO = 32
EPS = 1e-6
DEAD_LOG_WEIGHT = -104.0
VMEM_LIMIT = 56 * 1024 * 1024
MESH = pl.DeviceIdType.MESH

ADAM_LR = 0.001
ADAM_B1 = 0.9
ADAM_B2 = 0.999
ADAM_EPS = 1e-08
ADAM_WD = 0.01
ADAM_STEP = 10


def _params(sem=None):
    return pltpu.CompilerParams(dimension_semantics=sem, vmem_limit_bytes=VMEM_LIMIT)


def _pick(n, pref, unit=LANES):
    best = None
    for d in range(unit, min(n, pref) + 1, unit):
        if n % d == 0:
            best = d
    return best if best is not None else n


def _sigmoid(z):
    return 1.0 / (1.0 + jnp.exp(-z))


def _silu(z):
    return z * _sigmoid(z)


def _dsilu(z):
    s = _sigmoid(z)
    return s * (1.0 + z * (1.0 - s))


def _softplus(z):
    return jnp.maximum(z, 0.0) + jnp.log(1.0 + jnp.exp(-jnp.abs(z)))


def _dot(a, b, dims):
    return lax.dot_general(a, b, (dims, ((), ())), preferred_element_type=F32)


NN = ((1,), (0,))
NT = ((1,), (1,))
TN = ((0,), (0,))


def _adam(w, g, m, v):
    m = ADAM_B1 * m + (1.0 - ADAM_B1) * g
    v = ADAM_B2 * v + (1.0 - ADAM_B2) * (g * g)
    m_hat = m / (1.0 - ADAM_B1 ** ADAM_STEP)
    v_hat = v / (1.0 - ADAM_B2 ** ADAM_STEP)
    delta = -ADAM_LR * (m_hat / (jnp.sqrt(v_hat) + ADAM_EPS) + ADAM_WD * w)
    return delta, m, v


def _place():
    x, y, c = lax.axis_index("x"), lax.axis_index("y"), lax.axis_index("c")
    return x, y, c


def _flip(v, bit):
    return 1 - v if bit else v


def _all_gather(arrs, name, space):
    n = len(arrs)

    def body(*refs):
        ins, outs = refs[:n], refs[n:2 * n]
        send_sems, recv_sems, local_sems = refs[2 * n:]
        x, y, c = _place()
        me, sibling = (x, y, c), (x, y, 1 - c)
        chips = [(1 - x, y), (x, 1 - y), (1 - x, 1 - y)]

        def rows(a, p):
            return outs[a].at[4 * p[0] + 2 * p[1] + p[2]]

        def copy(a, k, block, to, src=None):
            return pltpu.make_async_remote_copy(
                src_ref=rows(a, block) if src is None else src, dst_ref=rows(a, block),
                send_sem=send_sems.at[7 * a + k], recv_sem=recv_sems.at[7 * a + k],
                device_id=to, device_id_type=MESH)

        mine = [pltpu.make_async_copy(ins[a], rows(a, me), local_sems.at[a]) for a in range(n)]
        for cp in mine:
            cp.start()
        first = []
        for a in range(n):
            first.append(copy(a, 0, me, sibling, src=ins[a]))
            first += [copy(a, 1 + j, me, (*chip, c), src=ins[a]) for j, chip in enumerate(chips)]
        for cp in first:
            cp.start()
        passed = []
        for j, chip in enumerate(chips):
            for a in range(n):
                copy(a, 1 + j, (*chip, c), me).wait_recv()
                fwd = copy(a, 4 + j, (*chip, c), sibling)
                fwd.start()
                passed.append(fwd)
        for a in range(n):
            copy(a, 0, sibling, me).wait_recv()
            for j, chip in enumerate(chips):
                copy(a, 4 + j, (*chip, 1 - c), me).wait_recv()
        for cp in first + passed:
            cp.wait_send()
        for cp in mine:
            cp.wait()

    spec = pl.BlockSpec(memory_space=space)
    return pl.pallas_call(
        body, name=name,
        out_shape=[jax.ShapeDtypeStruct((NDEV,) + a.shape, a.dtype) for a in arrs],
        in_specs=[spec] * n, out_specs=[spec] * n,
        scratch_shapes=[pltpu.SemaphoreType.DMA((7 * n,)), pltpu.SemaphoreType.DMA((7 * n,)),
                        pltpu.SemaphoreType.DMA((n,))],
        compiler_params=pltpu.CompilerParams(vmem_limit_bytes=VMEM_LIMIT),
    )(*arrs)


class _Ride:
    def __init__(self, ins, out_shapes, n_sems, start, finish, alias=None):
        self.ins, self.out_shapes, self.n_sems, self.start, self.finish = ins, out_shapes, n_sems, start, finish
        self.alias = alias or {}


def _call(body, *, name, grid, out_shape, in_specs, out_specs, args, scratch_shapes=(), rides=(), prefetch=None):
    sem = ("arbitrary",) * len(grid)
    rides = [r for r in rides if r is not None]
    n_pre = 0 if prefetch is None else 1
    n_in, n_out, n_scr = len(in_specs), len(out_specs), len(scratch_shapes)
    r_ins = [len(r.ins) for r in rides]
    r_outs = [len(r.out_shapes) for r in rides]

    def carried(*refs):
        pre, refs = refs[:n_pre], refs[n_pre:]
        ins, pos = refs[:n_in], n_in
        rins = []
        for k in r_ins:
            rins.append(refs[pos:pos + k])
            pos += k
        outs, pos = refs[pos:pos + n_out], pos + n_out
        routs = []
        for k in r_outs:
            routs.append(refs[pos:pos + k])
            pos += k
        scratch, pos = refs[pos:pos + n_scr], pos + n_scr
        sems = [refs[pos + 3 * i:pos + 3 * i + 3] for i in range(len(rides))]
        first = functools.reduce(lambda a, b: a & b, [pl.program_id(i) == 0 for i in range(len(grid))])
        last = functools.reduce(lambda a, b: a & b, [pl.program_id(i) == grid[i] - 1 for i in range(len(grid))])

        @pl.when(first)
        def _():
            for ride, ri, ro, s in zip(rides, rins, routs, sems):
                ride.start(ri, ro, *s)

        body(*pre, *ins, *outs, *scratch)

        @pl.when(last)
        def _():
            for ride, ri, ro, s in zip(rides, rins, routs, sems):
                ride.finish(ri, ro, *s)

    hbm = pl.BlockSpec(memory_space=pl.ANY)
    aliases, in_pos, out_pos = {}, n_pre + n_in, n_out
    for ride, ki, ko in zip(rides, r_ins, r_outs):
        aliases.update({in_pos + k: out_pos + o for k, o in ride.alias.items()})
        in_pos, out_pos = in_pos + ki, out_pos + ko
    all_scratch = list(scratch_shapes)
    for ride in rides:
        all_scratch += [pltpu.SemaphoreType.DMA((ride.n_sems,))] * 3
    all_in = list(in_specs) + [hbm] * sum(r_ins)
    all_out = list(out_specs) + [hbm] * sum(r_outs)
    shapes = list(out_shape) + [s for r in rides for s in r.out_shapes]
    operands = list(args) + [a for r in rides for a in r.ins]
    if prefetch is None:
        res = pl.pallas_call(
            carried, name=name, grid=grid, out_shape=shapes, in_specs=all_in, out_specs=all_out,
            scratch_shapes=all_scratch, input_output_aliases=aliases, compiler_params=_params(sem))(*operands)
    else:
        res = pl.pallas_call(
            carried, name=name, out_shape=shapes,
            grid_spec=pltpu.PrefetchScalarGridSpec(num_scalar_prefetch=1, grid=grid, in_specs=all_in,
                                                   out_specs=all_out, scratch_shapes=all_scratch),
            input_output_aliases=aliases, compiler_params=_params(sem))(prefetch, *operands)
    split, pos = [], n_out
    for k in r_outs:
        split.append(res[pos:pos + k])
        pos += k
    return res[:n_out], split


def _chips(x, y):
    return [(1 - x, y), (x, 1 - y), (1 - x, 1 - y)]


def _gather_ride(arrs):
    n = len(arrs)

    def copies(ins, outs, send_sems, recv_sems):
        x, y, c = _place()
        me = 4 * x + 2 * y + c
        peers = [(x, y, 1 - c)] + [(*chip, c) for chip in _chips(x, y)]
        return [pltpu.make_async_remote_copy(
            src_ref=ins[a], dst_ref=outs[a].at[me], send_sem=send_sems.at[4 * a + k], recv_sem=recv_sems.at[4 * a + k],
            device_id=p, device_id_type=MESH) for a in range(n) for k, p in enumerate(peers)], me

    def start(ins, outs, send_sems, recv_sems, local_sems):
        cps, me = copies(ins, outs, send_sems, recv_sems)
        for a in range(n):
            pltpu.make_async_copy(ins[a], outs[a].at[me], local_sems.at[a]).start()
        for cp in cps:
            cp.start()

    def finish(ins, outs, send_sems, recv_sems, local_sems):
        cps, me = copies(ins, outs, send_sems, recv_sems)
        for cp in cps:
            cp.wait_recv()
        for cp in cps:
            cp.wait_send()
        for a in range(n):
            pltpu.make_async_copy(ins[a], outs[a].at[me], local_sems.at[a]).wait()

    return _Ride(arrs, [jax.ShapeDtypeStruct((NDEV,) + a.shape, a.dtype) for a in arrs], 4 * n, start, finish)


def _gather_finish_ride(arrs):
    n = len(arrs)

    def copies(outs, send_sems, recv_sems):
        x, y, c = _place()
        cps = []
        for a in range(n):
            for k, chip in enumerate(_chips(x, y)):
                blk = 4 * chip[0] + 2 * chip[1]
                cps.append((pltpu.make_async_remote_copy(
                    src_ref=outs[a].at[blk + c], dst_ref=outs[a].at[blk + c],
                    send_sem=send_sems.at[3 * a + k], recv_sem=recv_sems.at[3 * a + k],
                    device_id=(x, y, 1 - c), device_id_type=MESH),
                    pltpu.make_async_remote_copy(
                    src_ref=outs[a].at[blk + 1 - c], dst_ref=outs[a].at[blk + 1 - c],
                    send_sem=send_sems.at[3 * a + k], recv_sem=recv_sems.at[3 * a + k],
                    device_id=(x, y, 1 - c), device_id_type=MESH)))
        return cps

    def start(ins, outs, send_sems, recv_sems, local_sems):
        for send, _ in copies(outs, send_sems, recv_sems):
            send.start()

    def finish(ins, outs, send_sems, recv_sems, local_sems):
        cps = copies(outs, send_sems, recv_sems)
        for _, recv in cps:
            recv.wait_recv()
        for send, _ in cps:
            send.wait_send()

    return _Ride(arrs, [jax.ShapeDtypeStruct(a.shape, a.dtype) for a in arrs], 3 * n, start, finish,
                 alias={a: a for a in range(n)})


def _sibling_ride(arrs):
    n = len(arrs)

    def copies(ins, outs, send_sems, recv_sems):
        x, y, c = _place()
        return [pltpu.make_async_remote_copy(
            src_ref=ins[a].at[:, 1 - c] if arrs[a].ndim == 4 else ins[a], dst_ref=outs[a],
            send_sem=send_sems.at[a], recv_sem=recv_sems.at[a],
            device_id=(x, y, 1 - c), device_id_type=MESH) for a in range(n)]

    def start(ins, outs, send_sems, recv_sems, local_sems):
        for cp in copies(ins, outs, send_sems, recv_sems):
            cp.start()

    def finish(ins, outs, send_sems, recv_sems, local_sems):
        for cp in copies(ins, outs, send_sems, recv_sems):
            cp.wait()

    return _Ride(arrs, [jax.ShapeDtypeStruct((4,) + a.shape[-2:], a.dtype) for a in arrs], n, start, finish)


def _pair_sum(mine, theirs, core, name):
    _, pick, r, c = mine.shape
    if pick == 1:
        core = jnp.zeros_like(core)
    tr = _pick(r, 512, 16)

    def body(core_ref, a_ref, b_ref, o_ref):
        o_ref[...] = (a_ref[...].astype(F32) + b_ref[...].astype(F32)).astype(BF16)

    return pl.pallas_call(
        body, name=name,
        grid_spec=pltpu.PrefetchScalarGridSpec(
            num_scalar_prefetch=1, grid=(4, r // tr),
            in_specs=[pl.BlockSpec((None, None, tr, c), lambda i, k, core_ref: (i, core_ref[0], k, 0)),
                      pl.BlockSpec((None, tr, c), lambda i, k, core_ref: (i, k, 0))],
            out_specs=pl.BlockSpec((None, tr, c), lambda i, k, core_ref: (i, k, 0))),
        out_shape=jax.ShapeDtypeStruct((4, r, c), BF16),
        compiler_params=_params(("arbitrary", "arbitrary")),
    )(core, mine, theirs)


def _chip_exchange_ride(arrs, rows=None, into=None):
    n = len(arrs)

    def part(ref):
        return ref if rows is None else ref.at[pl.ds(rows[0], rows[1]), :]

    def copies(ins, outs, send_sems, recv_sems):
        x, y, c = _place()
        mine = 2 * x + y
        return [pltpu.make_async_remote_copy(
            src_ref=part(ins[a].at[2 * chip[0] + chip[1]]), dst_ref=part(outs[a].at[mine]),
            send_sem=send_sems.at[3 * a + k], recv_sem=recv_sems.at[3 * a + k],
            device_id=(*chip, c), device_id_type=MESH) for a in range(n) for k, chip in enumerate(_chips(x, y))], mine

    def start(ins, outs, send_sems, recv_sems, local_sems):
        cps, mine = copies(ins, outs, send_sems, recv_sems)
        if into is None:
            for a in range(n):
                pltpu.make_async_copy(ins[a].at[mine], outs[a].at[mine], local_sems.at[a]).start()
        for cp in cps:
            cp.start()

    def finish(ins, outs, send_sems, recv_sems, local_sems):
        cps, mine = copies(ins, outs, send_sems, recv_sems)
        for cp in cps:
            cp.wait_recv()
        for cp in cps:
            cp.wait_send()
        if into is None:
            for a in range(n):
                pltpu.make_async_copy(ins[a].at[mine], outs[a].at[mine], local_sems.at[a]).wait()

    shapes = [jax.ShapeDtypeStruct(a.shape, a.dtype) for a in arrs]
    if into is None:
        return _Ride(arrs, shapes, 3 * n, start, finish)
    return _Ride(list(arrs) + list(into), shapes, 3 * n, start, finish, alias={n + a: a for a in range(n)})


def _ada_matmul(c_all, w_loc, b_loc):
    d, n = w_loc.shape
    bn = _pick(n, 512)

    def body(c_ref, w_ref, b_ref, o_ref):
        s = _silu(c_ref[...]).astype(BF16)
        o_ref[...] = _dot(s, w_ref[...].astype(BF16), NN) + b_ref[...]

    return pl.pallas_call(
        body, name="ada_matmul", grid=(n // bn,),
        out_shape=jax.ShapeDtypeStruct((NDEV, n), F32),
        in_specs=[pl.BlockSpec((NDEV, d), lambda j: (0, 0)), pl.BlockSpec((d, bn), lambda j: (0, j)),
                  pl.BlockSpec((1, bn), lambda j: (0, j))],
        out_specs=pl.BlockSpec((NDEV, bn), lambda j: (0, j)),
        compiler_params=_params(("arbitrary",)),
    )(c_all, w_loc, b_loc)


def _modulated_norm(x, norm_g, mod):
    t, d = x.shape
    tt = _pick(t, 256, 16)

    def body(x_ref, g_ref, mod_ref, h_ref):
        xv = x_ref[...]
        r = lax.rsqrt(jnp.mean(xv * xv, axis=-1, keepdims=True) + EPS)
        h = (xv * r) * g_ref[...] * (1.0 + mod_ref[1:2, :]) + mod_ref[0:1, :]
        h_ref[...] = h.astype(BF16)

    return pl.pallas_call(
        body, name="modulated_norm", grid=(t // tt,),
        out_shape=jax.ShapeDtypeStruct((t, d), BF16),
        in_specs=[pl.BlockSpec((tt, d), lambda i: (i, 0)), pl.BlockSpec((1, d), lambda i: (0, 0)),
                  pl.BlockSpec((3, d), lambda i: (0, 0))],
        out_specs=pl.BlockSpec((tt, d), lambda i: (i, 0)),
        compiler_params=_params(("arbitrary",)),
    )(x, norm_g, mod)


def _gather_proj(h, w_loc):
    t, d = h.shape
    ns = w_loc.shape[1]
    tm = _pick(t, 512, 16)
    nm = t // tm
    idx = lambda p: 4 * p[0] + 2 * p[1] + p[2]

    def peers():
        x, y, c = _place()
        flip = lambda a, b: a + b - 2 * a * b
        near, far = (flip(x, c), flip(y, 1 - c)), (flip(x, 1 - c), flip(y, c))
        return (x, y, c), (x, y, 1 - c), near, far, (1 - x, 1 - y), c

    me, sibling, near, far, diag, c = peers()
    order = [me, sibling, (*near, c), (*far, 1 - c), (*far, c), (*near, 1 - c), (*diag, c), (*diag, 1 - c)]
    order = jnp.stack([idx(p) for p in order]).astype(jnp.int32)

    def body(order_ref, a_ref, w_ref, wg_ref, o_ref, slab, send_sems, recv_sems, local_sems):
        j, m = pl.program_id(0), pl.program_id(1)
        me, sibling, near, far, diag, c = peers()

        def rows(p):
            return wg_ref.at[:, pl.ds(pl.multiple_of(idx(p) * ns, LANES), ns)]

        def copy(k, block, to, src=None):
            return pltpu.make_async_remote_copy(
                src_ref=rows(block) if src is None else src, dst_ref=rows(block),
                send_sem=send_sems.at[k], recv_sem=recv_sems.at[k], device_id=to, device_id_type=MESH)

        def load(src):
            cp = pltpu.make_async_copy(src, slab, local_sems.at[1])
            cp.start()
            cp.wait()

        keep = pltpu.make_async_copy(w_ref, rows(me), local_sems.at[0])
        own = [copy(0, me, sibling, src=w_ref), copy(1, me, (*near, c), src=w_ref), copy(2, me, (*far, c), src=w_ref)]
        relay = copy(3, (*near, c), (*far, c))
        passed = [copy(4, (*near, c), sibling), copy(5, (*far, c), sibling), copy(6, (*diag, c), sibling)]
        arrivals = [(1, 0, sibling, []), (2, 1, (*near, c), [passed[0], own[2], relay]), (3, 4, (*far, 1 - c), []),
                    (4, 2, (*far, c), [passed[1]]), (5, 5, (*near, 1 - c), []),
                    (6, 3, (*diag, c), [passed[2]]), (7, 6, (*diag, 1 - c), [])]

        @pl.when((j == 0) & (m == 0))
        def _():
            keep.start()
            for cp in own[:2]:
                cp.start()
            load(w_ref)

        for step, sem, block, onward in arrivals:
            @pl.when((j == step) & (m == 0))
            def _(sem=sem, block=block, onward=onward):
                copy(sem, block, me).wait_recv()
                for cp in onward:
                    cp.start()
                load(rows(block))

        o_ref[...] = _dot(a_ref[...], slab[...], NN)

        @pl.when((j == NDEV - 1) & (m == nm - 1))
        def _():
            for cp in own + [relay] + passed:
                cp.wait_send()
            keep.wait()

    hbm = pl.BlockSpec(memory_space=pl.ANY)
    return pl.pallas_call(
        body, name="gather_proj",
        grid_spec=pltpu.PrefetchScalarGridSpec(
            num_scalar_prefetch=1, grid=(NDEV, nm),
            in_specs=[pl.BlockSpec((tm, d), lambda j, m, order_ref: (m, 0)), hbm],
            out_specs=[hbm, pl.BlockSpec((tm, ns), lambda j, m, order_ref: (m, order_ref[j]))],
            scratch_shapes=[pltpu.VMEM((d, ns), BF16), pltpu.SemaphoreType.DMA((7,)), pltpu.SemaphoreType.DMA((7,)),
                            pltpu.SemaphoreType.DMA((2,))]),
        out_shape=[jax.ShapeDtypeStruct((d, NDEV * ns), BF16), jax.ShapeDtypeStruct((t, NDEV * ns), F32)],
        compiler_params=_params(("arbitrary", "arbitrary")),
    )(order, h, w_loc)


def _attention_fwd(proj, qg, kg, nh, d_model, ride):
    t = proj.shape[0]
    tq = _pick(t, 256, 16)
    nq = t // tq
    assert 2 <= nq <= LANES
    group = 4 if nq % 4 == 0 else 2 if nq % 2 == 0 else 1
    scale = HEAD_DIM ** -0.5

    def body(q_ref, k_ref, v_ref, g_ref, qg_ref, kg_ref, o_ref, tot_ref, y_ref, first_ref, qn, kn, vb):
        def norm(src, gain, dst):
            v = src[...]
            r = lax.rsqrt(jnp.mean(v * v, axis=-1, keepdims=True) + EPS)
            dst[...] = ((v * r) * gain[...]).astype(BF16)

        norm(q_ref, qg_ref, qn)
        norm(k_ref, kg_ref, kn)
        vb[...] = v_ref[...].astype(BF16)
        def after_matrix(n):
            return (lax.broadcasted_iota(jnp.int32, (n, n), 0) > lax.broadcasted_iota(jnp.int32, (n, n), 1)).astype(BF16)

        upper = {tq: after_matrix(tq), 2 * tq: after_matrix(2 * tq)}
        key_minus_query = (lax.broadcasted_iota(jnp.int32, (tq, 2 * tq), 1)
                           - lax.broadcasted_iota(jnp.int32, (tq, 2 * tq), 0))

        def block(qi, start, width, carry, acc, q_start=None):
            ks = pl.ds(pl.multiple_of(start, tq), width)
            z = _dot(qi, kn[ks, :], NT) * scale
            sp = _softplus(z)
            ls = -sp
            if q_start is not None:
                causal = key_minus_query < q_start - start
                ls = jnp.where(causal, ls, 0.0)
            hi = ls.astype(BF16)
            lo = (ls - hi.astype(F32)).astype(BF16)
            after = _dot(hi, upper[width], NN) + _dot(lo, upper[width], NN)
            w = jnp.exp(z - sp + after + carry)
            if q_start is not None:
                w = jnp.where(causal, w, 0.0)
            acc = acc + _dot(w.astype(BF16), vb[ks, :], NN)
            carry = carry + jnp.sum(ls, axis=1, keepdims=True)
            return carry, acc

        lane = lax.broadcasted_iota(jnp.int32, (8, LANES), 1)

        def live(carry):
            return (jnp.max(carry) > DEAD_LOG_WEIGHT).astype(jnp.int32)

        def wide_step(i):
            qi = qn[pl.ds(pl.multiple_of(i * tq, tq), tq), :]
            left = jnp.maximum(i - 1, 0)
            return block(qi, left * tq, 2 * tq, jnp.zeros((tq, 1), F32), jnp.zeros((tq, HEAD_DIM), F32), i * tq)

        def finish(i, carry, acc, firsts):
            qs = pl.ds(pl.multiple_of(i * tq, tq), tq)
            qi = qn[qs, :]
            left = jnp.maximum(i - 1, 0)

            def k_step(st):
                ca, ac = block(qi, (left - 1 - st[0]) * tq, tq, st[1], st[2])
                return st[0] + 1, ca, ac, live(ca)

            done, carry, acc, _ = lax.while_loop(
                lambda st: (st[0] < left) & (st[3] > 0), k_step, (jnp.int32(0), carry, acc, live(carry)))
            o_ref[qs, :] = acc
            tot_ref[qs, :] = jnp.broadcast_to(carry, (tq, HEAD_DIM))
            y_ref[qs, :] = (acc * _silu(g_ref[qs, :])).astype(BF16)
            return jnp.where(lane == i, (left - done).astype(F32), firsts)

        def q_group(p, firsts):
            blocks = [p + b * (nq // group) for b in range(group)]
            swept = [wide_step(i) for i in blocks]
            for i, (carry, acc) in zip(blocks, swept):
                firsts = finish(i, carry, acc, firsts)
            return firsts

        first_ref[...] = lax.fori_loop(0, nq // group, q_group, jnp.zeros((8, LANES), F32))

    col_block = lambda off: pl.BlockSpec((t, HEAD_DIM), lambda h: (0, off + h))
    vec = pl.BlockSpec((1, HEAD_DIM), lambda h: (0, 0))
    return _call(
        body, name="attention_fwd", grid=(nh,),
        out_shape=[jax.ShapeDtypeStruct((t, nh * HEAD_DIM), F32), jax.ShapeDtypeStruct((t, nh * HEAD_DIM), F32),
                   jax.ShapeDtypeStruct((t, d_model), BF16), jax.ShapeDtypeStruct((nh, 8, LANES), F32)],
        in_specs=[col_block(0), col_block(nh), col_block(2 * nh), col_block(3 * nh), vec, vec],
        out_specs=[col_block(0), col_block(0), col_block(0), pl.BlockSpec((None, 8, LANES), lambda h: (h, 0, 0))],
        scratch_shapes=[pltpu.VMEM((t, HEAD_DIM), BF16)] * 3,
        args=(proj, proj, proj, proj, qg, kg), rides=[ride])


def _fill_shifts(shifted, n_rows):
    for b in range(1, SUBLANES):
        shifted[b, pl.ds(0, n_rows), :] = shifted[0, pl.ds(b, n_rows), :]


def _shifted_rows(shifted, offset, n_rows, lo, cw):
    a, b = divmod(offset, SUBLANES)
    return shifted[b, pl.ds(SUBLANES * a, n_rows), pl.ds(lo, cw)]


def _conv_taps(shifted, w_ref, kc, lo, n_rows, cw, first_tap_row):
    acc = None
    for j in range(kc):
        term = w_ref[j:j + 1, lo:lo + cw] * _shifted_rows(shifted, first_tap_row(j), n_rows, lo, cw)
        acc = term if acc is None else acc + term
    return acc


def _glu_rows(u_ref, g_ref):
    return u_ref[...] * _sigmoid(g_ref[...])


def _conv_fwd(proj, w_dw, b_dw, ln_g, ln_b, kc, cw, ride):
    t = proj.shape[0]
    tt = _pick(t, 128, HALO)
    per = tt // HALO
    chunk = _pick(cw, 256)

    def body(u_ref, g_ref, up_ref, gp_ref, w_ref, b_ref, lg_ref, lb_ref, a_ref, hc_ref, buf):
        i = pl.program_id(0)
        buf[0, pl.ds(HALO, tt), :] = _glu_rows(u_ref, g_ref)
        halo = _glu_rows(up_ref, gp_ref)
        buf[0, pl.ds(0, HALO), :] = jnp.where(i > 0, halo, 0.0)
        _fill_shifts(buf, tt + HALO - SUBLANES)
        for lo in range(0, cw, chunk):
            conv = _conv_taps(buf, w_ref, kc, lo, tt, chunk, lambda j: HALO - (kc - 1) + j)
            hc_ref[:, lo:lo + chunk] = conv + b_ref[:, lo:lo + chunk]
        hc = hc_ref[...]
        mu = jnp.mean(hc, axis=-1, keepdims=True)
        xc = hc - mu
        var = jnp.mean(xc * xc, axis=-1, keepdims=True)
        ln = xc * lax.rsqrt(var + EPS) * lg_ref[...] + lb_ref[...]
        a_ref[...] = _silu(ln).astype(BF16)

    ncol = proj.shape[1] // cw
    tile = lambda g: pl.BlockSpec((tt, cw), lambda i: (i, g))
    prev = lambda g: pl.BlockSpec((HALO, cw), lambda i: (jnp.maximum(i * per - 1, 0), g))
    full = lambda r: pl.BlockSpec((r, cw), lambda i: (0, 0))
    return _call(
        body, name="conv_fwd", grid=(t // tt,),
        out_shape=[jax.ShapeDtypeStruct((t, cw), BF16), jax.ShapeDtypeStruct((t, cw), F32)],
        in_specs=[tile(ncol - 3), tile(ncol - 2), prev(ncol - 3), prev(ncol - 2),
                  full(w_dw.shape[0]), full(1), full(1), full(1)],
        out_specs=[pl.BlockSpec((tt, cw), lambda i: (i, 0))] * 2,
        scratch_shapes=[pltpu.VMEM((SUBLANES, HALO + tt, cw), F32)],
        args=(proj, proj, proj, proj, w_dw, b_dw, ln_g, ln_b), rides=[ride])


def _pointwise_fwd(a, wpw, b_pw, proj, ycat, cw):
    t = a.shape[0]
    tm = _pick(t, 256, 16)
    ncol = proj.shape[1] // cw
    ycol = ycat.shape[1] // cw - 1

    def body(a_ref, w_ref, b_ref, g_ref, y_in, z_ref, y_ref):
        z = _dot(a_ref[...], w_ref[...], NN) + b_ref[...]
        z_ref[...] = z
        y_ref[...] = (z * _silu(g_ref[...])).astype(BF16)

    return pl.pallas_call(
        body, name="pointwise_fwd", grid=(t // tm,),
        out_shape=[jax.ShapeDtypeStruct((t, cw), F32), jax.ShapeDtypeStruct(ycat.shape, ycat.dtype)],
        in_specs=[pl.BlockSpec((tm, cw), lambda m: (m, 0)),
                  pl.BlockSpec((cw, cw), lambda m: (0, 0)),
                  pl.BlockSpec((1, cw), lambda m: (0, 0)),
                  pl.BlockSpec((tm, cw), lambda m: (m, ncol - 1)),
                  pl.BlockSpec(memory_space=pl.ANY)],
        out_specs=[pl.BlockSpec((tm, cw), lambda m: (m, 0)), pl.BlockSpec((tm, cw), lambda m: (m, ycol))],
        input_output_aliases={4: 1},
        compiler_params=_params(("arbitrary",)),
    )(a, wpw, b_pw, proj, ycat)


def _out_matmul(ycat, wout, x, target, mod):
    t, d = x.shape
    kdim = wout.shape[0]
    tm, tn = _pick(t, 512, 16), _pick(d, 1024)
    inv_d = 1.0 / d

    def body(a_ref, w_ref, x_ref, tg_ref, mod_ref, dout_ref, dy_ref, sums_ref):
        @pl.when(pl.program_id(1) == 0)
        def _():
            sums_ref[...] = jnp.zeros_like(sums_ref)

        y = _dot(a_ref[...], w_ref[...], NN)
        gate = mod_ref[2:3, :]
        err = (x_ref[...] + gate * y) - tg_ref[...]
        dout = err * inv_d
        dout_ref[...] = dout
        dy_ref[...] = (dout * gate).astype(BF16)
        sums_ref[0:1, :] += jnp.sum(dout * y, axis=0, keepdims=True)
        sums_ref[1:2, :] += jnp.sum(err * err, axis=0, keepdims=True)

    mn = lambda n, m: (m, n)
    return pl.pallas_call(
        body, name="out_matmul", grid=(d // tn, t // tm),
        out_shape=[jax.ShapeDtypeStruct((t, d), F32), jax.ShapeDtypeStruct((t, d), BF16),
                   jax.ShapeDtypeStruct((8, d), F32)],
        in_specs=[pl.BlockSpec((tm, kdim), lambda n, m: (m, 0)),
                  pl.BlockSpec((kdim, tn), lambda n, m: (0, n)),
                  pl.BlockSpec((tm, tn), mn), pl.BlockSpec((tm, tn), mn),
                  pl.BlockSpec((3, tn), lambda n, m: (0, n))],
        out_specs=[pl.BlockSpec((tm, tn), mn), pl.BlockSpec((tm, tn), mn),
                   pl.BlockSpec((8, tn), lambda n, m: (0, n))],
        compiler_params=_params(("arbitrary", "arbitrary")),
    )(ycat, wout, x, target, mod)


def _mm(a, b, form, name, tm, tn, out_dtype, *, slabs=False, n_outer=False, ksplit=1, every_other=None, rides=()):
    if form == TN:
        kdim, m_dim = a.shape
    else:
        m_dim, kdim = a.shape
    n_dim = (b.shape[0] if form == NT else b.shape[1]) // (1 if every_other is None else 2)
    tk = kdim // ksplit
    gm, gn = m_dim // tm, n_dim // tn
    mn = (lambda g: (g[1], g[0])) if n_outer else (lambda g: (g[0], g[1]))
    b_col = (lambda g: mn(g)[1]) if every_other is None else (lambda g: 2 * mn(g)[1] + g[3][0])
    a_map = (lambda *g: (g[2], mn(g)[0])) if form == TN else (lambda *g: (mn(g)[0], g[2]))
    b_map = (lambda *g: (b_col(g), g[2])) if form == NT else (lambda *g: (g[2], b_col(g)))
    a_blk = (tk, tm) if form == TN else (tm, tk)
    b_blk = (tn, tk) if form == NT else (tk, tn)
    if slabs:
        out_shape = jax.ShapeDtypeStruct((gn, m_dim, tn), out_dtype)
        out_spec = pl.BlockSpec((None, tm, tn), lambda *g: (mn(g)[1], mn(g)[0], 0))
    else:
        out_shape = jax.ShapeDtypeStruct((m_dim, n_dim), out_dtype)
        out_spec = pl.BlockSpec((tm, tn), lambda *g: mn(g))

    def body(*refs):
        a_ref, b_ref, o_ref, *acc = refs if every_other is None else refs[1:]
        part = _dot(a_ref[...], b_ref[...], form)
        if ksplit == 1:
            o_ref[...] = part.astype(out_dtype)
            return
        k = pl.program_id(2)

        @pl.when(k == 0)
        def _():
            acc[0][...] = part

        @pl.when((k > 0) & (k < ksplit - 1))
        def _():
            acc[0][...] += part

        @pl.when(k == ksplit - 1)
        def _():
            o_ref[...] = (acc[0][...] + part).astype(out_dtype)

    return _call(
        body, name=name, grid=((gn, gm) if n_outer else (gm, gn)) + (ksplit,),
        out_shape=[out_shape], in_specs=[pl.BlockSpec(a_blk, a_map), pl.BlockSpec(b_blk, b_map)],
        out_specs=[out_spec], scratch_shapes=[pltpu.VMEM((tm, tn), F32)] if ksplit > 1 else [],
        args=(a, b), rides=rides, prefetch=every_other)


def _dh_matmul(dproj, w, tm, tn, ksplit, rides):
    t, kdim = dproj.shape
    d = w.shape[0]
    tk = kdim // ksplit

    def body(a_ref, b_ref, o_ref):
        k, n = pl.program_id(1), pl.program_id(2)
        cols = pl.ds(pl.multiple_of(n * tn, tn), tn)
        part = _dot(a_ref[...], b_ref[...], NT)

        @pl.when(k == 0)
        def _():
            o_ref[:, cols] = part

        @pl.when(k > 0)
        def _():
            o_ref[:, cols] += part

    return _call(
        body, name="dh_matmul", grid=(t // tm, ksplit, d // tn),
        out_shape=[jax.ShapeDtypeStruct((t, d), F32)],
        in_specs=[pl.BlockSpec((tm, tk), lambda m, k, n: (m, k)), pl.BlockSpec((tn, tk), lambda m, k, n: (n, k))],
        out_specs=[pl.BlockSpec((tm, d), lambda m, k, n: (m, 0))],
        args=(dproj, w), rides=rides)


def _attention_bwd(proj, o, tot, firsts, dycat, qg, kg, nh):
    t, in_cols = proj.shape
    tq = _pick(t, 256, 16)
    nq = t // tq
    group = 4 if nq % 4 == 0 else 2 if nq % 2 == 0 else 1
    scale = HEAD_DIM ** -0.5

    def body(q_ref, k_ref, v_ref, g_ref, o_ref, tot_ref, first_ref, dy_ref, qg_ref, kg_ref, dproj_ref, gains_ref,
             qn, kn, vb, dob, dk_acc, dv_acc, dq_acc, outs, sems):
        h = pl.program_id(0)

        def norm(src, gain, dst):
            v = src[...]
            r = lax.rsqrt(jnp.mean(v * v, axis=-1, keepdims=True) + EPS)
            dst[...] = ((v * r) * gain[...]).astype(BF16)

        norm(q_ref, qg_ref, qn)
        norm(k_ref, kg_ref, kn)
        vb[...] = v_ref[...].astype(BF16)
        gs = g_ref[...]
        dyv = dy_ref[...]
        dob[...] = (dyv * _silu(gs)).astype(BF16)
        outs[3] = (dyv * o_ref[...] * _dsilu(gs)).astype(BF16)
        dk_acc[...] = jnp.zeros_like(dk_acc)
        dv_acc[...] = jnp.zeros_like(dv_acc)

        def before_matrix(n):
            r = lax.broadcasted_iota(jnp.int32, (n, n), 0)
            c = lax.broadcasted_iota(jnp.int32, (n, n), 1)
            return (c <= r).astype(BF16)

        incl = {n: before_matrix(n) for n in (tq, 2 * tq)}
        lane = lax.broadcasted_iota(jnp.int32, (1, LANES), 1)
        key_minus_query = (lax.broadcasted_iota(jnp.int32, (2 * tq, tq), 0)
                           - lax.broadcasted_iota(jnp.int32, (2 * tq, tq), 1))

        def block(start, width, qi, doi, tot_row, p_left, g_left, dq, q_start=None):
            ks = pl.ds(pl.multiple_of(start, tq), width)
            kj = kn[ks, :]
            z = _dot(kj, qi, NT) * scale
            sp = _softplus(z)
            ls = -sp
            if q_start is not None:
                causal = key_minus_query < q_start - start
                ls = jnp.where(causal, ls, 0.0)
            hi = ls.astype(BF16)
            lo = (ls - hi.astype(F32)).astype(BF16)
            p_inc = _dot(incl[width], hi, NN) + _dot(incl[width], lo, NN) + p_left
            beta = jnp.exp(z - sp)
            w = beta * jnp.exp(tot_row - p_inc)
            if q_start is not None:
                w = jnp.where(causal, w, 0.0)
            dw = _dot(vb[ks, :], doi, NT)
            g = w * dw
            g_upto = _dot(incl[width], g.astype(BF16), NN) + g_left
            dz = g - beta * g_upto
            if q_start is not None:
                dz = jnp.where(causal, dz, 0.0)
            dzb = dz.astype(BF16)
            dv_acc[ks, :] += _dot(w.astype(BF16), doi, NN)
            dk_acc[ks, :] += _dot(dzb, qi, NN)
            dq = dq + _dot(dzb, kj, TN)
            p_left = p_left + jnp.sum(ls, axis=0, keepdims=True)
            g_left = g_left + jnp.sum(g, axis=0, keepdims=True)
            return p_left, g_left, dq

        def operands(i):
            qs = pl.ds(pl.multiple_of(i * tq, tq), tq)
            return qn[qs, :], dob[qs, :], jnp.transpose(tot_ref[qs, :])[0:1, :]

        def singles(i):
            qi, doi, tot_row = operands(i)
            zero_row = jnp.zeros((1, tq), F32)

            def k_step(j, carry):
                return block(j * tq, tq, qi, doi, tot_row, carry[0], carry[1], carry[2])

            left = jnp.maximum(i - 1, 0)
            first = jnp.sum(jnp.where(lane == i, first_ref[0:1, :], 0.0)).astype(jnp.int32)
            first = jnp.clip(first, 0, left)
            return lax.fori_loop(first, left, k_step, (zero_row, zero_row, jnp.zeros((tq, HEAD_DIM), F32)))

        def wide_step(i, carry):
            qi, doi, tot_row = operands(i)
            left = jnp.maximum(i - 1, 0)
            _, _, dq = block(left * tq, 2 * tq, qi, doi, tot_row, carry[0], carry[1], carry[2], i * tq)
            dq_acc[pl.ds(pl.multiple_of(i * tq, tq), tq), :] = dq * scale

        def q_group(p, _):
            blocks = [p + b * (nq // group) for b in range(group)]
            carries = [singles(i) for i in blocks]
            for i, carry in zip(blocks, carries):
                wide_step(i, carry)
            return 0

        lax.fori_loop(0, nq // group, q_group, 0)

        def norm_bwd(src, gain, dn, slot, gain_row):
            v = src[...]
            r = lax.rsqrt(jnp.mean(v * v, axis=-1, keepdims=True) + EPS)
            vhat = v * r
            gains_ref[gain_row:gain_row + 1, :] = jnp.sum(dn * vhat, axis=0, keepdims=True)
            dhat = dn * gain[...]
            outs[slot] = (r * (dhat - vhat * jnp.mean(dhat * vhat, axis=-1, keepdims=True))).astype(BF16)

        gains_ref[...] = jnp.zeros_like(gains_ref)
        norm_bwd(q_ref, qg_ref, dq_acc[...], 0, 0)
        norm_bwd(k_ref, kg_ref, dk_acc[...] * scale, 1, 1)
        outs[2] = dv_acc[...].astype(BF16)
        copies = [pltpu.make_async_copy(
            outs.at[s], dproj_ref.at[:, pl.ds(pl.multiple_of((s * nh + h) * HEAD_DIM, HEAD_DIM), HEAD_DIM)], sems.at[s])
            for s in range(4)]
        for cp in copies:
            cp.start()
        for cp in copies:
            cp.wait()

    col_block = lambda off: pl.BlockSpec((t, HEAD_DIM), lambda h: (0, off + h))
    vec = pl.BlockSpec((1, HEAD_DIM), lambda h: (0, 0))
    head_scr = lambda dt: pltpu.VMEM((t, HEAD_DIM), dt)
    return pl.pallas_call(
        body, name="attention_bwd", grid=(nh,),
        out_shape=[jax.ShapeDtypeStruct((t, in_cols), BF16), jax.ShapeDtypeStruct((nh, 8, HEAD_DIM), F32)],
        in_specs=[col_block(0), col_block(nh), col_block(2 * nh), col_block(3 * nh),
                  col_block(0), col_block(0), pl.BlockSpec((None, 8, LANES), lambda h: (h, 0, 0)), col_block(0), vec, vec],
        out_specs=[pl.BlockSpec(memory_space=pl.ANY), pl.BlockSpec((None, 8, HEAD_DIM), lambda h: (h, 0, 0))],
        scratch_shapes=[head_scr(BF16), head_scr(BF16), head_scr(BF16), head_scr(BF16),
                        head_scr(F32), head_scr(F32), head_scr(F32),
                        pltpu.VMEM((4, t, HEAD_DIM), BF16), pltpu.SemaphoreType.DMA((4,))],
        compiler_params=_params(("arbitrary",)),
    )(proj, proj, proj, proj, o, tot, firsts, dycat, qg, kg)


def _pointwise_bwd(dycat, z, proj, dproj, wpw, hc, ln_g, ln_b, cw):
    t = z.shape[0]
    tt = _pick(t, 256, 16)
    ncol = proj.shape[1] // cw
    ycol = dycat.shape[1] // cw - 1

    def body(dy_ref, z_ref, g_ref, w_ref, hc_ref, lg_ref, lb_ref, dp_in, dz_ref, dp_ref, dh_ref, sums_ref):
        i = pl.program_id(0)

        @pl.when(i == 0)
        def _():
            sums_ref[...] = jnp.zeros_like(sums_ref)

        g = g_ref[...]
        dy = dy_ref[...]
        dz = dy * _silu(g)
        dzb = dz.astype(BF16)
        dz_ref[...] = dzb
        dp_ref[...] = (dy * z_ref[...] * _dsilu(g)).astype(BF16)
        da = _dot(dzb, w_ref[...], NT)
        hcv = hc_ref[...]
        mu = jnp.mean(hcv, axis=-1, keepdims=True)
        xc = hcv - mu
        r = lax.rsqrt(jnp.mean(xc * xc, axis=-1, keepdims=True) + EPS)
        xhat = xc * r
        ln = xhat * lg_ref[...] + lb_ref[...]
        dln = da * _dsilu(ln)
        dxhat = dln * lg_ref[...]
        dhc = r * (dxhat - jnp.mean(dxhat, axis=-1, keepdims=True)
                   - xhat * jnp.mean(dxhat * xhat, axis=-1, keepdims=True))
        dh_ref[...] = dhc
        sums_ref[0:1, :] += jnp.sum(dz, axis=0, keepdims=True)
        sums_ref[1:2, :] += jnp.sum(dln * xhat, axis=0, keepdims=True)
        sums_ref[2:3, :] += jnp.sum(dln, axis=0, keepdims=True)
        sums_ref[3:4, :] += jnp.sum(dhc, axis=0, keepdims=True)

    tile = lambda col: pl.BlockSpec((tt, cw), lambda i: (i, col))
    vec = pl.BlockSpec((1, cw), lambda i: (0, 0))
    return pl.pallas_call(
        body, name="pointwise_bwd", grid=(t // tt,),
        out_shape=[jax.ShapeDtypeStruct((t, cw), BF16), jax.ShapeDtypeStruct(dproj.shape, dproj.dtype),
                   jax.ShapeDtypeStruct((t, cw), F32), jax.ShapeDtypeStruct((8, cw), F32)],
        in_specs=[tile(ycol), tile(0), tile(ncol - 1), pl.BlockSpec((cw, cw), lambda i: (0, 0)), tile(0), vec, vec,
                  pl.BlockSpec(memory_space=pl.ANY)],
        out_specs=[tile(0), tile(ncol - 1), tile(0), pl.BlockSpec((8, cw), lambda i: (0, 0))],
        input_output_aliases={7: 1},
        compiler_params=_params(("arbitrary",)),
    )(dycat, z, proj, wpw, hc, ln_g, ln_b, dproj)


def _conv_bwd(dhc, proj, w_dw, dproj, kc, cw):
    t = proj.shape[0]
    tt = _pick(t, 128, HALO)
    per = tt // HALO
    nt = t // tt
    chunk = _pick(cw, 256)
    ncol = proj.shape[1] // cw
    wr = w_dw.shape[0]

    def body(d_ref, dn_ref, u_ref, g_ref, up_ref, gp_ref, w_ref, dp_in, dp_ref, dw_ref, dbuf, hbuf, dw_acc):
        i = pl.program_id(0)

        @pl.when(i == 0)
        def _():
            dw_acc[...] = jnp.zeros_like(dw_acc)

        dbuf[0, pl.ds(0, tt), :] = d_ref[...]
        dbuf[0, pl.ds(tt, HALO), :] = jnp.where(i < nt - 1, dn_ref[...], 0.0)
        hbuf[0, pl.ds(HALO, tt), :] = _glu_rows(u_ref, g_ref)
        hbuf[0, pl.ds(0, HALO), :] = jnp.where(i > 0, _glu_rows(up_ref, gp_ref), 0.0)
        _fill_shifts(dbuf, tt + HALO - SUBLANES)
        _fill_shifts(hbuf, tt + HALO - SUBLANES)
        for lo in range(0, cw, chunk):
            dhg = _conv_taps(dbuf, w_ref, kc, lo, tt, chunk, lambda j: (kc - 1) - j)
            u = u_ref[:, lo:lo + chunk]
            sg = _sigmoid(g_ref[:, lo:lo + chunk])
            dp_ref[:, lo:lo + chunk] = (dhg * sg).astype(BF16)
            dp_ref[:, cw + lo:cw + lo + chunk] = (dhg * u * sg * (1.0 - sg)).astype(BF16)
            dtile = d_ref[:, lo:lo + chunk]
            for j in range(kc):
                prod = dtile * _shifted_rows(hbuf, HALO - (kc - 1) + j, tt, lo, chunk)
                dw_acc[j, :, lo:lo + chunk] += jnp.sum(prod.reshape(tt // SUBLANES, SUBLANES, chunk), axis=0)

        @pl.when(i == nt - 1)
        def _():
            dw_ref[...] = jnp.sum(dw_acc[...], axis=1)

    tile = lambda g: pl.BlockSpec((tt, cw), lambda i: (i, g))
    prev = lambda g: pl.BlockSpec((HALO, cw), lambda i: (jnp.maximum(i * per - 1, 0), g))
    return pl.pallas_call(
        body, name="conv_bwd", grid=(nt,),
        out_shape=[jax.ShapeDtypeStruct(dproj.shape, dproj.dtype), jax.ShapeDtypeStruct((wr, cw), F32)],
        in_specs=[pl.BlockSpec((tt, cw), lambda i: (i, 0)),
                  pl.BlockSpec((HALO, cw), lambda i: (jnp.minimum((i + 1) * per, nt * per - 1), 0)),
                  tile(ncol - 3), tile(ncol - 2), prev(ncol - 3), prev(ncol - 2),
                  pl.BlockSpec((wr, cw), lambda i: (0, 0)), pl.BlockSpec(memory_space=pl.ANY)],
        out_specs=[pl.BlockSpec((tt, 2 * cw), lambda i: (i, (ncol - 3) // 2)),
                   pl.BlockSpec((wr, cw), lambda i: (0, 0))],
        scratch_shapes=[pltpu.VMEM((SUBLANES, tt + HALO, cw), F32), pltpu.VMEM((SUBLANES, HALO + tt, cw), F32),
                        pltpu.VMEM((wr, SUBLANES, cw), F32)],
        input_output_aliases={7: 0},
        compiler_params=_params(("arbitrary",)),
    )(dhc, dhc, proj, proj, proj, proj, w_dw, dproj)


def _input_grad(dh, x, dout, norm_g, mod, rides=()):
    t, d = x.shape
    tt = _pick(t, 128, 16)

    def body(dh_ref, x_ref, do_ref, g_ref, mod_ref, gx_ref, sums_ref):
        i = pl.program_id(0)

        @pl.when(i == 0)
        def _():
            sums_ref[...] = jnp.zeros_like(sums_ref)

        xv = x_ref[...]
        dhv = dh_ref[...]
        r = lax.rsqrt(jnp.mean(xv * xv, axis=-1, keepdims=True) + EPS)
        xn = xv * r
        g = g_ref[...]
        one_scale = 1.0 + mod_ref[1:2, :]
        dxn = dhv * g * one_scale
        gx_ref[...] = do_ref[...] + r * (dxn - xn * jnp.mean(dxn * xn, axis=-1, keepdims=True))
        sums_ref[0:1, :] += jnp.sum(dhv, axis=0, keepdims=True)
        sums_ref[1:2, :] += jnp.sum(dhv * (xn * g), axis=0, keepdims=True)
        sums_ref[2:3, :] += jnp.sum(dhv * one_scale * xn, axis=0, keepdims=True)

    tile = pl.BlockSpec((tt, d), lambda i: (i, 0))
    return _call(
        body, name="input_grad", grid=(t // tt,),
        out_shape=[jax.ShapeDtypeStruct((t, d), F32), jax.ShapeDtypeStruct((8, d), F32)],
        in_specs=[tile, tile, tile, pl.BlockSpec((1, d), lambda i: (0, 0)), pl.BlockSpec((3, d), lambda i: (0, 0))],
        out_specs=[tile, pl.BlockSpec((8, d), lambda i: (0, 0))],
        args=(dh, x, dout, norm_g, mod), rides=rides)


def _sum_adam(parts, w, m, v, name):
    r, c = w.shape
    n_parts = parts.shape[0]
    tr = _pick(r, 128, 16) if r % 16 == 0 else r
    tc = _pick(c, 2048)

    def body(p_ref, w_ref, m_ref, v_ref, g_ref, d_ref, nm_ref, nv_ref):
        g = p_ref[0].astype(F32)
        for i in range(1, n_parts):
            g = g + p_ref[i].astype(F32)
        d, nm, nv = _adam(w_ref[...], g, m_ref[...], v_ref[...])
        g_ref[...] = g
        d_ref[...] = d
        nm_ref[...] = nm
        nv_ref[...] = nv

    tile = pl.BlockSpec((tr, tc), lambda i, j: (i, j))
    out = jax.ShapeDtypeStruct((r, c), F32)
    return pl.pallas_call(
        body, name=name, grid=(r // tr, c // tc),
        out_shape=[out] * 4,
        in_specs=[pl.BlockSpec((n_parts, tr, tc), lambda i, j: (0, i, j)), tile, tile, tile],
        out_specs=[tile] * 4,
        compiler_params=_params(("arbitrary", "arbitrary")),
    )(parts, w, m, v)


def _small_adam(parts, piece_rows, states):
    n_leaf = len(states)
    offsets = [sum(piece_rows[:i]) for i in range(len(piece_rows))]

    def total(p_ref, off, r):
        g = p_ref[0, off:off + r, :]
        for k in range(1, NDEV):
            g = g + p_ref[k, off:off + r, :]
        return g

    def body(p_ref, *refs):
        ins, outs = refs[:3 * n_leaf], refs[3 * n_leaf:]
        for i in range(n_leaf):
            w_ref, m_ref, v_ref = ins[3 * i:3 * i + 3]
            g = total(p_ref, offsets[i], w_ref.shape[0])
            d, nm, nv = _adam(w_ref[...], g, m_ref[...], v_ref[...])
            for o_ref, val in zip(outs[4 * i:4 * i + 4], (g, d, nm, nv)):
                o_ref[...] = val
        outs[4 * n_leaf][...] = total(p_ref, offsets[n_leaf], piece_rows[n_leaf])

    vmem = pl.BlockSpec(memory_space=pltpu.VMEM)
    flat = [a for leaf in states for a in leaf]
    out_shape = [jax.ShapeDtypeStruct(leaf[0].shape, F32) for leaf in states for _ in range(4)]
    out_shape.append(jax.ShapeDtypeStruct((piece_rows[n_leaf], LANES), F32))
    res = pl.pallas_call(
        body, name="small_adam", out_shape=out_shape,
        in_specs=[vmem] * (1 + len(flat)), out_specs=[vmem] * len(out_shape),
        compiler_params=pltpu.CompilerParams(vmem_limit_bytes=VMEM_LIMIT),
    )(parts, *flat)
    return [res[4 * i:4 * i + 4] for i in range(n_leaf)], res[4 * n_leaf]


def _ada_grad_adam(s_t, dm, w, m, v):
    d, n = w.shape
    tr = _pick(d, 256, 16)

    def body(s_ref, dm_ref, w_ref, m_ref, v_ref, g_ref, d_ref, nm_ref, nv_ref):
        g = lax.dot_general(s_ref[...], dm_ref[...], (NN, ((), ())), preferred_element_type=F32,
                            precision=lax.Precision.HIGHEST)
        dl, nm, nv = _adam(w_ref[...], g, m_ref[...], v_ref[...])
        g_ref[...] = g
        d_ref[...] = dl
        nm_ref[...] = nm
        nv_ref[...] = nv

    tile = pl.BlockSpec((tr, n), lambda i: (i, 0))
    out = jax.ShapeDtypeStruct((d, n), F32)
    return pl.pallas_call(
        body, name="ada_grad_adam", grid=(d // tr,),
        out_shape=[out] * 4,
        in_specs=[pl.BlockSpec((tr, NDEV), lambda i: (i, 0)), pl.BlockSpec((NDEV, n), lambda i: (0, 0)),
                  tile, tile, tile],
        out_specs=[tile] * 4,
        compiler_params=_params(("arbitrary",)),
    )(s_t, dm, w, m, v)


def _silu_t(c_all):
    n, d = c_all.shape

    def body(c_ref, o_ref):
        o_ref[...] = jnp.transpose(_silu(c_ref[...]))

    return pl.pallas_call(
        body, name="silu_t", out_shape=jax.ShapeDtypeStruct((d, n), F32),
        in_specs=[pl.BlockSpec(memory_space=pltpu.VMEM)], out_specs=pl.BlockSpec(memory_space=pltpu.VMEM),
        compiler_params=pltpu.CompilerParams(vmem_limit_bytes=VMEM_LIMIT),
    )(c_all)


def _rows128(v):
    return v.reshape(-1, LANES)


def _pad_rows(a, rows):
    return jnp.pad(a, ((0, rows - a.shape[0]), (0, 0)))


def kernel(x, c, norm_g, w_ada, b_ada, w_in, q_norm_g, k_norm_g, w_dw, b_dw, ln_g, ln_b, w_pw, b_pw, w_out, loss_target, m_norm_g, m_w_ada, m_b_ada, m_w_in, m_q_norm_g, m_k_norm_g, m_w_dw, m_b_dw, m_ln_g, m_ln_b, m_w_pw, m_b_pw, m_w_out, v_norm_g, v_w_ada, v_b_ada, v_w_in, v_q_norm_g, v_k_norm_g, v_w_dw, v_b_dw, v_ln_g, v_ln_b, v_w_pw, v_b_pw, v_w_out):
    _, t, d = x.shape
    n_ada = w_ada.shape[2]
    ns = w_in.shape[2]
    kc, cwl = w_dw.shape[1], w_dw.shape[2]
    cw = cwl * NDEV
    sb = d - cw
    nh = sb // HEAD_DIM
    assert sb == cw and kc - 1 <= HALO and NDEV * ns == 4 * sb + 3 * cw
    my = 4 * lax.axis_index("x") + 2 * lax.axis_index("y") + lax.axis_index("c")

    x2, tg2 = x[0], loss_target[0]

    wdw_rows = -(-kc // 8) * 8
    wdw_pad = _pad_rows(w_dw[0], wdw_rows)
    pay1 = jnp.concatenate([_rows128(c[0]), _rows128(wdw_pad.reshape(-1))], axis=0)
    (g1,) = _all_gather([pay1], "gather_cond", pltpu.VMEM)
    c_rows = d // LANES
    c_all = g1[:, :c_rows].reshape(NDEV, d)
    wdw_all = g1[:, c_rows:].reshape(NDEV, wdw_rows, cwl).transpose(1, 0, 2).reshape(wdw_rows, cw)

    b_ada_loc = lax.dynamic_slice(b_ada, (0, my * n_ada), (1, n_ada))
    mod_cols = _ada_matmul(c_all, w_ada[0], b_ada_loc)
    (g2,) = _all_gather([mod_cols], "gather_mod", pltpu.VMEM)
    mod_mine = lax.dynamic_index_in_dim(g2, my, axis=1, keepdims=False)
    mod = mod_mine.reshape(3, d)

    core = lax.axis_index("c").astype(jnp.int32).reshape(1)
    h = _modulated_norm(x2, norm_g, mod)
    wfull_in, proj = _gather_proj(h, w_in[0].astype(BF16))
    (o, tot, ycat, firsts), (partly,) = _attention_fwd(
        proj, q_norm_g, k_norm_g, nh, d, _gather_ride([w_out[0].astype(BF16), w_pw[0].astype(BF16)]))
    (a, hc), ((wg_out, wg_pw),) = _conv_fwd(proj, wdw_all, b_dw, ln_g, ln_b, kc, cw, _gather_finish_ride(partly))
    wfull_out, wfull_pw = wg_out.reshape(d, d), wg_pw.reshape(cw, cw)
    z, ycat = _pointwise_fwd(a, wfull_pw, b_pw, proj, ycat, cw)
    dout, dy, out_sums = _out_matmul(ycat, wfull_out, x2, tg2, mod)

    tile = _pick(t, 512, 16)
    (dycat,), _ = _mm(dy, wfull_out, NT, "dycat_matmul", _pick(t, 1024, 16), _pick(d, 512), F32)
    (p_wout,), _ = _mm(ycat, dy, TN, "w_out_grad", _pick(d, 512), _pick(d, 1024), BF16)
    p_wout = p_wout.reshape(NDEV, d // NDEV, d)
    dproj, gains = _attention_bwd(proj, o, tot, firsts, dycat, q_norm_g, k_norm_g, nh)
    dz, dproj, dhc, pw_sums = _pointwise_bwd(dycat, z, proj, dproj, wfull_pw, hc, ln_g, ln_b, cw)
    (p_wpw,), _ = _mm(a, dz, TN, "w_pw_grad", _pick(cw, 512), _pick(cw, 1024), BF16)
    p_wpw = p_wpw.reshape(NDEV, cw // NDEV, cw)
    dproj, dwdw = _conv_bwd(dhc, proj, wdw_all, dproj, kc, cw)
    p_wdw = dwdw.reshape(wdw_rows, NDEV, cwl).transpose(1, 0, 2).astype(BF16)

    def pair_sums(mine, theirs, name):
        return [_pair_sum(m, s, core, f"{name}_pair_sum_{i}") for i, (m, s) in enumerate(zip(mine, theirs))]

    lesser = [p.reshape(4, 2, *p.shape[1:]) for p in (p_wout, p_wpw, p_wdw)]
    (p_theirs,), (small_theirs,) = _mm(h, dproj, TN, "w_in_grad_sibling", _pick(d, 256), ns, BF16, slabs=True,
                                       n_outer=True, every_other=1 - core, rides=[_sibling_ride(lesser)])
    q_small = pair_sums(lesser, small_theirs, "small_grads")
    (p_mine,), (win_theirs, (r_wout, r_wpw, r_wdw)) = _mm(
        h, dproj, TN, "w_in_grad_own", _pick(d, 256), ns, BF16, slabs=True, n_outer=True, every_other=core,
        rides=[_sibling_ride([p_theirs]), _chip_exchange_ride(q_small)])
    q_win = pair_sums([p_mine[:, None]], win_theirs, "w_in_grad")
    head = d - d // 8
    (dh,), ((r_head,),) = _dh_matmul(dproj, wfull_in, tile, _pick(d, 512), 2,
                                     [_chip_exchange_ride(q_win, rows=(0, head))])
    (grad_x, in_sums), ((r_win,),) = _input_grad(
        dh, x2, dout, norm_g, mod, rides=[_chip_exchange_ride(q_win, rows=(head, d - head), into=[r_head])])

    dmod = jnp.concatenate([in_sums[0], in_sums[1], out_sums[0]])
    loss_part = 0.5 / d * jnp.sum(out_sums[1].reshape(-1, LANES), axis=0)
    small = [in_sums[2], dmod, jnp.sum(gains[:, 0], axis=0), jnp.sum(gains[:, 1], axis=0),
             pw_sums[3], pw_sums[1], pw_sums[2], pw_sums[0], loss_part]
    pieces = [_rows128(s) for s in small]
    pieces = [_pad_rows(p, -(-p.shape[0] // 8) * 8) for p in pieces]
    (g3,) = _all_gather([jnp.concatenate(pieces, axis=0)], "gather_small", pltpu.VMEM)
    states = [[_rows128(s.reshape(-1)) for s in leaf] for leaf in (
        (norm_g, m_norm_g, v_norm_g), (b_ada, m_b_ada, v_b_ada), (q_norm_g, m_q_norm_g, v_q_norm_g),
        (k_norm_g, m_k_norm_g, v_k_norm_g), (b_dw, m_b_dw, v_b_dw), (ln_g, m_ln_g, v_ln_g),
        (ln_b, m_ln_b, v_ln_b), (b_pw, m_b_pw, v_b_pw))]
    small_out, loss_rows = _small_adam(g3, [p.shape[0] for p in pieces], states)
    g_small, d_small, m_small, v_small = [[leaf[k].reshape(1, -1) for leaf in small_out] for k in range(4)]
    loss = jnp.sum(loss_rows[0])

    off = pieces[0].shape[0]
    dmod_all = g3[:, off:off + pieces[1].shape[0]].reshape(NDEV, 3 * d)
    dmod_loc = lax.dynamic_slice(dmod_all, (0, my * n_ada), (NDEV, n_ada))
    ada = _ada_grad_adam(_silu_t(c_all), dmod_loc, w_ada[0], m_w_ada[0], v_w_ada[0])
    win = _sum_adam(r_win, w_in[0], m_w_in[0], v_w_in[0], "w_in_adam")
    wout = _sum_adam(r_wout, w_out[0], m_w_out[0], v_w_out[0], "w_out_adam")
    wpw = _sum_adam(r_wpw, w_pw[0], m_w_pw[0], v_w_pw[0], "w_pw_adam")
    wdw_state = [_pad_rows(s[0], wdw_rows) for s in (w_dw, m_w_dw, v_w_dw)]
    wdw = [r[:kc] for r in _sum_adam(r_wdw, *wdw_state, "w_dw_adam")]

    def group(k, small_list):
        s = small_list
        return [s[0], ada[k][None], s[1], win[k][None], s[2], s[3], wdw[k][None], s[4], s[5], s[6],
                wpw[k][None], s[7], wout[k][None]]

    return (loss, grad_x[None], *group(0, g_small), *group(1, d_small), *group(2, m_small), *group(3, v_small))
```

```python
import functools

import jax
import jax.numpy as jnp
from jax import lax
from jax.experimental import pallas as pl
from jax.experimental.pallas import tpu as pltpu

F32 = jnp.float32
BF16 = jnp.bfloat16
NDEV = 8
HEAD_DIM = 128
LANES = 128
SUBLANES = 8
HALO = 32
EPS = 1e-6
DEAD_LOG_WEIGHT = -104.0
VMEM_LIMIT = 56 * 1024 * 1024
MESH = pl.DeviceIdType.MESH

ADAM_LR = 0.001
ADAM_B1 = 0.9
ADAM_B2 = 0.999
ADAM_EPS = 1e-08
ADAM_WD = 0.01
ADAM_STEP = 10


def _params(sem=None):
    return pltpu.CompilerParams(dimension_semantics=sem, vmem_limit_bytes=VMEM_LIMIT)


def _pick(n, pref, unit=LANES):
    best = None
    for d in range(unit, min(n, pref) + 1, unit):
        if n % d == 0:
            best = d
    return best if best is not None else n


def _sigmoid(z):
    return 1.0 / (1.0 + jnp.exp(-z))


def _silu(z):
    return z * _sigmoid(z)


def _dsilu(z):
    s = _sigmoid(z)
    return s * (1.0 + z * (1.0 - s))


def _softplus(z):
    return jnp.maximum(z, 0.0) + jnp.log(1.0 + jnp.exp(-jnp.abs(z)))


def _dot(a, b, dims):
    return lax.dot_general(a, b, (dims, ((), ())), preferred_element_type=F32)


NN = ((1,), (0,))
NT = ((1,), (1,))
TN = ((0,), (0,))


def _adam(w, g, m, v):
    m = ADAM_B1 * m + (1.0 - ADAM_B1) * g
    v = ADAM_B2 * v + (1.0 - ADAM_B2) * (g * g)
    m_hat = m / (1.0 - ADAM_B1 ** ADAM_STEP)
    v_hat = v / (1.0 - ADAM_B2 ** ADAM_STEP)
    delta = -ADAM_LR * (m_hat / (jnp.sqrt(v_hat) + ADAM_EPS) + ADAM_WD * w)
    return delta, m, v


def _place():
    x, y, c = lax.axis_index("x"), lax.axis_index("y"), lax.axis_index("c")
    return x, y, c


def _flip(v, bit):
    return 1 - v if bit else v


def _all_gather(arrs, name, space):
    n = len(arrs)

    def body(*refs):
        ins, outs = refs[:n], refs[n:2 * n]
        send_sems, recv_sems, local_sems = refs[2 * n:]
        x, y, c = _place()
        me, sibling = (x, y, c), (x, y, 1 - c)
        chips = [(1 - x, y), (x, 1 - y), (1 - x, 1 - y)]

        def rows(a, p):
            return outs[a].at[4 * p[0] + 2 * p[1] + p[2]]

        def copy(a, k, block, to, src=None):
            return pltpu.make_async_remote_copy(
                src_ref=rows(a, block) if src is None else src, dst_ref=rows(a, block),
                send_sem=send_sems.at[7 * a + k], recv_sem=recv_sems.at[7 * a + k],
                device_id=to, device_id_type=MESH)

        mine = [pltpu.make_async_copy(ins[a], rows(a, me), local_sems.at[a]) for a in range(n)]
        for cp in mine:
            cp.start()
        first = []
        for a in range(n):
            first.append(copy(a, 0, me, sibling, src=ins[a]))
            first += [copy(a, 1 + j, me, (*chip, c), src=ins[a]) for j, chip in enumerate(chips)]
        for cp in first:
            cp.start()
        passed = []
        for j, chip in enumerate(chips):
            for a in range(n):
                copy(a, 1 + j, (*chip, c), me).wait_recv()
                fwd = copy(a, 4 + j, (*chip, c), sibling)
                fwd.start()
                passed.append(fwd)
        for a in range(n):
            copy(a, 0, sibling, me).wait_recv()
            for j, chip in enumerate(chips):
                copy(a, 4 + j, (*chip, 1 - c), me).wait_recv()
        for cp in first + passed:
            cp.wait_send()
        for cp in mine:
            cp.wait()

    spec = pl.BlockSpec(memory_space=space)
    return pl.pallas_call(
        body, name=name,
        out_shape=[jax.ShapeDtypeStruct((NDEV,) + a.shape, a.dtype) for a in arrs],
        in_specs=[spec] * n, out_specs=[spec] * n,
        scratch_shapes=[pltpu.SemaphoreType.DMA((7 * n,)), pltpu.SemaphoreType.DMA((7 * n,)),
                        pltpu.SemaphoreType.DMA((n,))],
        compiler_params=pltpu.CompilerParams(vmem_limit_bytes=VMEM_LIMIT),
    )(*arrs)


class _Ride:
    def __init__(self, ins, out_shapes, n_sems, start, finish, alias=None):
        self.ins, self.out_shapes, self.n_sems, self.start, self.finish = ins, out_shapes, n_sems, start, finish
        self.alias = alias or {}


def _call(body, *, name, grid, out_shape, in_specs, out_specs, args, scratch_shapes=(), rides=(), prefetch=None):
    sem = ("arbitrary",) * len(grid)
    rides = [r for r in rides if r is not None]
    n_pre = 0 if prefetch is None else 1
    n_in, n_out, n_scr = len(in_specs), len(out_specs), len(scratch_shapes)
    r_ins = [len(r.ins) for r in rides]
    r_outs = [len(r.out_shapes) for r in rides]

    def carried(*refs):
        pre, refs = refs[:n_pre], refs[n_pre:]
        ins, pos = refs[:n_in], n_in
        rins = []
        for k in r_ins:
            rins.append(refs[pos:pos + k])
            pos += k
        outs, pos = refs[pos:pos + n_out], pos + n_out
        routs = []
        for k in r_outs:
            routs.append(refs[pos:pos + k])
            pos += k
        scratch, pos = refs[pos:pos + n_scr], pos + n_scr
        sems = [refs[pos + 3 * i:pos + 3 * i + 3] for i in range(len(rides))]
        first = functools.reduce(lambda a, b: a & b, [pl.program_id(i) == 0 for i in range(len(grid))])
        last = functools.reduce(lambda a, b: a & b, [pl.program_id(i) == grid[i] - 1 for i in range(len(grid))])

        @pl.when(first)
        def _():
            for ride, ri, ro, s in zip(rides, rins, routs, sems):
                ride.start(ri, ro, *s)

        body(*pre, *ins, *outs, *scratch)

        @pl.when(last)
        def _():
            for ride, ri, ro, s in zip(rides, rins, routs, sems):
                ride.finish(ri, ro, *s)

    hbm = pl.BlockSpec(memory_space=pl.ANY)
    aliases, in_pos, out_pos = {}, n_pre + n_in, n_out
    for ride, ki, ko in zip(rides, r_ins, r_outs):
        aliases.update({in_pos + k: out_pos + o for k, o in ride.alias.items()})
        in_pos, out_pos = in_pos + ki, out_pos + ko
    all_scratch = list(scratch_shapes)
    for ride in rides:
        all_scratch += [pltpu.SemaphoreType.DMA((ride.n_sems,))] * 3
    all_in = list(in_specs) + [hbm] * sum(r_ins)
    all_out = list(out_specs) + [hbm] * sum(r_outs)
    shapes = list(out_shape) + [s for r in rides for s in r.out_shapes]
    operands = list(args) + [a for r in rides for a in r.ins]
    if prefetch is None:
        res = pl.pallas_call(
            carried, name=name, grid=grid, out_shape=shapes, in_specs=all_in, out_specs=all_out,
            scratch_shapes=all_scratch, input_output_aliases=aliases, compiler_params=_params(sem))(*operands)
    else:
        res = pl.pallas_call(
            carried, name=name, out_shape=shapes,
            grid_spec=pltpu.PrefetchScalarGridSpec(num_scalar_prefetch=1, grid=grid, in_specs=all_in,
                                                   out_specs=all_out, scratch_shapes=all_scratch),
            input_output_aliases=aliases, compiler_params=_params(sem))(prefetch, *operands)
    split, pos = [], n_out
    for k in r_outs:
        split.append(res[pos:pos + k])
        pos += k
    return res[:n_out], split


def _chips(x, y):
    return [(1 - x, y), (x, 1 - y), (1 - x, 1 - y)]


def _gather_ride(arrs):
    n = len(arrs)

    def copies(ins, outs, send_sems, recv_sems):
        x, y, c = _place()
        me = 4 * x + 2 * y + c
        peers = [(x, y, 1 - c)] + [(*chip, c) for chip in _chips(x, y)]
        return [pltpu.make_async_remote_copy(
            src_ref=ins[a], dst_ref=outs[a].at[me], send_sem=send_sems.at[4 * a + k], recv_sem=recv_sems.at[4 * a + k],
            device_id=p, device_id_type=MESH) for a in range(n) for k, p in enumerate(peers)], me

    def start(ins, outs, send_sems, recv_sems, local_sems):
        cps, me = copies(ins, outs, send_sems, recv_sems)
        for a in range(n):
            pltpu.make_async_copy(ins[a], outs[a].at[me], local_sems.at[a]).start()
        for cp in cps:
            cp.start()

    def finish(ins, outs, send_sems, recv_sems, local_sems):
        cps, me = copies(ins, outs, send_sems, recv_sems)
        for cp in cps:
            cp.wait_recv()
        for cp in cps:
            cp.wait_send()
        for a in range(n):
            pltpu.make_async_copy(ins[a], outs[a].at[me], local_sems.at[a]).wait()

    return _Ride(arrs, [jax.ShapeDtypeStruct((NDEV,) + a.shape, a.dtype) for a in arrs], 4 * n, start, finish)


def _gather_finish_ride(arrs):
    n = len(arrs)

    def copies(outs, send_sems, recv_sems):
        x, y, c = _place()
        cps = []
        for a in range(n):
            for k, chip in enumerate(_chips(x, y)):
                blk = 4 * chip[0] + 2 * chip[1]
                cps.append((pltpu.make_async_remote_copy(
                    src_ref=outs[a].at[blk + c], dst_ref=outs[a].at[blk + c],
                    send_sem=send_sems.at[3 * a + k], recv_sem=recv_sems.at[3 * a + k],
                    device_id=(x, y, 1 - c), device_id_type=MESH),
                    pltpu.make_async_remote_copy(
                    src_ref=outs[a].at[blk + 1 - c], dst_ref=outs[a].at[blk + 1 - c],
                    send_sem=send_sems.at[3 * a + k], recv_sem=recv_sems.at[3 * a + k],
                    device_id=(x, y, 1 - c), device_id_type=MESH)))
        return cps

    def start(ins, outs, send_sems, recv_sems, local_sems):
        for send, _ in copies(outs, send_sems, recv_sems):
            send.start()

    def finish(ins, outs, send_sems, recv_sems, local_sems):
        cps = copies(outs, send_sems, recv_sems)
        for _, recv in cps:
            recv.wait_recv()
        for send, _ in cps:
            send.wait_send()

    return _Ride(arrs, [jax.ShapeDtypeStruct(a.shape, a.dtype) for a in arrs], 3 * n, start, finish,
                 alias={a: a for a in range(n)})


def _sibling_ride(arrs):
    n = len(arrs)

    def copies(ins, outs, send_sems, recv_sems):
        x, y, c = _place()
        return [pltpu.make_async_remote_copy(
            src_ref=ins[a].at[:, 1 - c] if arrs[a].ndim == 4 else ins[a], dst_ref=outs[a],
            send_sem=send_sems.at[a], recv_sem=recv_sems.at[a],
            device_id=(x, y, 1 - c), device_id_type=MESH) for a in range(n)]

    def start(ins, outs, send_sems, recv_sems, local_sems):
        for cp in copies(ins, outs, send_sems, recv_sems):
            cp.start()

    def finish(ins, outs, send_sems, recv_sems, local_sems):
        for cp in copies(ins, outs, send_sems, recv_sems):
            cp.wait()

    return _Ride(arrs, [jax.ShapeDtypeStruct((4,) + a.shape[-2:], a.dtype) for a in arrs], n, start, finish)


def _pair_sum(mine, theirs, core, name):
    _, pick, r, c = mine.shape
    if pick == 1:
        core = jnp.zeros_like(core)
    tr = _pick(r, 512, 16)

    def body(core_ref, a_ref, b_ref, o_ref):
        o_ref[...] = (a_ref[...].astype(F32) + b_ref[...].astype(F32)).astype(BF16)

    return pl.pallas_call(
        body, name=name,
        grid_spec=pltpu.PrefetchScalarGridSpec(
            num_scalar_prefetch=1, grid=(4, r // tr),
            in_specs=[pl.BlockSpec((None, None, tr, c), lambda i, k, core_ref: (i, core_ref[0], k, 0)),
                      pl.BlockSpec((None, tr, c), lambda i, k, core_ref: (i, k, 0))],
            out_specs=pl.BlockSpec((None, tr, c), lambda i, k, core_ref: (i, k, 0))),
        out_shape=jax.ShapeDtypeStruct((4, r, c), BF16),
        compiler_params=_params(("arbitrary", "arbitrary")),
    )(core, mine, theirs)


def _chip_exchange_ride(arrs, rows=None, into=None):
    n = len(arrs)

    def part(ref):
        return ref if rows is None else ref.at[pl.ds(rows[0], rows[1]), :]

    def copies(ins, outs, send_sems, recv_sems):
        x, y, c = _place()
        mine = 2 * x + y
        return [pltpu.make_async_remote_copy(
            src_ref=part(ins[a].at[2 * chip[0] + chip[1]]), dst_ref=part(outs[a].at[mine]),
            send_sem=send_sems.at[3 * a + k], recv_sem=recv_sems.at[3 * a + k],
            device_id=(*chip, c), device_id_type=MESH) for a in range(n) for k, chip in enumerate(_chips(x, y))], mine

    def start(ins, outs, send_sems, recv_sems, local_sems):
        cps, mine = copies(ins, outs, send_sems, recv_sems)
        if into is None:
            for a in range(n):
                pltpu.make_async_copy(ins[a].at[mine], outs[a].at[mine], local_sems.at[a]).start()
        for cp in cps:
            cp.start()

    def finish(ins, outs, send_sems, recv_sems, local_sems):
        cps, mine = copies(ins, outs, send_sems, recv_sems)
        for cp in cps:
            cp.wait_recv()
        for cp in cps:
            cp.wait_send()
        if into is None:
            for a in range(n):
                pltpu.make_async_copy(ins[a].at[mine], outs[a].at[mine], local_sems.at[a]).wait()

    shapes = [jax.ShapeDtypeStruct(a.shape, a.dtype) for a in arrs]
    if into is None:
        return _Ride(arrs, shapes, 3 * n, start, finish)
    return _Ride(list(arrs) + list(into), shapes, 3 * n, start, finish, alias={n + a: a for a in range(n)})


def _ada_matmul(c_all, w_loc, b_loc):
    d, n = w_loc.shape
    bn = _pick(n, 512)

    def body(c_ref, w_ref, b_ref, o_ref):
        s = _silu(c_ref[...]).astype(BF16)
        o_ref[...] = _dot(s, w_ref[...].astype(BF16), NN) + b_ref[...]

    return pl.pallas_call(
        body, name="ada_matmul", grid=(n // bn,),
        out_shape=jax.ShapeDtypeStruct((NDEV, n), F32),
        in_specs=[pl.BlockSpec((NDEV, d), lambda j: (0, 0)), pl.BlockSpec((d, bn), lambda j: (0, j)),
                  pl.BlockSpec((1, bn), lambda j: (0, j))],
        out_specs=pl.BlockSpec((NDEV, bn), lambda j: (0, j)),
        compiler_params=_params(("arbitrary",)),
    )(c_all, w_loc, b_loc)


def _modulated_norm(x, norm_g, mod):
    t, d = x.shape
    tt = _pick(t, 256, 16)

    def body(x_ref, g_ref, mod_ref, h_ref):
        xv = x_ref[...]
        r = lax.rsqrt(jnp.mean(xv * xv, axis=-1, keepdims=True) + EPS)
        h = (xv * r) * g_ref[...] * (1.0 + mod_ref[1:2, :]) + mod_ref[0:1, :]
        h_ref[...] = h.astype(BF16)

    return pl.pallas_call(
        body, name="modulated_norm", grid=(t // tt,),
        out_shape=jax.ShapeDtypeStruct((t, d), BF16),
        in_specs=[pl.BlockSpec((tt, d), lambda i: (i, 0)), pl.BlockSpec((1, d), lambda i: (0, 0)),
                  pl.BlockSpec((3, d), lambda i: (0, 0))],
        out_specs=pl.BlockSpec((tt, d), lambda i: (i, 0)),
        compiler_params=_params(("arbitrary",)),
    )(x, norm_g, mod)


def _gather_proj(h, w_loc):
    t, d = h.shape
    ns = w_loc.shape[1]
    tm = _pick(t, 512, 16)
    nm = t // tm
    idx = lambda p: 4 * p[0] + 2 * p[1] + p[2]

    def peers():
        x, y, c = _place()
        flip = lambda a, b: a + b - 2 * a * b
        near, far = (flip(x, c), flip(y, 1 - c)), (flip(x, 1 - c), flip(y, c))
        return (x, y, c), (x, y, 1 - c), near, far, (1 - x, 1 - y), c

    me, sibling, near, far, diag, c = peers()
    order = [me, sibling, (*near, c), (*far, 1 - c), (*far, c), (*near, 1 - c), (*diag, c), (*diag, 1 - c)]
    order = jnp.stack([idx(p) for p in order]).astype(jnp.int32)

    def body(order_ref, a_ref, w_ref, wg_ref, o_ref, slab, send_sems, recv_sems, local_sems):
        j, m = pl.program_id(0), pl.program_id(1)
        me, sibling, near, far, diag, c = peers()

        def rows(p):
            return wg_ref.at[:, pl.ds(pl.multiple_of(idx(p) * ns, LANES), ns)]

        def copy(k, block, to, src=None):
            return pltpu.make_async_remote_copy(
                src_ref=rows(block) if src is None else src, dst_ref=rows(block),
                send_sem=send_sems.at[k], recv_sem=recv_sems.at[k], device_id=to, device_id_type=MESH)

        def load(src):
            cp = pltpu.make_async_copy(src, slab, local_sems.at[1])
            cp.start()
            cp.wait()

        keep = pltpu.make_async_copy(w_ref, rows(me), local_sems.at[0])
        own = [copy(0, me, sibling, src=w_ref), copy(1, me, (*near, c), src=w_ref), copy(2, me, (*far, c), src=w_ref)]
        relay = copy(3, (*near, c), (*far, c))
        passed = [copy(4, (*near, c), sibling), copy(5, (*far, c), sibling), copy(6, (*diag, c), sibling)]
        arrivals = [(1, 0, sibling, []), (2, 1, (*near, c), [passed[0], own[2], relay]), (3, 4, (*far, 1 - c), []),
                    (4, 2, (*far, c), [passed[1]]), (5, 5, (*near, 1 - c), []),
                    (6, 3, (*diag, c), [passed[2]]), (7, 6, (*diag, 1 - c), [])]

        @pl.when((j == 0) & (m == 0))
        def _():
            keep.start()
            for cp in own[:2]:
                cp.start()
            load(w_ref)

        for step, sem, block, onward in arrivals:
            @pl.when((j == step) & (m == 0))
            def _(sem=sem, block=block, onward=onward):
                copy(sem, block, me).wait_recv()
                for cp in onward:
                    cp.start()
                load(rows(block))

        o_ref[...] = _dot(a_ref[...], slab[...], NN)

        @pl.when((j == NDEV - 1) & (m == nm - 1))
        def _():
            for cp in own + [relay] + passed:
                cp.wait_send()
            keep.wait()

    hbm = pl.BlockSpec(memory_space=pl.ANY)
    return pl.pallas_call(
        body, name="gather_proj",
        grid_spec=pltpu.PrefetchScalarGridSpec(
            num_scalar_prefetch=1, grid=(NDEV, nm),
            in_specs=[pl.BlockSpec((tm, d), lambda j, m, order_ref: (m, 0)), hbm],
            out_specs=[hbm, pl.BlockSpec((tm, ns), lambda j, m, order_ref: (m, order_ref[j]))],
            scratch_shapes=[pltpu.VMEM((d, ns), BF16), pltpu.SemaphoreType.DMA((7,)), pltpu.SemaphoreType.DMA((7,)),
                            pltpu.SemaphoreType.DMA((2,))]),
        out_shape=[jax.ShapeDtypeStruct((d, NDEV * ns), BF16), jax.ShapeDtypeStruct((t, NDEV * ns), F32)],
        compiler_params=_params(("arbitrary", "arbitrary")),
    )(order, h, w_loc)


def _attention_fwd(proj, qg, kg, nh, d_model, ride):
    t = proj.shape[0]
    tq = _pick(t, 256, 16)
    nq = t // tq
    assert 2 <= nq <= LANES
    group = 4 if nq % 4 == 0 else 2 if nq % 2 == 0 else 1
    scale = HEAD_DIM ** -0.5

    def body(q_ref, k_ref, v_ref, g_ref, qg_ref, kg_ref, o_ref, tot_ref, y_ref, first_ref, qn, kn, vb):
        def norm(src, gain, dst):
            v = src[...]
            r = lax.rsqrt(jnp.mean(v * v, axis=-1, keepdims=True) + EPS)
            dst[...] = ((v * r) * gain[...]).astype(BF16)

        norm(q_ref, qg_ref, qn)
        norm(k_ref, kg_ref, kn)
        vb[...] = v_ref[...].astype(BF16)
        def after_matrix(n):
            return (lax.broadcasted_iota(jnp.int32, (n, n), 0) > lax.broadcasted_iota(jnp.int32, (n, n), 1)).astype(BF16)

        upper = {tq: after_matrix(tq), 2 * tq: after_matrix(2 * tq)}
        key_minus_query = (lax.broadcasted_iota(jnp.int32, (tq, 2 * tq), 1)
                           - lax.broadcasted_iota(jnp.int32, (tq, 2 * tq), 0))

        def block(qi, start, width, carry, acc, q_start=None):
            ks = pl.ds(pl.multiple_of(start, tq), width)
            z = _dot(qi, kn[ks, :], NT) * scale
            sp = _softplus(z)
            ls = -sp
            if q_start is not None:
                causal = key_minus_query < q_start - start
                ls = jnp.where(causal, ls, 0.0)
            hi = ls.astype(BF16)
            lo = (ls - hi.astype(F32)).astype(BF16)
            after = _dot(hi, upper[width], NN) + _dot(lo, upper[width], NN)
            w = jnp.exp(z - sp + after + carry)
            if q_start is not None:
                w = jnp.where(causal, w, 0.0)
            acc = acc + _dot(w.astype(BF16), vb[ks, :], NN)
            carry = carry + jnp.sum(ls, axis=1, keepdims=True)
            return carry, acc

        lane = lax.broadcasted_iota(jnp.int32, (8, LANES), 1)

        def live(carry):
            return (jnp.max(carry) > DEAD_LOG_WEIGHT).astype(jnp.int32)

        def wide_step(i):
            qi = qn[pl.ds(pl.multiple_of(i * tq, tq), tq), :]
            left = jnp.maximum(i - 1, 0)
            return block(qi, left * tq, 2 * tq, jnp.zeros((tq, 1), F32), jnp.zeros((tq, HEAD_DIM), F32), i * tq)

        def finish(i, carry, acc, firsts):
            qs = pl.ds(pl.multiple_of(i * tq, tq), tq)
            qi = qn[qs, :]
            left = jnp.maximum(i - 1, 0)

            def k_step(st):
                ca, ac = block(qi, (left - 1 - st[0]) * tq, tq, st[1], st[2])
                return st[0] + 1, ca, ac, live(ca)

            done, carry, acc, _ = lax.while_loop(
                lambda st: (st[0] < left) & (st[3] > 0), k_step, (jnp.int32(0), carry, acc, live(carry)))
            o_ref[qs, :] = acc
            tot_ref[qs, :] = jnp.broadcast_to(carry, (tq, HEAD_DIM))
            y_ref[qs, :] = (acc * _silu(g_ref[qs, :])).astype(BF16)
            return jnp.where(lane == i, (left - done).astype(F32), firsts)

        def q_group(p, firsts):
            blocks = [p + b * (nq // group) for b in range(group)]
            swept = [wide_step(i) for i in blocks]
            for i, (carry, acc) in zip(blocks, swept):
                firsts = finish(i, carry, acc, firsts)
            return firsts

        first_ref[...] = lax.fori_loop(0, nq // group, q_group, jnp.zeros((8, LANES), F32))

    col_block = lambda off: pl.BlockSpec((t, HEAD_DIM), lambda h: (0, off + h))
    vec = pl.BlockSpec((1, HEAD_DIM), lambda h: (0, 0))
    return _call(
        body, name="attention_fwd", grid=(nh,),
        out_shape=[jax.ShapeDtypeStruct((t, nh * HEAD_DIM), F32), jax.ShapeDtypeStruct((t, nh * HEAD_DIM), F32),
                   jax.ShapeDtypeStruct((t, d_model), BF16), jax.ShapeDtypeStruct((nh, 8, LANES), F32)],
        in_specs=[col_block(0), col_block(nh), col_block(2 * nh), col_block(3 * nh), vec, vec],
        out_specs=[col_block(0), col_block(0), col_block(0), pl.BlockSpec((None, 8, LANES), lambda h: (h, 0, 0))],
        scratch_shapes=[pltpu.VMEM((t, HEAD_DIM), BF16)] * 3,
        args=(proj, proj, proj, proj, qg, kg), rides=[ride])


def _fill_shifts(shifted, n_rows):
    for b in range(1, SUBLANES):
        shifted[b, pl.ds(0, n_rows), :] = shifted[0, pl.ds(b, n_rows), :]


def _shifted_rows(shifted, offset, n_rows, lo, cw):
    a, b = divmod(offset, SUBLANES)
    return shifted[b, pl.ds(SUBLANES * a, n_rows), pl.ds(lo, cw)]


def _conv_taps(shifted, w_ref, kc, lo, n_rows, cw, first_tap_row):
    acc = None
    for j in range(kc):
        term = w_ref[j:j + 1, lo:lo + cw] * _shifted_rows(shifted, first_tap_row(j), n_rows, lo, cw)
        acc = term if acc is None else acc + term
    return acc


def _glu_rows(u_ref, g_ref):
    return u_ref[...] * _sigmoid(g_ref[...])


def _conv_fwd(proj, w_dw, b_dw, ln_g, ln_b, kc, cw, ride):
    t = proj.shape[0]
    tt = _pick(t, 128, HALO)
    per = tt // HALO
    chunk = _pick(cw, 256)

    def body(u_ref, g_ref, up_ref, gp_ref, w_ref, b_ref, lg_ref, lb_ref, a_ref, hc_ref, buf):
        i = pl.program_id(0)
        buf[0, pl.ds(HALO, tt), :] = _glu_rows(u_ref, g_ref)
        halo = _glu_rows(up_ref, gp_ref)
        buf[0, pl.ds(0, HALO), :] = jnp.where(i > 0, halo, 0.0)
        _fill_shifts(buf, tt + HALO - SUBLANES)
        for lo in range(0, cw, chunk):
            conv = _conv_taps(buf, w_ref, kc, lo, tt, chunk, lambda j: HALO - (kc - 1) + j)
            hc_ref[:, lo:lo + chunk] = conv + b_ref[:, lo:lo + chunk]
        hc = hc_ref[...]
        mu = jnp.mean(hc, axis=-1, keepdims=True)
        xc = hc - mu
        var = jnp.mean(xc * xc, axis=-1, keepdims=True)
        ln = xc * lax.rsqrt(var + EPS) * lg_ref[...] + lb_ref[...]
        a_ref[...] = _silu(ln).astype(BF16)

    ncol = proj.shape[1] // cw
    tile = lambda g: pl.BlockSpec((tt, cw), lambda i: (i, g))
    prev = lambda g: pl.BlockSpec((HALO, cw), lambda i: (jnp.maximum(i * per - 1, 0), g))
    full = lambda r: pl.BlockSpec((r, cw), lambda i: (0, 0))
    return _call(
        body, name="conv_fwd", grid=(t // tt,),
        out_shape=[jax.ShapeDtypeStruct((t, cw), BF16), jax.ShapeDtypeStruct((t, cw), F32)],
        in_specs=[tile(ncol - 3), tile(ncol - 2), prev(ncol - 3), prev(ncol - 2),
                  full(w_dw.shape[0]), full(1), full(1), full(1)],
        out_specs=[pl.BlockSpec((tt, cw), lambda i: (i, 0))] * 2,
        scratch_shapes=[pltpu.VMEM((SUBLANES, HALO + tt, cw), F32)],
        args=(proj, proj, proj, proj, w_dw, b_dw, ln_g, ln_b), rides=[ride])


def _pointwise_fwd(a, wpw, b_pw, proj, ycat, cw):
    t = a.shape[0]
    tm = _pick(t, 256, 16)
    ncol = proj.shape[1] // cw
    ycol = ycat.shape[1] // cw - 1

    def body(a_ref, w_ref, b_ref, g_ref, y_in, z_ref, y_ref):
        z = _dot(a_ref[...], w_ref[...], NN) + b_ref[...]
        z_ref[...] = z
        y_ref[...] = (z * _silu(g_ref[...])).astype(BF16)

    return pl.pallas_call(
        body, name="pointwise_fwd", grid=(t // tm,),
        out_shape=[jax.ShapeDtypeStruct((t, cw), F32), jax.ShapeDtypeStruct(ycat.shape, ycat.dtype)],
        in_specs=[pl.BlockSpec((tm, cw), lambda m: (m, 0)),
                  pl.BlockSpec((cw, cw), lambda m: (0, 0)),
                  pl.BlockSpec((1, cw), lambda m: (0, 0)),
                  pl.BlockSpec((tm, cw), lambda m: (m, ncol - 1)),
                  pl.BlockSpec(memory_space=pl.ANY)],
        out_specs=[pl.BlockSpec((tm, cw), lambda m: (m, 0)), pl.BlockSpec((tm, cw), lambda m: (m, ycol))],
        input_output_aliases={4: 1},
        compiler_params=_params(("arbitrary",)),
    )(a, wpw, b_pw, proj, ycat)


def _out_matmul(ycat, wout, x, target, mod):
    t, d = x.shape
    kdim = wout.shape[0]
    tm, tn = _pick(t, 512, 16), _pick(d, 1024)
    inv_d = 1.0 / d

    def body(a_ref, w_ref, x_ref, tg_ref, mod_ref, dout_ref, dy_ref, sums_ref):
        @pl.when(pl.program_id(1) == 0)
        def _():
            sums_ref[...] = jnp.zeros_like(sums_ref)

        y = _dot(a_ref[...], w_ref[...], NN)
        gate = mod_ref[2:3, :]
        err = (x_ref[...] + gate * y) - tg_ref[...]
        dout = err * inv_d
        dout_ref[...] = dout
        dy_ref[...] = (dout * gate).astype(BF16)
        sums_ref[0:1, :] += jnp.sum(dout * y, axis=0, keepdims=True)
        sums_ref[1:2, :] += jnp.sum(err * err, axis=0, keepdims=True)

    mn = lambda n, m: (m, n)
    return pl.pallas_call(
        body, name="out_matmul", grid=(d // tn, t // tm),
        out_shape=[jax.ShapeDtypeStruct((t, d), F32), jax.ShapeDtypeStruct((t, d), BF16),
                   jax.ShapeDtypeStruct((8, d), F32)],
        in_specs=[pl.BlockSpec((tm, kdim), lambda n, m: (m, 0)),
                  pl.BlockSpec((kdim, tn), lambda n, m: (0, n)),
                  pl.BlockSpec((tm, tn), mn), pl.BlockSpec((tm, tn), mn),
                  pl.BlockSpec((3, tn), lambda n, m: (0, n))],
        out_specs=[pl.BlockSpec((tm, tn), mn), pl.BlockSpec((tm, tn), mn),
                   pl.BlockSpec((8, tn), lambda n, m: (0, n))],
        compiler_params=_params(("arbitrary", "arbitrary")),
    )(ycat, wout, x, target, mod)


def _mm(a, b, form, name, tm, tn, out_dtype, *, slabs=False, n_outer=False, ksplit=1, every_other=None, rides=()):
    if form == TN:
        kdim, m_dim = a.shape
    else:
        m_dim, kdim = a.shape
    n_dim = (b.shape[0] if form == NT else b.shape[1]) // (1 if every_other is None else 2)
    tk = kdim // ksplit
    gm, gn = m_dim // tm, n_dim // tn
    mn = (lambda g: (g[1], g[0])) if n_outer else (lambda g: (g[0], g[1]))
    b_col = (lambda g: mn(g)[1]) if every_other is None else (lambda g: 2 * mn(g)[1] + g[3][0])
    a_map = (lambda *g: (g[2], mn(g)[0])) if form == TN else (lambda *g: (mn(g)[0], g[2]))
    b_map = (lambda *g: (b_col(g), g[2])) if form == NT else (lambda *g: (g[2], b_col(g)))
    a_blk = (tk, tm) if form == TN else (tm, tk)
    b_blk = (tn, tk) if form == NT else (tk, tn)
    if slabs:
        out_shape = jax.ShapeDtypeStruct((gn, m_dim, tn), out_dtype)
        out_spec = pl.BlockSpec((None, tm, tn), lambda *g: (mn(g)[1], mn(g)[0], 0))
    else:
        out_shape = jax.ShapeDtypeStruct((m_dim, n_dim), out_dtype)
        out_spec = pl.BlockSpec((tm, tn), lambda *g: mn(g))

    def body(*refs):
        a_ref, b_ref, o_ref, *acc = refs if every_other is None else refs[1:]
        part = _dot(a_ref[...], b_ref[...], form)
        if ksplit == 1:
            o_ref[...] = part.astype(out_dtype)
            return
        k = pl.program_id(2)

        @pl.when(k == 0)
        def _():
            acc[0][...] = part

        @pl.when((k > 0) & (k < ksplit - 1))
        def _():
            acc[0][...] += part

        @pl.when(k == ksplit - 1)
        def _():
            o_ref[...] = (acc[0][...] + part).astype(out_dtype)

    return _call(
        body, name=name, grid=((gn, gm) if n_outer else (gm, gn)) + (ksplit,),
        out_shape=[out_shape], in_specs=[pl.BlockSpec(a_blk, a_map), pl.BlockSpec(b_blk, b_map)],
        out_specs=[out_spec], scratch_shapes=[pltpu.VMEM((tm, tn), F32)] if ksplit > 1 else [],
        args=(a, b), rides=rides, prefetch=every_other)


def _dh_matmul(dproj, w, tm, tn, ksplit, rides):
    t, kdim = dproj.shape
    d = w.shape[0]
    tk = kdim // ksplit

    def body(a_ref, b_ref, o_ref):
        k, n = pl.program_id(1), pl.program_id(2)
        cols = pl.ds(pl.multiple_of(n * tn, tn), tn)
        part = _dot(a_ref[...], b_ref[...], NT)

        @pl.when(k == 0)
        def _():
            o_ref[:, cols] = part

        @pl.when(k > 0)
        def _():
            o_ref[:, cols] += part

    return _call(
        body, name="dh_matmul", grid=(t // tm, ksplit, d // tn),
        out_shape=[jax.ShapeDtypeStruct((t, d), F32)],
        in_specs=[pl.BlockSpec((tm, tk), lambda m, k, n: (m, k)), pl.BlockSpec((tn, tk), lambda m, k, n: (n, k))],
        out_specs=[pl.BlockSpec((tm, d), lambda m, k, n: (m, 0))],
        args=(dproj, w), rides=rides)


def _attention_bwd(proj, o, tot, firsts, dycat, qg, kg, nh):
    t, in_cols = proj.shape
    tq = _pick(t, 256, 16)
    nq = t // tq
    group = 4 if nq % 4 == 0 else 2 if nq % 2 == 0 else 1
    scale = HEAD_DIM ** -0.5

    def body(q_ref, k_ref, v_ref, g_ref, o_ref, tot_ref, first_ref, dy_ref, qg_ref, kg_ref, dproj_ref, gains_ref,
             qn, kn, vb, dob, dk_acc, dv_acc, dq_acc, outs, sems):
        h = pl.program_id(0)

        def norm(src, gain, dst):
            v = src[...]
            r = lax.rsqrt(jnp.mean(v * v, axis=-1, keepdims=True) + EPS)
            dst[...] = ((v * r) * gain[...]).astype(BF16)

        norm(q_ref, qg_ref, qn)
        norm(k_ref, kg_ref, kn)
        vb[...] = v_ref[...].astype(BF16)
        gs = g_ref[...]
        dyv = dy_ref[...]
        dob[...] = (dyv * _silu(gs)).astype(BF16)
        outs[3] = (dyv * o_ref[...] * _dsilu(gs)).astype(BF16)
        dk_acc[...] = jnp.zeros_like(dk_acc)
        dv_acc[...] = jnp.zeros_like(dv_acc)

        def before_matrix(n):
            r = lax.broadcasted_iota(jnp.int32, (n, n), 0)
            c = lax.broadcasted_iota(jnp.int32, (n, n), 1)
            return (c <= r).astype(BF16)

        incl = {n: before_matrix(n) for n in (tq, 2 * tq)}
        lane = lax.broadcasted_iota(jnp.int32, (1, LANES), 1)
        key_minus_query = (lax.broadcasted_iota(jnp.int32, (2 * tq, tq), 0)
                           - lax.broadcasted_iota(jnp.int32, (2 * tq, tq), 1))

        def block(start, width, qi, doi, tot_row, p_left, g_left, dq, q_start=None):
            ks = pl.ds(pl.multiple_of(start, tq), width)
            kj = kn[ks, :]
            z = _dot(kj, qi, NT) * scale
            sp = _softplus(z)
            ls = -sp
            if q_start is not None:
                causal = key_minus_query < q_start - start
                ls = jnp.where(causal, ls, 0.0)
            hi = ls.astype(BF16)
            lo = (ls - hi.astype(F32)).astype(BF16)
            p_inc = _dot(incl[width], hi, NN) + _dot(incl[width], lo, NN) + p_left
            beta = jnp.exp(z - sp)
            w = beta * jnp.exp(tot_row - p_inc)
            if q_start is not None:
                w = jnp.where(causal, w, 0.0)
            dw = _dot(vb[ks, :], doi, NT)
            g = w * dw
            g_upto = _dot(incl[width], g.astype(BF16), NN) + g_left
            dz = g - beta * g_upto
            if q_start is not None:
                dz = jnp.where(causal, dz, 0.0)
            dzb = dz.astype(BF16)
            dv_acc[ks, :] += _dot(w.astype(BF16), doi, NN)
            dk_acc[ks, :] += _dot(dzb, qi, NN)
            dq = dq + _dot(dzb, kj, TN)
            p_left = p_left + jnp.sum(ls, axis=0, keepdims=True)
            g_left = g_left + jnp.sum(g, axis=0, keepdims=True)
            return p_left, g_left, dq

        def operands(i):
            qs = pl.ds(pl.multiple_of(i * tq, tq), tq)
            return qn[qs, :], dob[qs, :], jnp.transpose(tot_ref[qs, :])[0:1, :]

        def singles(i):
            qi, doi, tot_row = operands(i)
            zero_row = jnp.zeros((1, tq), F32)

            def k_step(j, carry):
                return block(j * tq, tq, qi, doi, tot_row, carry[0], carry[1], carry[2])

            left = jnp.maximum(i - 1, 0)
            first = jnp.sum(jnp.where(lane == i, first_ref[0:1, :], 0.0)).astype(jnp.int32)
            first = jnp.clip(first, 0, left)
            return lax.fori_loop(first, left, k_step, (zero_row, zero_row, jnp.zeros((tq, HEAD_DIM), F32)))

        def wide_step(i, carry):
            qi, doi, tot_row = operands(i)
            left = jnp.maximum(i - 1, 0)
            _, _, dq = block(left * tq, 2 * tq, qi, doi, tot_row, carry[0], carry[1], carry[2], i * tq)
            dq_acc[pl.ds(pl.multiple_of(i * tq, tq), tq), :] = dq * scale

        def q_group(p, _):
            blocks = [p + b * (nq // group) for b in range(group)]
            carries = [singles(i) for i in blocks]
            for i, carry in zip(blocks, carries):
                wide_step(i, carry)
            return 0

        lax.fori_loop(0, nq // group, q_group, 0)

        def norm_bwd(src, gain, dn, slot, gain_row):
            v = src[...]
            r = lax.rsqrt(jnp.mean(v * v, axis=-1, keepdims=True) + EPS)
            vhat = v * r
            gains_ref[gain_row:gain_row + 1, :] = jnp.sum(dn * vhat, axis=0, keepdims=True)
            dhat = dn * gain[...]
            outs[slot] = (r * (dhat - vhat * jnp.mean(dhat * vhat, axis=-1, keepdims=True))).astype(BF16)

        gains_ref[...] = jnp.zeros_like(gains_ref)
        norm_bwd(q_ref, qg_ref, dq_acc[...], 0, 0)
        norm_bwd(k_ref, kg_ref, dk_acc[...] * scale, 1, 1)
        outs[2] = dv_acc[...].astype(BF16)
        copies = [pltpu.make_async_copy(
            outs.at[s], dproj_ref.at[:, pl.ds(pl.multiple_of((s * nh + h) * HEAD_DIM, HEAD_DIM), HEAD_DIM)], sems.at[s])
            for s in range(4)]
        for cp in copies:
            cp.start()
        for cp in copies:
            cp.wait()

    col_block = lambda off: pl.BlockSpec((t, HEAD_DIM), lambda h: (0, off + h))
    vec = pl.BlockSpec((1, HEAD_DIM), lambda h: (0, 0))
    head_scr = lambda dt: pltpu.VMEM((t, HEAD_DIM), dt)
    return pl.pallas_call(
        body, name="attention_bwd", grid=(nh,),
        out_shape=[jax.ShapeDtypeStruct((t, in_cols), BF16), jax.ShapeDtypeStruct((nh, 8, HEAD_DIM), F32)],
        in_specs=[col_block(0), col_block(nh), col_block(2 * nh), col_block(3 * nh),
                  col_block(0), col_block(0), pl.BlockSpec((None, 8, LANES), lambda h: (h, 0, 0)), col_block(0), vec, vec],
        out_specs=[pl.BlockSpec(memory_space=pl.ANY), pl.BlockSpec((None, 8, HEAD_DIM), lambda h: (h, 0, 0))],
        scratch_shapes=[head_scr(BF16), head_scr(BF16), head_scr(BF16), head_scr(BF16),
                        head_scr(F32), head_scr(F32), head_scr(F32),
                        pltpu.VMEM((4, t, HEAD_DIM), BF16), pltpu.SemaphoreType.DMA((4,))],
        compiler_params=_params(("arbitrary",)),
    )(proj, proj, proj, proj, o, tot, firsts, dycat, qg, kg)


def _pointwise_bwd(dycat, z, proj, dproj, wpw, hc, ln_g, ln_b, cw):
    t = z.shape[0]
    tt = _pick(t, 256, 16)
    ncol = proj.shape[1] // cw
    ycol = dycat.shape[1] // cw - 1

    def body(dy_ref, z_ref, g_ref, w_ref, hc_ref, lg_ref, lb_ref, dp_in, dz_ref, dp_ref, dh_ref, sums_ref):
        i = pl.program_id(0)

        @pl.when(i == 0)
        def _():
            sums_ref[...] = jnp.zeros_like(sums_ref)

        g = g_ref[...]
        dy = dy_ref[...]
        dz = dy * _silu(g)
        dzb = dz.astype(BF16)
        dz_ref[...] = dzb
        dp_ref[...] = (dy * z_ref[...] * _dsilu(g)).astype(BF16)
        da = _dot(dzb, w_ref[...], NT)
        hcv = hc_ref[...]
        mu = jnp.mean(hcv, axis=-1, keepdims=True)
        xc = hcv - mu
        r = lax.rsqrt(jnp.mean(xc * xc, axis=-1, keepdims=True) + EPS)
        xhat = xc * r
        ln = xhat * lg_ref[...] + lb_ref[...]
        dln = da * _dsilu(ln)
        dxhat = dln * lg_ref[...]
        dhc = r * (dxhat - jnp.mean(dxhat, axis=-1, keepdims=True)
                   - xhat * jnp.mean(dxhat * xhat, axis=-1, keepdims=True))
        dh_ref[...] = dhc
        sums_ref[0:1, :] += jnp.sum(dz, axis=0, keepdims=True)
        sums_ref[1:2, :] += jnp.sum(dln * xhat, axis=0, keepdims=True)
        sums_ref[2:3, :] += jnp.sum(dln, axis=0, keepdims=True)
        sums_ref[3:4, :] += jnp.sum(dhc, axis=0, keepdims=True)

    tile = lambda col: pl.BlockSpec((tt, cw), lambda i: (i, col))
    vec = pl.BlockSpec((1, cw), lambda i: (0, 0))
    return pl.pallas_call(
        body, name="pointwise_bwd", grid=(t // tt,),
        out_shape=[jax.ShapeDtypeStruct((t, cw), BF16), jax.ShapeDtypeStruct(dproj.shape, dproj.dtype),
                   jax.ShapeDtypeStruct((t, cw), F32), jax.ShapeDtypeStruct((8, cw), F32)],
        in_specs=[tile(ycol), tile(0), tile(ncol - 1), pl.BlockSpec((cw, cw), lambda i: (0, 0)), tile(0), vec, vec,
                  pl.BlockSpec(memory_space=pl.ANY)],
        out_specs=[tile(0), tile(ncol - 1), tile(0), pl.BlockSpec((8, cw), lambda i: (0, 0))],
        input_output_aliases={7: 1},
        compiler_params=_params(("arbitrary",)),
    )(dycat, z, proj, wpw, hc, ln_g, ln_b, dproj)


def _conv_bwd(dhc, proj, w_dw, dproj, kc, cw):
    t = proj.shape[0]
    tt = _pick(t, 128, HALO)
    per = tt // HALO
    nt = t // tt
    chunk = _pick(cw, 256)
    ncol = proj.shape[1] // cw
    wr = w_dw.shape[0]

    def body(d_ref, dn_ref, u_ref, g_ref, up_ref, gp_ref, w_ref, dp_in, dp_ref, dw_ref, dbuf, hbuf, dw_acc):
        i = pl.program_id(0)

        @pl.when(i == 0)
        def _():
            dw_acc[...] = jnp.zeros_like(dw_acc)

        dbuf[0, pl.ds(0, tt), :] = d_ref[...]
        dbuf[0, pl.ds(tt, HALO), :] = jnp.where(i < nt - 1, dn_ref[...], 0.0)
        hbuf[0, pl.ds(HALO, tt), :] = _glu_rows(u_ref, g_ref)
        hbuf[0, pl.ds(0, HALO), :] = jnp.where(i > 0, _glu_rows(up_ref, gp_ref), 0.0)
        _fill_shifts(dbuf, tt + HALO - SUBLANES)
        _fill_shifts(hbuf, tt + HALO - SUBLANES)
        for lo in range(0, cw, chunk):
            dhg = _conv_taps(dbuf, w_ref, kc, lo, tt, chunk, lambda j: (kc - 1) - j)
            u = u_ref[:, lo:lo + chunk]
            sg = _sigmoid(g_ref[:, lo:lo + chunk])
            dp_ref[:, lo:lo + chunk] = (dhg * sg).astype(BF16)
            dp_ref[:, cw + lo:cw + lo + chunk] = (dhg * u * sg * (1.0 - sg)).astype(BF16)
            dtile = d_ref[:, lo:lo + chunk]
            for j in range(kc):
                prod = dtile * _shifted_rows(hbuf, HALO - (kc - 1) + j, tt, lo, chunk)
                dw_acc[j, :, lo:lo + chunk] += jnp.sum(prod.reshape(tt // SUBLANES, SUBLANES, chunk), axis=0)

        @pl.when(i == nt - 1)
        def _():
            dw_ref[...] = jnp.sum(dw_acc[...], axis=1)

    tile = lambda g: pl.BlockSpec((tt, cw), lambda i: (i, g))
    prev = lambda g: pl.BlockSpec((HALO, cw), lambda i: (jnp.maximum(i * per - 1, 0), g))
    return pl.pallas_call(
        body, name="conv_bwd", grid=(nt,),
        out_shape=[jax.ShapeDtypeStruct(dproj.shape, dproj.dtype), jax.ShapeDtypeStruct((wr, cw), F32)],
        in_specs=[pl.BlockSpec((tt, cw), lambda i: (i, 0)),
                  pl.BlockSpec((HALO, cw), lambda i: (jnp.minimum((i + 1) * per, nt * per - 1), 0)),
                  tile(ncol - 3), tile(ncol - 2), prev(ncol - 3), prev(ncol - 2),
                  pl.BlockSpec((wr, cw), lambda i: (0, 0)), pl.BlockSpec(memory_space=pl.ANY)],
        out_specs=[pl.BlockSpec((tt, 2 * cw), lambda i: (i, (ncol - 3) // 2)),
                   pl.BlockSpec((wr, cw), lambda i: (0, 0))],
        scratch_shapes=[pltpu.VMEM((SUBLANES, tt + HALO, cw), F32), pltpu.VMEM((SUBLANES, HALO + tt, cw), F32),
                        pltpu.VMEM((wr, SUBLANES, cw), F32)],
        input_output_aliases={7: 0},
        compiler_params=_params(("arbitrary",)),
    )(dhc, dhc, proj, proj, proj, proj, w_dw, dproj)


def _input_grad(dh, x, dout, norm_g, mod, rides=()):
    t, d = x.shape
    tt = _pick(t, 128, 16)

    def body(dh_ref, x_ref, do_ref, g_ref, mod_ref, gx_ref, sums_ref):
        i = pl.program_id(0)

        @pl.when(i == 0)
        def _():
            sums_ref[...] = jnp.zeros_like(sums_ref)

        xv = x_ref[...]
        dhv = dh_ref[...]
        r = lax.rsqrt(jnp.mean(xv * xv, axis=-1, keepdims=True) + EPS)
        xn = xv * r
        g = g_ref[...]
        one_scale = 1.0 + mod_ref[1:2, :]
        dxn = dhv * g * one_scale
        gx_ref[...] = do_ref[...] + r * (dxn - xn * jnp.mean(dxn * xn, axis=-1, keepdims=True))
        sums_ref[0:1, :] += jnp.sum(dhv, axis=0, keepdims=True)
        sums_ref[1:2, :] += jnp.sum(dhv * (xn * g), axis=0, keepdims=True)
        sums_ref[2:3, :] += jnp.sum(dhv * one_scale * xn, axis=0, keepdims=True)

    tile = pl.BlockSpec((tt, d), lambda i: (i, 0))
    return _call(
        body, name="input_grad", grid=(t // tt,),
        out_shape=[jax.ShapeDtypeStruct((t, d), F32), jax.ShapeDtypeStruct((8, d), F32)],
        in_specs=[tile, tile, tile, pl.BlockSpec((1, d), lambda i: (0, 0)), pl.BlockSpec((3, d), lambda i: (0, 0))],
        out_specs=[tile, pl.BlockSpec((8, d), lambda i: (0, 0))],
        args=(dh, x, dout, norm_g, mod), rides=rides)


def _sum_adam(parts, w, m, v, name):
    r, c = w.shape
    n_parts = parts.shape[0]
    tr = _pick(r, 128, 16) if r % 16 == 0 else r
    tc = _pick(c, 2048)

    def body(p_ref, w_ref, m_ref, v_ref, g_ref, d_ref, nm_ref, nv_ref):
        g = p_ref[0].astype(F32)
        for i in range(1, n_parts):
            g = g + p_ref[i].astype(F32)
        d, nm, nv = _adam(w_ref[...], g, m_ref[...], v_ref[...])
        g_ref[...] = g
        d_ref[...] = d
        nm_ref[...] = nm
        nv_ref[...] = nv

    tile = pl.BlockSpec((tr, tc), lambda i, j: (i, j))
    out = jax.ShapeDtypeStruct((r, c), F32)
    return pl.pallas_call(
        body, name=name, grid=(r // tr, c // tc),
        out_shape=[out] * 4,
        in_specs=[pl.BlockSpec((n_parts, tr, tc), lambda i, j: (0, i, j)), tile, tile, tile],
        out_specs=[tile] * 4,
        compiler_params=_params(("arbitrary", "arbitrary")),
    )(parts, w, m, v)


def _small_adam(parts, piece_rows, states):
    n_leaf = len(states)
    offsets = [sum(piece_rows[:i]) for i in range(len(piece_rows))]

    def total(p_ref, off, r):
        g = p_ref[0, off:off + r, :]
        for k in range(1, NDEV):
            g = g + p_ref[k, off:off + r, :]
        return g

    def body(p_ref, *refs):
        ins, outs = refs[:3 * n_leaf], refs[3 * n_leaf:]
        for i in range(n_leaf):
            w_ref, m_ref, v_ref = ins[3 * i:3 * i + 3]
            g = total(p_ref, offsets[i], w_ref.shape[0])
            d, nm, nv = _adam(w_ref[...], g, m_ref[...], v_ref[...])
            for o_ref, val in zip(outs[4 * i:4 * i + 4], (g, d, nm, nv)):
                o_ref[...] = val
        outs[4 * n_leaf][...] = total(p_ref, offsets[n_leaf], piece_rows[n_leaf])

    vmem = pl.BlockSpec(memory_space=pltpu.VMEM)
    flat = [a for leaf in states for a in leaf]
    out_shape = [jax.ShapeDtypeStruct(leaf[0].shape, F32) for leaf in states for _ in range(4)]
    out_shape.append(jax.ShapeDtypeStruct((piece_rows[n_leaf], LANES), F32))
    res = pl.pallas_call(
        body, name="small_adam", out_shape=out_shape,
        in_specs=[vmem] * (1 + len(flat)), out_specs=[vmem] * len(out_shape),
        compiler_params=pltpu.CompilerParams(vmem_limit_bytes=VMEM_LIMIT),
    )(parts, *flat)
    return [res[4 * i:4 * i + 4] for i in range(n_leaf)], res[4 * n_leaf]


def _ada_grad_adam(s_t, dm, w, m, v):
    d, n = w.shape
    tr = _pick(d, 256, 16)

    def body(s_ref, dm_ref, w_ref, m_ref, v_ref, g_ref, d_ref, nm_ref, nv_ref):
        g = lax.dot_general(s_ref[...], dm_ref[...], (NN, ((), ())), preferred_element_type=F32,
                            precision=lax.Precision.HIGHEST)
        dl, nm, nv = _adam(w_ref[...], g, m_ref[...], v_ref[...])
        g_ref[...] = g
        d_ref[...] = dl
        nm_ref[...] = nm
        nv_ref[...] = nv

    tile = pl.BlockSpec((tr, n), lambda i: (i, 0))
    out = jax.ShapeDtypeStruct((d, n), F32)
    return pl.pallas_call(
        body, name="ada_grad_adam", grid=(d // tr,),
        out_shape=[out] * 4,
        in_specs=[pl.BlockSpec((tr, NDEV), lambda i: (i, 0)), pl.BlockSpec((NDEV, n), lambda i: (0, 0)),
                  tile, tile, tile],
        out_specs=[tile] * 4,
        compiler_params=_params(("arbitrary",)),
    )(s_t, dm, w, m, v)


def _silu_t(c_all):
    n, d = c_all.shape

    def body(c_ref, o_ref):
        o_ref[...] = jnp.transpose(_silu(c_ref[...]))

    return pl.pallas_call(
        body, name="silu_t", out_shape=jax.ShapeDtypeStruct((d, n), F32),
        in_specs=[pl.BlockSpec(memory_space=pltpu.VMEM)], out_specs=pl.BlockSpec(memory_space=pltpu.VMEM),
        compiler_params=pltpu.CompilerParams(vmem_limit_bytes=VMEM_LIMIT),
    )(c_all)


def _rows128(v):
    return v.reshape(-1, LANES)


def _pad_rows(a, rows):
    return jnp.pad(a, ((0, rows - a.shape[0]), (0, 0)))


def kernel(x, c, norm_g, w_ada, b_ada, w_in, q_norm_g, k_norm_g, w_dw, b_dw, ln_g, ln_b, w_pw, b_pw, w_out, loss_target, m_norm_g, m_w_ada, m_b_ada, m_w_in, m_q_norm_g, m_k_norm_g, m_w_dw, m_b_dw, m_ln_g, m_ln_b, m_w_pw, m_b_pw, m_w_out, v_norm_g, v_w_ada, v_b_ada, v_w_in, v_q_norm_g, v_k_norm_g, v_w_dw, v_b_dw, v_ln_g, v_ln_b, v_w_pw, v_b_pw, v_w_out):
    _, t, d = x.shape
    n_ada = w_ada.shape[2]
    ns = w_in.shape[2]
    kc, cwl = w_dw.shape[1], w_dw.shape[2]
    cw = cwl * NDEV
    sb = d - cw
    nh = sb // HEAD_DIM
    assert sb == cw and kc - 1 <= HALO and NDEV * ns == 4 * sb + 3 * cw
    my = 4 * lax.axis_index("x") + 2 * lax.axis_index("y") + lax.axis_index("c")

    x2, tg2 = x[0], loss_target[0]

    wdw_rows = -(-kc // 8) * 8
    wdw_pad = _pad_rows(w_dw[0], wdw_rows)
    pay1 = jnp.concatenate([_rows128(c[0]), _rows128(wdw_pad.reshape(-1))], axis=0)
    (g1,) = _all_gather([pay1], "gather_cond", pltpu.VMEM)
    c_rows = d // LANES
    c_all = g1[:, :c_rows].reshape(NDEV, d)
    wdw_all = g1[:, c_rows:].reshape(NDEV, wdw_rows, cwl).transpose(1, 0, 2).reshape(wdw_rows, cw)

    b_ada_loc = lax.dynamic_slice(b_ada, (0, my * n_ada), (1, n_ada))
    mod_cols = _ada_matmul(c_all, w_ada[0], b_ada_loc)
    (g2,) = _all_gather([mod_cols], "gather_mod", pltpu.VMEM)
    mod_mine = lax.dynamic_index_in_dim(g2, my, axis=1, keepdims=False)
    mod = mod_mine.reshape(3, d)

    core = lax.axis_index("c").astype(jnp.int32).reshape(1)
    h = _modulated_norm(x2, norm_g, mod)
    wfull_in, proj = _gather_proj(h, w_in[0].astype(BF16))
    (o, tot, ycat, firsts), (partly,) = _attention_fwd(
        proj, q_norm_g, k_norm_g, nh, d, _gather_ride([w_out[0].astype(BF16), w_pw[0].astype(BF16)]))
    (a, hc), ((wg_out, wg_pw),) = _conv_fwd(proj, wdw_all, b_dw, ln_g, ln_b, kc, cw, _gather_finish_ride(partly))
    wfull_out, wfull_pw = wg_out.reshape(d, d), wg_pw.reshape(cw, cw)
    z, ycat = _pointwise_fwd(a, wfull_pw, b_pw, proj, ycat, cw)
    dout, dy, out_sums = _out_matmul(ycat, wfull_out, x2, tg2, mod)

    tile = _pick(t, 512, 16)
    (dycat,), _ = _mm(dy, wfull_out, NT, "dycat_matmul", _pick(t, 1024, 16), _pick(d, 512), F32)
    (p_wout,), _ = _mm(ycat, dy, TN, "w_out_grad", _pick(d, 512), _pick(d, 1024), BF16)
    p_wout = p_wout.reshape(NDEV, d // NDEV, d)
    dproj, gains = _attention_bwd(proj, o, tot, firsts, dycat, q_norm_g, k_norm_g, nh)
    dz, dproj, dhc, pw_sums = _pointwise_bwd(dycat, z, proj, dproj, wfull_pw, hc, ln_g, ln_b, cw)
    (p_wpw,), _ = _mm(a, dz, TN, "w_pw_grad", _pick(cw, 512), _pick(cw, 1024), BF16)
    p_wpw = p_wpw.reshape(NDEV, cw // NDEV, cw)
    dproj, dwdw = _conv_bwd(dhc, proj, wdw_all, dproj, kc, cw)
    p_wdw = dwdw.reshape(wdw_rows, NDEV, cwl).transpose(1, 0, 2).astype(BF16)

    def pair_sums(mine, theirs, name):
        return [_pair_sum(m, s, core, f"{name}_pair_sum_{i}") for i, (m, s) in enumerate(zip(mine, theirs))]

    lesser = [p.reshape(4, 2, *p.shape[1:]) for p in (p_wout, p_wpw, p_wdw)]
    (p_theirs,), (small_theirs,) = _mm(h, dproj, TN, "w_in_grad_sibling", _pick(d, 512), ns, BF16, slabs=True,
                                       n_outer=True, every_other=1 - core, rides=[_sibling_ride(lesser)])
    q_small = pair_sums(lesser, small_theirs, "small_grads")
    (p_mine,), (win_theirs, (r_wout, r_wpw, r_wdw)) = _mm(
        h, dproj, TN, "w_in_grad_own", _pick(d, 512), ns, BF16, slabs=True, n_outer=True, every_other=core,
        rides=[_sibling_ride([p_theirs]), _chip_exchange_ride(q_small)])
    q_win = pair_sums([p_mine[:, None]], win_theirs, "w_in_grad")
    head = d - d // 8
    (dh,), ((r_head,),) = _dh_matmul(dproj, wfull_in, tile, _pick(d, 512), 2,
                                     [_chip_exchange_ride(q_win, rows=(0, head))])
    (grad_x, in_sums), ((r_win,),) = _input_grad(
        dh, x2, dout, norm_g, mod, rides=[_chip_exchange_ride(q_win, rows=(head, d - head), into=[r_head])])

    dmod = jnp.concatenate([in_sums[0], in_sums[1], out_sums[0]])
    loss_part = 0.5 / d * jnp.sum(out_sums[1].reshape(-1, LANES), axis=0)
    small = [in_sums[2], dmod, jnp.sum(gains[:, 0], axis=0), jnp.sum(gains[:, 1], axis=0),
             pw_sums[3], pw_sums[1], pw_sums[2], pw_sums[0], loss_part]
    pieces = [_rows128(s) for s in small]
    pieces = [_pad_rows(p, -(-p.shape[0] // 8) * 8) for p in pieces]
    (g3,) = _all_gather([jnp.concatenate(pieces, axis=0)], "gather_small", pltpu.VMEM)
    states = [[_rows128(s.reshape(-1)) for s in leaf] for leaf in (
        (norm_g, m_norm_g, v_norm_g), (b_ada, m_b_ada, v_b_ada), (q_norm_g, m_q_norm_g, v_q_norm_g),
        (k_norm_g, m_k_norm_g, v_k_norm_g), (b_dw, m_b_dw, v_b_dw), (ln_g, m_ln_g, v_ln_g),
        (ln_b, m_ln_b, v_ln_b), (b_pw, m_b_pw, v_b_pw))]
    small_out, loss_rows = _small_adam(g3, [p.shape[0] for p in pieces], states)
    g_small, d_small, m_small, v_small = [[leaf[k].reshape(1, -1) for leaf in small_out] for k in range(4)]
    loss = jnp.sum(loss_rows[0])

    off = pieces[0].shape[0]
    dmod_all = g3[:, off:off + pieces[1].shape[0]].reshape(NDEV, 3 * d)
    dmod_loc = lax.dynamic_slice(dmod_all, (0, my * n_ada), (NDEV, n_ada))
    ada = _ada_grad_adam(_silu_t(c_all), dmod_loc, w_ada[0], m_w_ada[0], v_w_ada[0])
    win = _sum_adam(r_win, w_in[0], m_w_in[0], v_w_in[0], "w_in_adam")
    wout = _sum_adam(r_wout, w_out[0], m_w_out[0], v_w_out[0], "w_out_adam")
    wpw = _sum_adam(r_wpw, w_pw[0], m_w_pw[0], v_w_pw[0], "w_pw_adam")
    wdw_state = [_pad_rows(s[0], wdw_rows) for s in (w_dw, m_w_dw, v_w_dw)]
    wdw = [r[:kc] for r in _sum_adam(r_wdw, *wdw_state, "w_dw_adam")]

    def group(k, small_list):
        s = small_list
        return [s[0], ada[k][None], s[1], win[k][None], s[2], s[3], wdw[k][None], s[4], s[5], s[6],
                wpw[k][None], s[7], wout[k][None]]

    return (loss, grad_x[None], *group(0, g_small), *group(1, d_small), *group(2, m_small), *group(3, v_small))
```
